```python
import jax, jax.numpy as jnp
from jax import lax
import numpy as np

D_MODEL = 1024
BATCH = 8
SEQ = 8192
DEPTH = 2

GRID_W = 64
CTX_LEN = 256
N_HEADS = 16
N_KV_HEADS = 4
HEAD_DIM = 64
Q_W = N_HEADS * HEAD_DIM
KV_W = N_KV_HEADS * HEAD_DIM
WINDOW = 128
BLK = 128
ROPE_THETA = 10000.0
CHUNK = 128
A_W = D_MODEL
A_GROUPS = 8
A_GW = A_W // A_GROUPS
B_W = D_MODEL
CONV_W = 3
N_BRANCH = 3
BRANCH_W = D_MODEL
D_FF = 2816
EPS = 1e-6
NEG = -1e30
OFF_Q = 0
OFF_K = OFF_Q + Q_W
OFF_V = OFF_K + KV_W
OFF_A = OFF_V + KV_W
OFF_B = OFF_A + 2 * A_W
OFF_G = OFF_B + 3 * B_W
IN_W = OFF_G + N_BRANCH * D_MODEL

kernel_name = "hybrid_gated_parallel_dit_block"


def rmsnorm(x, g):
    xf = x.astype(jnp.float32)
    y = xf * lax.rsqrt(jnp.mean(xf * xf, axis=-1, keepdims=True) + EPS)
    return (y * g.astype(jnp.float32)).astype(x.dtype)


def modulate(h, shift, scale):
    return h * (1.0 + scale) + shift


def dwconv3(x, w):
    ch = x.shape[-1]
    return lax.conv_general_dilated(
        x, w[:, None, :].astype(x.dtype), window_strides=(1,), padding=[(CONV_W // 2, CONV_W // 2)],
        dimension_numbers=("NWC", "WIO", "NWC"), feature_group_count=ch)


def axial_rope_tables(n):
    rows = n // GRID_W
    row = jnp.broadcast_to(jnp.arange(rows, dtype=jnp.float32)[:, None], (rows, GRID_W)).reshape(n)
    col = jnp.broadcast_to(jnp.arange(GRID_W, dtype=jnp.float32)[None, :], (rows, GRID_W)).reshape(n)
    half = HEAD_DIM // 2
    inv = ROPE_THETA ** (-jnp.arange(0, half, 2, dtype=jnp.float32) / half)
    ang = jnp.concatenate([row[:, None] * inv, col[:, None] * inv], axis=-1)
    return jnp.cos(ang), jnp.sin(ang)


def apply_rope(t, cos, sin):
    half = HEAD_DIM // 2
    tf = t.astype(jnp.float32)
    t1, t2 = tf[..., :half], tf[..., half:]
    cs, sn = cos[None, :, None, :], sin[None, :, None, :]
    return jnp.concatenate([t1 * cs - t2 * sn, t1 * sn + t2 * cs], axis=-1).astype(t.dtype)


def latent_attention(q, k, v, kc, vc, sink):
    b, n = q.shape[:2]
    nb = n // BLK
    grp = N_HEADS // N_KV_HEADS
    m = kc.shape[1]
    scale = HEAD_DIM ** -0.5
    qb = q.reshape(b, nb, BLK, N_KV_HEADS, grp, HEAD_DIM)

    def neighbours(t):
        tb = t.reshape(b, nb, BLK, N_KV_HEADS, HEAD_DIM)
        tp = jnp.pad(tb, ((0, 0), (1, 1), (0, 0), (0, 0), (0, 0)))
        return jnp.concatenate([tp[:, :-2], tp[:, 1:-1], tp[:, 2:]], axis=2)

    kn, vn = neighbours(k), neighbours(v)
    rel = (jnp.arange(3 * BLK)[None, :] - BLK) - jnp.arange(BLK)[:, None]
    band = jnp.abs(rel) <= WINDOW
    src_blk = jnp.arange(nb)[:, None] + jnp.arange(3 * BLK)[None, :] // BLK - 1
    in_range = (src_blk >= 0) & (src_blk < nb)
    sink_l = sink.astype(jnp.float32).reshape(N_KV_HEADS, grp)[None, :, :, None, None]

    def block(args):
        qi, ki, vi, ok = args
        s_loc = jnp.einsum("bqkgd,bjkd->bkgqj", qi, ki).astype(jnp.float32) * scale
        s_loc = jnp.where(band & ok[None, :], s_loc, NEG)
        s_ctx = jnp.einsum("bqkgd,bckd->bkgqc", qi, kc).astype(jnp.float32) * scale
        s_snk = jnp.broadcast_to(sink_l, s_ctx.shape[:-1] + (1,))
        p = jax.nn.softmax(jnp.concatenate([s_loc, s_ctx, s_snk], axis=-1), axis=-1).astype(vi.dtype)
        return (jnp.einsum("bkgqj,bjkd->bqkgd", p[..., :3 * BLK], vi)
                + jnp.einsum("bkgqc,bckd->bqkgd", p[..., 3 * BLK:3 * BLK + m], vc))

    xs = (jnp.moveaxis(qb, 1, 0), jnp.moveaxis(kn, 1, 0), jnp.moveaxis(vn, 1, 0), in_range)
    out = lax.map(block, xs)
    return jnp.moveaxis(out, 0, 1).reshape(b, n, Q_W)


def context_attention(qc, kc, vc, sink):
    b, m = qc.shape[:2]
    grp = N_HEADS // N_KV_HEADS
    qg = qc.reshape(b, m, N_KV_HEADS, grp, HEAD_DIM)
    s = jnp.einsum("bqkgd,bckd->bkgqc", qg, kc).astype(jnp.float32) * (HEAD_DIM ** -0.5)
    s_snk = jnp.broadcast_to(sink.astype(jnp.float32).reshape(N_KV_HEADS, grp)[None, :, :, None, None],
                             s.shape[:-1] + (1,))
    p = jax.nn.softmax(jnp.concatenate([s, s_snk], axis=-1), axis=-1).astype(vc.dtype)
    o = jnp.einsum("bkgqc,bckd->bqkgd", p[..., :m], vc)
    return o.reshape(b, m, Q_W)


def chunk_spatial_gating(z_a, w_s, b_s, g_v):
    z = jax.nn.gelu(z_a)
    u, v = z[..., :A_W], z[..., A_W:]
    v = rmsnorm(v, g_v)
    b, n = v.shape[:2]
    vr = v.reshape(b, n // CHUNK, CHUNK, A_GROUPS, A_GW)
    mixed = jnp.einsum("gpq,bnqgc->bnpgc", w_s, vr) + b_s.T[None, None, :, :, None]
    return u * mixed.reshape(b, n, A_W)


def short_conv_mixer(z_b, w_sconv):
    bg, cg, hb = z_b[..., :B_W], z_b[..., B_W:2 * B_W], z_b[..., 2 * B_W:]
    return bg * dwconv3(cg * hb, w_sconv)


def merge_branches(z, y_attn, w_s, b_s, g_v, w_sconv, b_gate, w_branch, w_out):
    y_a = chunk_spatial_gating(z[..., OFF_A:OFF_B], w_s, b_s, g_v)
    y_b = short_conv_mixer(z[..., OFF_B:OFF_G], w_sconv)
    gates = jax.nn.sigmoid(z[..., OFF_G:] + b_gate)
    g_c, g_a, g_b = gates[..., :D_MODEL], gates[..., D_MODEL:2 * D_MODEL], gates[..., 2 * D_MODEL:]
    merged = (g_c * (y_attn @ w_branch[0]) + g_a * (y_a @ w_branch[1]) + g_b * (y_b @ w_branch[2]))
    return merged @ w_out


def conv_ffn(h, w_up, w_fconv, w_down):
    up = h @ w_up
    a, g = up[..., :D_FF], up[..., D_FF:]
    return (jax.nn.silu(dwconv3(a, w_fconv)) * g) @ w_down


def _fwd_setup_inputs(seed: int = 0) -> dict:
    key = jax.random.key(seed)
    ks = jax.random.split(key, 24)
    f32 = jnp.float32

    def nrm(k, shape, scale):
        return jax.random.normal(k, shape, f32) * scale

    return {
        "x": nrm(ks[0], (BATCH, SEQ, D_MODEL), 1.0),
        "c": nrm(ks[1], (BATCH, D_MODEL), 1.0),
        "ctx": nrm(ks[2], (BATCH, CTX_LEN, D_MODEL), 1.0),
        "c_ctx": nrm(ks[3], (D_MODEL,), 1.0),
        "w_mod": nrm(ks[4], (DEPTH, D_MODEL, 6 * D_MODEL), D_MODEL ** -0.5),
        "b_mod": nrm(ks[5], (DEPTH, 6 * D_MODEL), 0.02),
        "g_mix": 1.0 + nrm(ks[6], (DEPTH, D_MODEL), 0.02),
        "w_in": nrm(ks[7], (DEPTH, D_MODEL, IN_W), D_MODEL ** -0.5),
        "b_gate": nrm(ks[8], (DEPTH, N_BRANCH * D_MODEL), 0.02),
        "sink": nrm(ks[9], (DEPTH, N_HEADS), 0.5),
        "w_spatial": nrm(ks[10], (DEPTH, A_GROUPS, CHUNK, CHUNK), CHUNK ** -0.5),
        "b_spatial": 1.0 + nrm(ks[11], (DEPTH, A_GROUPS, CHUNK), 0.02),
        "g_v": 1.0 + nrm(ks[12], (DEPTH, A_W), 0.02),
        "w_sconv": nrm(ks[13], (DEPTH, CONV_W, B_W), CONV_W ** -0.5),
        "w_branch": nrm(ks[14], (DEPTH, N_BRANCH, BRANCH_W, D_MODEL), BRANCH_W ** -0.5),
        "w_out": nrm(ks[15], (DEPTH, D_MODEL, D_MODEL), D_MODEL ** -0.5),
        "g_ffn": 1.0 + nrm(ks[16], (DEPTH, D_MODEL), 0.02),
        "w_up": nrm(ks[17], (DEPTH, D_MODEL, 2 * D_FF), D_MODEL ** -0.5),
        "w_fconv": nrm(ks[18], (DEPTH, CONV_W, D_FF), CONV_W ** -0.5),
        "w_down": nrm(ks[19], (DEPTH, D_FF, D_MODEL), D_FF ** -0.5),
        "g_final": 1.0 + nrm(ks[20], (D_MODEL,), 0.02),
    }


def _fwd_reference(x, c, ctx, c_ctx, w_mod, b_mod, g_mix, w_in, b_gate, sink, w_spatial, b_spatial, g_v,
              w_sconv, w_branch, w_out, g_ffn, w_up, w_fconv, w_down, g_final):
    b, n, _ = x.shape
    m = ctx.shape[1]
    cos, sin = axial_rope_tables(n)
    xc = ctx
    for l in range(DEPTH):
        last = l == DEPTH - 1
        mod = jax.nn.silu(c) @ w_mod[l] + b_mod[l]
        mod_c = jax.nn.silu(c_ctx) @ w_mod[l] + b_mod[l]
        sh1, sc1, gt1, sh2, sc2, gt2 = [t[:, None, :] for t in jnp.split(mod, 6, axis=-1)]
        csh1, csc1, cgt1, csh2, csc2, cgt2 = jnp.split(mod_c, 6, axis=-1)

        h = modulate(rmsnorm(x, g_mix[l]), sh1, sc1)
        hc = modulate(rmsnorm(xc, g_mix[l]), csh1, csc1)
        z = h @ w_in[l]
        q = apply_rope(z[..., OFF_Q:OFF_K].reshape(b, n, N_HEADS, HEAD_DIM), cos, sin)
        k = apply_rope(z[..., OFF_K:OFF_V].reshape(b, n, N_KV_HEADS, HEAD_DIM), cos, sin)
        v = z[..., OFF_V:OFF_A].reshape(b, n, N_KV_HEADS, HEAD_DIM)
        if last:
            zc = hc @ w_in[l][:, OFF_K:OFF_A]
            kc = zc[..., :KV_W].reshape(b, m, N_KV_HEADS, HEAD_DIM)
            vc = zc[..., KV_W:].reshape(b, m, N_KV_HEADS, HEAD_DIM)
        else:
            zc = hc @ w_in[l]
            kc = zc[..., OFF_K:OFF_V].reshape(b, m, N_KV_HEADS, HEAD_DIM)
            vc = zc[..., OFF_V:OFF_A].reshape(b, m, N_KV_HEADS, HEAD_DIM)
        y_attn = latent_attention(q, k, v, kc, vc, sink[l])
        x = x + gt1 * merge_branches(z, y_attn, w_spatial[l], b_spatial[l], g_v[l], w_sconv[l],
                                     b_gate[l], w_branch[l], w_out[l])
        h2 = modulate(rmsnorm(x, g_ffn[l]), sh2, sc2)
        x = x + gt2 * conv_ffn(h2, w_up[l], w_fconv[l], w_down[l])

        if not last:
            qc = zc[..., OFF_Q:OFF_K].reshape(b, m, N_HEADS, HEAD_DIM)
            yc_attn = context_attention(qc, kc, vc, sink[l])
            xc = xc + cgt1 * merge_branches(zc, yc_attn, w_spatial[l], b_spatial[l], g_v[l], w_sconv[l],
                                            b_gate[l], w_branch[l], w_out[l])
            hc2 = modulate(rmsnorm(xc, g_ffn[l]), csh2, csc2)
            xc = xc + cgt2 * conv_ffn(hc2, w_up[l], w_fconv[l], w_down[l])
    return rmsnorm(x, g_final)


import jax as _jax
import jax.numpy as _jnp

TWIN_FORMAT = 'train_step'
FWD_PARAMS = ['x', 'c', 'ctx', 'c_ctx', 'w_mod', 'b_mod', 'g_mix', 'w_in', 'b_gate', 'sink', 'w_spatial', 'b_spatial', 'g_v', 'w_sconv', 'w_branch', 'w_out', 'g_ffn', 'w_up', 'w_fconv', 'w_down', 'g_final']
TWIN_WEIGHTS = ['c_ctx', 'w_mod', 'b_mod', 'g_mix', 'w_in', 'b_gate', 'sink', 'w_spatial', 'b_spatial', 'g_v', 'w_sconv', 'w_branch', 'w_out', 'g_ffn', 'w_up', 'w_fconv', 'w_down', 'g_final']
TWIN_DIFF_INPUT = 'x'
TWIN_INPUTS = ['x', 'c', 'ctx', 'c_ctx', 'w_mod', 'b_mod', 'g_mix', 'w_in', 'b_gate', 'sink', 'w_spatial', 'b_spatial', 'g_v', 'w_sconv', 'w_branch', 'w_out', 'g_ffn', 'w_up', 'w_fconv', 'w_down', 'g_final', 'loss_target', 'm_c_ctx', 'm_w_mod', 'm_b_mod', 'm_g_mix', 'm_w_in', 'm_b_gate', 'm_sink', 'm_w_spatial', 'm_b_spatial', 'm_g_v', 'm_w_sconv', 'm_w_branch', 'm_w_out', 'm_g_ffn', 'm_w_up', 'm_w_fconv', 'm_w_down', 'm_g_final', 'v_c_ctx', 'v_w_mod', 'v_b_mod', 'v_g_mix', 'v_w_in', 'v_b_gate', 'v_sink', 'v_w_spatial', 'v_b_spatial', 'v_g_v', 'v_w_sconv', 'v_w_branch', 'v_w_out', 'v_g_ffn', 'v_w_up', 'v_w_fconv', 'v_w_down', 'v_g_final']
TWIN_OUTPUTS = ['loss', 'grad_x', 'grad_c_ctx', 'grad_w_mod', 'grad_b_mod', 'grad_g_mix', 'grad_w_in', 'grad_b_gate', 'grad_sink', 'grad_w_spatial', 'grad_b_spatial', 'grad_g_v', 'grad_w_sconv', 'grad_w_branch', 'grad_w_out', 'grad_g_ffn', 'grad_w_up', 'grad_w_fconv', 'grad_w_down', 'grad_g_final', 'delta_c_ctx', 'delta_w_mod', 'delta_b_mod', 'delta_g_mix', 'delta_w_in', 'delta_b_gate', 'delta_sink', 'delta_w_spatial', 'delta_b_spatial', 'delta_g_v', 'delta_w_sconv', 'delta_w_branch', 'delta_w_out', 'delta_g_ffn', 'delta_w_up', 'delta_w_fconv', 'delta_w_down', 'delta_g_final', 'new_m_c_ctx', 'new_m_w_mod', 'new_m_b_mod', 'new_m_g_mix', 'new_m_w_in', 'new_m_b_gate', 'new_m_sink', 'new_m_w_spatial', 'new_m_b_spatial', 'new_m_g_v', 'new_m_w_sconv', 'new_m_w_branch', 'new_m_w_out', 'new_m_g_ffn', 'new_m_w_up', 'new_m_w_fconv', 'new_m_w_down', 'new_m_g_final', 'new_v_c_ctx', 'new_v_w_mod', 'new_v_b_mod', 'new_v_g_mix', 'new_v_w_in', 'new_v_b_gate', 'new_v_sink', 'new_v_w_spatial', 'new_v_b_spatial', 'new_v_g_v', 'new_v_w_sconv', 'new_v_w_branch', 'new_v_w_out', 'new_v_g_ffn', 'new_v_w_up', 'new_v_w_fconv', 'new_v_w_down', 'new_v_g_final']
TWIN_LEAF_KINDS = {'loss': 'loss', 'grad_x': 'grad_x', 'grad_c_ctx': 'grad_w', 'grad_w_mod': 'grad_w', 'grad_b_mod': 'grad_w', 'grad_g_mix': 'grad_w', 'grad_w_in': 'grad_w', 'grad_b_gate': 'grad_w', 'grad_sink': 'grad_w', 'grad_w_spatial': 'grad_w', 'grad_b_spatial': 'grad_w', 'grad_g_v': 'grad_w', 'grad_w_sconv': 'grad_w', 'grad_w_branch': 'grad_w', 'grad_w_out': 'grad_w', 'grad_g_ffn': 'grad_w', 'grad_w_up': 'grad_w', 'grad_w_fconv': 'grad_w', 'grad_w_down': 'grad_w', 'grad_g_final': 'grad_w', 'delta_c_ctx': 'delta_w', 'delta_w_mod': 'delta_w', 'delta_b_mod': 'delta_w', 'delta_g_mix': 'delta_w', 'delta_w_in': 'delta_w', 'delta_b_gate': 'delta_w', 'delta_sink': 'delta_w', 'delta_w_spatial': 'delta_w', 'delta_b_spatial': 'delta_w', 'delta_g_v': 'delta_w', 'delta_w_sconv': 'delta_w', 'delta_w_branch': 'delta_w', 'delta_w_out': 'delta_w', 'delta_g_ffn': 'delta_w', 'delta_w_up': 'delta_w', 'delta_w_fconv': 'delta_w', 'delta_w_down': 'delta_w', 'delta_g_final': 'delta_w', 'new_m_c_ctx': 'new_m', 'new_m_w_mod': 'new_m', 'new_m_b_mod': 'new_m', 'new_m_g_mix': 'new_m', 'new_m_w_in': 'new_m', 'new_m_b_gate': 'new_m', 'new_m_sink': 'new_m', 'new_m_w_spatial': 'new_m', 'new_m_b_spatial': 'new_m', 'new_m_g_v': 'new_m', 'new_m_w_sconv': 'new_m', 'new_m_w_branch': 'new_m', 'new_m_w_out': 'new_m', 'new_m_g_ffn': 'new_m', 'new_m_w_up': 'new_m', 'new_m_w_fconv': 'new_m', 'new_m_w_down': 'new_m', 'new_m_g_final': 'new_m', 'new_v_c_ctx': 'new_v', 'new_v_w_mod': 'new_v', 'new_v_b_mod': 'new_v', 'new_v_g_mix': 'new_v', 'new_v_w_in': 'new_v', 'new_v_b_gate': 'new_v', 'new_v_sink': 'new_v', 'new_v_w_spatial': 'new_v', 'new_v_b_spatial': 'new_v', 'new_v_g_v': 'new_v', 'new_v_w_sconv': 'new_v', 'new_v_w_branch': 'new_v', 'new_v_w_out': 'new_v', 'new_v_g_ffn': 'new_v', 'new_v_w_up': 'new_v', 'new_v_w_fconv': 'new_v', 'new_v_w_down': 'new_v', 'new_v_g_final': 'new_v'}


def _forward(args):
    return _fwd_reference(*[args[k] for k in FWD_PARAMS])


def _output_shape():
    def fwd():
        inp = _fwd_setup_inputs(0)
        return _fwd_reference(*[inp[k] for k in FWD_PARAMS])
    out = _jax.eval_shape(fwd)
    return out.shape, out.dtype

N_MICROBATCH = 1
ADAM_LR = 0.001
ADAM_B1 = 0.9
ADAM_B2 = 0.999
ADAM_EPS = 1e-08
ADAM_WD = 0.01
ADAM_STEP = 10
PER_EXAMPLE_BATCH_AXIS = {'x': 0, 'c': 0, 'ctx': 0, 'loss_target': 0}
SHARED_INPUTS = []
_WEIGHT_DTYPES = {'c_ctx': _jnp.float32, 'w_mod': _jnp.float32, 'b_mod': _jnp.float32, 'g_mix': _jnp.float32, 'w_in': _jnp.float32, 'b_gate': _jnp.float32, 'sink': _jnp.float32, 'w_spatial': _jnp.float32, 'b_spatial': _jnp.float32, 'g_v': _jnp.float32, 'w_sconv': _jnp.float32, 'w_branch': _jnp.float32, 'w_out': _jnp.float32, 'g_ffn': _jnp.float32, 'w_up': _jnp.float32, 'w_fconv': _jnp.float32, 'w_down': _jnp.float32, 'g_final': _jnp.float32}
MOMENT_SCALE = {'c_ctx': 4.218731e-02, 'w_mod': 1.344263e-01, 'b_mod': 2.387470e-01, 'g_mix': 2.937131e-01, 'w_in': 9.691065e-02, 'b_gate': 3.595141e-02, 'sink': 4.458123e-04, 'w_spatial': 5.923405e-02, 'b_spatial': 6.004313e-02, 'g_v': 5.968314e-02, 'w_sconv': 1.550400e-01, 'w_branch': 1.019829e-01, 'w_out': 1.767157e-01, 'g_ffn': 1.455584e-01, 'w_up': 6.910860e-02, 'w_fconv': 7.138975e-02, 'w_down': 1.127296e-01, 'g_final': 6.520804e+01}


def _to_microbatches(a, axis):
    t = _jnp.moveaxis(a, axis, 0)
    t = t.reshape((N_MICROBATCH, t.shape[0] // N_MICROBATCH) + t.shape[1:])
    return _jnp.moveaxis(t, 1, axis + 1)


def setup_inputs(seed: int = 0) -> dict:
    inp = _fwd_setup_inputs(seed)
    key = _jax.random.fold_in(_jax.random.key(seed), 7919)
    shape, _ = _output_shape()
    out = dict(inp)
    out["loss_target"] = _jax.random.normal(_jax.random.fold_in(key, 0), shape, _jnp.float32)
    for i, name in enumerate(TWIN_WEIGHTS):
        w = inp[name].astype(_jnp.float32)
        if MOMENT_SCALE is None:
            s = _jnp.sqrt(_jnp.mean(_jnp.square(w)) + 1e-30)
        else:
            s = MOMENT_SCALE[name]
        km, kv = _jax.random.split(_jax.random.fold_in(key, i + 1))
        out[name] = w
        out["m_" + name] = s * _jax.random.normal(km, w.shape, _jnp.float32)
        out["v_" + name] = (s * s) * _jax.random.uniform(kv, w.shape, _jnp.float32, 0.5, 1.5)
    if N_MICROBATCH > 1:
        for name, axis in PER_EXAMPLE_BATCH_AXIS.items():
            out[name] = _to_microbatches(out[name], axis)
    return {'x': out['x'], 'c': out['c'], 'ctx': out['ctx'], 'c_ctx': out['c_ctx'], 'w_mod': out['w_mod'], 'b_mod': out['b_mod'], 'g_mix': out['g_mix'], 'w_in': out['w_in'], 'b_gate': out['b_gate'], 'sink': out['sink'], 'w_spatial': out['w_spatial'], 'b_spatial': out['b_spatial'], 'g_v': out['g_v'], 'w_sconv': out['w_sconv'], 'w_branch': out['w_branch'], 'w_out': out['w_out'], 'g_ffn': out['g_ffn'], 'w_up': out['w_up'], 'w_fconv': out['w_fconv'], 'w_down': out['w_down'], 'g_final': out['g_final'], 'loss_target': out['loss_target'], 'm_c_ctx': out['m_c_ctx'], 'm_w_mod': out['m_w_mod'], 'm_b_mod': out['m_b_mod'], 'm_g_mix': out['m_g_mix'], 'm_w_in': out['m_w_in'], 'm_b_gate': out['m_b_gate'], 'm_sink': out['m_sink'], 'm_w_spatial': out['m_w_spatial'], 'm_b_spatial': out['m_b_spatial'], 'm_g_v': out['m_g_v'], 'm_w_sconv': out['m_w_sconv'], 'm_w_branch': out['m_w_branch'], 'm_w_out': out['m_w_out'], 'm_g_ffn': out['m_g_ffn'], 'm_w_up': out['m_w_up'], 'm_w_fconv': out['m_w_fconv'], 'm_w_down': out['m_w_down'], 'm_g_final': out['m_g_final'], 'v_c_ctx': out['v_c_ctx'], 'v_w_mod': out['v_w_mod'], 'v_b_mod': out['v_b_mod'], 'v_g_mix': out['v_g_mix'], 'v_w_in': out['v_w_in'], 'v_b_gate': out['v_b_gate'], 'v_sink': out['v_sink'], 'v_w_spatial': out['v_w_spatial'], 'v_b_spatial': out['v_b_spatial'], 'v_g_v': out['v_g_v'], 'v_w_sconv': out['v_w_sconv'], 'v_w_branch': out['v_w_branch'], 'v_w_out': out['v_w_out'], 'v_g_ffn': out['v_g_ffn'], 'v_w_up': out['v_w_up'], 'v_w_fconv': out['v_w_fconv'], 'v_w_down': out['v_w_down'], 'v_g_final': out['v_g_final']}


def _loss(weights, diff, rest, loss_target):
    with _jax.named_scope("forward"):
        args = {**rest, TWIN_DIFF_INPUT: diff, **{k: w.astype(_WEIGHT_DTYPES[k]) for k, w in weights.items()}}
        y = _forward(args)
    with _jax.named_scope("loss_head"):
        err = _jnp.square(y.astype(_jnp.float32) - loss_target)
        return 0.5 * _jnp.sum(_jnp.mean(err, axis=-1)) if err.ndim else 0.5 * err


def _adamw(w, g, m, v):
    m = ADAM_B1 * m + (1.0 - ADAM_B1) * g
    v = ADAM_B2 * v + (1.0 - ADAM_B2) * _jnp.square(g)
    m_hat = m / (1.0 - ADAM_B1 ** ADAM_STEP)
    v_hat = v / (1.0 - ADAM_B2 ** ADAM_STEP)
    delta = -ADAM_LR * (m_hat / (_jnp.sqrt(v_hat) + ADAM_EPS) + ADAM_WD * w)
    return delta, m, v


def reference(x, c, ctx, c_ctx, w_mod, b_mod, g_mix, w_in, b_gate, sink, w_spatial, b_spatial, g_v, w_sconv, w_branch, w_out, g_ffn, w_up, w_fconv, w_down, g_final, loss_target, m_c_ctx, m_w_mod, m_b_mod, m_g_mix, m_w_in, m_b_gate, m_sink, m_w_spatial, m_b_spatial, m_g_v, m_w_sconv, m_w_branch, m_w_out, m_g_ffn, m_w_up, m_w_fconv, m_w_down, m_g_final, v_c_ctx, v_w_mod, v_b_mod, v_g_mix, v_w_in, v_b_gate, v_sink, v_w_spatial, v_b_spatial, v_g_v, v_w_sconv, v_w_branch, v_w_out, v_g_ffn, v_w_up, v_w_fconv, v_w_down, v_g_final):
    given = dict(x=x, c=c, ctx=ctx, c_ctx=c_ctx, w_mod=w_mod, b_mod=b_mod, g_mix=g_mix, w_in=w_in, b_gate=b_gate, sink=sink, w_spatial=w_spatial, b_spatial=b_spatial, g_v=g_v, w_sconv=w_sconv, w_branch=w_branch, w_out=w_out, g_ffn=g_ffn, w_up=w_up, w_fconv=w_fconv, w_down=w_down, g_final=g_final, loss_target=loss_target, m_c_ctx=m_c_ctx, m_w_mod=m_w_mod, m_b_mod=m_b_mod, m_g_mix=m_g_mix, m_w_in=m_w_in, m_b_gate=m_b_gate, m_sink=m_sink, m_w_spatial=m_w_spatial, m_b_spatial=m_b_spatial, m_g_v=m_g_v, m_w_sconv=m_w_sconv, m_w_branch=m_w_branch, m_w_out=m_w_out, m_g_ffn=m_g_ffn, m_w_up=m_w_up, m_w_fconv=m_w_fconv, m_w_down=m_w_down, m_g_final=m_g_final, v_c_ctx=v_c_ctx, v_w_mod=v_w_mod, v_b_mod=v_b_mod, v_g_mix=v_g_mix, v_w_in=v_w_in, v_b_gate=v_b_gate, v_sink=v_sink, v_w_spatial=v_w_spatial, v_b_spatial=v_b_spatial, v_g_v=v_g_v, v_w_sconv=v_w_sconv, v_w_branch=v_w_branch, v_w_out=v_w_out, v_g_ffn=v_g_ffn, v_w_up=v_w_up, v_w_fconv=v_w_fconv, v_w_down=v_w_down, v_g_final=v_g_final)
    weights = {n: given[n] for n in TWIN_WEIGHTS}
    shared = {n: given[n] for n in SHARED_INPUTS}
    per_example = {n: given[n] for n in ['x', 'c', 'ctx']}
    grad_fn = _jax.value_and_grad(_loss, argnums=(0, 1))

    def one_microbatch(ex, loss_target):
        ex = dict(ex)
        diff = ex.pop(TWIN_DIFF_INPUT)
        return grad_fn(weights, diff, {**shared, **ex}, loss_target)

    if N_MICROBATCH == 1:
        loss, (grad_w, grad_x) = one_microbatch(per_example, given["loss_target"])
    else:
        def body(carry, xs):
            loss_sum, grad_sum = carry
            l_k, (gw_k, gx_k) = one_microbatch(xs[0], xs[1])
            with _jax.named_scope("update"):
                return (loss_sum + l_k, _jax.tree.map(_jnp.add, grad_sum, gw_k)), gx_k

        init = (_jnp.zeros((), _jnp.float32), _jax.tree.map(_jnp.zeros_like, weights))
        (loss, grad_w), grad_x = _jax.lax.scan(body, init, (per_example, given["loss_target"]))
    with _jax.named_scope("update"):
        delta_w, new_m, new_v = {}, {}, {}
        for n in TWIN_WEIGHTS:
            delta_w[n], new_m[n], new_v[n] = _adamw(weights[n], grad_w[n], given["m_" + n], given["v_" + n])
    return (loss, grad_x, *[grad_w[n] for n in TWIN_WEIGHTS], *[delta_w[n] for n in TWIN_WEIGHTS],
            *[new_m[n] for n in TWIN_WEIGHTS], *[new_v[n] for n in TWIN_WEIGHTS])
```

```python
import jax
import jax.numpy as jnp
from jax import lax
from jax.experimental import pallas as pl
from jax.experimental.pallas import tpu as pltpu

F32, BF16 = jnp.float32, jnp.bfloat16

D_MODEL = 1024
DEPTH = 2
GRID_W = 64
N_HEADS = 16
N_KV_HEADS = 4
HEAD_DIM = 64
WINDOW = 128
BLK = 128
ROPE_THETA = 10000.0
CHUNK = 128
A_GROUPS = 8
D_FF = 2816
EPS = 1e-6
NEG = -1e30
IN_W = 9728
OFF_K, OFF_A, OFF_B, OFF_G = 1024, 1536, 3584, 6656
P_Q, P_A, P_B, P_KV = 3072, 4096, 6144, 9216

N_DEV = 8
LANES = 128
ROW_TILE = 256
CONV_CHUNK = 256
PAD = 8
VMEM_LIMIT = 52 * 1024 * 1024

ADAM_LR, ADAM_B1, ADAM_B2, ADAM_EPS, ADAM_WD, ADAM_STEP = 0.001, 0.9, 0.999, 1e-08, 0.01, 10

HBM_SPEC = pl.BlockSpec(memory_space=pltpu.HBM)
SMEM_SPEC = pl.BlockSpec(memory_space=pltpu.SMEM)


def _params():
    return pltpu.CompilerParams(vmem_limit_bytes=VMEM_LIMIT)


def _pick(n, prefs):
    for p in prefs:
        if n % p == 0:
            return p
    raise ValueError((n, prefs))


def _sigmoid(x):
    return 1.0 / (1.0 + jnp.exp(-x))


def _mm(a, b, *, name, ta=False, tb=False, bm, bn, bk, out_dtype, a_lead=None, b_lead=None, b_spec=None):
    ash = a.shape[1:] if a_lead is not None else a.shape
    bsh = b.shape[1:] if b_lead is not None else b.shape
    kc, mo = (ash[0], ash[1]) if ta else (ash[1], ash[0])
    no = (bsh[0] if tb else bsh[1]) if b_spec is None else b_spec[1]
    assert mo % bm == 0 and no % bn == 0 and kc % bk == 0, (name, mo, no, kc, bm, bn, bk)
    nk = kc // bk

    def lead(shape, fn, idx):
        if idx is None:
            return pl.BlockSpec(shape, fn)
        return pl.BlockSpec((None,) + shape, lambda i, j, k: (idx,) + fn(i, j, k))

    a_spec = lead((bk, bm), lambda i, j, k: (k, i), a_lead) if ta else lead((bm, bk), lambda i, j, k: (i, k), a_lead)
    if b_spec is not None:
        b_bs = b_spec[0]
    elif tb:
        b_bs = lead((bn, bk), lambda i, j, k: (j, k), b_lead)
    else:
        b_bs = lead((bk, bn), lambda i, j, k: (k, j), b_lead)
    dims = (((0 if ta else 1,), (1 if tb else 0,)), ((), ()))

    def body(a_ref, b_ref, o_ref, *scratch):
        part = lax.dot_general(a_ref[...], b_ref[...], dims, preferred_element_type=F32)
        if nk == 1:
            o_ref[...] = part.astype(o_ref.dtype)
        else:
            acc = scratch[0]
            k = pl.program_id(2)

            @pl.when(k == 0)
            def _():
                acc[...] = part

            @pl.when(k > 0)
            def _():
                acc[...] += part

            @pl.when(k == nk - 1)
            def _():
                o_ref[...] = acc[...].astype(o_ref.dtype)

    return pl.pallas_call(
        body, name=name, grid=(mo // bm, no // bn, nk),
        in_specs=[a_spec, b_bs],
        out_specs=pl.BlockSpec((bm, bn), lambda i, j, k: (i, j)),
        out_shape=jax.ShapeDtypeStruct((mo, no), out_dtype),
        scratch_shapes=[pltpu.VMEM((bm, bn), F32)] if nk > 1 else [],
        compiler_params=_params(),
    )(a, b)


def _mm_nt_pieces(pieces, w, *, name, bm, bk):
    kout = w.shape[0]
    starts, total = [], 0
    for (_, _, _, nblk, _) in pieces:
        starts.append(total)
        total += nblk
    mo = pieces[0][0].shape[-2]
    assert mo % bm == 0

    def a_spec(p):
        arr, lead, col0, nblk, _ = pieces[p]
        s = starts[p]

        def fn(i, k):
            kk = jnp.clip(k - s, 0, nblk - 1) + col0
            return (i, kk) if lead is None else (lead, i, kk)
        shape = (bm, bk) if lead is None else (None, bm, bk)
        return pl.BlockSpec(shape, fn)

    def w_map(i, k):
        col = 0
        for p, (_, _, _, nblk, wcol0) in enumerate(pieces):
            inside = (k >= starts[p]) & (k < starts[p] + nblk)
            col = col + jnp.where(inside, wcol0 + k - starts[p], 0)
        return (0, col)

    n_p = len(pieces)
    dims = (((1,), (1,)), ((), ()))

    def body(*refs):
        a_refs, w_ref, o_ref, acc = refs[:n_p], refs[n_p], refs[n_p + 1], refs[n_p + 2]
        k = pl.program_id(1)

        @pl.when(k == 0)
        def _():
            acc[...] = jnp.zeros_like(acc)

        for p in range(n_p):
            nblk = pieces[p][3]

            @pl.when((k >= starts[p]) & (k < starts[p] + nblk))
            def _(p=p):
                acc[...] += lax.dot_general(a_refs[p][...], w_ref[...], dims, preferred_element_type=F32)

        @pl.when(k == total - 1)
        def _():
            o_ref[...] = acc[...]

    return pl.pallas_call(
        body, name=name, grid=(mo // bm, total),
        in_specs=[a_spec(p) for p in range(n_p)] + [pl.BlockSpec((kout, bk), w_map)],
        out_specs=pl.BlockSpec((bm, kout), lambda i, k: (i, 0)),
        out_shape=jax.ShapeDtypeStruct((mo, kout), F32),
        scratch_shapes=[pltpu.VMEM((bm, kout), F32)],
        compiler_params=_params(),
    )(*[p[0] for p in pieces], w)


def _exchange(arrs, modes, name):
    n = len(arrs)
    out_shapes = []
    for a, m in zip(arrs, modes):
        shape = (N_DEV,) + a.shape if m == "gather" else a.shape
        out_shapes.append(jax.ShapeDtypeStruct(shape, a.dtype))

    def body(*refs):
        ins, outs = refs[:n], refs[n:2 * n]
        send_sems, recv_sems, local_sems = refs[2 * n:]
        x, y, c = lax.axis_index("x"), lax.axis_index("y"), lax.axis_index("c")
        me = 4 * x + 2 * y + c

        def src(t, slot):
            return ins[t] if modes[t] == "gather" else ins[t].at[slot]

        local = [pltpu.make_async_copy(src(t, me), outs[t].at[me], local_sems.at[t]) for t in range(n)]
        for cp in local:
            cp.start()
        sends, recvs = [], []
        for j in range(1, N_DEV):
            px = 1 - x if (j >> 2) & 1 else x
            py = 1 - y if (j >> 1) & 1 else y
            pc = 1 - c if j & 1 else c
            peer = 4 * px + 2 * py + pc
            for t in range(n):
                sends.append(pltpu.make_async_remote_copy(
                    src_ref=src(t, peer), dst_ref=outs[t].at[me],
                    send_sem=send_sems.at[t, j - 1], recv_sem=recv_sems.at[t, j - 1],
                    device_id=(px, py, pc), device_id_type=pl.DeviceIdType.MESH))
                recvs.append(pltpu.make_async_remote_copy(
                    src_ref=src(t, peer), dst_ref=outs[t].at[peer],
                    send_sem=send_sems.at[t, j - 1], recv_sem=recv_sems.at[t, j - 1],
                    device_id=(px, py, pc), device_id_type=pl.DeviceIdType.MESH))
        for cp in sends:
            cp.start()
        for cp in recvs:
            cp.wait_recv()
        for cp in sends:
            cp.wait_send()
        for cp in local:
            cp.wait()

    outs = pl.pallas_call(
        body, name=name,
        in_specs=[HBM_SPEC] * n, out_specs=[HBM_SPEC] * n, out_shape=out_shapes,
        scratch_shapes=[pltpu.SemaphoreType.DMA((n, N_DEV - 1)), pltpu.SemaphoreType.DMA((n, N_DEV - 1)),
                        pltpu.SemaphoreType.DMA((n,))],
        compiler_params=pltpu.CompilerParams(has_side_effects=True),
    )(*arrs)
    return list(outs)


def _row_specs(r, n_lat, tm):
    nbl = n_lat // tm
    row = pl.BlockSpec((tm, D_MODEL), lambda i: (i, 0))
    mod = pl.BlockSpec((None, 8, D_MODEL), lambda i: (i // nbl, 0, 0))
    vec = pl.BlockSpec((1, D_MODEL), lambda i: (0, 0))
    return nbl, row, mod, vec


def _rows(vals):
    width = [v for v in vals if v is not None][0].shape[1]
    return jnp.concatenate([jnp.zeros((1, width), F32) if v is None else v for v in vals], axis=0)


def _norm_fwd(xs, g, mods, n_lat, sh, sc, name, res=None):
    r = xs.shape[0]
    tm = ROW_TILE
    nbl, row, mod, vec = _row_specs(r, n_lat, tm)

    def norm(x, g_ref, m_ref, h_ref):
        rs = lax.rsqrt(jnp.mean(x * x, axis=-1, keepdims=True) + EPS)
        m = m_ref[...]
        h_ref[...] = ((x * rs * g_ref[...]) * (1.0 + m[sc:sc + 1]) + m[sh:sh + 1]).astype(BF16)

    if res is None:
        def body(x_ref, g_ref, m_ref, h_ref):
            norm(x_ref[...], g_ref, m_ref, h_ref)
        return pl.pallas_call(
            body, name=name, grid=(r // tm,), in_specs=[row, vec, mod], out_specs=row,
            out_shape=jax.ShapeDtypeStruct((r, D_MODEL), BF16), compiler_params=_params())(xs, g, mods)

    o, mods_res, gt = res

    def body(x_ref, o_ref, mr_ref, g_ref, m_ref, x1_ref, h_ref):
        x = x_ref[...] + mr_ref[...][gt:gt + 1] * o_ref[...]
        x1_ref[...] = x
        norm(x, g_ref, m_ref, h_ref)

    return pl.pallas_call(
        body, name=name, grid=(r // tm,), in_specs=[row, row, mod, vec, mod], out_specs=[row, row],
        out_shape=[jax.ShapeDtypeStruct((r, D_MODEL), F32), jax.ShapeDtypeStruct((r, D_MODEL), BF16)],
        compiler_params=_params())(xs, o, mods_res, g, mods)


def _rms_bwd(x, g, dy):
    rs = lax.rsqrt(jnp.mean(x * x, axis=-1, keepdims=True) + EPS)
    xh = x * rs
    dxh = dy * g
    dx = rs * (dxh - xh * jnp.mean(dxh * xh, axis=-1, keepdims=True))
    return dx, dy * xh, xh


def _acc_specs(nbl):
    acc_all = pl.BlockSpec((8, D_MODEL), lambda i: (0, 0))
    acc_stream = pl.BlockSpec((None, 8, D_MODEL), lambda i: (i // nbl, 0, 0))
    return acc_all, acc_stream


def _loss_bwd(xs, o, mods, g_final, target, n_lat, name):
    r = xs.shape[0]
    tm = ROW_TILE
    nbl, row, mod, vec = _row_specs(r, n_lat, tm)
    acc_all, acc_stream = _acc_specs(nbl)
    tgt = pl.BlockSpec((tm, D_MODEL), lambda i: (jnp.minimum(i, nbl - 1), 0))

    def body(x_ref, o_ref, m_ref, g_ref, t_ref, dx_ref, do_ref, acc_ref, accs_ref):
        i = pl.program_id(0)
        lat = i < nbl
        gate = m_ref[...][5:6]
        o_val = o_ref[...]
        x = x_ref[...] + gate * o_val
        g = g_ref[...]
        rs = lax.rsqrt(jnp.mean(x * x, axis=-1, keepdims=True) + EPS)
        y = x * rs * g
        err = jnp.where(lat, y - t_ref[...], 0.0)
        loss = 0.5 * jnp.sum(jnp.mean(err * err, axis=-1, keepdims=True), axis=0, keepdims=True)
        dy = err * (1.0 / D_MODEL)
        dx, dg_rows, _ = _rms_bwd(x, g, dy)
        dx_ref[...] = dx
        do_ref[...] = (gate * dx).astype(BF16)

        @pl.when(i == 0)
        def _():
            acc_ref[...] = jnp.zeros_like(acc_ref)

        @pl.when((i == 0) | (i == nbl))
        def _():
            accs_ref[...] = jnp.zeros_like(accs_ref)

        acc_ref[...] += _rows([jnp.sum(dg_rows, axis=0, keepdims=True), jnp.broadcast_to(loss, (1, D_MODEL))]
                              + [None] * 6)
        accs_ref[...] += _rows([None, None, jnp.sum(dx * o_val, axis=0, keepdims=True)] + [None] * 5)

    return pl.pallas_call(
        body, name=name, grid=(r // tm,), in_specs=[row, row, mod, vec, tgt],
        out_specs=[row, row, acc_all, acc_stream],
        out_shape=[jax.ShapeDtypeStruct((r, D_MODEL), F32), jax.ShapeDtypeStruct((r, D_MODEL), BF16),
                   jax.ShapeDtypeStruct((8, D_MODEL), F32), jax.ShapeDtypeStruct((2, 8, D_MODEL), F32)],
        compiler_params=_params())(xs, o, mods, g_final, target)


def _norm_bwd(xs, g, mods, dh, dx_in, n_lat, sh, sc, name, res=None):
    r = xs.shape[0]
    tm = ROW_TILE
    nbl, row, mod, vec = _row_specs(r, n_lat, tm)
    acc_all, acc_stream = _acc_specs(nbl)
    has_res = res is not None

    def body(*refs):
        if has_res:
            x_ref, g_ref, m_ref, dh_ref, dxi_ref, o_ref, mr_ref, dx_ref, do_ref, acc_ref, accs_ref = refs
        else:
            x_ref, g_ref, m_ref, dh_ref, dxi_ref, dx_ref, acc_ref, accs_ref = refs
        i = pl.program_id(0)
        x, g, m, dhv = x_ref[...], g_ref[...], m_ref[...], dh_ref[...]
        dy = dhv * (1.0 + m[sc:sc + 1])
        dxn, dg_rows, xh = _rms_bwd(x, g, dy)
        dx = dxi_ref[...] + dxn
        dx_ref[...] = dx
        d_gate = None
        if has_res:
            o_val = o_ref[...]
            do_ref[...] = (mr_ref[...][res[2]:res[2] + 1] * dx).astype(BF16)
            d_gate = jnp.sum(dx * o_val, axis=0, keepdims=True)

        @pl.when(i == 0)
        def _():
            acc_ref[...] = jnp.zeros_like(acc_ref)

        @pl.when((i == 0) | (i == nbl))
        def _():
            accs_ref[...] = jnp.zeros_like(accs_ref)

        acc_ref[...] += _rows([jnp.sum(dg_rows, axis=0, keepdims=True)] + [None] * 7)
        accs_ref[...] += _rows([jnp.sum(dhv, axis=0, keepdims=True),
                                jnp.sum(dhv * (xh * g), axis=0, keepdims=True), d_gate] + [None] * 5)

    ins = [xs, g, mods, dh, dx_in]
    in_specs = [row, vec, mod, row, row]
    out_specs = [row]
    out_shape = [jax.ShapeDtypeStruct((r, D_MODEL), F32)]
    if has_res:
        ins += [res[0], res[1]]
        in_specs += [row, mod]
        out_specs.append(row)
        out_shape.append(jax.ShapeDtypeStruct((r, D_MODEL), BF16))
    out_specs += [acc_all, acc_stream]
    out_shape += [jax.ShapeDtypeStruct((8, D_MODEL), F32), jax.ShapeDtypeStruct((2, 8, D_MODEL), F32)]
    return pl.pallas_call(body, name=name, grid=(r // tm,), in_specs=in_specs, out_specs=out_specs,
                          out_shape=out_shape, compiler_params=_params())(*ins)


def _rotate(t, cos, sin):
    width = t.shape[1]
    reps = width // LANES
    lane = lax.broadcasted_iota(jnp.int32, (1, width), 1)
    first = (lane % HEAD_DIM) < (HEAD_DIM // 2)
    swapped = jnp.where(first, pltpu.roll(t, width - HEAD_DIM // 2, 1), pltpu.roll(t, HEAD_DIM // 2, 1))
    return t * jnp.tile(cos, (1, reps)) + swapped * jnp.tile(sin, (1, reps))


def _rope_fwd(z, cos, sin, name):
    r = z.shape[0]
    tm = ROW_TILE
    kvw = 2 * N_KV_HEADS * HEAD_DIM
    tab = pl.BlockSpec((tm, LANES), lambda i: (i, 0))

    def body(q_ref, kv_ref, c_ref, s_ref, o_ref):
        c, s = c_ref[...], s_ref[...]
        kv = kv_ref[...]
        o_ref[:, :D_MODEL] = _rotate(q_ref[...].astype(F32), c, s).astype(BF16)
        o_ref[:, D_MODEL:D_MODEL + kvw // 2] = _rotate(kv[:, :kvw // 2].astype(F32), c, s).astype(BF16)
        o_ref[:, D_MODEL + kvw // 2:] = kv[:, kvw // 2:]

    return pl.pallas_call(
        body, name=name, grid=(r // tm,),
        in_specs=[pl.BlockSpec((tm, D_MODEL), lambda i: (i, P_Q // D_MODEL)),
                  pl.BlockSpec((tm, kvw), lambda i: (i, P_KV // kvw)), tab, tab],
        out_specs=pl.BlockSpec((tm, D_MODEL + kvw), lambda i: (i, 0)),
        out_shape=jax.ShapeDtypeStruct((r, D_MODEL + kvw), BF16), compiler_params=_params())(z, z, cos, sin)


def _rope_bwd(dq, dk, dv, cos, sin, name):
    r = dq.shape[0]
    tm = ROW_TILE
    kw = N_KV_HEADS * HEAD_DIM
    tab = pl.BlockSpec((tm, LANES), lambda i: (i, 0))

    def body(dq_ref, dk_ref, dv_ref, c_ref, s_ref, o_ref):
        c, s = c_ref[...], -s_ref[...]
        o_ref[:, :D_MODEL] = _rotate(dq_ref[...], c, s).astype(BF16)
        o_ref[:, D_MODEL:D_MODEL + kw] = _rotate(dk_ref[...], c, s).astype(BF16)
        o_ref[:, D_MODEL + kw:] = dv_ref[...].astype(BF16)

    return pl.pallas_call(
        body, name=name, grid=(r // tm,),
        in_specs=[pl.BlockSpec((tm, D_MODEL), lambda i: (i, 0)), pl.BlockSpec((tm, kw), lambda i: (i, 0)),
                  pl.BlockSpec((tm, kw), lambda i: (i, 0)), tab, tab],
        out_specs=pl.BlockSpec((tm, D_MODEL + 2 * kw), lambda i: (i, 0)),
        out_shape=jax.ShapeDtypeStruct((r, D_MODEL + 2 * kw), BF16), compiler_params=_params())(dq, dk, dv, cos, sin)


def _attn_setup(i, n_lat, m_ctx, nbl, k_ref, v_ref):
    start = pl.multiple_of(jnp.clip((i - 1) * BLK, 0, n_lat - 3 * BLK), BLK)
    nkeys = 3 * BLK + m_ctx
    rows = lax.broadcasted_iota(jnp.int32, (4 * BLK, nkeys), 0)
    cols = lax.broadcasted_iota(jnp.int32, (4 * BLK, nkeys), 1)
    qpos = i * BLK + (rows & (BLK - 1))
    mask = (cols >= 3 * BLK) | ((jnp.abs(start + cols - qpos) <= WINDOW) & (i < nbl))
    kblk = jnp.concatenate([k_ref[pl.ds(start, 3 * BLK), :], k_ref[pl.ds(n_lat, m_ctx), :]], axis=0)
    vblk = jnp.concatenate([v_ref[pl.ds(start, 3 * BLK), :], v_ref[pl.ds(n_lat, m_ctx), :]], axis=0)
    lo = lax.broadcasted_iota(jnp.int32, (1, LANES), 1) < HEAD_DIM
    return start, mask, kblk, vblk, lo


def _stack_heads(ref, kh, lo):
    a = ref[:, (2 * kh) * LANES:(2 * kh + 1) * LANES]
    b = ref[:, (2 * kh + 1) * LANES:(2 * kh + 2) * LANES]
    z = jnp.zeros_like(a)
    return jnp.concatenate([jnp.where(lo, a, z), jnp.where(lo, z, a), jnp.where(lo, b, z), jnp.where(lo, z, b)],
                           axis=0)


def _kv_variants(blk, rolled, kh, lo):
    z = jnp.zeros_like(blk)
    if kh == 0:
        return jnp.where(lo, blk, rolled), jnp.where(lo, blk, z), jnp.where(lo, z, rolled)
    return jnp.where(lo, rolled, blk), jnp.where(lo, rolled, z), jnp.where(lo, z, blk)


NT = (((1,), (1,)), ((), ()))
TN = (((0,), (0,)), ((), ()))


def _scores(qst, k2, mask, sink_ref, p, kh):
    s = lax.dot_general(qst, k2, NT, preferred_element_type=F32) * (HEAD_DIM ** -0.5)
    s = jnp.where(mask, s, NEG)
    snk = jnp.concatenate([jnp.full((BLK, 1), sink_ref[p * 8 + kh * 4 + g], F32) for g in range(4)], axis=0)
    return s, snk


def _attn_fwd(qkv, sink, n_lat, m_ctx, name):
    r = qkv.shape[0]
    nb, nbl = r // BLK, n_lat // BLK
    kcol = D_MODEL // LANES

    def body(sink_ref, q_ref, k_ref, v_ref, o_ref, lse_ref):
        p, i = pl.program_id(0), pl.program_id(1)
        _, mask, kblk, vblk, lo = _attn_setup(i, n_lat, m_ctx, nbl, k_ref, v_ref)
        kr, vr = pltpu.roll(kblk, HEAD_DIM, 1), pltpu.roll(vblk, HEAD_DIM, 1)
        for kh in range(2):
            k2, _, _ = _kv_variants(kblk, kr, kh, lo)
            _, vlo, vhi = _kv_variants(vblk, vr, kh, lo)
            qst = _stack_heads(q_ref, kh, lo)
            s, snk = _scores(qst, k2, mask, sink_ref, p, kh)
            mx = jnp.maximum(jnp.max(s, axis=-1, keepdims=True), snk)
            pe = jnp.exp(s - mx)
            den = jnp.sum(pe, axis=-1, keepdims=True) + jnp.exp(snk - mx)
            inv = 1.0 / den
            pb = pe.astype(BF16)
            for jp in range(2):
                r0 = 2 * jp * BLK
                pair = (jnp.dot(pb[r0:r0 + BLK], vlo, preferred_element_type=F32)
                        + jnp.dot(pb[r0 + BLK:r0 + 2 * BLK], vhi, preferred_element_type=F32))
                pair = pair * jnp.where(lo, inv[r0:r0 + BLK], inv[r0 + BLK:r0 + 2 * BLK])
                o_ref[:, (2 * kh + jp) * LANES:(2 * kh + jp + 1) * LANES] = pair.astype(BF16)
            lse = mx + jnp.log(den)
            for g in range(4):
                lse_ref[:, kh * 4 + g:kh * 4 + g + 1] = lse[g * BLK:(g + 1) * BLK]

    return pl.pallas_call(
        body, name=name, grid=(2, nb),
        in_specs=[SMEM_SPEC,
                  pl.BlockSpec((BLK, 4 * LANES), lambda p, i: (i, p)),
                  pl.BlockSpec((r, LANES), lambda p, i: (0, kcol + p)),
                  pl.BlockSpec((r, LANES), lambda p, i: (0, kcol + 2 + p))],
        out_specs=[pl.BlockSpec((BLK, 4 * LANES), lambda p, i: (i, p)),
                   pl.BlockSpec((None, BLK, 8), lambda p, i: (p, i, 0))],
        out_shape=[jax.ShapeDtypeStruct((r, D_MODEL), BF16), jax.ShapeDtypeStruct((2, r, 8), F32)],
        compiler_params=_params())(sink, qkv, qkv, qkv)


def _attn_bwd(qkv, sink, o, do, lse, n_lat, m_ctx, name):
    r = qkv.shape[0]
    nb, nbl = r // BLK, n_lat // BLK
    kcol = D_MODEL // LANES

    def body(sink_ref, q_ref, k_ref, v_ref, o_ref, do_ref, lse_ref, dq_ref, dk_ref, dv_ref, ds_ref):
        p, i = pl.program_id(0), pl.program_id(1)

        @pl.when(i == 0)
        def _():
            dk_ref[...] = jnp.zeros_like(dk_ref)
            dv_ref[...] = jnp.zeros_like(dv_ref)
            ds_ref[...] = jnp.zeros_like(ds_ref)

        start, mask, kblk, vblk, lo = _attn_setup(i, n_lat, m_ctx, nbl, k_ref, v_ref)
        kr, vr = pltpu.roll(kblk, HEAD_DIM, 1), pltpu.roll(vblk, HEAD_DIM, 1)
        lane = lax.broadcasted_iota(jnp.int32, (1, LANES), 1)
        dks, dvs = [], []
        for kh in range(2):
            k2, klo, khi = _kv_variants(kblk, kr, kh, lo)
            v2, _, _ = _kv_variants(vblk, vr, kh, lo)
            qst = _stack_heads(q_ref, kh, lo)
            dost = _stack_heads(do_ref, kh, lo)
            s, snk = _scores(qst, k2, mask, sink_ref, p, kh)
            lse4 = jnp.concatenate([lse_ref[:, kh * 4 + g:kh * 4 + g + 1] for g in range(4)], axis=0)
            pe = jnp.exp(s - lse4)
            dp = lax.dot_general(dost, v2, NT, preferred_element_type=F32)
            deltas = []
            for jp in range(2):
                cols = slice((2 * kh + jp) * LANES, (2 * kh + jp + 1) * LANES)
                prod = do_ref[:, cols].astype(F32) * o_ref[:, cols].astype(F32)
                deltas.append(jnp.sum(jnp.where(lo, prod, 0.0), axis=-1, keepdims=True))
                deltas.append(jnp.sum(jnp.where(lo, 0.0, prod), axis=-1, keepdims=True))
            delta = jnp.concatenate(deltas, axis=0)
            dsc = pe * (dp - delta) * (HEAD_DIM ** -0.5)
            dsb, pb = dsc.astype(BF16), pe.astype(BF16)
            for jp in range(2):
                r0 = 2 * jp * BLK
                dq_ref[:, (2 * kh + jp) * LANES:(2 * kh + jp + 1) * LANES] = (
                    jnp.dot(dsb[r0:r0 + BLK], klo, preferred_element_type=F32)
                    + jnp.dot(dsb[r0 + BLK:r0 + 2 * BLK], khi, preferred_element_type=F32))
            dkf = lax.dot_general(dsb, qst, TN, preferred_element_type=F32)
            dvf = lax.dot_general(pb, dost, TN, preferred_element_type=F32)
            dks.append(dkf + pltpu.roll(dkf, HEAD_DIM, 1))
            dvs.append(dvf + pltpu.roll(dvf, HEAD_DIM, 1))
            contrib = -jnp.exp(snk - lse4) * delta
            for g in range(4):
                tot = jnp.sum(contrib[g * BLK:(g + 1) * BLK], axis=0, keepdims=True)
                ds_ref[0:1, :] += jnp.where(lane == kh * 4 + g, tot, 0.0)
        dk_blk = jnp.where(lo, dks[0], dks[1])
        dv_blk = jnp.where(lo, dvs[0], dvs[1])
        dk_ref[pl.ds(start, 3 * BLK), :] += dk_blk[:3 * BLK]
        dk_ref[pl.ds(n_lat, m_ctx), :] += dk_blk[3 * BLK:]
        dv_ref[pl.ds(start, 3 * BLK), :] += dv_blk[:3 * BLK]
        dv_ref[pl.ds(n_lat, m_ctx), :] += dv_blk[3 * BLK:]

    qspec = pl.BlockSpec((BLK, 4 * LANES), lambda p, i: (i, p))
    return pl.pallas_call(
        body, name=name, grid=(2, nb),
        in_specs=[SMEM_SPEC, qspec,
                  pl.BlockSpec((r, LANES), lambda p, i: (0, kcol + p)),
                  pl.BlockSpec((r, LANES), lambda p, i: (0, kcol + 2 + p)),
                  qspec, qspec, pl.BlockSpec((None, BLK, 8), lambda p, i: (p, i, 0))],
        out_specs=[qspec, pl.BlockSpec((r, LANES), lambda p, i: (0, p)), pl.BlockSpec((r, LANES), lambda p, i: (0, p)),
                   pl.BlockSpec((None, 8, LANES), lambda p, i: (p, 0, 0))],
        out_shape=[jax.ShapeDtypeStruct((r, D_MODEL), F32), jax.ShapeDtypeStruct((r, 2 * LANES), F32),
                   jax.ShapeDtypeStruct((r, 2 * LANES), F32), jax.ShapeDtypeStruct((2, 8, LANES), F32)],
        compiler_params=_params())(sink, qkv, qkv, qkv, o, do, lse)


def _gating_parts(z_ref, gv_ref):
    za = z_ref[...].astype(F32)
    zg = jax.nn.gelu(za)
    u, v = zg[:, :D_MODEL], zg[:, D_MODEL:]
    rs = lax.rsqrt(jnp.mean(v * v, axis=-1, keepdims=True) + EPS)
    return za, u, v, rs, v * rs * gv_ref[...]


def _mix(w_ref, vals):
    vb = vals.astype(BF16)
    return jnp.concatenate(
        [jnp.dot(w_ref[g], vb[:, g * LANES:(g + 1) * LANES], preferred_element_type=F32) for g in range(A_GROUPS)],
        axis=1)


def _gating_fwd(z, ws, bias, g_v, name):
    r = z.shape[0]

    def body(z_ref, w_ref, b_ref, gv_ref, y_ref):
        _, u, _, _, vn = _gating_parts(z_ref, gv_ref)
        y_ref[...] = (u * (_mix(w_ref, vn) + b_ref[...])).astype(BF16)

    return pl.pallas_call(
        body, name=name, grid=(r // CHUNK,),
        in_specs=[pl.BlockSpec((CHUNK, 2 * D_MODEL), lambda i: (i, P_A // (2 * D_MODEL))),
                  pl.BlockSpec((A_GROUPS, CHUNK, CHUNK), lambda i: (0, 0, 0)),
                  pl.BlockSpec((CHUNK, D_MODEL), lambda i: (0, 0)),
                  pl.BlockSpec((1, D_MODEL), lambda i: (0, 0))],
        out_specs=pl.BlockSpec((CHUNK, D_MODEL), lambda i: (i, 0)),
        out_shape=jax.ShapeDtypeStruct((r, D_MODEL), BF16), compiler_params=_params())(z, ws, bias, g_v)


def _gating_bwd(z, ws, ws_t, bias, g_v, dy, name):
    r = z.shape[0]

    def body(z_ref, w_ref, wt_ref, b_ref, gv_ref, dy_ref, dz_ref, dw_ref, db_ref, dg_ref):
        i = pl.program_id(0)

        @pl.when(i == 0)
        def _():
            dw_ref[...] = jnp.zeros_like(dw_ref)
            db_ref[...] = jnp.zeros_like(db_ref)
            dg_ref[...] = jnp.zeros_like(dg_ref)

        za, u, v, rs, vn = _gating_parts(z_ref, gv_ref)
        dyv = dy_ref[...].astype(F32)
        du = dyv * (_mix(w_ref, vn) + b_ref[...])
        dmixed = dyv * u
        dvn = _mix(wt_ref, dmixed)
        dmb, vnb = dmixed.astype(BF16), vn.astype(BF16)
        for g in range(A_GROUPS):
            cols = slice(g * LANES, (g + 1) * LANES)
            dw_ref[g] += lax.dot_general(dmb[:, cols], vnb[:, cols], NT, preferred_element_type=F32)
            db_ref[:, g:g + 1] += jnp.sum(dmixed[:, cols], axis=-1, keepdims=True)
        gv = gv_ref[...]
        vh = v * rs
        dg_ref[0:1, :] += jnp.sum(dvn * vh, axis=0, keepdims=True)
        dvh = dvn * gv
        dv = rs * (dvh - vh * jnp.mean(dvh * vh, axis=-1, keepdims=True))
        _, vjp = jax.vjp(jax.nn.gelu, za)
        dz_ref[...] = vjp(jnp.concatenate([du, dv], axis=1))[0].astype(BF16)

    wspec = pl.BlockSpec((A_GROUPS, CHUNK, CHUNK), lambda i: (0, 0, 0))
    return pl.pallas_call(
        body, name=name, grid=(r // CHUNK,),
        in_specs=[pl.BlockSpec((CHUNK, 2 * D_MODEL), lambda i: (i, P_A // (2 * D_MODEL))), wspec, wspec,
                  pl.BlockSpec((CHUNK, D_MODEL), lambda i: (0, 0)), pl.BlockSpec((1, D_MODEL), lambda i: (0, 0)),
                  pl.BlockSpec((CHUNK, D_MODEL), lambda i: (i, 0))],
        out_specs=[pl.BlockSpec((CHUNK, 2 * D_MODEL), lambda i: (i, 0)), wspec,
                   pl.BlockSpec((CHUNK, A_GROUPS), lambda i: (0, 0)), pl.BlockSpec((8, D_MODEL), lambda i: (0, 0))],
        out_shape=[jax.ShapeDtypeStruct((r, 2 * D_MODEL), BF16), jax.ShapeDtypeStruct((A_GROUPS, CHUNK, CHUNK), F32),
                   jax.ShapeDtypeStruct((CHUNK, A_GROUPS), F32), jax.ShapeDtypeStruct((8, D_MODEL), F32)],
        compiler_params=_params())(z, ws, ws_t, bias, g_v, dy)


def _taps(scr, j, n_lat):
    c = CONV_CHUNK
    s = pl.multiple_of(j * c, c)
    ext = scr[pl.ds(s, c + 2 * PAD), :]
    t = s + lax.broadcasted_iota(jnp.int32, (c, 1), 0)
    xm = jnp.where(t == n_lat, 0.0, pltpu.roll(ext, 1, 0)[PAD:PAD + c])
    xp = jnp.where(t == n_lat - 1, 0.0, pltpu.roll(ext, c + 2 * PAD - 1, 0)[PAD:PAD + c])
    return s, xm, ext[PAD:PAD + c], xp


def _zero_pads(scr, r):
    scr[pl.ds(0, PAD), :] = jnp.zeros((PAD, LANES), F32)
    scr[pl.ds(PAD + r, PAD), :] = jnp.zeros((PAD, LANES), F32)


def _col(arr_cols, c0):
    return pl.BlockSpec((arr_cols, LANES), lambda c: (0, c0 + c))


def _ffn_conv_fwd(up, w, n_lat, name):
    r = up.shape[0]
    nct = D_FF // LANES
    nchunk = r // CONV_CHUNK

    def body(a_ref, g_ref, w_ref, f_ref, scr):
        _zero_pads(scr, r)

        def fill(j, _):
            s = pl.multiple_of(j * CONV_CHUNK, CONV_CHUNK)
            scr[pl.ds(PAD + s, CONV_CHUNK), :] = a_ref[pl.ds(s, CONV_CHUNK), :].astype(F32)
            return 0
        lax.fori_loop(0, nchunk, fill, 0)
        wv = w_ref[...]

        def step(j, _):
            s, xm, x0, xp = _taps(scr, j, n_lat)
            ca = wv[0:1] * xm + wv[1:2] * x0 + wv[2:3] * xp
            gv = g_ref[pl.ds(s, CONV_CHUNK), :].astype(F32)
            f_ref[pl.ds(s, CONV_CHUNK), :] = (ca * _sigmoid(ca) * gv).astype(BF16)
            return 0
        lax.fori_loop(0, nchunk, step, 0)

    return pl.pallas_call(
        body, name=name, grid=(nct,),
        in_specs=[_col(r, 0), _col(r, nct), _col(3, 0)],
        out_specs=_col(r, 0), out_shape=jax.ShapeDtypeStruct((r, D_FF), BF16),
        scratch_shapes=[pltpu.VMEM((r + 2 * PAD, LANES), F32)], compiler_params=_params())(up, up, w)


def _ffn_conv_bwd(up, w, df, n_lat, name):
    r = up.shape[0]
    nct = D_FF // LANES
    nchunk = r // CONV_CHUNK

    def body(a_ref, g_ref, w_ref, df_ref, dup_ref, dw_ref, scr, scr2):
        _zero_pads(scr, r)
        _zero_pads(scr2, r)

        def fill(j, _):
            s = pl.multiple_of(j * CONV_CHUNK, CONV_CHUNK)
            scr[pl.ds(PAD + s, CONV_CHUNK), :] = a_ref[pl.ds(s, CONV_CHUNK), :].astype(F32)
            return 0
        lax.fori_loop(0, nchunk, fill, 0)
        wv = w_ref[...]

        def first(j, carry):
            s, xm, x0, xp = _taps(scr, j, n_lat)
            ca = wv[0:1] * xm + wv[1:2] * x0 + wv[2:3] * xp
            sg = _sigmoid(ca)
            gv = g_ref[pl.ds(s, CONV_CHUNK), :].astype(F32)
            dfv = df_ref[pl.ds(s, CONV_CHUNK), :].astype(F32)
            dup_ref[1, pl.ds(s, CONV_CHUNK), :] = (dfv * ca * sg).astype(BF16)
            dca = dfv * gv * (sg * (1.0 + ca * (1.0 - sg)))
            scr2[pl.ds(PAD + s, CONV_CHUNK), :] = dca
            return tuple(cw + jnp.sum(dca * xv, axis=0, keepdims=True) for cw, xv in zip(carry, (xm, x0, xp)))
        zero = jnp.zeros((1, LANES), F32)
        dws = lax.fori_loop(0, nchunk, first, (zero, zero, zero))
        dw_ref[...] = _rows(list(dws) + [None] * 5)

        def second(j, _):
            s, ym, y0, yp = _taps(scr2, j, n_lat)
            dup_ref[0, pl.ds(s, CONV_CHUNK), :] = (wv[0:1] * yp + wv[1:2] * y0 + wv[2:3] * ym).astype(BF16)
            return 0
        lax.fori_loop(0, nchunk, second, 0)

    return pl.pallas_call(
        body, name=name, grid=(nct,),
        in_specs=[_col(r, 0), _col(r, nct), _col(3, 0), _col(r, 0)],
        out_specs=[pl.BlockSpec((2, r, LANES), lambda c: (0, 0, c)), _col(8, 0)],
        out_shape=[jax.ShapeDtypeStruct((2, r, D_FF), BF16), jax.ShapeDtypeStruct((8, D_FF), F32)],
        scratch_shapes=[pltpu.VMEM((r + 2 * PAD, LANES), F32), pltpu.VMEM((r + 2 * PAD, LANES), F32)],
        compiler_params=_params())(up, up, w, df)


def _sconv_fwd(z, w, n_lat, name):
    r = z.shape[0]
    nct = D_MODEL // LANES
    nchunk = r // CONV_CHUNK
    c0 = P_B // LANES

    def body(bg_ref, cg_ref, hb_ref, w_ref, y_ref, scr):
        _zero_pads(scr, r)

        def fill(j, _):
            s = pl.multiple_of(j * CONV_CHUNK, CONV_CHUNK)
            rows = pl.ds(s, CONV_CHUNK)
            scr[pl.ds(PAD + s, CONV_CHUNK), :] = cg_ref[rows, :].astype(F32) * hb_ref[rows, :].astype(F32)
            return 0
        lax.fori_loop(0, nchunk, fill, 0)
        wv = w_ref[...]

        def step(j, _):
            s, xm, x0, xp = _taps(scr, j, n_lat)
            conv = wv[0:1] * xm + wv[1:2] * x0 + wv[2:3] * xp
            y_ref[pl.ds(s, CONV_CHUNK), :] = (bg_ref[pl.ds(s, CONV_CHUNK), :].astype(F32) * conv).astype(BF16)
            return 0
        lax.fori_loop(0, nchunk, step, 0)

    return pl.pallas_call(
        body, name=name, grid=(nct,),
        in_specs=[_col(r, c0), _col(r, c0 + nct), _col(r, c0 + 2 * nct), _col(3, 0)],
        out_specs=_col(r, 0), out_shape=jax.ShapeDtypeStruct((r, D_MODEL), BF16),
        scratch_shapes=[pltpu.VMEM((r + 2 * PAD, LANES), F32)], compiler_params=_params())(z, z, z, w)


def _sconv_bwd(z, w, dy, n_lat, name):
    r = z.shape[0]
    nct = D_MODEL // LANES
    nchunk = r // CONV_CHUNK
    c0 = P_B // LANES

    def body(bg_ref, cg_ref, hb_ref, w_ref, dy_ref, dz_ref, dw_ref, scr, scr2):
        _zero_pads(scr, r)
        _zero_pads(scr2, r)

        def fill(j, _):
            s = pl.multiple_of(j * CONV_CHUNK, CONV_CHUNK)
            rows = pl.ds(s, CONV_CHUNK)
            scr[pl.ds(PAD + s, CONV_CHUNK), :] = cg_ref[rows, :].astype(F32) * hb_ref[rows, :].astype(F32)
            return 0
        lax.fori_loop(0, nchunk, fill, 0)
        wv = w_ref[...]

        def first(j, carry):
            s, xm, x0, xp = _taps(scr, j, n_lat)
            rows = pl.ds(s, CONV_CHUNK)
            conv = wv[0:1] * xm + wv[1:2] * x0 + wv[2:3] * xp
            dyv = dy_ref[rows, :].astype(F32)
            dz_ref[0, rows, :] = (dyv * conv).astype(BF16)
            dconv = dyv * bg_ref[rows, :].astype(F32)
            scr2[pl.ds(PAD + s, CONV_CHUNK), :] = dconv
            return tuple(cw + jnp.sum(dconv * xv, axis=0, keepdims=True) for cw, xv in zip(carry, (xm, x0, xp)))
        zero = jnp.zeros((1, LANES), F32)
        dws = lax.fori_loop(0, nchunk, first, (zero, zero, zero))
        dw_ref[...] = _rows(list(dws) + [None] * 5)

        def second(j, _):
            s, ym, y0, yp = _taps(scr2, j, n_lat)
            rows = pl.ds(s, CONV_CHUNK)
            dx = wv[0:1] * yp + wv[1:2] * y0 + wv[2:3] * ym
            dz_ref[1, rows, :] = (dx * hb_ref[rows, :].astype(F32)).astype(BF16)
            dz_ref[2, rows, :] = (dx * cg_ref[rows, :].astype(F32)).astype(BF16)
            return 0
        lax.fori_loop(0, nchunk, second, 0)

    return pl.pallas_call(
        body, name=name, grid=(nct,),
        in_specs=[_col(r, c0), _col(r, c0 + nct), _col(r, c0 + 2 * nct), _col(3, 0), _col(r, 0)],
        out_specs=[pl.BlockSpec((3, r, LANES), lambda c: (0, 0, c)), _col(8, 0)],
        out_shape=[jax.ShapeDtypeStruct((3, r, D_MODEL), BF16), jax.ShapeDtypeStruct((8, D_MODEL), F32)],
        scratch_shapes=[pltpu.VMEM((r + 2 * PAD, LANES), F32), pltpu.VMEM((r + 2 * PAD, LANES), F32)],
        compiler_params=_params())(z, z, z, w, dy)


def _merge_fwd(z, b_gate, ps, name):
    r = z.shape[0]
    tm = ROW_TILE
    row = pl.BlockSpec((tm, D_MODEL), lambda i: (i, 0))

    def body(zg_ref, b_ref, p0_ref, p1_ref, p2_ref, o_ref):
        gates = _sigmoid(zg_ref[...].astype(F32) + b_ref[...])
        acc = None
        for t, p_ref in enumerate((p0_ref, p1_ref, p2_ref)):
            term = gates[:, t * D_MODEL:(t + 1) * D_MODEL] * p_ref[...].astype(F32)
            acc = term if acc is None else acc + term
        o_ref[...] = acc.astype(BF16)

    return pl.pallas_call(
        body, name=name, grid=(r // tm,),
        in_specs=[pl.BlockSpec((tm, 3 * D_MODEL), lambda i: (i, 0)), pl.BlockSpec((1, 3 * D_MODEL), lambda i: (0, 0)),
                  row, row, row],
        out_specs=row, out_shape=jax.ShapeDtypeStruct((r, D_MODEL), BF16), compiler_params=_params())(z, b_gate, *ps)


def _merge_bwd(z, b_gate, ps, dmerged, name):
    r = z.shape[0]
    tm = ROW_TILE
    row = pl.BlockSpec((tm, D_MODEL), lambda i: (i, 0))
    wide = pl.BlockSpec((tm, 3 * D_MODEL), lambda i: (i, 0))

    def body(zg_ref, b_ref, p0_ref, p1_ref, p2_ref, dm_ref, d0_ref, d1_ref, d2_ref, dz_ref, db_ref):
        @pl.when(pl.program_id(0) == 0)
        def _():
            db_ref[...] = jnp.zeros_like(db_ref)

        gates = _sigmoid(zg_ref[...].astype(F32) + b_ref[...])
        dm = dm_ref[...].astype(F32)
        for t, (p_ref, d_ref) in enumerate(((p0_ref, d0_ref), (p1_ref, d1_ref), (p2_ref, d2_ref))):
            cols = slice(t * D_MODEL, (t + 1) * D_MODEL)
            gt = gates[:, cols]
            d_ref[...] = (dm * gt).astype(BF16)
            dlogit = dm * p_ref[...].astype(F32) * gt * (1.0 - gt)
            dz_ref[:, cols] = dlogit.astype(BF16)
            db_ref[0:1, cols] += jnp.sum(dlogit, axis=0, keepdims=True)

    shp = jax.ShapeDtypeStruct((r, D_MODEL), BF16)
    return pl.pallas_call(
        body, name=name, grid=(r // tm,),
        in_specs=[wide, pl.BlockSpec((1, 3 * D_MODEL), lambda i: (0, 0)), row, row, row, row],
        out_specs=[row, row, row, wide, pl.BlockSpec((8, 3 * D_MODEL), lambda i: (0, 0))],
        out_shape=[shp, shp, shp, jax.ShapeDtypeStruct((r, 3 * D_MODEL), BF16),
                   jax.ShapeDtypeStruct((8, 3 * D_MODEL), F32)],
        compiler_params=_params())(z, b_gate, *ps, dmerged)


def _sum_slots(buf, name):
    s, rows, _ = buf.shape
    tr = _pick(rows, [512, 256, 128, 64, 32, 16, 8])

    def body(b_ref, o_ref):
        acc = b_ref[0]
        for t in range(1, s):
            acc = acc + b_ref[t]
        o_ref[...] = acc

    return pl.pallas_call(
        body, name=name, grid=(rows // tr,),
        in_specs=[pl.BlockSpec((s, tr, LANES), lambda i: (0, i, 0))],
        out_specs=pl.BlockSpec((tr, LANES), lambda i: (i, 0)),
        out_shape=jax.ShapeDtypeStruct((rows, LANES), F32), compiler_params=_params())(buf)


def _adamw(w, gsrc, m, v, name, scale=None):
    rows, cols = w.shape
    s = gsrc.shape[0]
    tr = _pick(rows, [128, 64, 32, 16, 8])
    blk = pl.BlockSpec((tr, cols), lambda i: (i, 0))
    c1 = 1.0 / (1.0 - ADAM_B1 ** ADAM_STEP)
    c2 = 1.0 / (1.0 - ADAM_B2 ** ADAM_STEP)

    def body(*refs):
        if scale is None:
            w_ref, g_ref, m_ref, v_ref, go_ref, d_ref, mo_ref, vo_ref = refs
        else:
            w_ref, g_ref, m_ref, v_ref, sc_ref, go_ref, d_ref, mo_ref, vo_ref = refs
        g = g_ref[0].astype(F32)
        for t in range(1, s):
            g = g + g_ref[t].astype(F32)
        if scale is not None:
            g = g * sc_ref[...]
        mn = ADAM_B1 * m_ref[...] + (1.0 - ADAM_B1) * g
        vn = ADAM_B2 * v_ref[...] + (1.0 - ADAM_B2) * (g * g)
        go_ref[...] = g
        mo_ref[...] = mn
        vo_ref[...] = vn
        d_ref[...] = -ADAM_LR * ((mn * c1) / (jnp.sqrt(vn * c2) + ADAM_EPS) + ADAM_WD * w_ref[...])

    shp = jax.ShapeDtypeStruct((rows, cols), F32)
    ins = [w, gsrc, m, v] + ([] if scale is None else [scale])
    return pl.pallas_call(
        body, name=name, grid=(rows // tr,),
        in_specs=[blk, pl.BlockSpec((s, tr, cols), lambda i: (0, i, 0)), blk, blk] + ([] if scale is None else [blk]),
        out_specs=[blk] * 4, out_shape=[shp] * 4, compiler_params=_params())(*ins)


def _pack(arrs):
    flat = []
    for a in arrs:
        a = a.reshape(-1).astype(F32)
        pad = (-a.shape[0]) % (8 * LANES)
        flat.append(jnp.pad(a, (0, pad)) if pad else a)
    return jnp.concatenate(flat).reshape(-1, LANES)


def _unpack(buf, shapes, lead=()):
    out, row = [], 0
    for shp in shapes:
        n = 1
        for d in shp:
            n *= d
        nrows = -(-n // (8 * LANES)) * 8
        piece = buf[..., row:row + nrows, :].reshape(lead + (nrows * LANES,))[..., :n]
        out.append(piece.reshape(lead + tuple(shp)))
        row += nrows
    return out


def _silu(x):
    return x * jax.nn.sigmoid(x)


def _rope_tables(n_lat, m_ctx):
    pos = jnp.arange(n_lat)
    row = (pos // GRID_W).astype(F32)
    col = (pos % GRID_W).astype(F32)
    half = HEAD_DIM // 2
    inv = ROPE_THETA ** (-jnp.arange(0, half, 2, dtype=F32) / half)
    ang = jnp.concatenate([row[:, None] * inv, col[:, None] * inv], axis=-1)
    cos, sin = jnp.cos(ang), jnp.sin(ang)
    cos2 = jnp.tile(jnp.concatenate([cos, cos], axis=-1), (1, LANES // HEAD_DIM))
    sin2 = jnp.tile(jnp.concatenate([-sin, sin], axis=-1), (1, LANES // HEAD_DIM))
    return (jnp.concatenate([cos2, jnp.ones((m_ctx, LANES), F32)], axis=0),
            jnp.concatenate([sin2, jnp.zeros((m_ctx, LANES), F32)], axis=0))


def _to_slots(full, axis):
    shp = full.shape
    new = shp[:axis] + (N_DEV, shp[axis] // N_DEV) + shp[axis + 1:]
    return jnp.moveaxis(full.reshape(new), axis, 0)


def _from_slots(slots, axis):
    moved = jnp.moveaxis(slots, 0, axis)
    shp = moved.shape
    return moved.reshape(shp[:axis] + (shp[axis] * shp[axis + 1],) + shp[axis + 2:])


def _permute_in(w):
    return jnp.concatenate([w[:, OFF_G:], w[:, :OFF_K], w[:, OFF_A:OFF_B], w[:, OFF_B:OFF_G], w[:, OFF_K:OFF_A]],
                           axis=1)


def kernel(x, c, ctx, c_ctx, w_mod, b_mod, g_mix, w_in, b_gate, sink, w_spatial, b_spatial, g_v, w_sconv, w_branch, w_out, g_ffn, w_up, w_fconv, w_down, g_final, loss_target, m_c_ctx, m_w_mod, m_b_mod, m_g_mix, m_w_in, m_b_gate, m_sink, m_w_spatial, m_b_spatial, m_g_v, m_w_sconv, m_w_branch, m_w_out, m_g_ffn, m_w_up, m_w_fconv, m_w_down, m_g_final, v_c_ctx, v_w_mod, v_b_mod, v_g_mix, v_w_in, v_b_gate, v_sink, v_w_spatial, v_b_spatial, v_g_v, v_w_sconv, v_w_branch, v_w_out, v_g_ffn, v_w_up, v_w_fconv, v_w_down, v_g_final):
    n_lat, m_ctx = x.shape[1], ctx.shape[1]
    r = n_lat + m_ctx
    me = 4 * lax.axis_index("x") + 2 * lax.axis_index("y") + lax.axis_index("c")
    mod_w = w_mod.shape[2]
    bm = _pick(r, [768, 256])

    small_shapes = [c.shape, w_sconv.shape, w_fconv.shape]
    g_in, g_br, g_out, g_up, g_dn, g_small = _exchange(
        [w_in.astype(BF16), w_branch.astype(BF16), w_out.astype(BF16), w_up.astype(BF16), w_down.astype(BF16),
         _pack([c, w_sconv, w_fconv])], ["gather"] * 6, "gather_weights")
    c_all, sconv_all, fconv_all = _unpack(g_small, small_shapes, lead=(N_DEV,))
    c_all = c_all.reshape(N_DEV, D_MODEL)
    w_sconv_full = _from_slots(sconv_all, 2)
    w_fconv_full = _from_slots(fconv_all, 2)
    w_in_p = [_permute_in(_from_slots(g_in[:, l], 1)) for l in range(DEPTH)]
    w_br = [_from_slots(g_br[:, l], 1) for l in range(DEPTH)]
    w_o = [_from_slots(g_out[:, l], 0) for l in range(DEPTH)]
    w_u = [_from_slots(g_up[:, l], 1) for l in range(DEPTH)]
    w_d = [_from_slots(g_dn[:, l], 0) for l in range(DEPTH)]

    act = jnp.concatenate([_silu(c_all), _silu(c_ctx)[None], jnp.zeros((7, D_MODEL), F32)], axis=0)
    mod_part = jnp.stack([_mm(act, w_mod[l], name=f"mod_fwd{l}", bm=16, bn=mod_w, bk=D_MODEL, out_dtype=F32)
                          for l in range(DEPTH)])
    (mod_all,) = _exchange([mod_part], ["gather"], "gather_mod")
    mod_full = _from_slots(mod_all, 2) + b_mod[:, None, :]
    mods = []
    for l in range(DEPTH):
        mine = lax.dynamic_index_in_dim(mod_full[l], me, axis=0, keepdims=False).reshape(6, D_MODEL)
        theirs = mod_full[l, N_DEV].reshape(6, D_MODEL)
        mods.append(jnp.pad(jnp.stack([mine, theirs]), ((0, 0), (0, 2), (0, 0))))

    cos, sin = _rope_tables(n_lat, m_ctx)
    xs = jnp.concatenate([x[0], ctx[0]], axis=0)
    ws_b = w_spatial.astype(BF16)
    ws_t = jnp.swapaxes(w_spatial, 2, 3).astype(BF16)
    vec = lambda a: a.reshape(1, -1)

    saved = []
    res = None
    for l in range(DEPTH):
        s = {}
        if res is None:
            s["x0"] = xs
            s["h"] = _norm_fwd(xs, vec(g_mix[l]), mods[l], n_lat, 0, 1, f"norm_mix{l}")
        else:
            s["x0"], s["h"] = _norm_fwd(xs, vec(g_mix[l]), mods[l], n_lat, 0, 1, f"norm_mix{l}", res=res)
        s["z"] = _mm(s["h"], w_in_p[l], name=f"proj_in{l}", bm=bm, bn=_pick(IN_W, [2432, 512]), bk=D_MODEL,
                     out_dtype=BF16)
        s["qkv"] = _rope_fwd(s["z"], cos, sin, f"rope{l}")
        s["y0"], s["lse"] = _attn_fwd(s["qkv"], sink[l], n_lat, m_ctx, f"attn{l}")
        s["bias"] = jnp.repeat(b_spatial[l].T, LANES, axis=1)
        s["y1"] = _gating_fwd(s["z"], ws_b[l], s["bias"], vec(g_v[l]), f"gating{l}")
        s["y2"] = _sconv_fwd(s["z"], w_sconv_full[l], n_lat, f"sconv{l}")
        s["p"] = [_mm(s[f"y{t}"], w_br[l], b_lead=t, name=f"branch{l}_{t}", bm=bm, bn=D_MODEL, bk=D_MODEL,
                      out_dtype=BF16) for t in range(3)]
        s["merged"] = _merge_fwd(s["z"], vec(b_gate[l]), s["p"], f"merge{l}")
        s["o"] = _mm(s["merged"], w_o[l], name=f"proj_out{l}", bm=bm, bn=D_MODEL, bk=D_MODEL, out_dtype=F32)
        s["x1"], s["h2"] = _norm_fwd(s["x0"], vec(g_ffn[l]), mods[l], n_lat, 3, 4, f"norm_ffn{l}",
                                     res=(s["o"], mods[l], 2))
        s["up"] = _mm(s["h2"], w_u[l], name=f"ffn_up{l}", bm=bm, bn=_pick(2 * D_FF, [1408]), bk=D_MODEL,
                      out_dtype=BF16)
        s["f"] = _ffn_conv_fwd(s["up"], w_fconv_full[l], n_lat, f"ffn_conv{l}")
        s["dd"] = _mm(s["f"], w_d[l], name=f"ffn_down{l}", bm=bm, bn=D_MODEL, bk=D_FF, out_dtype=F32)
        saved.append(s)
        xs, res = s["x1"], (s["dd"], mods[l], 5)

    top = saved[DEPTH - 1]
    dxs, d_dd, acc_final, accs_top = _loss_bwd(top["x1"], top["dd"], mods[DEPTH - 1], vec(g_final), loss_target[0],
                                               n_lat, "loss")
    loss_part = acc_final[1, 0]
    dg_final = acc_final[0]
    dmods = [None] * DEPTH
    gate2 = accs_top[:, 2]
    grads = {k: [None] * DEPTH for k in ("g_mix", "g_ffn", "g_v", "b_gate", "sink", "w_spatial", "b_spatial",
                                         "w_sconv", "w_fconv", "w_in", "w_branch", "w_out", "w_up", "w_down")}
    bk_r = _pick(r, [768, 256])
    for l in reversed(range(DEPTH)):
        s = saved[l]
        df = _mm(d_dd, w_d[l], tb=True, name=f"d_ffn_down{l}", bm=bm, bn=_pick(D_FF, [1408]), bk=D_MODEL,
                 out_dtype=BF16)
        grads["w_down"][l] = _mm(s["f"], d_dd, ta=True, name=f"g_ffn_down{l}", bm=_pick(D_FF, [1408]), bn=D_MODEL,
                                 bk=bk_r, out_dtype=BF16)
        dup, dwf = _ffn_conv_bwd(s["up"], w_fconv_full[l], df, n_lat, f"d_ffn_conv{l}")
        grads["w_fconv"][l] = dwf[:3]
        kb = 1408
        nkb = D_FF // kb
        dh2 = _mm_nt_pieces([(dup, 0, 0, nkb, 0), (dup, 1, 0, nkb, nkb)], w_u[l], name=f"d_ffn_up{l}", bm=bm, bk=kb)
        nbh = D_FF // kb
        grads["w_up"][l] = _mm(
            s["h2"], dup, ta=True, name=f"g_ffn_up{l}", bm=D_MODEL, bn=kb, bk=bk_r, out_dtype=BF16,
            b_spec=(pl.BlockSpec((None, bk_r, kb), lambda i, j, k: (j // nbh, k, j % nbh)), 2 * D_FF))
        dx1, d_o, acc, accs = _norm_bwd(s["x1"], vec(g_ffn[l]), mods[l], dh2, dxs, n_lat, 3, 4, f"d_norm_ffn{l}",
                                        res=(s["o"], mods[l], 2))
        grads["g_ffn"][l] = acc[0]
        shift2, scale2, gate1 = accs[:, 0], accs[:, 1], accs[:, 2]
        dmerged = _mm(d_o, w_o[l], tb=True, name=f"d_proj_out{l}", bm=bm, bn=D_MODEL, bk=D_MODEL, out_dtype=BF16)
        grads["w_out"][l] = _mm(s["merged"], d_o, ta=True, name=f"g_proj_out{l}", bm=D_MODEL, bn=D_MODEL, bk=bk_r,
                                out_dtype=BF16)
        dp0, dp1, dp2, dz_g, dbg = _merge_bwd(s["z"], vec(b_gate[l]), s["p"], dmerged, f"d_merge{l}")
        grads["b_gate"][l] = dbg[0]
        dps = (dp0, dp1, dp2)
        dys = [_mm(dps[t], w_br[l], tb=True, b_lead=t, name=f"d_branch{l}_{t}", bm=bm, bn=D_MODEL, bk=D_MODEL,
                   out_dtype=BF16) for t in range(3)]
        grads["w_branch"][l] = jnp.stack(
            [_mm(s[f"y{t}"], dps[t], ta=True, name=f"g_branch{l}_{t}", bm=D_MODEL, bn=D_MODEL, bk=bk_r,
                 out_dtype=BF16) for t in range(3)])
        dq, dk, dv, dsk = _attn_bwd(s["qkv"], sink[l], s["y0"], dys[0], s["lse"], n_lat, m_ctx, f"d_attn{l}")
        grads["sink"][l] = dsk[:, 0, :8].reshape(N_HEADS)
        dz_qkv = _rope_bwd(dq, dk, dv, cos, sin, f"d_rope{l}")
        dz_a, dws, dbs, dgv = _gating_bwd(s["z"], ws_b[l], ws_t[l], s["bias"], vec(g_v[l]), dys[1], f"d_gating{l}")
        grads["w_spatial"][l], grads["b_spatial"][l], grads["g_v"][l] = dws, dbs.T, dgv[0]
        dz_b, dwsc = _sconv_bwd(s["z"], w_sconv_full[l], dys[2], n_lat, f"d_sconv{l}")
        grads["w_sconv"][l] = dwsc[:3]
        kb = 512
        pieces = [(dz_g, None, 0, 3 * D_MODEL // kb, 0), (dz_qkv, None, 0, D_MODEL // kb, P_Q // kb),
                  (dz_a, None, 0, 2 * D_MODEL // kb, P_A // kb)]
        pieces += [(dz_b, t, 0, D_MODEL // kb, (P_B + t * D_MODEL) // kb) for t in range(3)]
        pieces += [(dz_qkv, None, D_MODEL // kb, 1, P_KV // kb)]
        dh = _mm_nt_pieces(pieces, w_in_p[l], name=f"d_proj_in{l}", bm=bm, bk=kb)
        gw = lambda b, nm, bn, **kw: _mm(s["h"], b, ta=True, name=f"g_proj_in{l}_{nm}", bm=D_MODEL, bn=bn, bk=bk_r,
                                         out_dtype=BF16, **kw)
        gw_g = gw(dz_g, "gate", 1024)
        gw_qkv = gw(dz_qkv, "qkv", 512)
        gw_a = gw(dz_a, "gating", 1024)
        gw_b = [gw(dz_b, f"sconv{t}", 1024, b_lead=t) for t in range(3)]
        grads["w_in"][l] = jnp.concatenate([gw_qkv, gw_a] + gw_b + [gw_g], axis=1)
        below = None if l == 0 else (saved[l - 1]["dd"], mods[l - 1], 5)
        outs = _norm_bwd(s["x0"], vec(g_mix[l]), mods[l], dh, dx1, n_lat, 0, 1, f"d_norm_mix{l}", res=below)
        if below is None:
            dxs, acc, accs = outs
        else:
            dxs, d_dd, acc, accs = outs
        grads["g_mix"][l] = acc[0]
        dmods[l] = jnp.stack([accs[:, 0], accs[:, 1], gate1, shift2, scale2, gate2], axis=1)
        gate2 = accs[:, 2]

    dmod_own = jnp.stack([dmods[l][0].reshape(-1) for l in range(DEPTH)])
    dmod_ctx = jnp.stack([dmods[l][1].reshape(-1) for l in range(DEPTH)])
    stk = lambda k: jnp.stack(grads[k])
    small = [loss_part.reshape(1), dg_final, stk("g_mix"), stk("g_ffn"), stk("g_v"), stk("b_gate"), stk("sink"),
             stk("w_spatial"), stk("b_spatial"), stk("w_sconv"), stk("w_fconv"), dmod_own + dmod_ctx, dmod_ctx,
             dmod_own]
    small_shapes = [a.shape for a in small]
    (small_all,) = _exchange([_pack(small)], ["gather"], "gather_small_grads")
    sums = _unpack(_sum_slots(small_all, "sum_small_grads"), small_shapes)
    (loss_sum, g_final_g, g_mix_g, g_ffn_g, g_v_g, b_gate_g, sink_g, w_spatial_g, b_spatial_g, sconv_g, fconv_g,
     b_mod_g, dmodc_tot, _) = sums
    dmod_all = _unpack(small_all, small_shapes, lead=(N_DEV,))[-1]
    w_sconv_g = lax.dynamic_slice_in_dim(sconv_g, me * w_sconv.shape[2], w_sconv.shape[2], axis=2)
    w_fconv_g = lax.dynamic_slice_in_dim(fconv_g, me * w_fconv.shape[2], w_fconv.shape[2], axis=2)

    dmod_cols = lax.dynamic_slice_in_dim(dmod_all, me * mod_w, mod_w, axis=2)
    dmodc_cols = lax.dynamic_slice_in_dim(dmodc_tot, me * mod_w, mod_w, axis=1)
    g_w_mod, cctx_part = [], None
    for l in range(DEPTH):
        rhs = jnp.concatenate([dmod_cols[:, l], dmodc_cols[l][None], jnp.zeros((7, mod_w), F32)], axis=0)
        g_w_mod.append(_mm(act, rhs, ta=True, name=f"g_mod{l}", bm=D_MODEL, bn=mod_w, bk=16, out_dtype=F32))
        lhs = jnp.pad(dmodc_cols[l][None], ((0, 7), (0, 0)))
        part = _mm(lhs, w_mod[l], tb=True, name=f"d_cctx{l}", bm=8, bn=D_MODEL, bk=mod_w, out_dtype=F32)
        cctx_part = part if cctx_part is None else cctx_part + part
    sg = jax.nn.sigmoid(c_ctx)
    dsilu = (sg * (1.0 + c_ctx * (1.0 - sg))).reshape(8, LANES)

    r_in, r_br, r_out, r_up, r_dn, r_cctx = _exchange(
        [_to_slots(stk("w_in"), 2), _to_slots(stk("w_branch"), 2), _to_slots(stk("w_out"), 1),
         _to_slots(stk("w_up"), 2), _to_slots(stk("w_down"), 1), cctx_part[0].reshape(8, LANES)],
        ["a2a"] * 5 + ["gather"], "exchange_weight_grads")

    def flat(a):
        return a.reshape(-1, a.shape[-1])

    def flat_slots(a):
        return a.reshape(a.shape[0], -1, a.shape[-1])

    upd = {}
    for nm, wv, gsrc, mv, vv in (("w_in", w_in, r_in, m_w_in, v_w_in), ("w_branch", w_branch, r_br, m_w_branch, v_w_branch),
                                 ("w_out", w_out, r_out, m_w_out, v_w_out), ("w_up", w_up, r_up, m_w_up, v_w_up),
                                 ("w_down", w_down, r_dn, m_w_down, v_w_down),
                                 ("w_mod", w_mod, jnp.stack(g_w_mod)[None], m_w_mod, v_w_mod)):
        outs = _adamw(flat(wv), flat_slots(gsrc), flat(mv), flat(vv), f"adamw_{nm}")
        upd[nm] = [o.reshape(wv.shape) for o in outs]
    upd["c_ctx"] = [o.reshape(D_MODEL) for o in _adamw(
        c_ctx.reshape(8, LANES), r_cctx, m_c_ctx.reshape(8, LANES), v_c_ctx.reshape(8, LANES), "adamw_c_ctx",
        scale=dsilu)]

    names = ["b_mod", "g_mix", "b_gate", "sink", "w_spatial", "b_spatial", "g_v", "w_sconv", "g_ffn", "w_fconv",
             "g_final"]
    w_s = [b_mod, g_mix, b_gate, sink, w_spatial, b_spatial, g_v, w_sconv, g_ffn, w_fconv, g_final]
    g_s = [b_mod_g, g_mix_g, b_gate_g, sink_g, w_spatial_g, b_spatial_g, g_v_g, w_sconv_g, g_ffn_g, w_fconv_g,
           g_final_g]
    m_s = [m_b_mod, m_g_mix, m_b_gate, m_sink, m_w_spatial, m_b_spatial, m_g_v, m_w_sconv, m_g_ffn, m_w_fconv,
           m_g_final]
    v_s = [v_b_mod, v_g_mix, v_b_gate, v_sink, v_w_spatial, v_b_spatial, v_g_v, v_w_sconv, v_g_ffn, v_w_fconv,
           v_g_final]
    shapes = [a.shape for a in w_s]
    packed = _adamw(_pack(w_s), _pack(g_s)[None], _pack(m_s), _pack(v_s), "adamw_small")
    unpacked = [_unpack(o, shapes) for o in packed]
    for t, nm in enumerate(names):
        upd[nm] = [unpacked[q][t] for q in range(4)]

    order = ["c_ctx", "w_mod", "b_mod", "g_mix", "w_in", "b_gate", "sink", "w_spatial", "b_spatial", "g_v", "w_sconv",
             "w_branch", "w_out", "g_ffn", "w_up", "w_fconv", "w_down", "g_final"]
    result = [loss_sum.reshape(()), dxs[:n_lat][None]]
    for q in range(4):
        result += [upd[nm][q] for nm in order]
    return tuple(result)
```

```python
import jax
import jax.numpy as jnp
from jax import lax
from jax.experimental import pallas as pl
from jax.experimental.pallas import tpu as pltpu

F32, BF16 = jnp.float32, jnp.bfloat16

D_MODEL = 1024
DEPTH = 2
GRID_W = 64
N_HEADS = 16
N_KV_HEADS = 4
HEAD_DIM = 64
WINDOW = 128
BLK = 128
ROPE_THETA = 10000.0
CHUNK = 128
A_GROUPS = 8
D_FF = 2816
EPS = 1e-6
NEG = -1e30
IN_W = 9728
OFF_K, OFF_A, OFF_B, OFF_G = 1024, 1536, 3584, 6656
P_Q, P_A, P_B, P_KV = 3072, 4096, 6144, 9216

N_DEV = 8
LANES = 128
ROW_TILE = 256
CONV_CHUNK = 256
PAD = 8
VMEM_LIMIT = 52 * 1024 * 1024

ADAM_LR, ADAM_B1, ADAM_B2, ADAM_EPS, ADAM_WD, ADAM_STEP = 0.001, 0.9, 0.999, 1e-08, 0.01, 10

HBM_SPEC = pl.BlockSpec(memory_space=pltpu.HBM)
SMEM_SPEC = pl.BlockSpec(memory_space=pltpu.SMEM)


def _params():
    return pltpu.CompilerParams(vmem_limit_bytes=VMEM_LIMIT)


def _pick(n, prefs):
    for p in prefs:
        if n % p == 0:
            return p
    raise ValueError((n, prefs))


def _sigmoid(x):
    return 1.0 / (1.0 + jnp.exp(-x))


def _mm(a, b, *, name, ta=False, tb=False, bm, bn, bk, out_dtype, a_lead=None, b_lead=None, b_spec=None,
        carry=None):
    ash = a.shape[1:] if a_lead is not None else a.shape
    bsh = b.shape[1:] if b_lead is not None else b.shape
    kc, mo = (ash[0], ash[1]) if ta else (ash[1], ash[0])
    no = (bsh[0] if tb else bsh[1]) if b_spec is None else b_spec[1]
    assert mo % bm == 0 and no % bn == 0 and kc % bk == 0, (name, mo, no, kc, bm, bn, bk)
    nk = kc // bk

    def lead(shape, fn, idx):
        if idx is None:
            return pl.BlockSpec(shape, fn)
        return pl.BlockSpec((None,) + shape, lambda i, j, k: (idx,) + fn(i, j, k))

    a_spec = lead((bk, bm), lambda i, j, k: (k, i), a_lead) if ta else lead((bm, bk), lambda i, j, k: (i, k), a_lead)
    if b_spec is not None:
        b_bs = b_spec[0]
    elif tb:
        b_bs = lead((bn, bk), lambda i, j, k: (j, k), b_lead)
    else:
        b_bs = lead((bk, bn), lambda i, j, k: (k, j), b_lead)
    dims = (((0 if ta else 1,), (1 if tb else 0,)), ((), ()))

    def body(a_ref, b_ref, o_ref, *scratch):
        part = lax.dot_general(a_ref[...], b_ref[...], dims, preferred_element_type=F32)
        if nk == 1:
            o_ref[...] = part.astype(o_ref.dtype)
        else:
            acc = scratch[0]
            k = pl.program_id(2)

            @pl.when(k == 0)
            def _():
                acc[...] = part

            @pl.when(k > 0)
            def _():
                acc[...] += part

            @pl.when(k == nk - 1)
            def _():
                o_ref[...] = acc[...].astype(o_ref.dtype)

    outs, carried = _pcall(
        body, (a, b), name=name, grid=(mo // bm, no // bn, nk), in_specs=[a_spec, b_bs],
        out_specs=[pl.BlockSpec((bm, bn), lambda i, j, k: (i, j))],
        out_shape=[jax.ShapeDtypeStruct((mo, no), out_dtype)],
        scratch_shapes=[pltpu.VMEM((bm, bn), F32)] if nk > 1 else [], carry=carry)
    return outs[0] if carry is None else (outs[0], carried)


def _mm_nt_pieces(pieces, w, *, name, bm, bk, carry=None):
    kout = w.shape[0]
    starts, total = [], 0
    for (_, _, _, nblk, _) in pieces:
        starts.append(total)
        total += nblk
    mo = pieces[0][0].shape[-2]
    assert mo % bm == 0

    def a_spec(p):
        arr, lead, col0, nblk, _ = pieces[p]
        s = starts[p]

        def fn(i, k):
            kk = jnp.clip(k - s, 0, nblk - 1) + col0
            return (i, kk) if lead is None else (lead, i, kk)
        shape = (bm, bk) if lead is None else (None, bm, bk)
        return pl.BlockSpec(shape, fn)

    def w_map(i, k):
        col = 0
        for p, (_, _, _, nblk, wcol0) in enumerate(pieces):
            inside = (k >= starts[p]) & (k < starts[p] + nblk)
            col = col + jnp.where(inside, wcol0 + k - starts[p], 0)
        return (0, col)

    n_p = len(pieces)
    dims = (((1,), (1,)), ((), ()))

    def body(*refs):
        a_refs, w_ref, o_ref, acc = refs[:n_p], refs[n_p], refs[n_p + 1], refs[n_p + 2]
        k = pl.program_id(1)

        @pl.when(k == 0)
        def _():
            acc[...] = jnp.zeros_like(acc)

        for p in range(n_p):
            nblk = pieces[p][3]

            @pl.when((k >= starts[p]) & (k < starts[p] + nblk))
            def _(p=p):
                acc[...] += lax.dot_general(a_refs[p][...], w_ref[...], dims, preferred_element_type=F32)

        @pl.when(k == total - 1)
        def _():
            o_ref[...] = acc[...]

    outs, carried = _pcall(
        body, [p[0] for p in pieces] + [w], name=name, grid=(mo // bm, total),
        in_specs=[a_spec(p) for p in range(n_p)] + [pl.BlockSpec((kout, bk), w_map)],
        out_specs=[pl.BlockSpec((bm, kout), lambda i, k: (i, 0))],
        out_shape=[jax.ShapeDtypeStruct((mo, kout), F32)],
        scratch_shapes=[pltpu.VMEM((bm, kout), F32)], carry=carry)
    return outs[0] if carry is None else (outs[0], carried)


def _xchg_out_shapes(arrs, modes):
    return [jax.ShapeDtypeStruct((N_DEV,) + a.shape if m == "gather" else a.shape, a.dtype)
            for a, m in zip(arrs, modes)]


def _xchg_sems(n):
    return [pltpu.SemaphoreType.DMA((n, N_DEV - 1)), pltpu.SemaphoreType.DMA((n, N_DEV - 1)),
            pltpu.SemaphoreType.DMA((n,))]


def _xchg_copies(ins, outs, modes, sems, arrivals=True):
    send_sems, recv_sems, local_sems = sems
    n = len(ins)
    x, y, c = lax.axis_index("x"), lax.axis_index("y"), lax.axis_index("c")
    me = 4 * x + 2 * y + c

    def src(t, slot):
        return ins[t] if modes[t] == "gather" else ins[t].at[slot]

    local = [pltpu.make_async_copy(src(t, me), outs[t].at[me], local_sems.at[t]) for t in range(n)]
    sends, recvs = [], []
    for j in range(1, N_DEV):
        px = 1 - x if (j >> 2) & 1 else x
        py = 1 - y if (j >> 1) & 1 else y
        pc = 1 - c if j & 1 else c
        peer = 4 * px + 2 * py + pc
        for t in range(n):
            common = dict(src_ref=src(t, peer), send_sem=send_sems.at[t, j - 1], recv_sem=recv_sems.at[t, j - 1],
                          device_id=(px, py, pc), device_id_type=pl.DeviceIdType.MESH)
            sends.append(pltpu.make_async_remote_copy(dst_ref=outs[t].at[me], **common))
            if arrivals:
                recvs.append(pltpu.make_async_remote_copy(dst_ref=outs[t].at[peer], **common))
    return local, sends, recvs


def _xchg_start(copies):
    local, sends, _ = copies
    for cp in local + sends:
        cp.start()


def _xchg_wait(copies):
    local, sends, recvs = copies
    for cp in recvs:
        cp.wait_recv()
    for cp in sends:
        cp.wait_send()
    for cp in local:
        cp.wait()


def _exchange(arrs, modes, name):
    n = len(arrs)

    def body(*refs):
        copies = _xchg_copies(refs[:n], refs[n:2 * n], modes, refs[2 * n:])
        _xchg_start(copies)
        _xchg_wait(copies)

    outs = pl.pallas_call(
        body, name=name, in_specs=[HBM_SPEC] * n, out_specs=[HBM_SPEC] * n, out_shape=_xchg_out_shapes(arrs, modes),
        scratch_shapes=_xchg_sems(n), compiler_params=pltpu.CompilerParams(has_side_effects=True),
    )(*arrs)
    return list(outs)


def _pcall(body, operands, *, name, grid, in_specs, out_specs, out_shape, scratch_shapes=(), carry=None):
    out_specs, out_shape, scratch_shapes = list(out_specs), list(out_shape), list(scratch_shapes)
    if carry is None:
        outs = pl.pallas_call(body, name=name, grid=grid, in_specs=in_specs, out_specs=out_specs,
                              out_shape=out_shape, scratch_shapes=scratch_shapes, compiler_params=_params())(*operands)
        return list(outs), []
    arrs, modes = carry
    n, n_in, n_out, n_scr = len(arrs), len(in_specs), len(out_specs), len(scratch_shapes)

    def wrapped(*refs):
        ins, c_in = refs[:n_in], refs[n_in:n_in + n]
        outs, c_out = refs[n_in + n:n_in + n + n_out], refs[n_in + n + n_out:n_in + 2 * n + n_out]
        rest = refs[n_in + 2 * n + n_out:]
        scr, sems = rest[:n_scr], rest[n_scr:]
        first, last = None, None
        for d, size in enumerate(grid):
            f, e = pl.program_id(d) == 0, pl.program_id(d) == size - 1
            first = f if first is None else first & f
            last = e if last is None else last & e

        @pl.when(first)
        def _():
            _xchg_start(_xchg_copies(c_in, c_out, modes, sems, arrivals=False))

        body(*ins, *outs, *scr)

        @pl.when(last)
        def _():
            _xchg_wait(_xchg_copies(c_in, c_out, modes, sems))

    outs = pl.pallas_call(
        wrapped, name=name, grid=grid, in_specs=list(in_specs) + [HBM_SPEC] * n,
        out_specs=out_specs + [HBM_SPEC] * n, out_shape=out_shape + _xchg_out_shapes(arrs, modes),
        scratch_shapes=scratch_shapes + _xchg_sems(n), compiler_params=_params())(*operands, *arrs)
    return list(outs[:n_out]), list(outs[n_out:])


def _row_specs(r, n_lat, tm):
    nbl = n_lat // tm
    row = pl.BlockSpec((tm, D_MODEL), lambda i: (i, 0))
    mod = pl.BlockSpec((None, 8, D_MODEL), lambda i: (i // nbl, 0, 0))
    vec = pl.BlockSpec((1, D_MODEL), lambda i: (0, 0))
    return nbl, row, mod, vec


def _rows(vals):
    width = [v for v in vals if v is not None][0].shape[1]
    return jnp.concatenate([jnp.zeros((1, width), F32) if v is None else v for v in vals], axis=0)


def _norm_fwd(xs, g, mods, n_lat, sh, sc, name, res=None):
    r = xs.shape[0]
    tm = ROW_TILE
    nbl, row, mod, vec = _row_specs(r, n_lat, tm)

    def norm(x, g_ref, m_ref, h_ref):
        rs = lax.rsqrt(jnp.mean(x * x, axis=-1, keepdims=True) + EPS)
        m = m_ref[...]
        h_ref[...] = ((x * rs * g_ref[...]) * (1.0 + m[sc:sc + 1]) + m[sh:sh + 1]).astype(BF16)

    if res is None:
        def body(x_ref, g_ref, m_ref, h_ref):
            norm(x_ref[...], g_ref, m_ref, h_ref)
        return pl.pallas_call(
            body, name=name, grid=(r // tm,), in_specs=[row, vec, mod], out_specs=row,
            out_shape=jax.ShapeDtypeStruct((r, D_MODEL), BF16), compiler_params=_params())(xs, g, mods)

    o, mods_res, gt = res

    def body(x_ref, o_ref, mr_ref, g_ref, m_ref, x1_ref, h_ref):
        x = x_ref[...] + mr_ref[...][gt:gt + 1] * o_ref[...]
        x1_ref[...] = x
        norm(x, g_ref, m_ref, h_ref)

    return pl.pallas_call(
        body, name=name, grid=(r // tm,), in_specs=[row, row, mod, vec, mod], out_specs=[row, row],
        out_shape=[jax.ShapeDtypeStruct((r, D_MODEL), F32), jax.ShapeDtypeStruct((r, D_MODEL), BF16)],
        compiler_params=_params())(xs, o, mods_res, g, mods)


def _rms_bwd(x, g, dy):
    rs = lax.rsqrt(jnp.mean(x * x, axis=-1, keepdims=True) + EPS)
    xh = x * rs
    dxh = dy * g
    dx = rs * (dxh - xh * jnp.mean(dxh * xh, axis=-1, keepdims=True))
    return dx, dy * xh, xh


def _acc_specs(nbl):
    acc_all = pl.BlockSpec((8, D_MODEL), lambda i: (0, 0))
    acc_stream = pl.BlockSpec((None, 8, D_MODEL), lambda i: (i // nbl, 0, 0))
    return acc_all, acc_stream


def _loss_bwd(xs, o, mods, g_final, target, n_lat, name):
    r = xs.shape[0]
    tm = ROW_TILE
    nbl, row, mod, vec = _row_specs(r, n_lat, tm)
    acc_all, acc_stream = _acc_specs(nbl)
    tgt = pl.BlockSpec((tm, D_MODEL), lambda i: (jnp.minimum(i, nbl - 1), 0))

    def body(x_ref, o_ref, m_ref, g_ref, t_ref, dx_ref, do_ref, acc_ref, accs_ref):
        i = pl.program_id(0)
        lat = i < nbl
        gate = m_ref[...][5:6]
        o_val = o_ref[...]
        x = x_ref[...] + gate * o_val
        g = g_ref[...]
        rs = lax.rsqrt(jnp.mean(x * x, axis=-1, keepdims=True) + EPS)
        y = x * rs * g
        err = jnp.where(lat, y - t_ref[...], 0.0)
        loss = 0.5 * jnp.sum(jnp.mean(err * err, axis=-1, keepdims=True), axis=0, keepdims=True)
        dy = err * (1.0 / D_MODEL)
        dx, dg_rows, _ = _rms_bwd(x, g, dy)
        dx_ref[...] = dx
        do_ref[...] = (gate * dx).astype(BF16)

        @pl.when(i == 0)
        def _():
            acc_ref[...] = jnp.zeros_like(acc_ref)

        @pl.when((i == 0) | (i == nbl))
        def _():
            accs_ref[...] = jnp.zeros_like(accs_ref)

        acc_ref[...] += _rows([jnp.sum(dg_rows, axis=0, keepdims=True), jnp.broadcast_to(loss, (1, D_MODEL))]
                              + [None] * 6)
        accs_ref[...] += _rows([None, None, jnp.sum(dx * o_val, axis=0, keepdims=True)] + [None] * 5)

    return pl.pallas_call(
        body, name=name, grid=(r // tm,), in_specs=[row, row, mod, vec, tgt],
        out_specs=[row, row, acc_all, acc_stream],
        out_shape=[jax.ShapeDtypeStruct((r, D_MODEL), F32), jax.ShapeDtypeStruct((r, D_MODEL), BF16),
                   jax.ShapeDtypeStruct((8, D_MODEL), F32), jax.ShapeDtypeStruct((2, 8, D_MODEL), F32)],
        compiler_params=_params())(xs, o, mods, g_final, target)


def _norm_bwd(xs, g, mods, dh, dx_in, n_lat, sh, sc, name, res=None):
    r = xs.shape[0]
    tm = ROW_TILE
    nbl, row, mod, vec = _row_specs(r, n_lat, tm)
    acc_all, acc_stream = _acc_specs(nbl)
    has_res = res is not None

    def body(*refs):
        if has_res:
            x_ref, g_ref, m_ref, dh_ref, dxi_ref, o_ref, mr_ref, dx_ref, do_ref, acc_ref, accs_ref = refs
        else:
            x_ref, g_ref, m_ref, dh_ref, dxi_ref, dx_ref, acc_ref, accs_ref = refs
        i = pl.program_id(0)
        x, g, m, dhv = x_ref[...], g_ref[...], m_ref[...], dh_ref[...]
        dy = dhv * (1.0 + m[sc:sc + 1])
        dxn, dg_rows, xh = _rms_bwd(x, g, dy)
        dx = dxi_ref[...] + dxn
        dx_ref[...] = dx
        d_gate = None
        if has_res:
            o_val = o_ref[...]
            do_ref[...] = (mr_ref[...][res[2]:res[2] + 1] * dx).astype(BF16)
            d_gate = jnp.sum(dx * o_val, axis=0, keepdims=True)

        @pl.when(i == 0)
        def _():
            acc_ref[...] = jnp.zeros_like(acc_ref)

        @pl.when((i == 0) | (i == nbl))
        def _():
            accs_ref[...] = jnp.zeros_like(accs_ref)

        acc_ref[...] += _rows([jnp.sum(dg_rows, axis=0, keepdims=True)] + [None] * 7)
        accs_ref[...] += _rows([jnp.sum(dhv, axis=0, keepdims=True),
                                jnp.sum(dhv * (xh * g), axis=0, keepdims=True), d_gate] + [None] * 5)

    ins = [xs, g, mods, dh, dx_in]
    in_specs = [row, vec, mod, row, row]
    out_specs = [row]
    out_shape = [jax.ShapeDtypeStruct((r, D_MODEL), F32)]
    if has_res:
        ins += [res[0], res[1]]
        in_specs += [row, mod]
        out_specs.append(row)
        out_shape.append(jax.ShapeDtypeStruct((r, D_MODEL), BF16))
    out_specs += [acc_all, acc_stream]
    out_shape += [jax.ShapeDtypeStruct((8, D_MODEL), F32), jax.ShapeDtypeStruct((2, 8, D_MODEL), F32)]
    return pl.pallas_call(body, name=name, grid=(r // tm,), in_specs=in_specs, out_specs=out_specs,
                          out_shape=out_shape, compiler_params=_params())(*ins)


def _rotate(t, cos, sin):
    width = t.shape[1]
    reps = width // LANES
    lane = lax.broadcasted_iota(jnp.int32, (1, width), 1)
    first = (lane % HEAD_DIM) < (HEAD_DIM // 2)
    swapped = jnp.where(first, pltpu.roll(t, width - HEAD_DIM // 2, 1), pltpu.roll(t, HEAD_DIM // 2, 1))
    return t * jnp.tile(cos, (1, reps)) + swapped * jnp.tile(sin, (1, reps))


def _rope_fwd(z, cos, sin, name):
    r = z.shape[0]
    tm = ROW_TILE
    kvw = 2 * N_KV_HEADS * HEAD_DIM
    tab = pl.BlockSpec((tm, LANES), lambda i: (i, 0))

    def body(q_ref, kv_ref, c_ref, s_ref, o_ref):
        c, s = c_ref[...], s_ref[...]
        kv = kv_ref[...]
        o_ref[:, :D_MODEL] = _rotate(q_ref[...].astype(F32), c, s).astype(BF16)
        o_ref[:, D_MODEL:D_MODEL + kvw // 2] = _rotate(kv[:, :kvw // 2].astype(F32), c, s).astype(BF16)
        o_ref[:, D_MODEL + kvw // 2:] = kv[:, kvw // 2:]

    return pl.pallas_call(
        body, name=name, grid=(r // tm,),
        in_specs=[pl.BlockSpec((tm, D_MODEL), lambda i: (i, P_Q // D_MODEL)),
                  pl.BlockSpec((tm, kvw), lambda i: (i, P_KV // kvw)), tab, tab],
        out_specs=pl.BlockSpec((tm, D_MODEL + kvw), lambda i: (i, 0)),
        out_shape=jax.ShapeDtypeStruct((r, D_MODEL + kvw), BF16), compiler_params=_params())(z, z, cos, sin)


def _rope_bwd(dq, dk, dv, cos, sin, name):
    r = dq.shape[0]
    tm = ROW_TILE
    kw = N_KV_HEADS * HEAD_DIM
    tab = pl.BlockSpec((tm, LANES), lambda i: (i, 0))

    def body(dq_ref, dk_ref, dv_ref, c_ref, s_ref, o_ref):
        c, s = c_ref[...], -s_ref[...]
        o_ref[:, :D_MODEL] = _rotate(dq_ref[...], c, s).astype(BF16)
        o_ref[:, D_MODEL:D_MODEL + kw] = _rotate(dk_ref[...], c, s).astype(BF16)
        o_ref[:, D_MODEL + kw:] = dv_ref[...].astype(BF16)

    return pl.pallas_call(
        body, name=name, grid=(r // tm,),
        in_specs=[pl.BlockSpec((tm, D_MODEL), lambda i: (i, 0)), pl.BlockSpec((tm, kw), lambda i: (i, 0)),
                  pl.BlockSpec((tm, kw), lambda i: (i, 0)), tab, tab],
        out_specs=pl.BlockSpec((tm, D_MODEL + 2 * kw), lambda i: (i, 0)),
        out_shape=jax.ShapeDtypeStruct((r, D_MODEL + 2 * kw), BF16), compiler_params=_params())(dq, dk, dv, cos, sin)


def _attn_setup(i, n_lat, m_ctx, nbl, k_ref, v_ref):
    start = pl.multiple_of(jnp.clip((i - 1) * BLK, 0, n_lat - 3 * BLK), BLK)
    nkeys = 3 * BLK + m_ctx
    rows = lax.broadcasted_iota(jnp.int32, (4 * BLK, nkeys), 0)
    cols = lax.broadcasted_iota(jnp.int32, (4 * BLK, nkeys), 1)
    qpos = i * BLK + (rows & (BLK - 1))
    mask = (cols >= 3 * BLK) | ((jnp.abs(start + cols - qpos) <= WINDOW) & (i < nbl))
    kblk = jnp.concatenate([k_ref[pl.ds(start, 3 * BLK), :], k_ref[pl.ds(n_lat, m_ctx), :]], axis=0)
    vblk = jnp.concatenate([v_ref[pl.ds(start, 3 * BLK), :], v_ref[pl.ds(n_lat, m_ctx), :]], axis=0)
    lo = lax.broadcasted_iota(jnp.int32, (1, LANES), 1) < HEAD_DIM
    return start, mask, kblk, vblk, lo


def _stack_heads(ref, kh, lo):
    a = ref[:, (2 * kh) * LANES:(2 * kh + 1) * LANES]
    b = ref[:, (2 * kh + 1) * LANES:(2 * kh + 2) * LANES]
    z = jnp.zeros_like(a)
    return jnp.concatenate([jnp.where(lo, a, z), jnp.where(lo, z, a), jnp.where(lo, b, z), jnp.where(lo, z, b)],
                           axis=0)


def _kv_variants(blk, rolled, kh, lo):
    z = jnp.zeros_like(blk)
    if kh == 0:
        return jnp.where(lo, blk, rolled), jnp.where(lo, blk, z), jnp.where(lo, z, rolled)
    return jnp.where(lo, rolled, blk), jnp.where(lo, rolled, z), jnp.where(lo, z, blk)


NT = (((1,), (1,)), ((), ()))
TN = (((0,), (0,)), ((), ()))


def _scores(qst, k2, mask, sink_ref, p, kh):
    s = lax.dot_general(qst, k2, NT, preferred_element_type=F32) * (HEAD_DIM ** -0.5)
    s = jnp.where(mask, s, NEG)
    snk = jnp.concatenate([jnp.full((BLK, 1), sink_ref[p * 8 + kh * 4 + g], F32) for g in range(4)], axis=0)
    return s, snk


def _attn_fwd(qkv, sink, n_lat, m_ctx, name, carry=None):
    r = qkv.shape[0]
    nb, nbl = r // BLK, n_lat // BLK
    kcol = D_MODEL // LANES

    def body(sink_ref, q_ref, k_ref, v_ref, o_ref, lse_ref):
        p, i = pl.program_id(0), pl.program_id(1)
        _, mask, kblk, vblk, lo = _attn_setup(i, n_lat, m_ctx, nbl, k_ref, v_ref)
        kr, vr = pltpu.roll(kblk, HEAD_DIM, 1), pltpu.roll(vblk, HEAD_DIM, 1)
        for kh in range(2):
            k2, _, _ = _kv_variants(kblk, kr, kh, lo)
            _, vlo, vhi = _kv_variants(vblk, vr, kh, lo)
            qst = _stack_heads(q_ref, kh, lo)
            s, snk = _scores(qst, k2, mask, sink_ref, p, kh)
            mx = jnp.maximum(jnp.max(s, axis=-1, keepdims=True), snk)
            pe = jnp.exp(s - mx)
            den = jnp.sum(pe, axis=-1, keepdims=True) + jnp.exp(snk - mx)
            inv = 1.0 / den
            pb = pe.astype(BF16)
            for jp in range(2):
                r0 = 2 * jp * BLK
                pair = (jnp.dot(pb[r0:r0 + BLK], vlo, preferred_element_type=F32)
                        + jnp.dot(pb[r0 + BLK:r0 + 2 * BLK], vhi, preferred_element_type=F32))
                pair = pair * jnp.where(lo, inv[r0:r0 + BLK], inv[r0 + BLK:r0 + 2 * BLK])
                o_ref[:, (2 * kh + jp) * LANES:(2 * kh + jp + 1) * LANES] = pair.astype(BF16)
            lse = mx + jnp.log(den)
            for g in range(4):
                lse_ref[:, kh * 4 + g:kh * 4 + g + 1] = lse[g * BLK:(g + 1) * BLK]

    outs, carried = _pcall(
        body, (sink, qkv, qkv, qkv), name=name, grid=(2, nb),
        in_specs=[SMEM_SPEC,
                  pl.BlockSpec((BLK, 4 * LANES), lambda p, i: (i, p)),
                  pl.BlockSpec((r, LANES), lambda p, i: (0, kcol + p)),
                  pl.BlockSpec((r, LANES), lambda p, i: (0, kcol + 2 + p))],
        out_specs=[pl.BlockSpec((BLK, 4 * LANES), lambda p, i: (i, p)),
                   pl.BlockSpec((None, BLK, 8), lambda p, i: (p, i, 0))],
        out_shape=[jax.ShapeDtypeStruct((r, D_MODEL), BF16), jax.ShapeDtypeStruct((2, r, 8), F32)], carry=carry)
    return outs if carry is None else (outs, carried)


def _attn_bwd(qkv, sink, o, do, lse, n_lat, m_ctx, name, carry=None):
    r = qkv.shape[0]
    nb, nbl = r // BLK, n_lat // BLK
    kcol = D_MODEL // LANES

    def body(sink_ref, q_ref, k_ref, v_ref, o_ref, do_ref, lse_ref, dq_ref, dk_ref, dv_ref, ds_ref):
        p, i = pl.program_id(0), pl.program_id(1)

        @pl.when(i == 0)
        def _():
            dk_ref[...] = jnp.zeros_like(dk_ref)
            dv_ref[...] = jnp.zeros_like(dv_ref)
            ds_ref[...] = jnp.zeros_like(ds_ref)

        start, mask, kblk, vblk, lo = _attn_setup(i, n_lat, m_ctx, nbl, k_ref, v_ref)
        kr, vr = pltpu.roll(kblk, HEAD_DIM, 1), pltpu.roll(vblk, HEAD_DIM, 1)
        lane = lax.broadcasted_iota(jnp.int32, (1, LANES), 1)
        dks, dvs = [], []
        for kh in range(2):
            k2, klo, khi = _kv_variants(kblk, kr, kh, lo)
            v2, _, _ = _kv_variants(vblk, vr, kh, lo)
            qst = _stack_heads(q_ref, kh, lo)
            dost = _stack_heads(do_ref, kh, lo)
            s, snk = _scores(qst, k2, mask, sink_ref, p, kh)
            lse4 = jnp.concatenate([lse_ref[:, kh * 4 + g:kh * 4 + g + 1] for g in range(4)], axis=0)
            pe = jnp.exp(s - lse4)
            dp = lax.dot_general(dost, v2, NT, preferred_element_type=F32)
            deltas = []
            for jp in range(2):
                cols = slice((2 * kh + jp) * LANES, (2 * kh + jp + 1) * LANES)
                prod = do_ref[:, cols].astype(F32) * o_ref[:, cols].astype(F32)
                deltas.append(jnp.sum(jnp.where(lo, prod, 0.0), axis=-1, keepdims=True))
                deltas.append(jnp.sum(jnp.where(lo, 0.0, prod), axis=-1, keepdims=True))
            delta = jnp.concatenate(deltas, axis=0)
            dsc = pe * (dp - delta) * (HEAD_DIM ** -0.5)
            dsb, pb = dsc.astype(BF16), pe.astype(BF16)
            for jp in range(2):
                r0 = 2 * jp * BLK
                dq_ref[:, (2 * kh + jp) * LANES:(2 * kh + jp + 1) * LANES] = (
                    jnp.dot(dsb[r0:r0 + BLK], klo, preferred_element_type=F32)
                    + jnp.dot(dsb[r0 + BLK:r0 + 2 * BLK], khi, preferred_element_type=F32))
            dkf = lax.dot_general(dsb, qst, TN, preferred_element_type=F32)
            dvf = lax.dot_general(pb, dost, TN, preferred_element_type=F32)
            dks.append(dkf + pltpu.roll(dkf, HEAD_DIM, 1))
            dvs.append(dvf + pltpu.roll(dvf, HEAD_DIM, 1))
            contrib = -jnp.exp(snk - lse4) * delta
            for g in range(4):
                tot = jnp.sum(contrib[g * BLK:(g + 1) * BLK], axis=0, keepdims=True)
                ds_ref[0:1, :] += jnp.where(lane == kh * 4 + g, tot, 0.0)
        dk_blk = jnp.where(lo, dks[0], dks[1])
        dv_blk = jnp.where(lo, dvs[0], dvs[1])
        dk_ref[pl.ds(start, 3 * BLK), :] += dk_blk[:3 * BLK]
        dk_ref[pl.ds(n_lat, m_ctx), :] += dk_blk[3 * BLK:]
        dv_ref[pl.ds(start, 3 * BLK), :] += dv_blk[:3 * BLK]
        dv_ref[pl.ds(n_lat, m_ctx), :] += dv_blk[3 * BLK:]

    qspec = pl.BlockSpec((BLK, 4 * LANES), lambda p, i: (i, p))
    outs, carried = _pcall(
        body, (sink, qkv, qkv, qkv, o, do, lse), name=name, grid=(2, nb),
        in_specs=[SMEM_SPEC, qspec,
                  pl.BlockSpec((r, LANES), lambda p, i: (0, kcol + p)),
                  pl.BlockSpec((r, LANES), lambda p, i: (0, kcol + 2 + p)),
                  qspec, qspec, pl.BlockSpec((None, BLK, 8), lambda p, i: (p, i, 0))],
        out_specs=[qspec, pl.BlockSpec((r, LANES), lambda p, i: (0, p)), pl.BlockSpec((r, LANES), lambda p, i: (0, p)),
                   pl.BlockSpec((None, 8, LANES), lambda p, i: (p, 0, 0))],
        out_shape=[jax.ShapeDtypeStruct((r, D_MODEL), F32), jax.ShapeDtypeStruct((r, 2 * LANES), F32),
                   jax.ShapeDtypeStruct((r, 2 * LANES), F32), jax.ShapeDtypeStruct((2, 8, LANES), F32)], carry=carry)
    return outs if carry is None else (outs, carried)


def _gating_parts(z_ref, gv_ref):
    za = z_ref[...].astype(F32)
    zg = jax.nn.gelu(za)
    u, v = zg[:, :D_MODEL], zg[:, D_MODEL:]
    rs = lax.rsqrt(jnp.mean(v * v, axis=-1, keepdims=True) + EPS)
    return za, u, v, rs, v * rs * gv_ref[...]


def _mix(w_ref, vals):
    vb = vals.astype(BF16)
    return jnp.concatenate(
        [jnp.dot(w_ref[g], vb[:, g * LANES:(g + 1) * LANES], preferred_element_type=F32) for g in range(A_GROUPS)],
        axis=1)


def _gating_fwd(z, ws, bias, g_v, name):
    r = z.shape[0]

    def body(z_ref, w_ref, b_ref, gv_ref, y_ref):
        _, u, _, _, vn = _gating_parts(z_ref, gv_ref)
        y_ref[...] = (u * (_mix(w_ref, vn) + b_ref[...])).astype(BF16)

    return pl.pallas_call(
        body, name=name, grid=(r // CHUNK,),
        in_specs=[pl.BlockSpec((CHUNK, 2 * D_MODEL), lambda i: (i, P_A // (2 * D_MODEL))),
                  pl.BlockSpec((A_GROUPS, CHUNK, CHUNK), lambda i: (0, 0, 0)),
                  pl.BlockSpec((CHUNK, D_MODEL), lambda i: (0, 0)),
                  pl.BlockSpec((1, D_MODEL), lambda i: (0, 0))],
        out_specs=pl.BlockSpec((CHUNK, D_MODEL), lambda i: (i, 0)),
        out_shape=jax.ShapeDtypeStruct((r, D_MODEL), BF16), compiler_params=_params())(z, ws, bias, g_v)


def _gating_bwd(z, ws, ws_t, bias, g_v, dy, name):
    r = z.shape[0]

    def body(z_ref, w_ref, wt_ref, b_ref, gv_ref, dy_ref, dz_ref, dw_ref, db_ref, dg_ref):
        i = pl.program_id(0)

        @pl.when(i == 0)
        def _():
            dw_ref[...] = jnp.zeros_like(dw_ref)
            db_ref[...] = jnp.zeros_like(db_ref)
            dg_ref[...] = jnp.zeros_like(dg_ref)

        za, u, v, rs, vn = _gating_parts(z_ref, gv_ref)
        dyv = dy_ref[...].astype(F32)
        du = dyv * (_mix(w_ref, vn) + b_ref[...])
        dmixed = dyv * u
        dvn = _mix(wt_ref, dmixed)
        dmb, vnb = dmixed.astype(BF16), vn.astype(BF16)
        for g in range(A_GROUPS):
            cols = slice(g * LANES, (g + 1) * LANES)
            dw_ref[g] += lax.dot_general(dmb[:, cols], vnb[:, cols], NT, preferred_element_type=F32)
            db_ref[:, g:g + 1] += jnp.sum(dmixed[:, cols], axis=-1, keepdims=True)
        gv = gv_ref[...]
        vh = v * rs
        dg_ref[0:1, :] += jnp.sum(dvn * vh, axis=0, keepdims=True)
        dvh = dvn * gv
        dv = rs * (dvh - vh * jnp.mean(dvh * vh, axis=-1, keepdims=True))
        _, vjp = jax.vjp(jax.nn.gelu, za)
        dz_ref[...] = vjp(jnp.concatenate([du, dv], axis=1))[0].astype(BF16)

    wspec = pl.BlockSpec((A_GROUPS, CHUNK, CHUNK), lambda i: (0, 0, 0))
    return pl.pallas_call(
        body, name=name, grid=(r // CHUNK,),
        in_specs=[pl.BlockSpec((CHUNK, 2 * D_MODEL), lambda i: (i, P_A // (2 * D_MODEL))), wspec, wspec,
                  pl.BlockSpec((CHUNK, D_MODEL), lambda i: (0, 0)), pl.BlockSpec((1, D_MODEL), lambda i: (0, 0)),
                  pl.BlockSpec((CHUNK, D_MODEL), lambda i: (i, 0))],
        out_specs=[pl.BlockSpec((CHUNK, 2 * D_MODEL), lambda i: (i, 0)), wspec,
                   pl.BlockSpec((CHUNK, A_GROUPS), lambda i: (0, 0)), pl.BlockSpec((8, D_MODEL), lambda i: (0, 0))],
        out_shape=[jax.ShapeDtypeStruct((r, 2 * D_MODEL), BF16), jax.ShapeDtypeStruct((A_GROUPS, CHUNK, CHUNK), F32),
                   jax.ShapeDtypeStruct((CHUNK, A_GROUPS), F32), jax.ShapeDtypeStruct((8, D_MODEL), F32)],
        compiler_params=_params())(z, ws, ws_t, bias, g_v, dy)


def _taps(scr, j, n_lat):
    c = CONV_CHUNK
    s = pl.multiple_of(j * c, c)
    ext = scr[pl.ds(s, c + 2 * PAD), :]
    t = s + lax.broadcasted_iota(jnp.int32, (c, 1), 0)
    xm = jnp.where(t == n_lat, 0.0, pltpu.roll(ext, 1, 0)[PAD:PAD + c])
    xp = jnp.where(t == n_lat - 1, 0.0, pltpu.roll(ext, c + 2 * PAD - 1, 0)[PAD:PAD + c])
    return s, xm, ext[PAD:PAD + c], xp


def _zero_pads(scr, r):
    scr[pl.ds(0, PAD), :] = jnp.zeros((PAD, LANES), F32)
    scr[pl.ds(PAD + r, PAD), :] = jnp.zeros((PAD, LANES), F32)


def _col(arr_cols, c0):
    return pl.BlockSpec((arr_cols, LANES), lambda c: (0, c0 + c))


def _ffn_conv_fwd(up, w, n_lat, name):
    r = up.shape[0]
    nct = D_FF // LANES
    nchunk = r // CONV_CHUNK

    def body(a_ref, g_ref, w_ref, f_ref, scr):
        _zero_pads(scr, r)

        def fill(j, _):
            s = pl.multiple_of(j * CONV_CHUNK, CONV_CHUNK)
            scr[pl.ds(PAD + s, CONV_CHUNK), :] = a_ref[pl.ds(s, CONV_CHUNK), :].astype(F32)
            return 0
        lax.fori_loop(0, nchunk, fill, 0)
        wv = w_ref[...]

        def step(j, _):
            s, xm, x0, xp = _taps(scr, j, n_lat)
            ca = wv[0:1] * xm + wv[1:2] * x0 + wv[2:3] * xp
            gv = g_ref[pl.ds(s, CONV_CHUNK), :].astype(F32)
            f_ref[pl.ds(s, CONV_CHUNK), :] = (ca * _sigmoid(ca) * gv).astype(BF16)
            return 0
        lax.fori_loop(0, nchunk, step, 0)

    return pl.pallas_call(
        body, name=name, grid=(nct,),
        in_specs=[_col(r, 0), _col(r, nct), _col(3, 0)],
        out_specs=_col(r, 0), out_shape=jax.ShapeDtypeStruct((r, D_FF), BF16),
        scratch_shapes=[pltpu.VMEM((r + 2 * PAD, LANES), F32)], compiler_params=_params())(up, up, w)


def _ffn_conv_bwd(up, w, df, n_lat, name, carry=None):
    r = up.shape[0]
    nct = D_FF // LANES
    nchunk = r // CONV_CHUNK

    def body(a_ref, g_ref, w_ref, df_ref, dup_ref, dw_ref, scr, scr2):
        _zero_pads(scr, r)
        _zero_pads(scr2, r)

        def fill(j, _):
            s = pl.multiple_of(j * CONV_CHUNK, CONV_CHUNK)
            scr[pl.ds(PAD + s, CONV_CHUNK), :] = a_ref[pl.ds(s, CONV_CHUNK), :].astype(F32)
            return 0
        lax.fori_loop(0, nchunk, fill, 0)
        wv = w_ref[...]

        def first(j, carry):
            s, xm, x0, xp = _taps(scr, j, n_lat)
            ca = wv[0:1] * xm + wv[1:2] * x0 + wv[2:3] * xp
            sg = _sigmoid(ca)
            gv = g_ref[pl.ds(s, CONV_CHUNK), :].astype(F32)
            dfv = df_ref[pl.ds(s, CONV_CHUNK), :].astype(F32)
            dup_ref[1, pl.ds(s, CONV_CHUNK), :] = (dfv * ca * sg).astype(BF16)
            dca = dfv * gv * (sg * (1.0 + ca * (1.0 - sg)))
            scr2[pl.ds(PAD + s, CONV_CHUNK), :] = dca
            return tuple(cw + jnp.sum(dca * xv, axis=0, keepdims=True) for cw, xv in zip(carry, (xm, x0, xp)))
        zero = jnp.zeros((1, LANES), F32)
        dws = lax.fori_loop(0, nchunk, first, (zero, zero, zero))
        dw_ref[...] = _rows(list(dws) + [None] * 5)

        def second(j, _):
            s, ym, y0, yp = _taps(scr2, j, n_lat)
            dup_ref[0, pl.ds(s, CONV_CHUNK), :] = (wv[0:1] * yp + wv[1:2] * y0 + wv[2:3] * ym).astype(BF16)
            return 0
        lax.fori_loop(0, nchunk, second, 0)

    outs, carried = _pcall(
        body, (up, up, w, df), name=name, grid=(nct,),
        in_specs=[_col(r, 0), _col(r, nct), _col(3, 0), _col(r, 0)],
        out_specs=[pl.BlockSpec((2, r, LANES), lambda c: (0, 0, c)), _col(8, 0)],
        out_shape=[jax.ShapeDtypeStruct((2, r, D_FF), BF16), jax.ShapeDtypeStruct((8, D_FF), F32)],
        scratch_shapes=[pltpu.VMEM((r + 2 * PAD, LANES), F32), pltpu.VMEM((r + 2 * PAD, LANES), F32)], carry=carry)
    return outs if carry is None else (outs, carried)


def _sconv_fwd(z, w, n_lat, name):
    r = z.shape[0]
    nct = D_MODEL // LANES
    nchunk = r // CONV_CHUNK
    c0 = P_B // LANES

    def body(bg_ref, cg_ref, hb_ref, w_ref, y_ref, scr):
        _zero_pads(scr, r)

        def fill(j, _):
            s = pl.multiple_of(j * CONV_CHUNK, CONV_CHUNK)
            rows = pl.ds(s, CONV_CHUNK)
            scr[pl.ds(PAD + s, CONV_CHUNK), :] = cg_ref[rows, :].astype(F32) * hb_ref[rows, :].astype(F32)
            return 0
        lax.fori_loop(0, nchunk, fill, 0)
        wv = w_ref[...]

        def step(j, _):
            s, xm, x0, xp = _taps(scr, j, n_lat)
            conv = wv[0:1] * xm + wv[1:2] * x0 + wv[2:3] * xp
            y_ref[pl.ds(s, CONV_CHUNK), :] = (bg_ref[pl.ds(s, CONV_CHUNK), :].astype(F32) * conv).astype(BF16)
            return 0
        lax.fori_loop(0, nchunk, step, 0)

    return pl.pallas_call(
        body, name=name, grid=(nct,),
        in_specs=[_col(r, c0), _col(r, c0 + nct), _col(r, c0 + 2 * nct), _col(3, 0)],
        out_specs=_col(r, 0), out_shape=jax.ShapeDtypeStruct((r, D_MODEL), BF16),
        scratch_shapes=[pltpu.VMEM((r + 2 * PAD, LANES), F32)], compiler_params=_params())(z, z, z, w)


def _sconv_bwd(z, w, dy, n_lat, name):
    r = z.shape[0]
    nct = D_MODEL // LANES
    nchunk = r // CONV_CHUNK
    c0 = P_B // LANES

    def body(bg_ref, cg_ref, hb_ref, w_ref, dy_ref, dz_ref, dw_ref, scr, scr2):
        _zero_pads(scr, r)
        _zero_pads(scr2, r)

        def fill(j, _):
            s = pl.multiple_of(j * CONV_CHUNK, CONV_CHUNK)
            rows = pl.ds(s, CONV_CHUNK)
            scr[pl.ds(PAD + s, CONV_CHUNK), :] = cg_ref[rows, :].astype(F32) * hb_ref[rows, :].astype(F32)
            return 0
        lax.fori_loop(0, nchunk, fill, 0)
        wv = w_ref[...]

        def first(j, carry):
            s, xm, x0, xp = _taps(scr, j, n_lat)
            rows = pl.ds(s, CONV_CHUNK)
            conv = wv[0:1] * xm + wv[1:2] * x0 + wv[2:3] * xp
            dyv = dy_ref[rows, :].astype(F32)
            dz_ref[0, rows, :] = (dyv * conv).astype(BF16)
            dconv = dyv * bg_ref[rows, :].astype(F32)
            scr2[pl.ds(PAD + s, CONV_CHUNK), :] = dconv
            return tuple(cw + jnp.sum(dconv * xv, axis=0, keepdims=True) for cw, xv in zip(carry, (xm, x0, xp)))
        zero = jnp.zeros((1, LANES), F32)
        dws = lax.fori_loop(0, nchunk, first, (zero, zero, zero))
        dw_ref[...] = _rows(list(dws) + [None] * 5)

        def second(j, _):
            s, ym, y0, yp = _taps(scr2, j, n_lat)
            rows = pl.ds(s, CONV_CHUNK)
            dx = wv[0:1] * yp + wv[1:2] * y0 + wv[2:3] * ym
            dz_ref[1, rows, :] = (dx * hb_ref[rows, :].astype(F32)).astype(BF16)
            dz_ref[2, rows, :] = (dx * cg_ref[rows, :].astype(F32)).astype(BF16)
            return 0
        lax.fori_loop(0, nchunk, second, 0)

    return pl.pallas_call(
        body, name=name, grid=(nct,),
        in_specs=[_col(r, c0), _col(r, c0 + nct), _col(r, c0 + 2 * nct), _col(3, 0), _col(r, 0)],
        out_specs=[pl.BlockSpec((3, r, LANES), lambda c: (0, 0, c)), _col(8, 0)],
        out_shape=[jax.ShapeDtypeStruct((3, r, D_MODEL), BF16), jax.ShapeDtypeStruct((8, D_MODEL), F32)],
        scratch_shapes=[pltpu.VMEM((r + 2 * PAD, LANES), F32), pltpu.VMEM((r + 2 * PAD, LANES), F32)],
        compiler_params=_params())(z, z, z, w, dy)


def _merge_fwd(z, b_gate, ps, name):
    r = z.shape[0]
    tm = ROW_TILE
    row = pl.BlockSpec((tm, D_MODEL), lambda i: (i, 0))

    def body(zg_ref, b_ref, p0_ref, p1_ref, p2_ref, o_ref):
        gates = _sigmoid(zg_ref[...].astype(F32) + b_ref[...])
        acc = None
        for t, p_ref in enumerate((p0_ref, p1_ref, p2_ref)):
            term = gates[:, t * D_MODEL:(t + 1) * D_MODEL] * p_ref[...].astype(F32)
            acc = term if acc is None else acc + term
        o_ref[...] = acc.astype(BF16)

    return pl.pallas_call(
        body, name=name, grid=(r // tm,),
        in_specs=[pl.BlockSpec((tm, 3 * D_MODEL), lambda i: (i, 0)), pl.BlockSpec((1, 3 * D_MODEL), lambda i: (0, 0)),
                  row, row, row],
        out_specs=row, out_shape=jax.ShapeDtypeStruct((r, D_MODEL), BF16), compiler_params=_params())(z, b_gate, *ps)


def _merge_bwd(z, b_gate, ps, dmerged, name):
    r = z.shape[0]
    tm = ROW_TILE
    row = pl.BlockSpec((tm, D_MODEL), lambda i: (i, 0))
    wide = pl.BlockSpec((tm, 3 * D_MODEL), lambda i: (i, 0))

    def body(zg_ref, b_ref, p0_ref, p1_ref, p2_ref, dm_ref, d0_ref, d1_ref, d2_ref, dz_ref, db_ref):
        @pl.when(pl.program_id(0) == 0)
        def _():
            db_ref[...] = jnp.zeros_like(db_ref)

        gates = _sigmoid(zg_ref[...].astype(F32) + b_ref[...])
        dm = dm_ref[...].astype(F32)
        for t, (p_ref, d_ref) in enumerate(((p0_ref, d0_ref), (p1_ref, d1_ref), (p2_ref, d2_ref))):
            cols = slice(t * D_MODEL, (t + 1) * D_MODEL)
            gt = gates[:, cols]
            d_ref[...] = (dm * gt).astype(BF16)
            dlogit = dm * p_ref[...].astype(F32) * gt * (1.0 - gt)
            dz_ref[:, cols] = dlogit.astype(BF16)
            db_ref[0:1, cols] += jnp.sum(dlogit, axis=0, keepdims=True)

    shp = jax.ShapeDtypeStruct((r, D_MODEL), BF16)
    return pl.pallas_call(
        body, name=name, grid=(r // tm,),
        in_specs=[wide, pl.BlockSpec((1, 3 * D_MODEL), lambda i: (0, 0)), row, row, row, row],
        out_specs=[row, row, row, wide, pl.BlockSpec((8, 3 * D_MODEL), lambda i: (0, 0))],
        out_shape=[shp, shp, shp, jax.ShapeDtypeStruct((r, 3 * D_MODEL), BF16),
                   jax.ShapeDtypeStruct((8, 3 * D_MODEL), F32)],
        compiler_params=_params())(z, b_gate, *ps, dmerged)


def _sum_slots(buf, name):
    s, rows, _ = buf.shape
    tr = _pick(rows, [512, 256, 128, 64, 32, 16, 8])

    def body(b_ref, o_ref):
        acc = b_ref[0]
        for t in range(1, s):
            acc = acc + b_ref[t]
        o_ref[...] = acc

    return pl.pallas_call(
        body, name=name, grid=(rows // tr,),
        in_specs=[pl.BlockSpec((s, tr, LANES), lambda i: (0, i, 0))],
        out_specs=pl.BlockSpec((tr, LANES), lambda i: (i, 0)),
        out_shape=jax.ShapeDtypeStruct((rows, LANES), F32), compiler_params=_params())(buf)


def _adamw(w, gsrcs, m, v, name, scale=None):
    nl, rows, cols = w.shape
    assert len(gsrcs) == nl
    s = gsrcs[0].shape[0]
    tr = _pick(rows, [128, 64, 32, 16, 8])
    blk = pl.BlockSpec((None, tr, cols), lambda l, i: (l, i, 0))
    c1 = 1.0 / (1.0 - ADAM_B1 ** ADAM_STEP)
    c2 = 1.0 / (1.0 - ADAM_B2 ** ADAM_STEP)

    def gspec(t):
        return pl.BlockSpec((s, tr, cols), lambda l, i: (0, jnp.where(l == t, i, 0), 0))

    def body(*refs):
        w_ref, g_refs, (m_ref, v_ref) = refs[0], refs[1:1 + nl], refs[1 + nl:3 + nl]
        rest = refs[3 + nl:]
        if scale is not None:
            sc_ref, rest = rest[0], rest[1:]
        go_ref, d_ref, mo_ref, vo_ref = rest
        layer = pl.program_id(0)
        g = None
        for t in range(nl):
            gt = g_refs[t][0].astype(F32)
            for q in range(1, s):
                gt = gt + g_refs[t][q].astype(F32)
            g = gt if g is None else jnp.where(layer == t, gt, g)
        if scale is not None:
            g = g * sc_ref[...]
        mn = ADAM_B1 * m_ref[...] + (1.0 - ADAM_B1) * g
        vn = ADAM_B2 * v_ref[...] + (1.0 - ADAM_B2) * (g * g)
        go_ref[...] = g
        mo_ref[...] = mn
        vo_ref[...] = vn
        d_ref[...] = -ADAM_LR * ((mn * c1) / (jnp.sqrt(vn * c2) + ADAM_EPS) + ADAM_WD * w_ref[...])

    shp = jax.ShapeDtypeStruct((nl, rows, cols), F32)
    ins = [w] + list(gsrcs) + [m, v] + ([] if scale is None else [scale])
    return pl.pallas_call(
        body, name=name, grid=(nl, rows // tr),
        in_specs=[blk] + [gspec(t) for t in range(nl)] + [blk, blk] + ([] if scale is None else [blk]),
        out_specs=[blk] * 4, out_shape=[shp] * 4, compiler_params=_params())(*ins)


def _pack(arrs):
    flat = []
    for a in arrs:
        a = a.reshape(-1).astype(F32)
        pad = (-a.shape[0]) % (8 * LANES)
        flat.append(jnp.pad(a, (0, pad)) if pad else a)
    return jnp.concatenate(flat).reshape(-1, LANES)


def _unpack(buf, shapes, lead=()):
    out, row = [], 0
    for shp in shapes:
        n = 1
        for d in shp:
            n *= d
        nrows = -(-n // (8 * LANES)) * 8
        piece = buf[..., row:row + nrows, :].reshape(lead + (nrows * LANES,))[..., :n]
        out.append(piece.reshape(lead + tuple(shp)))
        row += nrows
    return out


def _silu(x):
    return x * jax.nn.sigmoid(x)


def _rope_tables(n_lat, m_ctx):
    pos = jnp.arange(n_lat)
    row = (pos // GRID_W).astype(F32)
    col = (pos % GRID_W).astype(F32)
    half = HEAD_DIM // 2
    inv = ROPE_THETA ** (-jnp.arange(0, half, 2, dtype=F32) / half)
    ang = jnp.concatenate([row[:, None] * inv, col[:, None] * inv], axis=-1)
    cos, sin = jnp.cos(ang), jnp.sin(ang)
    cos2 = jnp.tile(jnp.concatenate([cos, cos], axis=-1), (1, LANES // HEAD_DIM))
    sin2 = jnp.tile(jnp.concatenate([-sin, sin], axis=-1), (1, LANES // HEAD_DIM))
    return (jnp.concatenate([cos2, jnp.ones((m_ctx, LANES), F32)], axis=0),
            jnp.concatenate([sin2, jnp.zeros((m_ctx, LANES), F32)], axis=0))


def _to_slots(full, axis):
    shp = full.shape
    new = shp[:axis] + (N_DEV, shp[axis] // N_DEV) + shp[axis + 1:]
    return jnp.moveaxis(full.reshape(new), axis, 0)


def _from_slots(slots, axis):
    moved = jnp.moveaxis(slots, 0, axis)
    shp = moved.shape
    return moved.reshape(shp[:axis] + (shp[axis] * shp[axis + 1],) + shp[axis + 2:])


def _permute_in(w):
    return jnp.concatenate([w[:, OFF_G:], w[:, :OFF_K], w[:, OFF_A:OFF_B], w[:, OFF_B:OFF_G], w[:, OFF_K:OFF_A]],
                           axis=1)


def kernel(x, c, ctx, c_ctx, w_mod, b_mod, g_mix, w_in, b_gate, sink, w_spatial, b_spatial, g_v, w_sconv, w_branch, w_out, g_ffn, w_up, w_fconv, w_down, g_final, loss_target, m_c_ctx, m_w_mod, m_b_mod, m_g_mix, m_w_in, m_b_gate, m_sink, m_w_spatial, m_b_spatial, m_g_v, m_w_sconv, m_w_branch, m_w_out, m_g_ffn, m_w_up, m_w_fconv, m_w_down, m_g_final, v_c_ctx, v_w_mod, v_b_mod, v_g_mix, v_w_in, v_b_gate, v_sink, v_w_spatial, v_b_spatial, v_g_v, v_w_sconv, v_w_branch, v_w_out, v_g_ffn, v_w_up, v_w_fconv, v_w_down, v_g_final):
    n_lat, m_ctx = x.shape[1], ctx.shape[1]
    r = n_lat + m_ctx
    me = 4 * lax.axis_index("x") + 2 * lax.axis_index("y") + lax.axis_index("c")
    mod_w = w_mod.shape[2]
    bm = _pick(r, [768, 256])

    shard_axis = {"in": 1, "br": 1, "out": 0, "up": 1, "dn": 0}
    shards = {}
    for kind, wt in (("in", w_in), ("br", w_branch), ("out", w_out), ("up", w_up), ("dn", w_down)):
        wb = wt.astype(BF16)
        for l in range(DEPTH):
            shards[kind, l] = wb[l]
    full = {}

    def arrive(items, got):
        for key, slots in zip(items, got):
            wfull = _from_slots(slots, shard_axis[key[0]])
            full[key] = _permute_in(wfull) if key[0] == "in" else wfull

    def gather_of(items):
        return [shards[key] for key in items], ["gather"] * len(items)

    small_shapes = [c.shape, w_sconv.shape, w_fconv.shape]
    g_in0, g_small = _exchange([shards["in", 0], _pack([c, w_sconv, w_fconv])], ["gather"] * 2, "gather_first")
    arrive([("in", 0)], [g_in0])
    c_all, sconv_all, fconv_all = _unpack(g_small, small_shapes, lead=(N_DEV,))
    c_all = c_all.reshape(N_DEV, D_MODEL)
    w_sconv_full = _from_slots(sconv_all, 2)
    w_fconv_full = _from_slots(fconv_all, 2)

    act = jnp.concatenate([_silu(c_all), _silu(c_ctx)[None], jnp.zeros((7, D_MODEL), F32)], axis=0)
    mod_part = jnp.stack([_mm(act, w_mod[l], name=f"mod_fwd{l}", bm=16, bn=mod_w, bk=D_MODEL, out_dtype=F32)
                          for l in range(DEPTH)])
    (mod_all,) = _exchange([mod_part], ["gather"], "gather_mod")
    mod_full = _from_slots(mod_all, 2) + b_mod[:, None, :]
    mods = []
    for l in range(DEPTH):
        mine = lax.dynamic_index_in_dim(mod_full[l], me, axis=0, keepdims=False).reshape(6, D_MODEL)
        theirs = mod_full[l, N_DEV].reshape(6, D_MODEL)
        mods.append(jnp.pad(jnp.stack([mine, theirs]), ((0, 0), (0, 2), (0, 0))))

    cos, sin = _rope_tables(n_lat, m_ctx)
    xs = jnp.concatenate([x[0], ctx[0]], axis=0)
    ws_b = w_spatial.astype(BF16)
    ws_t = jnp.swapaxes(w_spatial, 2, 3).astype(BF16)
    vec = lambda a: a.reshape(1, -1)

    saved = []
    res = None
    for l in range(DEPTH):
        s = {}
        if res is None:
            s["x0"] = xs
            s["h"] = _norm_fwd(xs, vec(g_mix[l]), mods[l], n_lat, 0, 1, f"norm_mix{l}")
        else:
            s["x0"], s["h"] = _norm_fwd(xs, vec(g_mix[l]), mods[l], n_lat, 0, 1, f"norm_mix{l}", res=res)
        items = [("br", l), ("out", l), ("up", l)]
        s["z"], got = _mm(s["h"], full["in", l], name=f"proj_in{l}", bm=bm, bn=_pick(IN_W, [2432, 512]), bk=D_MODEL,
                          out_dtype=BF16, carry=gather_of(items))
        arrive(items, got)
        s["qkv"] = _rope_fwd(s["z"], cos, sin, f"rope{l}")
        items = [("dn", l)] + ([("in", l + 1)] if l + 1 < DEPTH else [])
        (s["y0"], s["lse"]), got = _attn_fwd(s["qkv"], sink[l], n_lat, m_ctx, f"attn{l}", carry=gather_of(items))
        arrive(items, got)
        s["bias"] = jnp.repeat(b_spatial[l].T, LANES, axis=1)
        s["y1"] = _gating_fwd(s["z"], ws_b[l], s["bias"], vec(g_v[l]), f"gating{l}")
        s["y2"] = _sconv_fwd(s["z"], w_sconv_full[l], n_lat, f"sconv{l}")
        s["p"] = [_mm(s[f"y{t}"], full["br", l], b_lead=t, name=f"branch{l}_{t}", bm=bm, bn=D_MODEL, bk=D_MODEL,
                      out_dtype=BF16) for t in range(3)]
        s["merged"] = _merge_fwd(s["z"], vec(b_gate[l]), s["p"], f"merge{l}")
        s["o"] = _mm(s["merged"], full["out", l], name=f"proj_out{l}", bm=bm, bn=D_MODEL, bk=D_MODEL, out_dtype=F32)
        s["x1"], s["h2"] = _norm_fwd(s["x0"], vec(g_ffn[l]), mods[l], n_lat, 3, 4, f"norm_ffn{l}",
                                     res=(s["o"], mods[l], 2))
        s["up"] = _mm(s["h2"], full["up", l], name=f"ffn_up{l}", bm=bm, bn=_pick(2 * D_FF, [1408]), bk=D_MODEL,
                      out_dtype=BF16)
        s["f"] = _ffn_conv_fwd(s["up"], w_fconv_full[l], n_lat, f"ffn_conv{l}")
        s["dd"] = _mm(s["f"], full["dn", l], name=f"ffn_down{l}", bm=bm, bn=D_MODEL, bk=D_FF, out_dtype=F32)
        saved.append(s)
        xs, res = s["x1"], (s["dd"], mods[l], 5)

    top = saved[DEPTH - 1]
    dxs, d_dd, acc_final, accs_top = _loss_bwd(top["x1"], top["dd"], mods[DEPTH - 1], vec(g_final), loss_target[0],
                                               n_lat, "loss")
    loss_part = acc_final[1, 0]
    dg_final = acc_final[0]
    dmods = [None] * DEPTH
    gate2 = accs_top[:, 2]
    grads = {k: [None] * DEPTH for k in ("g_mix", "g_ffn", "g_v", "b_gate", "sink", "w_spatial", "b_spatial",
                                         "w_sconv", "w_fconv", "w_in", "w_branch", "w_out", "w_up", "w_down")}
    bk_r = _pick(r, [768, 256])
    small_names = ["g_ffn", "g_v", "b_gate", "sink", "w_spatial", "b_spatial", "w_sconv", "w_fconv"]
    recv = {}
    small_recv, small_shapes_of = {}, {}

    def small_pack(l):
        arrs = [grads[k][l] for k in small_names]
        if l > 0:
            arrs.append(grads["g_mix"][l])
        if l == DEPTH - 1:
            arrs += [loss_part.reshape(1), dg_final]
        small_shapes_of[l] = [a.shape for a in arrs]
        return _pack(arrs)

    for l in reversed(range(DEPTH)):
        s = saved[l]
        df = _mm(d_dd, full["dn", l], tb=True, name=f"d_ffn_down{l}", bm=bm, bn=_pick(D_FF, [1408]), bk=D_MODEL,
                 out_dtype=BF16)
        grads["w_down"][l] = _mm(s["f"], d_dd, ta=True, name=f"g_ffn_down{l}", bm=_pick(D_FF, [1408]), bn=D_MODEL,
                                 bk=bk_r, out_dtype=BF16)
        carry = None if l + 1 == DEPTH else ([_to_slots(grads["w_in"][l + 1], 1)], ["a2a"])
        if carry is None:
            dup, dwf = _ffn_conv_bwd(s["up"], w_fconv_full[l], df, n_lat, f"d_ffn_conv{l}")
        else:
            (dup, dwf), (recv["in", l + 1],) = _ffn_conv_bwd(s["up"], w_fconv_full[l], df, n_lat, f"d_ffn_conv{l}",
                                                            carry=carry)
        grads["w_fconv"][l] = dwf[:3]
        kb = 1408
        nkb = D_FF // kb
        dh2 = _mm_nt_pieces([(dup, 0, 0, nkb, 0), (dup, 1, 0, nkb, nkb)], full["up", l], name=f"d_ffn_up{l}", bm=bm,
                            bk=kb)
        nbh = D_FF // kb
        grads["w_up"][l] = _mm(
            s["h2"], dup, ta=True, name=f"g_ffn_up{l}", bm=D_MODEL, bn=kb, bk=bk_r, out_dtype=BF16,
            b_spec=(pl.BlockSpec((None, bk_r, kb), lambda i, j, k: (j // nbh, k, j % nbh)), 2 * D_FF))
        dx1, d_o, acc, accs = _norm_bwd(s["x1"], vec(g_ffn[l]), mods[l], dh2, dxs, n_lat, 3, 4, f"d_norm_ffn{l}",
                                        res=(s["o"], mods[l], 2))
        grads["g_ffn"][l] = acc[0]
        shift2, scale2, gate1 = accs[:, 0], accs[:, 1], accs[:, 2]
        dmerged = _mm(d_o, full["out", l], tb=True, name=f"d_proj_out{l}", bm=bm, bn=D_MODEL, bk=D_MODEL,
                      out_dtype=BF16)
        grads["w_out"][l] = _mm(s["merged"], d_o, ta=True, name=f"g_proj_out{l}", bm=D_MODEL, bn=D_MODEL, bk=bk_r,
                                out_dtype=BF16)
        dp0, dp1, dp2, dz_g, dbg = _merge_bwd(s["z"], vec(b_gate[l]), s["p"], dmerged, f"d_merge{l}")
        grads["b_gate"][l] = dbg[0]
        dps = (dp0, dp1, dp2)
        dys = [_mm(dps[t], full["br", l], tb=True, b_lead=t, name=f"d_branch{l}_{t}", bm=bm, bn=D_MODEL, bk=D_MODEL,
                   out_dtype=BF16) for t in range(3)]
        grads["w_branch"][l] = jnp.stack(
            [_mm(s[f"y{t}"], dps[t], ta=True, name=f"g_branch{l}_{t}", bm=D_MODEL, bn=D_MODEL, bk=bk_r,
                 out_dtype=BF16) for t in range(3)])
        arrs = [_to_slots(grads["w_down"][l], 0), _to_slots(grads["w_up"][l], 1), _to_slots(grads["w_out"][l], 0),
                _to_slots(grads["w_branch"][l], 1)]
        modes = ["a2a"] * 4
        if l + 1 < DEPTH:
            arrs.append(small_pack(l + 1))
            modes.append("gather")
        (dq, dk, dv, dsk), got = _attn_bwd(s["qkv"], sink[l], s["y0"], dys[0], s["lse"], n_lat, m_ctx, f"d_attn{l}",
                                           carry=(arrs, modes))
        recv["dn", l], recv["up", l], recv["out", l], recv["br", l] = got[:4]
        if l + 1 < DEPTH:
            small_recv[l + 1] = got[4]
        grads["sink"][l] = dsk[:, 0, :8].reshape(N_HEADS)
        dz_qkv = _rope_bwd(dq, dk, dv, cos, sin, f"d_rope{l}")
        dz_a, dws, dbs, dgv = _gating_bwd(s["z"], ws_b[l], ws_t[l], s["bias"], vec(g_v[l]), dys[1], f"d_gating{l}")
        grads["w_spatial"][l], grads["b_spatial"][l], grads["g_v"][l] = dws, dbs.T, dgv[0]
        dz_b, dwsc = _sconv_bwd(s["z"], w_sconv_full[l], dys[2], n_lat, f"d_sconv{l}")
        grads["w_sconv"][l] = dwsc[:3]
        kb = 512
        pieces = [(dz_g, None, 0, 3 * D_MODEL // kb, 0), (dz_qkv, None, 0, D_MODEL // kb, P_Q // kb),
                  (dz_a, None, 0, 2 * D_MODEL // kb, P_A // kb)]
        pieces += [(dz_b, t, 0, D_MODEL // kb, (P_B + t * D_MODEL) // kb) for t in range(3)]
        pieces += [(dz_qkv, None, D_MODEL // kb, 1, P_KV // kb)]
        gw = lambda b, nm, bn, **kw: _mm(s["h"], b, ta=True, name=f"g_proj_in{l}_{nm}", bm=D_MODEL, bn=bn, bk=bk_r,
                                         out_dtype=BF16, **kw)
        gw_g = gw(dz_g, "gate", 1024)
        gw_qkv = gw(dz_qkv, "qkv", 512)
        gw_a = gw(dz_a, "gating", 1024)
        gw_b = [gw(dz_b, f"sconv{t}", 1024, b_lead=t) for t in range(3)]
        grads["w_in"][l] = jnp.concatenate([gw_qkv, gw_a] + gw_b + [gw_g], axis=1)
        if l > 0:
            dh = _mm_nt_pieces(pieces, full["in", l], name=f"d_proj_in{l}", bm=bm, bk=kb)
        else:
            dh, (recv["in", 0], small_recv[0]) = _mm_nt_pieces(
                pieces, full["in", l], name=f"d_proj_in{l}", bm=bm, bk=kb,
                carry=([_to_slots(grads["w_in"][0], 1), small_pack(0)], ["a2a", "gather"]))
        below = None if l == 0 else (saved[l - 1]["dd"], mods[l - 1], 5)
        outs = _norm_bwd(s["x0"], vec(g_mix[l]), mods[l], dh, dx1, n_lat, 0, 1, f"d_norm_mix{l}", res=below)
        if below is None:
            dxs, acc, accs = outs
        else:
            dxs, d_dd, acc, accs = outs
        grads["g_mix"][l] = acc[0]
        dmods[l] = jnp.stack([accs[:, 0], accs[:, 1], gate1, shift2, scale2, gate2], axis=1)
        gate2 = accs[:, 2]

    dmod_own = jnp.stack([dmods[l][0].reshape(-1) for l in range(DEPTH)])
    dmod_ctx = jnp.stack([dmods[l][1].reshape(-1) for l in range(DEPTH)])
    late = [grads["g_mix"][0], dmod_own + dmod_ctx, dmod_ctx, dmod_own]
    late_shapes = [a.shape for a in late]
    (late_all,) = _exchange([_pack(late)], ["gather"], "gather_late_grads")
    g_mix0_g, b_mod_g, dmodc_tot, _ = _unpack(_sum_slots(late_all, "sum_late_grads"), late_shapes)
    dmod_all = _unpack(late_all, late_shapes, lead=(N_DEV,))[-1]
    layer_sums = [_unpack(_sum_slots(small_recv[l], f"sum_small_grads{l}"), small_shapes_of[l]) for l in range(DEPTH)]
    by_name = {k: jnp.stack([layer_sums[l][t] for l in range(DEPTH)]) for t, k in enumerate(small_names)}
    g_mix_g = jnp.stack([g_mix0_g] + [layer_sums[l][len(small_names)] for l in range(1, DEPTH)])
    loss_sum, g_final_g = layer_sums[DEPTH - 1][-2], layer_sums[DEPTH - 1][-1]
    g_ffn_g, g_v_g, b_gate_g, sink_g = by_name["g_ffn"], by_name["g_v"], by_name["b_gate"], by_name["sink"]
    w_spatial_g, b_spatial_g = by_name["w_spatial"], by_name["b_spatial"]
    w_sconv_g = lax.dynamic_slice_in_dim(by_name["w_sconv"], me * w_sconv.shape[2], w_sconv.shape[2], axis=2)
    w_fconv_g = lax.dynamic_slice_in_dim(by_name["w_fconv"], me * w_fconv.shape[2], w_fconv.shape[2], axis=2)

    dmod_cols = lax.dynamic_slice_in_dim(dmod_all, me * mod_w, mod_w, axis=2)
    dmodc_cols = lax.dynamic_slice_in_dim(dmodc_tot, me * mod_w, mod_w, axis=1)
    g_w_mod, cctx_part = [], None
    for l in range(DEPTH):
        rhs = jnp.concatenate([dmod_cols[:, l], dmodc_cols[l][None], jnp.zeros((7, mod_w), F32)], axis=0)
        g_w_mod.append(_mm(act, rhs, ta=True, name=f"g_mod{l}", bm=D_MODEL, bn=mod_w, bk=16, out_dtype=F32))
        lhs = jnp.pad(dmodc_cols[l][None], ((0, 7), (0, 0)))
        part = _mm(lhs, w_mod[l], tb=True, name=f"d_cctx{l}", bm=8, bn=D_MODEL, bk=mod_w, out_dtype=F32)
        cctx_part = part if cctx_part is None else cctx_part + part
    sg = jax.nn.sigmoid(c_ctx)
    dsilu = (sg * (1.0 + c_ctx * (1.0 - sg))).reshape(8, LANES)

    (r_cctx,) = _exchange([cctx_part[0].reshape(8, LANES)], ["gather"], "gather_c_ctx_grad")

    def per_layer(a):
        return a.reshape(a.shape[0], -1, a.shape[-1])

    upd = {}
    for nm, kind, wv, mv, vv in (("w_in", "in", w_in, m_w_in, v_w_in), ("w_branch", "br", w_branch, m_w_branch, v_w_branch),
                                 ("w_out", "out", w_out, m_w_out, v_w_out), ("w_up", "up", w_up, m_w_up, v_w_up),
                                 ("w_down", "dn", w_down, m_w_down, v_w_down), ("w_mod", None, w_mod, m_w_mod, v_w_mod)):
        if kind is None:
            gsrcs = [g[None] for g in g_w_mod]
        else:
            gsrcs = [per_layer(recv[kind, l]) for l in range(DEPTH)]
        outs = _adamw(per_layer(wv), gsrcs, per_layer(mv), per_layer(vv), f"adamw_{nm}")
        upd[nm] = [o.reshape(wv.shape) for o in outs]
    as_tile = lambda a: a.reshape(1, 8, LANES)
    upd["c_ctx"] = [o.reshape(D_MODEL) for o in _adamw(
        as_tile(c_ctx), [r_cctx], as_tile(m_c_ctx), as_tile(v_c_ctx), "adamw_c_ctx", scale=as_tile(dsilu))]

    names = ["b_mod", "g_mix", "b_gate", "sink", "w_spatial", "b_spatial", "g_v", "w_sconv", "g_ffn", "w_fconv",
             "g_final"]
    w_s = [b_mod, g_mix, b_gate, sink, w_spatial, b_spatial, g_v, w_sconv, g_ffn, w_fconv, g_final]
    g_s = [b_mod_g, g_mix_g, b_gate_g, sink_g, w_spatial_g, b_spatial_g, g_v_g, w_sconv_g, g_ffn_g, w_fconv_g,
           g_final_g]
    m_s = [m_b_mod, m_g_mix, m_b_gate, m_sink, m_w_spatial, m_b_spatial, m_g_v, m_w_sconv, m_g_ffn, m_w_fconv,
           m_g_final]
    v_s = [v_b_mod, v_g_mix, v_b_gate, v_sink, v_w_spatial, v_b_spatial, v_g_v, v_w_sconv, v_g_ffn, v_w_fconv,
           v_g_final]
    shapes = [a.shape for a in w_s]
    packed = _adamw(_pack(w_s)[None], [_pack(g_s)[None]], _pack(m_s)[None], _pack(v_s)[None], "adamw_small")
    unpacked = [_unpack(o[0], shapes) for o in packed]
    for t, nm in enumerate(names):
        upd[nm] = [unpacked[q][t] for q in range(4)]

    order = ["c_ctx", "w_mod", "b_mod", "g_mix", "w_in", "b_gate", "sink", "w_spatial", "b_spatial", "g_v", "w_sconv",
             "w_branch", "w_out", "g_ffn", "w_up", "w_fconv", "w_down", "g_final"]
    result = [loss_sum.reshape(()), dxs[:n_lat][None]]
    for q in range(4):
        result += [upd[nm][q] for nm in order]
    return tuple(result)
```

```python
import jax
import jax.numpy as jnp
from jax import lax
from jax.experimental import pallas as pl
from jax.experimental.pallas import tpu as pltpu

F32, BF16 = jnp.float32, jnp.bfloat16

D_MODEL = 1024
DEPTH = 2
GRID_W = 64
N_HEADS = 16
N_KV_HEADS = 4
HEAD_DIM = 64
WINDOW = 128
BLK = 128
ROPE_THETA = 10000.0
CHUNK = 128
A_GROUPS = 8
D_FF = 2816
EPS = 1e-6
NEG = -1e30
IN_W = 9728
OFF_K, OFF_A, OFF_B, OFF_G = 1024, 1536, 3584, 6656
P_Q, P_A, P_B, P_KV = 3072, 4096, 6144, 9216

N_DEV = 8
LANES = 128
ROW_TILE = 256
CONV_CHUNK = 256
PAD = 8
VMEM_LIMIT = 52 * 1024 * 1024

ADAM_LR, ADAM_B1, ADAM_B2, ADAM_EPS, ADAM_WD, ADAM_STEP = 0.001, 0.9, 0.999, 1e-08, 0.01, 10

HBM_SPEC = pl.BlockSpec(memory_space=pltpu.HBM)
SMEM_SPEC = pl.BlockSpec(memory_space=pltpu.SMEM)


def _params():
    return pltpu.CompilerParams(vmem_limit_bytes=VMEM_LIMIT)


def _pick(n, prefs):
    for p in prefs:
        if n % p == 0:
            return p
    raise ValueError((n, prefs))


def _sigmoid(x):
    return 0.5 * jnp.tanh(0.5 * x) + 0.5


def _mm(a, b, *, name, ta=False, tb=False, bm, bn, bk, out_dtype, a_lead=None, b_lead=None, b_spec=None,
        carry=None):
    ash = a.shape[1:] if a_lead is not None else a.shape
    bsh = b.shape[1:] if b_lead is not None else b.shape
    kc, mo = (ash[0], ash[1]) if ta else (ash[1], ash[0])
    no = (bsh[0] if tb else bsh[1]) if b_spec is None else b_spec[1]
    assert mo % bm == 0 and no % bn == 0 and kc % bk == 0, (name, mo, no, kc, bm, bn, bk)
    nk = kc // bk

    def lead(shape, fn, idx):
        if idx is None:
            return pl.BlockSpec(shape, fn)
        return pl.BlockSpec((None,) + shape, lambda i, j, k: (idx,) + fn(i, j, k))

    a_spec = lead((bk, bm), lambda i, j, k: (k, i), a_lead) if ta else lead((bm, bk), lambda i, j, k: (i, k), a_lead)
    if b_spec is not None:
        b_bs = b_spec[0]
    elif tb:
        b_bs = lead((bn, bk), lambda i, j, k: (j, k), b_lead)
    else:
        b_bs = lead((bk, bn), lambda i, j, k: (k, j), b_lead)
    dims = (((0 if ta else 1,), (1 if tb else 0,)), ((), ()))

    def body(a_ref, b_ref, o_ref, *scratch):
        if nk == 1:
            o_ref[...] = lax.dot_general(a_ref[...], b_ref[...], dims, preferred_element_type=F32).astype(o_ref.dtype)
        else:
            acc = scratch[0]
            k = pl.program_id(2)

            @pl.when(k == 0)
            def _():
                acc[...] = jnp.zeros_like(acc)

            acc[...] += lax.dot_general(a_ref[...], b_ref[...], dims, preferred_element_type=F32)

            @pl.when(k == nk - 1)
            def _():
                o_ref[...] = acc[...].astype(o_ref.dtype)

    outs, carried = _pcall(
        body, (a, b), name=name, grid=(mo // bm, no // bn, nk), in_specs=[a_spec, b_bs],
        out_specs=[pl.BlockSpec((bm, bn), lambda i, j, k: (i, j))],
        out_shape=[jax.ShapeDtypeStruct((mo, no), out_dtype)],
        scratch_shapes=[pltpu.VMEM((bm, bn), F32)] if nk > 1 else [], carry=carry)
    return outs[0] if carry is None else (outs[0], carried)


def _mm_nt_pieces(pieces, w, *, name, bm, carry=None):
    kout = w.shape[0]
    starts, total = [], 0
    for piece in pieces:
        starts.append(total)
        total += piece[3]
    mo = pieces[0][0].shape[-2]
    assert mo % bm == 0
    widths = sorted({piece[5] for piece in pieces}, reverse=True)

    def inside(p, k):
        return (k >= starts[p]) & (k < starts[p] + pieces[p][3])

    def a_spec(p):
        _, lead, col0, nblk, _, bk = pieces[p]

        def fn(i, k):
            t = jnp.clip(k - starts[p], 0, nblk - 1)
            if lead is None:
                return (i, col0 + t)
            return (t, i, col0) if lead == "step" else (lead, i, col0 + t)
        return pl.BlockSpec((bm, bk) if lead is None else (None, bm, bk), fn)

    def w_spec(bk):
        def fn(i, k):
            col = 0
            for p, piece in enumerate(pieces):
                if piece[5] == bk:
                    col = col + jnp.where(inside(p, k), piece[4] + k - starts[p], 0)
            return (0, col)
        return pl.BlockSpec((kout, bk), fn)

    n_p, n_w = len(pieces), len(widths)

    def body(*refs):
        a_refs, w_refs, o_ref, acc = refs[:n_p], refs[n_p:n_p + n_w], refs[n_p + n_w], refs[n_p + n_w + 1]
        k = pl.program_id(1)

        @pl.when(k == 0)
        def _():
            acc[...] = jnp.zeros_like(acc)

        for p in range(n_p):
            w_ref = w_refs[widths.index(pieces[p][5])]

            @pl.when(inside(p, k))
            def _(p=p, w_ref=w_ref):
                acc[...] += lax.dot_general(a_refs[p][...], w_ref[...], NT, preferred_element_type=F32)

        @pl.when(k == total - 1)
        def _():
            o_ref[...] = acc[...]

    outs, carried = _pcall(
        body, [piece[0] for piece in pieces] + [w] * n_w, name=name, grid=(mo // bm, total),
        in_specs=[a_spec(p) for p in range(n_p)] + [w_spec(bk) for bk in widths],
        out_specs=[pl.BlockSpec((bm, kout), lambda i, k: (i, 0))],
        out_shape=[jax.ShapeDtypeStruct((mo, kout), F32)],
        scratch_shapes=[pltpu.VMEM((bm, kout), F32)], carry=carry)
    return outs[0] if carry is None else (outs[0], carried)


def _xchg_out_shapes(arrs, modes):
    return [jax.ShapeDtypeStruct((N_DEV,) + a.shape if m == "gather" else a.shape, a.dtype)
            for a, m in zip(arrs, modes)]


def _xchg_sems(n):
    return [pltpu.SemaphoreType.DMA((n, N_DEV - 1)), pltpu.SemaphoreType.DMA((n, N_DEV - 1)),
            pltpu.SemaphoreType.DMA((n,))]


def _xchg_copies(ins, outs, modes, sems, arrivals=True):
    send_sems, recv_sems, local_sems = sems
    n = len(ins)
    x, y, c = lax.axis_index("x"), lax.axis_index("y"), lax.axis_index("c")
    me = 4 * x + 2 * y + c

    def src(t, slot):
        return ins[t] if modes[t] == "gather" else ins[t].at[slot]

    local = [pltpu.make_async_copy(src(t, me), outs[t].at[me], local_sems.at[t]) for t in range(n)]
    sends, recvs = [], []
    for j in range(1, N_DEV):
        px = 1 - x if (j >> 2) & 1 else x
        py = 1 - y if (j >> 1) & 1 else y
        pc = 1 - c if j & 1 else c
        peer = 4 * px + 2 * py + pc
        for t in range(n):
            common = dict(src_ref=src(t, peer), send_sem=send_sems.at[t, j - 1], recv_sem=recv_sems.at[t, j - 1],
                          device_id=(px, py, pc), device_id_type=pl.DeviceIdType.MESH)
            sends.append(pltpu.make_async_remote_copy(dst_ref=outs[t].at[me], **common))
            if arrivals:
                recvs.append(pltpu.make_async_remote_copy(dst_ref=outs[t].at[peer], **common))
    return local, sends, recvs


def _xchg_start(copies):
    local, sends, _ = copies
    for cp in local + sends:
        cp.start()


def _xchg_wait(copies):
    local, sends, recvs = copies
    for cp in recvs:
        cp.wait_recv()
    for cp in sends:
        cp.wait_send()
    for cp in local:
        cp.wait()


def _exchange(arrs, modes, name):
    n = len(arrs)

    def body(*refs):
        copies = _xchg_copies(refs[:n], refs[n:2 * n], modes, refs[2 * n:])
        _xchg_start(copies)
        _xchg_wait(copies)

    outs = pl.pallas_call(
        body, name=name, in_specs=[HBM_SPEC] * n, out_specs=[HBM_SPEC] * n, out_shape=_xchg_out_shapes(arrs, modes),
        scratch_shapes=_xchg_sems(n), compiler_params=pltpu.CompilerParams(has_side_effects=True),
    )(*arrs)
    return list(outs)


def _pcall(body, operands, *, name, grid, in_specs, out_specs, out_shape, scratch_shapes=(), carry=None):
    out_specs, out_shape, scratch_shapes = list(out_specs), list(out_shape), list(scratch_shapes)
    if carry is None:
        outs = pl.pallas_call(body, name=name, grid=grid, in_specs=in_specs, out_specs=out_specs,
                              out_shape=out_shape, scratch_shapes=scratch_shapes, compiler_params=_params())(*operands)
        return list(outs), []
    arrs, modes = carry
    n, n_in, n_out, n_scr = len(arrs), len(in_specs), len(out_specs), len(scratch_shapes)

    def wrapped(*refs):
        ins, c_in = refs[:n_in], refs[n_in:n_in + n]
        outs, c_out = refs[n_in + n:n_in + n + n_out], refs[n_in + n + n_out:n_in + 2 * n + n_out]
        rest = refs[n_in + 2 * n + n_out:]
        scr, sems = rest[:n_scr], rest[n_scr:]
        first, last = None, None
        for d, size in enumerate(grid):
            f, e = pl.program_id(d) == 0, pl.program_id(d) == size - 1
            first = f if first is None else first & f
            last = e if last is None else last & e

        @pl.when(first)
        def _():
            _xchg_start(_xchg_copies(c_in, c_out, modes, sems, arrivals=False))

        body(*ins, *outs, *scr)

        @pl.when(last)
        def _():
            _xchg_wait(_xchg_copies(c_in, c_out, modes, sems))

    outs = pl.pallas_call(
        wrapped, name=name, grid=grid, in_specs=list(in_specs) + [HBM_SPEC] * n,
        out_specs=out_specs + [HBM_SPEC] * n, out_shape=out_shape + _xchg_out_shapes(arrs, modes),
        scratch_shapes=scratch_shapes + _xchg_sems(n), compiler_params=_params())(*operands, *arrs)
    return list(outs[:n_out]), list(outs[n_out:])


def _row_specs(r, n_lat, tm):
    nbl = n_lat // tm
    row = pl.BlockSpec((tm, D_MODEL), lambda i: (i, 0))
    mod = pl.BlockSpec((None, 8, D_MODEL), lambda i: (i // nbl, 0, 0))
    vec = pl.BlockSpec((1, D_MODEL), lambda i: (0, 0))
    return nbl, row, mod, vec


def _rows(vals):
    width = [v for v in vals if v is not None][0].shape[1]
    return jnp.concatenate([jnp.zeros((1, width), F32) if v is None else v for v in vals], axis=0)


def _norm_fwd(xs, g, mods, n_lat, sh, sc, name, res=None):
    r = xs.shape[0]
    tm = ROW_TILE
    nbl, row, mod, vec = _row_specs(r, n_lat, tm)

    def norm(x, g_ref, m_ref, h_ref):
        rs = lax.rsqrt(jnp.mean(x * x, axis=-1, keepdims=True) + EPS)
        m = m_ref[...]
        h_ref[...] = ((x * rs * g_ref[...]) * (1.0 + m[sc:sc + 1]) + m[sh:sh + 1]).astype(BF16)

    if res is None:
        def body(x_ref, g_ref, m_ref, h_ref):
            norm(x_ref[...], g_ref, m_ref, h_ref)
        return pl.pallas_call(
            body, name=name, grid=(r // tm,), in_specs=[row, vec, mod], out_specs=row,
            out_shape=jax.ShapeDtypeStruct((r, D_MODEL), BF16), compiler_params=_params())(xs, g, mods)

    o, mods_res, gt = res

    def body(x_ref, o_ref, mr_ref, g_ref, m_ref, x1_ref, h_ref):
        x = x_ref[...] + mr_ref[...][gt:gt + 1] * o_ref[...]
        x1_ref[...] = x
        norm(x, g_ref, m_ref, h_ref)

    return pl.pallas_call(
        body, name=name, grid=(r // tm,), in_specs=[row, row, mod, vec, mod], out_specs=[row, row],
        out_shape=[jax.ShapeDtypeStruct((r, D_MODEL), F32), jax.ShapeDtypeStruct((r, D_MODEL), BF16)],
        compiler_params=_params())(xs, o, mods_res, g, mods)


def _rms_bwd(x, g, dy):
    rs = lax.rsqrt(jnp.mean(x * x, axis=-1, keepdims=True) + EPS)
    xh = x * rs
    dxh = dy * g
    dx = rs * (dxh - xh * jnp.mean(dxh * xh, axis=-1, keepdims=True))
    return dx, dy * xh, xh


def _acc_specs(nbl):
    acc_all = pl.BlockSpec((8, D_MODEL), lambda i: (0, 0))
    acc_stream = pl.BlockSpec((None, 8, D_MODEL), lambda i: (i // nbl, 0, 0))
    return acc_all, acc_stream


def _loss_bwd(xs, o, mods, g_final, target, n_lat, name):
    r = xs.shape[0]
    tm = ROW_TILE
    nbl, row, mod, vec = _row_specs(r, n_lat, tm)
    acc_all, acc_stream = _acc_specs(nbl)
    tgt = pl.BlockSpec((tm, D_MODEL), lambda i: (jnp.minimum(i, nbl - 1), 0))

    def body(x_ref, o_ref, m_ref, g_ref, t_ref, dx_ref, do_ref, acc_ref, accs_ref):
        i = pl.program_id(0)
        lat = i < nbl
        gate = m_ref[...][5:6]
        o_val = o_ref[...]
        x = x_ref[...] + gate * o_val
        g = g_ref[...]
        rs = lax.rsqrt(jnp.mean(x * x, axis=-1, keepdims=True) + EPS)
        y = x * rs * g
        err = jnp.where(lat, y - t_ref[...], 0.0)
        loss = 0.5 * jnp.sum(jnp.mean(err * err, axis=-1, keepdims=True), axis=0, keepdims=True)
        dy = err * (1.0 / D_MODEL)
        dx, dg_rows, _ = _rms_bwd(x, g, dy)
        dx_ref[...] = dx
        do_ref[...] = (gate * dx).astype(BF16)

        @pl.when(i == 0)
        def _():
            acc_ref[...] = jnp.zeros_like(acc_ref)

        @pl.when((i == 0) | (i == nbl))
        def _():
            accs_ref[...] = jnp.zeros_like(accs_ref)

        acc_ref[...] += _rows([jnp.sum(dg_rows, axis=0, keepdims=True), jnp.broadcast_to(loss, (1, D_MODEL))]
                              + [None] * 6)
        accs_ref[...] += _rows([None, None, jnp.sum(dx * o_val, axis=0, keepdims=True)] + [None] * 5)

    return pl.pallas_call(
        body, name=name, grid=(r // tm,), in_specs=[row, row, mod, vec, tgt],
        out_specs=[row, row, acc_all, acc_stream],
        out_shape=[jax.ShapeDtypeStruct((r, D_MODEL), F32), jax.ShapeDtypeStruct((r, D_MODEL), BF16),
                   jax.ShapeDtypeStruct((8, D_MODEL), F32), jax.ShapeDtypeStruct((2, 8, D_MODEL), F32)],
        compiler_params=_params())(xs, o, mods, g_final, target)


def _norm_bwd(xs, g, mods, dh, dx_in, n_lat, sh, sc, name, res=None):
    r = xs.shape[0]
    tm = ROW_TILE
    nbl, row, mod, vec = _row_specs(r, n_lat, tm)
    acc_all, acc_stream = _acc_specs(nbl)
    has_res = res is not None

    def body(*refs):
        if has_res:
            x_ref, g_ref, m_ref, dh_ref, dxi_ref, o_ref, mr_ref, dx_ref, do_ref, acc_ref, accs_ref = refs
        else:
            x_ref, g_ref, m_ref, dh_ref, dxi_ref, dx_ref, acc_ref, accs_ref = refs
        i = pl.program_id(0)
        x, g, m, dhv = x_ref[...], g_ref[...], m_ref[...], dh_ref[...]
        dy = dhv * (1.0 + m[sc:sc + 1])
        dxn, dg_rows, xh = _rms_bwd(x, g, dy)
        dx = dxi_ref[...] + dxn
        dx_ref[...] = dx
        d_gate = None
        if has_res:
            o_val = o_ref[...]
            do_ref[...] = (mr_ref[...][res[2]:res[2] + 1] * dx).astype(BF16)
            d_gate = jnp.sum(dx * o_val, axis=0, keepdims=True)

        @pl.when(i == 0)
        def _():
            acc_ref[...] = jnp.zeros_like(acc_ref)

        @pl.when((i == 0) | (i == nbl))
        def _():
            accs_ref[...] = jnp.zeros_like(accs_ref)

        acc_ref[...] += _rows([jnp.sum(dg_rows, axis=0, keepdims=True)] + [None] * 7)
        accs_ref[...] += _rows([jnp.sum(dhv, axis=0, keepdims=True),
                                jnp.sum(dhv * (xh * g), axis=0, keepdims=True), d_gate] + [None] * 5)

    ins = [xs, g, mods, dh, dx_in]
    in_specs = [row, vec, mod, row, row]
    out_specs = [row]
    out_shape = [jax.ShapeDtypeStruct((r, D_MODEL), F32)]
    if has_res:
        ins += [res[0], res[1]]
        in_specs += [row, mod]
        out_specs.append(row)
        out_shape.append(jax.ShapeDtypeStruct((r, D_MODEL), BF16))
    out_specs += [acc_all, acc_stream]
    out_shape += [jax.ShapeDtypeStruct((8, D_MODEL), F32), jax.ShapeDtypeStruct((2, 8, D_MODEL), F32)]
    return pl.pallas_call(body, name=name, grid=(r // tm,), in_specs=in_specs, out_specs=out_specs,
                          out_shape=out_shape, compiler_params=_params())(*ins)


def _rotate(t, cos, sin):
    width = t.shape[1]
    reps = width // LANES
    lane = lax.broadcasted_iota(jnp.int32, (1, width), 1)
    first = (lane % HEAD_DIM) < (HEAD_DIM // 2)
    swapped = jnp.where(first, pltpu.roll(t, width - HEAD_DIM // 2, 1), pltpu.roll(t, HEAD_DIM // 2, 1))
    return t * jnp.tile(cos, (1, reps)) + swapped * jnp.tile(sin, (1, reps))


def _rope_fwd(z, cos, sin, name):
    r = z.shape[0]
    tm = ROW_TILE
    kvw = 2 * N_KV_HEADS * HEAD_DIM
    tab = pl.BlockSpec((tm, LANES), lambda i: (i, 0))

    def body(q_ref, kv_ref, c_ref, s_ref, o_ref):
        c, s = c_ref[...], s_ref[...]
        kv = kv_ref[...]
        o_ref[:, :D_MODEL] = _rotate(q_ref[...].astype(F32), c, s).astype(BF16)
        o_ref[:, D_MODEL:D_MODEL + kvw // 2] = _rotate(kv[:, :kvw // 2].astype(F32), c, s).astype(BF16)
        o_ref[:, D_MODEL + kvw // 2:] = kv[:, kvw // 2:]

    return pl.pallas_call(
        body, name=name, grid=(r // tm,),
        in_specs=[pl.BlockSpec((tm, D_MODEL), lambda i: (i, P_Q // D_MODEL)),
                  pl.BlockSpec((tm, kvw), lambda i: (i, P_KV // kvw)), tab, tab],
        out_specs=pl.BlockSpec((tm, D_MODEL + kvw), lambda i: (i, 0)),
        out_shape=jax.ShapeDtypeStruct((r, D_MODEL + kvw), BF16), compiler_params=_params())(z, z, cos, sin)


def _rope_bwd(dq, dk, dv, cos, sin, name):
    r = dq.shape[0]
    tm = ROW_TILE
    kw = N_KV_HEADS * HEAD_DIM
    tab = pl.BlockSpec((tm, LANES), lambda i: (i, 0))

    def body(dq_ref, dk_ref, dv_ref, c_ref, s_ref, o_ref):
        c, s = c_ref[...], -s_ref[...]
        o_ref[:, :D_MODEL] = _rotate(dq_ref[...], c, s).astype(BF16)
        o_ref[:, D_MODEL:D_MODEL + kw] = _rotate(dk_ref[...], c, s).astype(BF16)
        o_ref[:, D_MODEL + kw:] = dv_ref[...].astype(BF16)

    return pl.pallas_call(
        body, name=name, grid=(r // tm,),
        in_specs=[pl.BlockSpec((tm, D_MODEL), lambda i: (i, 0)), pl.BlockSpec((tm, kw), lambda i: (i, 0)),
                  pl.BlockSpec((tm, kw), lambda i: (i, 0)), tab, tab],
        out_specs=pl.BlockSpec((tm, D_MODEL + 2 * kw), lambda i: (i, 0)),
        out_shape=jax.ShapeDtypeStruct((r, D_MODEL + 2 * kw), BF16), compiler_params=_params())(dq, dk, dv, cos, sin)


def _attn_setup(i, n_lat, m_ctx, nbl, k_ref, v_ref):
    start = pl.multiple_of(jnp.clip((i - 1) * BLK, 0, n_lat - 3 * BLK), BLK)
    nkeys = 3 * BLK + m_ctx
    rows = lax.broadcasted_iota(jnp.int32, (4 * BLK, nkeys), 0)
    cols = lax.broadcasted_iota(jnp.int32, (4 * BLK, nkeys), 1)
    qpos = i * BLK + (rows & (BLK - 1))
    mask = (cols >= 3 * BLK) | ((jnp.abs(start + cols - qpos) <= WINDOW) & (i < nbl))
    kblk = jnp.concatenate([k_ref[pl.ds(start, 3 * BLK), :], k_ref[pl.ds(n_lat, m_ctx), :]], axis=0)
    vblk = jnp.concatenate([v_ref[pl.ds(start, 3 * BLK), :], v_ref[pl.ds(n_lat, m_ctx), :]], axis=0)
    lo = lax.broadcasted_iota(jnp.int32, (1, LANES), 1) < HEAD_DIM
    return start, mask, kblk, vblk, lo


def _stack_heads(ref, kh, lo):
    a = ref[:, (2 * kh) * LANES:(2 * kh + 1) * LANES]
    b = ref[:, (2 * kh + 1) * LANES:(2 * kh + 2) * LANES]
    z = jnp.zeros_like(a)
    return jnp.concatenate([jnp.where(lo, a, z), jnp.where(lo, z, a), jnp.where(lo, b, z), jnp.where(lo, z, b)],
                           axis=0)


def _kv_variants(blk, rolled, kh, lo):
    z = jnp.zeros_like(blk)
    if kh == 0:
        return jnp.where(lo, blk, rolled), jnp.where(lo, blk, z), jnp.where(lo, z, rolled)
    return jnp.where(lo, rolled, blk), jnp.where(lo, rolled, z), jnp.where(lo, z, blk)


NT = (((1,), (1,)), ((), ()))
TN = (((0,), (0,)), ((), ()))


def _scores(qst, k2, mask, sink_ref, p, kh):
    s = lax.dot_general(qst, k2, NT, preferred_element_type=F32) * (HEAD_DIM ** -0.5)
    s = jnp.where(mask, s, NEG)
    snk = jnp.concatenate([jnp.full((BLK, 1), sink_ref[p * 8 + kh * 4 + g], F32) for g in range(4)], axis=0)
    return s, snk


def _attn_fwd(qkv, sink, n_lat, m_ctx, name, carry=None):
    r = qkv.shape[0]
    nb, nbl = r // BLK, n_lat // BLK
    kcol = D_MODEL // LANES

    def body(sink_ref, q_ref, k_ref, v_ref, o_ref, lse_ref):
        p, i = pl.program_id(0), pl.program_id(1)
        _, mask, kblk, vblk, lo = _attn_setup(i, n_lat, m_ctx, nbl, k_ref, v_ref)
        kr, vr = pltpu.roll(kblk, HEAD_DIM, 1), pltpu.roll(vblk, HEAD_DIM, 1)
        for kh in range(2):
            k2, _, _ = _kv_variants(kblk, kr, kh, lo)
            _, vlo, vhi = _kv_variants(vblk, vr, kh, lo)
            qst = _stack_heads(q_ref, kh, lo)
            s, snk = _scores(qst, k2, mask, sink_ref, p, kh)
            mx = jnp.maximum(jnp.max(s, axis=-1, keepdims=True), snk)
            pe = jnp.exp(s - mx)
            den = jnp.sum(pe, axis=-1, keepdims=True) + jnp.exp(snk - mx)
            inv = 1.0 / den
            pb = pe.astype(BF16)
            for jp in range(2):
                r0 = 2 * jp * BLK
                pair = (jnp.dot(pb[r0:r0 + BLK], vlo, preferred_element_type=F32)
                        + jnp.dot(pb[r0 + BLK:r0 + 2 * BLK], vhi, preferred_element_type=F32))
                pair = pair * jnp.where(lo, inv[r0:r0 + BLK], inv[r0 + BLK:r0 + 2 * BLK])
                o_ref[:, (2 * kh + jp) * LANES:(2 * kh + jp + 1) * LANES] = pair.astype(BF16)
            lse = mx + jnp.log(den)
            for g in range(4):
                lse_ref[:, kh * 4 + g:kh * 4 + g + 1] = lse[g * BLK:(g + 1) * BLK]

    outs, carried = _pcall(
        body, (sink, qkv, qkv, qkv), name=name, grid=(2, nb),
        in_specs=[SMEM_SPEC,
                  pl.BlockSpec((BLK, 4 * LANES), lambda p, i: (i, p)),
                  pl.BlockSpec((r, LANES), lambda p, i: (0, kcol + p)),
                  pl.BlockSpec((r, LANES), lambda p, i: (0, kcol + 2 + p))],
        out_specs=[pl.BlockSpec((BLK, 4 * LANES), lambda p, i: (i, p)),
                   pl.BlockSpec((None, BLK, 8), lambda p, i: (p, i, 0))],
        out_shape=[jax.ShapeDtypeStruct((r, D_MODEL), BF16), jax.ShapeDtypeStruct((2, r, 8), F32)], carry=carry)
    return outs if carry is None else (outs, carried)


def _attn_bwd(qkv, sink, o, do, lse, n_lat, m_ctx, name, carry=None):
    r = qkv.shape[0]
    nb, nbl = r // BLK, n_lat // BLK
    kcol = D_MODEL // LANES

    def body(sink_ref, q_ref, k_ref, v_ref, o_ref, do_ref, lse_ref, dq_ref, dk_ref, dv_ref, ds_ref):
        p, i = pl.program_id(0), pl.program_id(1)

        @pl.when(i == 0)
        def _():
            dk_ref[...] = jnp.zeros_like(dk_ref)
            dv_ref[...] = jnp.zeros_like(dv_ref)
            ds_ref[...] = jnp.zeros_like(ds_ref)

        start, mask, kblk, vblk, lo = _attn_setup(i, n_lat, m_ctx, nbl, k_ref, v_ref)
        kr, vr = pltpu.roll(kblk, HEAD_DIM, 1), pltpu.roll(vblk, HEAD_DIM, 1)
        lane = lax.broadcasted_iota(jnp.int32, (1, LANES), 1)
        dks, dvs = [], []
        for kh in range(2):
            k2, klo, khi = _kv_variants(kblk, kr, kh, lo)
            v2, _, _ = _kv_variants(vblk, vr, kh, lo)
            qst = _stack_heads(q_ref, kh, lo)
            dost = _stack_heads(do_ref, kh, lo)
            s, snk = _scores(qst, k2, mask, sink_ref, p, kh)
            lse4 = jnp.concatenate([lse_ref[:, kh * 4 + g:kh * 4 + g + 1] for g in range(4)], axis=0)
            pe = jnp.exp(s - lse4)
            dp = lax.dot_general(dost, v2, NT, preferred_element_type=F32)
            deltas = []
            for jp in range(2):
                cols = slice((2 * kh + jp) * LANES, (2 * kh + jp + 1) * LANES)
                prod = do_ref[:, cols].astype(F32) * o_ref[:, cols].astype(F32)
                deltas.append(jnp.sum(jnp.where(lo, prod, 0.0), axis=-1, keepdims=True))
                deltas.append(jnp.sum(jnp.where(lo, 0.0, prod), axis=-1, keepdims=True))
            delta = jnp.concatenate(deltas, axis=0)
            dsc = pe * (dp - delta) * (HEAD_DIM ** -0.5)
            dsb, pb = dsc.astype(BF16), pe.astype(BF16)
            for jp in range(2):
                r0 = 2 * jp * BLK
                dq_ref[:, (2 * kh + jp) * LANES:(2 * kh + jp + 1) * LANES] = (
                    jnp.dot(dsb[r0:r0 + BLK], klo, preferred_element_type=F32)
                    + jnp.dot(dsb[r0 + BLK:r0 + 2 * BLK], khi, preferred_element_type=F32))
            dkf = lax.dot_general(dsb, qst, TN, preferred_element_type=F32)
            dvf = lax.dot_general(pb, dost, TN, preferred_element_type=F32)
            dks.append(dkf + pltpu.roll(dkf, HEAD_DIM, 1))
            dvs.append(dvf + pltpu.roll(dvf, HEAD_DIM, 1))
            contrib = -jnp.exp(snk - lse4) * delta
            for g in range(4):
                tot = jnp.sum(contrib[g * BLK:(g + 1) * BLK], axis=0, keepdims=True)
                ds_ref[0:1, :] += jnp.where(lane == kh * 4 + g, tot, 0.0)
        dk_blk = jnp.where(lo, dks[0], dks[1])
        dv_blk = jnp.where(lo, dvs[0], dvs[1])
        dk_ref[pl.ds(start, 3 * BLK), :] += dk_blk[:3 * BLK]
        dk_ref[pl.ds(n_lat, m_ctx), :] += dk_blk[3 * BLK:]
        dv_ref[pl.ds(start, 3 * BLK), :] += dv_blk[:3 * BLK]
        dv_ref[pl.ds(n_lat, m_ctx), :] += dv_blk[3 * BLK:]

    qspec = pl.BlockSpec((BLK, 4 * LANES), lambda p, i: (i, p))
    outs, carried = _pcall(
        body, (sink, qkv, qkv, qkv, o, do, lse), name=name, grid=(2, nb),
        in_specs=[SMEM_SPEC, qspec,
                  pl.BlockSpec((r, LANES), lambda p, i: (0, kcol + p)),
                  pl.BlockSpec((r, LANES), lambda p, i: (0, kcol + 2 + p)),
                  qspec, qspec, pl.BlockSpec((None, BLK, 8), lambda p, i: (p, i, 0))],
        out_specs=[qspec, pl.BlockSpec((r, LANES), lambda p, i: (0, p)), pl.BlockSpec((r, LANES), lambda p, i: (0, p)),
                   pl.BlockSpec((None, 8, LANES), lambda p, i: (p, 0, 0))],
        out_shape=[jax.ShapeDtypeStruct((r, D_MODEL), F32), jax.ShapeDtypeStruct((r, 2 * LANES), F32),
                   jax.ShapeDtypeStruct((r, 2 * LANES), F32), jax.ShapeDtypeStruct((2, 8, LANES), F32)], carry=carry)
    return outs if carry is None else (outs, carried)


def _gating_parts(z_ref, gv_ref):
    za = z_ref[...].astype(F32)
    zg = jax.nn.gelu(za)
    u, v = zg[:, :D_MODEL], zg[:, D_MODEL:]
    rs = lax.rsqrt(jnp.mean(v * v, axis=-1, keepdims=True) + EPS)
    return za, u, v, rs, v * rs * gv_ref[...]


def _mix(w_ref, vals):
    vb = vals.astype(BF16)
    return jnp.concatenate(
        [jnp.dot(w_ref[g], vb[:, g * LANES:(g + 1) * LANES], preferred_element_type=F32) for g in range(A_GROUPS)],
        axis=1)


def _gating_fwd(z, ws, bias, g_v, name):
    r = z.shape[0]

    def body(z_ref, w_ref, b_ref, gv_ref, y_ref):
        _, u, _, _, vn = _gating_parts(z_ref, gv_ref)
        y_ref[...] = (u * (_mix(w_ref, vn) + b_ref[...])).astype(BF16)

    return pl.pallas_call(
        body, name=name, grid=(r // CHUNK,),
        in_specs=[pl.BlockSpec((CHUNK, 2 * D_MODEL), lambda i: (i, P_A // (2 * D_MODEL))),
                  pl.BlockSpec((A_GROUPS, CHUNK, CHUNK), lambda i: (0, 0, 0)),
                  pl.BlockSpec((CHUNK, D_MODEL), lambda i: (0, 0)),
                  pl.BlockSpec((1, D_MODEL), lambda i: (0, 0))],
        out_specs=pl.BlockSpec((CHUNK, D_MODEL), lambda i: (i, 0)),
        out_shape=jax.ShapeDtypeStruct((r, D_MODEL), BF16), compiler_params=_params())(z, ws, bias, g_v)


def _gating_bwd(z, ws, ws_t, bias, g_v, dy, name):
    r = z.shape[0]

    def body(z_ref, w_ref, wt_ref, b_ref, gv_ref, dy_ref, dz_ref, dw_ref, db_ref, dg_ref):
        i = pl.program_id(0)

        @pl.when(i == 0)
        def _():
            dw_ref[...] = jnp.zeros_like(dw_ref)
            db_ref[...] = jnp.zeros_like(db_ref)
            dg_ref[...] = jnp.zeros_like(dg_ref)

        za, u, v, rs, vn = _gating_parts(z_ref, gv_ref)
        dyv = dy_ref[...].astype(F32)
        du = dyv * (_mix(w_ref, vn) + b_ref[...])
        dmixed = dyv * u
        dvn = _mix(wt_ref, dmixed)
        dmb, vnb = dmixed.astype(BF16), vn.astype(BF16)
        for g in range(A_GROUPS):
            cols = slice(g * LANES, (g + 1) * LANES)
            dw_ref[g] += lax.dot_general(dmb[:, cols], vnb[:, cols], NT, preferred_element_type=F32)
            db_ref[:, g:g + 1] += jnp.sum(dmixed[:, cols], axis=-1, keepdims=True)
        gv = gv_ref[...]
        vh = v * rs
        dg_ref[0:1, :] += jnp.sum(dvn * vh, axis=0, keepdims=True)
        dvh = dvn * gv
        dv = rs * (dvh - vh * jnp.mean(dvh * vh, axis=-1, keepdims=True))
        _, vjp = jax.vjp(jax.nn.gelu, za)
        dz_ref[...] = vjp(jnp.concatenate([du, dv], axis=1))[0].astype(BF16)

    wspec = pl.BlockSpec((A_GROUPS, CHUNK, CHUNK), lambda i: (0, 0, 0))
    return pl.pallas_call(
        body, name=name, grid=(r // CHUNK,),
        in_specs=[pl.BlockSpec((CHUNK, 2 * D_MODEL), lambda i: (i, P_A // (2 * D_MODEL))), wspec, wspec,
                  pl.BlockSpec((CHUNK, D_MODEL), lambda i: (0, 0)), pl.BlockSpec((1, D_MODEL), lambda i: (0, 0)),
                  pl.BlockSpec((CHUNK, D_MODEL), lambda i: (i, 0))],
        out_specs=[pl.BlockSpec((CHUNK, 2 * D_MODEL), lambda i: (i, 0)), wspec,
                   pl.BlockSpec((CHUNK, A_GROUPS), lambda i: (0, 0)), pl.BlockSpec((8, D_MODEL), lambda i: (0, 0))],
        out_shape=[jax.ShapeDtypeStruct((r, 2 * D_MODEL), BF16), jax.ShapeDtypeStruct((A_GROUPS, CHUNK, CHUNK), F32),
                   jax.ShapeDtypeStruct((CHUNK, A_GROUPS), F32), jax.ShapeDtypeStruct((8, D_MODEL), F32)],
        compiler_params=_params())(z, ws, ws_t, bias, g_v, dy)


def _taps(scr, j, n_lat):
    c = CONV_CHUNK
    s = pl.multiple_of(j * c, c)
    ext = scr[pl.ds(s, c + 2 * PAD), :]
    t = s + lax.broadcasted_iota(jnp.int32, (c, 1), 0)
    xm = jnp.where(t == n_lat, 0.0, pltpu.roll(ext, 1, 0)[PAD:PAD + c])
    xp = jnp.where(t == n_lat - 1, 0.0, pltpu.roll(ext, c + 2 * PAD - 1, 0)[PAD:PAD + c])
    return s, xm, ext[PAD:PAD + c], xp


def _zero_pads(scr, r):
    scr[pl.ds(0, PAD), :] = jnp.zeros((PAD, LANES), F32)
    scr[pl.ds(PAD + r, PAD), :] = jnp.zeros((PAD, LANES), F32)


def _col(arr_cols, c0):
    return pl.BlockSpec((arr_cols, LANES), lambda c: (0, c0 + c))


def _ffn_conv_fwd(up, w, n_lat, name):
    r = up.shape[0]
    nct = D_FF // LANES
    nchunk = r // CONV_CHUNK

    def body(a_ref, g_ref, w_ref, f_ref, scr):
        _zero_pads(scr, r)

        def fill(j, _):
            s = pl.multiple_of(j * CONV_CHUNK, CONV_CHUNK)
            scr[pl.ds(PAD + s, CONV_CHUNK), :] = a_ref[pl.ds(s, CONV_CHUNK), :].astype(F32)
            return 0
        lax.fori_loop(0, nchunk, fill, 0)
        wv = w_ref[...]

        def step(j, _):
            s, xm, x0, xp = _taps(scr, j, n_lat)
            ca = wv[0:1] * xm + wv[1:2] * x0 + wv[2:3] * xp
            gv = g_ref[pl.ds(s, CONV_CHUNK), :].astype(F32)
            f_ref[pl.ds(s, CONV_CHUNK), :] = (ca * _sigmoid(ca) * gv).astype(BF16)
            return 0
        lax.fori_loop(0, nchunk, step, 0)

    return pl.pallas_call(
        body, name=name, grid=(nct,),
        in_specs=[_col(r, 0), _col(r, nct), _col(3, 0)],
        out_specs=_col(r, 0), out_shape=jax.ShapeDtypeStruct((r, D_FF), BF16),
        scratch_shapes=[pltpu.VMEM((r + 2 * PAD, LANES), F32)], compiler_params=_params())(up, up, w)


def _ffn_conv_bwd(up, w, df, n_lat, name, carry=None):
    r = up.shape[0]
    nct = D_FF // LANES
    nchunk = r // CONV_CHUNK

    def body(a_ref, g_ref, w_ref, df_ref, dup_ref, dw_ref, scr, scr2):
        _zero_pads(scr, r)
        _zero_pads(scr2, r)

        def fill(j, _):
            s = pl.multiple_of(j * CONV_CHUNK, CONV_CHUNK)
            scr[pl.ds(PAD + s, CONV_CHUNK), :] = a_ref[pl.ds(s, CONV_CHUNK), :].astype(F32)
            return 0
        lax.fori_loop(0, nchunk, fill, 0)
        wv = w_ref[...]

        def first(j, carry):
            s, xm, x0, xp = _taps(scr, j, n_lat)
            ca = wv[0:1] * xm + wv[1:2] * x0 + wv[2:3] * xp
            sg = _sigmoid(ca)
            gv = g_ref[pl.ds(s, CONV_CHUNK), :].astype(F32)
            dfv = df_ref[pl.ds(s, CONV_CHUNK), :].astype(F32)
            dup_ref[1, pl.ds(s, CONV_CHUNK), :] = (dfv * ca * sg).astype(BF16)
            dca = dfv * gv * (sg * (1.0 + ca * (1.0 - sg)))
            scr2[pl.ds(PAD + s, CONV_CHUNK), :] = dca
            return tuple(cw + jnp.sum(dca * xv, axis=0, keepdims=True) for cw, xv in zip(carry, (xm, x0, xp)))
        zero = jnp.zeros((1, LANES), F32)
        dws = lax.fori_loop(0, nchunk, first, (zero, zero, zero))
        dw_ref[...] = _rows(list(dws) + [None] * 5)

        def second(j, _):
            s, ym, y0, yp = _taps(scr2, j, n_lat)
            dup_ref[0, pl.ds(s, CONV_CHUNK), :] = (wv[0:1] * yp + wv[1:2] * y0 + wv[2:3] * ym).astype(BF16)
            return 0
        lax.fori_loop(0, nchunk, second, 0)

    outs, carried = _pcall(
        body, (up, up, w, df), name=name, grid=(nct,),
        in_specs=[_col(r, 0), _col(r, nct), _col(3, 0), _col(r, 0)],
        out_specs=[pl.BlockSpec((2, r, LANES), lambda c: (0, 0, c)), _col(8, 0)],
        out_shape=[jax.ShapeDtypeStruct((2, r, D_FF), BF16), jax.ShapeDtypeStruct((8, D_FF), F32)],
        scratch_shapes=[pltpu.VMEM((r + 2 * PAD, LANES), F32), pltpu.VMEM((r + 2 * PAD, LANES), F32)], carry=carry)
    return outs if carry is None else (outs, carried)


def _sconv_fwd(z, w, n_lat, name):
    r = z.shape[0]
    nct = D_MODEL // LANES
    nchunk = r // CONV_CHUNK
    c0 = P_B // LANES

    def body(bg_ref, cg_ref, hb_ref, w_ref, y_ref, scr):
        _zero_pads(scr, r)

        def fill(j, _):
            s = pl.multiple_of(j * CONV_CHUNK, CONV_CHUNK)
            rows = pl.ds(s, CONV_CHUNK)
            scr[pl.ds(PAD + s, CONV_CHUNK), :] = cg_ref[rows, :].astype(F32) * hb_ref[rows, :].astype(F32)
            return 0
        lax.fori_loop(0, nchunk, fill, 0)
        wv = w_ref[...]

        def step(j, _):
            s, xm, x0, xp = _taps(scr, j, n_lat)
            conv = wv[0:1] * xm + wv[1:2] * x0 + wv[2:3] * xp
            y_ref[pl.ds(s, CONV_CHUNK), :] = (bg_ref[pl.ds(s, CONV_CHUNK), :].astype(F32) * conv).astype(BF16)
            return 0
        lax.fori_loop(0, nchunk, step, 0)

    return pl.pallas_call(
        body, name=name, grid=(nct,),
        in_specs=[_col(r, c0), _col(r, c0 + nct), _col(r, c0 + 2 * nct), _col(3, 0)],
        out_specs=_col(r, 0), out_shape=jax.ShapeDtypeStruct((r, D_MODEL), BF16),
        scratch_shapes=[pltpu.VMEM((r + 2 * PAD, LANES), F32)], compiler_params=_params())(z, z, z, w)


def _sconv_bwd(z, w, dy, n_lat, name):
    r = z.shape[0]
    nct = D_MODEL // LANES
    nchunk = r // CONV_CHUNK
    c0 = P_B // LANES

    def body(bg_ref, cg_ref, hb_ref, w_ref, dy_ref, dz_ref, dw_ref, scr, scr2):
        _zero_pads(scr, r)
        _zero_pads(scr2, r)

        def fill(j, _):
            s = pl.multiple_of(j * CONV_CHUNK, CONV_CHUNK)
            rows = pl.ds(s, CONV_CHUNK)
            scr[pl.ds(PAD + s, CONV_CHUNK), :] = cg_ref[rows, :].astype(F32) * hb_ref[rows, :].astype(F32)
            return 0
        lax.fori_loop(0, nchunk, fill, 0)
        wv = w_ref[...]

        def first(j, carry):
            s, xm, x0, xp = _taps(scr, j, n_lat)
            rows = pl.ds(s, CONV_CHUNK)
            conv = wv[0:1] * xm + wv[1:2] * x0 + wv[2:3] * xp
            dyv = dy_ref[rows, :].astype(F32)
            dz_ref[0, rows, :] = (dyv * conv).astype(BF16)
            dconv = dyv * bg_ref[rows, :].astype(F32)
            scr2[pl.ds(PAD + s, CONV_CHUNK), :] = dconv
            return tuple(cw + jnp.sum(dconv * xv, axis=0, keepdims=True) for cw, xv in zip(carry, (xm, x0, xp)))
        zero = jnp.zeros((1, LANES), F32)
        dws = lax.fori_loop(0, nchunk, first, (zero, zero, zero))
        dw_ref[...] = _rows(list(dws) + [None] * 5)

        def second(j, _):
            s, ym, y0, yp = _taps(scr2, j, n_lat)
            rows = pl.ds(s, CONV_CHUNK)
            dx = wv[0:1] * yp + wv[1:2] * y0 + wv[2:3] * ym
            dz_ref[1, rows, :] = (dx * hb_ref[rows, :].astype(F32)).astype(BF16)
            dz_ref[2, rows, :] = (dx * cg_ref[rows, :].astype(F32)).astype(BF16)
            return 0
        lax.fori_loop(0, nchunk, second, 0)

    return pl.pallas_call(
        body, name=name, grid=(nct,),
        in_specs=[_col(r, c0), _col(r, c0 + nct), _col(r, c0 + 2 * nct), _col(3, 0), _col(r, 0)],
        out_specs=[pl.BlockSpec((3, r, LANES), lambda c: (0, 0, c)), _col(8, 0)],
        out_shape=[jax.ShapeDtypeStruct((3, r, D_MODEL), BF16), jax.ShapeDtypeStruct((8, D_MODEL), F32)],
        scratch_shapes=[pltpu.VMEM((r + 2 * PAD, LANES), F32), pltpu.VMEM((r + 2 * PAD, LANES), F32)],
        compiler_params=_params())(z, z, z, w, dy)


def _merge_fwd(z, b_gate, ps, name):
    r = z.shape[0]
    tm = ROW_TILE
    row = pl.BlockSpec((tm, D_MODEL), lambda i: (i, 0))

    def body(zg_ref, b_ref, p0_ref, p1_ref, p2_ref, o_ref):
        gates = _sigmoid(zg_ref[...].astype(F32) + b_ref[...])
        acc = None
        for t, p_ref in enumerate((p0_ref, p1_ref, p2_ref)):
            term = gates[:, t * D_MODEL:(t + 1) * D_MODEL] * p_ref[...].astype(F32)
            acc = term if acc is None else acc + term
        o_ref[...] = acc.astype(BF16)

    return pl.pallas_call(
        body, name=name, grid=(r // tm,),
        in_specs=[pl.BlockSpec((tm, 3 * D_MODEL), lambda i: (i, 0)), pl.BlockSpec((1, 3 * D_MODEL), lambda i: (0, 0)),
                  row, row, row],
        out_specs=row, out_shape=jax.ShapeDtypeStruct((r, D_MODEL), BF16), compiler_params=_params())(z, b_gate, *ps)


def _merge_bwd(z, b_gate, ps, dmerged, name):
    r = z.shape[0]
    tm = ROW_TILE
    row = pl.BlockSpec((tm, D_MODEL), lambda i: (i, 0))
    wide = pl.BlockSpec((tm, 3 * D_MODEL), lambda i: (i, 0))

    def body(zg_ref, b_ref, p0_ref, p1_ref, p2_ref, dm_ref, d0_ref, d1_ref, d2_ref, dz_ref, db_ref):
        @pl.when(pl.program_id(0) == 0)
        def _():
            db_ref[...] = jnp.zeros_like(db_ref)

        gates = _sigmoid(zg_ref[...].astype(F32) + b_ref[...])
        dm = dm_ref[...].astype(F32)
        for t, (p_ref, d_ref) in enumerate(((p0_ref, d0_ref), (p1_ref, d1_ref), (p2_ref, d2_ref))):
            cols = slice(t * D_MODEL, (t + 1) * D_MODEL)
            gt = gates[:, cols]
            d_ref[...] = (dm * gt).astype(BF16)
            dlogit = dm * p_ref[...].astype(F32) * gt * (1.0 - gt)
            dz_ref[:, cols] = dlogit.astype(BF16)
            db_ref[0:1, cols] += jnp.sum(dlogit, axis=0, keepdims=True)

    shp = jax.ShapeDtypeStruct((r, D_MODEL), BF16)
    return pl.pallas_call(
        body, name=name, grid=(r // tm,),
        in_specs=[wide, pl.BlockSpec((1, 3 * D_MODEL), lambda i: (0, 0)), row, row, row, row],
        out_specs=[row, row, row, wide, pl.BlockSpec((8, 3 * D_MODEL), lambda i: (0, 0))],
        out_shape=[shp, shp, shp, jax.ShapeDtypeStruct((r, 3 * D_MODEL), BF16),
                   jax.ShapeDtypeStruct((8, 3 * D_MODEL), F32)],
        compiler_params=_params())(z, b_gate, *ps, dmerged)


def _sum_slots(buf, name):
    s, rows, _ = buf.shape
    tr = _pick(rows, [512, 256, 128, 64, 32, 16, 8])

    def body(b_ref, o_ref):
        acc = b_ref[0]
        for t in range(1, s):
            acc = acc + b_ref[t]
        o_ref[...] = acc

    return pl.pallas_call(
        body, name=name, grid=(rows // tr,),
        in_specs=[pl.BlockSpec((s, tr, LANES), lambda i: (0, i, 0))],
        out_specs=pl.BlockSpec((tr, LANES), lambda i: (i, 0)),
        out_shape=jax.ShapeDtypeStruct((rows, LANES), F32), compiler_params=_params())(buf)


def _adamw(w, gsrcs, m, v, name, scale=None):
    nl, rows, cols = w.shape
    assert len(gsrcs) == nl
    s = gsrcs[0].shape[0]
    tr = _pick(rows, [128, 64, 32, 16, 8])
    blk = pl.BlockSpec((None, tr, cols), lambda l, i: (l, i, 0))
    c1 = 1.0 / (1.0 - ADAM_B1 ** ADAM_STEP)
    c2 = 1.0 / (1.0 - ADAM_B2 ** ADAM_STEP)

    def gspec(t):
        return pl.BlockSpec((s, tr, cols), lambda l, i: (0, jnp.where(l == t, i, 0), 0))

    def body(*refs):
        w_ref, g_refs, (m_ref, v_ref) = refs[0], refs[1:1 + nl], refs[1 + nl:3 + nl]
        rest = refs[3 + nl:]
        if scale is not None:
            sc_ref, rest = rest[0], rest[1:]
        go_ref, d_ref, mo_ref, vo_ref = rest
        layer = pl.program_id(0)
        g = None
        for t in range(nl):
            gt = g_refs[t][0].astype(F32)
            for q in range(1, s):
                gt = gt + g_refs[t][q].astype(F32)
            g = gt if g is None else jnp.where(layer == t, gt, g)
        if scale is not None:
            g = g * sc_ref[...]
        mn = ADAM_B1 * m_ref[...] + (1.0 - ADAM_B1) * g
        vn = ADAM_B2 * v_ref[...] + (1.0 - ADAM_B2) * (g * g)
        go_ref[...] = g
        mo_ref[...] = mn
        vo_ref[...] = vn
        d_ref[...] = -ADAM_LR * ((mn * c1) / (jnp.sqrt(vn * c2) + ADAM_EPS) + ADAM_WD * w_ref[...])

    shp = jax.ShapeDtypeStruct((nl, rows, cols), F32)
    ins = [w] + list(gsrcs) + [m, v] + ([] if scale is None else [scale])
    return pl.pallas_call(
        body, name=name, grid=(nl, rows // tr),
        in_specs=[blk] + [gspec(t) for t in range(nl)] + [blk, blk] + ([] if scale is None else [blk]),
        out_specs=[blk] * 4, out_shape=[shp] * 4, compiler_params=_params())(*ins)


def _pack(arrs):
    flat = []
    for a in arrs:
        a = a.reshape(-1).astype(F32)
        pad = (-a.shape[0]) % (8 * LANES)
        flat.append(jnp.pad(a, (0, pad)) if pad else a)
    return jnp.concatenate(flat).reshape(-1, LANES)


def _unpack(buf, shapes, lead=()):
    out, row = [], 0
    for shp in shapes:
        n = 1
        for d in shp:
            n *= d
        nrows = -(-n // (8 * LANES)) * 8
        piece = buf[..., row:row + nrows, :].reshape(lead + (nrows * LANES,))[..., :n]
        out.append(piece.reshape(lead + tuple(shp)))
        row += nrows
    return out


def _silu(x):
    return x * jax.nn.sigmoid(x)


def _rope_tables(n_lat, m_ctx):
    pos = jnp.arange(n_lat)
    row = (pos // GRID_W).astype(F32)
    col = (pos % GRID_W).astype(F32)
    half = HEAD_DIM // 2
    inv = ROPE_THETA ** (-jnp.arange(0, half, 2, dtype=F32) / half)
    ang = jnp.concatenate([row[:, None] * inv, col[:, None] * inv], axis=-1)
    cos, sin = jnp.cos(ang), jnp.sin(ang)
    cos2 = jnp.tile(jnp.concatenate([cos, cos], axis=-1), (1, LANES // HEAD_DIM))
    sin2 = jnp.tile(jnp.concatenate([-sin, sin], axis=-1), (1, LANES // HEAD_DIM))
    return (jnp.concatenate([cos2, jnp.ones((m_ctx, LANES), F32)], axis=0),
            jnp.concatenate([sin2, jnp.zeros((m_ctx, LANES), F32)], axis=0))


def _to_slots(full, axis):
    shp = full.shape
    new = shp[:axis] + (N_DEV, shp[axis] // N_DEV) + shp[axis + 1:]
    return jnp.moveaxis(full.reshape(new), axis, 0)


def _from_slots(slots, axis):
    moved = jnp.moveaxis(slots, 0, axis)
    shp = moved.shape
    return moved.reshape(shp[:axis] + (shp[axis] * shp[axis + 1],) + shp[axis + 2:])


def _permute_in(w):
    return jnp.concatenate([w[:, OFF_G:], w[:, :OFF_K], w[:, OFF_A:OFF_B], w[:, OFF_B:OFF_G], w[:, OFF_K:OFF_A]],
                           axis=1)


def kernel(x, c, ctx, c_ctx, w_mod, b_mod, g_mix, w_in, b_gate, sink, w_spatial, b_spatial, g_v, w_sconv, w_branch, w_out, g_ffn, w_up, w_fconv, w_down, g_final, loss_target, m_c_ctx, m_w_mod, m_b_mod, m_g_mix, m_w_in, m_b_gate, m_sink, m_w_spatial, m_b_spatial, m_g_v, m_w_sconv, m_w_branch, m_w_out, m_g_ffn, m_w_up, m_w_fconv, m_w_down, m_g_final, v_c_ctx, v_w_mod, v_b_mod, v_g_mix, v_w_in, v_b_gate, v_sink, v_w_spatial, v_b_spatial, v_g_v, v_w_sconv, v_w_branch, v_w_out, v_g_ffn, v_w_up, v_w_fconv, v_w_down, v_g_final):
    n_lat, m_ctx = x.shape[1], ctx.shape[1]
    r = n_lat + m_ctx
    me = 4 * lax.axis_index("x") + 2 * lax.axis_index("y") + lax.axis_index("c")
    mod_w = w_mod.shape[2]
    bm = _pick(r, [768, 256])

    shard_axis = {"in": 1, "br": 1, "out": 0, "up": 1, "dn": 0}
    shards = {}
    for kind, wt in (("in", w_in), ("br", w_branch), ("out", w_out), ("up", w_up), ("dn", w_down)):
        wb = wt.astype(BF16)
        for l in range(DEPTH):
            shards[kind, l] = wb[l]
    full = {}

    def arrive(items, got):
        for key, slots in zip(items, got):
            wfull = _from_slots(slots, shard_axis[key[0]])
            full[key] = _permute_in(wfull) if key[0] == "in" else wfull

    def gather_of(items):
        return [shards[key] for key in items], ["gather"] * len(items)

    small_shapes = [c.shape, w_sconv.shape, w_fconv.shape]
    g_in0, g_small = _exchange([shards["in", 0], _pack([c, w_sconv, w_fconv])], ["gather"] * 2, "gather_first")
    arrive([("in", 0)], [g_in0])
    c_all, sconv_all, fconv_all = _unpack(g_small, small_shapes, lead=(N_DEV,))
    c_all = c_all.reshape(N_DEV, D_MODEL)
    w_sconv_full = _from_slots(sconv_all, 2)
    w_fconv_full = _from_slots(fconv_all, 2)

    act = jnp.concatenate([_silu(c_all), _silu(c_ctx)[None], jnp.zeros((7, D_MODEL), F32)], axis=0)
    mod_part = jnp.stack([_mm(act, w_mod[l], name=f"mod_fwd{l}", bm=16, bn=mod_w, bk=D_MODEL, out_dtype=F32)
                          for l in range(DEPTH)])
    (mod_all,) = _exchange([mod_part], ["gather"], "gather_mod")
    mod_full = _from_slots(mod_all, 2) + b_mod[:, None, :]
    mods = []
    for l in range(DEPTH):
        mine = lax.dynamic_index_in_dim(mod_full[l], me, axis=0, keepdims=False).reshape(6, D_MODEL)
        theirs = mod_full[l, N_DEV].reshape(6, D_MODEL)
        mods.append(jnp.pad(jnp.stack([mine, theirs]), ((0, 0), (0, 2), (0, 0))))

    cos, sin = _rope_tables(n_lat, m_ctx)
    xs = jnp.concatenate([x[0], ctx[0]], axis=0)
    ws_b = w_spatial.astype(BF16)
    ws_t = jnp.swapaxes(w_spatial, 2, 3).astype(BF16)
    vec = lambda a: a.reshape(1, -1)

    saved = []
    res = None
    for l in range(DEPTH):
        s = {}
        if res is None:
            s["x0"] = xs
            s["h"] = _norm_fwd(xs, vec(g_mix[l]), mods[l], n_lat, 0, 1, f"norm_mix{l}")
        else:
            s["x0"], s["h"] = _norm_fwd(xs, vec(g_mix[l]), mods[l], n_lat, 0, 1, f"norm_mix{l}", res=res)
        items = [("br", l), ("out", l), ("up", l)]
        s["z"], got = _mm(s["h"], full["in", l], name=f"proj_in{l}", bm=bm, bn=_pick(IN_W, [2432, 512]), bk=D_MODEL,
                          out_dtype=BF16, carry=gather_of(items))
        arrive(items, got)
        s["qkv"] = _rope_fwd(s["z"], cos, sin, f"rope{l}")
        items = [("dn", l)] + ([("in", l + 1)] if l + 1 < DEPTH else [])
        (s["y0"], s["lse"]), got = _attn_fwd(s["qkv"], sink[l], n_lat, m_ctx, f"attn{l}", carry=gather_of(items))
        arrive(items, got)
        s["bias"] = jnp.repeat(b_spatial[l].T, LANES, axis=1)
        s["y1"] = _gating_fwd(s["z"], ws_b[l], s["bias"], vec(g_v[l]), f"gating{l}")
        s["y2"] = _sconv_fwd(s["z"], w_sconv_full[l], n_lat, f"sconv{l}")
        s["p"] = [_mm(s[f"y{t}"], full["br", l], b_lead=t, name=f"branch{l}_{t}", bm=bm, bn=D_MODEL, bk=D_MODEL,
                      out_dtype=BF16) for t in range(3)]
        s["merged"] = _merge_fwd(s["z"], vec(b_gate[l]), s["p"], f"merge{l}")
        s["o"] = _mm(s["merged"], full["out", l], name=f"proj_out{l}", bm=bm, bn=D_MODEL, bk=D_MODEL, out_dtype=F32)
        s["x1"], s["h2"] = _norm_fwd(s["x0"], vec(g_ffn[l]), mods[l], n_lat, 3, 4, f"norm_ffn{l}",
                                     res=(s["o"], mods[l], 2))
        s["up"] = _mm(s["h2"], full["up", l], name=f"ffn_up{l}", bm=bm, bn=_pick(2 * D_FF, [1408]), bk=D_MODEL,
                      out_dtype=BF16)
        s["f"] = _ffn_conv_fwd(s["up"], w_fconv_full[l], n_lat, f"ffn_conv{l}")
        s["dd"] = _mm(s["f"], full["dn", l], name=f"ffn_down{l}", bm=bm, bn=D_MODEL, bk=D_FF, out_dtype=F32)
        saved.append(s)
        xs, res = s["x1"], (s["dd"], mods[l], 5)

    top = saved[DEPTH - 1]
    dxs, d_dd, acc_final, accs_top = _loss_bwd(top["x1"], top["dd"], mods[DEPTH - 1], vec(g_final), loss_target[0],
                                               n_lat, "loss")
    loss_part = acc_final[1, 0]
    dg_final = acc_final[0]
    dmods = [None] * DEPTH
    gate2 = accs_top[:, 2]
    grads = {k: [None] * DEPTH for k in ("g_mix", "g_ffn", "g_v", "b_gate", "sink", "w_spatial", "b_spatial",
                                         "w_sconv", "w_fconv", "w_in", "w_branch", "w_out", "w_up", "w_down")}
    bk_r = _pick(r, [1408, 768, 256])
    small_names = ["g_ffn", "g_v", "b_gate", "sink", "w_spatial", "b_spatial", "w_sconv", "w_fconv"]
    recv = {}
    small_recv, small_shapes_of = {}, {}

    def small_pack(l):
        arrs = [grads[k][l] for k in small_names]
        if l > 0:
            arrs.append(grads["g_mix"][l])
        if l == DEPTH - 1:
            arrs += [loss_part.reshape(1), dg_final]
        small_shapes_of[l] = [a.shape for a in arrs]
        return _pack(arrs)

    for l in reversed(range(DEPTH)):
        s = saved[l]
        df = _mm(d_dd, full["dn", l], tb=True, name=f"d_ffn_down{l}", bm=bm, bn=_pick(D_FF, [1408]), bk=D_MODEL,
                 out_dtype=BF16)
        grads["w_down"][l] = _mm(s["f"], d_dd, ta=True, name=f"g_ffn_down{l}", bm=_pick(D_FF, [1408]), bn=D_MODEL,
                                 bk=bk_r, out_dtype=BF16)
        carry = None if l + 1 == DEPTH else ([_to_slots(grads["w_in"][l + 1], 1)], ["a2a"])
        if carry is None:
            dup, dwf = _ffn_conv_bwd(s["up"], w_fconv_full[l], df, n_lat, f"d_ffn_conv{l}")
        else:
            (dup, dwf), (recv["in", l + 1],) = _ffn_conv_bwd(s["up"], w_fconv_full[l], df, n_lat, f"d_ffn_conv{l}",
                                                            carry=carry)
        grads["w_fconv"][l] = dwf[:3]
        dh2 = _mm_nt_pieces([(dup, "step", 0, 2, 0, D_FF)], full["up", l], name=f"d_ffn_up{l}", bm=bm)
        kb = 1408
        nbh = D_FF // kb
        grads["w_up"][l] = _mm(
            s["h2"], dup, ta=True, name=f"g_ffn_up{l}", bm=D_MODEL, bn=kb, bk=bk_r, out_dtype=BF16,
            b_spec=(pl.BlockSpec((None, bk_r, kb), lambda i, j, k: (j // nbh, k, j % nbh)), 2 * D_FF))
        dx1, d_o, acc, accs = _norm_bwd(s["x1"], vec(g_ffn[l]), mods[l], dh2, dxs, n_lat, 3, 4, f"d_norm_ffn{l}",
                                        res=(s["o"], mods[l], 2))
        grads["g_ffn"][l] = acc[0]
        shift2, scale2, gate1 = accs[:, 0], accs[:, 1], accs[:, 2]
        dmerged = _mm(d_o, full["out", l], tb=True, name=f"d_proj_out{l}", bm=bm, bn=D_MODEL, bk=D_MODEL,
                      out_dtype=BF16)
        grads["w_out"][l] = _mm(s["merged"], d_o, ta=True, name=f"g_proj_out{l}", bm=D_MODEL, bn=D_MODEL, bk=bk_r,
                                out_dtype=BF16)
        dp0, dp1, dp2, dz_g, dbg = _merge_bwd(s["z"], vec(b_gate[l]), s["p"], dmerged, f"d_merge{l}")
        grads["b_gate"][l] = dbg[0]
        dps = (dp0, dp1, dp2)
        dys = [_mm(dps[t], full["br", l], tb=True, b_lead=t, name=f"d_branch{l}_{t}", bm=bm, bn=D_MODEL, bk=D_MODEL,
                   out_dtype=BF16) for t in range(3)]
        grads["w_branch"][l] = jnp.stack(
            [_mm(s[f"y{t}"], dps[t], ta=True, name=f"g_branch{l}_{t}", bm=D_MODEL, bn=D_MODEL, bk=bk_r,
                 out_dtype=BF16) for t in range(3)])
        arrs = [_to_slots(grads["w_down"][l], 0), _to_slots(grads["w_up"][l], 1), _to_slots(grads["w_out"][l], 0),
                _to_slots(grads["w_branch"][l], 1)]
        modes = ["a2a"] * 4
        if l + 1 < DEPTH:
            arrs.append(small_pack(l + 1))
            modes.append("gather")
        (dq, dk, dv, dsk), got = _attn_bwd(s["qkv"], sink[l], s["y0"], dys[0], s["lse"], n_lat, m_ctx, f"d_attn{l}",
                                           carry=(arrs, modes))
        recv["dn", l], recv["up", l], recv["out", l], recv["br", l] = got[:4]
        if l + 1 < DEPTH:
            small_recv[l + 1] = got[4]
        grads["sink"][l] = dsk[:, 0, :8].reshape(N_HEADS)
        dz_qkv = _rope_bwd(dq, dk, dv, cos, sin, f"d_rope{l}")
        dz_a, dws, dbs, dgv = _gating_bwd(s["z"], ws_b[l], ws_t[l], s["bias"], vec(g_v[l]), dys[1], f"d_gating{l}")
        grads["w_spatial"][l], grads["b_spatial"][l], grads["g_v"][l] = dws, dbs.T, dgv[0]
        dz_b, dwsc = _sconv_bwd(s["z"], w_sconv_full[l], dys[2], n_lat, f"d_sconv{l}")
        grads["w_sconv"][l] = dwsc[:3]
        kvw = 2 * N_KV_HEADS * HEAD_DIM
        pieces = [(dz_g, None, 0, 3, 0, D_MODEL), (dz_qkv, None, 0, 1, P_Q // D_MODEL, D_MODEL),
                  (dz_a, None, 0, 2, P_A // D_MODEL, D_MODEL), (dz_b, "step", 0, 3, P_B // D_MODEL, D_MODEL),
                  (dz_qkv, None, D_MODEL // kvw, 1, P_KV // kvw, kvw)]
        gw = lambda b, nm, bn, **kw: _mm(s["h"], b, ta=True, name=f"g_proj_in{l}_{nm}", bm=D_MODEL, bn=bn, bk=bk_r,
                                         out_dtype=BF16, **kw)
        gw_g = gw(dz_g, "gate", 1024)
        gw_qkv = gw(dz_qkv, "qkv", 512)
        gw_a = gw(dz_a, "gating", 1024)
        gw_b = [gw(dz_b, f"sconv{t}", 1024, b_lead=t) for t in range(3)]
        grads["w_in"][l] = jnp.concatenate([gw_qkv, gw_a] + gw_b + [gw_g], axis=1)
        if l > 0:
            dh = _mm_nt_pieces(pieces, full["in", l], name=f"d_proj_in{l}", bm=bm)
        else:
            dh, (recv["in", 0], small_recv[0]) = _mm_nt_pieces(
                pieces, full["in", l], name=f"d_proj_in{l}", bm=bm,
                carry=([_to_slots(grads["w_in"][0], 1), small_pack(0)], ["a2a", "gather"]))
        below = None if l == 0 else (saved[l - 1]["dd"], mods[l - 1], 5)
        outs = _norm_bwd(s["x0"], vec(g_mix[l]), mods[l], dh, dx1, n_lat, 0, 1, f"d_norm_mix{l}", res=below)
        if below is None:
            dxs, acc, accs = outs
        else:
            dxs, d_dd, acc, accs = outs
        grads["g_mix"][l] = acc[0]
        dmods[l] = jnp.stack([accs[:, 0], accs[:, 1], gate1, shift2, scale2, gate2], axis=1)
        gate2 = accs[:, 2]

    dmod_own = jnp.stack([dmods[l][0].reshape(-1) for l in range(DEPTH)])
    dmod_ctx = jnp.stack([dmods[l][1].reshape(-1) for l in range(DEPTH)])
    late = [grads["g_mix"][0], dmod_own + dmod_ctx, dmod_ctx, dmod_own]
    late_shapes = [a.shape for a in late]
    (late_all,) = _exchange([_pack(late)], ["gather"], "gather_late_grads")
    g_mix0_g, b_mod_g, dmodc_tot, _ = _unpack(_sum_slots(late_all, "sum_late_grads"), late_shapes)
    dmod_all = _unpack(late_all, late_shapes, lead=(N_DEV,))[-1]
    layer_sums = [_unpack(_sum_slots(small_recv[l], f"sum_small_grads{l}"), small_shapes_of[l]) for l in range(DEPTH)]
    by_name = {k: jnp.stack([layer_sums[l][t] for l in range(DEPTH)]) for t, k in enumerate(small_names)}
    g_mix_g = jnp.stack([g_mix0_g] + [layer_sums[l][len(small_names)] for l in range(1, DEPTH)])
    loss_sum, g_final_g = layer_sums[DEPTH - 1][-2], layer_sums[DEPTH - 1][-1]
    g_ffn_g, g_v_g, b_gate_g, sink_g = by_name["g_ffn"], by_name["g_v"], by_name["b_gate"], by_name["sink"]
    w_spatial_g, b_spatial_g = by_name["w_spatial"], by_name["b_spatial"]
    w_sconv_g = lax.dynamic_slice_in_dim(by_name["w_sconv"], me * w_sconv.shape[2], w_sconv.shape[2], axis=2)
    w_fconv_g = lax.dynamic_slice_in_dim(by_name["w_fconv"], me * w_fconv.shape[2], w_fconv.shape[2], axis=2)

    dmod_cols = lax.dynamic_slice_in_dim(dmod_all, me * mod_w, mod_w, axis=2)
    dmodc_cols = lax.dynamic_slice_in_dim(dmodc_tot, me * mod_w, mod_w, axis=1)
    g_w_mod, cctx_part = [], None
    for l in range(DEPTH):
        rhs = jnp.concatenate([dmod_cols[:, l], dmodc_cols[l][None], jnp.zeros((7, mod_w), F32)], axis=0)
        g_w_mod.append(_mm(act, rhs, ta=True, name=f"g_mod{l}", bm=D_MODEL, bn=mod_w, bk=16, out_dtype=F32))
        lhs = jnp.pad(dmodc_cols[l][None], ((0, 7), (0, 0)))
        part = _mm(lhs, w_mod[l], tb=True, name=f"d_cctx{l}", bm=8, bn=D_MODEL, bk=mod_w, out_dtype=F32)
        cctx_part = part if cctx_part is None else cctx_part + part
    sg = jax.nn.sigmoid(c_ctx)
    dsilu = (sg * (1.0 + c_ctx * (1.0 - sg))).reshape(8, LANES)

    (r_cctx,) = _exchange([cctx_part[0].reshape(8, LANES)], ["gather"], "gather_c_ctx_grad")

    def per_layer(a):
        return a.reshape(a.shape[0], -1, a.shape[-1])

    upd = {}
    for nm, kind, wv, mv, vv in (("w_in", "in", w_in, m_w_in, v_w_in), ("w_branch", "br", w_branch, m_w_branch, v_w_branch),
                                 ("w_out", "out", w_out, m_w_out, v_w_out), ("w_up", "up", w_up, m_w_up, v_w_up),
                                 ("w_down", "dn", w_down, m_w_down, v_w_down), ("w_mod", None, w_mod, m_w_mod, v_w_mod)):
        if kind is None:
            gsrcs = [g[None] for g in g_w_mod]
        else:
            gsrcs = [per_layer(recv[kind, l]) for l in range(DEPTH)]
        outs = _adamw(per_layer(wv), gsrcs, per_layer(mv), per_layer(vv), f"adamw_{nm}")
        upd[nm] = [o.reshape(wv.shape) for o in outs]
    as_tile = lambda a: a.reshape(1, 8, LANES)
    upd["c_ctx"] = [o.reshape(D_MODEL) for o in _adamw(
        as_tile(c_ctx), [r_cctx], as_tile(m_c_ctx), as_tile(v_c_ctx), "adamw_c_ctx", scale=as_tile(dsilu))]

    names = ["b_mod", "g_mix", "b_gate", "sink", "w_spatial", "b_spatial", "g_v", "w_sconv", "g_ffn", "w_fconv",
             "g_final"]
    w_s = [b_mod, g_mix, b_gate, sink, w_spatial, b_spatial, g_v, w_sconv, g_ffn, w_fconv, g_final]
    g_s = [b_mod_g, g_mix_g, b_gate_g, sink_g, w_spatial_g, b_spatial_g, g_v_g, w_sconv_g, g_ffn_g, w_fconv_g,
           g_final_g]
    m_s = [m_b_mod, m_g_mix, m_b_gate, m_sink, m_w_spatial, m_b_spatial, m_g_v, m_w_sconv, m_g_ffn, m_w_fconv,
           m_g_final]
    v_s = [v_b_mod, v_g_mix, v_b_gate, v_sink, v_w_spatial, v_b_spatial, v_g_v, v_w_sconv, v_g_ffn, v_w_fconv,
           v_g_final]
    shapes = [a.shape for a in w_s]
    packed = _adamw(_pack(w_s)[None], [_pack(g_s)[None]], _pack(m_s)[None], _pack(v_s)[None], "adamw_small")
    unpacked = [_unpack(o[0], shapes) for o in packed]
    for t, nm in enumerate(names):
        upd[nm] = [unpacked[q][t] for q in range(4)]

    order = ["c_ctx", "w_mod", "b_mod", "g_mix", "w_in", "b_gate", "sink", "w_spatial", "b_spatial", "g_v", "w_sconv",
             "w_branch", "w_out", "g_ffn", "w_up", "w_fconv", "w_down", "g_final"]
    result = [loss_sum.reshape(()), dxs[:n_lat][None]]
    for q in range(4):
        result += [upd[nm][q] for nm in order]
    return tuple(result)
```

```python
import jax
import jax.numpy as jnp
from jax import lax
from jax.experimental import pallas as pl
from jax.experimental.pallas import tpu as pltpu

F32, BF16 = jnp.float32, jnp.bfloat16

D_MODEL = 1024
DEPTH = 2
GRID_W = 64
N_HEADS = 16
N_KV_HEADS = 4
HEAD_DIM = 64
WINDOW = 128
BLK = 128
ROPE_THETA = 10000.0
CHUNK = 128
A_GROUPS = 8
D_FF = 2816
EPS = 1e-6
NEG = -1e30
IN_W = 9728
OFF_K, OFF_A, OFF_B, OFF_G = 1024, 1536, 3584, 6656
P_Q, P_A, P_B, P_KV = 3072, 4096, 6144, 9216

N_DEV = 8
LANES = 128
ROW_TILE = 256
CONV_CHUNK = 256
PAD = 8
VMEM_LIMIT = 52 * 1024 * 1024

ADAM_LR, ADAM_B1, ADAM_B2, ADAM_EPS, ADAM_WD, ADAM_STEP = 0.001, 0.9, 0.999, 1e-08, 0.01, 10

HBM_SPEC = pl.BlockSpec(memory_space=pltpu.HBM)
SMEM_SPEC = pl.BlockSpec(memory_space=pltpu.SMEM)


def _params():
    return pltpu.CompilerParams(vmem_limit_bytes=VMEM_LIMIT)


def _pick(n, prefs):
    for p in prefs:
        if n % p == 0:
            return p
    raise ValueError((n, prefs))


def _sigmoid(x):
    return 0.5 * jnp.tanh(0.5 * x) + 0.5


def _mm(a, b, *, name, ta=False, tb=False, bm, bn, bk, out_dtype, a_lead=None, b_lead=None, a_spec=None,
        b_spec=None, cols_outer=False, carry=None):
    ash = a.shape[1:] if a_lead is not None else a.shape
    bsh = b.shape[1:] if b_lead is not None else b.shape
    kc = (bsh[1] if tb else bsh[0]) if b_spec is None else (ash[0] if ta else ash[1])
    mo =(ash[1] if ta else ash[0]) if a_spec is None else a_spec[2]
    no = (bsh[0] if tb else bsh[1]) if b_spec is None else b_spec[2]
    assert mo % bm == 0 and no % bn == 0 and kc % bk == 0, (name, mo, no, kc, bm, bn, bk)
    nk = kc // bk

    def spec(shape, fn, idx=None):
        if idx is not None:
            shape, inner = (None,) + shape, fn
            fn = lambda i, j, k: (idx,) + inner(i, j, k)
        if cols_outer:
            return pl.BlockSpec(shape, lambda j, i, k: fn(i, j, k))
        return pl.BlockSpec(shape, fn)

    if a_spec is not None:
        a_bs = spec(a_spec[0], a_spec[1])
    elif ta:
        a_bs = spec((bk, bm), lambda i, j, k: (k, i), a_lead)
    else:
        a_bs = spec((bm, bk), lambda i, j, k: (i, k), a_lead)
    if b_spec is not None:
        b_bs = spec(b_spec[0], b_spec[1])
    elif tb:
        b_bs = spec((bn, bk), lambda i, j, k: (j, k), b_lead)
    else:
        b_bs = spec((bk, bn), lambda i, j, k: (k, j), b_lead)
    dims = (((0 if ta else 1,), (1 if tb else 0,)), ((), ()))
    grid = (no // bn, mo // bm, nk) if cols_outer else (mo // bm, no // bn, nk)

    def body(a_ref, b_ref, o_ref, *scratch):
        if nk == 1:
            o_ref[...] = lax.dot_general(a_ref[...], b_ref[...], dims, preferred_element_type=F32).astype(o_ref.dtype)
        else:
            acc = scratch[0]
            k = pl.program_id(2)

            @pl.when(k == 0)
            def _():
                acc[...] = jnp.zeros_like(acc)

            acc[...] += lax.dot_general(a_ref[...], b_ref[...], dims, preferred_element_type=F32)

            @pl.when(k == nk - 1)
            def _():
                o_ref[...] = acc[...].astype(o_ref.dtype)

    outs, carried = _pcall(
        body, (a, b), name=name, grid=grid, in_specs=[a_bs, b_bs],
        out_specs=[spec((bm, bn), lambda i, j, k: (i, j))],
        out_shape=[jax.ShapeDtypeStruct((mo, no), out_dtype)],
        scratch_shapes=[pltpu.VMEM((bm, bn), F32)] if nk > 1 else [], carry=carry)
    return outs[0] if carry is None else (outs[0], carried)


def _mm_pieces(pieces, w, *, name, bm, w_t=False, carry=None):
    kout = w.shape[1] if w_t else w.shape[0]
    starts, total = [], 0
    for piece in pieces:
        starts.append(total)
        total += piece[3]
    mo = pieces[0][0].shape[-2]
    assert mo % bm == 0
    widths = sorted({piece[5] for piece in pieces}, reverse=True)

    def inside(p, k):
        return (k >= starts[p]) & (k < starts[p] + pieces[p][3])

    def a_spec(p):
        _, lead, col0, nblk, _, bk = pieces[p]

        def fn(i, k):
            t = jnp.clip(k - starts[p], 0, nblk - 1)
            if lead is None:
                return (i, col0 + t)
            return (t, i, col0) if lead == "step" else (lead, i, col0 + t)
        return pl.BlockSpec((bm, bk) if lead is None else (None, bm, bk), fn)

    def w_spec(bk):
        def fn(i, k):
            col = 0
            for p, piece in enumerate(pieces):
                if piece[5] == bk:
                    col = col + jnp.where(inside(p, k), piece[4] + k - starts[p], 0)
            return (col, 0) if w_t else (0, col)
        return pl.BlockSpec((bk, kout) if w_t else (kout, bk), fn)

    n_p, n_w = len(pieces), len(widths)

    def body(*refs):
        a_refs, w_refs, o_ref, acc = refs[:n_p], refs[n_p:n_p + n_w], refs[n_p + n_w], refs[n_p + n_w + 1]
        k = pl.program_id(1)

        @pl.when(k == 0)
        def _():
            acc[...] = jnp.zeros_like(acc)

        for p in range(n_p):
            w_ref = w_refs[widths.index(pieces[p][5])]

            @pl.when(inside(p, k))
            def _(p=p, w_ref=w_ref):
                acc[...] += lax.dot_general(a_refs[p][...], w_ref[...], NN if w_t else NT,
                                            preferred_element_type=F32)

        @pl.when(k == total - 1)
        def _():
            o_ref[...] = acc[...]

    outs, carried = _pcall(
        body, [piece[0] for piece in pieces] + [w] * n_w, name=name, grid=(mo // bm, total),
        in_specs=[a_spec(p) for p in range(n_p)] + [w_spec(bk) for bk in widths],
        out_specs=[pl.BlockSpec((bm, kout), lambda i, k: (i, 0))],
        out_shape=[jax.ShapeDtypeStruct((mo, kout), F32)],
        scratch_shapes=[pltpu.VMEM((bm, kout), F32)], carry=carry)
    return outs[0] if carry is None else (outs[0], carried)


def _xchg_out_shapes(arrs, modes):
    return [jax.ShapeDtypeStruct((N_DEV,) + a.shape if m == "gather" else a.shape, a.dtype)
            for a, m in zip(arrs, modes)]


def _xchg_sems(n):
    return [pltpu.SemaphoreType.DMA((n, N_DEV - 1)), pltpu.SemaphoreType.DMA((n, N_DEV - 1)),
            pltpu.SemaphoreType.DMA((n,))]


def _xchg_copies(ins, outs, modes, sems, arrivals=True):
    send_sems, recv_sems, local_sems = sems
    n = len(ins)
    x, y, c = lax.axis_index("x"), lax.axis_index("y"), lax.axis_index("c")
    me = 4 * x + 2 * y + c

    def src(t, slot):
        return ins[t] if modes[t] == "gather" else ins[t].at[slot]

    local = [pltpu.make_async_copy(src(t, me), outs[t].at[me], local_sems.at[t]) for t in range(n)]
    sends, recvs = [], []
    for j in range(1, N_DEV):
        px = 1 - x if (j >> 2) & 1 else x
        py = 1 - y if (j >> 1) & 1 else y
        pc = 1 - c if j & 1 else c
        peer = 4 * px + 2 * py + pc
        for t in range(n):
            common = dict(src_ref=src(t, peer), send_sem=send_sems.at[t, j - 1], recv_sem=recv_sems.at[t, j - 1],
                          device_id=(px, py, pc), device_id_type=pl.DeviceIdType.MESH)
            sends.append(pltpu.make_async_remote_copy(dst_ref=outs[t].at[me], **common))
            if arrivals:
                recvs.append(pltpu.make_async_remote_copy(dst_ref=outs[t].at[peer], **common))
    return local, sends, recvs


def _xchg_start(copies):
    local, sends, _ = copies
    for cp in local + sends:
        cp.start()


def _xchg_wait(copies):
    local, sends, recvs = copies
    for cp in recvs:
        cp.wait_recv()
    for cp in sends:
        cp.wait_send()
    for cp in local:
        cp.wait()


def _exchange(arrs, modes, name):
    n = len(arrs)

    def body(*refs):
        copies = _xchg_copies(refs[:n], refs[n:2 * n], modes, refs[2 * n:])
        _xchg_start(copies)
        _xchg_wait(copies)

    outs = pl.pallas_call(
        body, name=name, in_specs=[HBM_SPEC] * n, out_specs=[HBM_SPEC] * n, out_shape=_xchg_out_shapes(arrs, modes),
        scratch_shapes=_xchg_sems(n), compiler_params=pltpu.CompilerParams(has_side_effects=True),
    )(*arrs)
    return list(outs)


def _pcall(body, operands, *, name, grid, in_specs, out_specs, out_shape, scratch_shapes=(), carry=None):
    out_specs, out_shape, scratch_shapes = list(out_specs), list(out_shape), list(scratch_shapes)
    if carry is None:
        outs = pl.pallas_call(body, name=name, grid=grid, in_specs=in_specs, out_specs=out_specs,
                              out_shape=out_shape, scratch_shapes=scratch_shapes, compiler_params=_params())(*operands)
        return list(outs), []
    arrs, modes = carry
    n, n_in, n_out, n_scr = len(arrs), len(in_specs), len(out_specs), len(scratch_shapes)

    def wrapped(*refs):
        ins, c_in = refs[:n_in], refs[n_in:n_in + n]
        outs, c_out = refs[n_in + n:n_in + n + n_out], refs[n_in + n + n_out:n_in + 2 * n + n_out]
        rest = refs[n_in + 2 * n + n_out:]
        scr, sems = rest[:n_scr], rest[n_scr:]
        first, last = None, None
        for d, size in enumerate(grid):
            f, e = pl.program_id(d) == 0, pl.program_id(d) == size - 1
            first = f if first is None else first & f
            last = e if last is None else last & e

        @pl.when(first)
        def _():
            _xchg_start(_xchg_copies(c_in, c_out, modes, sems, arrivals=False))

        body(*ins, *outs, *scr)

        @pl.when(last)
        def _():
            _xchg_wait(_xchg_copies(c_in, c_out, modes, sems))

    outs = pl.pallas_call(
        wrapped, name=name, grid=grid, in_specs=list(in_specs) + [HBM_SPEC] * n,
        out_specs=out_specs + [HBM_SPEC] * n, out_shape=out_shape + _xchg_out_shapes(arrs, modes),
        scratch_shapes=scratch_shapes + _xchg_sems(n), compiler_params=_params())(*operands, *arrs)
    return list(outs[:n_out]), list(outs[n_out:])


def _row_specs(r, n_lat, tm):
    nbl = n_lat // tm
    row = pl.BlockSpec((tm, D_MODEL), lambda i: (i, 0))
    mod = pl.BlockSpec((None, 8, D_MODEL), lambda i: (i // nbl, 0, 0))
    vec = pl.BlockSpec((1, D_MODEL), lambda i: (0, 0))
    return nbl, row, mod, vec


def _rows(vals):
    width = [v for v in vals if v is not None][0].shape[1]
    return jnp.concatenate([jnp.zeros((1, width), F32) if v is None else v for v in vals], axis=0)


def _norm_fwd(xs, g, mods, n_lat, sh, sc, name, res=None):
    r = xs.shape[0]
    tm = ROW_TILE
    nbl, row, mod, vec = _row_specs(r, n_lat, tm)

    def norm(x, g_ref, m_ref, h_ref):
        rs = lax.rsqrt(jnp.mean(x * x, axis=-1, keepdims=True) + EPS)
        m = m_ref[...]
        h_ref[...] = ((x * rs * g_ref[...]) * (1.0 + m[sc:sc + 1]) + m[sh:sh + 1]).astype(BF16)

    if res is None:
        def body(x_ref, g_ref, m_ref, h_ref):
            norm(x_ref[...], g_ref, m_ref, h_ref)
        return pl.pallas_call(
            body, name=name, grid=(r // tm,), in_specs=[row, vec, mod], out_specs=row,
            out_shape=jax.ShapeDtypeStruct((r, D_MODEL), BF16), compiler_params=_params())(xs, g, mods)

    o, mods_res, gt = res

    def body(x_ref, o_ref, mr_ref, g_ref, m_ref, x1_ref, h_ref):
        x = x_ref[...] + mr_ref[...][gt:gt + 1] * o_ref[...]
        x1_ref[...] = x
        norm(x, g_ref, m_ref, h_ref)

    return pl.pallas_call(
        body, name=name, grid=(r // tm,), in_specs=[row, row, mod, vec, mod], out_specs=[row, row],
        out_shape=[jax.ShapeDtypeStruct((r, D_MODEL), F32), jax.ShapeDtypeStruct((r, D_MODEL), BF16)],
        compiler_params=_params())(xs, o, mods_res, g, mods)


def _rms_bwd(x, g, dy):
    rs = lax.rsqrt(jnp.mean(x * x, axis=-1, keepdims=True) + EPS)
    xh = x * rs
    dxh = dy * g
    dx = rs * (dxh - xh * jnp.mean(dxh * xh, axis=-1, keepdims=True))
    return dx, dy * xh, xh


def _acc_specs(nbl):
    acc_all = pl.BlockSpec((8, D_MODEL), lambda i: (0, 0))
    acc_stream = pl.BlockSpec((None, 8, D_MODEL), lambda i: (i // nbl, 0, 0))
    return acc_all, acc_stream


def _loss_bwd(xs, o, mods, g_final, target, n_lat, name):
    r = xs.shape[0]
    tm = ROW_TILE
    nbl, row, mod, vec = _row_specs(r, n_lat, tm)
    acc_all, acc_stream = _acc_specs(nbl)
    tgt = pl.BlockSpec((tm, D_MODEL), lambda i: (jnp.minimum(i, nbl - 1), 0))

    def body(x_ref, o_ref, m_ref, g_ref, t_ref, dx_ref, do_ref, acc_ref, accs_ref):
        i = pl.program_id(0)
        lat = i < nbl
        gate = m_ref[...][5:6]
        o_val = o_ref[...]
        x = x_ref[...] + gate * o_val
        g = g_ref[...]
        rs = lax.rsqrt(jnp.mean(x * x, axis=-1, keepdims=True) + EPS)
        y = x * rs * g
        err = jnp.where(lat, y - t_ref[...], 0.0)
        loss = 0.5 * jnp.sum(jnp.mean(err * err, axis=-1, keepdims=True), axis=0, keepdims=True)
        dy = err * (1.0 / D_MODEL)
        dx, dg_rows, _ = _rms_bwd(x, g, dy)
        dx_ref[...] = dx
        do_ref[...] = (gate * dx).astype(BF16)

        @pl.when(i == 0)
        def _():
            acc_ref[...] = jnp.zeros_like(acc_ref)

        @pl.when((i == 0) | (i == nbl))
        def _():
            accs_ref[...] = jnp.zeros_like(accs_ref)

        acc_ref[...] += _rows([jnp.sum(dg_rows, axis=0, keepdims=True), jnp.broadcast_to(loss, (1, D_MODEL))]
                              + [None] * 6)
        accs_ref[...] += _rows([None, None, jnp.sum(dx * o_val, axis=0, keepdims=True)] + [None] * 5)

    return pl.pallas_call(
        body, name=name, grid=(r // tm,), in_specs=[row, row, mod, vec, tgt],
        out_specs=[row, row, acc_all, acc_stream],
        out_shape=[jax.ShapeDtypeStruct((r, D_MODEL), F32), jax.ShapeDtypeStruct((r, D_MODEL), BF16),
                   jax.ShapeDtypeStruct((8, D_MODEL), F32), jax.ShapeDtypeStruct((2, 8, D_MODEL), F32)],
        compiler_params=_params())(xs, o, mods, g_final, target)


def _norm_bwd(xs, g, mods, dh, dx_in, n_lat, sh, sc, name, res=None):
    r = xs.shape[0]
    tm = ROW_TILE
    nbl, row, mod, vec = _row_specs(r, n_lat, tm)
    acc_all, acc_stream = _acc_specs(nbl)
    has_res = res is not None

    def body(*refs):
        if has_res:
            x_ref, g_ref, m_ref, dh_ref, dxi_ref, o_ref, mr_ref, dx_ref, do_ref, acc_ref, accs_ref = refs
        else:
            x_ref, g_ref, m_ref, dh_ref, dxi_ref, dx_ref, acc_ref, accs_ref = refs
        i = pl.program_id(0)
        x, g, m, dhv = x_ref[...], g_ref[...], m_ref[...], dh_ref[...]
        dy = dhv * (1.0 + m[sc:sc + 1])
        dxn, dg_rows, xh = _rms_bwd(x, g, dy)
        dx = dxi_ref[...] + dxn
        dx_ref[...] = dx
        d_gate = None
        if has_res:
            o_val = o_ref[...]
            do_ref[...] = (mr_ref[...][res[2]:res[2] + 1] * dx).astype(BF16)
            d_gate = jnp.sum(dx * o_val, axis=0, keepdims=True)

        @pl.when(i == 0)
        def _():
            acc_ref[...] = jnp.zeros_like(acc_ref)

        @pl.when((i == 0) | (i == nbl))
        def _():
            accs_ref[...] = jnp.zeros_like(accs_ref)

        acc_ref[...] += _rows([jnp.sum(dg_rows, axis=0, keepdims=True)] + [None] * 7)
        accs_ref[...] += _rows([jnp.sum(dhv, axis=0, keepdims=True),
                                jnp.sum(dhv * (xh * g), axis=0, keepdims=True), d_gate] + [None] * 5)

    ins = [xs, g, mods, dh, dx_in]
    in_specs = [row, vec, mod, row, row]
    out_specs = [row]
    out_shape = [jax.ShapeDtypeStruct((r, D_MODEL), F32)]
    if has_res:
        ins += [res[0], res[1]]
        in_specs += [row, mod]
        out_specs.append(row)
        out_shape.append(jax.ShapeDtypeStruct((r, D_MODEL), BF16))
    out_specs += [acc_all, acc_stream]
    out_shape += [jax.ShapeDtypeStruct((8, D_MODEL), F32), jax.ShapeDtypeStruct((2, 8, D_MODEL), F32)]
    return pl.pallas_call(body, name=name, grid=(r // tm,), in_specs=in_specs, out_specs=out_specs,
                          out_shape=out_shape, compiler_params=_params())(*ins)


def _rotate(t, cos, sin):
    width = t.shape[1]
    reps = width // LANES
    lane = lax.broadcasted_iota(jnp.int32, (1, width), 1)
    first = (lane % HEAD_DIM) < (HEAD_DIM // 2)
    swapped = jnp.where(first, pltpu.roll(t, width - HEAD_DIM // 2, 1), pltpu.roll(t, HEAD_DIM // 2, 1))
    return t * jnp.tile(cos, (1, reps)) + swapped * jnp.tile(sin, (1, reps))


def _rope_fwd(z, cos, sin, name):
    r = z.shape[0]
    tm = ROW_TILE
    kvw = 2 * N_KV_HEADS * HEAD_DIM
    tab = pl.BlockSpec((tm, LANES), lambda i: (i, 0))

    def body(q_ref, kv_ref, c_ref, s_ref, o_ref):
        c, s = c_ref[...], s_ref[...]
        kv = kv_ref[...]
        o_ref[:, :D_MODEL] = _rotate(q_ref[...].astype(F32), c, s).astype(BF16)
        o_ref[:, D_MODEL:D_MODEL + kvw // 2] = _rotate(kv[:, :kvw // 2].astype(F32), c, s).astype(BF16)
        o_ref[:, D_MODEL + kvw // 2:] = kv[:, kvw // 2:]

    return pl.pallas_call(
        body, name=name, grid=(r // tm,),
        in_specs=[pl.BlockSpec((tm, D_MODEL), lambda i: (i, P_Q // D_MODEL)),
                  pl.BlockSpec((tm, kvw), lambda i: (i, P_KV // kvw)), tab, tab],
        out_specs=pl.BlockSpec((tm, D_MODEL + kvw), lambda i: (i, 0)),
        out_shape=jax.ShapeDtypeStruct((r, D_MODEL + kvw), BF16), compiler_params=_params())(z, z, cos, sin)


def _rope_bwd(dq, dk, dv, cos, sin, name):
    r = dq.shape[0]
    tm = ROW_TILE
    kw = N_KV_HEADS * HEAD_DIM
    tab = pl.BlockSpec((tm, LANES), lambda i: (i, 0))

    def body(dq_ref, dk_ref, dv_ref, c_ref, s_ref, o_ref):
        c, s = c_ref[...], -s_ref[...]
        o_ref[:, :D_MODEL] = _rotate(dq_ref[...], c, s).astype(BF16)
        o_ref[:, D_MODEL:D_MODEL + kw] = _rotate(dk_ref[...], c, s).astype(BF16)
        o_ref[:, D_MODEL + kw:] = dv_ref[...].astype(BF16)

    return pl.pallas_call(
        body, name=name, grid=(r // tm,),
        in_specs=[pl.BlockSpec((tm, D_MODEL), lambda i: (i, 0)), pl.BlockSpec((tm, kw), lambda i: (i, 0)),
                  pl.BlockSpec((tm, kw), lambda i: (i, 0)), tab, tab],
        out_specs=pl.BlockSpec((tm, D_MODEL + 2 * kw), lambda i: (i, 0)),
        out_shape=jax.ShapeDtypeStruct((r, D_MODEL + 2 * kw), BF16), compiler_params=_params())(dq, dk, dv, cos, sin)


def _attn_setup(i, n_lat, m_ctx, nbl, k_ref, v_ref):
    start = pl.multiple_of(jnp.clip((i - 1) * BLK, 0, n_lat - 3 * BLK), BLK)
    nkeys = 3 * BLK + m_ctx
    rows = lax.broadcasted_iota(jnp.int32, (4 * BLK, nkeys), 0)
    cols = lax.broadcasted_iota(jnp.int32, (4 * BLK, nkeys), 1)
    qpos = i * BLK + (rows & (BLK - 1))
    mask = (cols >= 3 * BLK) | ((jnp.abs(start + cols - qpos) <= WINDOW) & (i < nbl))
    kblk = jnp.concatenate([k_ref[pl.ds(start, 3 * BLK), :], k_ref[pl.ds(n_lat, m_ctx), :]], axis=0)
    vblk = jnp.concatenate([v_ref[pl.ds(start, 3 * BLK), :], v_ref[pl.ds(n_lat, m_ctx), :]], axis=0)
    lo = lax.broadcasted_iota(jnp.int32, (1, LANES), 1) < HEAD_DIM
    return start, mask, kblk, vblk, lo


def _stack_heads(ref, kh, lo):
    a = ref[:, (2 * kh) * LANES:(2 * kh + 1) * LANES]
    b = ref[:, (2 * kh + 1) * LANES:(2 * kh + 2) * LANES]
    z = jnp.zeros_like(a)
    return jnp.concatenate([jnp.where(lo, a, z), jnp.where(lo, z, a), jnp.where(lo, b, z), jnp.where(lo, z, b)],
                           axis=0)


def _kv_variants(blk, rolled, kh, lo):
    z = jnp.zeros_like(blk)
    if kh == 0:
        return jnp.where(lo, blk, rolled), jnp.where(lo, blk, z), jnp.where(lo, z, rolled)
    return jnp.where(lo, rolled, blk), jnp.where(lo, rolled, z), jnp.where(lo, z, blk)


NN = (((1,), (0,)), ((), ()))
NT = (((1,), (1,)), ((), ()))
TN = (((0,), (0,)), ((), ()))


def _scores(qst, k2, mask, sink_ref, p, kh):
    s = lax.dot_general(qst, k2, NT, preferred_element_type=F32) * (HEAD_DIM ** -0.5)
    s = jnp.where(mask, s, NEG)
    snk = jnp.concatenate([jnp.full((BLK, 1), sink_ref[p * 8 + kh * 4 + g], F32) for g in range(4)], axis=0)
    return s, snk


def _attn_fwd(qkv, sink, n_lat, m_ctx, name, carry=None):
    r = qkv.shape[0]
    nb, nbl = r // BLK, n_lat // BLK
    kcol = D_MODEL // LANES

    def body(sink_ref, q_ref, k_ref, v_ref, o_ref, lse_ref):
        p, i = pl.program_id(0), pl.program_id(1)
        _, mask, kblk, vblk, lo = _attn_setup(i, n_lat, m_ctx, nbl, k_ref, v_ref)
        kr, vr = pltpu.roll(kblk, HEAD_DIM, 1), pltpu.roll(vblk, HEAD_DIM, 1)
        for kh in range(2):
            k2, _, _ = _kv_variants(kblk, kr, kh, lo)
            _, vlo, vhi = _kv_variants(vblk, vr, kh, lo)
            qst = _stack_heads(q_ref, kh, lo)
            s, snk = _scores(qst, k2, mask, sink_ref, p, kh)
            mx = jnp.maximum(jnp.max(s, axis=-1, keepdims=True), snk)
            pe = jnp.exp(s - mx)
            den = jnp.sum(pe, axis=-1, keepdims=True) + jnp.exp(snk - mx)
            inv = 1.0 / den
            pb = pe.astype(BF16)
            for jp in range(2):
                r0 = 2 * jp * BLK
                pair = (jnp.dot(pb[r0:r0 + BLK], vlo, preferred_element_type=F32)
                        + jnp.dot(pb[r0 + BLK:r0 + 2 * BLK], vhi, preferred_element_type=F32))
                pair = pair * jnp.where(lo, inv[r0:r0 + BLK], inv[r0 + BLK:r0 + 2 * BLK])
                o_ref[:, (2 * kh + jp) * LANES:(2 * kh + jp + 1) * LANES] = pair.astype(BF16)
            lse = mx + jnp.log(den)
            for g in range(4):
                lse_ref[:, kh * 4 + g:kh * 4 + g + 1] = lse[g * BLK:(g + 1) * BLK]

    outs, carried = _pcall(
        body, (sink, qkv, qkv, qkv), name=name, grid=(2, nb),
        in_specs=[SMEM_SPEC,
                  pl.BlockSpec((BLK, 4 * LANES), lambda p, i: (i, p)),
                  pl.BlockSpec((r, LANES), lambda p, i: (0, kcol + p)),
                  pl.BlockSpec((r, LANES), lambda p, i: (0, kcol + 2 + p))],
        out_specs=[pl.BlockSpec((BLK, 4 * LANES), lambda p, i: (i, p)),
                   pl.BlockSpec((None, BLK, 8), lambda p, i: (p, i, 0))],
        out_shape=[jax.ShapeDtypeStruct((r, D_MODEL), BF16), jax.ShapeDtypeStruct((2, r, 8), F32)], carry=carry)
    return outs if carry is None else (outs, carried)


def _attn_bwd(qkv, sink, o, do, lse, n_lat, m_ctx, name, carry=None):
    r = qkv.shape[0]
    nb, nbl = r // BLK, n_lat // BLK
    kcol = D_MODEL // LANES

    def body(sink_ref, q_ref, k_ref, v_ref, o_ref, do_ref, lse_ref, dq_ref, dk_ref, dv_ref, ds_ref):
        p, i = pl.program_id(0), pl.program_id(1)

        @pl.when(i == 0)
        def _():
            dk_ref[...] = jnp.zeros_like(dk_ref)
            dv_ref[...] = jnp.zeros_like(dv_ref)
            ds_ref[...] = jnp.zeros_like(ds_ref)

        start, mask, kblk, vblk, lo = _attn_setup(i, n_lat, m_ctx, nbl, k_ref, v_ref)
        kr, vr = pltpu.roll(kblk, HEAD_DIM, 1), pltpu.roll(vblk, HEAD_DIM, 1)
        lane = lax.broadcasted_iota(jnp.int32, (1, LANES), 1)
        dks, dvs = [], []
        for kh in range(2):
            k2, klo, khi = _kv_variants(kblk, kr, kh, lo)
            v2, _, _ = _kv_variants(vblk, vr, kh, lo)
            qst = _stack_heads(q_ref, kh, lo)
            dost = _stack_heads(do_ref, kh, lo)
            s, snk = _scores(qst, k2, mask, sink_ref, p, kh)
            lse4 = jnp.concatenate([lse_ref[:, kh * 4 + g:kh * 4 + g + 1] for g in range(4)], axis=0)
            pe = jnp.exp(s - lse4)
            dp = lax.dot_general(dost, v2, NT, preferred_element_type=F32)
            deltas = []
            for jp in range(2):
                cols = slice((2 * kh + jp) * LANES, (2 * kh + jp + 1) * LANES)
                prod = do_ref[:, cols].astype(F32) * o_ref[:, cols].astype(F32)
                deltas.append(jnp.sum(jnp.where(lo, prod, 0.0), axis=-1, keepdims=True))
                deltas.append(jnp.sum(jnp.where(lo, 0.0, prod), axis=-1, keepdims=True))
            delta = jnp.concatenate(deltas, axis=0)
            dsc = pe * (dp - delta) * (HEAD_DIM ** -0.5)
            dsb, pb = dsc.astype(BF16), pe.astype(BF16)
            for jp in range(2):
                r0 = 2 * jp * BLK
                dq_ref[:, (2 * kh + jp) * LANES:(2 * kh + jp + 1) * LANES] = (
                    jnp.dot(dsb[r0:r0 + BLK], klo, preferred_element_type=F32)
                    + jnp.dot(dsb[r0 + BLK:r0 + 2 * BLK], khi, preferred_element_type=F32))
            dkf = lax.dot_general(dsb, qst, TN, preferred_element_type=F32)
            dvf = lax.dot_general(pb, dost, TN, preferred_element_type=F32)
            dks.append(dkf + pltpu.roll(dkf, HEAD_DIM, 1))
            dvs.append(dvf + pltpu.roll(dvf, HEAD_DIM, 1))
            contrib = -jnp.exp(snk - lse4) * delta
            for g in range(4):
                tot = jnp.sum(contrib[g * BLK:(g + 1) * BLK], axis=0, keepdims=True)
                ds_ref[0:1, :] += jnp.where(lane == kh * 4 + g, tot, 0.0)
        dk_blk = jnp.where(lo, dks[0], dks[1])
        dv_blk = jnp.where(lo, dvs[0], dvs[1])
        dk_ref[pl.ds(start, 3 * BLK), :] += dk_blk[:3 * BLK]
        dk_ref[pl.ds(n_lat, m_ctx), :] += dk_blk[3 * BLK:]
        dv_ref[pl.ds(start, 3 * BLK), :] += dv_blk[:3 * BLK]
        dv_ref[pl.ds(n_lat, m_ctx), :] += dv_blk[3 * BLK:]

    qspec = pl.BlockSpec((BLK, 4 * LANES), lambda p, i: (i, p))
    outs, carried = _pcall(
        body, (sink, qkv, qkv, qkv, o, do, lse), name=name, grid=(2, nb),
        in_specs=[SMEM_SPEC, qspec,
                  pl.BlockSpec((r, LANES), lambda p, i: (0, kcol + p)),
                  pl.BlockSpec((r, LANES), lambda p, i: (0, kcol + 2 + p)),
                  qspec, qspec, pl.BlockSpec((None, BLK, 8), lambda p, i: (p, i, 0))],
        out_specs=[qspec, pl.BlockSpec((r, LANES), lambda p, i: (0, p)), pl.BlockSpec((r, LANES), lambda p, i: (0, p)),
                   pl.BlockSpec((None, 8, LANES), lambda p, i: (p, 0, 0))],
        out_shape=[jax.ShapeDtypeStruct((r, D_MODEL), F32), jax.ShapeDtypeStruct((r, 2 * LANES), F32),
                   jax.ShapeDtypeStruct((r, 2 * LANES), F32), jax.ShapeDtypeStruct((2, 8, LANES), F32)], carry=carry)
    return outs if carry is None else (outs, carried)


def _gating_parts(z_ref, gv_ref):
    za = z_ref[...].astype(F32)
    zg = jax.nn.gelu(za)
    u, v = zg[:, :D_MODEL], zg[:, D_MODEL:]
    rs = lax.rsqrt(jnp.mean(v * v, axis=-1, keepdims=True) + EPS)
    return za, u, v, rs, v * rs * gv_ref[...]


def _mix(w_ref, vals):
    vb = vals.astype(BF16)
    return jnp.concatenate(
        [jnp.dot(w_ref[g], vb[:, g * LANES:(g + 1) * LANES], preferred_element_type=F32) for g in range(A_GROUPS)],
        axis=1)


def _gating_fwd(z, ws, bias, g_v, name):
    r = z.shape[0]

    def body(z_ref, w_ref, b_ref, gv_ref, y_ref):
        _, u, _, _, vn = _gating_parts(z_ref, gv_ref)
        y_ref[...] = (u * (_mix(w_ref, vn) + b_ref[...])).astype(BF16)

    return pl.pallas_call(
        body, name=name, grid=(r // CHUNK,),
        in_specs=[pl.BlockSpec((CHUNK, 2 * D_MODEL), lambda i: (i, P_A // (2 * D_MODEL))),
                  pl.BlockSpec((A_GROUPS, CHUNK, CHUNK), lambda i: (0, 0, 0)),
                  pl.BlockSpec((CHUNK, D_MODEL), lambda i: (0, 0)),
                  pl.BlockSpec((1, D_MODEL), lambda i: (0, 0))],
        out_specs=pl.BlockSpec((CHUNK, D_MODEL), lambda i: (i, 0)),
        out_shape=jax.ShapeDtypeStruct((r, D_MODEL), BF16), compiler_params=_params())(z, ws, bias, g_v)


def _gating_bwd(z, ws, ws_t, bias, g_v, dy, name):
    r = z.shape[0]

    def body(z_ref, w_ref, wt_ref, b_ref, gv_ref, dy_ref, dz_ref, dw_ref, db_ref, dg_ref):
        i = pl.program_id(0)

        @pl.when(i == 0)
        def _():
            dw_ref[...] = jnp.zeros_like(dw_ref)
            db_ref[...] = jnp.zeros_like(db_ref)
            dg_ref[...] = jnp.zeros_like(dg_ref)

        za, u, v, rs, vn = _gating_parts(z_ref, gv_ref)
        dyv = dy_ref[...].astype(F32)
        du = dyv * (_mix(w_ref, vn) + b_ref[...])
        dmixed = dyv * u
        dvn = _mix(wt_ref, dmixed)
        dmb, vnb = dmixed.astype(BF16), vn.astype(BF16)
        for g in range(A_GROUPS):
            cols = slice(g * LANES, (g + 1) * LANES)
            dw_ref[g] += lax.dot_general(dmb[:, cols], vnb[:, cols], NT, preferred_element_type=F32)
            db_ref[:, g:g + 1] += jnp.sum(dmixed[:, cols], axis=-1, keepdims=True)
        gv = gv_ref[...]
        vh = v * rs
        dg_ref[0:1, :] += jnp.sum(dvn * vh, axis=0, keepdims=True)
        dvh = dvn * gv
        dv = rs * (dvh - vh * jnp.mean(dvh * vh, axis=-1, keepdims=True))
        _, vjp = jax.vjp(jax.nn.gelu, za)
        dz_ref[...] = vjp(jnp.concatenate([du, dv], axis=1))[0].astype(BF16)

    wspec = pl.BlockSpec((A_GROUPS, CHUNK, CHUNK), lambda i: (0, 0, 0))
    return pl.pallas_call(
        body, name=name, grid=(r // CHUNK,),
        in_specs=[pl.BlockSpec((CHUNK, 2 * D_MODEL), lambda i: (i, P_A // (2 * D_MODEL))), wspec, wspec,
                  pl.BlockSpec((CHUNK, D_MODEL), lambda i: (0, 0)), pl.BlockSpec((1, D_MODEL), lambda i: (0, 0)),
                  pl.BlockSpec((CHUNK, D_MODEL), lambda i: (i, 0))],
        out_specs=[pl.BlockSpec((CHUNK, 2 * D_MODEL), lambda i: (i, 0)), wspec,
                   pl.BlockSpec((CHUNK, A_GROUPS), lambda i: (0, 0)), pl.BlockSpec((8, D_MODEL), lambda i: (0, 0))],
        out_shape=[jax.ShapeDtypeStruct((r, 2 * D_MODEL), BF16), jax.ShapeDtypeStruct((A_GROUPS, CHUNK, CHUNK), F32),
                   jax.ShapeDtypeStruct((CHUNK, A_GROUPS), F32), jax.ShapeDtypeStruct((8, D_MODEL), F32)],
        compiler_params=_params())(z, ws, ws_t, bias, g_v, dy)


def _scr_rows(j, n_lat):
    s = pl.multiple_of(j * CONV_CHUNK, CONV_CHUNK)
    shift = jnp.where(j >= n_lat // CONV_CHUNK, 2 * PAD, PAD)
    return s, pl.multiple_of(s + shift, PAD)


def _taps(scr, j, n_lat):
    c = CONV_CHUNK
    s, at = _scr_rows(j, n_lat)
    ext = scr[pl.ds(pl.multiple_of(at - PAD, PAD), c + 2 * PAD), :]
    xm = pltpu.roll(ext, 1, 0)[PAD:PAD + c]
    xp = pltpu.roll(ext, c + 2 * PAD - 1, 0)[PAD:PAD + c]
    return s, xm, ext[PAD:PAD + c], xp


def _zero_pads(scr, r, n_lat):
    for at in (0, PAD + n_lat, 2 * PAD + r):
        scr[pl.ds(at, PAD), :] = jnp.zeros((PAD, LANES), F32)


def _col(arr_cols, c0):
    return pl.BlockSpec((arr_cols, LANES), lambda c: (0, c0 + c))


def _ffn_conv_fwd(up, w, n_lat, name):
    r = up.shape[0]
    nct = D_FF // LANES
    nchunk = r // CONV_CHUNK

    def body(a_ref, g_ref, w_ref, f_ref, scr):
        _zero_pads(scr, r, n_lat)

        def fill(j, _):
            s, at = _scr_rows(j, n_lat)
            scr[pl.ds(at, CONV_CHUNK), :] = a_ref[pl.ds(s, CONV_CHUNK), :].astype(F32)
            return 0
        lax.fori_loop(0, nchunk, fill, 0)
        wv = w_ref[...]

        def step(j, _):
            s, xm, x0, xp = _taps(scr, j, n_lat)
            ca = wv[0:1] * xm + wv[1:2] * x0 + wv[2:3] * xp
            gv = g_ref[pl.ds(s, CONV_CHUNK), :].astype(F32)
            f_ref[pl.ds(s, CONV_CHUNK), :] = (ca * _sigmoid(ca) * gv).astype(BF16)
            return 0
        lax.fori_loop(0, nchunk, step, 0)

    return pl.pallas_call(
        body, name=name, grid=(nct,),
        in_specs=[_col(r, 0), _col(r, nct), _col(3, 0)],
        out_specs=_col(r, 0), out_shape=jax.ShapeDtypeStruct((r, D_FF), BF16),
        scratch_shapes=[pltpu.VMEM((r + 3 * PAD, LANES), F32)], compiler_params=_params())(up, up, w)


def _ffn_conv_bwd(up, w, df, n_lat, name, carry=None):
    r = up.shape[0]
    nct = D_FF // LANES
    nchunk = r // CONV_CHUNK

    def body(a_ref, g_ref, w_ref, df_ref, dup_ref, dw_ref, scr, scr2):
        _zero_pads(scr, r, n_lat)
        _zero_pads(scr2, r, n_lat)

        def fill(j, _):
            s, at = _scr_rows(j, n_lat)
            scr[pl.ds(at, CONV_CHUNK), :] = a_ref[pl.ds(s, CONV_CHUNK), :].astype(F32)
            return 0
        lax.fori_loop(0, nchunk, fill, 0)
        wv = w_ref[...]

        def first(j, carry):
            s, xm, x0, xp = _taps(scr, j, n_lat)
            ca = wv[0:1] * xm + wv[1:2] * x0 + wv[2:3] * xp
            sg = _sigmoid(ca)
            gv = g_ref[pl.ds(s, CONV_CHUNK), :].astype(F32)
            dfv = df_ref[pl.ds(s, CONV_CHUNK), :].astype(F32)
            dup_ref[1, pl.ds(s, CONV_CHUNK), :] = (dfv * ca * sg).astype(BF16)
            dca = dfv * gv * (sg * (1.0 + ca * (1.0 - sg)))
            scr2[pl.ds(_scr_rows(j, n_lat)[1], CONV_CHUNK), :] = dca
            return tuple(cw + jnp.sum(dca * xv, axis=0, keepdims=True) for cw, xv in zip(carry, (xm, x0, xp)))
        zero = jnp.zeros((1, LANES), F32)
        dws = lax.fori_loop(0, nchunk, first, (zero, zero, zero))
        dw_ref[...] = _rows(list(dws) + [None] * 5)

        def second(j, _):
            s, ym, y0, yp = _taps(scr2, j, n_lat)
            dup_ref[0, pl.ds(s, CONV_CHUNK), :] = (wv[0:1] * yp + wv[1:2] * y0 + wv[2:3] * ym).astype(BF16)
            return 0
        lax.fori_loop(0, nchunk, second, 0)

    outs, carried = _pcall(
        body, (up, up, w, df), name=name, grid=(nct,),
        in_specs=[_col(r, 0), _col(r, nct), _col(3, 0), _col(r, 0)],
        out_specs=[pl.BlockSpec((2, r, LANES), lambda c: (0, 0, c)), _col(8, 0)],
        out_shape=[jax.ShapeDtypeStruct((2, r, D_FF), BF16), jax.ShapeDtypeStruct((8, D_FF), F32)],
        scratch_shapes=[pltpu.VMEM((r + 3 * PAD, LANES), F32), pltpu.VMEM((r + 3 * PAD, LANES), F32)], carry=carry)
    return outs if carry is None else (outs, carried)


def _sconv_fwd(z, w, n_lat, name):
    r = z.shape[0]
    nct = D_MODEL // LANES
    nchunk = r // CONV_CHUNK
    c0 = P_B // LANES

    def body(bg_ref, cg_ref, hb_ref, w_ref, y_ref, scr):
        _zero_pads(scr, r, n_lat)

        def fill(j, _):
            s, at = _scr_rows(j, n_lat)
            rows = pl.ds(s, CONV_CHUNK)
            scr[pl.ds(at, CONV_CHUNK), :] = cg_ref[rows, :].astype(F32) * hb_ref[rows, :].astype(F32)
            return 0
        lax.fori_loop(0, nchunk, fill, 0)
        wv = w_ref[...]

        def step(j, _):
            s, xm, x0, xp = _taps(scr, j, n_lat)
            conv = wv[0:1] * xm + wv[1:2] * x0 + wv[2:3] * xp
            y_ref[pl.ds(s, CONV_CHUNK), :] = (bg_ref[pl.ds(s, CONV_CHUNK), :].astype(F32) * conv).astype(BF16)
            return 0
        lax.fori_loop(0, nchunk, step, 0)

    return pl.pallas_call(
        body, name=name, grid=(nct,),
        in_specs=[_col(r, c0), _col(r, c0 + nct), _col(r, c0 + 2 * nct), _col(3, 0)],
        out_specs=_col(r, 0), out_shape=jax.ShapeDtypeStruct((r, D_MODEL), BF16),
        scratch_shapes=[pltpu.VMEM((r + 3 * PAD, LANES), F32)], compiler_params=_params())(z, z, z, w)


def _sconv_bwd(z, w, dy, n_lat, name):
    r = z.shape[0]
    nct = D_MODEL // LANES
    nchunk = r // CONV_CHUNK
    c0 = P_B // LANES

    def body(bg_ref, cg_ref, hb_ref, w_ref, dy_ref, dz_ref, dw_ref, scr, scr2):
        _zero_pads(scr, r, n_lat)
        _zero_pads(scr2, r, n_lat)

        def fill(j, _):
            s, at = _scr_rows(j, n_lat)
            rows = pl.ds(s, CONV_CHUNK)
            scr[pl.ds(at, CONV_CHUNK), :] = cg_ref[rows, :].astype(F32) * hb_ref[rows, :].astype(F32)
            return 0
        lax.fori_loop(0, nchunk, fill, 0)
        wv = w_ref[...]

        def first(j, carry):
            s, xm, x0, xp = _taps(scr, j, n_lat)
            rows = pl.ds(s, CONV_CHUNK)
            conv = wv[0:1] * xm + wv[1:2] * x0 + wv[2:3] * xp
            dyv = dy_ref[rows, :].astype(F32)
            dz_ref[0, rows, :] = (dyv * conv).astype(BF16)
            dconv = dyv * bg_ref[rows, :].astype(F32)
            scr2[pl.ds(_scr_rows(j, n_lat)[1], CONV_CHUNK), :] = dconv
            return tuple(cw + jnp.sum(dconv * xv, axis=0, keepdims=True) for cw, xv in zip(carry, (xm, x0, xp)))
        zero = jnp.zeros((1, LANES), F32)
        dws = lax.fori_loop(0, nchunk, first, (zero, zero, zero))
        dw_ref[...] = _rows(list(dws) + [None] * 5)

        def second(j, _):
            s, ym, y0, yp = _taps(scr2, j, n_lat)
            rows = pl.ds(s, CONV_CHUNK)
            dx = wv[0:1] * yp + wv[1:2] * y0 + wv[2:3] * ym
            dz_ref[1, rows, :] = (dx * hb_ref[rows, :].astype(F32)).astype(BF16)
            dz_ref[2, rows, :] = (dx * cg_ref[rows, :].astype(F32)).astype(BF16)
            return 0
        lax.fori_loop(0, nchunk, second, 0)

    return pl.pallas_call(
        body, name=name, grid=(nct,),
        in_specs=[_col(r, c0), _col(r, c0 + nct), _col(r, c0 + 2 * nct), _col(3, 0), _col(r, 0)],
        out_specs=[pl.BlockSpec((3, r, LANES), lambda c: (0, 0, c)), _col(8, 0)],
        out_shape=[jax.ShapeDtypeStruct((3, r, D_MODEL), BF16), jax.ShapeDtypeStruct((8, D_MODEL), F32)],
        scratch_shapes=[pltpu.VMEM((r + 3 * PAD, LANES), F32), pltpu.VMEM((r + 3 * PAD, LANES), F32)],
        compiler_params=_params())(z, z, z, w, dy)


def _merge_fwd(z, b_gate, ps, name):
    r = z.shape[0]
    tm = ROW_TILE
    row = pl.BlockSpec((tm, D_MODEL), lambda i: (i, 0))

    def body(zg_ref, b_ref, p0_ref, p1_ref, p2_ref, o_ref):
        gates = _sigmoid(zg_ref[...].astype(F32) + b_ref[...])
        acc = None
        for t, p_ref in enumerate((p0_ref, p1_ref, p2_ref)):
            term = gates[:, t * D_MODEL:(t + 1) * D_MODEL] * p_ref[...].astype(F32)
            acc = term if acc is None else acc + term
        o_ref[...] = acc.astype(BF16)

    return pl.pallas_call(
        body, name=name, grid=(r // tm,),
        in_specs=[pl.BlockSpec((tm, 3 * D_MODEL), lambda i: (i, 0)), pl.BlockSpec((1, 3 * D_MODEL), lambda i: (0, 0)),
                  row, row, row],
        out_specs=row, out_shape=jax.ShapeDtypeStruct((r, D_MODEL), BF16), compiler_params=_params())(z, b_gate, *ps)


def _merge_bwd(z, b_gate, ps, dmerged, name):
    r = z.shape[0]
    tm = ROW_TILE
    row = pl.BlockSpec((tm, D_MODEL), lambda i: (i, 0))
    wide = pl.BlockSpec((tm, 3 * D_MODEL), lambda i: (i, 0))

    def body(zg_ref, b_ref, p0_ref, p1_ref, p2_ref, dm_ref, d0_ref, d1_ref, d2_ref, dz_ref, db_ref):
        @pl.when(pl.program_id(0) == 0)
        def _():
            db_ref[...] = jnp.zeros_like(db_ref)

        gates = _sigmoid(zg_ref[...].astype(F32) + b_ref[...])
        dm = dm_ref[...].astype(F32)
        for t, (p_ref, d_ref) in enumerate(((p0_ref, d0_ref), (p1_ref, d1_ref), (p2_ref, d2_ref))):
            cols = slice(t * D_MODEL, (t + 1) * D_MODEL)
            gt = gates[:, cols]
            d_ref[...] = (dm * gt).astype(BF16)
            dlogit = dm * p_ref[...].astype(F32) * gt * (1.0 - gt)
            dz_ref[:, cols] = dlogit.astype(BF16)
            db_ref[0:1, cols] += jnp.sum(dlogit, axis=0, keepdims=True)

    shp = jax.ShapeDtypeStruct((r, D_MODEL), BF16)
    return pl.pallas_call(
        body, name=name, grid=(r // tm,),
        in_specs=[wide, pl.BlockSpec((1, 3 * D_MODEL), lambda i: (0, 0)), row, row, row, row],
        out_specs=[row, row, row, wide, pl.BlockSpec((8, 3 * D_MODEL), lambda i: (0, 0))],
        out_shape=[shp, shp, shp, jax.ShapeDtypeStruct((r, 3 * D_MODEL), BF16),
                   jax.ShapeDtypeStruct((8, 3 * D_MODEL), F32)],
        compiler_params=_params())(z, b_gate, *ps, dmerged)


def _sum_slots(buf, name):
    s, rows, _ = buf.shape
    tr = _pick(rows, [512, 256, 128, 64, 32, 16, 8])

    def body(b_ref, o_ref):
        acc = b_ref[0]
        for t in range(1, s):
            acc = acc + b_ref[t]
        o_ref[...] = acc

    return pl.pallas_call(
        body, name=name, grid=(rows // tr,),
        in_specs=[pl.BlockSpec((s, tr, LANES), lambda i: (0, i, 0))],
        out_specs=pl.BlockSpec((tr, LANES), lambda i: (i, 0)),
        out_shape=jax.ShapeDtypeStruct((rows, LANES), F32), compiler_params=_params())(buf)


def _adamw(w, gsrcs, m, v, name, scale=None):
    nl, rows, cols = w.shape
    assert len(gsrcs) == nl
    s = gsrcs[0].shape[0]
    tr = _pick(rows, [128, 64, 32, 16, 8])
    blk = pl.BlockSpec((None, tr, cols), lambda l, i: (l, i, 0))
    c1 = 1.0 / (1.0 - ADAM_B1 ** ADAM_STEP)
    c2 = 1.0 / (1.0 - ADAM_B2 ** ADAM_STEP)

    def gspec(t):
        return pl.BlockSpec((s, tr, cols), lambda l, i: (0, jnp.where(l == t, i, 0), 0))

    def body(*refs):
        w_ref, g_refs, (m_ref, v_ref) = refs[0], refs[1:1 + nl], refs[1 + nl:3 + nl]
        rest = refs[3 + nl:]
        if scale is not None:
            sc_ref, rest = rest[0], rest[1:]
        go_ref, d_ref, mo_ref, vo_ref = rest
        layer = pl.program_id(0)
        g = None
        for t in range(nl):
            gt = g_refs[t][0].astype(F32)
            for q in range(1, s):
                gt = gt + g_refs[t][q].astype(F32)
            g = gt if g is None else jnp.where(layer == t, gt, g)
        if scale is not None:
            g = g * sc_ref[...]
        mn = ADAM_B1 * m_ref[...] + (1.0 - ADAM_B1) * g
        vn = ADAM_B2 * v_ref[...] + (1.0 - ADAM_B2) * (g * g)
        go_ref[...] = g
        mo_ref[...] = mn
        vo_ref[...] = vn
        d_ref[...] = -ADAM_LR * ((mn * c1) / (jnp.sqrt(vn * c2) + ADAM_EPS) + ADAM_WD * w_ref[...])

    shp = jax.ShapeDtypeStruct((nl, rows, cols), F32)
    ins = [w] + list(gsrcs) + [m, v] + ([] if scale is None else [scale])
    return pl.pallas_call(
        body, name=name, grid=(nl, rows // tr),
        in_specs=[blk] + [gspec(t) for t in range(nl)] + [blk, blk] + ([] if scale is None else [blk]),
        out_specs=[blk] * 4, out_shape=[shp] * 4, compiler_params=_params())(*ins)


def _pack(arrs):
    flat = []
    for a in arrs:
        a = a.reshape(-1).astype(F32)
        pad = (-a.shape[0]) % (8 * LANES)
        flat.append(jnp.pad(a, (0, pad)) if pad else a)
    return jnp.concatenate(flat).reshape(-1, LANES)


def _unpack(buf, shapes, lead=()):
    out, row = [], 0
    for shp in shapes:
        n = 1
        for d in shp:
            n *= d
        nrows = -(-n // (8 * LANES)) * 8
        piece = buf[..., row:row + nrows, :].reshape(lead + (nrows * LANES,))[..., :n]
        out.append(piece.reshape(lead + tuple(shp)))
        row += nrows
    return out


def _silu(x):
    return x * jax.nn.sigmoid(x)


def _rope_tables(n_lat, m_ctx):
    pos = jnp.arange(n_lat)
    row = (pos // GRID_W).astype(F32)
    col = (pos % GRID_W).astype(F32)
    half = HEAD_DIM // 2
    inv = ROPE_THETA ** (-jnp.arange(0, half, 2, dtype=F32) / half)
    ang = jnp.concatenate([row[:, None] * inv, col[:, None] * inv], axis=-1)
    cos, sin = jnp.cos(ang), jnp.sin(ang)
    cos2 = jnp.tile(jnp.concatenate([cos, cos], axis=-1), (1, LANES // HEAD_DIM))
    sin2 = jnp.tile(jnp.concatenate([-sin, sin], axis=-1), (1, LANES // HEAD_DIM))
    return (jnp.concatenate([cos2, jnp.ones((m_ctx, LANES), F32)], axis=0),
            jnp.concatenate([sin2, jnp.zeros((m_ctx, LANES), F32)], axis=0))


def _to_slots(full, axis):
    shp = full.shape
    new = shp[:axis] + (N_DEV, shp[axis] // N_DEV) + shp[axis + 1:]
    return jnp.moveaxis(full.reshape(new), axis, 0)


def _from_slots(slots, axis):
    moved = jnp.moveaxis(slots, 0, axis)
    shp = moved.shape
    return moved.reshape(shp[:axis] + (shp[axis] * shp[axis + 1],) + shp[axis + 2:])


def _permute_in(wt):
    return jnp.concatenate([wt[OFF_G:], wt[:OFF_K], wt[OFF_A:OFF_B], wt[OFF_B:OFF_G], wt[OFF_K:OFF_A]], axis=0)


def kernel(x, c, ctx, c_ctx, w_mod, b_mod, g_mix, w_in, b_gate, sink, w_spatial, b_spatial, g_v, w_sconv, w_branch, w_out, g_ffn, w_up, w_fconv, w_down, g_final, loss_target, m_c_ctx, m_w_mod, m_b_mod, m_g_mix, m_w_in, m_b_gate, m_sink, m_w_spatial, m_b_spatial, m_g_v, m_w_sconv, m_w_branch, m_w_out, m_g_ffn, m_w_up, m_w_fconv, m_w_down, m_g_final, v_c_ctx, v_w_mod, v_b_mod, v_g_mix, v_w_in, v_b_gate, v_sink, v_w_spatial, v_b_spatial, v_g_v, v_w_sconv, v_w_branch, v_w_out, v_g_ffn, v_w_up, v_w_fconv, v_w_down, v_g_final):
    n_lat, m_ctx = x.shape[1], ctx.shape[1]
    r = n_lat + m_ctx
    me = 4 * lax.axis_index("x") + 2 * lax.axis_index("y") + lax.axis_index("c")
    mod_w = w_mod.shape[2]
    bm = _pick(r, [768, 256])

    tr = lambda a: jnp.swapaxes(a, -1, -2)
    shard_axis = {"in": 0, "br": 1, "out": 0, "up": 0, "dn": 0}
    shards = {}
    for kind, wt in (("in", tr(w_in)), ("br", w_branch), ("out", w_out), ("up", tr(w_up)), ("dn", w_down)):
        wb = wt.astype(BF16)
        for l in range(DEPTH):
            shards[kind, l] = wb[l]
    full = {}

    def arrive(items, got):
        for key, slots in zip(items, got):
            wfull = _from_slots(slots, shard_axis[key[0]])
            full[key] = _permute_in(wfull) if key[0] == "in" else wfull

    def gather_of(items):
        return [shards[key] for key in items], ["gather"] * len(items)

    small_shapes = [c.shape, w_sconv.shape, w_fconv.shape]
    g_in0, g_small = _exchange([shards["in", 0], _pack([c, w_sconv, w_fconv])], ["gather"] * 2, "gather_first")
    arrive([("in", 0)], [g_in0])
    c_all, sconv_all, fconv_all = _unpack(g_small, small_shapes, lead=(N_DEV,))
    c_all = c_all.reshape(N_DEV, D_MODEL)
    w_sconv_full = _from_slots(sconv_all, 2)
    w_fconv_full = _from_slots(fconv_all, 2)

    act = jnp.concatenate([_silu(c_all), _silu(c_ctx)[None], jnp.zeros((7, D_MODEL), F32)], axis=0)
    mod_part = jnp.stack([_mm(act, w_mod[l], name=f"mod_fwd{l}", bm=16, bn=mod_w, bk=D_MODEL, out_dtype=F32)
                          for l in range(DEPTH)])
    (mod_all,) = _exchange([mod_part], ["gather"], "gather_mod")
    mod_full = _from_slots(mod_all, 2) + b_mod[:, None, :]
    mods = []
    for l in range(DEPTH):
        mine = lax.dynamic_index_in_dim(mod_full[l], me, axis=0, keepdims=False).reshape(6, D_MODEL)
        theirs = mod_full[l, N_DEV].reshape(6, D_MODEL)
        mods.append(jnp.pad(jnp.stack([mine, theirs]), ((0, 0), (0, 2), (0, 0))))

    cos, sin = _rope_tables(n_lat, m_ctx)
    xs = jnp.concatenate([x[0], ctx[0]], axis=0)
    ws_b = w_spatial.astype(BF16)
    ws_t = jnp.swapaxes(w_spatial, 2, 3).astype(BF16)
    vec = lambda a: a.reshape(1, -1)

    saved = []
    res = None
    for l in range(DEPTH):
        s = {}
        if res is None:
            s["x0"] = xs
            s["h"] = _norm_fwd(xs, vec(g_mix[l]), mods[l], n_lat, 0, 1, f"norm_mix{l}")
        else:
            s["x0"], s["h"] = _norm_fwd(xs, vec(g_mix[l]), mods[l], n_lat, 0, 1, f"norm_mix{l}", res=res)
        items = [("br", l), ("out", l), ("up", l)]
        s["z"], got = _mm(s["h"], full["in", l], tb=True, name=f"proj_in{l}", bm=bm, bn=_pick(IN_W, [2432, 512]),
                          bk=D_MODEL, out_dtype=BF16, cols_outer=True, carry=gather_of(items))
        arrive(items, got)
        s["qkv"] = _rope_fwd(s["z"], cos, sin, f"rope{l}")
        items = [("dn", l)] + ([("in", l + 1)] if l + 1 < DEPTH else [])
        (s["y0"], s["lse"]), got = _attn_fwd(s["qkv"], sink[l], n_lat, m_ctx, f"attn{l}", carry=gather_of(items))
        arrive(items, got)
        s["bias"] = jnp.repeat(b_spatial[l].T, LANES, axis=1)
        s["y1"] = _gating_fwd(s["z"], ws_b[l], s["bias"], vec(g_v[l]), f"gating{l}")
        s["y2"] = _sconv_fwd(s["z"], w_sconv_full[l], n_lat, f"sconv{l}")
        s["p"] = [_mm(s[f"y{t}"], full["br", l], b_lead=t, name=f"branch{l}_{t}", bm=bm, bn=D_MODEL, bk=D_MODEL,
                      out_dtype=BF16) for t in range(3)]
        s["merged"] = _merge_fwd(s["z"], vec(b_gate[l]), s["p"], f"merge{l}")
        s["o"] = _mm(s["merged"], full["out", l], name=f"proj_out{l}", bm=bm, bn=D_MODEL, bk=D_MODEL, out_dtype=F32)
        s["x1"], s["h2"] = _norm_fwd(s["x0"], vec(g_ffn[l]), mods[l], n_lat, 3, 4, f"norm_ffn{l}",
                                     res=(s["o"], mods[l], 2))
        s["up"] = _mm(s["h2"], full["up", l], tb=True, name=f"ffn_up{l}", bm=bm, bn=_pick(2 * D_FF, [1408]),
                      bk=D_MODEL, out_dtype=BF16, cols_outer=True)
        s["f"] = _ffn_conv_fwd(s["up"], w_fconv_full[l], n_lat, f"ffn_conv{l}")
        s["dd"] = _mm(s["f"], full["dn", l], name=f"ffn_down{l}", bm=bm, bn=D_MODEL, bk=D_FF, out_dtype=F32)
        saved.append(s)
        xs, res = s["x1"], (s["dd"], mods[l], 5)

    top = saved[DEPTH - 1]
    dxs, d_dd, acc_final, accs_top = _loss_bwd(top["x1"], top["dd"], mods[DEPTH - 1], vec(g_final), loss_target[0],
                                               n_lat, "loss")
    loss_part = acc_final[1, 0]
    dg_final = acc_final[0]
    dmods = [None] * DEPTH
    gate2 = accs_top[:, 2]
    grads = {k: [None] * DEPTH for k in ("g_mix", "g_ffn", "g_v", "b_gate", "sink", "w_spatial", "b_spatial",
                                         "w_sconv", "w_fconv", "w_in", "w_branch", "w_out", "w_up", "w_down")}
    bk_r = _pick(r, [1408, 768, 256])
    small_names = ["g_ffn", "g_v", "b_gate", "sink", "w_spatial", "b_spatial", "w_sconv", "w_fconv"]
    recv = {}
    small_recv, small_shapes_of = {}, {}

    def small_pack(l):
        arrs = [grads[k][l] for k in small_names]
        if l > 0:
            arrs.append(grads["g_mix"][l])
        if l == DEPTH - 1:
            arrs += [loss_part.reshape(1), dg_final]
        small_shapes_of[l] = [a.shape for a in arrs]
        return _pack(arrs)

    for l in reversed(range(DEPTH)):
        s = saved[l]
        df = _mm(d_dd, full["dn", l], tb=True, name=f"d_ffn_down{l}", bm=bm, bn=_pick(D_FF, [1408]), bk=D_MODEL,
                 out_dtype=BF16)
        grads["w_down"][l] = _mm(s["f"], d_dd, ta=True, name=f"g_ffn_down{l}", bm=_pick(D_FF, [1408]), bn=D_MODEL,
                                 bk=bk_r, out_dtype=BF16)
        carry = None if l + 1 == DEPTH else ([_to_slots(grads["w_in"][l + 1], 0)], ["a2a"])
        if carry is None:
            dup, dwf = _ffn_conv_bwd(s["up"], w_fconv_full[l], df, n_lat, f"d_ffn_conv{l}")
        else:
            (dup, dwf), (recv["in", l + 1],) = _ffn_conv_bwd(s["up"], w_fconv_full[l], df, n_lat, f"d_ffn_conv{l}",
                                                            carry=carry)
        grads["w_fconv"][l] = dwf[:3]
        dh2 = _mm_pieces([(dup, "step", 0, 2, 0, D_FF)], full["up", l], w_t=True, name=f"d_ffn_up{l}", bm=bm)
        kb = 1408
        nbh = D_FF // kb
        grads["w_up"][l] = _mm(
            dup, s["h2"], ta=True, name=f"g_ffn_up{l}", bm=kb, bn=D_MODEL, bk=bk_r, out_dtype=BF16,
            a_spec=((None, bk_r, kb), lambda i, j, k: (i // nbh, k, i % nbh), 2 * D_FF))
        dx1, d_o, acc, accs = _norm_bwd(s["x1"], vec(g_ffn[l]), mods[l], dh2, dxs, n_lat, 3, 4, f"d_norm_ffn{l}",
                                        res=(s["o"], mods[l], 2))
        grads["g_ffn"][l] = acc[0]
        shift2, scale2, gate1 = accs[:, 0], accs[:, 1], accs[:, 2]
        dmerged = _mm(d_o, full["out", l], tb=True, name=f"d_proj_out{l}", bm=bm, bn=D_MODEL, bk=D_MODEL,
                      out_dtype=BF16)
        grads["w_out"][l] = _mm(s["merged"], d_o, ta=True, name=f"g_proj_out{l}", bm=D_MODEL, bn=D_MODEL, bk=bk_r,
                                out_dtype=BF16)
        dp0, dp1, dp2, dz_g, dbg = _merge_bwd(s["z"], vec(b_gate[l]), s["p"], dmerged, f"d_merge{l}")
        grads["b_gate"][l] = dbg[0]
        dps = (dp0, dp1, dp2)
        dys = [_mm(dps[t], full["br", l], tb=True, b_lead=t, name=f"d_branch{l}_{t}", bm=bm, bn=D_MODEL, bk=D_MODEL,
                   out_dtype=BF16) for t in range(3)]
        grads["w_branch"][l] = jnp.stack(
            [_mm(s[f"y{t}"], dps[t], ta=True, name=f"g_branch{l}_{t}", bm=D_MODEL, bn=D_MODEL, bk=bk_r,
                 out_dtype=BF16) for t in range(3)])
        arrs = [_to_slots(grads["w_down"][l], 0), _to_slots(grads["w_up"][l], 0), _to_slots(grads["w_out"][l], 0),
                _to_slots(grads["w_branch"][l], 1)]
        modes = ["a2a"] * 4
        if l + 1 < DEPTH:
            arrs.append(small_pack(l + 1))
            modes.append("gather")
        (dq, dk, dv, dsk), got = _attn_bwd(s["qkv"], sink[l], s["y0"], dys[0], s["lse"], n_lat, m_ctx, f"d_attn{l}",
                                           carry=(arrs, modes))
        recv["dn", l], recv["up", l], recv["out", l], recv["br", l] = got[:4]
        if l + 1 < DEPTH:
            small_recv[l + 1] = got[4]
        grads["sink"][l] = dsk[:, 0, :8].reshape(N_HEADS)
        dz_qkv = _rope_bwd(dq, dk, dv, cos, sin, f"d_rope{l}")
        dz_a, dws, dbs, dgv = _gating_bwd(s["z"], ws_b[l], ws_t[l], s["bias"], vec(g_v[l]), dys[1], f"d_gating{l}")
        grads["w_spatial"][l], grads["b_spatial"][l], grads["g_v"][l] = dws, dbs.T, dgv[0]
        dz_b, dwsc = _sconv_bwd(s["z"], w_sconv_full[l], dys[2], n_lat, f"d_sconv{l}")
        grads["w_sconv"][l] = dwsc[:3]
        kvw = 2 * N_KV_HEADS * HEAD_DIM
        pieces = [(dz_g, None, 0, 3, 0, D_MODEL), (dz_qkv, None, 0, 1, P_Q // D_MODEL, D_MODEL),
                  (dz_a, None, 0, 2, P_A // D_MODEL, D_MODEL), (dz_b, "step", 0, 3, P_B // D_MODEL, D_MODEL),
                  (dz_qkv, None, D_MODEL // kvw, 1, P_KV // kvw, kvw)]
        gw = lambda a, nm, rows, **kw: _mm(a, s["h"], ta=True, name=f"g_proj_in{l}_{nm}", bm=rows, bn=D_MODEL,
                                           bk=bk_r, out_dtype=BF16, **kw)
        gw_g = gw(dz_g, "gate", 1536)
        gw_qkv = gw(dz_qkv, "qkv", 1536)
        gw_a = gw(dz_a, "gating", 1024)
        gw_b = [gw(dz_b, f"sconv{t}", 1024, a_lead=t) for t in range(3)]
        grads["w_in"][l] = jnp.concatenate([gw_qkv, gw_a] + gw_b + [gw_g], axis=0)
        bm_in = _pick(r, [1056, 768, 256])
        if l > 0:
            dh = _mm_pieces(pieces, full["in", l], w_t=True, name=f"d_proj_in{l}", bm=bm_in)
        else:
            dh, (recv["in", 0], small_recv[0]) = _mm_pieces(
                pieces, full["in", l], w_t=True, name=f"d_proj_in{l}", bm=bm_in,
                carry=([_to_slots(grads["w_in"][0], 0), small_pack(0)], ["a2a", "gather"]))
        below = None if l == 0 else (saved[l - 1]["dd"], mods[l - 1], 5)
        outs = _norm_bwd(s["x0"], vec(g_mix[l]), mods[l], dh, dx1, n_lat, 0, 1, f"d_norm_mix{l}", res=below)
        if below is None:
            dxs, acc, accs = outs
        else:
            dxs, d_dd, acc, accs = outs
        grads["g_mix"][l] = acc[0]
        dmods[l] = jnp.stack([accs[:, 0], accs[:, 1], gate1, shift2, scale2, gate2], axis=1)
        gate2 = accs[:, 2]

    dmod_own = jnp.stack([dmods[l][0].reshape(-1) for l in range(DEPTH)])
    dmod_ctx = jnp.stack([dmods[l][1].reshape(-1) for l in range(DEPTH)])
    late = [grads["g_mix"][0], dmod_own + dmod_ctx, dmod_ctx, dmod_own]
    late_shapes = [a.shape for a in late]
    (late_all,) = _exchange([_pack(late)], ["gather"], "gather_late_grads")
    g_mix0_g, b_mod_g, dmodc_tot, _ = _unpack(_sum_slots(late_all, "sum_late_grads"), late_shapes)
    dmod_all = _unpack(late_all, late_shapes, lead=(N_DEV,))[-1]
    layer_sums = [_unpack(_sum_slots(small_recv[l], f"sum_small_grads{l}"), small_shapes_of[l]) for l in range(DEPTH)]
    by_name = {k: jnp.stack([layer_sums[l][t] for l in range(DEPTH)]) for t, k in enumerate(small_names)}
    g_mix_g = jnp.stack([g_mix0_g] + [layer_sums[l][len(small_names)] for l in range(1, DEPTH)])
    loss_sum, g_final_g = layer_sums[DEPTH - 1][-2], layer_sums[DEPTH - 1][-1]
    g_ffn_g, g_v_g, b_gate_g, sink_g = by_name["g_ffn"], by_name["g_v"], by_name["b_gate"], by_name["sink"]
    w_spatial_g, b_spatial_g = by_name["w_spatial"], by_name["b_spatial"]
    w_sconv_g = lax.dynamic_slice_in_dim(by_name["w_sconv"], me * w_sconv.shape[2], w_sconv.shape[2], axis=2)
    w_fconv_g = lax.dynamic_slice_in_dim(by_name["w_fconv"], me * w_fconv.shape[2], w_fconv.shape[2], axis=2)

    dmod_cols = lax.dynamic_slice_in_dim(dmod_all, me * mod_w, mod_w, axis=2)
    dmodc_cols = lax.dynamic_slice_in_dim(dmodc_tot, me * mod_w, mod_w, axis=1)
    g_w_mod, cctx_part = [], None
    for l in range(DEPTH):
        rhs = jnp.concatenate([dmod_cols[:, l], dmodc_cols[l][None], jnp.zeros((7, mod_w), F32)], axis=0)
        g_w_mod.append(_mm(act, rhs, ta=True, name=f"g_mod{l}", bm=D_MODEL, bn=mod_w, bk=16, out_dtype=F32))
        lhs = jnp.pad(dmodc_cols[l][None], ((0, 7), (0, 0)))
        part = _mm(lhs, w_mod[l], tb=True, name=f"d_cctx{l}", bm=8, bn=D_MODEL, bk=mod_w, out_dtype=F32)
        cctx_part = part if cctx_part is None else cctx_part + part
    sg = jax.nn.sigmoid(c_ctx)
    dsilu = (sg * (1.0 + c_ctx * (1.0 - sg))).reshape(8, LANES)

    (r_cctx,) = _exchange([cctx_part[0].reshape(8, LANES)], ["gather"], "gather_c_ctx_grad")

    def per_layer(a):
        return a.reshape(a.shape[0], -1, a.shape[-1])

    upd = {}
    for nm, kind, wv, mv, vv in (("w_in", "in", w_in, m_w_in, v_w_in), ("w_branch", "br", w_branch, m_w_branch, v_w_branch),
                                 ("w_out", "out", w_out, m_w_out, v_w_out), ("w_up", "up", w_up, m_w_up, v_w_up),
                                 ("w_down", "dn", w_down, m_w_down, v_w_down), ("w_mod", None, w_mod, m_w_mod, v_w_mod)):
        if kind is None:
            gsrcs = [g[None] for g in g_w_mod]
        else:
            gsrcs = [per_layer(recv[kind, l]) for l in range(DEPTH)]
        if kind in ("in", "up"):
            outs = _adamw(tr(wv), gsrcs, tr(mv), tr(vv), f"adamw_{nm}")
            upd[nm] = [tr(o) for o in outs]
        else:
            outs = _adamw(per_layer(wv), gsrcs, per_layer(mv), per_layer(vv), f"adamw_{nm}")
            upd[nm] = [o.reshape(wv.shape) for o in outs]
    as_tile = lambda a: a.reshape(1, 8, LANES)
    upd["c_ctx"] = [o.reshape(D_MODEL) for o in _adamw(
        as_tile(c_ctx), [r_cctx], as_tile(m_c_ctx), as_tile(v_c_ctx), "adamw_c_ctx", scale=as_tile(dsilu))]

    names = ["b_mod", "g_mix", "b_gate", "sink", "w_spatial", "b_spatial", "g_v", "w_sconv", "g_ffn", "w_fconv",
             "g_final"]
    w_s = [b_mod, g_mix, b_gate, sink, w_spatial, b_spatial, g_v, w_sconv, g_ffn, w_fconv, g_final]
    g_s = [b_mod_g, g_mix_g, b_gate_g, sink_g, w_spatial_g, b_spatial_g, g_v_g, w_sconv_g, g_ffn_g, w_fconv_g,
           g_final_g]
    m_s = [m_b_mod, m_g_mix, m_b_gate, m_sink, m_w_spatial, m_b_spatial, m_g_v, m_w_sconv, m_g_ffn, m_w_fconv,
           m_g_final]
    v_s = [v_b_mod, v_g_mix, v_b_gate, v_sink, v_w_spatial, v_b_spatial, v_g_v, v_w_sconv, v_g_ffn, v_w_fconv,
           v_g_final]
    shapes = [a.shape for a in w_s]
    packed = _adamw(_pack(w_s)[None], [_pack(g_s)[None]], _pack(m_s)[None], _pack(v_s)[None], "adamw_small")
    unpacked = [_unpack(o[0], shapes) for o in packed]
    for t, nm in enumerate(names):
        upd[nm] = [unpacked[q][t] for q in range(4)]

    order = ["c_ctx", "w_mod", "b_mod", "g_mix", "w_in", "b_gate", "sink", "w_spatial", "b_spatial", "g_v", "w_sconv",
             "w_branch", "w_out", "g_ffn", "w_up", "w_fconv", "w_down", "g_final"]
    result = [loss_sum.reshape(()), dxs[:n_lat][None]]
    for q in range(4):
        result += [upd[nm][q] for nm in order]
    return tuple(result)
```

```python
import jax
import jax.numpy as jnp
from jax import lax
from jax.experimental import pallas as pl
from jax.experimental.pallas import tpu as pltpu

F32, BF16 = jnp.float32, jnp.bfloat16

D_MODEL = 1024
DEPTH = 2
GRID_W = 64
N_HEADS = 16
N_KV_HEADS = 4
HEAD_DIM = 64
WINDOW = 128
BLK = 128
ROPE_THETA = 10000.0
CHUNK = 128
A_GROUPS = 8
D_FF = 2816
EPS = 1e-6
NEG = -1e30
IN_W = 9728
OFF_K, OFF_A, OFF_B, OFF_G = 1024, 1536, 3584, 6656
P_Q, P_A, P_B, P_KV = 3072, 4096, 6144, 9216

N_DEV = 8
LANES = 128
ROW_TILE = 256
CONV_CHUNK = 256
PAD = 8
VMEM_LIMIT = 52 * 1024 * 1024

ADAM_LR, ADAM_B1, ADAM_B2, ADAM_EPS, ADAM_WD, ADAM_STEP = 0.001, 0.9, 0.999, 1e-08, 0.01, 10

HBM_SPEC = pl.BlockSpec(memory_space=pltpu.HBM)
SMEM_SPEC = pl.BlockSpec(memory_space=pltpu.SMEM)


def _params():
    return pltpu.CompilerParams(vmem_limit_bytes=VMEM_LIMIT)


def _pick(n, prefs):
    for p in prefs:
        if n % p == 0:
            return p
    raise ValueError((n, prefs))


def _sigmoid(x):
    return 0.5 * jnp.tanh(0.5 * x) + 0.5


def _mm(a, b, *, name, ta=False, tb=False, bm, bn, bk, out_dtype, a_lead=None, b_lead=None, a_spec=None,
        b_spec=None, cols_outer=False, carry=None):
    ash = a.shape[1:] if a_lead is not None else a.shape
    bsh = b.shape[1:] if b_lead is not None else b.shape
    kc = (bsh[1] if tb else bsh[0]) if b_spec is None else (ash[0] if ta else ash[1])
    mo =(ash[1] if ta else ash[0]) if a_spec is None else a_spec[2]
    no = (bsh[0] if tb else bsh[1]) if b_spec is None else b_spec[2]
    assert mo % bm == 0 and no % bn == 0 and kc % bk == 0, (name, mo, no, kc, bm, bn, bk)
    nk = kc // bk

    def spec(shape, fn, idx=None):
        if idx is not None:
            shape, inner = (None,) + shape, fn
            fn = lambda i, j, k: (idx,) + inner(i, j, k)
        if cols_outer:
            return pl.BlockSpec(shape, lambda j, i, k: fn(i, j, k))
        return pl.BlockSpec(shape, fn)

    if a_spec is not None:
        a_bs = spec(a_spec[0], a_spec[1])
    elif ta:
        a_bs = spec((bk, bm), lambda i, j, k: (k, i), a_lead)
    else:
        a_bs = spec((bm, bk), lambda i, j, k: (i, k), a_lead)
    if b_spec is not None:
        b_bs = spec(b_spec[0], b_spec[1])
    elif tb:
        b_bs = spec((bn, bk), lambda i, j, k: (j, k), b_lead)
    else:
        b_bs = spec((bk, bn), lambda i, j, k: (k, j), b_lead)
    dims = (((0 if ta else 1,), (1 if tb else 0,)), ((), ()))
    grid = (no // bn, mo // bm, nk) if cols_outer else (mo // bm, no // bn, nk)

    def body(a_ref, b_ref, o_ref, *scratch):
        if nk == 1:
            o_ref[...] = lax.dot_general(a_ref[...], b_ref[...], dims, preferred_element_type=F32).astype(o_ref.dtype)
        else:
            acc = scratch[0]
            k = pl.program_id(2)

            @pl.when(k == 0)
            def _():
                acc[...] = jnp.zeros_like(acc)

            acc[...] += lax.dot_general(a_ref[...], b_ref[...], dims, preferred_element_type=F32)

            @pl.when(k == nk - 1)
            def _():
                o_ref[...] = acc[...].astype(o_ref.dtype)

    outs, carried = _pcall(
        body, (a, b), name=name, grid=grid, in_specs=[a_bs, b_bs],
        out_specs=[spec((bm, bn), lambda i, j, k: (i, j))],
        out_shape=[jax.ShapeDtypeStruct((mo, no), out_dtype)],
        scratch_shapes=[pltpu.VMEM((bm, bn), F32)] if nk > 1 else [], carry=carry)
    return outs[0] if carry is None else (outs[0], carried)


def _mm_pieces(pieces, w, *, name, bm, w_t=False, carry=None):
    kout = w.shape[1] if w_t else w.shape[0]
    starts, total = [], 0
    for piece in pieces:
        starts.append(total)
        total += piece[3]
    mo = pieces[0][0].shape[-2]
    assert mo % bm == 0
    widths = sorted({piece[5] for piece in pieces}, reverse=True)

    def inside(p, k):
        return (k >= starts[p]) & (k < starts[p] + pieces[p][3])

    def a_spec(p):
        _, lead, col0, nblk, _, bk = pieces[p]

        def fn(i, k):
            t = jnp.clip(k - starts[p], 0, nblk - 1)
            if lead is None:
                return (i, col0 + t)
            return (t, i, col0) if lead == "step" else (lead, i, col0 + t)
        return pl.BlockSpec((bm, bk) if lead is None else (None, bm, bk), fn)

    def w_spec(bk):
        def fn(i, k):
            col = 0
            for p, piece in enumerate(pieces):
                if piece[5] == bk:
                    col = col + jnp.where(inside(p, k), piece[4] + k - starts[p], 0)
            return (col, 0) if w_t else (0, col)
        return pl.BlockSpec((bk, kout) if w_t else (kout, bk), fn)

    n_p, n_w = len(pieces), len(widths)

    def body(*refs):
        a_refs, w_refs, o_ref, acc = refs[:n_p], refs[n_p:n_p + n_w], refs[n_p + n_w], refs[n_p + n_w + 1]
        k = pl.program_id(1)

        @pl.when(k == 0)
        def _():
            acc[...] = jnp.zeros_like(acc)

        for p in range(n_p):
            w_ref = w_refs[widths.index(pieces[p][5])]

            @pl.when(inside(p, k))
            def _(p=p, w_ref=w_ref):
                acc[...] += lax.dot_general(a_refs[p][...], w_ref[...], NN if w_t else NT,
                                            preferred_element_type=F32)

        @pl.when(k == total - 1)
        def _():
            o_ref[...] = acc[...]

    outs, carried = _pcall(
        body, [piece[0] for piece in pieces] + [w] * n_w, name=name, grid=(mo // bm, total),
        in_specs=[a_spec(p) for p in range(n_p)] + [w_spec(bk) for bk in widths],
        out_specs=[pl.BlockSpec((bm, kout), lambda i, k: (i, 0))],
        out_shape=[jax.ShapeDtypeStruct((mo, kout), F32)],
        scratch_shapes=[pltpu.VMEM((bm, kout), F32)], carry=carry)
    return outs[0] if carry is None else (outs[0], carried)


def _xchg_out_shapes(arrs, modes):
    return [jax.ShapeDtypeStruct((N_DEV,) + a.shape if m == "gather" else a.shape, a.dtype)
            for a, m in zip(arrs, modes)]


def _xchg_sems(n):
    return [pltpu.SemaphoreType.DMA((n, N_DEV - 1)), pltpu.SemaphoreType.DMA((n, N_DEV - 1)),
            pltpu.SemaphoreType.DMA((n,))]


def _xchg_copies(ins, outs, modes, sems):
    send_sems, recv_sems, local_sems = sems
    x, y, c = lax.axis_index("x"), lax.axis_index("y"), lax.axis_index("c")
    me = 4 * x + 2 * y + c

    def place(q):
        px = 1 - x if (q >> 2) & 1 else x
        py = 1 - y if (q >> 1) & 1 else y
        pc = 1 - c if q & 1 else c
        return (px, py, pc), 4 * px + 2 * py + pc

    sibling, _ = place(1)
    out = dict(local=[], direct=[], landed=[], passed=[], others=[])
    for t, mode in enumerate(modes):
        gather = mode == "gather"
        mine = ins[t] if gather else ins[t].at[me]
        out["local"].append(pltpu.make_async_copy(mine, outs[t].at[me], local_sems.at[t]))
        for q in range(1, N_DEV):
            peer, slot = place(q)
            sem = dict(send_sem=send_sems.at[t, q - 1], recv_sem=recv_sems.at[t, q - 1],
                       device_id_type=pl.DeviceIdType.MESH)
            arrival = pltpu.make_async_remote_copy(src_ref=outs[t].at[slot], dst_ref=outs[t].at[slot],
                                                   device_id=peer, **sem)
            if not gather:
                out["direct"].append(pltpu.make_async_remote_copy(src_ref=ins[t].at[slot], dst_ref=outs[t].at[me],
                                                                  device_id=peer, **sem))
                out["others"].append(arrival)
            elif q == 1 or q % 2 == 0:
                out["direct"].append(pltpu.make_async_remote_copy(src_ref=ins[t], dst_ref=outs[t].at[me],
                                                                  device_id=peer, **sem))
                out["others" if q == 1 else "landed"].append(arrival)
            else:
                _, origin = place(q - 1)
                out["passed"].append(pltpu.make_async_remote_copy(src_ref=outs[t].at[origin], dst_ref=outs[t].at[origin],
                                                                  device_id=sibling, **sem))
                out["others"].append(arrival)
    return out


def _xchg_start(copies):
    for cp in copies["local"] + copies["direct"]:
        cp.start()


def _xchg_wait(copies):
    for cp in copies["landed"]:
        cp.wait_recv()
    for cp in copies["passed"]:
        cp.start()
    for cp in copies["others"]:
        cp.wait_recv()
    for cp in copies["direct"] + copies["passed"]:
        cp.wait_send()
    for cp in copies["local"]:
        cp.wait()


def _exchange(arrs, modes, name):
    n = len(arrs)

    def body(*refs):
        copies = _xchg_copies(refs[:n], refs[n:2 * n], modes, refs[2 * n:])
        _xchg_start(copies)
        _xchg_wait(copies)

    outs = pl.pallas_call(
        body, name=name, in_specs=[HBM_SPEC] * n, out_specs=[HBM_SPEC] * n, out_shape=_xchg_out_shapes(arrs, modes),
        scratch_shapes=_xchg_sems(n), compiler_params=pltpu.CompilerParams(has_side_effects=True),
    )(*arrs)
    return list(outs)


def _pcall(body, operands, *, name, grid, in_specs, out_specs, out_shape, scratch_shapes=(), carry=None):
    out_specs, out_shape, scratch_shapes = list(out_specs), list(out_shape), list(scratch_shapes)
    if carry is None:
        outs = pl.pallas_call(body, name=name, grid=grid, in_specs=in_specs, out_specs=out_specs,
                              out_shape=out_shape, scratch_shapes=scratch_shapes, compiler_params=_params())(*operands)
        return list(outs), []
    arrs, modes = carry
    n, n_in, n_out, n_scr = len(arrs), len(in_specs), len(out_specs), len(scratch_shapes)

    def wrapped(*refs):
        ins, c_in = refs[:n_in], refs[n_in:n_in + n]
        outs, c_out = refs[n_in + n:n_in + n + n_out], refs[n_in + n + n_out:n_in + 2 * n + n_out]
        rest = refs[n_in + 2 * n + n_out:]
        scr, sems = rest[:n_scr], rest[n_scr:]
        first, last = None, None
        for d, size in enumerate(grid):
            f, e = pl.program_id(d) == 0, pl.program_id(d) == size - 1
            first = f if first is None else first & f
            last = e if last is None else last & e

        @pl.when(first)
        def _():
            _xchg_start(_xchg_copies(c_in, c_out, modes, sems))

        body(*ins, *outs, *scr)

        @pl.when(last)
        def _():
            _xchg_wait(_xchg_copies(c_in, c_out, modes, sems))

    outs = pl.pallas_call(
        wrapped, name=name, grid=grid, in_specs=list(in_specs) + [HBM_SPEC] * n,
        out_specs=out_specs + [HBM_SPEC] * n, out_shape=out_shape + _xchg_out_shapes(arrs, modes),
        scratch_shapes=scratch_shapes + _xchg_sems(n), compiler_params=_params())(*operands, *arrs)
    return list(outs[:n_out]), list(outs[n_out:])


def _row_specs(r, n_lat, tm):
    nbl = n_lat // tm
    row = pl.BlockSpec((tm, D_MODEL), lambda i: (i, 0))
    mod = pl.BlockSpec((None, 8, D_MODEL), lambda i: (i // nbl, 0, 0))
    vec = pl.BlockSpec((1, D_MODEL), lambda i: (0, 0))
    return nbl, row, mod, vec


def _rows(vals):
    width = [v for v in vals if v is not None][0].shape[1]
    return jnp.concatenate([jnp.zeros((1, width), F32) if v is None else v for v in vals], axis=0)


def _norm_fwd(xs, g, mods, n_lat, sh, sc, name, res=None):
    r = xs.shape[0]
    tm = ROW_TILE
    nbl, row, mod, vec = _row_specs(r, n_lat, tm)

    def norm(x, g_ref, m_ref, h_ref):
        rs = lax.rsqrt(jnp.mean(x * x, axis=-1, keepdims=True) + EPS)
        m = m_ref[...]
        h_ref[...] = ((x * rs * g_ref[...]) * (1.0 + m[sc:sc + 1]) + m[sh:sh + 1]).astype(BF16)

    if res is None:
        def body(x_ref, g_ref, m_ref, h_ref):
            norm(x_ref[...], g_ref, m_ref, h_ref)
        return pl.pallas_call(
            body, name=name, grid=(r // tm,), in_specs=[row, vec, mod], out_specs=row,
            out_shape=jax.ShapeDtypeStruct((r, D_MODEL), BF16), compiler_params=_params())(xs, g, mods)

    o, mods_res, gt = res

    def body(x_ref, o_ref, mr_ref, g_ref, m_ref, x1_ref, h_ref):
        x = x_ref[...] + mr_ref[...][gt:gt + 1] * o_ref[...]
        x1_ref[...] = x
        norm(x, g_ref, m_ref, h_ref)

    return pl.pallas_call(
        body, name=name, grid=(r // tm,), in_specs=[row, row, mod, vec, mod], out_specs=[row, row],
        out_shape=[jax.ShapeDtypeStruct((r, D_MODEL), F32), jax.ShapeDtypeStruct((r, D_MODEL), BF16)],
        compiler_params=_params())(xs, o, mods_res, g, mods)


def _rms_bwd(x, g, dy):
    rs = lax.rsqrt(jnp.mean(x * x, axis=-1, keepdims=True) + EPS)
    xh = x * rs
    dxh = dy * g
    dx = rs * (dxh - xh * jnp.mean(dxh * xh, axis=-1, keepdims=True))
    return dx, dy * xh, xh


def _acc_specs(nbl):
    acc_all = pl.BlockSpec((8, D_MODEL), lambda i: (0, 0))
    acc_stream = pl.BlockSpec((None, 8, D_MODEL), lambda i: (i // nbl, 0, 0))
    return acc_all, acc_stream


def _loss_bwd(xs, o, mods, g_final, target, n_lat, name):
    r = xs.shape[0]
    tm = ROW_TILE
    nbl, row, mod, vec = _row_specs(r, n_lat, tm)
    acc_all, acc_stream = _acc_specs(nbl)
    tgt = pl.BlockSpec((tm, D_MODEL), lambda i: (jnp.minimum(i, nbl - 1), 0))

    def body(x_ref, o_ref, m_ref, g_ref, t_ref, dx_ref, do_ref, acc_ref, accs_ref):
        i = pl.program_id(0)
        lat = i < nbl
        gate = m_ref[...][5:6]
        o_val = o_ref[...]
        x = x_ref[...] + gate * o_val
        g = g_ref[...]
        rs = lax.rsqrt(jnp.mean(x * x, axis=-1, keepdims=True) + EPS)
        y = x * rs * g
        err = jnp.where(lat, y - t_ref[...], 0.0)
        loss = 0.5 * jnp.sum(jnp.mean(err * err, axis=-1, keepdims=True), axis=0, keepdims=True)
        dy = err * (1.0 / D_MODEL)
        dx, dg_rows, _ = _rms_bwd(x, g, dy)
        dx_ref[...] = dx
        do_ref[...] = (gate * dx).astype(BF16)

        @pl.when(i == 0)
        def _():
            acc_ref[...] = jnp.zeros_like(acc_ref)

        @pl.when((i == 0) | (i == nbl))
        def _():
            accs_ref[...] = jnp.zeros_like(accs_ref)

        acc_ref[...] += _rows([jnp.sum(dg_rows, axis=0, keepdims=True), jnp.broadcast_to(loss, (1, D_MODEL))]
                              + [None] * 6)
        accs_ref[...] += _rows([None, None, jnp.sum(dx * o_val, axis=0, keepdims=True)] + [None] * 5)

    return pl.pallas_call(
        body, name=name, grid=(r // tm,), in_specs=[row, row, mod, vec, tgt],
        out_specs=[row, row, acc_all, acc_stream],
        out_shape=[jax.ShapeDtypeStruct((r, D_MODEL), F32), jax.ShapeDtypeStruct((r, D_MODEL), BF16),
                   jax.ShapeDtypeStruct((8, D_MODEL), F32), jax.ShapeDtypeStruct((2, 8, D_MODEL), F32)],
        compiler_params=_params())(xs, o, mods, g_final, target)


def _norm_bwd(xs, g, mods, dh, dx_in, n_lat, sh, sc, name, res=None):
    r = xs.shape[0]
    tm = ROW_TILE
    nbl, row, mod, vec = _row_specs(r, n_lat, tm)
    acc_all, acc_stream = _acc_specs(nbl)
    has_res = res is not None

    def body(*refs):
        if has_res:
            x_ref, g_ref, m_ref, dh_ref, dxi_ref, o_ref, mr_ref, dx_ref, do_ref, acc_ref, accs_ref = refs
        else:
            x_ref, g_ref, m_ref, dh_ref, dxi_ref, dx_ref, acc_ref, accs_ref = refs
        i = pl.program_id(0)
        x, g, m, dhv = x_ref[...], g_ref[...], m_ref[...], dh_ref[...]
        dy = dhv * (1.0 + m[sc:sc + 1])
        dxn, dg_rows, xh = _rms_bwd(x, g, dy)
        dx = dxi_ref[...] + dxn
        if has_res:
            dx_ref[...] = dx
        else:
            @pl.when(i < nbl)
            def _():
                dx_ref[...] = dx
        d_gate = None
        if has_res:
            o_val = o_ref[...]
            do_ref[...] = (mr_ref[...][res[2]:res[2] + 1] * dx).astype(BF16)
            d_gate = jnp.sum(dx * o_val, axis=0, keepdims=True)

        @pl.when(i == 0)
        def _():
            acc_ref[...] = jnp.zeros_like(acc_ref)

        @pl.when((i == 0) | (i == nbl))
        def _():
            accs_ref[...] = jnp.zeros_like(accs_ref)

        acc_ref[...] += _rows([jnp.sum(dg_rows, axis=0, keepdims=True)] + [None] * 7)
        accs_ref[...] += _rows([jnp.sum(dhv, axis=0, keepdims=True),
                                jnp.sum(dhv * (xh * g), axis=0, keepdims=True), d_gate] + [None] * 5)

    ins = [xs, g, mods, dh, dx_in]
    in_specs = [row, vec, mod, row, row]
    if has_res:
        out_specs, out_shape = [row], [jax.ShapeDtypeStruct((r, D_MODEL), F32)]
    else:
        out_specs = [pl.BlockSpec((tm, D_MODEL), lambda i: (jnp.minimum(i, nbl - 1), 0))]
        out_shape = [jax.ShapeDtypeStruct((n_lat, D_MODEL), F32)]
    if has_res:
        ins += [res[0], res[1]]
        in_specs += [row, mod]
        out_specs.append(row)
        out_shape.append(jax.ShapeDtypeStruct((r, D_MODEL), BF16))
    out_specs += [acc_all, acc_stream]
    out_shape += [jax.ShapeDtypeStruct((8, D_MODEL), F32), jax.ShapeDtypeStruct((2, 8, D_MODEL), F32)]
    return pl.pallas_call(body, name=name, grid=(r // tm,), in_specs=in_specs, out_specs=out_specs,
                          out_shape=out_shape, compiler_params=_params())(*ins)


def _rotate(t, cos, sin):
    width = t.shape[1]
    reps = width // LANES
    lane = lax.broadcasted_iota(jnp.int32, (1, width), 1)
    first = (lane % HEAD_DIM) < (HEAD_DIM // 2)
    swapped = jnp.where(first, pltpu.roll(t, width - HEAD_DIM // 2, 1), pltpu.roll(t, HEAD_DIM // 2, 1))
    return t * jnp.tile(cos, (1, reps)) + swapped * jnp.tile(sin, (1, reps))


def _rope_fwd(z, cos, sin, name):
    r = z.shape[0]
    tm = ROW_TILE
    kvw = 2 * N_KV_HEADS * HEAD_DIM
    tab = pl.BlockSpec((tm, LANES), lambda i: (i, 0))

    def body(q_ref, kv_ref, c_ref, s_ref, o_ref):
        c, s = c_ref[...], s_ref[...]
        kv = kv_ref[...]
        o_ref[:, :D_MODEL] = (_rotate(q_ref[...].astype(F32), c, s) * (HEAD_DIM ** -0.5)).astype(BF16)
        o_ref[:, D_MODEL:D_MODEL + kvw // 2] = _rotate(kv[:, :kvw // 2].astype(F32), c, s).astype(BF16)
        o_ref[:, D_MODEL + kvw // 2:] = kv[:, kvw // 2:]

    return pl.pallas_call(
        body, name=name, grid=(r // tm,),
        in_specs=[pl.BlockSpec((tm, D_MODEL), lambda i: (i, P_Q // D_MODEL)),
                  pl.BlockSpec((tm, kvw), lambda i: (i, P_KV // kvw)), tab, tab],
        out_specs=pl.BlockSpec((tm, D_MODEL + kvw), lambda i: (i, 0)),
        out_shape=jax.ShapeDtypeStruct((r, D_MODEL + kvw), BF16), compiler_params=_params())(z, z, cos, sin)


def _rope_bwd(dq, dk, dv, cos, sin, name):
    r = dq.shape[0]
    tm = ROW_TILE
    kw = N_KV_HEADS * HEAD_DIM
    tab = pl.BlockSpec((tm, LANES), lambda i: (i, 0))

    def body(dq_ref, dk_ref, dv_ref, c_ref, s_ref, o_ref):
        c, s = c_ref[...], -s_ref[...]
        o_ref[:, :D_MODEL] = (_rotate(dq_ref[...], c, s) * (HEAD_DIM ** -0.5)).astype(BF16)
        o_ref[:, D_MODEL:D_MODEL + kw] = _rotate(dk_ref[...], c, s).astype(BF16)
        o_ref[:, D_MODEL + kw:] = dv_ref[...].astype(BF16)

    return pl.pallas_call(
        body, name=name, grid=(r // tm,),
        in_specs=[pl.BlockSpec((tm, D_MODEL), lambda i: (i, 0)), pl.BlockSpec((tm, kw), lambda i: (i, 0)),
                  pl.BlockSpec((tm, kw), lambda i: (i, 0)), tab, tab],
        out_specs=pl.BlockSpec((tm, D_MODEL + 2 * kw), lambda i: (i, 0)),
        out_shape=jax.ShapeDtypeStruct((r, D_MODEL + 2 * kw), BF16), compiler_params=_params())(dq, dk, dv, cos, sin)


def _attn_setup(i, n_lat, m_ctx, nbl, k_ref, v_ref):
    start = pl.multiple_of(jnp.clip((i - 1) * BLK, 0, n_lat - 3 * BLK), BLK)
    nkeys = 3 * BLK + m_ctx
    rows = lax.broadcasted_iota(jnp.int32, (4 * BLK, nkeys), 0)
    cols = lax.broadcasted_iota(jnp.int32, (4 * BLK, nkeys), 1)
    qpos = i * BLK + (rows & (BLK - 1))
    seen = (cols >= 3 * BLK) | ((jnp.abs(start + cols - qpos) <= WINDOW) & (i < nbl))
    mask = jnp.where(seen, 0.0, NEG)
    kblk = jnp.concatenate([k_ref[pl.ds(start, 3 * BLK), :], k_ref[pl.ds(n_lat, m_ctx), :]], axis=0)
    vblk = jnp.concatenate([v_ref[pl.ds(start, 3 * BLK), :], v_ref[pl.ds(n_lat, m_ctx), :]], axis=0)
    lo = lax.broadcasted_iota(jnp.int32, (1, LANES), 1) < HEAD_DIM
    return start, mask, kblk, vblk, lo


def _stack_heads(ref, kh, lo):
    a = ref[:, (2 * kh) * LANES:(2 * kh + 1) * LANES]
    b = ref[:, (2 * kh + 1) * LANES:(2 * kh + 2) * LANES]
    z = jnp.zeros_like(a)
    return jnp.concatenate([jnp.where(lo, a, z), jnp.where(lo, z, a), jnp.where(lo, b, z), jnp.where(lo, z, b)],
                           axis=0)


def _kv_variants(blk, rolled, kh, lo):
    z = jnp.zeros_like(blk)
    if kh == 0:
        return jnp.where(lo, blk, rolled), jnp.where(lo, blk, z), jnp.where(lo, z, rolled)
    return jnp.where(lo, rolled, blk), jnp.where(lo, rolled, z), jnp.where(lo, z, blk)


NN = (((1,), (0,)), ((), ()))
NT = (((1,), (1,)), ((), ()))
TN = (((0,), (0,)), ((), ()))


def _scores(qst, k2, mask, sink_ref, p, kh):
    s = lax.dot_general(qst, k2, NT, preferred_element_type=F32) + mask
    snk = jnp.concatenate([jnp.full((BLK, 1), sink_ref[p * 8 + kh * 4 + g], F32) for g in range(4)], axis=0)
    return s, snk


def _attn_fwd(qkv, sink, n_lat, m_ctx, name, carry=None):
    r = qkv.shape[0]
    nb, nbl = r // BLK, n_lat // BLK
    kcol = D_MODEL // LANES

    def body(sink_ref, q_ref, k_ref, v_ref, o_ref, lse_ref):
        p, i = pl.program_id(0), pl.program_id(1)
        _, mask, kblk, vblk, lo = _attn_setup(i, n_lat, m_ctx, nbl, k_ref, v_ref)
        kr, vr = pltpu.roll(kblk, HEAD_DIM, 1), pltpu.roll(vblk, HEAD_DIM, 1)
        for kh in range(2):
            k2, _, _ = _kv_variants(kblk, kr, kh, lo)
            _, vlo, vhi = _kv_variants(vblk, vr, kh, lo)
            qst = _stack_heads(q_ref, kh, lo)
            s, snk = _scores(qst, k2, mask, sink_ref, p, kh)
            mx = jnp.maximum(jnp.max(s, axis=-1, keepdims=True), snk)
            pe = jnp.exp(s - mx)
            den = jnp.sum(pe, axis=-1, keepdims=True) + jnp.exp(snk - mx)
            inv = 1.0 / den
            pb = pe.astype(BF16)
            for jp in range(2):
                r0 = 2 * jp * BLK
                pair = (jnp.dot(pb[r0:r0 + BLK], vlo, preferred_element_type=F32)
                        + jnp.dot(pb[r0 + BLK:r0 + 2 * BLK], vhi, preferred_element_type=F32))
                pair = pair * jnp.where(lo, inv[r0:r0 + BLK], inv[r0 + BLK:r0 + 2 * BLK])
                o_ref[:, (2 * kh + jp) * LANES:(2 * kh + jp + 1) * LANES] = pair.astype(BF16)
            lse = mx + jnp.log(den)
            for g in range(4):
                lse_ref[:, kh * 4 + g:kh * 4 + g + 1] = lse[g * BLK:(g + 1) * BLK]

    outs, carried = _pcall(
        body, (sink, qkv, qkv, qkv), name=name, grid=(2, nb),
        in_specs=[SMEM_SPEC,
                  pl.BlockSpec((BLK, 4 * LANES), lambda p, i: (i, p)),
                  pl.BlockSpec((r, LANES), lambda p, i: (0, kcol + p)),
                  pl.BlockSpec((r, LANES), lambda p, i: (0, kcol + 2 + p))],
        out_specs=[pl.BlockSpec((BLK, 4 * LANES), lambda p, i: (i, p)),
                   pl.BlockSpec((None, BLK, 8), lambda p, i: (p, i, 0))],
        out_shape=[jax.ShapeDtypeStruct((r, D_MODEL), BF16), jax.ShapeDtypeStruct((2, r, 8), F32)], carry=carry)
    return outs if carry is None else (outs, carried)


def _attn_bwd(qkv, sink, o, do, lse, n_lat, m_ctx, name, carry=None):
    r = qkv.shape[0]
    nb, nbl = r // BLK, n_lat // BLK
    kcol = D_MODEL // LANES

    def body(sink_ref, q_ref, k_ref, v_ref, o_ref, do_ref, lse_ref, dq_ref, dk_ref, dv_ref, ds_ref):
        p, i = pl.program_id(0), pl.program_id(1)

        @pl.when(i == 0)
        def _():
            dk_ref[...] = jnp.zeros_like(dk_ref)
            dv_ref[...] = jnp.zeros_like(dv_ref)
            ds_ref[...] = jnp.zeros_like(ds_ref)

        start, mask, kblk, vblk, lo = _attn_setup(i, n_lat, m_ctx, nbl, k_ref, v_ref)
        kr, vr = pltpu.roll(kblk, HEAD_DIM, 1), pltpu.roll(vblk, HEAD_DIM, 1)
        lane = lax.broadcasted_iota(jnp.int32, (1, LANES), 1)
        dks, dvs = [], []
        for kh in range(2):
            k2, klo, khi = _kv_variants(kblk, kr, kh, lo)
            v2, _, _ = _kv_variants(vblk, vr, kh, lo)
            qst = _stack_heads(q_ref, kh, lo)
            dost = _stack_heads(do_ref, kh, lo)
            s, snk = _scores(qst, k2, mask, sink_ref, p, kh)
            lse4 = jnp.concatenate([lse_ref[:, kh * 4 + g:kh * 4 + g + 1] for g in range(4)], axis=0)
            pe = jnp.exp(s - lse4)
            dp = lax.dot_general(dost, v2, NT, preferred_element_type=F32)
            deltas = []
            for jp in range(2):
                cols = slice((2 * kh + jp) * LANES, (2 * kh + jp + 1) * LANES)
                prod = do_ref[:, cols].astype(F32) * o_ref[:, cols].astype(F32)
                deltas.append(jnp.sum(jnp.where(lo, prod, 0.0), axis=-1, keepdims=True))
                deltas.append(jnp.sum(jnp.where(lo, 0.0, prod), axis=-1, keepdims=True))
            delta = jnp.concatenate(deltas, axis=0)
            dsc = pe * (dp - delta)
            dsb, pb = dsc.astype(BF16), pe.astype(BF16)
            for jp in range(2):
                r0 = 2 * jp * BLK
                dq_ref[:, (2 * kh + jp) * LANES:(2 * kh + jp + 1) * LANES] = (
                    jnp.dot(dsb[r0:r0 + BLK], klo, preferred_element_type=F32)
                    + jnp.dot(dsb[r0 + BLK:r0 + 2 * BLK], khi, preferred_element_type=F32))
            dkf = lax.dot_general(dsb, qst, TN, preferred_element_type=F32)
            dvf = lax.dot_general(pb, dost, TN, preferred_element_type=F32)
            dks.append(dkf + pltpu.roll(dkf, HEAD_DIM, 1))
            dvs.append(dvf + pltpu.roll(dvf, HEAD_DIM, 1))
            contrib = -jnp.exp(snk - lse4) * delta
            for g in range(4):
                tot = jnp.sum(contrib[g * BLK:(g + 1) * BLK], axis=0, keepdims=True)
                ds_ref[0:1, :] += jnp.where(lane == kh * 4 + g, tot, 0.0)
        dk_blk = jnp.where(lo, dks[0], dks[1])
        dv_blk = jnp.where(lo, dvs[0], dvs[1])
        dk_ref[pl.ds(start, 3 * BLK), :] += dk_blk[:3 * BLK]
        dk_ref[pl.ds(n_lat, m_ctx), :] += dk_blk[3 * BLK:]
        dv_ref[pl.ds(start, 3 * BLK), :] += dv_blk[:3 * BLK]
        dv_ref[pl.ds(n_lat, m_ctx), :] += dv_blk[3 * BLK:]

    qspec = pl.BlockSpec((BLK, 4 * LANES), lambda p, i: (i, p))
    outs, carried = _pcall(
        body, (sink, qkv, qkv, qkv, o, do, lse), name=name, grid=(2, nb),
        in_specs=[SMEM_SPEC, qspec,
                  pl.BlockSpec((r, LANES), lambda p, i: (0, kcol + p)),
                  pl.BlockSpec((r, LANES), lambda p, i: (0, kcol + 2 + p)),
                  qspec, qspec, pl.BlockSpec((None, BLK, 8), lambda p, i: (p, i, 0))],
        out_specs=[qspec, pl.BlockSpec((r, LANES), lambda p, i: (0, p)), pl.BlockSpec((r, LANES), lambda p, i: (0, p)),
                   pl.BlockSpec((None, 8, LANES), lambda p, i: (p, 0, 0))],
        out_shape=[jax.ShapeDtypeStruct((r, D_MODEL), F32), jax.ShapeDtypeStruct((r, 2 * LANES), F32),
                   jax.ShapeDtypeStruct((r, 2 * LANES), F32), jax.ShapeDtypeStruct((2, 8, LANES), F32)], carry=carry)
    return outs if carry is None else (outs, carried)


def _gating_parts(z_ref, gv_ref):
    za = z_ref[...].astype(F32)
    zg = jax.nn.gelu(za)
    u, v = zg[:, :D_MODEL], zg[:, D_MODEL:]
    rs = lax.rsqrt(jnp.mean(v * v, axis=-1, keepdims=True) + EPS)
    return za, u, v, rs, v * rs * gv_ref[...]


def _mix(w_ref, vals):
    vb = vals.astype(BF16)
    return jnp.concatenate(
        [jnp.dot(w_ref[g], vb[:, g * LANES:(g + 1) * LANES], preferred_element_type=F32) for g in range(A_GROUPS)],
        axis=1)


def _gating_fwd(z, ws, bias, g_v, name):
    r = z.shape[0]

    def body(z_ref, w_ref, b_ref, gv_ref, y_ref):
        _, u, _, _, vn = _gating_parts(z_ref, gv_ref)
        y_ref[...] = (u * (_mix(w_ref, vn) + b_ref[...])).astype(BF16)

    return pl.pallas_call(
        body, name=name, grid=(r // CHUNK,),
        in_specs=[pl.BlockSpec((CHUNK, 2 * D_MODEL), lambda i: (i, P_A // (2 * D_MODEL))),
                  pl.BlockSpec((A_GROUPS, CHUNK, CHUNK), lambda i: (0, 0, 0)),
                  pl.BlockSpec((CHUNK, D_MODEL), lambda i: (0, 0)),
                  pl.BlockSpec((1, D_MODEL), lambda i: (0, 0))],
        out_specs=pl.BlockSpec((CHUNK, D_MODEL), lambda i: (i, 0)),
        out_shape=jax.ShapeDtypeStruct((r, D_MODEL), BF16), compiler_params=_params())(z, ws, bias, g_v)


def _gating_bwd(z, ws, ws_t, bias, g_v, dy, name):
    r = z.shape[0]

    def body(z_ref, w_ref, wt_ref, b_ref, gv_ref, dy_ref, dz_ref, dw_ref, db_ref, dg_ref):
        i = pl.program_id(0)

        @pl.when(i == 0)
        def _():
            dw_ref[...] = jnp.zeros_like(dw_ref)
            db_ref[...] = jnp.zeros_like(db_ref)
            dg_ref[...] = jnp.zeros_like(dg_ref)

        za, u, v, rs, vn = _gating_parts(z_ref, gv_ref)
        dyv = dy_ref[...].astype(F32)
        du = dyv * (_mix(w_ref, vn) + b_ref[...])
        dmixed = dyv * u
        dvn = _mix(wt_ref, dmixed)
        dmb, vnb = dmixed.astype(BF16), vn.astype(BF16)
        for g in range(A_GROUPS):
            cols = slice(g * LANES, (g + 1) * LANES)
            dw_ref[g] += lax.dot_general(dmb[:, cols], vnb[:, cols], NT, preferred_element_type=F32)
            db_ref[:, g:g + 1] += jnp.sum(dmixed[:, cols], axis=-1, keepdims=True)
        gv = gv_ref[...]
        vh = v * rs
        dg_ref[0:1, :] += jnp.sum(dvn * vh, axis=0, keepdims=True)
        dvh = dvn * gv
        dv = rs * (dvh - vh * jnp.mean(dvh * vh, axis=-1, keepdims=True))
        _, vjp = jax.vjp(jax.nn.gelu, za)
        dz_ref[...] = vjp(jnp.concatenate([du, dv], axis=1))[0].astype(BF16)

    wspec = pl.BlockSpec((A_GROUPS, CHUNK, CHUNK), lambda i: (0, 0, 0))
    return pl.pallas_call(
        body, name=name, grid=(r // CHUNK,),
        in_specs=[pl.BlockSpec((CHUNK, 2 * D_MODEL), lambda i: (i, P_A // (2 * D_MODEL))), wspec, wspec,
                  pl.BlockSpec((CHUNK, D_MODEL), lambda i: (0, 0)), pl.BlockSpec((1, D_MODEL), lambda i: (0, 0)),
                  pl.BlockSpec((CHUNK, D_MODEL), lambda i: (i, 0))],
        out_specs=[pl.BlockSpec((CHUNK, 2 * D_MODEL), lambda i: (i, 0)), wspec,
                   pl.BlockSpec((CHUNK, A_GROUPS), lambda i: (0, 0)), pl.BlockSpec((8, D_MODEL), lambda i: (0, 0))],
        out_shape=[jax.ShapeDtypeStruct((r, 2 * D_MODEL), BF16), jax.ShapeDtypeStruct((A_GROUPS, CHUNK, CHUNK), F32),
                   jax.ShapeDtypeStruct((CHUNK, A_GROUPS), F32), jax.ShapeDtypeStruct((8, D_MODEL), F32)],
        compiler_params=_params())(z, ws, ws_t, bias, g_v, dy)


def _scr_rows(j, n_lat):
    s = pl.multiple_of(j * CONV_CHUNK, CONV_CHUNK)
    shift = jnp.where(j >= n_lat // CONV_CHUNK, 2 * PAD, PAD)
    return s, pl.multiple_of(s + shift, PAD)


def _taps(scr, j, n_lat):
    c = CONV_CHUNK
    s, at = _scr_rows(j, n_lat)
    ext = scr[pl.ds(pl.multiple_of(at - PAD, PAD), c + 2 * PAD), :]
    xm = pltpu.roll(ext, 1, 0)[PAD:PAD + c]
    xp = pltpu.roll(ext, c + 2 * PAD - 1, 0)[PAD:PAD + c]
    return s, xm, ext[PAD:PAD + c], xp


def _zero_pads(scr, r, n_lat):
    for at in (0, PAD + n_lat, 2 * PAD + r):
        scr[pl.ds(at, PAD), :] = jnp.zeros((PAD, LANES), F32)


def _col(arr_cols, c0):
    return pl.BlockSpec((arr_cols, LANES), lambda c: (0, c0 + c))


def _ffn_conv_fwd(up, w, n_lat, name):
    r = up.shape[0]
    nct = D_FF // LANES
    nchunk = r // CONV_CHUNK

    def body(a_ref, g_ref, w_ref, f_ref, scr):
        _zero_pads(scr, r, n_lat)

        def fill(j, _):
            s, at = _scr_rows(j, n_lat)
            scr[pl.ds(at, CONV_CHUNK), :] = a_ref[pl.ds(s, CONV_CHUNK), :].astype(F32)
            return 0
        lax.fori_loop(0, nchunk, fill, 0)
        wv = w_ref[...]

        def step(j, _):
            s, xm, x0, xp = _taps(scr, j, n_lat)
            ca = wv[0:1] * xm + wv[1:2] * x0 + wv[2:3] * xp
            gv = g_ref[pl.ds(s, CONV_CHUNK), :].astype(F32)
            f_ref[pl.ds(s, CONV_CHUNK), :] = (ca * _sigmoid(ca) * gv).astype(BF16)
            return 0
        lax.fori_loop(0, nchunk, step, 0)

    return pl.pallas_call(
        body, name=name, grid=(nct,),
        in_specs=[_col(r, 0), _col(r, nct), _col(3, 0)],
        out_specs=_col(r, 0), out_shape=jax.ShapeDtypeStruct((r, D_FF), BF16),
        scratch_shapes=[pltpu.VMEM((r + 3 * PAD, LANES), F32)], compiler_params=_params())(up, up, w)


def _ffn_conv_bwd(up, w, df, n_lat, name, carry=None):
    r = up.shape[0]
    nct = D_FF // LANES
    nchunk = r // CONV_CHUNK

    def body(a_ref, g_ref, w_ref, df_ref, dup_ref, dw_ref, scr, scr2):
        _zero_pads(scr, r, n_lat)
        _zero_pads(scr2, r, n_lat)

        def fill(j, _):
            s, at = _scr_rows(j, n_lat)
            scr[pl.ds(at, CONV_CHUNK), :] = a_ref[pl.ds(s, CONV_CHUNK), :].astype(F32)
            return 0
        lax.fori_loop(0, nchunk, fill, 0)
        wv = w_ref[...]

        def first(j, carry):
            s, xm, x0, xp = _taps(scr, j, n_lat)
            ca = wv[0:1] * xm + wv[1:2] * x0 + wv[2:3] * xp
            sg = _sigmoid(ca)
            gv = g_ref[pl.ds(s, CONV_CHUNK), :].astype(F32)
            dfv = df_ref[pl.ds(s, CONV_CHUNK), :].astype(F32)
            dup_ref[1, pl.ds(s, CONV_CHUNK), :] = (dfv * ca * sg).astype(BF16)
            dca = dfv * gv * (sg * (1.0 + ca * (1.0 - sg)))
            scr2[pl.ds(_scr_rows(j, n_lat)[1], CONV_CHUNK), :] = dca
            return tuple(cw + jnp.sum(dca * xv, axis=0, keepdims=True) for cw, xv in zip(carry, (xm, x0, xp)))
        zero = jnp.zeros((1, LANES), F32)
        dws = lax.fori_loop(0, nchunk, first, (zero, zero, zero))
        dw_ref[...] = _rows(list(dws) + [None] * 5)

        def second(j, _):
            s, ym, y0, yp = _taps(scr2, j, n_lat)
            dup_ref[0, pl.ds(s, CONV_CHUNK), :] = (wv[0:1] * yp + wv[1:2] * y0 + wv[2:3] * ym).astype(BF16)
            return 0
        lax.fori_loop(0, nchunk, second, 0)

    outs, carried = _pcall(
        body, (up, up, w, df), name=name, grid=(nct,),
        in_specs=[_col(r, 0), _col(r, nct), _col(3, 0), _col(r, 0)],
        out_specs=[pl.BlockSpec((2, r, LANES), lambda c: (0, 0, c)), _col(8, 0)],
        out_shape=[jax.ShapeDtypeStruct((2, r, D_FF), BF16), jax.ShapeDtypeStruct((8, D_FF), F32)],
        scratch_shapes=[pltpu.VMEM((r + 3 * PAD, LANES), F32), pltpu.VMEM((r + 3 * PAD, LANES), F32)], carry=carry)
    return outs if carry is None else (outs, carried)


def _sconv_fwd(z, w, n_lat, name):
    r = z.shape[0]
    nct = D_MODEL // LANES
    nchunk = r // CONV_CHUNK
    c0 = P_B // LANES

    def body(bg_ref, cg_ref, hb_ref, w_ref, y_ref, scr):
        _zero_pads(scr, r, n_lat)

        def fill(j, _):
            s, at = _scr_rows(j, n_lat)
            rows = pl.ds(s, CONV_CHUNK)
            scr[pl.ds(at, CONV_CHUNK), :] = cg_ref[rows, :].astype(F32) * hb_ref[rows, :].astype(F32)
            return 0
        lax.fori_loop(0, nchunk, fill, 0)
        wv = w_ref[...]

        def step(j, _):
            s, xm, x0, xp = _taps(scr, j, n_lat)
            conv = wv[0:1] * xm + wv[1:2] * x0 + wv[2:3] * xp
            y_ref[pl.ds(s, CONV_CHUNK), :] = (bg_ref[pl.ds(s, CONV_CHUNK), :].astype(F32) * conv).astype(BF16)
            return 0
        lax.fori_loop(0, nchunk, step, 0)

    return pl.pallas_call(
        body, name=name, grid=(nct,),
        in_specs=[_col(r, c0), _col(r, c0 + nct), _col(r, c0 + 2 * nct), _col(3, 0)],
        out_specs=_col(r, 0), out_shape=jax.ShapeDtypeStruct((r, D_MODEL), BF16),
        scratch_shapes=[pltpu.VMEM((r + 3 * PAD, LANES), F32)], compiler_params=_params())(z, z, z, w)


def _sconv_bwd(z, w, dy, n_lat, name):
    r = z.shape[0]
    nct = D_MODEL // LANES
    nchunk = r // CONV_CHUNK
    c0 = P_B // LANES

    def body(bg_ref, cg_ref, hb_ref, w_ref, dy_ref, dz_ref, dw_ref, scr, scr2):
        _zero_pads(scr, r, n_lat)
        _zero_pads(scr2, r, n_lat)

        def fill(j, _):
            s, at = _scr_rows(j, n_lat)
            rows = pl.ds(s, CONV_CHUNK)
            scr[pl.ds(at, CONV_CHUNK), :] = cg_ref[rows, :].astype(F32) * hb_ref[rows, :].astype(F32)
            return 0
        lax.fori_loop(0, nchunk, fill, 0)
        wv = w_ref[...]

        def first(j, carry):
            s, xm, x0, xp = _taps(scr, j, n_lat)
            rows = pl.ds(s, CONV_CHUNK)
            conv = wv[0:1] * xm + wv[1:2] * x0 + wv[2:3] * xp
            dyv = dy_ref[rows, :].astype(F32)
            dz_ref[0, rows, :] = (dyv * conv).astype(BF16)
            dconv = dyv * bg_ref[rows, :].astype(F32)
            scr2[pl.ds(_scr_rows(j, n_lat)[1], CONV_CHUNK), :] = dconv
            return tuple(cw + jnp.sum(dconv * xv, axis=0, keepdims=True) for cw, xv in zip(carry, (xm, x0, xp)))
        zero = jnp.zeros((1, LANES), F32)
        dws = lax.fori_loop(0, nchunk, first, (zero, zero, zero))
        dw_ref[...] = _rows(list(dws) + [None] * 5)

        def second(j, _):
            s, ym, y0, yp = _taps(scr2, j, n_lat)
            rows = pl.ds(s, CONV_CHUNK)
            dx = wv[0:1] * yp + wv[1:2] * y0 + wv[2:3] * ym
            dz_ref[1, rows, :] = (dx * hb_ref[rows, :].astype(F32)).astype(BF16)
            dz_ref[2, rows, :] = (dx * cg_ref[rows, :].astype(F32)).astype(BF16)
            return 0
        lax.fori_loop(0, nchunk, second, 0)

    return pl.pallas_call(
        body, name=name, grid=(nct,),
        in_specs=[_col(r, c0), _col(r, c0 + nct), _col(r, c0 + 2 * nct), _col(3, 0), _col(r, 0)],
        out_specs=[pl.BlockSpec((3, r, LANES), lambda c: (0, 0, c)), _col(8, 0)],
        out_shape=[jax.ShapeDtypeStruct((3, r, D_MODEL), BF16), jax.ShapeDtypeStruct((8, D_MODEL), F32)],
        scratch_shapes=[pltpu.VMEM((r + 3 * PAD, LANES), F32), pltpu.VMEM((r + 3 * PAD, LANES), F32)],
        compiler_params=_params())(z, z, z, w, dy)


def _merge_fwd(z, b_gate, ps, name):
    r = z.shape[0]
    tm = ROW_TILE
    row = pl.BlockSpec((tm, D_MODEL), lambda i: (i, 0))

    def body(zg_ref, b_ref, p0_ref, p1_ref, p2_ref, o_ref):
        gates = _sigmoid(zg_ref[...].astype(F32) + b_ref[...])
        acc = None
        for t, p_ref in enumerate((p0_ref, p1_ref, p2_ref)):
            term = gates[:, t * D_MODEL:(t + 1) * D_MODEL] * p_ref[...].astype(F32)
            acc = term if acc is None else acc + term
        o_ref[...] = acc.astype(BF16)

    return pl.pallas_call(
        body, name=name, grid=(r // tm,),
        in_specs=[pl.BlockSpec((tm, 3 * D_MODEL), lambda i: (i, 0)), pl.BlockSpec((1, 3 * D_MODEL), lambda i: (0, 0)),
                  row, row, row],
        out_specs=row, out_shape=jax.ShapeDtypeStruct((r, D_MODEL), BF16), compiler_params=_params())(z, b_gate, *ps)


def _merge_bwd(z, b_gate, ps, dmerged, name):
    r = z.shape[0]
    tm = ROW_TILE
    row = pl.BlockSpec((tm, D_MODEL), lambda i: (i, 0))
    wide = pl.BlockSpec((tm, 3 * D_MODEL), lambda i: (i, 0))

    def body(zg_ref, b_ref, p0_ref, p1_ref, p2_ref, dm_ref, d0_ref, d1_ref, d2_ref, dz_ref, db_ref):
        @pl.when(pl.program_id(0) == 0)
        def _():
            db_ref[...] = jnp.zeros_like(db_ref)

        gates = _sigmoid(zg_ref[...].astype(F32) + b_ref[...])
        dm = dm_ref[...].astype(F32)
        for t, (p_ref, d_ref) in enumerate(((p0_ref, d0_ref), (p1_ref, d1_ref), (p2_ref, d2_ref))):
            cols = slice(t * D_MODEL, (t + 1) * D_MODEL)
            gt = gates[:, cols]
            d_ref[...] = (dm * gt).astype(BF16)
            dlogit = dm * p_ref[...].astype(F32) * gt * (1.0 - gt)
            dz_ref[:, cols] = dlogit.astype(BF16)
            db_ref[0:1, cols] += jnp.sum(dlogit, axis=0, keepdims=True)

    shp = jax.ShapeDtypeStruct((r, D_MODEL), BF16)
    return pl.pallas_call(
        body, name=name, grid=(r // tm,),
        in_specs=[wide, pl.BlockSpec((1, 3 * D_MODEL), lambda i: (0, 0)), row, row, row, row],
        out_specs=[row, row, row, wide, pl.BlockSpec((8, 3 * D_MODEL), lambda i: (0, 0))],
        out_shape=[shp, shp, shp, jax.ShapeDtypeStruct((r, 3 * D_MODEL), BF16),
                   jax.ShapeDtypeStruct((8, 3 * D_MODEL), F32)],
        compiler_params=_params())(z, b_gate, *ps, dmerged)


def _sum_slots(buf, name):
    s, rows, _ = buf.shape
    whole_bytes = s * rows * LANES * 4
    tr = rows if whole_bytes <= VMEM_LIMIT // 8 else _pick(rows, [512, 256, 128, 64, 32, 16, 8])

    def body(b_ref, o_ref):
        acc = b_ref[0]
        for t in range(1, s):
            acc = acc + b_ref[t]
        o_ref[...] = acc

    return pl.pallas_call(
        body, name=name, grid=(rows // tr,),
        in_specs=[pl.BlockSpec((s, tr, LANES), lambda i: (0, i, 0))],
        out_specs=pl.BlockSpec((tr, LANES), lambda i: (i, 0)),
        out_shape=jax.ShapeDtypeStruct((rows, LANES), F32), compiler_params=_params())(buf)


def _adamw(w, gsrcs, m, v, name, scale=None):
    nl, rows, cols = w.shape
    assert len(gsrcs) == nl
    s = gsrcs[0].shape[0]
    tr = _pick(rows, [128, 64, 32, 16, 8])
    blk = pl.BlockSpec((None, tr, cols), lambda l, i: (l, i, 0))
    c1 = 1.0 / (1.0 - ADAM_B1 ** ADAM_STEP)
    c2 = 1.0 / (1.0 - ADAM_B2 ** ADAM_STEP)

    def gspec(t):
        return pl.BlockSpec((s, tr, cols), lambda l, i: (0, jnp.where(l == t, i, 0), 0))

    def body(*refs):
        w_ref, g_refs, (m_ref, v_ref) = refs[0], refs[1:1 + nl], refs[1 + nl:3 + nl]
        rest = refs[3 + nl:]
        if scale is not None:
            sc_ref, rest = rest[0], rest[1:]
        go_ref, d_ref, mo_ref, vo_ref = rest
        layer = pl.program_id(0)
        g = None
        for t in range(nl):
            gt = g_refs[t][0].astype(F32)
            for q in range(1, s):
                gt = gt + g_refs[t][q].astype(F32)
            g = gt if g is None else jnp.where(layer == t, gt, g)
        if scale is not None:
            g = g * sc_ref[...]
        mn = ADAM_B1 * m_ref[...] + (1.0 - ADAM_B1) * g
        vn = ADAM_B2 * v_ref[...] + (1.0 - ADAM_B2) * (g * g)
        go_ref[...] = g
        mo_ref[...] = mn
        vo_ref[...] = vn
        d_ref[...] = -ADAM_LR * ((mn * c1) / (jnp.sqrt(vn * c2) + ADAM_EPS) + ADAM_WD * w_ref[...])

    shp = jax.ShapeDtypeStruct((nl, rows, cols), F32)
    ins = [w] + list(gsrcs) + [m, v] + ([] if scale is None else [scale])
    return pl.pallas_call(
        body, name=name, grid=(nl, rows // tr),
        in_specs=[blk] + [gspec(t) for t in range(nl)] + [blk, blk] + ([] if scale is None else [blk]),
        out_specs=[blk] * 4, out_shape=[shp] * 4, compiler_params=_params())(*ins)


def _pack(arrs):
    flat = []
    for a in arrs:
        a = a.reshape(-1).astype(F32)
        pad = (-a.shape[0]) % (8 * LANES)
        flat.append(jnp.pad(a, (0, pad)) if pad else a)
    return jnp.concatenate(flat).reshape(-1, LANES)


def _unpack(buf, shapes, lead=()):
    out, row = [], 0
    for shp in shapes:
        n = 1
        for d in shp:
            n *= d
        nrows = -(-n // (8 * LANES)) * 8
        piece = buf[..., row:row + nrows, :].reshape(lead + (nrows * LANES,))[..., :n]
        out.append(piece.reshape(lead + tuple(shp)))
        row += nrows
    return out


def _silu(x):
    return x * jax.nn.sigmoid(x)


def _rope_tables(n_lat, m_ctx):
    pos = jnp.arange(n_lat)
    row = (pos // GRID_W).astype(F32)
    col = (pos % GRID_W).astype(F32)
    half = HEAD_DIM // 2
    inv = ROPE_THETA ** (-jnp.arange(0, half, 2, dtype=F32) / half)
    ang = jnp.concatenate([row[:, None] * inv, col[:, None] * inv], axis=-1)
    cos, sin = jnp.cos(ang), jnp.sin(ang)
    cos2 = jnp.tile(jnp.concatenate([cos, cos], axis=-1), (1, LANES // HEAD_DIM))
    sin2 = jnp.tile(jnp.concatenate([-sin, sin], axis=-1), (1, LANES // HEAD_DIM))
    return (jnp.concatenate([cos2, jnp.ones((m_ctx, LANES), F32)], axis=0),
            jnp.concatenate([sin2, jnp.zeros((m_ctx, LANES), F32)], axis=0))


def _to_slots(full, axis):
    shp = full.shape
    new = shp[:axis] + (N_DEV, shp[axis] // N_DEV) + shp[axis + 1:]
    return jnp.moveaxis(full.reshape(new), axis, 0)


def _from_slots(slots, axis):
    moved = jnp.moveaxis(slots, 0, axis)
    shp = moved.shape
    return moved.reshape(shp[:axis] + (shp[axis] * shp[axis + 1],) + shp[axis + 2:])


def _permute_in(wt, name):
    segs = [(OFF_G, IN_W), (0, OFF_K), (OFF_A, OFF_B), (OFF_B, OFF_G), (OFF_K, OFF_A)]

    def body(src, dst, sems):
        copies, at = [], 0
        for t, (lo, hi) in enumerate(segs):
            copies.append(pltpu.make_async_copy(src.at[pl.ds(lo, hi - lo)], dst.at[pl.ds(at, hi - lo)], sems.at[t]))
            at += hi - lo
        for cp in copies:
            cp.start()
        for cp in copies:
            cp.wait()

    return pl.pallas_call(
        body, name=name, in_specs=[HBM_SPEC], out_specs=HBM_SPEC, out_shape=jax.ShapeDtypeStruct(wt.shape, wt.dtype),
        scratch_shapes=[pltpu.SemaphoreType.DMA((len(segs),))])(wt)


def kernel(x, c, ctx, c_ctx, w_mod, b_mod, g_mix, w_in, b_gate, sink, w_spatial, b_spatial, g_v, w_sconv, w_branch, w_out, g_ffn, w_up, w_fconv, w_down, g_final, loss_target, m_c_ctx, m_w_mod, m_b_mod, m_g_mix, m_w_in, m_b_gate, m_sink, m_w_spatial, m_b_spatial, m_g_v, m_w_sconv, m_w_branch, m_w_out, m_g_ffn, m_w_up, m_w_fconv, m_w_down, m_g_final, v_c_ctx, v_w_mod, v_b_mod, v_g_mix, v_w_in, v_b_gate, v_sink, v_w_spatial, v_b_spatial, v_g_v, v_w_sconv, v_w_branch, v_w_out, v_g_ffn, v_w_up, v_w_fconv, v_w_down, v_g_final):
    n_lat, m_ctx = x.shape[1], ctx.shape[1]
    r = n_lat + m_ctx
    me = 4 * lax.axis_index("x") + 2 * lax.axis_index("y") + lax.axis_index("c")
    mod_w = w_mod.shape[2]
    bm = _pick(r, [768, 256])

    tr = lambda a: jnp.swapaxes(a, -1, -2)
    shard_axis = {"in": 0, "br": 1, "out": 0, "up": 0, "dn": 0}
    shards = {}
    for kind, wt in (("in", tr(w_in)), ("br", w_branch), ("out", w_out), ("up", tr(w_up)), ("dn", w_down)):
        wb = wt.astype(BF16)
        for l in range(DEPTH):
            shards[kind, l] = wb[l]
    full = {}

    def arrive(items, got):
        for key, slots in zip(items, got):
            wfull = _from_slots(slots, shard_axis[key[0]])
            full[key] = _permute_in(wfull, f"permute_in{key[1]}") if key[0] == "in" else wfull

    def gather_of(items):
        return [shards[key] for key in items], ["gather"] * len(items)

    small_shapes = [c.shape, w_sconv.shape, w_fconv.shape]
    g_in0, g_small = _exchange([shards["in", 0], _pack([c, w_sconv, w_fconv])], ["gather"] * 2, "gather_first")
    arrive([("in", 0)], [g_in0])
    c_all, sconv_all, fconv_all = _unpack(g_small, small_shapes, lead=(N_DEV,))
    c_all = c_all.reshape(N_DEV, D_MODEL)
    w_sconv_full = _from_slots(sconv_all, 2)
    w_fconv_full = _from_slots(fconv_all, 2)

    act = jnp.concatenate([_silu(c_all), _silu(c_ctx)[None], jnp.zeros((7, D_MODEL), F32)], axis=0)
    mod_part = jnp.stack([_mm(act, w_mod[l], name=f"mod_fwd{l}", bm=16, bn=mod_w, bk=D_MODEL, out_dtype=F32)
                          for l in range(DEPTH)])
    (mod_all,) = _exchange([mod_part], ["gather"], "gather_mod")
    mod_full = _from_slots(mod_all, 2) + b_mod[:, None, :]
    mods = []
    for l in range(DEPTH):
        mine = lax.dynamic_index_in_dim(mod_full[l], me, axis=0, keepdims=False).reshape(6, D_MODEL)
        theirs = mod_full[l, N_DEV].reshape(6, D_MODEL)
        mods.append(jnp.pad(jnp.stack([mine, theirs]), ((0, 0), (0, 2), (0, 0))))

    cos, sin = _rope_tables(n_lat, m_ctx)
    xs = jnp.concatenate([x[0], ctx[0]], axis=0)
    ws_b = w_spatial.astype(BF16)
    ws_t = jnp.swapaxes(w_spatial, 2, 3).astype(BF16)
    vec = lambda a: a.reshape(1, -1)

    saved = []
    res = None
    for l in range(DEPTH):
        s = {}
        if res is None:
            s["x0"] = xs
            s["h"] = _norm_fwd(xs, vec(g_mix[l]), mods[l], n_lat, 0, 1, f"norm_mix{l}")
        else:
            s["x0"], s["h"] = _norm_fwd(xs, vec(g_mix[l]), mods[l], n_lat, 0, 1, f"norm_mix{l}", res=res)
        items = [("br", l), ("out", l), ("up", l)]
        s["z"], got = _mm(s["h"], full["in", l], tb=True, name=f"proj_in{l}", bm=bm, bn=_pick(IN_W, [2432, 512]),
                          bk=D_MODEL, out_dtype=BF16, cols_outer=True, carry=gather_of(items))
        arrive(items, got)
        s["qkv"] = _rope_fwd(s["z"], cos, sin, f"rope{l}")
        items = [("dn", l)] + ([("in", l + 1)] if l + 1 < DEPTH else [])
        (s["y0"], s["lse"]), got = _attn_fwd(s["qkv"], sink[l], n_lat, m_ctx, f"attn{l}", carry=gather_of(items))
        arrive(items, got)
        s["bias"] = jnp.repeat(b_spatial[l].T, LANES, axis=1)
        s["y1"] = _gating_fwd(s["z"], ws_b[l], s["bias"], vec(g_v[l]), f"gating{l}")
        s["y2"] = _sconv_fwd(s["z"], w_sconv_full[l], n_lat, f"sconv{l}")
        s["p"] = [_mm(s[f"y{t}"], full["br", l], b_lead=t, name=f"branch{l}_{t}", bm=bm, bn=D_MODEL, bk=D_MODEL,
                      out_dtype=BF16) for t in range(3)]
        s["merged"] = _merge_fwd(s["z"], vec(b_gate[l]), s["p"], f"merge{l}")
        s["o"] = _mm(s["merged"], full["out", l], name=f"proj_out{l}", bm=bm, bn=D_MODEL, bk=D_MODEL, out_dtype=F32)
        s["x1"], s["h2"] = _norm_fwd(s["x0"], vec(g_ffn[l]), mods[l], n_lat, 3, 4, f"norm_ffn{l}",
                                     res=(s["o"], mods[l], 2))
        s["up"] = _mm(s["h2"], full["up", l], tb=True, name=f"ffn_up{l}", bm=bm, bn=_pick(2 * D_FF, [1408]),
                      bk=D_MODEL, out_dtype=BF16, cols_outer=True)
        s["f"] = _ffn_conv_fwd(s["up"], w_fconv_full[l], n_lat, f"ffn_conv{l}")
        s["dd"] = _mm(s["f"], full["dn", l], name=f"ffn_down{l}", bm=bm, bn=D_MODEL, bk=D_FF, out_dtype=F32)
        saved.append(s)
        xs, res = s["x1"], (s["dd"], mods[l], 5)

    top = saved[DEPTH - 1]
    dxs, d_dd, acc_final, accs_top = _loss_bwd(top["x1"], top["dd"], mods[DEPTH - 1], vec(g_final), loss_target[0],
                                               n_lat, "loss")
    loss_part = acc_final[1, 0]
    dg_final = acc_final[0]
    dmods = [None] * DEPTH
    gate2 = accs_top[:, 2]
    grads = {k: [None] * DEPTH for k in ("g_mix", "g_ffn", "g_v", "b_gate", "sink", "w_spatial", "b_spatial",
                                         "w_sconv", "w_fconv", "w_in", "w_branch", "w_out", "w_up", "w_down")}
    bk_r = _pick(r, [1408, 768, 256])
    small_names = ["g_ffn", "g_v", "b_gate", "sink", "w_spatial", "b_spatial", "w_sconv", "w_fconv"]
    recv = {}
    small_recv, small_shapes_of = {}, {}

    def small_pack(l):
        arrs = [grads[k][l] for k in small_names]
        if l > 0:
            arrs.append(grads["g_mix"][l])
        if l == DEPTH - 1:
            arrs += [loss_part.reshape(1), dg_final]
        small_shapes_of[l] = [a.shape for a in arrs]
        return _pack(arrs)

    for l in reversed(range(DEPTH)):
        s = saved[l]
        df = _mm(d_dd, full["dn", l], tb=True, name=f"d_ffn_down{l}", bm=bm, bn=_pick(D_FF, [1408]), bk=D_MODEL,
                 out_dtype=BF16)
        grads["w_down"][l] = _mm(s["f"], d_dd, ta=True, name=f"g_ffn_down{l}", bm=_pick(D_FF, [1408]), bn=D_MODEL,
                                 bk=bk_r, out_dtype=BF16)
        carry = None if l + 1 == DEPTH else ([_to_slots(grads["w_in"][l + 1], 0)], ["a2a"])
        if carry is None:
            dup, dwf = _ffn_conv_bwd(s["up"], w_fconv_full[l], df, n_lat, f"d_ffn_conv{l}")
        else:
            (dup, dwf), (recv["in", l + 1],) = _ffn_conv_bwd(s["up"], w_fconv_full[l], df, n_lat, f"d_ffn_conv{l}",
                                                            carry=carry)
        grads["w_fconv"][l] = dwf[:3]
        dh2 = _mm_pieces([(dup, "step", 0, 2, 0, D_FF)], full["up", l], w_t=True, name=f"d_ffn_up{l}", bm=bm)
        kb = 1408
        nbh = D_FF // kb
        grads["w_up"][l] = _mm(
            dup, s["h2"], ta=True, name=f"g_ffn_up{l}", bm=kb, bn=D_MODEL, bk=bk_r, out_dtype=BF16,
            a_spec=((None, bk_r, kb), lambda i, j, k: (i // nbh, k, i % nbh), 2 * D_FF))
        dx1, d_o, acc, accs = _norm_bwd(s["x1"], vec(g_ffn[l]), mods[l], dh2, dxs, n_lat, 3, 4, f"d_norm_ffn{l}",
                                        res=(s["o"], mods[l], 2))
        grads["g_ffn"][l] = acc[0]
        shift2, scale2, gate1 = accs[:, 0], accs[:, 1], accs[:, 2]
        dmerged = _mm(d_o, full["out", l], tb=True, name=f"d_proj_out{l}", bm=bm, bn=D_MODEL, bk=D_MODEL,
                      out_dtype=BF16)
        grads["w_out"][l] = _mm(s["merged"], d_o, ta=True, name=f"g_proj_out{l}", bm=D_MODEL, bn=D_MODEL, bk=bk_r,
                                out_dtype=BF16)
        dp0, dp1, dp2, dz_g, dbg = _merge_bwd(s["z"], vec(b_gate[l]), s["p"], dmerged, f"d_merge{l}")
        grads["b_gate"][l] = dbg[0]
        dps = (dp0, dp1, dp2)
        dys = [_mm(dps[t], full["br", l], tb=True, b_lead=t, name=f"d_branch{l}_{t}", bm=bm, bn=D_MODEL, bk=D_MODEL,
                   out_dtype=BF16) for t in range(3)]
        grads["w_branch"][l] = jnp.stack(
            [_mm(s[f"y{t}"], dps[t], ta=True, name=f"g_branch{l}_{t}", bm=D_MODEL, bn=D_MODEL, bk=bk_r,
                 out_dtype=BF16) for t in range(3)])
        arrs = [_to_slots(grads["w_down"][l], 0), _to_slots(grads["w_up"][l], 0), _to_slots(grads["w_out"][l], 0),
                _to_slots(grads["w_branch"][l], 1)]
        modes = ["a2a"] * 4
        if l + 1 < DEPTH:
            arrs.append(small_pack(l + 1))
            modes.append("gather")
        (dq, dk, dv, dsk), got = _attn_bwd(s["qkv"], sink[l], s["y0"], dys[0], s["lse"], n_lat, m_ctx, f"d_attn{l}",
                                           carry=(arrs, modes))
        recv["dn", l], recv["up", l], recv["out", l], recv["br", l] = got[:4]
        if l + 1 < DEPTH:
            small_recv[l + 1] = got[4]
        grads["sink"][l] = dsk[:, 0, :8].reshape(N_HEADS)
        dz_qkv = _rope_bwd(dq, dk, dv, cos, sin, f"d_rope{l}")
        dz_a, dws, dbs, dgv = _gating_bwd(s["z"], ws_b[l], ws_t[l], s["bias"], vec(g_v[l]), dys[1], f"d_gating{l}")
        grads["w_spatial"][l], grads["b_spatial"][l], grads["g_v"][l] = dws, dbs.T, dgv[0]
        dz_b, dwsc = _sconv_bwd(s["z"], w_sconv_full[l], dys[2], n_lat, f"d_sconv{l}")
        grads["w_sconv"][l] = dwsc[:3]
        kvw = 2 * N_KV_HEADS * HEAD_DIM
        pieces = [(dz_g, None, 0, 3, 0, D_MODEL), (dz_qkv, None, 0, 1, P_Q // D_MODEL, D_MODEL),
                  (dz_a, None, 0, 2, P_A // D_MODEL, D_MODEL), (dz_b, "step", 0, 3, P_B // D_MODEL, D_MODEL),
                  (dz_qkv, None, D_MODEL // kvw, 1, P_KV // kvw, kvw)]
        gw = lambda a, nm, rows, **kw: _mm(a, s["h"], ta=True, name=f"g_proj_in{l}_{nm}", bm=rows, bn=D_MODEL,
                                           bk=bk_r, out_dtype=BF16, **kw)
        gw_g = gw(dz_g, "gate", 1536)
        gw_qkv = gw(dz_qkv, "qkv", 1536)
        gw_a = gw(dz_a, "gating", 1024)
        gw_b = [gw(dz_b, f"sconv{t}", 1024, a_lead=t) for t in range(3)]
        grads["w_in"][l] = jnp.concatenate([gw_qkv, gw_a] + gw_b + [gw_g], axis=0)
        bm_in = _pick(r, [1056, 768, 256])
        if l > 0:
            dh = _mm_pieces(pieces, full["in", l], w_t=True, name=f"d_proj_in{l}", bm=bm_in)
        else:
            dh, (recv["in", 0], small_recv[0]) = _mm_pieces(
                pieces, full["in", l], w_t=True, name=f"d_proj_in{l}", bm=bm_in,
                carry=([_to_slots(grads["w_in"][0], 0), small_pack(0)], ["a2a", "gather"]))
        below = None if l == 0 else (saved[l - 1]["dd"], mods[l - 1], 5)
        outs = _norm_bwd(s["x0"], vec(g_mix[l]), mods[l], dh, dx1, n_lat, 0, 1, f"d_norm_mix{l}", res=below)
        if below is None:
            dxs, acc, accs = outs
        else:
            dxs, d_dd, acc, accs = outs
        grads["g_mix"][l] = acc[0]
        dmods[l] = jnp.stack([accs[:, 0], accs[:, 1], gate1, shift2, scale2, gate2], axis=1)
        gate2 = accs[:, 2]

    dmod_own = jnp.stack([dmods[l][0].reshape(-1) for l in range(DEPTH)])
    dmod_ctx = jnp.stack([dmods[l][1].reshape(-1) for l in range(DEPTH)])
    late = [grads["g_mix"][0], dmod_own + dmod_ctx, dmod_ctx, dmod_own]
    late_shapes = [a.shape for a in late]
    (late_all,) = _exchange([_pack(late)], ["gather"], "gather_late_grads")
    g_mix0_g, b_mod_g, dmodc_tot, _ = _unpack(_sum_slots(late_all, "sum_late_grads"), late_shapes)
    dmod_all = _unpack(late_all, late_shapes, lead=(N_DEV,))[-1]
    layer_sums = [_unpack(_sum_slots(small_recv[l], f"sum_small_grads{l}"), small_shapes_of[l]) for l in range(DEPTH)]
    by_name = {k: jnp.stack([layer_sums[l][t] for l in range(DEPTH)]) for t, k in enumerate(small_names)}
    g_mix_g = jnp.stack([g_mix0_g] + [layer_sums[l][len(small_names)] for l in range(1, DEPTH)])
    loss_sum, g_final_g = layer_sums[DEPTH - 1][-2], layer_sums[DEPTH - 1][-1]
    g_ffn_g, g_v_g, b_gate_g, sink_g = by_name["g_ffn"], by_name["g_v"], by_name["b_gate"], by_name["sink"]
    w_spatial_g, b_spatial_g = by_name["w_spatial"], by_name["b_spatial"]
    w_sconv_g = lax.dynamic_slice_in_dim(by_name["w_sconv"], me * w_sconv.shape[2], w_sconv.shape[2], axis=2)
    w_fconv_g = lax.dynamic_slice_in_dim(by_name["w_fconv"], me * w_fconv.shape[2], w_fconv.shape[2], axis=2)

    dmod_cols = lax.dynamic_slice_in_dim(dmod_all, me * mod_w, mod_w, axis=2)
    dmodc_cols = lax.dynamic_slice_in_dim(dmodc_tot, me * mod_w, mod_w, axis=1)
    g_w_mod, cctx_part = [], None
    for l in range(DEPTH):
        rhs = jnp.concatenate([dmod_cols[:, l], dmodc_cols[l][None], jnp.zeros((7, mod_w), F32)], axis=0)
        g_w_mod.append(_mm(act, rhs, ta=True, name=f"g_mod{l}", bm=D_MODEL, bn=mod_w, bk=16, out_dtype=F32))
        lhs = jnp.pad(dmodc_cols[l][None], ((0, 7), (0, 0)))
        part = _mm(lhs, w_mod[l], tb=True, name=f"d_cctx{l}", bm=8, bn=D_MODEL, bk=mod_w, out_dtype=F32)
        cctx_part = part if cctx_part is None else cctx_part + part
    sg = jax.nn.sigmoid(c_ctx)
    dsilu = (sg * (1.0 + c_ctx * (1.0 - sg))).reshape(8, LANES)

    (r_cctx,) = _exchange([cctx_part[0].reshape(8, LANES)], ["gather"], "gather_c_ctx_grad")

    def per_layer(a):
        return a.reshape(a.shape[0], -1, a.shape[-1])

    upd = {}
    for nm, kind, wv, mv, vv in (("w_in", "in", w_in, m_w_in, v_w_in), ("w_branch", "br", w_branch, m_w_branch, v_w_branch),
                                 ("w_out", "out", w_out, m_w_out, v_w_out), ("w_up", "up", w_up, m_w_up, v_w_up),
                                 ("w_down", "dn", w_down, m_w_down, v_w_down), ("w_mod", None, w_mod, m_w_mod, v_w_mod)):
        if kind is None:
            gsrcs = [g[None] for g in g_w_mod]
        else:
            gsrcs = [per_layer(recv[kind, l]) for l in range(DEPTH)]
        if kind in ("in", "up"):
            outs = _adamw(tr(wv), gsrcs, tr(mv), tr(vv), f"adamw_{nm}")
            upd[nm] = [tr(o) for o in outs]
        else:
            outs = _adamw(per_layer(wv), gsrcs, per_layer(mv), per_layer(vv), f"adamw_{nm}")
            upd[nm] = [o.reshape(wv.shape) for o in outs]
    as_tile = lambda a: a.reshape(1, 8, LANES)
    upd["c_ctx"] = [o.reshape(D_MODEL) for o in _adamw(
        as_tile(c_ctx), [r_cctx], as_tile(m_c_ctx), as_tile(v_c_ctx), "adamw_c_ctx", scale=as_tile(dsilu))]

    names = ["b_mod", "g_mix", "b_gate", "sink", "w_spatial", "b_spatial", "g_v", "w_sconv", "g_ffn", "w_fconv",
             "g_final"]
    w_s = [b_mod, g_mix, b_gate, sink, w_spatial, b_spatial, g_v, w_sconv, g_ffn, w_fconv, g_final]
    g_s = [b_mod_g, g_mix_g, b_gate_g, sink_g, w_spatial_g, b_spatial_g, g_v_g, w_sconv_g, g_ffn_g, w_fconv_g,
           g_final_g]
    m_s = [m_b_mod, m_g_mix, m_b_gate, m_sink, m_w_spatial, m_b_spatial, m_g_v, m_w_sconv, m_g_ffn, m_w_fconv,
           m_g_final]
    v_s = [v_b_mod, v_g_mix, v_b_gate, v_sink, v_w_spatial, v_b_spatial, v_g_v, v_w_sconv, v_g_ffn, v_w_fconv,
           v_g_final]
    shapes = [a.shape for a in w_s]
    packed = _adamw(_pack(w_s)[None], [_pack(g_s)[None]], _pack(m_s)[None], _pack(v_s)[None], "adamw_small")
    unpacked = [_unpack(o[0], shapes) for o in packed]
    for t, nm in enumerate(names):
        upd[nm] = [unpacked[q][t] for q in range(4)]

    order = ["c_ctx", "w_mod", "b_mod", "g_mix", "w_in", "b_gate", "sink", "w_spatial", "b_spatial", "g_v", "w_sconv",
             "w_branch", "w_out", "g_ffn", "w_up", "w_fconv", "w_down", "g_final"]
    result = [loss_sum.reshape(()), dxs[None]]
    for q in range(4):
        result += [upd[nm][q] for nm in order]
    return tuple(result)
```

```python
import jax
import jax.numpy as jnp
from jax import lax
from jax.experimental import pallas as pl
from jax.experimental.pallas import tpu as pltpu

F32, BF16 = jnp.float32, jnp.bfloat16

D_MODEL = 1024
DEPTH = 2
GRID_W = 64
N_HEADS = 16
N_KV_HEADS = 4
HEAD_DIM = 64
WINDOW = 128
BLK = 128
ROPE_THETA = 10000.0
CHUNK = 128
A_GROUPS = 8
D_FF = 2816
EPS = 1e-6
NEG = -1e30
IN_W = 9728
OFF_K, OFF_A, OFF_B, OFF_G = 1024, 1536, 3584, 6656
P_Q, P_A, P_B, P_KV = 3072, 4096, 6144, 9216

N_DEV = 8
LANES = 128
ROW_TILE = 256
CONV_CHUNK = 256
PAD = 8
VMEM_LIMIT = 52 * 1024 * 1024

ADAM_LR, ADAM_B1, ADAM_B2, ADAM_EPS, ADAM_WD, ADAM_STEP = 0.001, 0.9, 0.999, 1e-08, 0.01, 10

HBM_SPEC = pl.BlockSpec(memory_space=pltpu.HBM)
SMEM_SPEC = pl.BlockSpec(memory_space=pltpu.SMEM)


def _params():
    return pltpu.CompilerParams(vmem_limit_bytes=VMEM_LIMIT)


def _pick(n, prefs):
    for p in prefs:
        if n % p == 0:
            return p
    raise ValueError((n, prefs))


def _sigmoid(x):
    return 0.5 * jnp.tanh(0.5 * x) + 0.5


def _mm(a, b, *, name, ta=False, tb=False, bm, bn, bk, out_dtype, a_lead=None, b_lead=None, a_spec=None,
        b_spec=None, cols_outer=False, carry=None):
    ash = a.shape[1:] if a_lead is not None else a.shape
    bsh = b.shape[1:] if b_lead is not None else b.shape
    kc = (bsh[1] if tb else bsh[0]) if b_spec is None else (ash[0] if ta else ash[1])
    mo =(ash[1] if ta else ash[0]) if a_spec is None else a_spec[2]
    no = (bsh[0] if tb else bsh[1]) if b_spec is None else b_spec[2]
    assert mo % bm == 0 and no % bn == 0 and kc % bk == 0, (name, mo, no, kc, bm, bn, bk)
    nk = kc // bk

    def spec(shape, fn, idx=None):
        if idx is not None:
            shape, inner = (None,) + shape, fn
            fn = lambda i, j, k: (idx,) + inner(i, j, k)
        if cols_outer:
            return pl.BlockSpec(shape, lambda j, i, k: fn(i, j, k))
        return pl.BlockSpec(shape, fn)

    if a_spec is not None:
        a_bs = spec(a_spec[0], a_spec[1])
    elif ta:
        a_bs = spec((bk, bm), lambda i, j, k: (k, i), a_lead)
    else:
        a_bs = spec((bm, bk), lambda i, j, k: (i, k), a_lead)
    if b_spec is not None:
        b_bs = spec(b_spec[0], b_spec[1])
    elif tb:
        b_bs = spec((bn, bk), lambda i, j, k: (j, k), b_lead)
    else:
        b_bs = spec((bk, bn), lambda i, j, k: (k, j), b_lead)
    dims = (((0 if ta else 1,), (1 if tb else 0,)), ((), ()))
    grid = (no // bn, mo // bm, nk) if cols_outer else (mo // bm, no // bn, nk)

    def body(a_ref, b_ref, o_ref, *scratch):
        if nk == 1:
            o_ref[...] = lax.dot_general(a_ref[...], b_ref[...], dims, preferred_element_type=F32).astype(o_ref.dtype)
        else:
            acc = scratch[0]
            k = pl.program_id(2)

            @pl.when(k == 0)
            def _():
                acc[...] = jnp.zeros_like(acc)

            acc[...] += lax.dot_general(a_ref[...], b_ref[...], dims, preferred_element_type=F32)

            @pl.when(k == nk - 1)
            def _():
                o_ref[...] = acc[...].astype(o_ref.dtype)

    outs, carried = _pcall(
        body, (a, b), name=name, grid=grid, in_specs=[a_bs, b_bs],
        out_specs=[spec((bm, bn), lambda i, j, k: (i, j))],
        out_shape=[jax.ShapeDtypeStruct((mo, no), out_dtype)],
        scratch_shapes=[pltpu.VMEM((bm, bn), F32)] if nk > 1 else [], carry=carry)
    return outs[0] if carry is None else (outs[0], carried)


def _mm_pieces(pieces, w, *, name, bm, w_t=False, carry=None):
    kout = w.shape[1] if w_t else w.shape[0]
    starts, total = [], 0
    for piece in pieces:
        starts.append(total)
        total += piece[3]
    mo = pieces[0][0].shape[-2]
    assert mo % bm == 0
    widths = sorted({piece[5] for piece in pieces}, reverse=True)

    def inside(p, k):
        return (k >= starts[p]) & (k < starts[p] + pieces[p][3])

    def a_spec(p):
        _, lead, col0, nblk, _, bk = pieces[p]

        def fn(i, k):
            t = jnp.clip(k - starts[p], 0, nblk - 1)
            if lead is None:
                return (i, col0 + t)
            return (t, i, col0) if lead == "step" else (lead, i, col0 + t)
        return pl.BlockSpec((bm, bk) if lead is None else (None, bm, bk), fn)

    def w_spec(bk):
        def fn(i, k):
            col = 0
            for p, piece in enumerate(pieces):
                if piece[5] == bk:
                    col = col + jnp.where(inside(p, k), piece[4] + k - starts[p], 0)
            return (col, 0) if w_t else (0, col)
        return pl.BlockSpec((bk, kout) if w_t else (kout, bk), fn)

    n_p, n_w = len(pieces), len(widths)

    def body(*refs):
        a_refs, w_refs, o_ref, acc = refs[:n_p], refs[n_p:n_p + n_w], refs[n_p + n_w], refs[n_p + n_w + 1]
        k = pl.program_id(1)

        @pl.when(k == 0)
        def _():
            acc[...] = jnp.zeros_like(acc)

        for p in range(n_p):
            w_ref = w_refs[widths.index(pieces[p][5])]

            @pl.when(inside(p, k))
            def _(p=p, w_ref=w_ref):
                acc[...] += lax.dot_general(a_refs[p][...], w_ref[...], NN if w_t else NT,
                                            preferred_element_type=F32)

        @pl.when(k == total - 1)
        def _():
            o_ref[...] = acc[...]

    outs, carried = _pcall(
        body, [piece[0] for piece in pieces] + [w] * n_w, name=name, grid=(mo // bm, total),
        in_specs=[a_spec(p) for p in range(n_p)] + [w_spec(bk) for bk in widths],
        out_specs=[pl.BlockSpec((bm, kout), lambda i, k: (i, 0))],
        out_shape=[jax.ShapeDtypeStruct((mo, kout), F32)],
        scratch_shapes=[pltpu.VMEM((bm, kout), F32)], carry=carry)
    return outs[0] if carry is None else (outs[0], carried)


def _xchg_out_shapes(arrs, modes):
    return [jax.ShapeDtypeStruct((N_DEV,) + a.shape if m == "gather" else a.shape, a.dtype)
            for a, m in zip(arrs, modes)]


def _xchg_sems(n):
    return [pltpu.SemaphoreType.DMA((n, N_DEV - 1)), pltpu.SemaphoreType.DMA((n, N_DEV - 1)),
            pltpu.SemaphoreType.DMA((n,))]


def _xchg_copies(ins, outs, modes, sems):
    send_sems, recv_sems, local_sems = sems
    x, y, c = lax.axis_index("x"), lax.axis_index("y"), lax.axis_index("c")
    me = 4 * x + 2 * y + c

    def place(q):
        px = 1 - x if (q >> 2) & 1 else x
        py = 1 - y if (q >> 1) & 1 else y
        pc = 1 - c if q & 1 else c
        return (px, py, pc), 4 * px + 2 * py + pc

    sibling, _ = place(1)
    out = dict(local=[], direct=[], landed=[], passed=[], others=[])
    for t, mode in enumerate(modes):
        gather = mode == "gather"
        mine = ins[t] if gather else ins[t].at[me]
        out["local"].append(pltpu.make_async_copy(mine, outs[t].at[me], local_sems.at[t]))
        for q in range(1, N_DEV):
            peer, slot = place(q)
            sem = dict(send_sem=send_sems.at[t, q - 1], recv_sem=recv_sems.at[t, q - 1],
                       device_id_type=pl.DeviceIdType.MESH)
            arrival = pltpu.make_async_remote_copy(src_ref=outs[t].at[slot], dst_ref=outs[t].at[slot],
                                                   device_id=peer, **sem)
            if not gather:
                out["direct"].append(pltpu.make_async_remote_copy(src_ref=ins[t].at[slot], dst_ref=outs[t].at[me],
                                                                  device_id=peer, **sem))
                out["others"].append(arrival)
            elif q == 1 or q % 2 == 0:
                out["direct"].append(pltpu.make_async_remote_copy(src_ref=ins[t], dst_ref=outs[t].at[me],
                                                                  device_id=peer, **sem))
                out["others" if q == 1 else "landed"].append(arrival)
            else:
                _, origin = place(q - 1)
                out["passed"].append(pltpu.make_async_remote_copy(src_ref=outs[t].at[origin], dst_ref=outs[t].at[origin],
                                                                  device_id=sibling, **sem))
                out["others"].append(arrival)
    return out


def _xchg_start(copies):
    for cp in copies["local"] + copies["direct"]:
        cp.start()


def _xchg_wait(copies):
    for cp in copies["landed"]:
        cp.wait_recv()
    for cp in copies["passed"]:
        cp.start()
    for cp in copies["others"]:
        cp.wait_recv()
    for cp in copies["direct"] + copies["passed"]:
        cp.wait_send()
    for cp in copies["local"]:
        cp.wait()


def _exchange(arrs, modes, name):
    n = len(arrs)

    def body(*refs):
        copies = _xchg_copies(refs[:n], refs[n:2 * n], modes, refs[2 * n:])
        _xchg_start(copies)
        _xchg_wait(copies)

    outs = pl.pallas_call(
        body, name=name, in_specs=[HBM_SPEC] * n, out_specs=[HBM_SPEC] * n, out_shape=_xchg_out_shapes(arrs, modes),
        scratch_shapes=_xchg_sems(n), compiler_params=pltpu.CompilerParams(has_side_effects=True),
    )(*arrs)
    return list(outs)


def _pcall(body, operands, *, name, grid, in_specs, out_specs, out_shape, scratch_shapes=(), carry=None):
    out_specs, out_shape, scratch_shapes = list(out_specs), list(out_shape), list(scratch_shapes)
    if carry is None:
        outs = pl.pallas_call(body, name=name, grid=grid, in_specs=in_specs, out_specs=out_specs,
                              out_shape=out_shape, scratch_shapes=scratch_shapes, compiler_params=_params())(*operands)
        return list(outs), []
    arrs, modes = carry
    n, n_in, n_out, n_scr = len(arrs), len(in_specs), len(out_specs), len(scratch_shapes)

    def wrapped(*refs):
        ins, c_in = refs[:n_in], refs[n_in:n_in + n]
        outs, c_out = refs[n_in + n:n_in + n + n_out], refs[n_in + n + n_out:n_in + 2 * n + n_out]
        rest = refs[n_in + 2 * n + n_out:]
        scr, sems = rest[:n_scr], rest[n_scr:]
        first, last = None, None
        for d, size in enumerate(grid):
            f, e = pl.program_id(d) == 0, pl.program_id(d) == size - 1
            first = f if first is None else first & f
            last = e if last is None else last & e

        @pl.when(first)
        def _():
            _xchg_start(_xchg_copies(c_in, c_out, modes, sems))

        body(*ins, *outs, *scr)

        @pl.when(last)
        def _():
            _xchg_wait(_xchg_copies(c_in, c_out, modes, sems))

    outs = pl.pallas_call(
        wrapped, name=name, grid=grid, in_specs=list(in_specs) + [HBM_SPEC] * n,
        out_specs=out_specs + [HBM_SPEC] * n, out_shape=out_shape + _xchg_out_shapes(arrs, modes),
        scratch_shapes=scratch_shapes + _xchg_sems(n), compiler_params=_params())(*operands, *arrs)
    return list(outs[:n_out]), list(outs[n_out:])


def _row_specs(r, n_lat, tm):
    nbl = n_lat // tm
    row = pl.BlockSpec((tm, D_MODEL), lambda i: (i, 0))
    mod = pl.BlockSpec((None, 8, D_MODEL), lambda i: (i // nbl, 0, 0))
    vec = pl.BlockSpec((1, D_MODEL), lambda i: (0, 0))
    return nbl, row, mod, vec


def _rows(vals):
    width = [v for v in vals if v is not None][0].shape[1]
    return jnp.concatenate([jnp.zeros((1, width), F32) if v is None else v for v in vals], axis=0)


def _norm_fwd(xs, g, mods, n_lat, sh, sc, name, res=None):
    r = xs.shape[0]
    tm = ROW_TILE
    nbl, row, mod, vec = _row_specs(r, n_lat, tm)

    def norm(x, g_ref, m_ref, h_ref):
        rs = lax.rsqrt(jnp.mean(x * x, axis=-1, keepdims=True) + EPS)
        m = m_ref[...]
        h_ref[...] = ((x * rs * g_ref[...]) * (1.0 + m[sc:sc + 1]) + m[sh:sh + 1]).astype(BF16)

    if res is None:
        def body(x_ref, g_ref, m_ref, h_ref):
            norm(x_ref[...], g_ref, m_ref, h_ref)
        return pl.pallas_call(
            body, name=name, grid=(r // tm,), in_specs=[row, vec, mod], out_specs=row,
            out_shape=jax.ShapeDtypeStruct((r, D_MODEL), BF16), compiler_params=_params())(xs, g, mods)

    o, mods_res, gt = res

    def body(x_ref, o_ref, mr_ref, g_ref, m_ref, x1_ref, h_ref):
        x = x_ref[...] + mr_ref[...][gt:gt + 1] * o_ref[...]
        x1_ref[...] = x
        norm(x, g_ref, m_ref, h_ref)

    return pl.pallas_call(
        body, name=name, grid=(r // tm,), in_specs=[row, row, mod, vec, mod], out_specs=[row, row],
        out_shape=[jax.ShapeDtypeStruct((r, D_MODEL), F32), jax.ShapeDtypeStruct((r, D_MODEL), BF16)],
        compiler_params=_params())(xs, o, mods_res, g, mods)


def _rms_bwd(x, g, dy):
    rs = lax.rsqrt(jnp.mean(x * x, axis=-1, keepdims=True) + EPS)
    xh = x * rs
    dxh = dy * g
    dx = rs * (dxh - xh * jnp.mean(dxh * xh, axis=-1, keepdims=True))
    return dx, dy * xh, xh


def _acc_specs(nbl):
    acc_all = pl.BlockSpec((8, D_MODEL), lambda i: (0, 0))
    acc_stream = pl.BlockSpec((None, 8, D_MODEL), lambda i: (i // nbl, 0, 0))
    return acc_all, acc_stream


def _loss_bwd(xs, o, mods, g_final, target, n_lat, name):
    r = xs.shape[0]
    tm = ROW_TILE
    nbl, row, mod, vec = _row_specs(r, n_lat, tm)
    acc_all, acc_stream = _acc_specs(nbl)
    tgt = pl.BlockSpec((tm, D_MODEL), lambda i: (jnp.minimum(i, nbl - 1), 0))

    def body(x_ref, o_ref, m_ref, g_ref, t_ref, dx_ref, do_ref, acc_ref, accs_ref):
        i = pl.program_id(0)
        lat = i < nbl
        gate = m_ref[...][5:6]
        o_val = o_ref[...]
        x = x_ref[...] + gate * o_val
        g = g_ref[...]
        rs = lax.rsqrt(jnp.mean(x * x, axis=-1, keepdims=True) + EPS)
        y = x * rs * g
        err = jnp.where(lat, y - t_ref[...], 0.0)
        loss = 0.5 * jnp.sum(jnp.mean(err * err, axis=-1, keepdims=True), axis=0, keepdims=True)
        dy = err * (1.0 / D_MODEL)
        dx, dg_rows, _ = _rms_bwd(x, g, dy)
        dx_ref[...] = dx
        do_ref[...] = (gate * dx).astype(BF16)

        @pl.when(i == 0)
        def _():
            acc_ref[...] = jnp.zeros_like(acc_ref)

        @pl.when((i == 0) | (i == nbl))
        def _():
            accs_ref[...] = jnp.zeros_like(accs_ref)

        acc_ref[...] += _rows([jnp.sum(dg_rows, axis=0, keepdims=True), jnp.broadcast_to(loss, (1, D_MODEL))]
                              + [None] * 6)
        accs_ref[...] += _rows([None, None, jnp.sum(dx * o_val, axis=0, keepdims=True)] + [None] * 5)

    return pl.pallas_call(
        body, name=name, grid=(r // tm,), in_specs=[row, row, mod, vec, tgt],
        out_specs=[row, row, acc_all, acc_stream],
        out_shape=[jax.ShapeDtypeStruct((r, D_MODEL), F32), jax.ShapeDtypeStruct((r, D_MODEL), BF16),
                   jax.ShapeDtypeStruct((8, D_MODEL), F32), jax.ShapeDtypeStruct((2, 8, D_MODEL), F32)],
        compiler_params=_params())(xs, o, mods, g_final, target)


def _norm_bwd(xs, g, mods, dh, dx_in, n_lat, sh, sc, name, res=None):
    r = xs.shape[0]
    tm = ROW_TILE
    nbl, row, mod, vec = _row_specs(r, n_lat, tm)
    acc_all, acc_stream = _acc_specs(nbl)
    has_res = res is not None

    def body(*refs):
        if has_res:
            x_ref, g_ref, m_ref, dh_ref, dxi_ref, o_ref, mr_ref, dx_ref, do_ref, acc_ref, accs_ref = refs
        else:
            x_ref, g_ref, m_ref, dh_ref, dxi_ref, dx_ref, acc_ref, accs_ref = refs
        i = pl.program_id(0)
        x, g, m, dhv = x_ref[...], g_ref[...], m_ref[...], dh_ref[...]
        dy = dhv * (1.0 + m[sc:sc + 1])
        dxn, dg_rows, xh = _rms_bwd(x, g, dy)
        dx = dxi_ref[...] + dxn
        if has_res:
            dx_ref[...] = dx
        else:
            @pl.when(i < nbl)
            def _():
                dx_ref[...] = dx
        d_gate = None
        if has_res:
            o_val = o_ref[...]
            do_ref[...] = (mr_ref[...][res[2]:res[2] + 1] * dx).astype(BF16)
            d_gate = jnp.sum(dx * o_val, axis=0, keepdims=True)

        @pl.when(i == 0)
        def _():
            acc_ref[...] = jnp.zeros_like(acc_ref)

        @pl.when((i == 0) | (i == nbl))
        def _():
            accs_ref[...] = jnp.zeros_like(accs_ref)

        acc_ref[...] += _rows([jnp.sum(dg_rows, axis=0, keepdims=True)] + [None] * 7)
        accs_ref[...] += _rows([jnp.sum(dhv, axis=0, keepdims=True),
                                jnp.sum(dhv * (xh * g), axis=0, keepdims=True), d_gate] + [None] * 5)

    ins = [xs, g, mods, dh, dx_in]
    in_specs = [row, vec, mod, row, row]
    if has_res:
        out_specs, out_shape = [row], [jax.ShapeDtypeStruct((r, D_MODEL), F32)]
    else:
        out_specs = [pl.BlockSpec((tm, D_MODEL), lambda i: (jnp.minimum(i, nbl - 1), 0))]
        out_shape = [jax.ShapeDtypeStruct((n_lat, D_MODEL), F32)]
    if has_res:
        ins += [res[0], res[1]]
        in_specs += [row, mod]
        out_specs.append(row)
        out_shape.append(jax.ShapeDtypeStruct((r, D_MODEL), BF16))
    out_specs += [acc_all, acc_stream]
    out_shape += [jax.ShapeDtypeStruct((8, D_MODEL), F32), jax.ShapeDtypeStruct((2, 8, D_MODEL), F32)]
    return pl.pallas_call(body, name=name, grid=(r // tm,), in_specs=in_specs, out_specs=out_specs,
                          out_shape=out_shape, compiler_params=_params())(*ins)


def _rotate(t, cos, sin):
    width = t.shape[1]
    reps = width // LANES
    lane = lax.broadcasted_iota(jnp.int32, (1, width), 1)
    first = (lane % HEAD_DIM) < (HEAD_DIM // 2)
    swapped = jnp.where(first, pltpu.roll(t, width - HEAD_DIM // 2, 1), pltpu.roll(t, HEAD_DIM // 2, 1))
    return t * jnp.tile(cos, (1, reps)) + swapped * jnp.tile(sin, (1, reps))


def _rope_fwd(z, cos, sin, name):
    r = z.shape[0]
    tm = ROW_TILE
    kvw = 2 * N_KV_HEADS * HEAD_DIM
    tab = pl.BlockSpec((tm, LANES), lambda i: (i, 0))

    def body(q_ref, kv_ref, c_ref, s_ref, o_ref):
        c, s = c_ref[...], s_ref[...]
        kv = kv_ref[...]
        o_ref[:, :D_MODEL] = (_rotate(q_ref[...].astype(F32), c, s) * (HEAD_DIM ** -0.5)).astype(BF16)
        o_ref[:, D_MODEL:D_MODEL + kvw // 2] = _rotate(kv[:, :kvw // 2].astype(F32), c, s).astype(BF16)
        o_ref[:, D_MODEL + kvw // 2:] = kv[:, kvw // 2:]

    return pl.pallas_call(
        body, name=name, grid=(r // tm,),
        in_specs=[pl.BlockSpec((tm, D_MODEL), lambda i: (i, P_Q // D_MODEL)),
                  pl.BlockSpec((tm, kvw), lambda i: (i, P_KV // kvw)), tab, tab],
        out_specs=pl.BlockSpec((tm, D_MODEL + kvw), lambda i: (i, 0)),
        out_shape=jax.ShapeDtypeStruct((r, D_MODEL + kvw), BF16), compiler_params=_params())(z, z, cos, sin)


def _rope_bwd(dq, dk, dv, cos, sin, name):
    r = dq.shape[0]
    tm = ROW_TILE
    kw = N_KV_HEADS * HEAD_DIM
    tab = pl.BlockSpec((tm, LANES), lambda i: (i, 0))

    def body(dq_ref, dk_ref, dv_ref, c_ref, s_ref, o_ref):
        c, s = c_ref[...], -s_ref[...]
        o_ref[:, :D_MODEL] = (_rotate(dq_ref[...], c, s) * (HEAD_DIM ** -0.5)).astype(BF16)
        o_ref[:, D_MODEL:D_MODEL + kw] = _rotate(dk_ref[...], c, s).astype(BF16)
        o_ref[:, D_MODEL + kw:] = dv_ref[...].astype(BF16)

    return pl.pallas_call(
        body, name=name, grid=(r // tm,),
        in_specs=[pl.BlockSpec((tm, D_MODEL), lambda i: (i, 0)), pl.BlockSpec((tm, kw), lambda i: (i, 0)),
                  pl.BlockSpec((tm, kw), lambda i: (i, 0)), tab, tab],
        out_specs=pl.BlockSpec((tm, D_MODEL + 2 * kw), lambda i: (i, 0)),
        out_shape=jax.ShapeDtypeStruct((r, D_MODEL + 2 * kw), BF16), compiler_params=_params())(dq, dk, dv, cos, sin)


def _attn_setup(i, n_lat, m_ctx, nbl, k_ref, v_ref):
    start = pl.multiple_of(jnp.clip((i - 1) * BLK, 0, n_lat - 3 * BLK), BLK)
    nkeys = 3 * BLK + m_ctx
    rows = lax.broadcasted_iota(jnp.int32, (4 * BLK, nkeys), 0)
    cols = lax.broadcasted_iota(jnp.int32, (4 * BLK, nkeys), 1)
    qpos = i * BLK + (rows & (BLK - 1))
    seen = (cols >= 3 * BLK) | ((jnp.abs(start + cols - qpos) <= WINDOW) & (i < nbl))
    mask = jnp.where(seen, 0.0, NEG)
    kblk = jnp.concatenate([k_ref[pl.ds(start, 3 * BLK), :], k_ref[pl.ds(n_lat, m_ctx), :]], axis=0)
    vblk = jnp.concatenate([v_ref[pl.ds(start, 3 * BLK), :], v_ref[pl.ds(n_lat, m_ctx), :]], axis=0)
    lo = lax.broadcasted_iota(jnp.int32, (1, LANES), 1) < HEAD_DIM
    return start, mask, kblk, vblk, lo


def _stack_heads(ref, kh, lo):
    a = ref[:, (2 * kh) * LANES:(2 * kh + 1) * LANES]
    b = ref[:, (2 * kh + 1) * LANES:(2 * kh + 2) * LANES]
    z = jnp.zeros_like(a)
    return jnp.concatenate([jnp.where(lo, a, z), jnp.where(lo, z, a), jnp.where(lo, b, z), jnp.where(lo, z, b)],
                           axis=0)


def _kv_variants(blk, rolled, kh, lo):
    z = jnp.zeros_like(blk)
    if kh == 0:
        return jnp.where(lo, blk, rolled), jnp.where(lo, blk, z), jnp.where(lo, z, rolled)
    return jnp.where(lo, rolled, blk), jnp.where(lo, rolled, z), jnp.where(lo, z, blk)


NN = (((1,), (0,)), ((), ()))
NT = (((1,), (1,)), ((), ()))
TN = (((0,), (0,)), ((), ()))


def _scores(qst, k2, mask, sink_ref, p, kh):
    s = lax.dot_general(qst, k2, NT, preferred_element_type=F32) + mask
    snk = jnp.concatenate([jnp.full((BLK, 1), sink_ref[p * 8 + kh * 4 + g], F32) for g in range(4)], axis=0)
    return s, snk


def _attn_fwd(qkv, sink, n_lat, m_ctx, name, carry=None):
    r = qkv.shape[0]
    nb, nbl = r // BLK, n_lat // BLK
    kcol = D_MODEL // LANES

    def body(sink_ref, q_ref, k_ref, v_ref, o_ref, lse_ref):
        p, i = pl.program_id(0), pl.program_id(1)
        _, mask, kblk, vblk, lo = _attn_setup(i, n_lat, m_ctx, nbl, k_ref, v_ref)
        kr, vr = pltpu.roll(kblk, HEAD_DIM, 1), pltpu.roll(vblk, HEAD_DIM, 1)
        for kh in range(2):
            k2, _, _ = _kv_variants(kblk, kr, kh, lo)
            _, vlo, vhi = _kv_variants(vblk, vr, kh, lo)
            qst = _stack_heads(q_ref, kh, lo)
            s, snk = _scores(qst, k2, mask, sink_ref, p, kh)
            mx = jnp.maximum(jnp.max(s, axis=-1, keepdims=True), snk)
            pe = jnp.exp(s - mx)
            den = jnp.sum(pe, axis=-1, keepdims=True) + jnp.exp(snk - mx)
            inv = 1.0 / den
            pb = pe.astype(BF16)
            for jp in range(2):
                r0 = 2 * jp * BLK
                pair = (jnp.dot(pb[r0:r0 + BLK], vlo, preferred_element_type=F32)
                        + jnp.dot(pb[r0 + BLK:r0 + 2 * BLK], vhi, preferred_element_type=F32))
                pair = pair * jnp.where(lo, inv[r0:r0 + BLK], inv[r0 + BLK:r0 + 2 * BLK])
                o_ref[:, (2 * kh + jp) * LANES:(2 * kh + jp + 1) * LANES] = pair.astype(BF16)
            lse = mx + jnp.log(den)
            for g in range(4):
                lse_ref[:, kh * 4 + g:kh * 4 + g + 1] = lse[g * BLK:(g + 1) * BLK]

    outs, carried = _pcall(
        body, (sink, qkv, qkv, qkv), name=name, grid=(2, nb),
        in_specs=[SMEM_SPEC,
                  pl.BlockSpec((BLK, 4 * LANES), lambda p, i: (i, p)),
                  pl.BlockSpec((r, LANES), lambda p, i: (0, kcol + p)),
                  pl.BlockSpec((r, LANES), lambda p, i: (0, kcol + 2 + p))],
        out_specs=[pl.BlockSpec((BLK, 4 * LANES), lambda p, i: (i, p)),
                   pl.BlockSpec((None, BLK, 8), lambda p, i: (p, i, 0))],
        out_shape=[jax.ShapeDtypeStruct((r, D_MODEL), BF16), jax.ShapeDtypeStruct((2, r, 8), F32)], carry=carry)
    return outs if carry is None else (outs, carried)


def _attn_bwd(qkv, sink, o, do, lse, n_lat, m_ctx, name, carry=None):
    r = qkv.shape[0]
    nb, nbl = r // BLK, n_lat // BLK
    kcol = D_MODEL // LANES

    def body(sink_ref, q_ref, k_ref, v_ref, o_ref, do_ref, lse_ref, dq_ref, dk_ref, dv_ref, ds_ref):
        p, i = pl.program_id(0), pl.program_id(1)

        @pl.when(i == 0)
        def _():
            dk_ref[...] = jnp.zeros_like(dk_ref)
            dv_ref[...] = jnp.zeros_like(dv_ref)
            ds_ref[...] = jnp.zeros_like(ds_ref)

        start, mask, kblk, vblk, lo = _attn_setup(i, n_lat, m_ctx, nbl, k_ref, v_ref)
        kr, vr = pltpu.roll(kblk, HEAD_DIM, 1), pltpu.roll(vblk, HEAD_DIM, 1)
        lane = lax.broadcasted_iota(jnp.int32, (1, LANES), 1)
        dks, dvs = [], []
        for kh in range(2):
            k2, klo, khi = _kv_variants(kblk, kr, kh, lo)
            v2, _, _ = _kv_variants(vblk, vr, kh, lo)
            qst = _stack_heads(q_ref, kh, lo)
            dost = _stack_heads(do_ref, kh, lo)
            s, snk = _scores(qst, k2, mask, sink_ref, p, kh)
            lse4 = jnp.concatenate([lse_ref[:, kh * 4 + g:kh * 4 + g + 1] for g in range(4)], axis=0)
            pe = jnp.exp(s - lse4)
            dp = lax.dot_general(dost, v2, NT, preferred_element_type=F32)
            deltas = []
            for jp in range(2):
                cols = slice((2 * kh + jp) * LANES, (2 * kh + jp + 1) * LANES)
                prod = do_ref[:, cols].astype(F32) * o_ref[:, cols].astype(F32)
                deltas.append(jnp.sum(jnp.where(lo, prod, 0.0), axis=-1, keepdims=True))
                deltas.append(jnp.sum(jnp.where(lo, 0.0, prod), axis=-1, keepdims=True))
            delta = jnp.concatenate(deltas, axis=0)
            dsc = pe * (dp - delta)
            dsb, pb = dsc.astype(BF16), pe.astype(BF16)
            for jp in range(2):
                r0 = 2 * jp * BLK
                dq_ref[:, (2 * kh + jp) * LANES:(2 * kh + jp + 1) * LANES] = (
                    jnp.dot(dsb[r0:r0 + BLK], klo, preferred_element_type=F32)
                    + jnp.dot(dsb[r0 + BLK:r0 + 2 * BLK], khi, preferred_element_type=F32))
            dkf = lax.dot_general(dsb, qst, TN, preferred_element_type=F32)
            dvf = lax.dot_general(pb, dost, TN, preferred_element_type=F32)
            dks.append(dkf + pltpu.roll(dkf, HEAD_DIM, 1))
            dvs.append(dvf + pltpu.roll(dvf, HEAD_DIM, 1))
            contrib = -jnp.exp(snk - lse4) * delta
            for g in range(4):
                tot = jnp.sum(contrib[g * BLK:(g + 1) * BLK], axis=0, keepdims=True)
                ds_ref[0:1, :] += jnp.where(lane == kh * 4 + g, tot, 0.0)
        dk_blk = jnp.where(lo, dks[0], dks[1])
        dv_blk = jnp.where(lo, dvs[0], dvs[1])
        dk_ref[pl.ds(start, 3 * BLK), :] += dk_blk[:3 * BLK]
        dk_ref[pl.ds(n_lat, m_ctx), :] += dk_blk[3 * BLK:]
        dv_ref[pl.ds(start, 3 * BLK), :] += dv_blk[:3 * BLK]
        dv_ref[pl.ds(n_lat, m_ctx), :] += dv_blk[3 * BLK:]

    qspec = pl.BlockSpec((BLK, 4 * LANES), lambda p, i: (i, p))
    outs, carried = _pcall(
        body, (sink, qkv, qkv, qkv, o, do, lse), name=name, grid=(2, nb),
        in_specs=[SMEM_SPEC, qspec,
                  pl.BlockSpec((r, LANES), lambda p, i: (0, kcol + p)),
                  pl.BlockSpec((r, LANES), lambda p, i: (0, kcol + 2 + p)),
                  qspec, qspec, pl.BlockSpec((None, BLK, 8), lambda p, i: (p, i, 0))],
        out_specs=[qspec, pl.BlockSpec((r, LANES), lambda p, i: (0, p)), pl.BlockSpec((r, LANES), lambda p, i: (0, p)),
                   pl.BlockSpec((None, 8, LANES), lambda p, i: (p, 0, 0))],
        out_shape=[jax.ShapeDtypeStruct((r, D_MODEL), F32), jax.ShapeDtypeStruct((r, 2 * LANES), F32),
                   jax.ShapeDtypeStruct((r, 2 * LANES), F32), jax.ShapeDtypeStruct((2, 8, LANES), F32)], carry=carry)
    return outs if carry is None else (outs, carried)


def _gating_parts(z_ref, gv_ref):
    za = z_ref[...].astype(F32)
    zg = jax.nn.gelu(za)
    u, v = zg[:, :D_MODEL], zg[:, D_MODEL:]
    rs = lax.rsqrt(jnp.mean(v * v, axis=-1, keepdims=True) + EPS)
    return za, u, v, rs, v * rs * gv_ref[...]


def _mix(w_ref, vals):
    vb = vals.astype(BF16)
    return jnp.concatenate(
        [jnp.dot(w_ref[g], vb[:, g * LANES:(g + 1) * LANES], preferred_element_type=F32) for g in range(A_GROUPS)],
        axis=1)


def _gating_fwd(z, ws, bias, g_v, name):
    r = z.shape[0]

    def body(z_ref, w_ref, b_ref, gv_ref, y_ref):
        _, u, _, _, vn = _gating_parts(z_ref, gv_ref)
        y_ref[...] = (u * (_mix(w_ref, vn) + b_ref[...])).astype(BF16)

    return pl.pallas_call(
        body, name=name, grid=(r // CHUNK,),
        in_specs=[pl.BlockSpec((CHUNK, 2 * D_MODEL), lambda i: (i, P_A // (2 * D_MODEL))),
                  pl.BlockSpec((A_GROUPS, CHUNK, CHUNK), lambda i: (0, 0, 0)),
                  pl.BlockSpec((CHUNK, D_MODEL), lambda i: (0, 0)),
                  pl.BlockSpec((1, D_MODEL), lambda i: (0, 0))],
        out_specs=pl.BlockSpec((CHUNK, D_MODEL), lambda i: (i, 0)),
        out_shape=jax.ShapeDtypeStruct((r, D_MODEL), BF16), compiler_params=_params())(z, ws, bias, g_v)


def _gating_bwd(z, ws, ws_t, bias, g_v, dy, name):
    r = z.shape[0]

    def body(z_ref, w_ref, wt_ref, b_ref, gv_ref, dy_ref, dz_ref, dw_ref, db_ref, dg_ref):
        i = pl.program_id(0)

        @pl.when(i == 0)
        def _():
            dw_ref[...] = jnp.zeros_like(dw_ref)
            db_ref[...] = jnp.zeros_like(db_ref)
            dg_ref[...] = jnp.zeros_like(dg_ref)

        za, u, v, rs, vn = _gating_parts(z_ref, gv_ref)
        dyv = dy_ref[...].astype(F32)
        du = dyv * (_mix(w_ref, vn) + b_ref[...])
        dmixed = dyv * u
        dvn = _mix(wt_ref, dmixed)
        dmb, vnb = dmixed.astype(BF16), vn.astype(BF16)
        for g in range(A_GROUPS):
            cols = slice(g * LANES, (g + 1) * LANES)
            dw_ref[g] += lax.dot_general(dmb[:, cols], vnb[:, cols], NT, preferred_element_type=F32)
            db_ref[:, g:g + 1] += jnp.sum(dmixed[:, cols], axis=-1, keepdims=True)
        gv = gv_ref[...]
        vh = v * rs
        dg_ref[0:1, :] += jnp.sum(dvn * vh, axis=0, keepdims=True)
        dvh = dvn * gv
        dv = rs * (dvh - vh * jnp.mean(dvh * vh, axis=-1, keepdims=True))
        _, vjp = jax.vjp(jax.nn.gelu, za)
        dz_ref[...] = vjp(jnp.concatenate([du, dv], axis=1))[0].astype(BF16)

    wspec = pl.BlockSpec((A_GROUPS, CHUNK, CHUNK), lambda i: (0, 0, 0))
    return pl.pallas_call(
        body, name=name, grid=(r // CHUNK,),
        in_specs=[pl.BlockSpec((CHUNK, 2 * D_MODEL), lambda i: (i, P_A // (2 * D_MODEL))), wspec, wspec,
                  pl.BlockSpec((CHUNK, D_MODEL), lambda i: (0, 0)), pl.BlockSpec((1, D_MODEL), lambda i: (0, 0)),
                  pl.BlockSpec((CHUNK, D_MODEL), lambda i: (i, 0))],
        out_specs=[pl.BlockSpec((CHUNK, 2 * D_MODEL), lambda i: (i, 0)), wspec,
                   pl.BlockSpec((CHUNK, A_GROUPS), lambda i: (0, 0)), pl.BlockSpec((8, D_MODEL), lambda i: (0, 0))],
        out_shape=[jax.ShapeDtypeStruct((r, 2 * D_MODEL), BF16), jax.ShapeDtypeStruct((A_GROUPS, CHUNK, CHUNK), F32),
                   jax.ShapeDtypeStruct((CHUNK, A_GROUPS), F32), jax.ShapeDtypeStruct((8, D_MODEL), F32)],
        compiler_params=_params())(z, ws, ws_t, bias, g_v, dy)


def _scr_rows(j, n_lat):
    s = pl.multiple_of(j * CONV_CHUNK, CONV_CHUNK)
    shift = jnp.where(j >= n_lat // CONV_CHUNK, 2 * PAD, PAD)
    return s, pl.multiple_of(s + shift, PAD)


def _taps(scr, j, n_lat):
    c = CONV_CHUNK
    s, at = _scr_rows(j, n_lat)
    ext = scr[pl.ds(pl.multiple_of(at - PAD, PAD), c + 2 * PAD), :]
    xm = pltpu.roll(ext, 1, 0)[PAD:PAD + c]
    xp = pltpu.roll(ext, c + 2 * PAD - 1, 0)[PAD:PAD + c]
    return s, xm, ext[PAD:PAD + c], xp


def _zero_pads(scr, r, n_lat):
    for at in (0, PAD + n_lat, 2 * PAD + r):
        scr[pl.ds(at, PAD), :] = jnp.zeros((PAD, LANES), F32)


def _col(arr_cols, c0):
    return pl.BlockSpec((arr_cols, LANES), lambda c: (0, c0 + c))


def _ffn_conv_fwd(up, w, n_lat, name):
    r = up.shape[0]
    nct = D_FF // LANES
    nchunk = r // CONV_CHUNK

    def body(a_ref, g_ref, w_ref, f_ref, scr):
        _zero_pads(scr, r, n_lat)

        def fill(j, _):
            s, at = _scr_rows(j, n_lat)
            scr[pl.ds(at, CONV_CHUNK), :] = a_ref[pl.ds(s, CONV_CHUNK), :].astype(F32)
            return 0
        lax.fori_loop(0, nchunk, fill, 0)
        wv = w_ref[...]

        def step(j, _):
            s, xm, x0, xp = _taps(scr, j, n_lat)
            ca = wv[0:1] * xm + wv[1:2] * x0 + wv[2:3] * xp
            gv = g_ref[pl.ds(s, CONV_CHUNK), :].astype(F32)
            f_ref[pl.ds(s, CONV_CHUNK), :] = (ca * _sigmoid(ca) * gv).astype(BF16)
            return 0
        lax.fori_loop(0, nchunk, step, 0)

    return pl.pallas_call(
        body, name=name, grid=(nct,),
        in_specs=[_col(r, 0), _col(r, nct), _col(3, 0)],
        out_specs=_col(r, 0), out_shape=jax.ShapeDtypeStruct((r, D_FF), BF16),
        scratch_shapes=[pltpu.VMEM((r + 3 * PAD, LANES), F32)], compiler_params=_params())(up, up, w)


def _ffn_conv_bwd(up, w, df, n_lat, name, carry=None):
    r = up.shape[0]
    nct = D_FF // LANES
    nchunk = r // CONV_CHUNK

    def body(a_ref, g_ref, w_ref, df_ref, dup_ref, dw_ref, scr, scr2):
        _zero_pads(scr, r, n_lat)
        _zero_pads(scr2, r, n_lat)

        def fill(j, _):
            s, at = _scr_rows(j, n_lat)
            scr[pl.ds(at, CONV_CHUNK), :] = a_ref[pl.ds(s, CONV_CHUNK), :].astype(F32)
            return 0
        lax.fori_loop(0, nchunk, fill, 0)
        wv = w_ref[...]

        def first(j, carry):
            s, xm, x0, xp = _taps(scr, j, n_lat)
            ca = wv[0:1] * xm + wv[1:2] * x0 + wv[2:3] * xp
            sg = _sigmoid(ca)
            gv = g_ref[pl.ds(s, CONV_CHUNK), :].astype(F32)
            dfv = df_ref[pl.ds(s, CONV_CHUNK), :].astype(F32)
            dup_ref[1, pl.ds(s, CONV_CHUNK), :] = (dfv * ca * sg).astype(BF16)
            dca = dfv * gv * (sg * (1.0 + ca * (1.0 - sg)))
            scr2[pl.ds(_scr_rows(j, n_lat)[1], CONV_CHUNK), :] = dca
            return tuple(cw + jnp.sum(dca * xv, axis=0, keepdims=True) for cw, xv in zip(carry, (xm, x0, xp)))
        zero = jnp.zeros((1, LANES), F32)
        dws = lax.fori_loop(0, nchunk, first, (zero, zero, zero))
        dw_ref[...] = _rows(list(dws) + [None] * 5)

        def second(j, _):
            s, ym, y0, yp = _taps(scr2, j, n_lat)
            dup_ref[0, pl.ds(s, CONV_CHUNK), :] = (wv[0:1] * yp + wv[1:2] * y0 + wv[2:3] * ym).astype(BF16)
            return 0
        lax.fori_loop(0, nchunk, second, 0)

    outs, carried = _pcall(
        body, (up, up, w, df), name=name, grid=(nct,),
        in_specs=[_col(r, 0), _col(r, nct), _col(3, 0), _col(r, 0)],
        out_specs=[pl.BlockSpec((2, r, LANES), lambda c: (0, 0, c)), _col(8, 0)],
        out_shape=[jax.ShapeDtypeStruct((2, r, D_FF), BF16), jax.ShapeDtypeStruct((8, D_FF), F32)],
        scratch_shapes=[pltpu.VMEM((r + 3 * PAD, LANES), F32), pltpu.VMEM((r + 3 * PAD, LANES), F32)], carry=carry)
    return outs if carry is None else (outs, carried)


def _sconv_fwd(z, w, n_lat, name):
    r = z.shape[0]
    nct = D_MODEL // LANES
    nchunk = r // CONV_CHUNK
    c0 = P_B // LANES

    def body(bg_ref, cg_ref, hb_ref, w_ref, y_ref, scr):
        _zero_pads(scr, r, n_lat)

        def fill(j, _):
            s, at = _scr_rows(j, n_lat)
            rows = pl.ds(s, CONV_CHUNK)
            scr[pl.ds(at, CONV_CHUNK), :] = cg_ref[rows, :].astype(F32) * hb_ref[rows, :].astype(F32)
            return 0
        lax.fori_loop(0, nchunk, fill, 0)
        wv = w_ref[...]

        def step(j, _):
            s, xm, x0, xp = _taps(scr, j, n_lat)
            conv = wv[0:1] * xm + wv[1:2] * x0 + wv[2:3] * xp
            y_ref[pl.ds(s, CONV_CHUNK), :] = (bg_ref[pl.ds(s, CONV_CHUNK), :].astype(F32) * conv).astype(BF16)
            return 0
        lax.fori_loop(0, nchunk, step, 0)

    return pl.pallas_call(
        body, name=name, grid=(nct,),
        in_specs=[_col(r, c0), _col(r, c0 + nct), _col(r, c0 + 2 * nct), _col(3, 0)],
        out_specs=_col(r, 0), out_shape=jax.ShapeDtypeStruct((r, D_MODEL), BF16),
        scratch_shapes=[pltpu.VMEM((r + 3 * PAD, LANES), F32)], compiler_params=_params())(z, z, z, w)


def _sconv_bwd(z, w, dy, n_lat, name):
    r = z.shape[0]
    nct = D_MODEL // LANES
    nchunk = r // CONV_CHUNK
    c0 = P_B // LANES

    def body(bg_ref, cg_ref, hb_ref, w_ref, dy_ref, dz_ref, dw_ref, scr, scr2):
        _zero_pads(scr, r, n_lat)
        _zero_pads(scr2, r, n_lat)

        def fill(j, _):
            s, at = _scr_rows(j, n_lat)
            rows = pl.ds(s, CONV_CHUNK)
            scr[pl.ds(at, CONV_CHUNK), :] = cg_ref[rows, :].astype(F32) * hb_ref[rows, :].astype(F32)
            return 0
        lax.fori_loop(0, nchunk, fill, 0)
        wv = w_ref[...]

        def first(j, carry):
            s, xm, x0, xp = _taps(scr, j, n_lat)
            rows = pl.ds(s, CONV_CHUNK)
            conv = wv[0:1] * xm + wv[1:2] * x0 + wv[2:3] * xp
            dyv = dy_ref[rows, :].astype(F32)
            dz_ref[0, rows, :] = (dyv * conv).astype(BF16)
            dconv = dyv * bg_ref[rows, :].astype(F32)
            scr2[pl.ds(_scr_rows(j, n_lat)[1], CONV_CHUNK), :] = dconv
            return tuple(cw + jnp.sum(dconv * xv, axis=0, keepdims=True) for cw, xv in zip(carry, (xm, x0, xp)))
        zero = jnp.zeros((1, LANES), F32)
        dws = lax.fori_loop(0, nchunk, first, (zero, zero, zero))
        dw_ref[...] = _rows(list(dws) + [None] * 5)

        def second(j, _):
            s, ym, y0, yp = _taps(scr2, j, n_lat)
            rows = pl.ds(s, CONV_CHUNK)
            dx = wv[0:1] * yp + wv[1:2] * y0 + wv[2:3] * ym
            dz_ref[1, rows, :] = (dx * hb_ref[rows, :].astype(F32)).astype(BF16)
            dz_ref[2, rows, :] = (dx * cg_ref[rows, :].astype(F32)).astype(BF16)
            return 0
        lax.fori_loop(0, nchunk, second, 0)

    return pl.pallas_call(
        body, name=name, grid=(nct,),
        in_specs=[_col(r, c0), _col(r, c0 + nct), _col(r, c0 + 2 * nct), _col(3, 0), _col(r, 0)],
        out_specs=[pl.BlockSpec((3, r, LANES), lambda c: (0, 0, c)), _col(8, 0)],
        out_shape=[jax.ShapeDtypeStruct((3, r, D_MODEL), BF16), jax.ShapeDtypeStruct((8, D_MODEL), F32)],
        scratch_shapes=[pltpu.VMEM((r + 3 * PAD, LANES), F32), pltpu.VMEM((r + 3 * PAD, LANES), F32)],
        compiler_params=_params())(z, z, z, w, dy)


def _merge_fwd(z, b_gate, ps, name):
    r = z.shape[0]
    tm = ROW_TILE
    row = pl.BlockSpec((tm, D_MODEL), lambda i: (i, 0))

    def body(zg_ref, b_ref, p0_ref, p1_ref, p2_ref, o_ref):
        gates = _sigmoid(zg_ref[...].astype(F32) + b_ref[...])
        acc = None
        for t, p_ref in enumerate((p0_ref, p1_ref, p2_ref)):
            term = gates[:, t * D_MODEL:(t + 1) * D_MODEL] * p_ref[...].astype(F32)
            acc = term if acc is None else acc + term
        o_ref[...] = acc.astype(BF16)

    return pl.pallas_call(
        body, name=name, grid=(r // tm,),
        in_specs=[pl.BlockSpec((tm, 3 * D_MODEL), lambda i: (i, 0)), pl.BlockSpec((1, 3 * D_MODEL), lambda i: (0, 0)),
                  row, row, row],
        out_specs=row, out_shape=jax.ShapeDtypeStruct((r, D_MODEL), BF16), compiler_params=_params())(z, b_gate, *ps)


def _merge_bwd(z, b_gate, ps, dmerged, name):
    r = z.shape[0]
    tm = ROW_TILE
    row = pl.BlockSpec((tm, D_MODEL), lambda i: (i, 0))
    wide = pl.BlockSpec((tm, 3 * D_MODEL), lambda i: (i, 0))

    def body(zg_ref, b_ref, p0_ref, p1_ref, p2_ref, dm_ref, d0_ref, d1_ref, d2_ref, dz_ref, db_ref):
        @pl.when(pl.program_id(0) == 0)
        def _():
            db_ref[...] = jnp.zeros_like(db_ref)

        gates = _sigmoid(zg_ref[...].astype(F32) + b_ref[...])
        dm = dm_ref[...].astype(F32)
        for t, (p_ref, d_ref) in enumerate(((p0_ref, d0_ref), (p1_ref, d1_ref), (p2_ref, d2_ref))):
            cols = slice(t * D_MODEL, (t + 1) * D_MODEL)
            gt = gates[:, cols]
            d_ref[...] = (dm * gt).astype(BF16)
            dlogit = dm * p_ref[...].astype(F32) * gt * (1.0 - gt)
            dz_ref[:, cols] = dlogit.astype(BF16)
            db_ref[0:1, cols] += jnp.sum(dlogit, axis=0, keepdims=True)

    shp = jax.ShapeDtypeStruct((r, D_MODEL), BF16)
    return pl.pallas_call(
        body, name=name, grid=(r // tm,),
        in_specs=[wide, pl.BlockSpec((1, 3 * D_MODEL), lambda i: (0, 0)), row, row, row, row],
        out_specs=[row, row, row, wide, pl.BlockSpec((8, 3 * D_MODEL), lambda i: (0, 0))],
        out_shape=[shp, shp, shp, jax.ShapeDtypeStruct((r, 3 * D_MODEL), BF16),
                   jax.ShapeDtypeStruct((8, 3 * D_MODEL), F32)],
        compiler_params=_params())(z, b_gate, *ps, dmerged)


def _sum_slots(buf, name):
    s, rows, _ = buf.shape
    whole_bytes = s * rows * LANES * 4
    tr = rows if whole_bytes <= VMEM_LIMIT // 8 else _pick(rows, [512, 256, 128, 64, 32, 16, 8])

    def body(b_ref, o_ref):
        acc = b_ref[0]
        for t in range(1, s):
            acc = acc + b_ref[t]
        o_ref[...] = acc

    return pl.pallas_call(
        body, name=name, grid=(rows // tr,),
        in_specs=[pl.BlockSpec((s, tr, LANES), lambda i: (0, i, 0))],
        out_specs=pl.BlockSpec((tr, LANES), lambda i: (i, 0)),
        out_shape=jax.ShapeDtypeStruct((rows, LANES), F32), compiler_params=_params())(buf)


def _adamw(w, gsrcs, m, v, name, scale=None):
    nl, rows, cols = w.shape
    assert len(gsrcs) == nl
    s = gsrcs[0].shape[0]
    tr = _pick(rows, [128, 64, 32, 16, 8])
    blk = pl.BlockSpec((None, tr, cols), lambda l, i: (l, i, 0))
    c1 = 1.0 / (1.0 - ADAM_B1 ** ADAM_STEP)
    c2 = 1.0 / (1.0 - ADAM_B2 ** ADAM_STEP)

    def gspec(t):
        return pl.BlockSpec((s, tr, cols), lambda l, i: (0, jnp.where(l == t, i, 0), 0))

    def body(*refs):
        w_ref, g_refs, (m_ref, v_ref) = refs[0], refs[1:1 + nl], refs[1 + nl:3 + nl]
        rest = refs[3 + nl:]
        if scale is not None:
            sc_ref, rest = rest[0], rest[1:]
        go_ref, d_ref, mo_ref, vo_ref = rest
        layer = pl.program_id(0)
        g = None
        for t in range(nl):
            gt = g_refs[t][0].astype(F32)
            for q in range(1, s):
                gt = gt + g_refs[t][q].astype(F32)
            g = gt if g is None else jnp.where(layer == t, gt, g)
        if scale is not None:
            g = g * sc_ref[...]
        mn = ADAM_B1 * m_ref[...] + (1.0 - ADAM_B1) * g
        vn = ADAM_B2 * v_ref[...] + (1.0 - ADAM_B2) * (g * g)
        go_ref[...] = g
        mo_ref[...] = mn
        vo_ref[...] = vn
        d_ref[...] = -ADAM_LR * ((mn * c1) / (jnp.sqrt(vn * c2) + ADAM_EPS) + ADAM_WD * w_ref[...])

    shp = jax.ShapeDtypeStruct((nl, rows, cols), F32)
    ins = [w] + list(gsrcs) + [m, v] + ([] if scale is None else [scale])
    return pl.pallas_call(
        body, name=name, grid=(nl, rows // tr),
        in_specs=[blk] + [gspec(t) for t in range(nl)] + [blk, blk] + ([] if scale is None else [blk]),
        out_specs=[blk] * 4, out_shape=[shp] * 4, compiler_params=_params())(*ins)


def _pack(arrs):
    flat = []
    for a in arrs:
        a = a.reshape(-1).astype(F32)
        pad = (-a.shape[0]) % (8 * LANES)
        flat.append(jnp.pad(a, (0, pad)) if pad else a)
    return jnp.concatenate(flat).reshape(-1, LANES)


def _unpack(buf, shapes, lead=()):
    out, row = [], 0
    for shp in shapes:
        n = 1
        for d in shp:
            n *= d
        nrows = -(-n // (8 * LANES)) * 8
        piece = buf[..., row:row + nrows, :].reshape(lead + (nrows * LANES,))[..., :n]
        out.append(piece.reshape(lead + tuple(shp)))
        row += nrows
    return out


def _silu(x):
    return x * jax.nn.sigmoid(x)


def _rope_tables(n_lat, m_ctx):
    pos = jnp.arange(n_lat)
    row = (pos // GRID_W).astype(F32)
    col = (pos % GRID_W).astype(F32)
    half = HEAD_DIM // 2
    inv = ROPE_THETA ** (-jnp.arange(0, half, 2, dtype=F32) / half)
    ang = jnp.concatenate([row[:, None] * inv, col[:, None] * inv], axis=-1)
    cos, sin = jnp.cos(ang), jnp.sin(ang)
    cos2 = jnp.tile(jnp.concatenate([cos, cos], axis=-1), (1, LANES // HEAD_DIM))
    sin2 = jnp.tile(jnp.concatenate([-sin, sin], axis=-1), (1, LANES // HEAD_DIM))
    return (jnp.concatenate([cos2, jnp.ones((m_ctx, LANES), F32)], axis=0),
            jnp.concatenate([sin2, jnp.zeros((m_ctx, LANES), F32)], axis=0))


def _to_slots(full, axis):
    shp = full.shape
    new = shp[:axis] + (N_DEV, shp[axis] // N_DEV) + shp[axis + 1:]
    return jnp.moveaxis(full.reshape(new), axis, 0)


def _from_slots(slots, axis):
    moved = jnp.moveaxis(slots, 0, axis)
    shp = moved.shape
    return moved.reshape(shp[:axis] + (shp[axis] * shp[axis + 1],) + shp[axis + 2:])


def _permute_in(wt, name):
    blk = 512
    segs = [(OFF_G, IN_W), (0, OFF_K), (OFF_A, OFF_B), (OFF_B, OFF_G), (OFF_K, OFF_A)]

    def source(j):
        src, at = 0, 0
        for lo, hi in segs:
            n = (hi - lo) // blk
            src = src + jnp.where((j >= at) & (j < at + n), j - at + lo // blk, 0)
            at += n
        return (src, 0)

    def body(i_ref, o_ref):
        o_ref[...] = i_ref[...]

    return pl.pallas_call(
        body, name=name, grid=(IN_W // blk,), in_specs=[pl.BlockSpec((blk, D_MODEL), source)],
        out_specs=pl.BlockSpec((blk, D_MODEL), lambda j: (j, 0)), out_shape=jax.ShapeDtypeStruct(wt.shape, wt.dtype),
        compiler_params=_params())(wt)


def kernel(x, c, ctx, c_ctx, w_mod, b_mod, g_mix, w_in, b_gate, sink, w_spatial, b_spatial, g_v, w_sconv, w_branch, w_out, g_ffn, w_up, w_fconv, w_down, g_final, loss_target, m_c_ctx, m_w_mod, m_b_mod, m_g_mix, m_w_in, m_b_gate, m_sink, m_w_spatial, m_b_spatial, m_g_v, m_w_sconv, m_w_branch, m_w_out, m_g_ffn, m_w_up, m_w_fconv, m_w_down, m_g_final, v_c_ctx, v_w_mod, v_b_mod, v_g_mix, v_w_in, v_b_gate, v_sink, v_w_spatial, v_b_spatial, v_g_v, v_w_sconv, v_w_branch, v_w_out, v_g_ffn, v_w_up, v_w_fconv, v_w_down, v_g_final):
    n_lat, m_ctx = x.shape[1], ctx.shape[1]
    r = n_lat + m_ctx
    me = 4 * lax.axis_index("x") + 2 * lax.axis_index("y") + lax.axis_index("c")
    mod_w = w_mod.shape[2]
    bm = _pick(r, [768, 256])

    tr = lambda a: jnp.swapaxes(a, -1, -2)
    shard_axis = {"in": 0, "br": 1, "out": 0, "up": 0, "dn": 0}
    shards = {}
    for kind, wt in (("in", tr(w_in)), ("br", w_branch), ("out", w_out), ("up", tr(w_up)), ("dn", w_down)):
        wb = wt.astype(BF16)
        for l in range(DEPTH):
            shards[kind, l] = wb[l]
    full = {}

    def arrive(items, got):
        for key, slots in zip(items, got):
            wfull = _from_slots(slots, shard_axis[key[0]])
            full[key] = _permute_in(wfull, f"permute_in{key[1]}") if key[0] == "in" else wfull

    def gather_of(items):
        return [shards[key] for key in items], ["gather"] * len(items)

    small_shapes = [c.shape, w_sconv.shape, w_fconv.shape]
    g_in0, g_small = _exchange([shards["in", 0], _pack([c, w_sconv, w_fconv])], ["gather"] * 2, "gather_first")
    arrive([("in", 0)], [g_in0])
    c_all, sconv_all, fconv_all = _unpack(g_small, small_shapes, lead=(N_DEV,))
    c_all = c_all.reshape(N_DEV, D_MODEL)
    w_sconv_full = _from_slots(sconv_all, 2)
    w_fconv_full = _from_slots(fconv_all, 2)

    act = jnp.concatenate([_silu(c_all), _silu(c_ctx)[None], jnp.zeros((7, D_MODEL), F32)], axis=0)
    mod_part = jnp.stack([_mm(act, w_mod[l], name=f"mod_fwd{l}", bm=16, bn=mod_w, bk=D_MODEL, out_dtype=F32)
                          for l in range(DEPTH)])
    (mod_all,) = _exchange([mod_part], ["gather"], "gather_mod")
    mod_full = _from_slots(mod_all, 2) + b_mod[:, None, :]
    mods = []
    for l in range(DEPTH):
        mine = lax.dynamic_index_in_dim(mod_full[l], me, axis=0, keepdims=False).reshape(6, D_MODEL)
        theirs = mod_full[l, N_DEV].reshape(6, D_MODEL)
        mods.append(jnp.pad(jnp.stack([mine, theirs]), ((0, 0), (0, 2), (0, 0))))

    cos, sin = _rope_tables(n_lat, m_ctx)
    xs = jnp.concatenate([x[0], ctx[0]], axis=0)
    ws_b = w_spatial.astype(BF16)
    ws_t = jnp.swapaxes(w_spatial, 2, 3).astype(BF16)
    vec = lambda a: a.reshape(1, -1)

    saved = []
    res = None
    for l in range(DEPTH):
        s = {}
        if res is None:
            s["x0"] = xs
            s["h"] = _norm_fwd(xs, vec(g_mix[l]), mods[l], n_lat, 0, 1, f"norm_mix{l}")
        else:
            s["x0"], s["h"] = _norm_fwd(xs, vec(g_mix[l]), mods[l], n_lat, 0, 1, f"norm_mix{l}", res=res)
        items = [("br", l), ("out", l), ("up", l)]
        s["z"], got = _mm(s["h"], full["in", l], tb=True, name=f"proj_in{l}", bm=bm, bn=_pick(IN_W, [2432, 512]),
                          bk=D_MODEL, out_dtype=BF16, cols_outer=True, carry=gather_of(items))
        arrive(items, got)
        s["qkv"] = _rope_fwd(s["z"], cos, sin, f"rope{l}")
        items = [("dn", l)] + ([("in", l + 1)] if l + 1 < DEPTH else [])
        (s["y0"], s["lse"]), got = _attn_fwd(s["qkv"], sink[l], n_lat, m_ctx, f"attn{l}", carry=gather_of(items))
        arrive(items, got)
        s["bias"] = jnp.repeat(b_spatial[l].T, LANES, axis=1)
        s["y1"] = _gating_fwd(s["z"], ws_b[l], s["bias"], vec(g_v[l]), f"gating{l}")
        s["y2"] = _sconv_fwd(s["z"], w_sconv_full[l], n_lat, f"sconv{l}")
        s["p"] = [_mm(s[f"y{t}"], full["br", l], b_lead=t, name=f"branch{l}_{t}", bm=bm, bn=D_MODEL, bk=D_MODEL,
                      out_dtype=BF16) for t in range(3)]
        s["merged"] = _merge_fwd(s["z"], vec(b_gate[l]), s["p"], f"merge{l}")
        s["o"] = _mm(s["merged"], full["out", l], name=f"proj_out{l}", bm=bm, bn=D_MODEL, bk=D_MODEL, out_dtype=F32)
        s["x1"], s["h2"] = _norm_fwd(s["x0"], vec(g_ffn[l]), mods[l], n_lat, 3, 4, f"norm_ffn{l}",
                                     res=(s["o"], mods[l], 2))
        s["up"] = _mm(s["h2"], full["up", l], tb=True, name=f"ffn_up{l}", bm=bm, bn=_pick(2 * D_FF, [1408]),
                      bk=D_MODEL, out_dtype=BF16, cols_outer=True)
        s["f"] = _ffn_conv_fwd(s["up"], w_fconv_full[l], n_lat, f"ffn_conv{l}")
        s["dd"] = _mm(s["f"], full["dn", l], name=f"ffn_down{l}", bm=bm, bn=D_MODEL, bk=D_FF, out_dtype=F32)
        saved.append(s)
        xs, res = s["x1"], (s["dd"], mods[l], 5)

    top = saved[DEPTH - 1]
    dxs, d_dd, acc_final, accs_top = _loss_bwd(top["x1"], top["dd"], mods[DEPTH - 1], vec(g_final), loss_target[0],
                                               n_lat, "loss")
    loss_part = acc_final[1, 0]
    dg_final = acc_final[0]
    dmods = [None] * DEPTH
    gate2 = accs_top[:, 2]
    grads = {k: [None] * DEPTH for k in ("g_mix", "g_ffn", "g_v", "b_gate", "sink", "w_spatial", "b_spatial",
                                         "w_sconv", "w_fconv", "w_in", "w_branch", "w_out", "w_up", "w_down")}
    bk_r = _pick(r, [1408, 768, 256])
    small_names = ["g_ffn", "g_v", "b_gate", "sink", "w_spatial", "b_spatial", "w_sconv", "w_fconv"]
    recv = {}
    small_recv, small_shapes_of = {}, {}

    def small_pack(l):
        arrs = [grads[k][l] for k in small_names]
        if l > 0:
            arrs.append(grads["g_mix"][l])
        if l == DEPTH - 1:
            arrs += [loss_part.reshape(1), dg_final]
        small_shapes_of[l] = [a.shape for a in arrs]
        return _pack(arrs)

    for l in reversed(range(DEPTH)):
        s = saved[l]
        df = _mm(d_dd, full["dn", l], tb=True, name=f"d_ffn_down{l}", bm=bm, bn=_pick(D_FF, [1408]), bk=D_MODEL,
                 out_dtype=BF16)
        grads["w_down"][l] = _mm(s["f"], d_dd, ta=True, name=f"g_ffn_down{l}", bm=_pick(D_FF, [1408]), bn=D_MODEL,
                                 bk=bk_r, out_dtype=BF16)
        carry = None if l + 1 == DEPTH else ([_to_slots(grads["w_in"][l + 1], 0)], ["a2a"])
        if carry is None:
            dup, dwf = _ffn_conv_bwd(s["up"], w_fconv_full[l], df, n_lat, f"d_ffn_conv{l}")
        else:
            (dup, dwf), (recv["in", l + 1],) = _ffn_conv_bwd(s["up"], w_fconv_full[l], df, n_lat, f"d_ffn_conv{l}",
                                                            carry=carry)
        grads["w_fconv"][l] = dwf[:3]
        dh2 = _mm_pieces([(dup, "step", 0, 2, 0, D_FF)], full["up", l], w_t=True, name=f"d_ffn_up{l}", bm=bm)
        kb = 1408
        nbh = D_FF // kb
        grads["w_up"][l] = _mm(
            dup, s["h2"], ta=True, name=f"g_ffn_up{l}", bm=kb, bn=D_MODEL, bk=bk_r, out_dtype=BF16,
            a_spec=((None, bk_r, kb), lambda i, j, k: (i // nbh, k, i % nbh), 2 * D_FF))
        dx1, d_o, acc, accs = _norm_bwd(s["x1"], vec(g_ffn[l]), mods[l], dh2, dxs, n_lat, 3, 4, f"d_norm_ffn{l}",
                                        res=(s["o"], mods[l], 2))
        grads["g_ffn"][l] = acc[0]
        shift2, scale2, gate1 = accs[:, 0], accs[:, 1], accs[:, 2]
        dmerged = _mm(d_o, full["out", l], tb=True, name=f"d_proj_out{l}", bm=bm, bn=D_MODEL, bk=D_MODEL,
                      out_dtype=BF16)
        grads["w_out"][l] = _mm(s["merged"], d_o, ta=True, name=f"g_proj_out{l}", bm=D_MODEL, bn=D_MODEL, bk=bk_r,
                                out_dtype=BF16)
        dp0, dp1, dp2, dz_g, dbg = _merge_bwd(s["z"], vec(b_gate[l]), s["p"], dmerged, f"d_merge{l}")
        grads["b_gate"][l] = dbg[0]
        dps = (dp0, dp1, dp2)
        dys = [_mm(dps[t], full["br", l], tb=True, b_lead=t, name=f"d_branch{l}_{t}", bm=bm, bn=D_MODEL, bk=D_MODEL,
                   out_dtype=BF16) for t in range(3)]
        grads["w_branch"][l] = jnp.stack(
            [_mm(s[f"y{t}"], dps[t], ta=True, name=f"g_branch{l}_{t}", bm=D_MODEL, bn=D_MODEL, bk=bk_r,
                 out_dtype=BF16) for t in range(3)])
        arrs = [_to_slots(grads["w_down"][l], 0), _to_slots(grads["w_up"][l], 0), _to_slots(grads["w_out"][l], 0),
                _to_slots(grads["w_branch"][l], 1)]
        modes = ["a2a"] * 4
        if l + 1 < DEPTH:
            arrs.append(small_pack(l + 1))
            modes.append("gather")
        (dq, dk, dv, dsk), got = _attn_bwd(s["qkv"], sink[l], s["y0"], dys[0], s["lse"], n_lat, m_ctx, f"d_attn{l}",
                                           carry=(arrs, modes))
        recv["dn", l], recv["up", l], recv["out", l], recv["br", l] = got[:4]
        if l + 1 < DEPTH:
            small_recv[l + 1] = got[4]
        grads["sink"][l] = dsk[:, 0, :8].reshape(N_HEADS)
        dz_qkv = _rope_bwd(dq, dk, dv, cos, sin, f"d_rope{l}")
        dz_a, dws, dbs, dgv = _gating_bwd(s["z"], ws_b[l], ws_t[l], s["bias"], vec(g_v[l]), dys[1], f"d_gating{l}")
        grads["w_spatial"][l], grads["b_spatial"][l], grads["g_v"][l] = dws, dbs.T, dgv[0]
        dz_b, dwsc = _sconv_bwd(s["z"], w_sconv_full[l], dys[2], n_lat, f"d_sconv{l}")
        grads["w_sconv"][l] = dwsc[:3]
        kvw = 2 * N_KV_HEADS * HEAD_DIM
        pieces = [(dz_g, None, 0, 3, 0, D_MODEL), (dz_qkv, None, 0, 1, P_Q // D_MODEL, D_MODEL),
                  (dz_a, None, 0, 2, P_A // D_MODEL, D_MODEL), (dz_b, "step", 0, 3, P_B // D_MODEL, D_MODEL),
                  (dz_qkv, None, D_MODEL // kvw, 1, P_KV // kvw, kvw)]
        gw = lambda a, nm, rows, **kw: _mm(a, s["h"], ta=True, name=f"g_proj_in{l}_{nm}", bm=rows, bn=D_MODEL,
                                           bk=bk_r, out_dtype=BF16, **kw)
        gw_g = gw(dz_g, "gate", 1536)
        gw_qkv = gw(dz_qkv, "qkv", 1536)
        gw_a = gw(dz_a, "gating", 1024)
        gw_b = [gw(dz_b, f"sconv{t}", 1024, a_lead=t) for t in range(3)]
        grads["w_in"][l] = jnp.concatenate([gw_qkv, gw_a] + gw_b + [gw_g], axis=0)
        bm_in = _pick(r, [1056, 768, 256])
        if l > 0:
            dh = _mm_pieces(pieces, full["in", l], w_t=True, name=f"d_proj_in{l}", bm=bm_in)
        else:
            dh, (recv["in", 0], small_recv[0]) = _mm_pieces(
                pieces, full["in", l], w_t=True, name=f"d_proj_in{l}", bm=bm_in,
                carry=([_to_slots(grads["w_in"][0], 0), small_pack(0)], ["a2a", "gather"]))
        below = None if l == 0 else (saved[l - 1]["dd"], mods[l - 1], 5)
        outs = _norm_bwd(s["x0"], vec(g_mix[l]), mods[l], dh, dx1, n_lat, 0, 1, f"d_norm_mix{l}", res=below)
        if below is None:
            dxs, acc, accs = outs
        else:
            dxs, d_dd, acc, accs = outs
        grads["g_mix"][l] = acc[0]
        dmods[l] = jnp.stack([accs[:, 0], accs[:, 1], gate1, shift2, scale2, gate2], axis=1)
        gate2 = accs[:, 2]

    dmod_own = jnp.stack([dmods[l][0].reshape(-1) for l in range(DEPTH)])
    dmod_ctx = jnp.stack([dmods[l][1].reshape(-1) for l in range(DEPTH)])
    late = [grads["g_mix"][0], dmod_own + dmod_ctx, dmod_ctx, dmod_own]
    late_shapes = [a.shape for a in late]
    (late_all,) = _exchange([_pack(late)], ["gather"], "gather_late_grads")
    g_mix0_g, b_mod_g, dmodc_tot, _ = _unpack(_sum_slots(late_all, "sum_late_grads"), late_shapes)
    dmod_all = _unpack(late_all, late_shapes, lead=(N_DEV,))[-1]
    layer_sums = [_unpack(_sum_slots(small_recv[l], f"sum_small_grads{l}"), small_shapes_of[l]) for l in range(DEPTH)]
    by_name = {k: jnp.stack([layer_sums[l][t] for l in range(DEPTH)]) for t, k in enumerate(small_names)}
    g_mix_g = jnp.stack([g_mix0_g] + [layer_sums[l][len(small_names)] for l in range(1, DEPTH)])
    loss_sum, g_final_g = layer_sums[DEPTH - 1][-2], layer_sums[DEPTH - 1][-1]
    g_ffn_g, g_v_g, b_gate_g, sink_g = by_name["g_ffn"], by_name["g_v"], by_name["b_gate"], by_name["sink"]
    w_spatial_g, b_spatial_g = by_name["w_spatial"], by_name["b_spatial"]
    w_sconv_g = lax.dynamic_slice_in_dim(by_name["w_sconv"], me * w_sconv.shape[2], w_sconv.shape[2], axis=2)
    w_fconv_g = lax.dynamic_slice_in_dim(by_name["w_fconv"], me * w_fconv.shape[2], w_fconv.shape[2], axis=2)

    dmod_cols = lax.dynamic_slice_in_dim(dmod_all, me * mod_w, mod_w, axis=2)
    dmodc_cols = lax.dynamic_slice_in_dim(dmodc_tot, me * mod_w, mod_w, axis=1)
    g_w_mod, cctx_part = [], None
    for l in range(DEPTH):
        rhs = jnp.concatenate([dmod_cols[:, l], dmodc_cols[l][None], jnp.zeros((7, mod_w), F32)], axis=0)
        g_w_mod.append(_mm(act, rhs, ta=True, name=f"g_mod{l}", bm=D_MODEL, bn=mod_w, bk=16, out_dtype=F32))
        lhs = jnp.pad(dmodc_cols[l][None], ((0, 7), (0, 0)))
        part = _mm(lhs, w_mod[l], tb=True, name=f"d_cctx{l}", bm=8, bn=D_MODEL, bk=mod_w, out_dtype=F32)
        cctx_part = part if cctx_part is None else cctx_part + part
    sg = jax.nn.sigmoid(c_ctx)
    dsilu = (sg * (1.0 + c_ctx * (1.0 - sg))).reshape(8, LANES)

    (r_cctx,) = _exchange([cctx_part[0].reshape(8, LANES)], ["gather"], "gather_c_ctx_grad")

    def per_layer(a):
        return a.reshape(a.shape[0], -1, a.shape[-1])

    upd = {}
    for nm, kind, wv, mv, vv in (("w_in", "in", w_in, m_w_in, v_w_in), ("w_branch", "br", w_branch, m_w_branch, v_w_branch),
                                 ("w_out", "out", w_out, m_w_out, v_w_out), ("w_up", "up", w_up, m_w_up, v_w_up),
                                 ("w_down", "dn", w_down, m_w_down, v_w_down), ("w_mod", None, w_mod, m_w_mod, v_w_mod)):
        if kind is None:
            gsrcs = [g[None] for g in g_w_mod]
        else:
            gsrcs = [per_layer(recv[kind, l]) for l in range(DEPTH)]
        if kind in ("in", "up"):
            outs = _adamw(tr(wv), gsrcs, tr(mv), tr(vv), f"adamw_{nm}")
            upd[nm] = [tr(o) for o in outs]
        else:
            outs = _adamw(per_layer(wv), gsrcs, per_layer(mv), per_layer(vv), f"adamw_{nm}")
            upd[nm] = [o.reshape(wv.shape) for o in outs]
    as_tile = lambda a: a.reshape(1, 8, LANES)
    upd["c_ctx"] = [o.reshape(D_MODEL) for o in _adamw(
        as_tile(c_ctx), [r_cctx], as_tile(m_c_ctx), as_tile(v_c_ctx), "adamw_c_ctx", scale=as_tile(dsilu))]

    names = ["b_mod", "g_mix", "b_gate", "sink", "w_spatial", "b_spatial", "g_v", "w_sconv", "g_ffn", "w_fconv",
             "g_final"]
    w_s = [b_mod, g_mix, b_gate, sink, w_spatial, b_spatial, g_v, w_sconv, g_ffn, w_fconv, g_final]
    g_s = [b_mod_g, g_mix_g, b_gate_g, sink_g, w_spatial_g, b_spatial_g, g_v_g, w_sconv_g, g_ffn_g, w_fconv_g,
           g_final_g]
    m_s = [m_b_mod, m_g_mix, m_b_gate, m_sink, m_w_spatial, m_b_spatial, m_g_v, m_w_sconv, m_g_ffn, m_w_fconv,
           m_g_final]
    v_s = [v_b_mod, v_g_mix, v_b_gate, v_sink, v_w_spatial, v_b_spatial, v_g_v, v_w_sconv, v_g_ffn, v_w_fconv,
           v_g_final]
    shapes = [a.shape for a in w_s]
    packed = _adamw(_pack(w_s)[None], [_pack(g_s)[None]], _pack(m_s)[None], _pack(v_s)[None], "adamw_small")
    unpacked = [_unpack(o[0], shapes) for o in packed]
    for t, nm in enumerate(names):
        upd[nm] = [unpacked[q][t] for q in range(4)]

    order = ["c_ctx", "w_mod", "b_mod", "g_mix", "w_in", "b_gate", "sink", "w_spatial", "b_spatial", "g_v", "w_sconv",
             "w_branch", "w_out", "g_ffn", "w_up", "w_fconv", "w_down", "g_final"]
    result = [loss_sum.reshape(()), dxs[None]]
    for q in range(4):
        result += [upd[nm][q] for nm in order]
    return tuple(result)
```

```python
import jax
import jax.numpy as jnp
from jax import lax
from jax.experimental import pallas as pl
from jax.experimental.pallas import tpu as pltpu

F32, BF16 = jnp.float32, jnp.bfloat16

D_MODEL = 1024
DEPTH = 2
GRID_W = 64
N_HEADS = 16
N_KV_HEADS = 4
HEAD_DIM = 64
WINDOW = 128
BLK = 128
ROPE_THETA = 10000.0
CHUNK = 128
A_GROUPS = 8
D_FF = 2816
EPS = 1e-6
NEG = -1e30
IN_W = 9728
OFF_K, OFF_A, OFF_B, OFF_G = 1024, 1536, 3584, 6656
P_Q, P_A, P_B, P_KV = 3072, 4096, 6144, 9216

N_DEV = 8
LANES = 128
ROW_TILE = 256
CONV_CHUNK = 256
PAD = 8
VMEM_LIMIT = 52 * 1024 * 1024

ADAM_LR, ADAM_B1, ADAM_B2, ADAM_EPS, ADAM_WD, ADAM_STEP = 0.001, 0.9, 0.999, 1e-08, 0.01, 10

HBM_SPEC = pl.BlockSpec(memory_space=pltpu.HBM)
SMEM_SPEC = pl.BlockSpec(memory_space=pltpu.SMEM)


def _params():
    return pltpu.CompilerParams(vmem_limit_bytes=VMEM_LIMIT)


def _pick(n, prefs):
    for p in prefs:
        if n % p == 0:
            return p
    raise ValueError((n, prefs))


def _sigmoid(x):
    return 0.5 * jnp.tanh(0.5 * x) + 0.5


def _mm(a, b, *, name, ta=False, tb=False, bm, bn, bk, out_dtype, a_lead=None, b_lead=None, a_spec=None,
        b_spec=None, cols_outer=False, carry=None):
    ash = a.shape[1:] if a_lead is not None else a.shape
    bsh = b.shape[1:] if b_lead is not None else b.shape
    kc = (bsh[1] if tb else bsh[0]) if b_spec is None else (ash[0] if ta else ash[1])
    mo =(ash[1] if ta else ash[0]) if a_spec is None else a_spec[2]
    no = (bsh[0] if tb else bsh[1]) if b_spec is None else b_spec[2]
    assert mo % bm == 0 and no % bn == 0 and kc % bk == 0, (name, mo, no, kc, bm, bn, bk)
    nk = kc // bk

    def spec(shape, fn, idx=None):
        if idx is not None:
            shape, inner = (None,) + shape, fn
            fn = lambda i, j, k: (idx,) + inner(i, j, k)
        if cols_outer:
            return pl.BlockSpec(shape, lambda j, i, k: fn(i, j, k))
        return pl.BlockSpec(shape, fn)

    if a_spec is not None:
        a_bs = spec(a_spec[0], a_spec[1])
    elif ta:
        a_bs = spec((bk, bm), lambda i, j, k: (k, i), a_lead)
    else:
        a_bs = spec((bm, bk), lambda i, j, k: (i, k), a_lead)
    if b_spec is not None:
        b_bs = spec(b_spec[0], b_spec[1])
    elif tb:
        b_bs = spec((bn, bk), lambda i, j, k: (j, k), b_lead)
    else:
        b_bs = spec((bk, bn), lambda i, j, k: (k, j), b_lead)
    dims = (((0 if ta else 1,), (1 if tb else 0,)), ((), ()))
    grid = (no // bn, mo // bm, nk) if cols_outer else (mo // bm, no // bn, nk)

    def body(a_ref, b_ref, o_ref, *scratch):
        if nk == 1:
            o_ref[...] = lax.dot_general(a_ref[...], b_ref[...], dims, preferred_element_type=F32).astype(o_ref.dtype)
        else:
            acc = scratch[0]
            k = pl.program_id(2)

            @pl.when(k == 0)
            def _():
                acc[...] = jnp.zeros_like(acc)

            acc[...] += lax.dot_general(a_ref[...], b_ref[...], dims, preferred_element_type=F32)

            @pl.when(k == nk - 1)
            def _():
                o_ref[...] = acc[...].astype(o_ref.dtype)

    outs, carried = _pcall(
        body, (a, b), name=name, grid=grid, in_specs=[a_bs, b_bs],
        out_specs=[spec((bm, bn), lambda i, j, k: (i, j))],
        out_shape=[jax.ShapeDtypeStruct((mo, no), out_dtype)],
        scratch_shapes=[pltpu.VMEM((bm, bn), F32)] if nk > 1 else [], carry=carry)
    return outs[0] if carry is None else (outs[0], carried)


def _mm_pieces(pieces, w, *, name, bm, w_t=False, out_dtype=F32, carry=None):
    kout = w.shape[1] if w_t else w.shape[0]
    starts, total = [], 0
    for piece in pieces:
        starts.append(total)
        total += piece[3]
    mo = pieces[0][0].shape[-2]
    assert mo % bm == 0
    widths = sorted({piece[5] for piece in pieces}, reverse=True)

    def inside(p, k):
        return (k >= starts[p]) & (k < starts[p] + pieces[p][3])

    def a_spec(p):
        _, lead, col0, nblk, _, bk = pieces[p]

        def fn(i, k):
            t = jnp.clip(k - starts[p], 0, nblk - 1)
            if lead is None:
                return (i, col0 + t)
            return (t, i, col0) if lead == "step" else (lead, i, col0 + t)
        return pl.BlockSpec((bm, bk) if lead is None else (None, bm, bk), fn)

    def w_spec(bk):
        def fn(i, k):
            col = 0
            for p, piece in enumerate(pieces):
                if piece[5] == bk:
                    col = col + jnp.where(inside(p, k), piece[4] + k - starts[p], 0)
            return (col, 0) if w_t else (0, col)
        return pl.BlockSpec((bk, kout) if w_t else (kout, bk), fn)

    n_p, n_w = len(pieces), len(widths)

    def body(*refs):
        a_refs, w_refs, o_ref, acc = refs[:n_p], refs[n_p:n_p + n_w], refs[n_p + n_w], refs[n_p + n_w + 1]
        k = pl.program_id(1)

        @pl.when(k == 0)
        def _():
            acc[...] = jnp.zeros_like(acc)

        for p in range(n_p):
            w_ref = w_refs[widths.index(pieces[p][5])]

            @pl.when(inside(p, k))
            def _(p=p, w_ref=w_ref):
                acc[...] += lax.dot_general(a_refs[p][...], w_ref[...], NN if w_t else NT,
                                            preferred_element_type=F32)

        @pl.when(k == total - 1)
        def _():
            o_ref[...] = acc[...].astype(o_ref.dtype)

    outs, carried = _pcall(
        body, [piece[0] for piece in pieces] + [w] * n_w, name=name, grid=(mo // bm, total),
        in_specs=[a_spec(p) for p in range(n_p)] + [w_spec(bk) for bk in widths],
        out_specs=[pl.BlockSpec((bm, kout), lambda i, k: (i, 0))],
        out_shape=[jax.ShapeDtypeStruct((mo, kout), out_dtype)],
        scratch_shapes=[pltpu.VMEM((bm, kout), F32)], carry=carry)
    return outs[0] if carry is None else (outs[0], carried)


def _xchg_out_shapes(arrs, modes):
    return [jax.ShapeDtypeStruct((N_DEV,) + a.shape if m == "gather" else a.shape, a.dtype)
            for a, m in zip(arrs, modes)]


def _xchg_sems(n):
    return [pltpu.SemaphoreType.DMA((n, N_DEV - 1)), pltpu.SemaphoreType.DMA((n, N_DEV - 1)),
            pltpu.SemaphoreType.DMA((n,))]


def _xchg_copies(ins, outs, modes, sems):
    send_sems, recv_sems, local_sems = sems
    x, y, c = lax.axis_index("x"), lax.axis_index("y"), lax.axis_index("c")
    me = 4 * x + 2 * y + c

    def place(q):
        px = 1 - x if (q >> 2) & 1 else x
        py = 1 - y if (q >> 1) & 1 else y
        pc = 1 - c if q & 1 else c
        return (px, py, pc), 4 * px + 2 * py + pc

    sibling, _ = place(1)
    out = dict(local=[], direct=[], landed=[], passed=[], others=[])
    for t, mode in enumerate(modes):
        gather = mode == "gather"
        mine = ins[t] if gather else ins[t].at[me]
        out["local"].append(pltpu.make_async_copy(mine, outs[t].at[me], local_sems.at[t]))
        for q in range(1, N_DEV):
            peer, slot = place(q)
            sem = dict(send_sem=send_sems.at[t, q - 1], recv_sem=recv_sems.at[t, q - 1],
                       device_id_type=pl.DeviceIdType.MESH)
            arrival = pltpu.make_async_remote_copy(src_ref=outs[t].at[slot], dst_ref=outs[t].at[slot],
                                                   device_id=peer, **sem)
            if not gather:
                out["direct"].append(pltpu.make_async_remote_copy(src_ref=ins[t].at[slot], dst_ref=outs[t].at[me],
                                                                  device_id=peer, **sem))
                out["others"].append(arrival)
            elif q == 1 or q % 2 == 0:
                out["direct"].append(pltpu.make_async_remote_copy(src_ref=ins[t], dst_ref=outs[t].at[me],
                                                                  device_id=peer, **sem))
                out["others" if q == 1 else "landed"].append(arrival)
            else:
                _, origin = place(q - 1)
                out["passed"].append(pltpu.make_async_remote_copy(src_ref=outs[t].at[origin], dst_ref=outs[t].at[origin],
                                                                  device_id=sibling, **sem))
                out["others"].append(arrival)
    return out


def _xchg_start(copies):
    for cp in copies["local"] + copies["direct"]:
        cp.start()


def _xchg_wait(copies):
    for cp in copies["landed"]:
        cp.wait_recv()
    for cp in copies["passed"]:
        cp.start()
    for cp in copies["others"]:
        cp.wait_recv()
    for cp in copies["direct"] + copies["passed"]:
        cp.wait_send()
    for cp in copies["local"]:
        cp.wait()


def _exchange(arrs, modes, name):
    n = len(arrs)

    def body(*refs):
        copies = _xchg_copies(refs[:n], refs[n:2 * n], modes, refs[2 * n:])
        _xchg_start(copies)
        _xchg_wait(copies)

    outs = pl.pallas_call(
        body, name=name, in_specs=[HBM_SPEC] * n, out_specs=[HBM_SPEC] * n, out_shape=_xchg_out_shapes(arrs, modes),
        scratch_shapes=_xchg_sems(n), compiler_params=pltpu.CompilerParams(has_side_effects=True),
    )(*arrs)
    return list(outs)


def _pcall(body, operands, *, name, grid, in_specs, out_specs, out_shape, scratch_shapes=(), carry=None):
    out_specs, out_shape, scratch_shapes = list(out_specs), list(out_shape), list(scratch_shapes)
    if carry is None:
        outs = pl.pallas_call(body, name=name, grid=grid, in_specs=in_specs, out_specs=out_specs,
                              out_shape=out_shape, scratch_shapes=scratch_shapes, compiler_params=_params())(*operands)
        return list(outs), []
    arrs, modes = carry
    n, n_in, n_out, n_scr = len(arrs), len(in_specs), len(out_specs), len(scratch_shapes)

    def wrapped(*refs):
        ins, c_in = refs[:n_in], refs[n_in:n_in + n]
        outs, c_out = refs[n_in + n:n_in + n + n_out], refs[n_in + n + n_out:n_in + 2 * n + n_out]
        rest = refs[n_in + 2 * n + n_out:]
        scr, sems = rest[:n_scr], rest[n_scr:]
        first, last = None, None
        for d, size in enumerate(grid):
            f, e = pl.program_id(d) == 0, pl.program_id(d) == size - 1
            first = f if first is None else first & f
            last = e if last is None else last & e

        @pl.when(first)
        def _():
            _xchg_start(_xchg_copies(c_in, c_out, modes, sems))

        body(*ins, *outs, *scr)

        @pl.when(last)
        def _():
            _xchg_wait(_xchg_copies(c_in, c_out, modes, sems))

    outs = pl.pallas_call(
        wrapped, name=name, grid=grid, in_specs=list(in_specs) + [HBM_SPEC] * n,
        out_specs=out_specs + [HBM_SPEC] * n, out_shape=out_shape + _xchg_out_shapes(arrs, modes),
        scratch_shapes=scratch_shapes + _xchg_sems(n), compiler_params=_params())(*operands, *arrs)
    return list(outs[:n_out]), list(outs[n_out:])


def _row_specs(r, n_lat, tm):
    nbl = n_lat // tm
    row = pl.BlockSpec((tm, D_MODEL), lambda i: (i, 0))
    mod = pl.BlockSpec((None, 8, D_MODEL), lambda i: (i // nbl, 0, 0))
    vec = pl.BlockSpec((1, D_MODEL), lambda i: (0, 0))
    return nbl, row, mod, vec


def _rows(vals):
    width = [v for v in vals if v is not None][0].shape[1]
    return jnp.concatenate([jnp.zeros((1, width), F32) if v is None else v for v in vals], axis=0)


def _norm_fwd(xs, g, mods, n_lat, sh, sc, name, res=None, carry=None):
    r = xs.shape[0]
    tm = ROW_TILE
    nbl, row, mod, vec = _row_specs(r, n_lat, tm)

    def norm(x, g_ref, m_ref, h_ref):
        rs = lax.rsqrt(jnp.mean(x * x, axis=-1, keepdims=True) + EPS)
        m = m_ref[...]
        h_ref[...] = ((x * rs * g_ref[...]) * (1.0 + m[sc:sc + 1]) + m[sh:sh + 1]).astype(BF16)

    if res is None:
        def body(x_ref, g_ref, m_ref, h_ref):
            norm(x_ref[...], g_ref, m_ref, h_ref)
        outs, carried = _pcall(body, (xs, g, mods), name=name, grid=(r // tm,), in_specs=[row, vec, mod],
                               out_specs=[row], out_shape=[jax.ShapeDtypeStruct((r, D_MODEL), BF16)], carry=carry)
        return outs[0] if carry is None else (outs[0], carried)

    o, mods_res, gt = res

    def body(x_ref, o_ref, mr_ref, g_ref, m_ref, x1_ref, h_ref):
        x = x_ref[...] + mr_ref[...][gt:gt + 1] * o_ref[...]
        x1_ref[...] = x
        norm(x, g_ref, m_ref, h_ref)

    return pl.pallas_call(
        body, name=name, grid=(r // tm,), in_specs=[row, row, mod, vec, mod], out_specs=[row, row],
        out_shape=[jax.ShapeDtypeStruct((r, D_MODEL), F32), jax.ShapeDtypeStruct((r, D_MODEL), BF16)],
        compiler_params=_params())(xs, o, mods_res, g, mods)


def _rms_bwd(x, g, dy):
    rs = lax.rsqrt(jnp.mean(x * x, axis=-1, keepdims=True) + EPS)
    xh = x * rs
    dxh = dy * g
    dx = rs * (dxh - xh * jnp.mean(dxh * xh, axis=-1, keepdims=True))
    return dx, dy * xh, xh


def _acc_specs(nbl):
    acc_all = pl.BlockSpec((8, D_MODEL), lambda i: (0, 0))
    acc_stream = pl.BlockSpec((None, 8, D_MODEL), lambda i: (i // nbl, 0, 0))
    return acc_all, acc_stream


def _loss_bwd(xs, o, mods, g_final, target, n_lat, name):
    r = xs.shape[0]
    tm = ROW_TILE
    nbl, row, mod, vec = _row_specs(r, n_lat, tm)
    acc_all, acc_stream = _acc_specs(nbl)
    tgt = pl.BlockSpec((tm, D_MODEL), lambda i: (jnp.minimum(i, nbl - 1), 0))

    def body(x_ref, o_ref, m_ref, g_ref, t_ref, dx_ref, do_ref, acc_ref, accs_ref):
        i = pl.program_id(0)
        lat = i < nbl
        gate = m_ref[...][5:6]
        o_val = o_ref[...]
        x = x_ref[...] + gate * o_val
        g = g_ref[...]
        rs = lax.rsqrt(jnp.mean(x * x, axis=-1, keepdims=True) + EPS)
        y = x * rs * g
        err = jnp.where(lat, y - t_ref[...], 0.0)
        loss = 0.5 * jnp.sum(jnp.mean(err * err, axis=-1, keepdims=True), axis=0, keepdims=True)
        dy = err * (1.0 / D_MODEL)
        dx, dg_rows, _ = _rms_bwd(x, g, dy)
        dx_ref[...] = dx
        do_ref[...] = (gate * dx).astype(BF16)

        @pl.when(i == 0)
        def _():
            acc_ref[...] = jnp.zeros_like(acc_ref)

        @pl.when((i == 0) | (i == nbl))
        def _():
            accs_ref[...] = jnp.zeros_like(accs_ref)

        acc_ref[...] += _rows([jnp.sum(dg_rows, axis=0, keepdims=True), jnp.broadcast_to(loss, (1, D_MODEL))]
                              + [None] * 6)
        accs_ref[...] += _rows([None, None, jnp.sum(dx * o_val, axis=0, keepdims=True)] + [None] * 5)

    return pl.pallas_call(
        body, name=name, grid=(r // tm,), in_specs=[row, row, mod, vec, tgt],
        out_specs=[row, row, acc_all, acc_stream],
        out_shape=[jax.ShapeDtypeStruct((r, D_MODEL), F32), jax.ShapeDtypeStruct((r, D_MODEL), BF16),
                   jax.ShapeDtypeStruct((8, D_MODEL), F32), jax.ShapeDtypeStruct((2, 8, D_MODEL), F32)],
        compiler_params=_params())(xs, o, mods, g_final, target)


def _norm_bwd(xs, g, mods, dh, dx_in, n_lat, sh, sc, name, res=None):
    r = xs.shape[0]
    tm = ROW_TILE
    nbl, row, mod, vec = _row_specs(r, n_lat, tm)
    acc_all, acc_stream = _acc_specs(nbl)
    has_res = res is not None

    def body(*refs):
        if has_res:
            x_ref, g_ref, m_ref, dh_ref, dxi_ref, o_ref, mr_ref, dx_ref, do_ref, acc_ref, accs_ref = refs
        else:
            x_ref, g_ref, m_ref, dh_ref, dxi_ref, dx_ref, acc_ref, accs_ref = refs
        i = pl.program_id(0)
        x, g, m, dhv = x_ref[...], g_ref[...], m_ref[...], dh_ref[...].astype(F32)
        dy = dhv * (1.0 + m[sc:sc + 1])
        dxn, dg_rows, xh = _rms_bwd(x, g, dy)
        dx = dxi_ref[...] + dxn
        if has_res:
            dx_ref[...] = dx
        else:
            @pl.when(i < nbl)
            def _():
                dx_ref[...] = dx
        d_gate = None
        if has_res:
            o_val = o_ref[...]
            do_ref[...] = (mr_ref[...][res[2]:res[2] + 1] * dx).astype(BF16)
            d_gate = jnp.sum(dx * o_val, axis=0, keepdims=True)

        @pl.when(i == 0)
        def _():
            acc_ref[...] = jnp.zeros_like(acc_ref)

        @pl.when((i == 0) | (i == nbl))
        def _():
            accs_ref[...] = jnp.zeros_like(accs_ref)

        acc_ref[...] += _rows([jnp.sum(dg_rows, axis=0, keepdims=True)] + [None] * 7)
        accs_ref[...] += _rows([jnp.sum(dhv, axis=0, keepdims=True),
                                jnp.sum(dhv * (xh * g), axis=0, keepdims=True), d_gate] + [None] * 5)

    ins = [xs, g, mods, dh, dx_in]
    in_specs = [row, vec, mod, row, row]
    if has_res:
        out_specs, out_shape = [row], [jax.ShapeDtypeStruct((r, D_MODEL), F32)]
    else:
        out_specs = [pl.BlockSpec((tm, D_MODEL), lambda i: (jnp.minimum(i, nbl - 1), 0))]
        out_shape = [jax.ShapeDtypeStruct((n_lat, D_MODEL), F32)]
    if has_res:
        ins += [res[0], res[1]]
        in_specs += [row, mod]
        out_specs.append(row)
        out_shape.append(jax.ShapeDtypeStruct((r, D_MODEL), BF16))
    out_specs += [acc_all, acc_stream]
    out_shape += [jax.ShapeDtypeStruct((8, D_MODEL), F32), jax.ShapeDtypeStruct((2, 8, D_MODEL), F32)]
    return pl.pallas_call(body, name=name, grid=(r // tm,), in_specs=in_specs, out_specs=out_specs,
                          out_shape=out_shape, compiler_params=_params())(*ins)


def _rotate(t, cos, sin):
    width = t.shape[1]
    reps = width // LANES
    lane = lax.broadcasted_iota(jnp.int32, (1, width), 1)
    first = (lane % HEAD_DIM) < (HEAD_DIM // 2)
    swapped = jnp.where(first, pltpu.roll(t, width - HEAD_DIM // 2, 1), pltpu.roll(t, HEAD_DIM // 2, 1))
    return t * jnp.tile(cos, (1, reps)) + swapped * jnp.tile(sin, (1, reps))


def _rope_fwd(z, cos, sin, name):
    r = z.shape[0]
    tm = ROW_TILE
    kvw = 2 * N_KV_HEADS * HEAD_DIM
    tab = pl.BlockSpec((tm, LANES), lambda i: (i, 0))

    def body(q_ref, kv_ref, c_ref, s_ref, o_ref):
        c, s = c_ref[...], s_ref[...]
        kv = kv_ref[...]
        o_ref[:, :D_MODEL] = (_rotate(q_ref[...].astype(F32), c, s) * (HEAD_DIM ** -0.5)).astype(BF16)
        o_ref[:, D_MODEL:D_MODEL + kvw // 2] = _rotate(kv[:, :kvw // 2].astype(F32), c, s).astype(BF16)
        o_ref[:, D_MODEL + kvw // 2:] = kv[:, kvw // 2:]

    return pl.pallas_call(
        body, name=name, grid=(r // tm,),
        in_specs=[pl.BlockSpec((tm, D_MODEL), lambda i: (i, P_Q // D_MODEL)),
                  pl.BlockSpec((tm, kvw), lambda i: (i, P_KV // kvw)), tab, tab],
        out_specs=pl.BlockSpec((tm, D_MODEL + kvw), lambda i: (i, 0)),
        out_shape=jax.ShapeDtypeStruct((r, D_MODEL + kvw), BF16), compiler_params=_params())(z, z, cos, sin)


def _rope_bwd(dq, dk, dv, cos, sin, name):
    r = dq.shape[0]
    tm = ROW_TILE
    kw = N_KV_HEADS * HEAD_DIM
    tab = pl.BlockSpec((tm, LANES), lambda i: (i, 0))

    def body(dq_ref, dk_ref, dv_ref, c_ref, s_ref, o_ref):
        c, s = c_ref[...], -s_ref[...]
        o_ref[:, :D_MODEL] = (_rotate(dq_ref[...], c, s) * (HEAD_DIM ** -0.5)).astype(BF16)
        o_ref[:, D_MODEL:D_MODEL + kw] = _rotate(dk_ref[...], c, s).astype(BF16)
        o_ref[:, D_MODEL + kw:] = dv_ref[...].astype(BF16)

    return pl.pallas_call(
        body, name=name, grid=(r // tm,),
        in_specs=[pl.BlockSpec((tm, D_MODEL), lambda i: (i, 0)), pl.BlockSpec((tm, kw), lambda i: (i, 0)),
                  pl.BlockSpec((tm, kw), lambda i: (i, 0)), tab, tab],
        out_specs=pl.BlockSpec((tm, D_MODEL + 2 * kw), lambda i: (i, 0)),
        out_shape=jax.ShapeDtypeStruct((r, D_MODEL + 2 * kw), BF16), compiler_params=_params())(dq, dk, dv, cos, sin)


def _attn_setup(i, n_lat, m_ctx, nbl, k_ref, v_ref):
    start = pl.multiple_of(jnp.clip((i - 1) * BLK, 0, n_lat - 3 * BLK), BLK)
    nkeys = 3 * BLK + m_ctx
    rows = lax.broadcasted_iota(jnp.int32, (4 * BLK, nkeys), 0)
    cols = lax.broadcasted_iota(jnp.int32, (4 * BLK, nkeys), 1)
    qpos = i * BLK + (rows & (BLK - 1))
    seen = (cols >= 3 * BLK) | ((jnp.abs(start + cols - qpos) <= WINDOW) & (i < nbl))
    mask = jnp.where(seen, 0.0, NEG)
    kblk = jnp.concatenate([k_ref[pl.ds(start, 3 * BLK), :], k_ref[pl.ds(n_lat, m_ctx), :]], axis=0)
    vblk = jnp.concatenate([v_ref[pl.ds(start, 3 * BLK), :], v_ref[pl.ds(n_lat, m_ctx), :]], axis=0)
    lo = lax.broadcasted_iota(jnp.int32, (1, LANES), 1) < HEAD_DIM
    return start, mask, kblk, vblk, lo


def _stack_heads(ref, kh, lo):
    a = ref[:, (2 * kh) * LANES:(2 * kh + 1) * LANES]
    b = ref[:, (2 * kh + 1) * LANES:(2 * kh + 2) * LANES]
    z = jnp.zeros_like(a)
    return jnp.concatenate([jnp.where(lo, a, z), jnp.where(lo, z, a), jnp.where(lo, b, z), jnp.where(lo, z, b)],
                           axis=0)


def _kv_variants(blk, rolled, kh, lo):
    z = jnp.zeros_like(blk)
    if kh == 0:
        return jnp.where(lo, blk, rolled), jnp.where(lo, blk, z), jnp.where(lo, z, rolled)
    return jnp.where(lo, rolled, blk), jnp.where(lo, rolled, z), jnp.where(lo, z, blk)


NN = (((1,), (0,)), ((), ()))
NT = (((1,), (1,)), ((), ()))
TN = (((0,), (0,)), ((), ()))


def _scores(qst, k2, mask, sink_ref, p, kh):
    s = lax.dot_general(qst, k2, NT, preferred_element_type=F32) + mask
    snk = jnp.concatenate([jnp.full((BLK, 1), sink_ref[p * 8 + kh * 4 + g], F32) for g in range(4)], axis=0)
    return s, snk


def _attn_fwd(qkv, sink, n_lat, m_ctx, name, carry=None):
    r = qkv.shape[0]
    nb, nbl = r // BLK, n_lat // BLK
    kcol = D_MODEL // LANES

    def body(sink_ref, q_ref, k_ref, v_ref, o_ref, lse_ref):
        p, i = pl.program_id(0), pl.program_id(1)
        _, mask, kblk, vblk, lo = _attn_setup(i, n_lat, m_ctx, nbl, k_ref, v_ref)
        kr, vr = pltpu.roll(kblk, HEAD_DIM, 1), pltpu.roll(vblk, HEAD_DIM, 1)
        for kh in range(2):
            k2, _, _ = _kv_variants(kblk, kr, kh, lo)
            _, vlo, vhi = _kv_variants(vblk, vr, kh, lo)
            qst = _stack_heads(q_ref, kh, lo)
            s, snk = _scores(qst, k2, mask, sink_ref, p, kh)
            mx = jnp.maximum(jnp.max(s, axis=-1, keepdims=True), snk)
            pe = jnp.exp(s - mx)
            den = jnp.sum(pe, axis=-1, keepdims=True) + jnp.exp(snk - mx)
            inv = 1.0 / den
            pb = pe.astype(BF16)
            for jp in range(2):
                r0 = 2 * jp * BLK
                pair = (jnp.dot(pb[r0:r0 + BLK], vlo, preferred_element_type=F32)
                        + jnp.dot(pb[r0 + BLK:r0 + 2 * BLK], vhi, preferred_element_type=F32))
                pair = pair * jnp.where(lo, inv[r0:r0 + BLK], inv[r0 + BLK:r0 + 2 * BLK])
                o_ref[:, (2 * kh + jp) * LANES:(2 * kh + jp + 1) * LANES] = pair.astype(BF16)
            lse = mx + jnp.log(den)
            for g in range(4):
                lse_ref[:, kh * 4 + g:kh * 4 + g + 1] = lse[g * BLK:(g + 1) * BLK]

    outs, carried = _pcall(
        body, (sink, qkv, qkv, qkv), name=name, grid=(2, nb),
        in_specs=[SMEM_SPEC,
                  pl.BlockSpec((BLK, 4 * LANES), lambda p, i: (i, p)),
                  pl.BlockSpec((r, LANES), lambda p, i: (0, kcol + p)),
                  pl.BlockSpec((r, LANES), lambda p, i: (0, kcol + 2 + p))],
        out_specs=[pl.BlockSpec((BLK, 4 * LANES), lambda p, i: (i, p)),
                   pl.BlockSpec((None, BLK, 8), lambda p, i: (p, i, 0))],
        out_shape=[jax.ShapeDtypeStruct((r, D_MODEL), BF16), jax.ShapeDtypeStruct((2, r, 8), F32)], carry=carry)
    return outs if carry is None else (outs, carried)


def _attn_bwd(qkv, sink, o, do, lse, n_lat, m_ctx, name, carry=None):
    r = qkv.shape[0]
    nb, nbl = r // BLK, n_lat // BLK
    kcol = D_MODEL // LANES

    def body(sink_ref, q_ref, k_ref, v_ref, o_ref, do_ref, lse_ref, dq_ref, dk_ref, dv_ref, ds_ref):
        p, i = pl.program_id(0), pl.program_id(1)

        @pl.when(i == 0)
        def _():
            dk_ref[...] = jnp.zeros_like(dk_ref)
            dv_ref[...] = jnp.zeros_like(dv_ref)
            ds_ref[...] = jnp.zeros_like(ds_ref)

        start, mask, kblk, vblk, lo = _attn_setup(i, n_lat, m_ctx, nbl, k_ref, v_ref)
        kr, vr = pltpu.roll(kblk, HEAD_DIM, 1), pltpu.roll(vblk, HEAD_DIM, 1)
        lane = lax.broadcasted_iota(jnp.int32, (1, LANES), 1)
        dks, dvs = [], []
        for kh in range(2):
            k2, klo, khi = _kv_variants(kblk, kr, kh, lo)
            v2, _, _ = _kv_variants(vblk, vr, kh, lo)
            qst = _stack_heads(q_ref, kh, lo)
            dost = _stack_heads(do_ref, kh, lo)
            s, snk = _scores(qst, k2, mask, sink_ref, p, kh)
            lse4 = jnp.concatenate([lse_ref[:, kh * 4 + g:kh * 4 + g + 1] for g in range(4)], axis=0)
            pe = jnp.exp(s - lse4)
            dp = lax.dot_general(dost, v2, NT, preferred_element_type=F32)
            deltas = []
            for jp in range(2):
                cols = slice((2 * kh + jp) * LANES, (2 * kh + jp + 1) * LANES)
                prod = do_ref[:, cols].astype(F32) * o_ref[:, cols].astype(F32)
                deltas.append(jnp.sum(jnp.where(lo, prod, 0.0), axis=-1, keepdims=True))
                deltas.append(jnp.sum(jnp.where(lo, 0.0, prod), axis=-1, keepdims=True))
            delta = jnp.concatenate(deltas, axis=0)
            dsc = pe * (dp - delta)
            dsb, pb = dsc.astype(BF16), pe.astype(BF16)
            for jp in range(2):
                r0 = 2 * jp * BLK
                dq_ref[:, (2 * kh + jp) * LANES:(2 * kh + jp + 1) * LANES] = (
                    jnp.dot(dsb[r0:r0 + BLK], klo, preferred_element_type=F32)
                    + jnp.dot(dsb[r0 + BLK:r0 + 2 * BLK], khi, preferred_element_type=F32))
            dkf = lax.dot_general(dsb, qst, TN, preferred_element_type=F32)
            dvf = lax.dot_general(pb, dost, TN, preferred_element_type=F32)
            dks.append(dkf + pltpu.roll(dkf, HEAD_DIM, 1))
            dvs.append(dvf + pltpu.roll(dvf, HEAD_DIM, 1))
            contrib = -jnp.exp(snk - lse4) * delta
            for g in range(4):
                tot = jnp.sum(contrib[g * BLK:(g + 1) * BLK], axis=0, keepdims=True)
                ds_ref[0:1, :] += jnp.where(lane == kh * 4 + g, tot, 0.0)
        dk_blk = jnp.where(lo, dks[0], dks[1])
        dv_blk = jnp.where(lo, dvs[0], dvs[1])
        dk_ref[pl.ds(start, 3 * BLK), :] += dk_blk[:3 * BLK]
        dk_ref[pl.ds(n_lat, m_ctx), :] += dk_blk[3 * BLK:]
        dv_ref[pl.ds(start, 3 * BLK), :] += dv_blk[:3 * BLK]
        dv_ref[pl.ds(n_lat, m_ctx), :] += dv_blk[3 * BLK:]

    qspec = pl.BlockSpec((BLK, 4 * LANES), lambda p, i: (i, p))
    outs, carried = _pcall(
        body, (sink, qkv, qkv, qkv, o, do, lse), name=name, grid=(2, nb),
        in_specs=[SMEM_SPEC, qspec,
                  pl.BlockSpec((r, LANES), lambda p, i: (0, kcol + p)),
                  pl.BlockSpec((r, LANES), lambda p, i: (0, kcol + 2 + p)),
                  qspec, qspec, pl.BlockSpec((None, BLK, 8), lambda p, i: (p, i, 0))],
        out_specs=[qspec, pl.BlockSpec((r, LANES), lambda p, i: (0, p)), pl.BlockSpec((r, LANES), lambda p, i: (0, p)),
                   pl.BlockSpec((None, 8, LANES), lambda p, i: (p, 0, 0))],
        out_shape=[jax.ShapeDtypeStruct((r, D_MODEL), F32), jax.ShapeDtypeStruct((r, 2 * LANES), F32),
                   jax.ShapeDtypeStruct((r, 2 * LANES), F32), jax.ShapeDtypeStruct((2, 8, LANES), F32)], carry=carry)
    return outs if carry is None else (outs, carried)


def _gating_parts(z_ref, gv_ref):
    za = z_ref[...].astype(F32)
    zg = jax.nn.gelu(za)
    u, v = zg[:, :D_MODEL], zg[:, D_MODEL:]
    rs = lax.rsqrt(jnp.mean(v * v, axis=-1, keepdims=True) + EPS)
    return za, u, v, rs, v * rs * gv_ref[...]


def _mix(w_ref, vals):
    vb = vals.astype(BF16)
    return jnp.concatenate(
        [jnp.dot(w_ref[g], vb[:, g * LANES:(g + 1) * LANES], preferred_element_type=F32) for g in range(A_GROUPS)],
        axis=1)


def _gating_fwd(z, ws, bias, g_v, name):
    r = z.shape[0]

    def body(z_ref, w_ref, b_ref, gv_ref, y_ref):
        _, u, _, _, vn = _gating_parts(z_ref, gv_ref)
        y_ref[...] = (u * (_mix(w_ref, vn) + b_ref[...])).astype(BF16)

    return pl.pallas_call(
        body, name=name, grid=(r // CHUNK,),
        in_specs=[pl.BlockSpec((CHUNK, 2 * D_MODEL), lambda i: (i, P_A // (2 * D_MODEL))),
                  pl.BlockSpec((A_GROUPS, CHUNK, CHUNK), lambda i: (0, 0, 0)),
                  pl.BlockSpec((CHUNK, D_MODEL), lambda i: (0, 0)),
                  pl.BlockSpec((1, D_MODEL), lambda i: (0, 0))],
        out_specs=pl.BlockSpec((CHUNK, D_MODEL), lambda i: (i, 0)),
        out_shape=jax.ShapeDtypeStruct((r, D_MODEL), BF16), compiler_params=_params())(z, ws, bias, g_v)


def _gating_bwd(z, ws, ws_t, bias, g_v, dy, name, carry=None):
    r = z.shape[0]

    def body(z_ref, w_ref, wt_ref, b_ref, gv_ref, dy_ref, dz_ref, dw_ref, db_ref, dg_ref):
        i = pl.program_id(0)

        @pl.when(i == 0)
        def _():
            dw_ref[...] = jnp.zeros_like(dw_ref)
            db_ref[...] = jnp.zeros_like(db_ref)
            dg_ref[...] = jnp.zeros_like(dg_ref)

        za, u, v, rs, vn = _gating_parts(z_ref, gv_ref)
        dyv = dy_ref[...].astype(F32)
        du = dyv * (_mix(w_ref, vn) + b_ref[...])
        dmixed = dyv * u
        dvn = _mix(wt_ref, dmixed)
        dmb, vnb = dmixed.astype(BF16), vn.astype(BF16)
        for g in range(A_GROUPS):
            cols = slice(g * LANES, (g + 1) * LANES)
            dw_ref[g] += lax.dot_general(dmb[:, cols], vnb[:, cols], NT, preferred_element_type=F32)
            db_ref[:, g:g + 1] += jnp.sum(dmixed[:, cols], axis=-1, keepdims=True)
        gv = gv_ref[...]
        vh = v * rs
        dg_ref[0:1, :] += jnp.sum(dvn * vh, axis=0, keepdims=True)
        dvh = dvn * gv
        dv = rs * (dvh - vh * jnp.mean(dvh * vh, axis=-1, keepdims=True))
        _, vjp = jax.vjp(jax.nn.gelu, za)
        dz_ref[...] = vjp(jnp.concatenate([du, dv], axis=1))[0].astype(BF16)

    wspec = pl.BlockSpec((A_GROUPS, CHUNK, CHUNK), lambda i: (0, 0, 0))
    outs, carried = _pcall(
        body, (z, ws, ws_t, bias, g_v, dy), name=name, grid=(r // CHUNK,),
        in_specs=[pl.BlockSpec((CHUNK, 2 * D_MODEL), lambda i: (i, P_A // (2 * D_MODEL))), wspec, wspec,
                  pl.BlockSpec((CHUNK, D_MODEL), lambda i: (0, 0)), pl.BlockSpec((1, D_MODEL), lambda i: (0, 0)),
                  pl.BlockSpec((CHUNK, D_MODEL), lambda i: (i, 0))],
        out_specs=[pl.BlockSpec((CHUNK, 2 * D_MODEL), lambda i: (i, 0)), wspec,
                   pl.BlockSpec((CHUNK, A_GROUPS), lambda i: (0, 0)), pl.BlockSpec((8, D_MODEL), lambda i: (0, 0))],
        out_shape=[jax.ShapeDtypeStruct((r, 2 * D_MODEL), BF16), jax.ShapeDtypeStruct((A_GROUPS, CHUNK, CHUNK), F32),
                   jax.ShapeDtypeStruct((CHUNK, A_GROUPS), F32), jax.ShapeDtypeStruct((8, D_MODEL), F32)], carry=carry)
    return outs if carry is None else (outs, carried)


def _scr_rows(j, n_lat):
    s = pl.multiple_of(j * CONV_CHUNK, CONV_CHUNK)
    shift = jnp.where(j >= n_lat // CONV_CHUNK, 2 * PAD, PAD)
    return s, pl.multiple_of(s + shift, PAD)


def _taps(scr, j, n_lat):
    c = CONV_CHUNK
    s, at = _scr_rows(j, n_lat)
    ext = scr[pl.ds(pl.multiple_of(at - PAD, PAD), c + 2 * PAD), :]
    xm = pltpu.roll(ext, 1, 0)[PAD:PAD + c]
    xp = pltpu.roll(ext, c + 2 * PAD - 1, 0)[PAD:PAD + c]
    return s, xm, ext[PAD:PAD + c], xp


def _zero_pads(scr, r, n_lat):
    for at in (0, PAD + n_lat, 2 * PAD + r):
        scr[pl.ds(at, PAD), :] = jnp.zeros((PAD, LANES), F32)


def _col(arr_cols, c0):
    return pl.BlockSpec((arr_cols, LANES), lambda c: (0, c0 + c))


def _ffn_conv_fwd(up, w, n_lat, name):
    r = up.shape[0]
    nct = D_FF // LANES
    nchunk = r // CONV_CHUNK

    def body(a_ref, g_ref, w_ref, f_ref, scr):
        _zero_pads(scr, r, n_lat)

        def fill(j, _):
            s, at = _scr_rows(j, n_lat)
            scr[pl.ds(at, CONV_CHUNK), :] = a_ref[pl.ds(s, CONV_CHUNK), :].astype(F32)
            return 0
        lax.fori_loop(0, nchunk, fill, 0)
        wv = w_ref[...]

        def step(j, _):
            s, xm, x0, xp = _taps(scr, j, n_lat)
            ca = wv[0:1] * xm + wv[1:2] * x0 + wv[2:3] * xp
            gv = g_ref[pl.ds(s, CONV_CHUNK), :].astype(F32)
            f_ref[pl.ds(s, CONV_CHUNK), :] = (ca * _sigmoid(ca) * gv).astype(BF16)
            return 0
        lax.fori_loop(0, nchunk, step, 0)

    return pl.pallas_call(
        body, name=name, grid=(nct,),
        in_specs=[_col(r, 0), _col(r, nct), _col(3, 0)],
        out_specs=_col(r, 0), out_shape=jax.ShapeDtypeStruct((r, D_FF), BF16),
        scratch_shapes=[pltpu.VMEM((r + 3 * PAD, LANES), F32)], compiler_params=_params())(up, up, w)


def _ffn_conv_bwd(up, w, df, n_lat, name, carry=None):
    r = up.shape[0]
    nct = D_FF // LANES
    nchunk = r // CONV_CHUNK

    def body(a_ref, g_ref, w_ref, df_ref, dup_ref, dw_ref, scr, scr2):
        _zero_pads(scr, r, n_lat)
        _zero_pads(scr2, r, n_lat)

        def fill(j, _):
            s, at = _scr_rows(j, n_lat)
            scr[pl.ds(at, CONV_CHUNK), :] = a_ref[pl.ds(s, CONV_CHUNK), :].astype(F32)
            return 0
        lax.fori_loop(0, nchunk, fill, 0)
        wv = w_ref[...]

        def first(j, carry):
            s, xm, x0, xp = _taps(scr, j, n_lat)
            ca = wv[0:1] * xm + wv[1:2] * x0 + wv[2:3] * xp
            sg = _sigmoid(ca)
            gv = g_ref[pl.ds(s, CONV_CHUNK), :].astype(F32)
            dfv = df_ref[pl.ds(s, CONV_CHUNK), :].astype(F32)
            dup_ref[1, pl.ds(s, CONV_CHUNK), :] = (dfv * ca * sg).astype(BF16)
            dca = dfv * gv * (sg * (1.0 + ca * (1.0 - sg)))
            scr2[pl.ds(_scr_rows(j, n_lat)[1], CONV_CHUNK), :] = dca
            return tuple(cw + jnp.sum(dca * xv, axis=0, keepdims=True) for cw, xv in zip(carry, (xm, x0, xp)))
        zero = jnp.zeros((1, LANES), F32)
        dws = lax.fori_loop(0, nchunk, first, (zero, zero, zero))
        dw_ref[...] = _rows(list(dws) + [None] * 5)

        def second(j, _):
            s, ym, y0, yp = _taps(scr2, j, n_lat)
            dup_ref[0, pl.ds(s, CONV_CHUNK), :] = (wv[0:1] * yp + wv[1:2] * y0 + wv[2:3] * ym).astype(BF16)
            return 0
        lax.fori_loop(0, nchunk, second, 0)

    outs, carried = _pcall(
        body, (up, up, w, df), name=name, grid=(nct,),
        in_specs=[_col(r, 0), _col(r, nct), _col(3, 0), _col(r, 0)],
        out_specs=[pl.BlockSpec((2, r, LANES), lambda c: (0, 0, c)), _col(8, 0)],
        out_shape=[jax.ShapeDtypeStruct((2, r, D_FF), BF16), jax.ShapeDtypeStruct((8, D_FF), F32)],
        scratch_shapes=[pltpu.VMEM((r + 3 * PAD, LANES), F32), pltpu.VMEM((r + 3 * PAD, LANES), F32)], carry=carry)
    return outs if carry is None else (outs, carried)


def _sconv_fwd(z, w, n_lat, name):
    r = z.shape[0]
    nct = D_MODEL // LANES
    nchunk = r // CONV_CHUNK
    c0 = P_B // LANES

    def body(bg_ref, cg_ref, hb_ref, w_ref, y_ref, scr):
        _zero_pads(scr, r, n_lat)

        def fill(j, _):
            s, at = _scr_rows(j, n_lat)
            rows = pl.ds(s, CONV_CHUNK)
            scr[pl.ds(at, CONV_CHUNK), :] = cg_ref[rows, :].astype(F32) * hb_ref[rows, :].astype(F32)
            return 0
        lax.fori_loop(0, nchunk, fill, 0)
        wv = w_ref[...]

        def step(j, _):
            s, xm, x0, xp = _taps(scr, j, n_lat)
            conv = wv[0:1] * xm + wv[1:2] * x0 + wv[2:3] * xp
            y_ref[pl.ds(s, CONV_CHUNK), :] = (bg_ref[pl.ds(s, CONV_CHUNK), :].astype(F32) * conv).astype(BF16)
            return 0
        lax.fori_loop(0, nchunk, step, 0)

    return pl.pallas_call(
        body, name=name, grid=(nct,),
        in_specs=[_col(r, c0), _col(r, c0 + nct), _col(r, c0 + 2 * nct), _col(3, 0)],
        out_specs=_col(r, 0), out_shape=jax.ShapeDtypeStruct((r, D_MODEL), BF16),
        scratch_shapes=[pltpu.VMEM((r + 3 * PAD, LANES), F32)], compiler_params=_params())(z, z, z, w)


def _sconv_bwd(z, w, dy, n_lat, name):
    r = z.shape[0]
    nct = D_MODEL // LANES
    nchunk = r // CONV_CHUNK
    c0 = P_B // LANES

    def body(bg_ref, cg_ref, hb_ref, w_ref, dy_ref, dz_ref, dw_ref, scr, scr2):
        _zero_pads(scr, r, n_lat)
        _zero_pads(scr2, r, n_lat)

        def fill(j, _):
            s, at = _scr_rows(j, n_lat)
            rows = pl.ds(s, CONV_CHUNK)
            scr[pl.ds(at, CONV_CHUNK), :] = cg_ref[rows, :].astype(F32) * hb_ref[rows, :].astype(F32)
            return 0
        lax.fori_loop(0, nchunk, fill, 0)
        wv = w_ref[...]

        def first(j, carry):
            s, xm, x0, xp = _taps(scr, j, n_lat)
            rows = pl.ds(s, CONV_CHUNK)
            conv = wv[0:1] * xm + wv[1:2] * x0 + wv[2:3] * xp
            dyv = dy_ref[rows, :].astype(F32)
            dz_ref[0, rows, :] = (dyv * conv).astype(BF16)
            dconv = dyv * bg_ref[rows, :].astype(F32)
            scr2[pl.ds(_scr_rows(j, n_lat)[1], CONV_CHUNK), :] = dconv
            return tuple(cw + jnp.sum(dconv * xv, axis=0, keepdims=True) for cw, xv in zip(carry, (xm, x0, xp)))
        zero = jnp.zeros((1, LANES), F32)
        dws = lax.fori_loop(0, nchunk, first, (zero, zero, zero))
        dw_ref[...] = _rows(list(dws) + [None] * 5)

        def second(j, _):
            s, ym, y0, yp = _taps(scr2, j, n_lat)
            rows = pl.ds(s, CONV_CHUNK)
            dx = wv[0:1] * yp + wv[1:2] * y0 + wv[2:3] * ym
            dz_ref[1, rows, :] = (dx * hb_ref[rows, :].astype(F32)).astype(BF16)
            dz_ref[2, rows, :] = (dx * cg_ref[rows, :].astype(F32)).astype(BF16)
            return 0
        lax.fori_loop(0, nchunk, second, 0)

    return pl.pallas_call(
        body, name=name, grid=(nct,),
        in_specs=[_col(r, c0), _col(r, c0 + nct), _col(r, c0 + 2 * nct), _col(3, 0), _col(r, 0)],
        out_specs=[pl.BlockSpec((3, r, LANES), lambda c: (0, 0, c)), _col(8, 0)],
        out_shape=[jax.ShapeDtypeStruct((3, r, D_MODEL), BF16), jax.ShapeDtypeStruct((8, D_MODEL), F32)],
        scratch_shapes=[pltpu.VMEM((r + 3 * PAD, LANES), F32), pltpu.VMEM((r + 3 * PAD, LANES), F32)],
        compiler_params=_params())(z, z, z, w, dy)


def _merge_fwd(z, b_gate, ps, name):
    r = z.shape[0]
    tm = ROW_TILE
    row = pl.BlockSpec((tm, D_MODEL), lambda i: (i, 0))

    def body(zg_ref, b_ref, p0_ref, p1_ref, p2_ref, o_ref):
        gates = _sigmoid(zg_ref[...].astype(F32) + b_ref[...])
        acc = None
        for t, p_ref in enumerate((p0_ref, p1_ref, p2_ref)):
            term = gates[:, t * D_MODEL:(t + 1) * D_MODEL] * p_ref[...].astype(F32)
            acc = term if acc is None else acc + term
        o_ref[...] = acc.astype(BF16)

    return pl.pallas_call(
        body, name=name, grid=(r // tm,),
        in_specs=[pl.BlockSpec((tm, 3 * D_MODEL), lambda i: (i, 0)), pl.BlockSpec((1, 3 * D_MODEL), lambda i: (0, 0)),
                  row, row, row],
        out_specs=row, out_shape=jax.ShapeDtypeStruct((r, D_MODEL), BF16), compiler_params=_params())(z, b_gate, *ps)


def _merge_bwd(z, b_gate, ps, dmerged, name):
    r = z.shape[0]
    tm = ROW_TILE
    row = pl.BlockSpec((tm, D_MODEL), lambda i: (i, 0))
    wide = pl.BlockSpec((tm, 3 * D_MODEL), lambda i: (i, 0))

    def body(zg_ref, b_ref, p0_ref, p1_ref, p2_ref, dm_ref, d0_ref, d1_ref, d2_ref, dz_ref, db_ref):
        @pl.when(pl.program_id(0) == 0)
        def _():
            db_ref[...] = jnp.zeros_like(db_ref)

        gates = _sigmoid(zg_ref[...].astype(F32) + b_ref[...])
        dm = dm_ref[...].astype(F32)
        for t, (p_ref, d_ref) in enumerate(((p0_ref, d0_ref), (p1_ref, d1_ref), (p2_ref, d2_ref))):
            cols = slice(t * D_MODEL, (t + 1) * D_MODEL)
            gt = gates[:, cols]
            d_ref[...] = (dm * gt).astype(BF16)
            dlogit = dm * p_ref[...].astype(F32) * gt * (1.0 - gt)
            dz_ref[:, cols] = dlogit.astype(BF16)
            db_ref[0:1, cols] += jnp.sum(dlogit, axis=0, keepdims=True)

    shp = jax.ShapeDtypeStruct((r, D_MODEL), BF16)
    return pl.pallas_call(
        body, name=name, grid=(r // tm,),
        in_specs=[wide, pl.BlockSpec((1, 3 * D_MODEL), lambda i: (0, 0)), row, row, row, row],
        out_specs=[row, row, row, wide, pl.BlockSpec((8, 3 * D_MODEL), lambda i: (0, 0))],
        out_shape=[shp, shp, shp, jax.ShapeDtypeStruct((r, 3 * D_MODEL), BF16),
                   jax.ShapeDtypeStruct((8, 3 * D_MODEL), F32)],
        compiler_params=_params())(z, b_gate, *ps, dmerged)


def _sum_slots(buf, name):
    s, rows, _ = buf.shape
    whole_bytes = s * rows * LANES * 4
    tr = rows if whole_bytes <= VMEM_LIMIT // 8 else _pick(rows, [512, 256, 128, 64, 32, 16, 8])

    def body(b_ref, o_ref):
        acc = b_ref[0]
        for t in range(1, s):
            acc = acc + b_ref[t]
        o_ref[...] = acc

    return pl.pallas_call(
        body, name=name, grid=(rows // tr,),
        in_specs=[pl.BlockSpec((s, tr, LANES), lambda i: (0, i, 0))],
        out_specs=pl.BlockSpec((tr, LANES), lambda i: (i, 0)),
        out_shape=jax.ShapeDtypeStruct((rows, LANES), F32), compiler_params=_params())(buf)


def _adamw(w, gsrcs, m, v, name, scale=None):
    nl, rows, cols = w.shape
    assert len(gsrcs) == nl
    s = gsrcs[0].shape[0]
    tr = _pick(rows, [128, 64, 32, 16, 8])
    blk = pl.BlockSpec((None, tr, cols), lambda l, i: (l, i, 0))
    c1 = 1.0 / (1.0 - ADAM_B1 ** ADAM_STEP)
    c2 = 1.0 / (1.0 - ADAM_B2 ** ADAM_STEP)

    def gspec(t):
        return pl.BlockSpec((s, tr, cols), lambda l, i: (0, jnp.where(l == t, i, 0), 0))

    def body(*refs):
        w_ref, g_refs, (m_ref, v_ref) = refs[0], refs[1:1 + nl], refs[1 + nl:3 + nl]
        rest = refs[3 + nl:]
        if scale is not None:
            sc_ref, rest = rest[0], rest[1:]
        go_ref, d_ref, mo_ref, vo_ref = rest
        layer = pl.program_id(0)
        g = None
        for t in range(nl):
            gt = g_refs[t][0].astype(F32)
            for q in range(1, s):
                gt = gt + g_refs[t][q].astype(F32)
            g = gt if g is None else jnp.where(layer == t, gt, g)
        if scale is not None:
            g = g * sc_ref[...]
        mn = ADAM_B1 * m_ref[...] + (1.0 - ADAM_B1) * g
        vn = ADAM_B2 * v_ref[...] + (1.0 - ADAM_B2) * (g * g)
        go_ref[...] = g
        mo_ref[...] = mn
        vo_ref[...] = vn
        d_ref[...] = -ADAM_LR * ((mn * c1) / (jnp.sqrt(vn * c2) + ADAM_EPS) + ADAM_WD * w_ref[...])

    shp = jax.ShapeDtypeStruct((nl, rows, cols), F32)
    ins = [w] + list(gsrcs) + [m, v] + ([] if scale is None else [scale])
    return pl.pallas_call(
        body, name=name, grid=(nl, rows // tr),
        in_specs=[blk] + [gspec(t) for t in range(nl)] + [blk, blk] + ([] if scale is None else [blk]),
        out_specs=[blk] * 4, out_shape=[shp] * 4, compiler_params=_params())(*ins)


def _pack(arrs):
    flat = []
    for a in arrs:
        a = a.reshape(-1).astype(F32)
        pad = (-a.shape[0]) % (8 * LANES)
        flat.append(jnp.pad(a, (0, pad)) if pad else a)
    return jnp.concatenate(flat).reshape(-1, LANES)


def _unpack(buf, shapes, lead=()):
    out, row = [], 0
    for shp in shapes:
        n = 1
        for d in shp:
            n *= d
        nrows = -(-n // (8 * LANES)) * 8
        piece = buf[..., row:row + nrows, :].reshape(lead + (nrows * LANES,))[..., :n]
        out.append(piece.reshape(lead + tuple(shp)))
        row += nrows
    return out


def _silu(x):
    return x * jax.nn.sigmoid(x)


def _rope_tables(n_lat, m_ctx):
    pos = jnp.arange(n_lat)
    row = (pos // GRID_W).astype(F32)
    col = (pos % GRID_W).astype(F32)
    half = HEAD_DIM // 2
    inv = ROPE_THETA ** (-jnp.arange(0, half, 2, dtype=F32) / half)
    ang = jnp.concatenate([row[:, None] * inv, col[:, None] * inv], axis=-1)
    cos, sin = jnp.cos(ang), jnp.sin(ang)
    cos2 = jnp.tile(jnp.concatenate([cos, cos], axis=-1), (1, LANES // HEAD_DIM))
    sin2 = jnp.tile(jnp.concatenate([-sin, sin], axis=-1), (1, LANES // HEAD_DIM))
    return (jnp.concatenate([cos2, jnp.ones((m_ctx, LANES), F32)], axis=0),
            jnp.concatenate([sin2, jnp.zeros((m_ctx, LANES), F32)], axis=0))


def _to_slots(full, axis):
    shp = full.shape
    new = shp[:axis] + (N_DEV, shp[axis] // N_DEV) + shp[axis + 1:]
    return jnp.moveaxis(full.reshape(new), axis, 0)


def _from_slots(slots, axis):
    moved = jnp.moveaxis(slots, 0, axis)
    shp = moved.shape
    return moved.reshape(shp[:axis] + (shp[axis] * shp[axis + 1],) + shp[axis + 2:])


def _permute_in(wt, name):
    blk = 512
    segs = [(OFF_G, IN_W), (0, OFF_K), (OFF_A, OFF_B), (OFF_B, OFF_G), (OFF_K, OFF_A)]

    def source(j):
        src, at = 0, 0
        for lo, hi in segs:
            n = (hi - lo) // blk
            src = src + jnp.where((j >= at) & (j < at + n), j - at + lo // blk, 0)
            at += n
        return (src, 0)

    def body(i_ref, o_ref):
        o_ref[...] = i_ref[...]

    return pl.pallas_call(
        body, name=name, grid=(IN_W // blk,), in_specs=[pl.BlockSpec((blk, D_MODEL), source)],
        out_specs=pl.BlockSpec((blk, D_MODEL), lambda j: (j, 0)), out_shape=jax.ShapeDtypeStruct(wt.shape, wt.dtype),
        compiler_params=_params())(wt)


def kernel(x, c, ctx, c_ctx, w_mod, b_mod, g_mix, w_in, b_gate, sink, w_spatial, b_spatial, g_v, w_sconv, w_branch, w_out, g_ffn, w_up, w_fconv, w_down, g_final, loss_target, m_c_ctx, m_w_mod, m_b_mod, m_g_mix, m_w_in, m_b_gate, m_sink, m_w_spatial, m_b_spatial, m_g_v, m_w_sconv, m_w_branch, m_w_out, m_g_ffn, m_w_up, m_w_fconv, m_w_down, m_g_final, v_c_ctx, v_w_mod, v_b_mod, v_g_mix, v_w_in, v_b_gate, v_sink, v_w_spatial, v_b_spatial, v_g_v, v_w_sconv, v_w_branch, v_w_out, v_g_ffn, v_w_up, v_w_fconv, v_w_down, v_g_final):
    n_lat, m_ctx = x.shape[1], ctx.shape[1]
    r = n_lat + m_ctx
    me = 4 * lax.axis_index("x") + 2 * lax.axis_index("y") + lax.axis_index("c")
    mod_w = w_mod.shape[2]
    bm = _pick(r, [768, 256])

    tr = lambda a: jnp.swapaxes(a, -1, -2)
    shard_axis = {"in": 0, "br": 1, "out": 0, "up": 0, "dn": 0}
    shards = {}
    for kind, wt in (("in", tr(w_in)), ("br", w_branch), ("out", w_out), ("up", tr(w_up)), ("dn", w_down)):
        wb = wt.astype(BF16)
        for l in range(DEPTH):
            shards[kind, l] = wb[l]
    full = {}

    def arrive(items, got):
        for key, slots in zip(items, got):
            wfull = _from_slots(slots, shard_axis[key[0]])
            full[key] = _permute_in(wfull, f"permute_in{key[1]}") if key[0] == "in" else wfull

    def gather_of(items):
        return [shards[key] for key in items], ["gather"] * len(items)

    small_shapes = [c.shape, w_sconv.shape, w_fconv.shape]
    (g_small,) = _exchange([_pack([c, w_sconv, w_fconv])], ["gather"], "gather_first")
    c_all, sconv_all, fconv_all = _unpack(g_small, small_shapes, lead=(N_DEV,))
    c_all = c_all.reshape(N_DEV, D_MODEL)
    w_sconv_full = _from_slots(sconv_all, 2)
    w_fconv_full = _from_slots(fconv_all, 2)

    act = jnp.concatenate([_silu(c_all), _silu(c_ctx)[None], jnp.zeros((7, D_MODEL), F32)], axis=0)
    mod_part = jnp.stack([_mm(act, w_mod[l], name=f"mod_fwd{l}", bm=16, bn=mod_w, bk=D_MODEL, out_dtype=F32)
                          for l in range(DEPTH)])
    (mod_all,) = _exchange([mod_part], ["gather"], "gather_mod")
    mod_full = _from_slots(mod_all, 2) + b_mod[:, None, :]
    mods = []
    for l in range(DEPTH):
        mine = lax.dynamic_index_in_dim(mod_full[l], me, axis=0, keepdims=False).reshape(6, D_MODEL)
        theirs = mod_full[l, N_DEV].reshape(6, D_MODEL)
        mods.append(jnp.pad(jnp.stack([mine, theirs]), ((0, 0), (0, 2), (0, 0))))

    cos, sin = _rope_tables(n_lat, m_ctx)
    xs = jnp.concatenate([x[0], ctx[0]], axis=0)
    ws_b = w_spatial.astype(BF16)
    ws_t = jnp.swapaxes(w_spatial, 2, 3).astype(BF16)
    vec = lambda a: a.reshape(1, -1)

    saved = []
    res = None
    for l in range(DEPTH):
        s = {}
        if res is None:
            s["x0"] = xs
            s["h"], got = _norm_fwd(xs, vec(g_mix[l]), mods[l], n_lat, 0, 1, f"norm_mix{l}",
                                    carry=gather_of([("in", l)]))
            arrive([("in", l)], got)
        else:
            s["x0"], s["h"] = _norm_fwd(xs, vec(g_mix[l]), mods[l], n_lat, 0, 1, f"norm_mix{l}", res=res)
        items = [("br", l), ("out", l), ("up", l)]
        s["z"], got = _mm(s["h"], full["in", l], tb=True, name=f"proj_in{l}", bm=bm, bn=_pick(IN_W, [2432, 512]),
                          bk=D_MODEL, out_dtype=BF16, cols_outer=True, carry=gather_of(items))
        arrive(items, got)
        s["qkv"] = _rope_fwd(s["z"], cos, sin, f"rope{l}")
        items = [("dn", l)] + ([("in", l + 1)] if l + 1 < DEPTH else [])
        (s["y0"], s["lse"]), got = _attn_fwd(s["qkv"], sink[l], n_lat, m_ctx, f"attn{l}", carry=gather_of(items))
        arrive(items, got)
        s["bias"] = jnp.repeat(b_spatial[l].T, LANES, axis=1)
        s["y1"] = _gating_fwd(s["z"], ws_b[l], s["bias"], vec(g_v[l]), f"gating{l}")
        s["y2"] = _sconv_fwd(s["z"], w_sconv_full[l], n_lat, f"sconv{l}")
        s["p"] = [_mm(s[f"y{t}"], full["br", l], b_lead=t, name=f"branch{l}_{t}", bm=bm, bn=D_MODEL, bk=D_MODEL,
                      out_dtype=BF16) for t in range(3)]
        s["merged"] = _merge_fwd(s["z"], vec(b_gate[l]), s["p"], f"merge{l}")
        s["o"] = _mm(s["merged"], full["out", l], name=f"proj_out{l}", bm=bm, bn=D_MODEL, bk=D_MODEL, out_dtype=F32)
        s["x1"], s["h2"] = _norm_fwd(s["x0"], vec(g_ffn[l]), mods[l], n_lat, 3, 4, f"norm_ffn{l}",
                                     res=(s["o"], mods[l], 2))
        s["up"] = _mm(s["h2"], full["up", l], tb=True, name=f"ffn_up{l}", bm=bm, bn=_pick(2 * D_FF, [1408]),
                      bk=D_MODEL, out_dtype=BF16, cols_outer=True)
        s["f"] = _ffn_conv_fwd(s["up"], w_fconv_full[l], n_lat, f"ffn_conv{l}")
        s["dd"] = _mm(s["f"], full["dn", l], name=f"ffn_down{l}", bm=bm, bn=D_MODEL, bk=D_FF, out_dtype=F32)
        saved.append(s)
        xs, res = s["x1"], (s["dd"], mods[l], 5)

    top = saved[DEPTH - 1]
    dxs, d_dd, acc_final, accs_top = _loss_bwd(top["x1"], top["dd"], mods[DEPTH - 1], vec(g_final), loss_target[0],
                                               n_lat, "loss")
    loss_part = acc_final[1, 0]
    dg_final = acc_final[0]
    dmods = [None] * DEPTH
    gate2 = accs_top[:, 2]
    grads = {k: [None] * DEPTH for k in ("g_mix", "g_ffn", "g_v", "b_gate", "sink", "w_spatial", "b_spatial",
                                         "w_sconv", "w_fconv", "w_in", "w_branch", "w_out", "w_up", "w_down")}
    bk_r = _pick(r, [1408, 768, 256])
    small_names = ["g_ffn", "g_v", "b_gate", "sink", "w_spatial", "b_spatial", "w_sconv", "w_fconv"]
    recv = {}
    small_recv, small_shapes_of = {}, {}

    def small_pack(l):
        arrs = [grads[k][l] for k in small_names]
        if l > 0:
            arrs.append(grads["g_mix"][l])
        if l == DEPTH - 1:
            arrs += [loss_part.reshape(1), dg_final]
        small_shapes_of[l] = [a.shape for a in arrs]
        return _pack(arrs)

    for l in reversed(range(DEPTH)):
        s = saved[l]
        df = _mm(d_dd, full["dn", l], tb=True, name=f"d_ffn_down{l}", bm=bm, bn=_pick(D_FF, [1408]), bk=D_MODEL,
                 out_dtype=BF16)
        grads["w_down"][l] = _mm(s["f"], d_dd, ta=True, name=f"g_ffn_down{l}", bm=_pick(D_FF, [1408]), bn=D_MODEL,
                                 bk=bk_r, out_dtype=BF16)
        arrs, modes = [_to_slots(grads["w_down"][l], 0)], ["a2a"]
        if l + 1 < DEPTH:
            arrs.append(small_pack(l + 1))
            modes.append("gather")
        (dup, dwf), got = _ffn_conv_bwd(s["up"], w_fconv_full[l], df, n_lat, f"d_ffn_conv{l}", carry=(arrs, modes))
        recv["dn", l] = got[0]
        if l + 1 < DEPTH:
            small_recv[l + 1] = got[1]
        grads["w_fconv"][l] = dwf[:3]
        dh2 = _mm_pieces([(dup, "step", 0, 2, 0, D_FF)], full["up", l], w_t=True, name=f"d_ffn_up{l}", bm=bm,
                         out_dtype=BF16)
        kb = 1408
        nbh = D_FF // kb
        grads["w_up"][l] = _mm(
            dup, s["h2"], ta=True, name=f"g_ffn_up{l}", bm=kb, bn=D_MODEL, bk=bk_r, out_dtype=BF16,
            a_spec=((None, bk_r, kb), lambda i, j, k: (i // nbh, k, i % nbh), 2 * D_FF))
        dx1, d_o, acc, accs = _norm_bwd(s["x1"], vec(g_ffn[l]), mods[l], dh2, dxs, n_lat, 3, 4, f"d_norm_ffn{l}",
                                        res=(s["o"], mods[l], 2))
        grads["g_ffn"][l] = acc[0]
        shift2, scale2, gate1 = accs[:, 0], accs[:, 1], accs[:, 2]
        dmerged = _mm(d_o, full["out", l], tb=True, name=f"d_proj_out{l}", bm=bm, bn=D_MODEL, bk=D_MODEL,
                      out_dtype=BF16)
        grads["w_out"][l] = _mm(s["merged"], d_o, ta=True, name=f"g_proj_out{l}", bm=D_MODEL, bn=D_MODEL, bk=bk_r,
                                out_dtype=BF16)
        dp0, dp1, dp2, dz_g, dbg = _merge_bwd(s["z"], vec(b_gate[l]), s["p"], dmerged, f"d_merge{l}")
        grads["b_gate"][l] = dbg[0]
        dps = (dp0, dp1, dp2)
        dys = [_mm(dps[t], full["br", l], tb=True, b_lead=t, name=f"d_branch{l}_{t}", bm=bm, bn=D_MODEL, bk=D_MODEL,
                   out_dtype=BF16) for t in range(3)]
        grads["w_branch"][l] = jnp.stack(
            [_mm(s[f"y{t}"], dps[t], ta=True, name=f"g_branch{l}_{t}", bm=D_MODEL, bn=D_MODEL, bk=bk_r,
                 out_dtype=BF16) for t in range(3)])
        arrs = [_to_slots(grads["w_up"][l], 0)] + ([_to_slots(grads["w_in"][l + 1], 0)] if l + 1 < DEPTH else [])
        (dq, dk, dv, dsk), got = _attn_bwd(s["qkv"], sink[l], s["y0"], dys[0], s["lse"], n_lat, m_ctx, f"d_attn{l}",
                                           carry=(arrs, ["a2a"] * len(arrs)))
        recv["up", l] = got[0]
        if l + 1 < DEPTH:
            recv["in", l + 1] = got[1]
        grads["sink"][l] = dsk[:, 0, :8].reshape(N_HEADS)
        dz_qkv = _rope_bwd(dq, dk, dv, cos, sin, f"d_rope{l}")
        (dz_a, dws, dbs, dgv), (recv["out", l], recv["br", l]) = _gating_bwd(
            s["z"], ws_b[l], ws_t[l], s["bias"], vec(g_v[l]), dys[1], f"d_gating{l}",
            carry=([_to_slots(grads["w_out"][l], 0), _to_slots(grads["w_branch"][l], 1)], ["a2a"] * 2))
        grads["w_spatial"][l], grads["b_spatial"][l], grads["g_v"][l] = dws, dbs.T, dgv[0]
        dz_b, dwsc = _sconv_bwd(s["z"], w_sconv_full[l], dys[2], n_lat, f"d_sconv{l}")
        grads["w_sconv"][l] = dwsc[:3]
        kvw = 2 * N_KV_HEADS * HEAD_DIM
        pieces = [(dz_g, None, 0, 3, 0, D_MODEL), (dz_qkv, None, 0, 1, P_Q // D_MODEL, D_MODEL),
                  (dz_a, None, 0, 2, P_A // D_MODEL, D_MODEL), (dz_b, "step", 0, 3, P_B // D_MODEL, D_MODEL),
                  (dz_qkv, None, D_MODEL // kvw, 1, P_KV // kvw, kvw)]
        gw = lambda a, nm, rows, **kw: _mm(a, s["h"], ta=True, name=f"g_proj_in{l}_{nm}", bm=rows, bn=D_MODEL,
                                           bk=bk_r, out_dtype=BF16, **kw)
        gw_g = gw(dz_g, "gate", 1536)
        gw_qkv = gw(dz_qkv, "qkv", 1536)
        gw_a = gw(dz_a, "gating", 1024)
        gw_b = [gw(dz_b, f"sconv{t}", 1024, a_lead=t) for t in range(3)]
        grads["w_in"][l] = jnp.concatenate([gw_qkv, gw_a] + gw_b + [gw_g], axis=0)
        bm_in = _pick(r, [1056, 768, 256])
        if l > 0:
            dh = _mm_pieces(pieces, full["in", l], w_t=True, name=f"d_proj_in{l}", bm=bm_in, out_dtype=BF16)
        else:
            dh, (recv["in", 0], small_recv[0]) = _mm_pieces(
                pieces, full["in", l], w_t=True, name=f"d_proj_in{l}", bm=bm_in, out_dtype=BF16,
                carry=([_to_slots(grads["w_in"][0], 0), small_pack(0)], ["a2a", "gather"]))
        below = None if l == 0 else (saved[l - 1]["dd"], mods[l - 1], 5)
        outs = _norm_bwd(s["x0"], vec(g_mix[l]), mods[l], dh, dx1, n_lat, 0, 1, f"d_norm_mix{l}", res=below)
        if below is None:
            dxs, acc, accs = outs
        else:
            dxs, d_dd, acc, accs = outs
        grads["g_mix"][l] = acc[0]
        dmods[l] = jnp.stack([accs[:, 0], accs[:, 1], gate1, shift2, scale2, gate2], axis=1)
        gate2 = accs[:, 2]

    dmod_own = jnp.stack([dmods[l][0].reshape(-1) for l in range(DEPTH)])
    dmod_ctx = jnp.stack([dmods[l][1].reshape(-1) for l in range(DEPTH)])
    late = [grads["g_mix"][0], dmod_own + dmod_ctx, dmod_ctx, dmod_own]
    late_shapes = [a.shape for a in late]
    (late_all,) = _exchange([_pack(late)], ["gather"], "gather_late_grads")
    g_mix0_g, b_mod_g, dmodc_tot, _ = _unpack(_sum_slots(late_all, "sum_late_grads"), late_shapes)
    dmod_all = _unpack(late_all, late_shapes, lead=(N_DEV,))[-1]
    layer_sums = [_unpack(_sum_slots(small_recv[l], f"sum_small_grads{l}"), small_shapes_of[l]) for l in range(DEPTH)]
    by_name = {k: jnp.stack([layer_sums[l][t] for l in range(DEPTH)]) for t, k in enumerate(small_names)}
    g_mix_g = jnp.stack([g_mix0_g] + [layer_sums[l][len(small_names)] for l in range(1, DEPTH)])
    loss_sum, g_final_g = layer_sums[DEPTH - 1][-2], layer_sums[DEPTH - 1][-1]
    g_ffn_g, g_v_g, b_gate_g, sink_g = by_name["g_ffn"], by_name["g_v"], by_name["b_gate"], by_name["sink"]
    w_spatial_g, b_spatial_g = by_name["w_spatial"], by_name["b_spatial"]
    w_sconv_g = lax.dynamic_slice_in_dim(by_name["w_sconv"], me * w_sconv.shape[2], w_sconv.shape[2], axis=2)
    w_fconv_g = lax.dynamic_slice_in_dim(by_name["w_fconv"], me * w_fconv.shape[2], w_fconv.shape[2], axis=2)

    dmod_cols = lax.dynamic_slice_in_dim(dmod_all, me * mod_w, mod_w, axis=2)
    dmodc_cols = lax.dynamic_slice_in_dim(dmodc_tot, me * mod_w, mod_w, axis=1)
    g_w_mod, cctx_part = [], None
    for l in range(DEPTH):
        rhs = jnp.concatenate([dmod_cols[:, l], dmodc_cols[l][None], jnp.zeros((7, mod_w), F32)], axis=0)
        g_w_mod.append(_mm(act, rhs, ta=True, name=f"g_mod{l}", bm=D_MODEL, bn=mod_w, bk=16, out_dtype=F32))
        lhs = jnp.pad(dmodc_cols[l][None], ((0, 7), (0, 0)))
        part = _mm(lhs, w_mod[l], tb=True, name=f"d_cctx{l}", bm=8, bn=D_MODEL, bk=mod_w, out_dtype=F32)
        cctx_part = part if cctx_part is None else cctx_part + part
    sg = jax.nn.sigmoid(c_ctx)
    dsilu = (sg * (1.0 + c_ctx * (1.0 - sg))).reshape(8, LANES)

    (r_cctx,) = _exchange([cctx_part[0].reshape(8, LANES)], ["gather"], "gather_c_ctx_grad")

    def per_layer(a):
        return a.reshape(a.shape[0], -1, a.shape[-1])

    upd = {}
    for nm, kind, wv, mv, vv in (("w_in", "in", w_in, m_w_in, v_w_in), ("w_branch", "br", w_branch, m_w_branch, v_w_branch),
                                 ("w_out", "out", w_out, m_w_out, v_w_out), ("w_up", "up", w_up, m_w_up, v_w_up),
                                 ("w_down", "dn", w_down, m_w_down, v_w_down), ("w_mod", None, w_mod, m_w_mod, v_w_mod)):
        if kind is None:
            gsrcs = [g[None] for g in g_w_mod]
        else:
            gsrcs = [per_layer(recv[kind, l]) for l in range(DEPTH)]
        if kind in ("in", "up"):
            outs = _adamw(tr(wv), gsrcs, tr(mv), tr(vv), f"adamw_{nm}")
            upd[nm] = [tr(o) for o in outs]
        else:
            outs = _adamw(per_layer(wv), gsrcs, per_layer(mv), per_layer(vv), f"adamw_{nm}")
            upd[nm] = [o.reshape(wv.shape) for o in outs]
    as_tile = lambda a: a.reshape(1, 8, LANES)
    upd["c_ctx"] = [o.reshape(D_MODEL) for o in _adamw(
        as_tile(c_ctx), [r_cctx], as_tile(m_c_ctx), as_tile(v_c_ctx), "adamw_c_ctx", scale=as_tile(dsilu))]

    names = ["b_mod", "g_mix", "b_gate", "sink", "w_spatial", "b_spatial", "g_v", "w_sconv", "g_ffn", "w_fconv",
             "g_final"]
    w_s = [b_mod, g_mix, b_gate, sink, w_spatial, b_spatial, g_v, w_sconv, g_ffn, w_fconv, g_final]
    g_s = [b_mod_g, g_mix_g, b_gate_g, sink_g, w_spatial_g, b_spatial_g, g_v_g, w_sconv_g, g_ffn_g, w_fconv_g,
           g_final_g]
    m_s = [m_b_mod, m_g_mix, m_b_gate, m_sink, m_w_spatial, m_b_spatial, m_g_v, m_w_sconv, m_g_ffn, m_w_fconv,
           m_g_final]
    v_s = [v_b_mod, v_g_mix, v_b_gate, v_sink, v_w_spatial, v_b_spatial, v_g_v, v_w_sconv, v_g_ffn, v_w_fconv,
           v_g_final]
    shapes = [a.shape for a in w_s]
    packed = _adamw(_pack(w_s)[None], [_pack(g_s)[None]], _pack(m_s)[None], _pack(v_s)[None], "adamw_small")
    unpacked = [_unpack(o[0], shapes) for o in packed]
    for t, nm in enumerate(names):
        upd[nm] = [unpacked[q][t] for q in range(4)]

    order = ["c_ctx", "w_mod", "b_mod", "g_mix", "w_in", "b_gate", "sink", "w_spatial", "b_spatial", "g_v", "w_sconv",
             "w_branch", "w_out", "g_ffn", "w_up", "w_fconv", "w_down", "g_final"]
    result = [loss_sum.reshape(()), dxs[None]]
    for q in range(4):
        result += [upd[nm][q] for nm in order]
    return tuple(result)
```

```python
import jax
import jax.numpy as jnp
from jax import lax
from jax.experimental import pallas as pl
from jax.experimental.pallas import tpu as pltpu

F32, BF16 = jnp.float32, jnp.bfloat16

D_MODEL = 1024
DEPTH = 2
GRID_W = 64
N_HEADS = 16
N_KV_HEADS = 4
HEAD_DIM = 64
WINDOW = 128
BLK = 128
ROPE_THETA = 10000.0
CHUNK = 128
A_GROUPS = 8
D_FF = 2816
EPS = 1e-6
NEG = -1e30
IN_W = 9728
OFF_K, OFF_A, OFF_B, OFF_G = 1024, 1536, 3584, 6656
P_Q, P_A, P_B, P_KV = 3072, 4096, 6144, 9216

N_DEV = 8
LANES = 128
ROW_TILE = 256
CONV_CHUNK = 256
PAD = 8
VMEM_LIMIT = 52 * 1024 * 1024

ADAM_LR, ADAM_B1, ADAM_B2, ADAM_EPS, ADAM_WD, ADAM_STEP = 0.001, 0.9, 0.999, 1e-08, 0.01, 10

HBM_SPEC = pl.BlockSpec(memory_space=pltpu.HBM)
SMEM_SPEC = pl.BlockSpec(memory_space=pltpu.SMEM)


def _params():
    return pltpu.CompilerParams(vmem_limit_bytes=VMEM_LIMIT)


def _pick(n, prefs):
    for p in prefs:
        if n % p == 0:
            return p
    raise ValueError((n, prefs))


def _sigmoid(x):
    return 0.5 * jnp.tanh(0.5 * x) + 0.5


def _mm(a, b, *, name, ta=False, tb=False, bm, bn, bk, out_dtype, a_lead=None, b_lead=None, a_spec=None,
        b_spec=None, cols_outer=False, carry=None):
    ash = a.shape[1:] if a_lead is not None else a.shape
    bsh = b.shape[1:] if b_lead is not None else b.shape
    kc = (bsh[1] if tb else bsh[0]) if b_spec is None else (ash[0] if ta else ash[1])
    mo =(ash[1] if ta else ash[0]) if a_spec is None else a_spec[2]
    no = (bsh[0] if tb else bsh[1]) if b_spec is None else b_spec[2]
    assert mo % bm == 0 and no % bn == 0 and kc % bk == 0, (name, mo, no, kc, bm, bn, bk)
    nk = kc // bk

    def spec(shape, fn, idx=None):
        if idx is not None:
            shape, inner = (None,) + shape, fn
            fn = lambda i, j, k: (idx,) + inner(i, j, k)
        if cols_outer:
            return pl.BlockSpec(shape, lambda j, i, k: fn(i, j, k))
        return pl.BlockSpec(shape, fn)

    if a_spec is not None:
        a_bs = spec(a_spec[0], a_spec[1])
    elif ta:
        a_bs = spec((bk, bm), lambda i, j, k: (k, i), a_lead)
    else:
        a_bs = spec((bm, bk), lambda i, j, k: (i, k), a_lead)
    if b_spec is not None:
        b_bs = spec(b_spec[0], b_spec[1])
    elif tb:
        b_bs = spec((bn, bk), lambda i, j, k: (j, k), b_lead)
    else:
        b_bs = spec((bk, bn), lambda i, j, k: (k, j), b_lead)
    dims = (((0 if ta else 1,), (1 if tb else 0,)), ((), ()))
    grid = (no // bn, mo // bm, nk) if cols_outer else (mo // bm, no // bn, nk)

    def body(a_ref, b_ref, o_ref, *scratch):
        if nk == 1:
            o_ref[...] = lax.dot_general(a_ref[...], b_ref[...], dims, preferred_element_type=F32).astype(o_ref.dtype)
        else:
            acc = scratch[0]
            k = pl.program_id(2)

            @pl.when(k == 0)
            def _():
                acc[...] = jnp.zeros_like(acc)

            acc[...] += lax.dot_general(a_ref[...], b_ref[...], dims, preferred_element_type=F32)

            @pl.when(k == nk - 1)
            def _():
                o_ref[...] = acc[...].astype(o_ref.dtype)

    outs, carried = _pcall(
        body, (a, b), name=name, grid=grid, in_specs=[a_bs, b_bs],
        out_specs=[spec((bm, bn), lambda i, j, k: (i, j))],
        out_shape=[jax.ShapeDtypeStruct((mo, no), out_dtype)],
        scratch_shapes=[pltpu.VMEM((bm, bn), F32)] if nk > 1 else [], carry=carry)
    return outs[0] if carry is None else (outs[0], carried)


def _mm_pieces(pieces, w, *, name, bm, w_t=False, out_dtype=F32, carry=None):
    kout = w.shape[1] if w_t else w.shape[0]
    starts, total = [], 0
    for piece in pieces:
        starts.append(total)
        total += piece[3]
    mo = pieces[0][0].shape[-2]
    assert mo % bm == 0
    widths = sorted({piece[5] for piece in pieces}, reverse=True)

    def inside(p, k):
        return (k >= starts[p]) & (k < starts[p] + pieces[p][3])

    def a_spec(p):
        _, lead, col0, nblk, _, bk = pieces[p]

        def fn(i, k):
            t = jnp.clip(k - starts[p], 0, nblk - 1)
            if lead is None:
                return (i, col0 + t)
            return (t, i, col0) if lead == "step" else (lead, i, col0 + t)
        return pl.BlockSpec((bm, bk) if lead is None else (None, bm, bk), fn)

    def w_spec(bk):
        def fn(i, k):
            col = 0
            for p, piece in enumerate(pieces):
                if piece[5] == bk:
                    col = col + jnp.where(inside(p, k), piece[4] + k - starts[p], 0)
            return (col, 0) if w_t else (0, col)
        return pl.BlockSpec((bk, kout) if w_t else (kout, bk), fn)

    n_p, n_w = len(pieces), len(widths)

    def body(*refs):
        a_refs, w_refs, o_ref, acc = refs[:n_p], refs[n_p:n_p + n_w], refs[n_p + n_w], refs[n_p + n_w + 1]
        k = pl.program_id(1)

        @pl.when(k == 0)
        def _():
            acc[...] = jnp.zeros_like(acc)

        for p in range(n_p):
            w_ref = w_refs[widths.index(pieces[p][5])]

            @pl.when(inside(p, k))
            def _(p=p, w_ref=w_ref):
                acc[...] += lax.dot_general(a_refs[p][...], w_ref[...], NN if w_t else NT,
                                            preferred_element_type=F32)

        @pl.when(k == total - 1)
        def _():
            o_ref[...] = acc[...].astype(o_ref.dtype)

    outs, carried = _pcall(
        body, [piece[0] for piece in pieces] + [w] * n_w, name=name, grid=(mo // bm, total),
        in_specs=[a_spec(p) for p in range(n_p)] + [w_spec(bk) for bk in widths],
        out_specs=[pl.BlockSpec((bm, kout), lambda i, k: (i, 0))],
        out_shape=[jax.ShapeDtypeStruct((mo, kout), out_dtype)],
        scratch_shapes=[pltpu.VMEM((bm, kout), F32)], carry=carry)
    return outs[0] if carry is None else (outs[0], carried)


def _xchg_out_shapes(arrs, modes):
    return [jax.ShapeDtypeStruct((N_DEV,) + a.shape if m == "gather" else a.shape, a.dtype)
            for a, m in zip(arrs, modes)]


def _xchg_sems(n):
    return [pltpu.SemaphoreType.DMA((n, N_DEV - 1)), pltpu.SemaphoreType.DMA((n, N_DEV - 1)),
            pltpu.SemaphoreType.DMA((n,))]


def _xchg_copies(ins, outs, modes, sems):
    send_sems, recv_sems, local_sems = sems
    x, y, c = lax.axis_index("x"), lax.axis_index("y"), lax.axis_index("c")
    me = 4 * x + 2 * y + c

    def place(q):
        px = 1 - x if (q >> 2) & 1 else x
        py = 1 - y if (q >> 1) & 1 else y
        pc = 1 - c if q & 1 else c
        return (px, py, pc), 4 * px + 2 * py + pc

    sibling, _ = place(1)
    out = dict(local=[], direct=[], landed=[], passed=[], others=[])
    for t, mode in enumerate(modes):
        gather = mode == "gather"
        mine = ins[t] if gather else ins[t].at[me]
        out["local"].append(pltpu.make_async_copy(mine, outs[t].at[me], local_sems.at[t]))
        for q in range(1, N_DEV):
            peer, slot = place(q)
            sem = dict(send_sem=send_sems.at[t, q - 1], recv_sem=recv_sems.at[t, q - 1],
                       device_id_type=pl.DeviceIdType.MESH)
            arrival = pltpu.make_async_remote_copy(src_ref=outs[t].at[slot], dst_ref=outs[t].at[slot],
                                                   device_id=peer, **sem)
            if not gather:
                out["direct"].append(pltpu.make_async_remote_copy(src_ref=ins[t].at[slot], dst_ref=outs[t].at[me],
                                                                  device_id=peer, **sem))
                out["others"].append(arrival)
            elif q == 1 or q % 2 == 0:
                out["direct"].append(pltpu.make_async_remote_copy(src_ref=ins[t], dst_ref=outs[t].at[me],
                                                                  device_id=peer, **sem))
                out["others" if q == 1 else "landed"].append(arrival)
            else:
                _, origin = place(q - 1)
                out["passed"].append(pltpu.make_async_remote_copy(src_ref=outs[t].at[origin], dst_ref=outs[t].at[origin],
                                                                  device_id=sibling, **sem))
                out["others"].append(arrival)
    return out


def _xchg_start(copies):
    for cp in copies["local"] + copies["direct"]:
        cp.start()


def _xchg_wait(copies):
    for cp in copies["landed"]:
        cp.wait_recv()
    for cp in copies["passed"]:
        cp.start()
    for cp in copies["others"]:
        cp.wait_recv()
    for cp in copies["direct"] + copies["passed"]:
        cp.wait_send()
    for cp in copies["local"]:
        cp.wait()


def _exchange(arrs, modes, name):
    n = len(arrs)

    def body(*refs):
        copies = _xchg_copies(refs[:n], refs[n:2 * n], modes, refs[2 * n:])
        _xchg_start(copies)
        _xchg_wait(copies)

    outs = pl.pallas_call(
        body, name=name, in_specs=[HBM_SPEC] * n, out_specs=[HBM_SPEC] * n, out_shape=_xchg_out_shapes(arrs, modes),
        scratch_shapes=_xchg_sems(n), compiler_params=pltpu.CompilerParams(has_side_effects=True),
    )(*arrs)
    return list(outs)


def _pcall(body, operands, *, name, grid, in_specs, out_specs, out_shape, scratch_shapes=(), carry=None):
    out_specs, out_shape, scratch_shapes = list(out_specs), list(out_shape), list(scratch_shapes)
    if carry is None:
        outs = pl.pallas_call(body, name=name, grid=grid, in_specs=in_specs, out_specs=out_specs,
                              out_shape=out_shape, scratch_shapes=scratch_shapes, compiler_params=_params())(*operands)
        return list(outs), []
    arrs, modes = carry
    n, n_in, n_out, n_scr = len(arrs), len(in_specs), len(out_specs), len(scratch_shapes)

    def wrapped(*refs):
        ins, c_in = refs[:n_in], refs[n_in:n_in + n]
        outs, c_out = refs[n_in + n:n_in + n + n_out], refs[n_in + n + n_out:n_in + 2 * n + n_out]
        rest = refs[n_in + 2 * n + n_out:]
        scr, sems = rest[:n_scr], rest[n_scr:]
        first, last = None, None
        for d, size in enumerate(grid):
            f, e = pl.program_id(d) == 0, pl.program_id(d) == size - 1
            first = f if first is None else first & f
            last = e if last is None else last & e

        @pl.when(first)
        def _():
            _xchg_start(_xchg_copies(c_in, c_out, modes, sems))

        body(*ins, *outs, *scr)

        @pl.when(last)
        def _():
            _xchg_wait(_xchg_copies(c_in, c_out, modes, sems))

    outs = pl.pallas_call(
        wrapped, name=name, grid=grid, in_specs=list(in_specs) + [HBM_SPEC] * n,
        out_specs=out_specs + [HBM_SPEC] * n, out_shape=out_shape + _xchg_out_shapes(arrs, modes),
        scratch_shapes=scratch_shapes + _xchg_sems(n), compiler_params=_params())(*operands, *arrs)
    return list(outs[:n_out]), list(outs[n_out:])


def _row_specs(r, n_lat, tm):
    nbl = n_lat // tm
    row = pl.BlockSpec((tm, D_MODEL), lambda i: (i, 0))
    mod = pl.BlockSpec((None, 8, D_MODEL), lambda i: (i // nbl, 0, 0))
    vec = pl.BlockSpec((1, D_MODEL), lambda i: (0, 0))
    return nbl, row, mod, vec


def _rows(vals):
    width = [v for v in vals if v is not None][0].shape[1]
    return jnp.concatenate([jnp.zeros((1, width), F32) if v is None else v for v in vals], axis=0)


def _norm_fwd(xs, g, mods, n_lat, sh, sc, name, res=None, carry=None):
    r = xs.shape[0]
    tm = ROW_TILE
    nbl, row, mod, vec = _row_specs(r, n_lat, tm)

    def norm(x, g_ref, m_ref, h_ref):
        rs = lax.rsqrt(jnp.mean(x * x, axis=-1, keepdims=True) + EPS)
        m = m_ref[...]
        h_ref[...] = ((x * rs * g_ref[...]) * (1.0 + m[sc:sc + 1]) + m[sh:sh + 1]).astype(BF16)

    if res is None:
        def body(x_ref, g_ref, m_ref, h_ref):
            norm(x_ref[...], g_ref, m_ref, h_ref)
        outs, carried = _pcall(body, (xs, g, mods), name=name, grid=(r // tm,), in_specs=[row, vec, mod],
                               out_specs=[row], out_shape=[jax.ShapeDtypeStruct((r, D_MODEL), BF16)], carry=carry)
        return outs[0] if carry is None else (outs[0], carried)

    o, mods_res, gt = res

    def body(x_ref, o_ref, mr_ref, g_ref, m_ref, x1_ref, h_ref):
        x = x_ref[...] + mr_ref[...][gt:gt + 1] * o_ref[...]
        x1_ref[...] = x
        norm(x, g_ref, m_ref, h_ref)

    return pl.pallas_call(
        body, name=name, grid=(r // tm,), in_specs=[row, row, mod, vec, mod], out_specs=[row, row],
        out_shape=[jax.ShapeDtypeStruct((r, D_MODEL), F32), jax.ShapeDtypeStruct((r, D_MODEL), BF16)],
        compiler_params=_params())(xs, o, mods_res, g, mods)


def _rms_bwd(x, g, dy):
    rs = lax.rsqrt(jnp.mean(x * x, axis=-1, keepdims=True) + EPS)
    xh = x * rs
    dxh = dy * g
    dx = rs * (dxh - xh * jnp.mean(dxh * xh, axis=-1, keepdims=True))
    return dx, dy * xh, xh


def _acc_specs(nbl):
    acc_all = pl.BlockSpec((8, D_MODEL), lambda i: (0, 0))
    acc_stream = pl.BlockSpec((None, 8, D_MODEL), lambda i: (i // nbl, 0, 0))
    return acc_all, acc_stream


def _loss_bwd(xs, o, mods, g_final, target, n_lat, name):
    r = xs.shape[0]
    tm = ROW_TILE
    nbl, row, mod, vec = _row_specs(r, n_lat, tm)
    acc_all, acc_stream = _acc_specs(nbl)
    tgt = pl.BlockSpec((tm, D_MODEL), lambda i: (jnp.minimum(i, nbl - 1), 0))

    def body(x_ref, o_ref, m_ref, g_ref, t_ref, dx_ref, do_ref, acc_ref, accs_ref):
        i = pl.program_id(0)
        lat = i < nbl
        gate = m_ref[...][5:6]
        o_val = o_ref[...]
        x = x_ref[...] + gate * o_val
        g = g_ref[...]
        rs = lax.rsqrt(jnp.mean(x * x, axis=-1, keepdims=True) + EPS)
        y = x * rs * g
        err = jnp.where(lat, y - t_ref[...], 0.0)
        loss = 0.5 * jnp.sum(jnp.mean(err * err, axis=-1, keepdims=True), axis=0, keepdims=True)
        dy = err * (1.0 / D_MODEL)
        dx, dg_rows, _ = _rms_bwd(x, g, dy)
        dx_ref[...] = dx
        do_ref[...] = (gate * dx).astype(BF16)

        @pl.when(i == 0)
        def _():
            acc_ref[...] = jnp.zeros_like(acc_ref)

        @pl.when((i == 0) | (i == nbl))
        def _():
            accs_ref[...] = jnp.zeros_like(accs_ref)

        acc_ref[...] += _rows([jnp.sum(dg_rows, axis=0, keepdims=True), jnp.broadcast_to(loss, (1, D_MODEL))]
                              + [None] * 6)
        accs_ref[...] += _rows([None, None, jnp.sum(dx * o_val, axis=0, keepdims=True)] + [None] * 5)

    return pl.pallas_call(
        body, name=name, grid=(r // tm,), in_specs=[row, row, mod, vec, tgt],
        out_specs=[row, row, acc_all, acc_stream],
        out_shape=[jax.ShapeDtypeStruct((r, D_MODEL), F32), jax.ShapeDtypeStruct((r, D_MODEL), BF16),
                   jax.ShapeDtypeStruct((8, D_MODEL), F32), jax.ShapeDtypeStruct((2, 8, D_MODEL), F32)],
        compiler_params=_params())(xs, o, mods, g_final, target)


def _norm_bwd(xs, g, mods, dh, dx_in, n_lat, sh, sc, name, res=None):
    r = xs.shape[0]
    tm = ROW_TILE
    nbl, row, mod, vec = _row_specs(r, n_lat, tm)
    acc_all, acc_stream = _acc_specs(nbl)
    has_res = res is not None

    def body(*refs):
        if has_res:
            x_ref, g_ref, m_ref, dh_ref, dxi_ref, o_ref, mr_ref, dx_ref, do_ref, acc_ref, accs_ref = refs
        else:
            x_ref, g_ref, m_ref, dh_ref, dxi_ref, dx_ref, acc_ref, accs_ref = refs
        i = pl.program_id(0)
        x, g, m, dhv = x_ref[...], g_ref[...], m_ref[...], dh_ref[...].astype(F32)
        dy = dhv * (1.0 + m[sc:sc + 1])
        dxn, dg_rows, xh = _rms_bwd(x, g, dy)
        dx = dxi_ref[...] + dxn
        if has_res:
            dx_ref[...] = dx
        else:
            @pl.when(i < nbl)
            def _():
                dx_ref[...] = dx
        d_gate = None
        if has_res:
            o_val = o_ref[...]
            do_ref[...] = (mr_ref[...][res[2]:res[2] + 1] * dx).astype(BF16)
            d_gate = jnp.sum(dx * o_val, axis=0, keepdims=True)

        @pl.when(i == 0)
        def _():
            acc_ref[...] = jnp.zeros_like(acc_ref)

        @pl.when((i == 0) | (i == nbl))
        def _():
            accs_ref[...] = jnp.zeros_like(accs_ref)

        acc_ref[...] += _rows([jnp.sum(dg_rows, axis=0, keepdims=True)] + [None] * 7)
        accs_ref[...] += _rows([jnp.sum(dhv, axis=0, keepdims=True),
                                jnp.sum(dhv * (xh * g), axis=0, keepdims=True), d_gate] + [None] * 5)

    ins = [xs, g, mods, dh, dx_in]
    in_specs = [row, vec, mod, row, row]
    if has_res:
        out_specs, out_shape = [row], [jax.ShapeDtypeStruct((r, D_MODEL), F32)]
    else:
        out_specs = [pl.BlockSpec((tm, D_MODEL), lambda i: (jnp.minimum(i, nbl - 1), 0))]
        out_shape = [jax.ShapeDtypeStruct((n_lat, D_MODEL), F32)]
    if has_res:
        ins += [res[0], res[1]]
        in_specs += [row, mod]
        out_specs.append(row)
        out_shape.append(jax.ShapeDtypeStruct((r, D_MODEL), BF16))
    out_specs += [acc_all, acc_stream]
    out_shape += [jax.ShapeDtypeStruct((8, D_MODEL), F32), jax.ShapeDtypeStruct((2, 8, D_MODEL), F32)]
    return pl.pallas_call(body, name=name, grid=(r // tm,), in_specs=in_specs, out_specs=out_specs,
                          out_shape=out_shape, compiler_params=_params())(*ins)


def _rotate(t, cos, sin):
    width = t.shape[1]
    reps = width // LANES
    lane = lax.broadcasted_iota(jnp.int32, (1, width), 1)
    first = (lane % HEAD_DIM) < (HEAD_DIM // 2)
    swapped = jnp.where(first, pltpu.roll(t, width - HEAD_DIM // 2, 1), pltpu.roll(t, HEAD_DIM // 2, 1))
    return t * jnp.tile(cos, (1, reps)) + swapped * jnp.tile(sin, (1, reps))


def _rope_fwd(z, cos, sin, name):
    r = z.shape[0]
    tm = ROW_TILE
    kvw = 2 * N_KV_HEADS * HEAD_DIM
    tab = pl.BlockSpec((tm, LANES), lambda i: (i, 0))

    def body(q_ref, kv_ref, c_ref, s_ref, o_ref):
        c, s = c_ref[...], s_ref[...]
        kv = kv_ref[...]
        o_ref[:, :D_MODEL] = (_rotate(q_ref[...].astype(F32), c, s) * (HEAD_DIM ** -0.5)).astype(BF16)
        o_ref[:, D_MODEL:D_MODEL + kvw // 2] = _rotate(kv[:, :kvw // 2].astype(F32), c, s).astype(BF16)
        o_ref[:, D_MODEL + kvw // 2:] = kv[:, kvw // 2:]

    return pl.pallas_call(
        body, name=name, grid=(r // tm,),
        in_specs=[pl.BlockSpec((tm, D_MODEL), lambda i: (i, P_Q // D_MODEL)),
                  pl.BlockSpec((tm, kvw), lambda i: (i, P_KV // kvw)), tab, tab],
        out_specs=pl.BlockSpec((tm, D_MODEL + kvw), lambda i: (i, 0)),
        out_shape=jax.ShapeDtypeStruct((r, D_MODEL + kvw), BF16), compiler_params=_params())(z, z, cos, sin)


def _rope_bwd(dq, dk, dv, cos, sin, name):
    r = dq.shape[0]
    tm = ROW_TILE
    kw = N_KV_HEADS * HEAD_DIM
    tab = pl.BlockSpec((tm, LANES), lambda i: (i, 0))

    def body(dq_ref, dk_ref, dv_ref, c_ref, s_ref, o_ref):
        c, s = c_ref[...], -s_ref[...]
        o_ref[:, :D_MODEL] = (_rotate(dq_ref[...], c, s) * (HEAD_DIM ** -0.5)).astype(BF16)
        o_ref[:, D_MODEL:D_MODEL + kw] = _rotate(dk_ref[...], c, s).astype(BF16)
        o_ref[:, D_MODEL + kw:] = dv_ref[...].astype(BF16)

    return pl.pallas_call(
        body, name=name, grid=(r // tm,),
        in_specs=[pl.BlockSpec((tm, D_MODEL), lambda i: (i, 0)), pl.BlockSpec((tm, kw), lambda i: (i, 0)),
                  pl.BlockSpec((tm, kw), lambda i: (i, 0)), tab, tab],
        out_specs=pl.BlockSpec((tm, D_MODEL + 2 * kw), lambda i: (i, 0)),
        out_shape=jax.ShapeDtypeStruct((r, D_MODEL + 2 * kw), BF16), compiler_params=_params())(dq, dk, dv, cos, sin)


def _attn_setup(i, n_lat, m_ctx, nbl, k_ref, v_ref):
    start = pl.multiple_of(jnp.clip((i - 1) * BLK, 0, n_lat - 3 * BLK), BLK)
    nkeys = 3 * BLK + m_ctx
    rows = lax.broadcasted_iota(jnp.int32, (4 * BLK, nkeys), 0)
    cols = lax.broadcasted_iota(jnp.int32, (4 * BLK, nkeys), 1)
    qpos = i * BLK + (rows & (BLK - 1))
    seen = (cols >= 3 * BLK) | ((jnp.abs(start + cols - qpos) <= WINDOW) & (i < nbl))
    mask = jnp.where(seen, 0.0, NEG)
    kblk = jnp.concatenate([k_ref[pl.ds(start, 3 * BLK), :], k_ref[pl.ds(n_lat, m_ctx), :]], axis=0)
    vblk = jnp.concatenate([v_ref[pl.ds(start, 3 * BLK), :], v_ref[pl.ds(n_lat, m_ctx), :]], axis=0)
    lo = lax.broadcasted_iota(jnp.int32, (1, LANES), 1) < HEAD_DIM
    return start, mask, kblk, vblk, lo


def _stack_heads(ref, kh, lo):
    a = ref[:, (2 * kh) * LANES:(2 * kh + 1) * LANES]
    b = ref[:, (2 * kh + 1) * LANES:(2 * kh + 2) * LANES]
    z = jnp.zeros_like(a)
    return jnp.concatenate([jnp.where(lo, a, z), jnp.where(lo, z, a), jnp.where(lo, b, z), jnp.where(lo, z, b)],
                           axis=0)


def _kv_variants(blk, rolled, kh, lo):
    z = jnp.zeros_like(blk)
    if kh == 0:
        return jnp.where(lo, blk, rolled), jnp.where(lo, blk, z), jnp.where(lo, z, rolled)
    return jnp.where(lo, rolled, blk), jnp.where(lo, rolled, z), jnp.where(lo, z, blk)


NN = (((1,), (0,)), ((), ()))
NT = (((1,), (1,)), ((), ()))
TN = (((0,), (0,)), ((), ()))


def _scores(qst, k2, mask, sink_ref, p, kh):
    s = lax.dot_general(qst, k2, NT, preferred_element_type=F32) + mask
    snk = jnp.concatenate([jnp.full((BLK, 1), sink_ref[p * 8 + kh * 4 + g], F32) for g in range(4)], axis=0)
    return s, snk


def _attn_fwd(qkv, sink, n_lat, m_ctx, name, carry=None):
    r = qkv.shape[0]
    nb, nbl = r // BLK, n_lat // BLK
    kcol = D_MODEL // LANES

    def body(sink_ref, q_ref, k_ref, v_ref, o_ref, lse_ref):
        p, i = pl.program_id(0), pl.program_id(1)
        _, mask, kblk, vblk, lo = _attn_setup(i, n_lat, m_ctx, nbl, k_ref, v_ref)
        kr, vr = pltpu.roll(kblk, HEAD_DIM, 1), pltpu.roll(vblk, HEAD_DIM, 1)
        for kh in range(2):
            k2, _, _ = _kv_variants(kblk, kr, kh, lo)
            _, vlo, vhi = _kv_variants(vblk, vr, kh, lo)
            qst = _stack_heads(q_ref, kh, lo)
            s, snk = _scores(qst, k2, mask, sink_ref, p, kh)
            mx = jnp.maximum(jnp.max(s, axis=-1, keepdims=True), snk)
            pe = jnp.exp(s - mx)
            den = jnp.sum(pe, axis=-1, keepdims=True) + jnp.exp(snk - mx)
            inv = 1.0 / den
            pb = pe.astype(BF16)
            for jp in range(2):
                r0 = 2 * jp * BLK
                pair = (jnp.dot(pb[r0:r0 + BLK], vlo, preferred_element_type=F32)
                        + jnp.dot(pb[r0 + BLK:r0 + 2 * BLK], vhi, preferred_element_type=F32))
                pair = pair * jnp.where(lo, inv[r0:r0 + BLK], inv[r0 + BLK:r0 + 2 * BLK])
                o_ref[:, (2 * kh + jp) * LANES:(2 * kh + jp + 1) * LANES] = pair.astype(BF16)
            lse = mx + jnp.log(den)
            for g in range(4):
                lse_ref[:, kh * 4 + g:kh * 4 + g + 1] = lse[g * BLK:(g + 1) * BLK]

    outs, carried = _pcall(
        body, (sink, qkv, qkv, qkv), name=name, grid=(2, nb),
        in_specs=[SMEM_SPEC,
                  pl.BlockSpec((BLK, 4 * LANES), lambda p, i: (i, p)),
                  pl.BlockSpec((r, LANES), lambda p, i: (0, kcol + p)),
                  pl.BlockSpec((r, LANES), lambda p, i: (0, kcol + 2 + p))],
        out_specs=[pl.BlockSpec((BLK, 4 * LANES), lambda p, i: (i, p)),
                   pl.BlockSpec((None, BLK, 8), lambda p, i: (p, i, 0))],
        out_shape=[jax.ShapeDtypeStruct((r, D_MODEL), BF16), jax.ShapeDtypeStruct((2, r, 8), F32)], carry=carry)
    return outs if carry is None else (outs, carried)


def _attn_bwd(qkv, sink, o, do, lse, n_lat, m_ctx, name, carry=None):
    r = qkv.shape[0]
    nb, nbl = r // BLK, n_lat // BLK
    kcol = D_MODEL // LANES

    def body(sink_ref, q_ref, k_ref, v_ref, o_ref, do_ref, lse_ref, dq_ref, dk_ref, dv_ref, ds_ref):
        p, i = pl.program_id(0), pl.program_id(1)

        @pl.when(i == 0)
        def _():
            dk_ref[...] = jnp.zeros_like(dk_ref)
            dv_ref[...] = jnp.zeros_like(dv_ref)
            ds_ref[...] = jnp.zeros_like(ds_ref)

        start, mask, kblk, vblk, lo = _attn_setup(i, n_lat, m_ctx, nbl, k_ref, v_ref)
        kr, vr = pltpu.roll(kblk, HEAD_DIM, 1), pltpu.roll(vblk, HEAD_DIM, 1)
        lane = lax.broadcasted_iota(jnp.int32, (1, LANES), 1)
        dks, dvs = [], []
        for kh in range(2):
            k2, klo, khi = _kv_variants(kblk, kr, kh, lo)
            v2, _, _ = _kv_variants(vblk, vr, kh, lo)
            qst = _stack_heads(q_ref, kh, lo)
            dost = _stack_heads(do_ref, kh, lo)
            s, snk = _scores(qst, k2, mask, sink_ref, p, kh)
            lse4 = jnp.concatenate([lse_ref[:, kh * 4 + g:kh * 4 + g + 1] for g in range(4)], axis=0)
            pe = jnp.exp(s - lse4)
            dp = lax.dot_general(dost, v2, NT, preferred_element_type=F32)
            deltas = []
            for jp in range(2):
                cols = slice((2 * kh + jp) * LANES, (2 * kh + jp + 1) * LANES)
                prod = do_ref[:, cols].astype(F32) * o_ref[:, cols].astype(F32)
                deltas.append(jnp.sum(jnp.where(lo, prod, 0.0), axis=-1, keepdims=True))
                deltas.append(jnp.sum(jnp.where(lo, 0.0, prod), axis=-1, keepdims=True))
            delta = jnp.concatenate(deltas, axis=0)
            dsc = pe * (dp - delta)
            dsb, pb = dsc.astype(BF16), pe.astype(BF16)
            for jp in range(2):
                r0 = 2 * jp * BLK
                dq_ref[:, (2 * kh + jp) * LANES:(2 * kh + jp + 1) * LANES] = (
                    jnp.dot(dsb[r0:r0 + BLK], klo, preferred_element_type=F32)
                    + jnp.dot(dsb[r0 + BLK:r0 + 2 * BLK], khi, preferred_element_type=F32))
            dkf = lax.dot_general(dsb, qst, TN, preferred_element_type=F32)
            dvf = lax.dot_general(pb, dost, TN, preferred_element_type=F32)
            dks.append(dkf + pltpu.roll(dkf, HEAD_DIM, 1))
            dvs.append(dvf + pltpu.roll(dvf, HEAD_DIM, 1))
            contrib = -jnp.exp(snk - lse4) * delta
            for g in range(4):
                tot = jnp.sum(contrib[g * BLK:(g + 1) * BLK], axis=0, keepdims=True)
                ds_ref[0:1, :] += jnp.where(lane == kh * 4 + g, tot, 0.0)
        dk_blk = jnp.where(lo, dks[0], dks[1])
        dv_blk = jnp.where(lo, dvs[0], dvs[1])
        dk_ref[pl.ds(start, 3 * BLK), :] += dk_blk[:3 * BLK]
        dk_ref[pl.ds(n_lat, m_ctx), :] += dk_blk[3 * BLK:]
        dv_ref[pl.ds(start, 3 * BLK), :] += dv_blk[:3 * BLK]
        dv_ref[pl.ds(n_lat, m_ctx), :] += dv_blk[3 * BLK:]

    qspec = pl.BlockSpec((BLK, 4 * LANES), lambda p, i: (i, p))
    outs, carried = _pcall(
        body, (sink, qkv, qkv, qkv, o, do, lse), name=name, grid=(2, nb),
        in_specs=[SMEM_SPEC, qspec,
                  pl.BlockSpec((r, LANES), lambda p, i: (0, kcol + p)),
                  pl.BlockSpec((r, LANES), lambda p, i: (0, kcol + 2 + p)),
                  qspec, qspec, pl.BlockSpec((None, BLK, 8), lambda p, i: (p, i, 0))],
        out_specs=[qspec, pl.BlockSpec((r, LANES), lambda p, i: (0, p)), pl.BlockSpec((r, LANES), lambda p, i: (0, p)),
                   pl.BlockSpec((None, 8, LANES), lambda p, i: (p, 0, 0))],
        out_shape=[jax.ShapeDtypeStruct((r, D_MODEL), F32), jax.ShapeDtypeStruct((r, 2 * LANES), F32),
                   jax.ShapeDtypeStruct((r, 2 * LANES), F32), jax.ShapeDtypeStruct((2, 8, LANES), F32)], carry=carry)
    return outs if carry is None else (outs, carried)


def _gating_parts(z_ref, gv_ref):
    za = z_ref[...].astype(F32)
    zg = jax.nn.gelu(za)
    u, v = zg[:, :D_MODEL], zg[:, D_MODEL:]
    rs = lax.rsqrt(jnp.mean(v * v, axis=-1, keepdims=True) + EPS)
    return za, u, v, rs, v * rs * gv_ref[...]


def _mix(w_ref, vals):
    vb = vals.astype(BF16)
    return jnp.concatenate(
        [jnp.dot(w_ref[g], vb[:, g * LANES:(g + 1) * LANES], preferred_element_type=F32) for g in range(A_GROUPS)],
        axis=1)


def _gating_fwd(z, ws, bias, g_v, name):
    r = z.shape[0]

    def body(z_ref, w_ref, b_ref, gv_ref, y_ref):
        _, u, _, _, vn = _gating_parts(z_ref, gv_ref)
        y_ref[...] = (u * (_mix(w_ref, vn) + b_ref[...])).astype(BF16)

    return pl.pallas_call(
        body, name=name, grid=(r // CHUNK,),
        in_specs=[pl.BlockSpec((CHUNK, 2 * D_MODEL), lambda i: (i, P_A // (2 * D_MODEL))),
                  pl.BlockSpec((A_GROUPS, CHUNK, CHUNK), lambda i: (0, 0, 0)),
                  pl.BlockSpec((CHUNK, D_MODEL), lambda i: (0, 0)),
                  pl.BlockSpec((1, D_MODEL), lambda i: (0, 0))],
        out_specs=pl.BlockSpec((CHUNK, D_MODEL), lambda i: (i, 0)),
        out_shape=jax.ShapeDtypeStruct((r, D_MODEL), BF16), compiler_params=_params())(z, ws, bias, g_v)


def _gating_bwd(z, ws, ws_t, bias, g_v, dy, name, carry=None):
    r = z.shape[0]

    def body(z_ref, w_ref, wt_ref, b_ref, gv_ref, dy_ref, dz_ref, dw_ref, db_ref, dg_ref):
        i = pl.program_id(0)

        @pl.when(i == 0)
        def _():
            dw_ref[...] = jnp.zeros_like(dw_ref)
            db_ref[...] = jnp.zeros_like(db_ref)
            dg_ref[...] = jnp.zeros_like(dg_ref)

        za, u, v, rs, vn = _gating_parts(z_ref, gv_ref)
        dyv = dy_ref[...].astype(F32)
        du = dyv * (_mix(w_ref, vn) + b_ref[...])
        dmixed = dyv * u
        dvn = _mix(wt_ref, dmixed)
        dmb, vnb = dmixed.astype(BF16), vn.astype(BF16)
        for g in range(A_GROUPS):
            cols = slice(g * LANES, (g + 1) * LANES)
            dw_ref[g] += lax.dot_general(dmb[:, cols], vnb[:, cols], NT, preferred_element_type=F32)
            db_ref[:, g:g + 1] += jnp.sum(dmixed[:, cols], axis=-1, keepdims=True)
        gv = gv_ref[...]
        vh = v * rs
        dg_ref[0:1, :] += jnp.sum(dvn * vh, axis=0, keepdims=True)
        dvh = dvn * gv
        dv = rs * (dvh - vh * jnp.mean(dvh * vh, axis=-1, keepdims=True))
        _, vjp = jax.vjp(jax.nn.gelu, za)
        dz_ref[...] = vjp(jnp.concatenate([du, dv], axis=1))[0].astype(BF16)

    wspec = pl.BlockSpec((A_GROUPS, CHUNK, CHUNK), lambda i: (0, 0, 0))
    outs, carried = _pcall(
        body, (z, ws, ws_t, bias, g_v, dy), name=name, grid=(r // CHUNK,),
        in_specs=[pl.BlockSpec((CHUNK, 2 * D_MODEL), lambda i: (i, P_A // (2 * D_MODEL))), wspec, wspec,
                  pl.BlockSpec((CHUNK, D_MODEL), lambda i: (0, 0)), pl.BlockSpec((1, D_MODEL), lambda i: (0, 0)),
                  pl.BlockSpec((CHUNK, D_MODEL), lambda i: (i, 0))],
        out_specs=[pl.BlockSpec((CHUNK, 2 * D_MODEL), lambda i: (i, 0)), wspec,
                   pl.BlockSpec((CHUNK, A_GROUPS), lambda i: (0, 0)), pl.BlockSpec((8, D_MODEL), lambda i: (0, 0))],
        out_shape=[jax.ShapeDtypeStruct((r, 2 * D_MODEL), BF16), jax.ShapeDtypeStruct((A_GROUPS, CHUNK, CHUNK), F32),
                   jax.ShapeDtypeStruct((CHUNK, A_GROUPS), F32), jax.ShapeDtypeStruct((8, D_MODEL), F32)], carry=carry)
    return outs if carry is None else (outs, carried)


def _scr_rows(j, n_lat):
    s = pl.multiple_of(j * CONV_CHUNK, CONV_CHUNK)
    shift = jnp.where(j >= n_lat // CONV_CHUNK, 2 * PAD, PAD)
    return s, pl.multiple_of(s + shift, PAD)


def _taps(scr, j, n_lat):
    c = CONV_CHUNK
    s, at = _scr_rows(j, n_lat)
    ext = scr[pl.ds(pl.multiple_of(at - PAD, PAD), c + 2 * PAD), :]
    xm = pltpu.roll(ext, 1, 0)[PAD:PAD + c]
    xp = pltpu.roll(ext, c + 2 * PAD - 1, 0)[PAD:PAD + c]
    return s, xm, ext[PAD:PAD + c], xp


def _zero_pads(scr, r, n_lat):
    for at in (0, PAD + n_lat, 2 * PAD + r):
        scr[pl.ds(at, PAD), :] = jnp.zeros((PAD, LANES), F32)


def _col(arr_cols, c0):
    return pl.BlockSpec((arr_cols, LANES), lambda c: (0, c0 + c))


def _ffn_conv_fwd(up, w, n_lat, name):
    r = up.shape[0]
    nct = D_FF // LANES
    nchunk = r // CONV_CHUNK

    def body(a_ref, g_ref, w_ref, f_ref, scr):
        _zero_pads(scr, r, n_lat)

        def fill(j, _):
            s, at = _scr_rows(j, n_lat)
            scr[pl.ds(at, CONV_CHUNK), :] = a_ref[pl.ds(s, CONV_CHUNK), :].astype(F32)
            return 0
        lax.fori_loop(0, nchunk, fill, 0)
        wv = w_ref[...]

        def step(j, _):
            s, xm, x0, xp = _taps(scr, j, n_lat)
            ca = wv[0:1] * xm + wv[1:2] * x0 + wv[2:3] * xp
            gv = g_ref[pl.ds(s, CONV_CHUNK), :].astype(F32)
            f_ref[pl.ds(s, CONV_CHUNK), :] = (ca * _sigmoid(ca) * gv).astype(BF16)
            return 0
        lax.fori_loop(0, nchunk, step, 0)

    return pl.pallas_call(
        body, name=name, grid=(nct,),
        in_specs=[_col(r, 0), _col(r, nct), _col(3, 0)],
        out_specs=_col(r, 0), out_shape=jax.ShapeDtypeStruct((r, D_FF), BF16),
        scratch_shapes=[pltpu.VMEM((r + 3 * PAD, LANES), F32)], compiler_params=_params())(up, up, w)


def _ffn_conv_bwd(up, w, df, n_lat, name, carry=None):
    r = up.shape[0]
    nct = D_FF // LANES
    nchunk = r // CONV_CHUNK

    def body(a_ref, g_ref, w_ref, df_ref, dup_ref, dw_ref, scr, scr2):
        _zero_pads(scr, r, n_lat)
        _zero_pads(scr2, r, n_lat)

        def fill(j, _):
            s, at = _scr_rows(j, n_lat)
            scr[pl.ds(at, CONV_CHUNK), :] = a_ref[pl.ds(s, CONV_CHUNK), :].astype(F32)
            return 0
        lax.fori_loop(0, nchunk, fill, 0)
        wv = w_ref[...]

        def first(j, carry):
            s, xm, x0, xp = _taps(scr, j, n_lat)
            ca = wv[0:1] * xm + wv[1:2] * x0 + wv[2:3] * xp
            sg = _sigmoid(ca)
            gv = g_ref[pl.ds(s, CONV_CHUNK), :].astype(F32)
            dfv = df_ref[pl.ds(s, CONV_CHUNK), :].astype(F32)
            dup_ref[1, pl.ds(s, CONV_CHUNK), :] = (dfv * ca * sg).astype(BF16)
            dca = dfv * gv * (sg * (1.0 + ca * (1.0 - sg)))
            scr2[pl.ds(_scr_rows(j, n_lat)[1], CONV_CHUNK), :] = dca
            return tuple(cw + jnp.sum(dca * xv, axis=0, keepdims=True) for cw, xv in zip(carry, (xm, x0, xp)))
        zero = jnp.zeros((1, LANES), F32)
        dws = lax.fori_loop(0, nchunk, first, (zero, zero, zero))
        dw_ref[...] = _rows(list(dws) + [None] * 5)

        def second(j, _):
            s, ym, y0, yp = _taps(scr2, j, n_lat)
            dup_ref[0, pl.ds(s, CONV_CHUNK), :] = (wv[0:1] * yp + wv[1:2] * y0 + wv[2:3] * ym).astype(BF16)
            return 0
        lax.fori_loop(0, nchunk, second, 0)

    outs, carried = _pcall(
        body, (up, up, w, df), name=name, grid=(nct,),
        in_specs=[_col(r, 0), _col(r, nct), _col(3, 0), _col(r, 0)],
        out_specs=[pl.BlockSpec((2, r, LANES), lambda c: (0, 0, c)), _col(8, 0)],
        out_shape=[jax.ShapeDtypeStruct((2, r, D_FF), BF16), jax.ShapeDtypeStruct((8, D_FF), F32)],
        scratch_shapes=[pltpu.VMEM((r + 3 * PAD, LANES), F32), pltpu.VMEM((r + 3 * PAD, LANES), F32)], carry=carry)
    return outs if carry is None else (outs, carried)


def _sconv_fwd(z, w, n_lat, name):
    r = z.shape[0]
    nct = D_MODEL // LANES
    nchunk = r // CONV_CHUNK
    c0 = P_B // LANES

    def body(bg_ref, cg_ref, hb_ref, w_ref, y_ref, scr):
        _zero_pads(scr, r, n_lat)

        def fill(j, _):
            s, at = _scr_rows(j, n_lat)
            rows = pl.ds(s, CONV_CHUNK)
            scr[pl.ds(at, CONV_CHUNK), :] = cg_ref[rows, :].astype(F32) * hb_ref[rows, :].astype(F32)
            return 0
        lax.fori_loop(0, nchunk, fill, 0)
        wv = w_ref[...]

        def step(j, _):
            s, xm, x0, xp = _taps(scr, j, n_lat)
            conv = wv[0:1] * xm + wv[1:2] * x0 + wv[2:3] * xp
            y_ref[pl.ds(s, CONV_CHUNK), :] = (bg_ref[pl.ds(s, CONV_CHUNK), :].astype(F32) * conv).astype(BF16)
            return 0
        lax.fori_loop(0, nchunk, step, 0)

    return pl.pallas_call(
        body, name=name, grid=(nct,),
        in_specs=[_col(r, c0), _col(r, c0 + nct), _col(r, c0 + 2 * nct), _col(3, 0)],
        out_specs=_col(r, 0), out_shape=jax.ShapeDtypeStruct((r, D_MODEL), BF16),
        scratch_shapes=[pltpu.VMEM((r + 3 * PAD, LANES), F32)], compiler_params=_params())(z, z, z, w)


def _sconv_bwd(z, w, dy, n_lat, name):
    r = z.shape[0]
    nct = D_MODEL // LANES
    nchunk = r // CONV_CHUNK
    c0 = P_B // LANES

    def body(bg_ref, cg_ref, hb_ref, w_ref, dy_ref, dz_ref, dw_ref, scr, scr2):
        _zero_pads(scr, r, n_lat)
        _zero_pads(scr2, r, n_lat)

        def fill(j, _):
            s, at = _scr_rows(j, n_lat)
            rows = pl.ds(s, CONV_CHUNK)
            scr[pl.ds(at, CONV_CHUNK), :] = cg_ref[rows, :].astype(F32) * hb_ref[rows, :].astype(F32)
            return 0
        lax.fori_loop(0, nchunk, fill, 0)
        wv = w_ref[...]

        def first(j, carry):
            s, xm, x0, xp = _taps(scr, j, n_lat)
            rows = pl.ds(s, CONV_CHUNK)
            conv = wv[0:1] * xm + wv[1:2] * x0 + wv[2:3] * xp
            dyv = dy_ref[rows, :].astype(F32)
            dz_ref[0, rows, :] = (dyv * conv).astype(BF16)
            dconv = dyv * bg_ref[rows, :].astype(F32)
            scr2[pl.ds(_scr_rows(j, n_lat)[1], CONV_CHUNK), :] = dconv
            return tuple(cw + jnp.sum(dconv * xv, axis=0, keepdims=True) for cw, xv in zip(carry, (xm, x0, xp)))
        zero = jnp.zeros((1, LANES), F32)
        dws = lax.fori_loop(0, nchunk, first, (zero, zero, zero))
        dw_ref[...] = _rows(list(dws) + [None] * 5)

        def second(j, _):
            s, ym, y0, yp = _taps(scr2, j, n_lat)
            rows = pl.ds(s, CONV_CHUNK)
            dx = wv[0:1] * yp + wv[1:2] * y0 + wv[2:3] * ym
            dz_ref[1, rows, :] = (dx * hb_ref[rows, :].astype(F32)).astype(BF16)
            dz_ref[2, rows, :] = (dx * cg_ref[rows, :].astype(F32)).astype(BF16)
            return 0
        lax.fori_loop(0, nchunk, second, 0)

    return pl.pallas_call(
        body, name=name, grid=(nct,),
        in_specs=[_col(r, c0), _col(r, c0 + nct), _col(r, c0 + 2 * nct), _col(3, 0), _col(r, 0)],
        out_specs=[pl.BlockSpec((3, r, LANES), lambda c: (0, 0, c)), _col(8, 0)],
        out_shape=[jax.ShapeDtypeStruct((3, r, D_MODEL), BF16), jax.ShapeDtypeStruct((8, D_MODEL), F32)],
        scratch_shapes=[pltpu.VMEM((r + 3 * PAD, LANES), F32), pltpu.VMEM((r + 3 * PAD, LANES), F32)],
        compiler_params=_params())(z, z, z, w, dy)


def _merge_proj_out(z, b_gate, ps, w_out, bm, name):
    r = z.shape[0]
    sub = ROW_TILE
    row = pl.BlockSpec((bm, D_MODEL), lambda i: (i, 0))

    def body(zg_ref, b_ref, p0_ref, p1_ref, p2_ref, w_ref, m_ref, o_ref):
        def chunk(c, _):
            rows = pl.ds(pl.multiple_of(c * sub, sub), sub)
            gates = _sigmoid(zg_ref[rows, :].astype(F32) + b_ref[...])
            acc = None
            for t, p_ref in enumerate((p0_ref, p1_ref, p2_ref)):
                term = gates[:, t * D_MODEL:(t + 1) * D_MODEL] * p_ref[rows, :].astype(F32)
                acc = term if acc is None else acc + term
            m_ref[rows, :] = acc.astype(BF16)
            return 0
        lax.fori_loop(0, bm // sub, chunk, 0)
        o_ref[...] = jnp.dot(m_ref[...], w_ref[...], preferred_element_type=F32)

    return pl.pallas_call(
        body, name=name, grid=(r // bm,),
        in_specs=[pl.BlockSpec((bm, 3 * D_MODEL), lambda i: (i, 0)), pl.BlockSpec((1, 3 * D_MODEL), lambda i: (0, 0)),
                  row, row, row, pl.BlockSpec((D_MODEL, D_MODEL), lambda i: (0, 0))],
        out_specs=[row, row],
        out_shape=[jax.ShapeDtypeStruct((r, D_MODEL), BF16), jax.ShapeDtypeStruct((r, D_MODEL), F32)],
        compiler_params=_params())(z, b_gate, *ps, w_out)


def _merge_bwd(z, b_gate, ps, dmerged, name):
    r = z.shape[0]
    tm = ROW_TILE
    row = pl.BlockSpec((tm, D_MODEL), lambda i: (i, 0))
    wide = pl.BlockSpec((tm, 3 * D_MODEL), lambda i: (i, 0))

    def body(zg_ref, b_ref, p0_ref, p1_ref, p2_ref, dm_ref, d0_ref, d1_ref, d2_ref, dz_ref, db_ref):
        @pl.when(pl.program_id(0) == 0)
        def _():
            db_ref[...] = jnp.zeros_like(db_ref)

        gates = _sigmoid(zg_ref[...].astype(F32) + b_ref[...])
        dm = dm_ref[...].astype(F32)
        for t, (p_ref, d_ref) in enumerate(((p0_ref, d0_ref), (p1_ref, d1_ref), (p2_ref, d2_ref))):
            cols = slice(t * D_MODEL, (t + 1) * D_MODEL)
            gt = gates[:, cols]
            d_ref[...] = (dm * gt).astype(BF16)
            dlogit = dm * p_ref[...].astype(F32) * gt * (1.0 - gt)
            dz_ref[:, cols] = dlogit.astype(BF16)
            db_ref[0:1, cols] += jnp.sum(dlogit, axis=0, keepdims=True)

    shp = jax.ShapeDtypeStruct((r, D_MODEL), BF16)
    return pl.pallas_call(
        body, name=name, grid=(r // tm,),
        in_specs=[wide, pl.BlockSpec((1, 3 * D_MODEL), lambda i: (0, 0)), row, row, row, row],
        out_specs=[row, row, row, wide, pl.BlockSpec((8, 3 * D_MODEL), lambda i: (0, 0))],
        out_shape=[shp, shp, shp, jax.ShapeDtypeStruct((r, 3 * D_MODEL), BF16),
                   jax.ShapeDtypeStruct((8, 3 * D_MODEL), F32)],
        compiler_params=_params())(z, b_gate, *ps, dmerged)


def _sum_slots(buf, name):
    s, rows, _ = buf.shape
    whole_bytes = s * rows * LANES * 4
    tr = rows if whole_bytes <= VMEM_LIMIT // 8 else _pick(rows, [512, 256, 128, 64, 32, 16, 8])

    def body(b_ref, o_ref):
        acc = b_ref[0]
        for t in range(1, s):
            acc = acc + b_ref[t]
        o_ref[...] = acc

    return pl.pallas_call(
        body, name=name, grid=(rows // tr,),
        in_specs=[pl.BlockSpec((s, tr, LANES), lambda i: (0, i, 0))],
        out_specs=pl.BlockSpec((tr, LANES), lambda i: (i, 0)),
        out_shape=jax.ShapeDtypeStruct((rows, LANES), F32), compiler_params=_params())(buf)


def _adamw(w, gsrcs, m, v, name, scale=None):
    nl, rows, cols = w.shape
    assert len(gsrcs) == nl
    s = gsrcs[0].shape[0]
    tr = _pick(rows, [304, 352, 256, 128, 64, 32, 16, 8])
    blk = pl.BlockSpec((None, tr, cols), lambda l, i: (l, i, 0))
    c1 = 1.0 / (1.0 - ADAM_B1 ** ADAM_STEP)
    c2 = 1.0 / (1.0 - ADAM_B2 ** ADAM_STEP)

    def gspec(t):
        return pl.BlockSpec((s, tr, cols), lambda l, i: (0, jnp.where(l == t, i, 0), 0))

    def body(*refs):
        w_ref, g_refs, (m_ref, v_ref) = refs[0], refs[1:1 + nl], refs[1 + nl:3 + nl]
        rest = refs[3 + nl:]
        if scale is not None:
            sc_ref, rest = rest[0], rest[1:]
        go_ref, d_ref, mo_ref, vo_ref = rest
        layer = pl.program_id(0)
        g = None
        for t in range(nl):
            gt = g_refs[t][0].astype(F32)
            for q in range(1, s):
                gt = gt + g_refs[t][q].astype(F32)
            g = gt if g is None else jnp.where(layer == t, gt, g)
        if scale is not None:
            g = g * sc_ref[...]
        mn = ADAM_B1 * m_ref[...] + (1.0 - ADAM_B1) * g
        vn = ADAM_B2 * v_ref[...] + (1.0 - ADAM_B2) * (g * g)
        go_ref[...] = g
        mo_ref[...] = mn
        vo_ref[...] = vn
        d_ref[...] = -ADAM_LR * ((mn * c1) / (jnp.sqrt(vn * c2) + ADAM_EPS) + ADAM_WD * w_ref[...])

    shp = jax.ShapeDtypeStruct((nl, rows, cols), F32)
    ins = [w] + list(gsrcs) + [m, v] + ([] if scale is None else [scale])
    return pl.pallas_call(
        body, name=name, grid=(nl, rows // tr),
        in_specs=[blk] + [gspec(t) for t in range(nl)] + [blk, blk] + ([] if scale is None else [blk]),
        out_specs=[blk] * 4, out_shape=[shp] * 4, compiler_params=_params())(*ins)


def _pack(arrs):
    flat = []
    for a in arrs:
        a = a.reshape(-1).astype(F32)
        pad = (-a.shape[0]) % (8 * LANES)
        flat.append(jnp.pad(a, (0, pad)) if pad else a)
    return jnp.concatenate(flat).reshape(-1, LANES)


def _unpack(buf, shapes, lead=()):
    out, row = [], 0
    for shp in shapes:
        n = 1
        for d in shp:
            n *= d
        nrows = -(-n // (8 * LANES)) * 8
        piece = buf[..., row:row + nrows, :].reshape(lead + (nrows * LANES,))[..., :n]
        out.append(piece.reshape(lead + tuple(shp)))
        row += nrows
    return out


def _silu(x):
    return x * jax.nn.sigmoid(x)


def _rope_tables(n_lat, m_ctx):
    pos = jnp.arange(n_lat)
    row = (pos // GRID_W).astype(F32)
    col = (pos % GRID_W).astype(F32)
    half = HEAD_DIM // 2
    inv = ROPE_THETA ** (-jnp.arange(0, half, 2, dtype=F32) / half)
    ang = jnp.concatenate([row[:, None] * inv, col[:, None] * inv], axis=-1)
    cos, sin = jnp.cos(ang), jnp.sin(ang)
    cos2 = jnp.tile(jnp.concatenate([cos, cos], axis=-1), (1, LANES // HEAD_DIM))
    sin2 = jnp.tile(jnp.concatenate([-sin, sin], axis=-1), (1, LANES // HEAD_DIM))
    return (jnp.concatenate([cos2, jnp.ones((m_ctx, LANES), F32)], axis=0),
            jnp.concatenate([sin2, jnp.zeros((m_ctx, LANES), F32)], axis=0))


def _to_slots(full, axis):
    shp = full.shape
    new = shp[:axis] + (N_DEV, shp[axis] // N_DEV) + shp[axis + 1:]
    return jnp.moveaxis(full.reshape(new), axis, 0)


def _from_slots(slots, axis):
    moved = jnp.moveaxis(slots, 0, axis)
    shp = moved.shape
    return moved.reshape(shp[:axis] + (shp[axis] * shp[axis + 1],) + shp[axis + 2:])


def _permute_in(wt, name):
    blk = 512
    segs = [(OFF_G, IN_W), (0, OFF_K), (OFF_A, OFF_B), (OFF_B, OFF_G), (OFF_K, OFF_A)]

    def source(j):
        src, at = 0, 0
        for lo, hi in segs:
            n = (hi - lo) // blk
            src = src + jnp.where((j >= at) & (j < at + n), j - at + lo // blk, 0)
            at += n
        return (src, 0)

    def body(i_ref, o_ref):
        o_ref[...] = i_ref[...]

    return pl.pallas_call(
        body, name=name, grid=(IN_W // blk,), in_specs=[pl.BlockSpec((blk, D_MODEL), source)],
        out_specs=pl.BlockSpec((blk, D_MODEL), lambda j: (j, 0)), out_shape=jax.ShapeDtypeStruct(wt.shape, wt.dtype),
        compiler_params=_params())(wt)


def kernel(x, c, ctx, c_ctx, w_mod, b_mod, g_mix, w_in, b_gate, sink, w_spatial, b_spatial, g_v, w_sconv, w_branch, w_out, g_ffn, w_up, w_fconv, w_down, g_final, loss_target, m_c_ctx, m_w_mod, m_b_mod, m_g_mix, m_w_in, m_b_gate, m_sink, m_w_spatial, m_b_spatial, m_g_v, m_w_sconv, m_w_branch, m_w_out, m_g_ffn, m_w_up, m_w_fconv, m_w_down, m_g_final, v_c_ctx, v_w_mod, v_b_mod, v_g_mix, v_w_in, v_b_gate, v_sink, v_w_spatial, v_b_spatial, v_g_v, v_w_sconv, v_w_branch, v_w_out, v_g_ffn, v_w_up, v_w_fconv, v_w_down, v_g_final):
    n_lat, m_ctx = x.shape[1], ctx.shape[1]
    r = n_lat + m_ctx
    me = 4 * lax.axis_index("x") + 2 * lax.axis_index("y") + lax.axis_index("c")
    mod_w = w_mod.shape[2]
    bm = _pick(r, [768, 256])

    tr = lambda a: jnp.swapaxes(a, -1, -2)
    shard_axis = {"in": 0, "br": 1, "out": 0, "up": 0, "dn": 0}
    shards = {}
    for kind, wt in (("in", tr(w_in)), ("br", w_branch), ("out", w_out), ("up", tr(w_up)), ("dn", w_down)):
        wb = wt.astype(BF16)
        for l in range(DEPTH):
            shards[kind, l] = wb[l]
    full = {}

    def arrive(items, got):
        for key, slots in zip(items, got):
            wfull = _from_slots(slots, shard_axis[key[0]])
            full[key] = _permute_in(wfull, f"permute_in{key[1]}") if key[0] == "in" else wfull

    def gather_of(items):
        return [shards[key] for key in items], ["gather"] * len(items)

    small_shapes = [c.shape, w_sconv.shape, w_fconv.shape]
    (g_small,) = _exchange([_pack([c, w_sconv, w_fconv])], ["gather"], "gather_first")
    c_all, sconv_all, fconv_all = _unpack(g_small, small_shapes, lead=(N_DEV,))
    c_all = c_all.reshape(N_DEV, D_MODEL)
    w_sconv_full = _from_slots(sconv_all, 2)
    w_fconv_full = _from_slots(fconv_all, 2)

    act = jnp.concatenate([_silu(c_all), _silu(c_ctx)[None], jnp.zeros((7, D_MODEL), F32)], axis=0)
    mod_part = jnp.stack([_mm(act, w_mod[l], name=f"mod_fwd{l}", bm=16, bn=mod_w, bk=D_MODEL, out_dtype=F32)
                          for l in range(DEPTH)])
    (mod_all,) = _exchange([mod_part], ["gather"], "gather_mod")
    mod_full = _from_slots(mod_all, 2) + b_mod[:, None, :]
    mods = []
    for l in range(DEPTH):
        mine = lax.dynamic_index_in_dim(mod_full[l], me, axis=0, keepdims=False).reshape(6, D_MODEL)
        theirs = mod_full[l, N_DEV].reshape(6, D_MODEL)
        mods.append(jnp.pad(jnp.stack([mine, theirs]), ((0, 0), (0, 2), (0, 0))))

    cos, sin = _rope_tables(n_lat, m_ctx)
    xs = jnp.concatenate([x[0], ctx[0]], axis=0)
    ws_b = w_spatial.astype(BF16)
    ws_t = jnp.swapaxes(w_spatial, 2, 3).astype(BF16)
    vec = lambda a: a.reshape(1, -1)

    saved = []
    res = None
    for l in range(DEPTH):
        s = {}
        if res is None:
            s["x0"] = xs
            s["h"], got = _norm_fwd(xs, vec(g_mix[l]), mods[l], n_lat, 0, 1, f"norm_mix{l}",
                                    carry=gather_of([("in", l)]))
            arrive([("in", l)], got)
        else:
            s["x0"], s["h"] = _norm_fwd(xs, vec(g_mix[l]), mods[l], n_lat, 0, 1, f"norm_mix{l}", res=res)
        items = [("br", l), ("out", l), ("up", l)]
        s["z"], got = _mm(s["h"], full["in", l], tb=True, name=f"proj_in{l}", bm=bm, bn=_pick(IN_W, [2432, 512]),
                          bk=D_MODEL, out_dtype=BF16, cols_outer=True, carry=gather_of(items))
        arrive(items, got)
        s["qkv"] = _rope_fwd(s["z"], cos, sin, f"rope{l}")
        items = [("dn", l)] + ([("in", l + 1)] if l + 1 < DEPTH else [])
        (s["y0"], s["lse"]), got = _attn_fwd(s["qkv"], sink[l], n_lat, m_ctx, f"attn{l}", carry=gather_of(items))
        arrive(items, got)
        s["bias"] = jnp.repeat(b_spatial[l].T, LANES, axis=1)
        s["y1"] = _gating_fwd(s["z"], ws_b[l], s["bias"], vec(g_v[l]), f"gating{l}")
        s["y2"] = _sconv_fwd(s["z"], w_sconv_full[l], n_lat, f"sconv{l}")
        s["p"] = [_mm(s[f"y{t}"], full["br", l], b_lead=t, name=f"branch{l}_{t}", bm=bm, bn=D_MODEL, bk=D_MODEL,
                      out_dtype=BF16) for t in range(3)]
        s["merged"], s["o"] = _merge_proj_out(s["z"], vec(b_gate[l]), s["p"], full["out", l], bm, f"proj_out{l}")
        s["x1"], s["h2"] = _norm_fwd(s["x0"], vec(g_ffn[l]), mods[l], n_lat, 3, 4, f"norm_ffn{l}",
                                     res=(s["o"], mods[l], 2))
        s["up"] = _mm(s["h2"], full["up", l], tb=True, name=f"ffn_up{l}", bm=bm, bn=_pick(2 * D_FF, [1408]),
                      bk=D_MODEL, out_dtype=BF16, cols_outer=True)
        s["f"] = _ffn_conv_fwd(s["up"], w_fconv_full[l], n_lat, f"ffn_conv{l}")
        s["dd"] = _mm(s["f"], full["dn", l], name=f"ffn_down{l}", bm=bm, bn=D_MODEL, bk=D_FF, out_dtype=F32)
        saved.append(s)
        xs, res = s["x1"], (s["dd"], mods[l], 5)

    top = saved[DEPTH - 1]
    dxs, d_dd, acc_final, accs_top = _loss_bwd(top["x1"], top["dd"], mods[DEPTH - 1], vec(g_final), loss_target[0],
                                               n_lat, "loss")
    loss_part = acc_final[1, 0]
    dg_final = acc_final[0]
    dmods = [None] * DEPTH
    gate2 = accs_top[:, 2]
    grads = {k: [None] * DEPTH for k in ("g_mix", "g_ffn", "g_v", "b_gate", "sink", "w_spatial", "b_spatial",
                                         "w_sconv", "w_fconv", "w_in", "w_branch", "w_out", "w_up", "w_down")}
    bk_r = _pick(r, [2816, 768, 256])
    small_names = ["g_ffn", "g_v", "b_gate", "sink", "w_spatial", "b_spatial", "w_sconv", "w_fconv"]
    recv = {}
    small_recv, small_shapes_of = {}, {}

    def small_pack(l):
        arrs = [grads[k][l] for k in small_names]
        if l > 0:
            arrs.append(grads["g_mix"][l])
        if l == DEPTH - 1:
            arrs += [loss_part.reshape(1), dg_final]
        small_shapes_of[l] = [a.shape for a in arrs]
        return _pack(arrs)

    for l in reversed(range(DEPTH)):
        s = saved[l]
        df = _mm(d_dd, full["dn", l], tb=True, name=f"d_ffn_down{l}", bm=bm, bn=_pick(D_FF, [1408]), bk=D_MODEL,
                 out_dtype=BF16)
        grads["w_down"][l] = _mm(s["f"], d_dd, ta=True, name=f"g_ffn_down{l}", bm=_pick(D_FF, [1408]), bn=D_MODEL,
                                 bk=bk_r, out_dtype=BF16)
        arrs, modes = [_to_slots(grads["w_down"][l], 0)], ["a2a"]
        if l + 1 < DEPTH:
            arrs.append(small_pack(l + 1))
            modes.append("gather")
        (dup, dwf), got = _ffn_conv_bwd(s["up"], w_fconv_full[l], df, n_lat, f"d_ffn_conv{l}", carry=(arrs, modes))
        recv["dn", l] = got[0]
        if l + 1 < DEPTH:
            small_recv[l + 1] = got[1]
        grads["w_fconv"][l] = dwf[:3]
        dh2 = _mm_pieces([(dup, "step", 0, 2, 0, D_FF)], full["up", l], w_t=True, name=f"d_ffn_up{l}", bm=bm,
                         out_dtype=BF16)
        kb = 1408
        nbh = D_FF // kb
        grads["w_up"][l] = _mm(
            dup, s["h2"], ta=True, name=f"g_ffn_up{l}", bm=kb, bn=D_MODEL, bk=bk_r, out_dtype=BF16,
            a_spec=((None, bk_r, kb), lambda i, j, k: (i // nbh, k, i % nbh), 2 * D_FF))
        dx1, d_o, acc, accs = _norm_bwd(s["x1"], vec(g_ffn[l]), mods[l], dh2, dxs, n_lat, 3, 4, f"d_norm_ffn{l}",
                                        res=(s["o"], mods[l], 2))
        grads["g_ffn"][l] = acc[0]
        shift2, scale2, gate1 = accs[:, 0], accs[:, 1], accs[:, 2]
        dmerged = _mm(d_o, full["out", l], tb=True, name=f"d_proj_out{l}", bm=bm, bn=D_MODEL, bk=D_MODEL,
                      out_dtype=BF16)
        grads["w_out"][l] = _mm(s["merged"], d_o, ta=True, name=f"g_proj_out{l}", bm=D_MODEL, bn=D_MODEL, bk=bk_r,
                                out_dtype=BF16)
        dp0, dp1, dp2, dz_g, dbg = _merge_bwd(s["z"], vec(b_gate[l]), s["p"], dmerged, f"d_merge{l}")
        grads["b_gate"][l] = dbg[0]
        dps = (dp0, dp1, dp2)
        dys = [_mm(dps[t], full["br", l], tb=True, b_lead=t, name=f"d_branch{l}_{t}", bm=bm, bn=D_MODEL, bk=D_MODEL,
                   out_dtype=BF16) for t in range(3)]
        grads["w_branch"][l] = jnp.stack(
            [_mm(s[f"y{t}"], dps[t], ta=True, name=f"g_branch{l}_{t}", bm=D_MODEL, bn=D_MODEL, bk=bk_r,
                 out_dtype=BF16) for t in range(3)])
        arrs = [_to_slots(grads["w_up"][l], 0)] + ([_to_slots(grads["w_in"][l + 1], 0)] if l + 1 < DEPTH else [])
        (dq, dk, dv, dsk), got = _attn_bwd(s["qkv"], sink[l], s["y0"], dys[0], s["lse"], n_lat, m_ctx, f"d_attn{l}",
                                           carry=(arrs, ["a2a"] * len(arrs)))
        recv["up", l] = got[0]
        if l + 1 < DEPTH:
            recv["in", l + 1] = got[1]
        grads["sink"][l] = dsk[:, 0, :8].reshape(N_HEADS)
        dz_qkv = _rope_bwd(dq, dk, dv, cos, sin, f"d_rope{l}")
        (dz_a, dws, dbs, dgv), (recv["out", l], recv["br", l]) = _gating_bwd(
            s["z"], ws_b[l], ws_t[l], s["bias"], vec(g_v[l]), dys[1], f"d_gating{l}",
            carry=([_to_slots(grads["w_out"][l], 0), _to_slots(grads["w_branch"][l], 1)], ["a2a"] * 2))
        grads["w_spatial"][l], grads["b_spatial"][l], grads["g_v"][l] = dws, dbs.T, dgv[0]
        dz_b, dwsc = _sconv_bwd(s["z"], w_sconv_full[l], dys[2], n_lat, f"d_sconv{l}")
        grads["w_sconv"][l] = dwsc[:3]
        kvw = 2 * N_KV_HEADS * HEAD_DIM
        pieces = [(dz_g, None, 0, 3, 0, D_MODEL), (dz_qkv, None, 0, 1, P_Q // D_MODEL, D_MODEL),
                  (dz_a, None, 0, 2, P_A // D_MODEL, D_MODEL), (dz_b, "step", 0, 3, P_B // D_MODEL, D_MODEL),
                  (dz_qkv, None, D_MODEL // kvw, 1, P_KV // kvw, kvw)]
        gw = lambda a, nm, rows, **kw: _mm(a, s["h"], ta=True, name=f"g_proj_in{l}_{nm}", bm=rows, bn=D_MODEL,
                                           bk=bk_r, out_dtype=BF16, **kw)
        gw_g = gw(dz_g, "gate", 1536)
        gw_qkv = gw(dz_qkv, "qkv", 1536)
        gw_a = gw(dz_a, "gating", 1024)
        gw_b = [gw(dz_b, f"sconv{t}", 1024, a_lead=t) for t in range(3)]
        grads["w_in"][l] = jnp.concatenate([gw_qkv, gw_a] + gw_b + [gw_g], axis=0)
        bm_in = _pick(r, [1056, 768, 256])
        if l > 0:
            dh = _mm_pieces(pieces, full["in", l], w_t=True, name=f"d_proj_in{l}", bm=bm_in, out_dtype=BF16)
        else:
            dh, (recv["in", 0], small_recv[0]) = _mm_pieces(
                pieces, full["in", l], w_t=True, name=f"d_proj_in{l}", bm=bm_in, out_dtype=BF16,
                carry=([_to_slots(grads["w_in"][0], 0), small_pack(0)], ["a2a", "gather"]))
        below = None if l == 0 else (saved[l - 1]["dd"], mods[l - 1], 5)
        outs = _norm_bwd(s["x0"], vec(g_mix[l]), mods[l], dh, dx1, n_lat, 0, 1, f"d_norm_mix{l}", res=below)
        if below is None:
            dxs, acc, accs = outs
        else:
            dxs, d_dd, acc, accs = outs
        grads["g_mix"][l] = acc[0]
        dmods[l] = jnp.stack([accs[:, 0], accs[:, 1], gate1, shift2, scale2, gate2], axis=1)
        gate2 = accs[:, 2]

    dmod_own = jnp.stack([dmods[l][0].reshape(-1) for l in range(DEPTH)])
    dmod_ctx = jnp.stack([dmods[l][1].reshape(-1) for l in range(DEPTH)])
    late = [grads["g_mix"][0], dmod_own + dmod_ctx, dmod_ctx, dmod_own]
    late_shapes = [a.shape for a in late]
    (late_all,) = _exchange([_pack(late)], ["gather"], "gather_late_grads")
    g_mix0_g, b_mod_g, dmodc_tot, _ = _unpack(_sum_slots(late_all, "sum_late_grads"), late_shapes)
    dmod_all = _unpack(late_all, late_shapes, lead=(N_DEV,))[-1]
    layer_sums = [_unpack(_sum_slots(small_recv[l], f"sum_small_grads{l}"), small_shapes_of[l]) for l in range(DEPTH)]
    by_name = {k: jnp.stack([layer_sums[l][t] for l in range(DEPTH)]) for t, k in enumerate(small_names)}
    g_mix_g = jnp.stack([g_mix0_g] + [layer_sums[l][len(small_names)] for l in range(1, DEPTH)])
    loss_sum, g_final_g = layer_sums[DEPTH - 1][-2], layer_sums[DEPTH - 1][-1]
    g_ffn_g, g_v_g, b_gate_g, sink_g = by_name["g_ffn"], by_name["g_v"], by_name["b_gate"], by_name["sink"]
    w_spatial_g, b_spatial_g = by_name["w_spatial"], by_name["b_spatial"]
    w_sconv_g = lax.dynamic_slice_in_dim(by_name["w_sconv"], me * w_sconv.shape[2], w_sconv.shape[2], axis=2)
    w_fconv_g = lax.dynamic_slice_in_dim(by_name["w_fconv"], me * w_fconv.shape[2], w_fconv.shape[2], axis=2)

    dmod_cols = lax.dynamic_slice_in_dim(dmod_all, me * mod_w, mod_w, axis=2)
    dmodc_cols = lax.dynamic_slice_in_dim(dmodc_tot, me * mod_w, mod_w, axis=1)
    g_w_mod, cctx_part = [], None
    for l in range(DEPTH):
        rhs = jnp.concatenate([dmod_cols[:, l], dmodc_cols[l][None], jnp.zeros((7, mod_w), F32)], axis=0)
        g_w_mod.append(_mm(act, rhs, ta=True, name=f"g_mod{l}", bm=D_MODEL, bn=mod_w, bk=16, out_dtype=F32))
        lhs = jnp.pad(dmodc_cols[l][None], ((0, 7), (0, 0)))
        part = _mm(lhs, w_mod[l], tb=True, name=f"d_cctx{l}", bm=8, bn=D_MODEL, bk=mod_w, out_dtype=F32)
        cctx_part = part if cctx_part is None else cctx_part + part
    sg = jax.nn.sigmoid(c_ctx)
    dsilu = (sg * (1.0 + c_ctx * (1.0 - sg))).reshape(8, LANES)

    (r_cctx,) = _exchange([cctx_part[0].reshape(8, LANES)], ["gather"], "gather_c_ctx_grad")

    def per_layer(a):
        return a.reshape(a.shape[0], -1, a.shape[-1])

    upd = {}
    for nm, kind, wv, mv, vv in (("w_in", "in", w_in, m_w_in, v_w_in), ("w_branch", "br", w_branch, m_w_branch, v_w_branch),
                                 ("w_out", "out", w_out, m_w_out, v_w_out), ("w_up", "up", w_up, m_w_up, v_w_up),
                                 ("w_down", "dn", w_down, m_w_down, v_w_down), ("w_mod", None, w_mod, m_w_mod, v_w_mod)):
        if kind is None:
            gsrcs = [g[None] for g in g_w_mod]
        else:
            gsrcs = [per_layer(recv[kind, l]) for l in range(DEPTH)]
        if kind in ("in", "up"):
            outs = _adamw(tr(wv), gsrcs, tr(mv), tr(vv), f"adamw_{nm}")
            upd[nm] = [tr(o) for o in outs]
        else:
            outs = _adamw(per_layer(wv), gsrcs, per_layer(mv), per_layer(vv), f"adamw_{nm}")
            upd[nm] = [o.reshape(wv.shape) for o in outs]
    as_tile = lambda a: a.reshape(1, 8, LANES)
    upd["c_ctx"] = [o.reshape(D_MODEL) for o in _adamw(
        as_tile(c_ctx), [r_cctx], as_tile(m_c_ctx), as_tile(v_c_ctx), "adamw_c_ctx", scale=as_tile(dsilu))]

    names = ["b_mod", "g_mix", "b_gate", "sink", "w_spatial", "b_spatial", "g_v", "w_sconv", "g_ffn", "w_fconv",
             "g_final"]
    w_s = [b_mod, g_mix, b_gate, sink, w_spatial, b_spatial, g_v, w_sconv, g_ffn, w_fconv, g_final]
    g_s = [b_mod_g, g_mix_g, b_gate_g, sink_g, w_spatial_g, b_spatial_g, g_v_g, w_sconv_g, g_ffn_g, w_fconv_g,
           g_final_g]
    m_s = [m_b_mod, m_g_mix, m_b_gate, m_sink, m_w_spatial, m_b_spatial, m_g_v, m_w_sconv, m_g_ffn, m_w_fconv,
           m_g_final]
    v_s = [v_b_mod, v_g_mix, v_b_gate, v_sink, v_w_spatial, v_b_spatial, v_g_v, v_w_sconv, v_g_ffn, v_w_fconv,
           v_g_final]
    shapes = [a.shape for a in w_s]
    packed = _adamw(_pack(w_s)[None], [_pack(g_s)[None]], _pack(m_s)[None], _pack(v_s)[None], "adamw_small")
    unpacked = [_unpack(o[0], shapes) for o in packed]
    for t, nm in enumerate(names):
        upd[nm] = [unpacked[q][t] for q in range(4)]

    order = ["c_ctx", "w_mod", "b_mod", "g_mix", "w_in", "b_gate", "sink", "w_spatial", "b_spatial", "g_v", "w_sconv",
             "w_branch", "w_out", "g_ffn", "w_up", "w_fconv", "w_down", "g_final"]
    result = [loss_sum.reshape(()), dxs[None]]
    for q in range(4):
        result += [upd[nm][q] for nm in order]
    return tuple(result)
```

```python
import jax
import jax.numpy as jnp
from jax import lax
from jax.experimental import pallas as pl
from jax.experimental.pallas import tpu as pltpu

F32, BF16 = jnp.float32, jnp.bfloat16

D_MODEL = 1024
DEPTH = 2
GRID_W = 64
N_HEADS = 16
N_KV_HEADS = 4
HEAD_DIM = 64
WINDOW = 128
BLK = 128
ROPE_THETA = 10000.0
CHUNK = 128
A_GROUPS = 8
D_FF = 2816
EPS = 1e-6
NEG = -1e30
IN_W = 9728
OFF_K, OFF_A, OFF_B, OFF_G = 1024, 1536, 3584, 6656
P_Q, P_A, P_B, P_KV = 3072, 4096, 6144, 9216

N_DEV = 8
LANES = 128
ROW_TILE = 256
CONV_CHUNK = 256
PAD = 8
VMEM_LIMIT = 52 * 1024 * 1024

ADAM_LR, ADAM_B1, ADAM_B2, ADAM_EPS, ADAM_WD, ADAM_STEP = 0.001, 0.9, 0.999, 1e-08, 0.01, 10

HBM_SPEC = pl.BlockSpec(memory_space=pltpu.HBM)
SMEM_SPEC = pl.BlockSpec(memory_space=pltpu.SMEM)


def _params():
    return pltpu.CompilerParams(vmem_limit_bytes=VMEM_LIMIT)


def _pick(n, prefs):
    for p in prefs:
        if n % p == 0:
            return p
    raise ValueError((n, prefs))


def _sigmoid(x):
    return 0.5 * jnp.tanh(0.5 * x) + 0.5


def _mm(a, b, *, name, ta=False, tb=False, bm, bn, bk, out_dtype, a_lead=None, b_lead=None, a_spec=None,
        b_spec=None, cols_outer=False, carry=None):
    ash = a.shape[1:] if a_lead is not None else a.shape
    bsh = b.shape[1:] if b_lead is not None else b.shape
    kc = (bsh[1] if tb else bsh[0]) if b_spec is None else (ash[0] if ta else ash[1])
    mo =(ash[1] if ta else ash[0]) if a_spec is None else a_spec[2]
    no = (bsh[0] if tb else bsh[1]) if b_spec is None else b_spec[2]
    assert mo % bm == 0 and no % bn == 0 and kc % bk == 0, (name, mo, no, kc, bm, bn, bk)
    nk = kc // bk

    def spec(shape, fn, idx=None):
        if idx is not None:
            shape, inner = (None,) + shape, fn
            fn = lambda i, j, k: (idx,) + inner(i, j, k)
        if cols_outer:
            return pl.BlockSpec(shape, lambda j, i, k: fn(i, j, k))
        return pl.BlockSpec(shape, fn)

    if a_spec is not None:
        a_bs = spec(a_spec[0], a_spec[1])
    elif ta:
        a_bs = spec((bk, bm), lambda i, j, k: (k, i), a_lead)
    else:
        a_bs = spec((bm, bk), lambda i, j, k: (i, k), a_lead)
    if b_spec is not None:
        b_bs = spec(b_spec[0], b_spec[1])
    elif tb:
        b_bs = spec((bn, bk), lambda i, j, k: (j, k), b_lead)
    else:
        b_bs = spec((bk, bn), lambda i, j, k: (k, j), b_lead)
    dims = (((0 if ta else 1,), (1 if tb else 0,)), ((), ()))
    grid = (no // bn, mo // bm, nk) if cols_outer else (mo // bm, no // bn, nk)

    def body(a_ref, b_ref, o_ref, *scratch):
        if nk == 1:
            o_ref[...] = lax.dot_general(a_ref[...], b_ref[...], dims, preferred_element_type=F32).astype(o_ref.dtype)
        else:
            acc = scratch[0]
            k = pl.program_id(2)

            @pl.when(k == 0)
            def _():
                acc[...] = jnp.zeros_like(acc)

            acc[...] += lax.dot_general(a_ref[...], b_ref[...], dims, preferred_element_type=F32)

            @pl.when(k == nk - 1)
            def _():
                o_ref[...] = acc[...].astype(o_ref.dtype)

    outs, carried = _pcall(
        body, (a, b), name=name, grid=grid, in_specs=[a_bs, b_bs],
        out_specs=[spec((bm, bn), lambda i, j, k: (i, j))],
        out_shape=[jax.ShapeDtypeStruct((mo, no), out_dtype)],
        scratch_shapes=[pltpu.VMEM((bm, bn), F32)] if nk > 1 else [], carry=carry)
    return outs[0] if carry is None else (outs[0], carried)


def _mm_pieces(pieces, w, *, name, bm, w_t=False, out_dtype=F32, carry=None):
    kout = w.shape[1] if w_t else w.shape[0]
    starts, total = [], 0
    for piece in pieces:
        starts.append(total)
        total += piece[3]
    mo = pieces[0][0].shape[-2]
    assert mo % bm == 0
    widths = sorted({piece[5] for piece in pieces}, reverse=True)

    def inside(p, k):
        return (k >= starts[p]) & (k < starts[p] + pieces[p][3])

    def a_spec(p):
        _, lead, col0, nblk, _, bk = pieces[p]

        def fn(i, k):
            t = jnp.clip(k - starts[p], 0, nblk - 1)
            if lead is None:
                return (i, col0 + t)
            return (t, i, col0) if lead == "step" else (lead, i, col0 + t)
        return pl.BlockSpec((bm, bk) if lead is None else (None, bm, bk), fn)

    def w_spec(bk):
        def fn(i, k):
            col = 0
            for p, piece in enumerate(pieces):
                if piece[5] == bk:
                    col = col + jnp.where(inside(p, k), piece[4] + k - starts[p], 0)
            return (col, 0) if w_t else (0, col)
        return pl.BlockSpec((bk, kout) if w_t else (kout, bk), fn)

    n_p, n_w = len(pieces), len(widths)

    def body(*refs):
        a_refs, w_refs, o_ref, acc = refs[:n_p], refs[n_p:n_p + n_w], refs[n_p + n_w], refs[n_p + n_w + 1]
        k = pl.program_id(1)

        @pl.when(k == 0)
        def _():
            acc[...] = jnp.zeros_like(acc)

        for p in range(n_p):
            w_ref = w_refs[widths.index(pieces[p][5])]

            @pl.when(inside(p, k))
            def _(p=p, w_ref=w_ref):
                acc[...] += lax.dot_general(a_refs[p][...], w_ref[...], NN if w_t else NT,
                                            preferred_element_type=F32)

        @pl.when(k == total - 1)
        def _():
            o_ref[...] = acc[...].astype(o_ref.dtype)

    outs, carried = _pcall(
        body, [piece[0] for piece in pieces] + [w] * n_w, name=name, grid=(mo // bm, total),
        in_specs=[a_spec(p) for p in range(n_p)] + [w_spec(bk) for bk in widths],
        out_specs=[pl.BlockSpec((bm, kout), lambda i, k: (i, 0))],
        out_shape=[jax.ShapeDtypeStruct((mo, kout), out_dtype)],
        scratch_shapes=[pltpu.VMEM((bm, kout), F32)], carry=carry)
    return outs[0] if carry is None else (outs[0], carried)


def _xchg_out_shapes(arrs, modes):
    return [jax.ShapeDtypeStruct((N_DEV,) + a.shape if m == "gather" else a.shape, a.dtype)
            for a, m in zip(arrs, modes)]


def _xchg_sems(n):
    return [pltpu.SemaphoreType.DMA((n, N_DEV - 1)), pltpu.SemaphoreType.DMA((n, N_DEV - 1)),
            pltpu.SemaphoreType.DMA((n,))]


def _xchg_copies(ins, outs, modes, sems):
    send_sems, recv_sems, local_sems = sems
    x, y, c = lax.axis_index("x"), lax.axis_index("y"), lax.axis_index("c")
    me = 4 * x + 2 * y + c

    def place(q):
        px = 1 - x if (q >> 2) & 1 else x
        py = 1 - y if (q >> 1) & 1 else y
        pc = 1 - c if q & 1 else c
        return (px, py, pc), 4 * px + 2 * py + pc

    sibling, _ = place(1)
    out = dict(local=[], direct=[], landed=[], passed=[], others=[])
    for t, mode in enumerate(modes):
        gather = mode == "gather"
        mine = ins[t] if gather else ins[t].at[me]
        out["local"].append(pltpu.make_async_copy(mine, outs[t].at[me], local_sems.at[t]))
        for q in range(1, N_DEV):
            peer, slot = place(q)
            sem = dict(send_sem=send_sems.at[t, q - 1], recv_sem=recv_sems.at[t, q - 1],
                       device_id_type=pl.DeviceIdType.MESH)
            arrival = pltpu.make_async_remote_copy(src_ref=outs[t].at[slot], dst_ref=outs[t].at[slot],
                                                   device_id=peer, **sem)
            if not gather:
                out["direct"].append(pltpu.make_async_remote_copy(src_ref=ins[t].at[slot], dst_ref=outs[t].at[me],
                                                                  device_id=peer, **sem))
                out["others"].append(arrival)
            elif q == 1 or q % 2 == 0:
                out["direct"].append(pltpu.make_async_remote_copy(src_ref=ins[t], dst_ref=outs[t].at[me],
                                                                  device_id=peer, **sem))
                out["others" if q == 1 else "landed"].append(arrival)
            else:
                _, origin = place(q - 1)
                out["passed"].append(pltpu.make_async_remote_copy(src_ref=outs[t].at[origin], dst_ref=outs[t].at[origin],
                                                                  device_id=sibling, **sem))
                out["others"].append(arrival)
    return out


def _xchg_start(copies):
    for cp in copies["local"] + copies["direct"]:
        cp.start()


def _xchg_wait(copies):
    for cp in copies["landed"]:
        cp.wait_recv()
    for cp in copies["passed"]:
        cp.start()
    for cp in copies["others"]:
        cp.wait_recv()
    for cp in copies["direct"] + copies["passed"]:
        cp.wait_send()
    for cp in copies["local"]:
        cp.wait()


def _exchange(arrs, modes, name):
    n = len(arrs)

    def body(*refs):
        copies = _xchg_copies(refs[:n], refs[n:2 * n], modes, refs[2 * n:])
        _xchg_start(copies)
        _xchg_wait(copies)

    outs = pl.pallas_call(
        body, name=name, in_specs=[HBM_SPEC] * n, out_specs=[HBM_SPEC] * n, out_shape=_xchg_out_shapes(arrs, modes),
        scratch_shapes=_xchg_sems(n), compiler_params=pltpu.CompilerParams(has_side_effects=True),
    )(*arrs)
    return list(outs)


def _pcall(body, operands, *, name, grid, in_specs, out_specs, out_shape, scratch_shapes=(), carry=None):
    out_specs, out_shape, scratch_shapes = list(out_specs), list(out_shape), list(scratch_shapes)
    if carry is None:
        outs = pl.pallas_call(body, name=name, grid=grid, in_specs=in_specs, out_specs=out_specs,
                              out_shape=out_shape, scratch_shapes=scratch_shapes, compiler_params=_params())(*operands)
        return list(outs), []
    arrs, modes = carry
    n, n_in, n_out, n_scr = len(arrs), len(in_specs), len(out_specs), len(scratch_shapes)

    def wrapped(*refs):
        ins, c_in = refs[:n_in], refs[n_in:n_in + n]
        outs, c_out = refs[n_in + n:n_in + n + n_out], refs[n_in + n + n_out:n_in + 2 * n + n_out]
        rest = refs[n_in + 2 * n + n_out:]
        scr, sems = rest[:n_scr], rest[n_scr:]
        first, last = None, None
        for d, size in enumerate(grid):
            f, e = pl.program_id(d) == 0, pl.program_id(d) == size - 1
            first = f if first is None else first & f
            last = e if last is None else last & e

        @pl.when(first)
        def _():
            _xchg_start(_xchg_copies(c_in, c_out, modes, sems))

        body(*ins, *outs, *scr)

        @pl.when(last)
        def _():
            _xchg_wait(_xchg_copies(c_in, c_out, modes, sems))

    outs = pl.pallas_call(
        wrapped, name=name, grid=grid, in_specs=list(in_specs) + [HBM_SPEC] * n,
        out_specs=out_specs + [HBM_SPEC] * n, out_shape=out_shape + _xchg_out_shapes(arrs, modes),
        scratch_shapes=scratch_shapes + _xchg_sems(n), compiler_params=_params())(*operands, *arrs)
    return list(outs[:n_out]), list(outs[n_out:])


def _row_specs(r, n_lat, tm):
    nbl = n_lat // tm
    row = pl.BlockSpec((tm, D_MODEL), lambda i: (i, 0))
    mod = pl.BlockSpec((None, 8, D_MODEL), lambda i: (i // nbl, 0, 0))
    vec = pl.BlockSpec((1, D_MODEL), lambda i: (0, 0))
    return nbl, row, mod, vec


def _rows(vals):
    width = [v for v in vals if v is not None][0].shape[1]
    return jnp.concatenate([jnp.zeros((1, width), F32) if v is None else v for v in vals], axis=0)


def _norm_fwd(xs, g, mods, n_lat, sh, sc, name, res=None, carry=None):
    r = xs.shape[0]
    tm = ROW_TILE
    nbl, row, mod, vec = _row_specs(r, n_lat, tm)

    def norm(x, g_ref, m_ref, h_ref):
        rs = lax.rsqrt(jnp.mean(x * x, axis=-1, keepdims=True) + EPS)
        m = m_ref[...]
        h_ref[...] = ((x * rs * g_ref[...]) * (1.0 + m[sc:sc + 1]) + m[sh:sh + 1]).astype(BF16)

    if res is None:
        def body(x_ref, g_ref, m_ref, h_ref):
            norm(x_ref[...], g_ref, m_ref, h_ref)
        outs, carried = _pcall(body, (xs, g, mods), name=name, grid=(r // tm,), in_specs=[row, vec, mod],
                               out_specs=[row], out_shape=[jax.ShapeDtypeStruct((r, D_MODEL), BF16)], carry=carry)
        return outs[0] if carry is None else (outs[0], carried)

    o, mods_res, gt = res

    def body(x_ref, o_ref, mr_ref, g_ref, m_ref, x1_ref, h_ref):
        x = x_ref[...] + mr_ref[...][gt:gt + 1] * o_ref[...]
        x1_ref[...] = x
        norm(x, g_ref, m_ref, h_ref)

    return pl.pallas_call(
        body, name=name, grid=(r // tm,), in_specs=[row, row, mod, vec, mod], out_specs=[row, row],
        out_shape=[jax.ShapeDtypeStruct((r, D_MODEL), F32), jax.ShapeDtypeStruct((r, D_MODEL), BF16)],
        compiler_params=_params())(xs, o, mods_res, g, mods)


def _rms_bwd(x, g, dy):
    rs = lax.rsqrt(jnp.mean(x * x, axis=-1, keepdims=True) + EPS)
    xh = x * rs
    dxh = dy * g
    dx = rs * (dxh - xh * jnp.mean(dxh * xh, axis=-1, keepdims=True))
    return dx, dy * xh, xh


def _acc_specs(nbl):
    acc_all = pl.BlockSpec((8, D_MODEL), lambda i: (0, 0))
    acc_stream = pl.BlockSpec((None, 8, D_MODEL), lambda i: (i // nbl, 0, 0))
    return acc_all, acc_stream


def _loss_bwd(xs, o, mods, g_final, target, n_lat, name):
    r = xs.shape[0]
    tm = ROW_TILE
    nbl, row, mod, vec = _row_specs(r, n_lat, tm)
    acc_all, acc_stream = _acc_specs(nbl)
    tgt = pl.BlockSpec((tm, D_MODEL), lambda i: (jnp.minimum(i, nbl - 1), 0))

    def body(x_ref, o_ref, m_ref, g_ref, t_ref, dx_ref, do_ref, acc_ref, accs_ref):
        i = pl.program_id(0)
        lat = i < nbl
        gate = m_ref[...][5:6]
        o_val = o_ref[...]
        x = x_ref[...] + gate * o_val
        g = g_ref[...]
        rs = lax.rsqrt(jnp.mean(x * x, axis=-1, keepdims=True) + EPS)
        y = x * rs * g
        err = jnp.where(lat, y - t_ref[...], 0.0)
        loss = 0.5 * jnp.sum(jnp.mean(err * err, axis=-1, keepdims=True), axis=0, keepdims=True)
        dy = err * (1.0 / D_MODEL)
        dx, dg_rows, _ = _rms_bwd(x, g, dy)
        dx_ref[...] = dx
        do_ref[...] = (gate * dx).astype(BF16)

        @pl.when(i == 0)
        def _():
            acc_ref[...] = jnp.zeros_like(acc_ref)

        @pl.when((i == 0) | (i == nbl))
        def _():
            accs_ref[...] = jnp.zeros_like(accs_ref)

        acc_ref[...] += _rows([jnp.sum(dg_rows, axis=0, keepdims=True), jnp.broadcast_to(loss, (1, D_MODEL))]
                              + [None] * 6)
        accs_ref[...] += _rows([None, None, jnp.sum(dx * o_val, axis=0, keepdims=True)] + [None] * 5)

    return pl.pallas_call(
        body, name=name, grid=(r // tm,), in_specs=[row, row, mod, vec, tgt],
        out_specs=[row, row, acc_all, acc_stream],
        out_shape=[jax.ShapeDtypeStruct((r, D_MODEL), F32), jax.ShapeDtypeStruct((r, D_MODEL), BF16),
                   jax.ShapeDtypeStruct((8, D_MODEL), F32), jax.ShapeDtypeStruct((2, 8, D_MODEL), F32)],
        compiler_params=_params())(xs, o, mods, g_final, target)


def _norm_bwd(xs, g, mods, dh, dx_in, n_lat, sh, sc, name, res=None):
    r = xs.shape[0]
    tm = ROW_TILE
    nbl, row, mod, vec = _row_specs(r, n_lat, tm)
    acc_all, acc_stream = _acc_specs(nbl)
    has_res = res is not None

    def body(*refs):
        if has_res:
            x_ref, g_ref, m_ref, dh_ref, dxi_ref, o_ref, mr_ref, dx_ref, do_ref, acc_ref, accs_ref = refs
        else:
            x_ref, g_ref, m_ref, dh_ref, dxi_ref, dx_ref, acc_ref, accs_ref = refs
        i = pl.program_id(0)
        x, g, m, dhv = x_ref[...], g_ref[...], m_ref[...], dh_ref[...].astype(F32)
        dy = dhv * (1.0 + m[sc:sc + 1])
        dxn, dg_rows, xh = _rms_bwd(x, g, dy)
        dx = dxi_ref[...] + dxn
        if has_res:
            dx_ref[...] = dx
        else:
            @pl.when(i < nbl)
            def _():
                dx_ref[...] = dx
        d_gate = None
        if has_res:
            o_val = o_ref[...]
            do_ref[...] = (mr_ref[...][res[2]:res[2] + 1] * dx).astype(BF16)
            d_gate = jnp.sum(dx * o_val, axis=0, keepdims=True)

        @pl.when(i == 0)
        def _():
            acc_ref[...] = jnp.zeros_like(acc_ref)

        @pl.when((i == 0) | (i == nbl))
        def _():
            accs_ref[...] = jnp.zeros_like(accs_ref)

        acc_ref[...] += _rows([jnp.sum(dg_rows, axis=0, keepdims=True)] + [None] * 7)
        accs_ref[...] += _rows([jnp.sum(dhv, axis=0, keepdims=True),
                                jnp.sum(dhv * (xh * g), axis=0, keepdims=True), d_gate] + [None] * 5)

    ins = [xs, g, mods, dh, dx_in]
    in_specs = [row, vec, mod, row, row]
    if has_res:
        out_specs, out_shape = [row], [jax.ShapeDtypeStruct((r, D_MODEL), F32)]
    else:
        out_specs = [pl.BlockSpec((tm, D_MODEL), lambda i: (jnp.minimum(i, nbl - 1), 0))]
        out_shape = [jax.ShapeDtypeStruct((n_lat, D_MODEL), F32)]
    if has_res:
        ins += [res[0], res[1]]
        in_specs += [row, mod]
        out_specs.append(row)
        out_shape.append(jax.ShapeDtypeStruct((r, D_MODEL), BF16))
    out_specs += [acc_all, acc_stream]
    out_shape += [jax.ShapeDtypeStruct((8, D_MODEL), F32), jax.ShapeDtypeStruct((2, 8, D_MODEL), F32)]
    return pl.pallas_call(body, name=name, grid=(r // tm,), in_specs=in_specs, out_specs=out_specs,
                          out_shape=out_shape, compiler_params=_params())(*ins)


def _rotate(t, cos, sin):
    width = t.shape[1]
    reps = width // LANES
    lane = lax.broadcasted_iota(jnp.int32, (1, width), 1)
    first = (lane % HEAD_DIM) < (HEAD_DIM // 2)
    swapped = jnp.where(first, pltpu.roll(t, width - HEAD_DIM // 2, 1), pltpu.roll(t, HEAD_DIM // 2, 1))
    return t * jnp.tile(cos, (1, reps)) + swapped * jnp.tile(sin, (1, reps))


def _rope_fwd(z, cos, sin, name):
    r = z.shape[0]
    tm = ROW_TILE
    kvw = 2 * N_KV_HEADS * HEAD_DIM
    tab = pl.BlockSpec((tm, LANES), lambda i: (i, 0))

    def body(q_ref, kv_ref, c_ref, s_ref, o_ref):
        c, s = c_ref[...], s_ref[...]
        kv = kv_ref[...]
        o_ref[:, :D_MODEL] = (_rotate(q_ref[...].astype(F32), c, s) * (HEAD_DIM ** -0.5)).astype(BF16)
        o_ref[:, D_MODEL:D_MODEL + kvw // 2] = _rotate(kv[:, :kvw // 2].astype(F32), c, s).astype(BF16)
        o_ref[:, D_MODEL + kvw // 2:] = kv[:, kvw // 2:]

    return pl.pallas_call(
        body, name=name, grid=(r // tm,),
        in_specs=[pl.BlockSpec((tm, D_MODEL), lambda i: (i, P_Q // D_MODEL)),
                  pl.BlockSpec((tm, kvw), lambda i: (i, P_KV // kvw)), tab, tab],
        out_specs=pl.BlockSpec((tm, D_MODEL + kvw), lambda i: (i, 0)),
        out_shape=jax.ShapeDtypeStruct((r, D_MODEL + kvw), BF16), compiler_params=_params())(z, z, cos, sin)


def _rope_bwd(dq, dk, dv, cos, sin, name):
    r = dq.shape[0]
    tm = ROW_TILE
    kw = N_KV_HEADS * HEAD_DIM
    tab = pl.BlockSpec((tm, LANES), lambda i: (i, 0))

    def body(dq_ref, dk_ref, dv_ref, c_ref, s_ref, o_ref):
        c, s = c_ref[...], -s_ref[...]
        o_ref[:, :D_MODEL] = (_rotate(dq_ref[...].astype(F32), c, s) * (HEAD_DIM ** -0.5)).astype(BF16)
        o_ref[:, D_MODEL:D_MODEL + kw] = _rotate(dk_ref[...], c, s).astype(BF16)
        o_ref[:, D_MODEL + kw:] = dv_ref[...].astype(BF16)

    return pl.pallas_call(
        body, name=name, grid=(r // tm,),
        in_specs=[pl.BlockSpec((tm, D_MODEL), lambda i: (i, 0)), pl.BlockSpec((tm, kw), lambda i: (i, 0)),
                  pl.BlockSpec((tm, kw), lambda i: (i, 0)), tab, tab],
        out_specs=pl.BlockSpec((tm, D_MODEL + 2 * kw), lambda i: (i, 0)),
        out_shape=jax.ShapeDtypeStruct((r, D_MODEL + 2 * kw), BF16), compiler_params=_params())(dq, dk, dv, cos, sin)


def _attn_setup(i, n_lat, m_ctx, nbl, k_ref, v_ref):
    start = pl.multiple_of(jnp.clip((i - 1) * BLK, 0, n_lat - 3 * BLK), BLK)
    nkeys = 3 * BLK + m_ctx
    rows = lax.broadcasted_iota(jnp.int32, (4 * BLK, nkeys), 0)
    cols = lax.broadcasted_iota(jnp.int32, (4 * BLK, nkeys), 1)
    qpos = i * BLK + (rows & (BLK - 1))
    seen = (cols >= 3 * BLK) | ((jnp.abs(start + cols - qpos) <= WINDOW) & (i < nbl))
    mask = jnp.where(seen, 0.0, NEG)
    kblk = jnp.concatenate([k_ref[pl.ds(start, 3 * BLK), :], k_ref[pl.ds(n_lat, m_ctx), :]], axis=0)
    vblk = jnp.concatenate([v_ref[pl.ds(start, 3 * BLK), :], v_ref[pl.ds(n_lat, m_ctx), :]], axis=0)
    lo = lax.broadcasted_iota(jnp.int32, (1, LANES), 1) < HEAD_DIM
    return start, mask, kblk, vblk, lo


def _stack_heads(ref, kh, lo):
    a = ref[:, (2 * kh) * LANES:(2 * kh + 1) * LANES]
    b = ref[:, (2 * kh + 1) * LANES:(2 * kh + 2) * LANES]
    z = jnp.zeros_like(a)
    return jnp.concatenate([jnp.where(lo, a, z), jnp.where(lo, z, a), jnp.where(lo, b, z), jnp.where(lo, z, b)],
                           axis=0)


def _kv_variants(blk, rolled, kh, lo):
    z = jnp.zeros_like(blk)
    if kh == 0:
        return jnp.where(lo, blk, rolled), jnp.where(lo, blk, z), jnp.where(lo, z, rolled)
    return jnp.where(lo, rolled, blk), jnp.where(lo, rolled, z), jnp.where(lo, z, blk)


NN = (((1,), (0,)), ((), ()))
NT = (((1,), (1,)), ((), ()))
TN = (((0,), (0,)), ((), ()))


def _scores(qst, k2, mask, sink_ref, p, kh):
    s = lax.dot_general(qst, k2, NT, preferred_element_type=F32) + mask
    snk = jnp.concatenate([jnp.full((BLK, 1), sink_ref[p * 8 + kh * 4 + g], F32) for g in range(4)], axis=0)
    return s, snk


def _attn_fwd(qkv, sink, n_lat, m_ctx, name, carry=None):
    r = qkv.shape[0]
    nb, nbl = r // BLK, n_lat // BLK
    kcol = D_MODEL // LANES

    def body(sink_ref, q_ref, k_ref, v_ref, o_ref, lse_ref):
        p, i = pl.program_id(0), pl.program_id(1)
        _, mask, kblk, vblk, lo = _attn_setup(i, n_lat, m_ctx, nbl, k_ref, v_ref)
        kr, vr = pltpu.roll(kblk, HEAD_DIM, 1), pltpu.roll(vblk, HEAD_DIM, 1)
        for kh in range(2):
            k2, _, _ = _kv_variants(kblk, kr, kh, lo)
            _, vlo, vhi = _kv_variants(vblk, vr, kh, lo)
            qst = _stack_heads(q_ref, kh, lo)
            s, snk = _scores(qst, k2, mask, sink_ref, p, kh)
            mx = jnp.maximum(jnp.max(s, axis=-1, keepdims=True), snk)
            pe = jnp.exp(s - mx)
            den = jnp.sum(pe, axis=-1, keepdims=True) + jnp.exp(snk - mx)
            inv = 1.0 / den
            pb = pe.astype(BF16)
            for jp in range(2):
                r0 = 2 * jp * BLK
                pair = (jnp.dot(pb[r0:r0 + BLK], vlo, preferred_element_type=F32)
                        + jnp.dot(pb[r0 + BLK:r0 + 2 * BLK], vhi, preferred_element_type=F32))
                pair = pair * jnp.where(lo, inv[r0:r0 + BLK], inv[r0 + BLK:r0 + 2 * BLK])
                o_ref[:, (2 * kh + jp) * LANES:(2 * kh + jp + 1) * LANES] = pair.astype(BF16)
            lse = mx + jnp.log(den)
            for g in range(4):
                lse_ref[:, kh * 4 + g:kh * 4 + g + 1] = lse[g * BLK:(g + 1) * BLK]

    outs, carried = _pcall(
        body, (sink, qkv, qkv, qkv), name=name, grid=(2, nb),
        in_specs=[SMEM_SPEC,
                  pl.BlockSpec((BLK, 4 * LANES), lambda p, i: (i, p)),
                  pl.BlockSpec((r, LANES), lambda p, i: (0, kcol + p)),
                  pl.BlockSpec((r, LANES), lambda p, i: (0, kcol + 2 + p))],
        out_specs=[pl.BlockSpec((BLK, 4 * LANES), lambda p, i: (i, p)),
                   pl.BlockSpec((None, BLK, 8), lambda p, i: (p, i, 0))],
        out_shape=[jax.ShapeDtypeStruct((r, D_MODEL), BF16), jax.ShapeDtypeStruct((2, r, 8), F32)], carry=carry)
    return outs if carry is None else (outs, carried)


def _attn_bwd(qkv, sink, o, do, lse, n_lat, m_ctx, name, carry=None):
    r = qkv.shape[0]
    nb, nbl = r // BLK, n_lat // BLK
    kcol = D_MODEL // LANES

    def body(sink_ref, q_ref, k_ref, v_ref, o_ref, do_ref, lse_ref, dq_ref, dk_ref, dv_ref, ds_ref):
        p, i = pl.program_id(0), pl.program_id(1)

        @pl.when(i == 0)
        def _():
            dk_ref[...] = jnp.zeros_like(dk_ref)
            dv_ref[...] = jnp.zeros_like(dv_ref)
            ds_ref[...] = jnp.zeros_like(ds_ref)

        start, mask, kblk, vblk, lo = _attn_setup(i, n_lat, m_ctx, nbl, k_ref, v_ref)
        kr, vr = pltpu.roll(kblk, HEAD_DIM, 1), pltpu.roll(vblk, HEAD_DIM, 1)
        lane = lax.broadcasted_iota(jnp.int32, (1, LANES), 1)
        dks, dvs = [], []
        for kh in range(2):
            k2, klo, khi = _kv_variants(kblk, kr, kh, lo)
            v2, _, _ = _kv_variants(vblk, vr, kh, lo)
            qst = _stack_heads(q_ref, kh, lo)
            dost = _stack_heads(do_ref, kh, lo)
            s, snk = _scores(qst, k2, mask, sink_ref, p, kh)
            lse4 = jnp.concatenate([lse_ref[:, kh * 4 + g:kh * 4 + g + 1] for g in range(4)], axis=0)
            pe = jnp.exp(s - lse4)
            dp = lax.dot_general(dost, v2, NT, preferred_element_type=F32)
            deltas = []
            for jp in range(2):
                cols = slice((2 * kh + jp) * LANES, (2 * kh + jp + 1) * LANES)
                prod = do_ref[:, cols].astype(F32) * o_ref[:, cols].astype(F32)
                deltas.append(jnp.sum(jnp.where(lo, prod, 0.0), axis=-1, keepdims=True))
                deltas.append(jnp.sum(jnp.where(lo, 0.0, prod), axis=-1, keepdims=True))
            delta = jnp.concatenate(deltas, axis=0)
            dsc = pe * (dp - delta)
            dsb, pb = dsc.astype(BF16), pe.astype(BF16)
            for jp in range(2):
                r0 = 2 * jp * BLK
                dq_ref[:, (2 * kh + jp) * LANES:(2 * kh + jp + 1) * LANES] = (
                    jnp.dot(dsb[r0:r0 + BLK], klo, preferred_element_type=F32)
                    + jnp.dot(dsb[r0 + BLK:r0 + 2 * BLK], khi, preferred_element_type=F32)).astype(BF16)
            dkf = lax.dot_general(dsb, qst, TN, preferred_element_type=F32)
            dvf = lax.dot_general(pb, dost, TN, preferred_element_type=F32)
            dks.append(dkf + pltpu.roll(dkf, HEAD_DIM, 1))
            dvs.append(dvf + pltpu.roll(dvf, HEAD_DIM, 1))
            contrib = -jnp.exp(snk - lse4) * delta
            for g in range(4):
                tot = jnp.sum(contrib[g * BLK:(g + 1) * BLK], axis=0, keepdims=True)
                ds_ref[0:1, :] += jnp.where(lane == kh * 4 + g, tot, 0.0)
        dk_blk = jnp.where(lo, dks[0], dks[1])
        dv_blk = jnp.where(lo, dvs[0], dvs[1])
        dk_ref[pl.ds(start, 3 * BLK), :] += dk_blk[:3 * BLK]
        dk_ref[pl.ds(n_lat, m_ctx), :] += dk_blk[3 * BLK:]
        dv_ref[pl.ds(start, 3 * BLK), :] += dv_blk[:3 * BLK]
        dv_ref[pl.ds(n_lat, m_ctx), :] += dv_blk[3 * BLK:]

    qspec = pl.BlockSpec((BLK, 4 * LANES), lambda p, i: (i, p))
    outs, carried = _pcall(
        body, (sink, qkv, qkv, qkv, o, do, lse), name=name, grid=(2, nb),
        in_specs=[SMEM_SPEC, qspec,
                  pl.BlockSpec((r, LANES), lambda p, i: (0, kcol + p)),
                  pl.BlockSpec((r, LANES), lambda p, i: (0, kcol + 2 + p)),
                  qspec, qspec, pl.BlockSpec((None, BLK, 8), lambda p, i: (p, i, 0))],
        out_specs=[qspec, pl.BlockSpec((r, LANES), lambda p, i: (0, p)), pl.BlockSpec((r, LANES), lambda p, i: (0, p)),
                   pl.BlockSpec((None, 8, LANES), lambda p, i: (p, 0, 0))],
        out_shape=[jax.ShapeDtypeStruct((r, D_MODEL), BF16), jax.ShapeDtypeStruct((r, 2 * LANES), F32),
                   jax.ShapeDtypeStruct((r, 2 * LANES), F32), jax.ShapeDtypeStruct((2, 8, LANES), F32)], carry=carry)
    return outs if carry is None else (outs, carried)


def _gating_parts(z_ref, gv_ref):
    za = z_ref[...].astype(F32)
    zg = jax.nn.gelu(za)
    u, v = zg[:, :D_MODEL], zg[:, D_MODEL:]
    rs = lax.rsqrt(jnp.mean(v * v, axis=-1, keepdims=True) + EPS)
    return za, u, v, rs, v * rs * gv_ref[...]


def _mix(w_ref, vals):
    vb = vals.astype(BF16)
    return jnp.concatenate(
        [jnp.dot(w_ref[g], vb[:, g * LANES:(g + 1) * LANES], preferred_element_type=F32) for g in range(A_GROUPS)],
        axis=1)


def _gating_fwd(z, ws, bias, g_v, name):
    r = z.shape[0]

    def body(z_ref, w_ref, b_ref, gv_ref, y_ref):
        _, u, _, _, vn = _gating_parts(z_ref, gv_ref)
        y_ref[...] = (u * (_mix(w_ref, vn) + b_ref[...])).astype(BF16)

    return pl.pallas_call(
        body, name=name, grid=(r // CHUNK,),
        in_specs=[pl.BlockSpec((CHUNK, 2 * D_MODEL), lambda i: (i, P_A // (2 * D_MODEL))),
                  pl.BlockSpec((A_GROUPS, CHUNK, CHUNK), lambda i: (0, 0, 0)),
                  pl.BlockSpec((CHUNK, D_MODEL), lambda i: (0, 0)),
                  pl.BlockSpec((1, D_MODEL), lambda i: (0, 0))],
        out_specs=pl.BlockSpec((CHUNK, D_MODEL), lambda i: (i, 0)),
        out_shape=jax.ShapeDtypeStruct((r, D_MODEL), BF16), compiler_params=_params())(z, ws, bias, g_v)


def _gating_bwd(z, ws, ws_t, bias, g_v, dy, name, carry=None):
    r = z.shape[0]

    def body(z_ref, w_ref, wt_ref, b_ref, gv_ref, dy_ref, dz_ref, dw_ref, db_ref, dg_ref):
        i = pl.program_id(0)

        @pl.when(i == 0)
        def _():
            dw_ref[...] = jnp.zeros_like(dw_ref)
            db_ref[...] = jnp.zeros_like(db_ref)
            dg_ref[...] = jnp.zeros_like(dg_ref)

        za, u, v, rs, vn = _gating_parts(z_ref, gv_ref)
        dyv = dy_ref[...].astype(F32)
        du = dyv * (_mix(w_ref, vn) + b_ref[...])
        dmixed = dyv * u
        dvn = _mix(wt_ref, dmixed)
        dmb, vnb = dmixed.astype(BF16), vn.astype(BF16)
        for g in range(A_GROUPS):
            cols = slice(g * LANES, (g + 1) * LANES)
            dw_ref[g] += lax.dot_general(dmb[:, cols], vnb[:, cols], NT, preferred_element_type=F32)
            db_ref[:, g:g + 1] += jnp.sum(dmixed[:, cols], axis=-1, keepdims=True)
        gv = gv_ref[...]
        vh = v * rs
        dg_ref[0:1, :] += jnp.sum(dvn * vh, axis=0, keepdims=True)
        dvh = dvn * gv
        dv = rs * (dvh - vh * jnp.mean(dvh * vh, axis=-1, keepdims=True))
        _, vjp = jax.vjp(jax.nn.gelu, za)
        dz_ref[...] = vjp(jnp.concatenate([du, dv], axis=1))[0].astype(BF16)

    wspec = pl.BlockSpec((A_GROUPS, CHUNK, CHUNK), lambda i: (0, 0, 0))
    outs, carried = _pcall(
        body, (z, ws, ws_t, bias, g_v, dy), name=name, grid=(r // CHUNK,),
        in_specs=[pl.BlockSpec((CHUNK, 2 * D_MODEL), lambda i: (i, P_A // (2 * D_MODEL))), wspec, wspec,
                  pl.BlockSpec((CHUNK, D_MODEL), lambda i: (0, 0)), pl.BlockSpec((1, D_MODEL), lambda i: (0, 0)),
                  pl.BlockSpec((CHUNK, D_MODEL), lambda i: (i, 0))],
        out_specs=[pl.BlockSpec((CHUNK, 2 * D_MODEL), lambda i: (i, 0)), wspec,
                   pl.BlockSpec((CHUNK, A_GROUPS), lambda i: (0, 0)), pl.BlockSpec((8, D_MODEL), lambda i: (0, 0))],
        out_shape=[jax.ShapeDtypeStruct((r, 2 * D_MODEL), BF16), jax.ShapeDtypeStruct((A_GROUPS, CHUNK, CHUNK), F32),
                   jax.ShapeDtypeStruct((CHUNK, A_GROUPS), F32), jax.ShapeDtypeStruct((8, D_MODEL), F32)], carry=carry)
    return outs if carry is None else (outs, carried)


def _scr_rows(j, n_lat):
    s = pl.multiple_of(j * CONV_CHUNK, CONV_CHUNK)
    shift = jnp.where(j >= n_lat // CONV_CHUNK, 2 * PAD, PAD)
    return s, pl.multiple_of(s + shift, PAD)


def _taps(scr, j, n_lat):
    c = CONV_CHUNK
    s, at = _scr_rows(j, n_lat)
    ext = scr[pl.ds(pl.multiple_of(at - PAD, PAD), c + 2 * PAD), :]
    xm = pltpu.roll(ext, 1, 0)[PAD:PAD + c]
    xp = pltpu.roll(ext, c + 2 * PAD - 1, 0)[PAD:PAD + c]
    return s, xm, ext[PAD:PAD + c], xp


def _zero_pads(scr, r, n_lat):
    for at in (0, PAD + n_lat, 2 * PAD + r):
        scr[pl.ds(at, PAD), :] = jnp.zeros((PAD, LANES), F32)


def _col(arr_cols, c0):
    return pl.BlockSpec((arr_cols, LANES), lambda c: (0, c0 + c))


def _ffn_conv_fwd(up, w, n_lat, name):
    r = up.shape[0]
    nct = D_FF // LANES
    nchunk = r // CONV_CHUNK

    def body(a_ref, g_ref, w_ref, f_ref, scr):
        _zero_pads(scr, r, n_lat)

        def fill(j, _):
            s, at = _scr_rows(j, n_lat)
            scr[pl.ds(at, CONV_CHUNK), :] = a_ref[pl.ds(s, CONV_CHUNK), :].astype(F32)
            return 0
        lax.fori_loop(0, nchunk, fill, 0)
        wv = w_ref[...]

        def step(j, _):
            s, xm, x0, xp = _taps(scr, j, n_lat)
            ca = wv[0:1] * xm + wv[1:2] * x0 + wv[2:3] * xp
            gv = g_ref[pl.ds(s, CONV_CHUNK), :].astype(F32)
            f_ref[pl.ds(s, CONV_CHUNK), :] = (ca * _sigmoid(ca) * gv).astype(BF16)
            return 0
        lax.fori_loop(0, nchunk, step, 0)

    return pl.pallas_call(
        body, name=name, grid=(nct,),
        in_specs=[_col(r, 0), _col(r, nct), _col(3, 0)],
        out_specs=_col(r, 0), out_shape=jax.ShapeDtypeStruct((r, D_FF), BF16),
        scratch_shapes=[pltpu.VMEM((r + 3 * PAD, LANES), F32)], compiler_params=_params())(up, up, w)


def _ffn_conv_bwd(up, w, df, n_lat, name, carry=None):
    r = up.shape[0]
    nct = D_FF // LANES
    nchunk = r // CONV_CHUNK

    def body(a_ref, g_ref, w_ref, df_ref, dup_ref, dw_ref, scr, scr2):
        _zero_pads(scr, r, n_lat)
        _zero_pads(scr2, r, n_lat)

        def fill(j, _):
            s, at = _scr_rows(j, n_lat)
            scr[pl.ds(at, CONV_CHUNK), :] = a_ref[pl.ds(s, CONV_CHUNK), :].astype(F32)
            return 0
        lax.fori_loop(0, nchunk, fill, 0)
        wv = w_ref[...]

        def first(j, carry):
            s, xm, x0, xp = _taps(scr, j, n_lat)
            ca = wv[0:1] * xm + wv[1:2] * x0 + wv[2:3] * xp
            sg = _sigmoid(ca)
            gv = g_ref[pl.ds(s, CONV_CHUNK), :].astype(F32)
            dfv = df_ref[pl.ds(s, CONV_CHUNK), :].astype(F32)
            dup_ref[1, pl.ds(s, CONV_CHUNK), :] = (dfv * ca * sg).astype(BF16)
            dca = dfv * gv * (sg * (1.0 + ca * (1.0 - sg)))
            scr2[pl.ds(_scr_rows(j, n_lat)[1], CONV_CHUNK), :] = dca
            return tuple(cw + jnp.sum(dca * xv, axis=0, keepdims=True) for cw, xv in zip(carry, (xm, x0, xp)))
        zero = jnp.zeros((1, LANES), F32)
        dws = lax.fori_loop(0, nchunk, first, (zero, zero, zero))
        dw_ref[...] = _rows(list(dws) + [None] * 5)

        def second(j, _):
            s, ym, y0, yp = _taps(scr2, j, n_lat)
            dup_ref[0, pl.ds(s, CONV_CHUNK), :] = (wv[0:1] * yp + wv[1:2] * y0 + wv[2:3] * ym).astype(BF16)
            return 0
        lax.fori_loop(0, nchunk, second, 0)

    outs, carried = _pcall(
        body, (up, up, w, df), name=name, grid=(nct,),
        in_specs=[_col(r, 0), _col(r, nct), _col(3, 0), _col(r, 0)],
        out_specs=[pl.BlockSpec((2, r, LANES), lambda c: (0, 0, c)), _col(8, 0)],
        out_shape=[jax.ShapeDtypeStruct((2, r, D_FF), BF16), jax.ShapeDtypeStruct((8, D_FF), F32)],
        scratch_shapes=[pltpu.VMEM((r + 3 * PAD, LANES), F32), pltpu.VMEM((r + 3 * PAD, LANES), F32)], carry=carry)
    return outs if carry is None else (outs, carried)


def _sconv_fwd(z, w, n_lat, name):
    r = z.shape[0]
    nct = D_MODEL // LANES
    nchunk = r // CONV_CHUNK
    c0 = P_B // LANES

    def body(bg_ref, cg_ref, hb_ref, w_ref, y_ref, scr):
        _zero_pads(scr, r, n_lat)

        def fill(j, _):
            s, at = _scr_rows(j, n_lat)
            rows = pl.ds(s, CONV_CHUNK)
            scr[pl.ds(at, CONV_CHUNK), :] = cg_ref[rows, :].astype(F32) * hb_ref[rows, :].astype(F32)
            return 0
        lax.fori_loop(0, nchunk, fill, 0)
        wv = w_ref[...]

        def step(j, _):
            s, xm, x0, xp = _taps(scr, j, n_lat)
            conv = wv[0:1] * xm + wv[1:2] * x0 + wv[2:3] * xp
            y_ref[pl.ds(s, CONV_CHUNK), :] = (bg_ref[pl.ds(s, CONV_CHUNK), :].astype(F32) * conv).astype(BF16)
            return 0
        lax.fori_loop(0, nchunk, step, 0)

    return pl.pallas_call(
        body, name=name, grid=(nct,),
        in_specs=[_col(r, c0), _col(r, c0 + nct), _col(r, c0 + 2 * nct), _col(3, 0)],
        out_specs=_col(r, 0), out_shape=jax.ShapeDtypeStruct((r, D_MODEL), BF16),
        scratch_shapes=[pltpu.VMEM((r + 3 * PAD, LANES), F32)], compiler_params=_params())(z, z, z, w)


def _sconv_bwd(z, w, dy, n_lat, name):
    r = z.shape[0]
    nct = D_MODEL // LANES
    nchunk = r // CONV_CHUNK
    c0 = P_B // LANES

    def body(bg_ref, cg_ref, hb_ref, w_ref, dy_ref, dz_ref, dw_ref, scr, scr2):
        _zero_pads(scr, r, n_lat)
        _zero_pads(scr2, r, n_lat)

        def fill(j, _):
            s, at = _scr_rows(j, n_lat)
            rows = pl.ds(s, CONV_CHUNK)
            scr[pl.ds(at, CONV_CHUNK), :] = cg_ref[rows, :].astype(F32) * hb_ref[rows, :].astype(F32)
            return 0
        lax.fori_loop(0, nchunk, fill, 0)
        wv = w_ref[...]

        def first(j, carry):
            s, xm, x0, xp = _taps(scr, j, n_lat)
            rows = pl.ds(s, CONV_CHUNK)
            conv = wv[0:1] * xm + wv[1:2] * x0 + wv[2:3] * xp
            dyv = dy_ref[rows, :].astype(F32)
            dz_ref[0, rows, :] = (dyv * conv).astype(BF16)
            dconv = dyv * bg_ref[rows, :].astype(F32)
            scr2[pl.ds(_scr_rows(j, n_lat)[1], CONV_CHUNK), :] = dconv
            return tuple(cw + jnp.sum(dconv * xv, axis=0, keepdims=True) for cw, xv in zip(carry, (xm, x0, xp)))
        zero = jnp.zeros((1, LANES), F32)
        dws = lax.fori_loop(0, nchunk, first, (zero, zero, zero))
        dw_ref[...] = _rows(list(dws) + [None] * 5)

        def second(j, _):
            s, ym, y0, yp = _taps(scr2, j, n_lat)
            rows = pl.ds(s, CONV_CHUNK)
            dx = wv[0:1] * yp + wv[1:2] * y0 + wv[2:3] * ym
            dz_ref[1, rows, :] = (dx * hb_ref[rows, :].astype(F32)).astype(BF16)
            dz_ref[2, rows, :] = (dx * cg_ref[rows, :].astype(F32)).astype(BF16)
            return 0
        lax.fori_loop(0, nchunk, second, 0)

    return pl.pallas_call(
        body, name=name, grid=(nct,),
        in_specs=[_col(r, c0), _col(r, c0 + nct), _col(r, c0 + 2 * nct), _col(3, 0), _col(r, 0)],
        out_specs=[pl.BlockSpec((3, r, LANES), lambda c: (0, 0, c)), _col(8, 0)],
        out_shape=[jax.ShapeDtypeStruct((3, r, D_MODEL), BF16), jax.ShapeDtypeStruct((8, D_MODEL), F32)],
        scratch_shapes=[pltpu.VMEM((r + 3 * PAD, LANES), F32), pltpu.VMEM((r + 3 * PAD, LANES), F32)],
        compiler_params=_params())(z, z, z, w, dy)


def _merge_proj_out(z, b_gate, ps, w_out, bm, name):
    r = z.shape[0]
    sub = ROW_TILE
    row = pl.BlockSpec((bm, D_MODEL), lambda i: (i, 0))

    def body(zg_ref, b_ref, p0_ref, p1_ref, p2_ref, w_ref, m_ref, o_ref):
        def chunk(c, _):
            rows = pl.ds(pl.multiple_of(c * sub, sub), sub)
            gates = _sigmoid(zg_ref[rows, :].astype(F32) + b_ref[...])
            acc = None
            for t, p_ref in enumerate((p0_ref, p1_ref, p2_ref)):
                term = gates[:, t * D_MODEL:(t + 1) * D_MODEL] * p_ref[rows, :].astype(F32)
                acc = term if acc is None else acc + term
            m_ref[rows, :] = acc.astype(BF16)
            return 0
        lax.fori_loop(0, bm // sub, chunk, 0)
        o_ref[...] = jnp.dot(m_ref[...], w_ref[...], preferred_element_type=F32)

    return pl.pallas_call(
        body, name=name, grid=(r // bm,),
        in_specs=[pl.BlockSpec((bm, 3 * D_MODEL), lambda i: (i, 0)), pl.BlockSpec((1, 3 * D_MODEL), lambda i: (0, 0)),
                  row, row, row, pl.BlockSpec((D_MODEL, D_MODEL), lambda i: (0, 0))],
        out_specs=[row, row],
        out_shape=[jax.ShapeDtypeStruct((r, D_MODEL), BF16), jax.ShapeDtypeStruct((r, D_MODEL), F32)],
        compiler_params=_params())(z, b_gate, *ps, w_out)


def _merge_bwd(z, b_gate, ps, dmerged, name):
    r = z.shape[0]
    tm = ROW_TILE
    row = pl.BlockSpec((tm, D_MODEL), lambda i: (i, 0))
    wide = pl.BlockSpec((tm, 3 * D_MODEL), lambda i: (i, 0))

    def body(zg_ref, b_ref, p0_ref, p1_ref, p2_ref, dm_ref, d0_ref, d1_ref, d2_ref, dz_ref, db_ref):
        @pl.when(pl.program_id(0) == 0)
        def _():
            db_ref[...] = jnp.zeros_like(db_ref)

        gates = _sigmoid(zg_ref[...].astype(F32) + b_ref[...])
        dm = dm_ref[...].astype(F32)
        for t, (p_ref, d_ref) in enumerate(((p0_ref, d0_ref), (p1_ref, d1_ref), (p2_ref, d2_ref))):
            cols = slice(t * D_MODEL, (t + 1) * D_MODEL)
            gt = gates[:, cols]
            d_ref[...] = (dm * gt).astype(BF16)
            dlogit = dm * p_ref[...].astype(F32) * gt * (1.0 - gt)
            dz_ref[:, cols] = dlogit.astype(BF16)
            db_ref[0:1, cols] += jnp.sum(dlogit, axis=0, keepdims=True)

    shp = jax.ShapeDtypeStruct((r, D_MODEL), BF16)
    return pl.pallas_call(
        body, name=name, grid=(r // tm,),
        in_specs=[wide, pl.BlockSpec((1, 3 * D_MODEL), lambda i: (0, 0)), row, row, row, row],
        out_specs=[row, row, row, wide, pl.BlockSpec((8, 3 * D_MODEL), lambda i: (0, 0))],
        out_shape=[shp, shp, shp, jax.ShapeDtypeStruct((r, 3 * D_MODEL), BF16),
                   jax.ShapeDtypeStruct((8, 3 * D_MODEL), F32)],
        compiler_params=_params())(z, b_gate, *ps, dmerged)


def _sum_slots(buf, name):
    s, rows, _ = buf.shape
    whole_bytes = s * rows * LANES * 4
    tr = rows if whole_bytes <= VMEM_LIMIT // 8 else _pick(rows, [512, 256, 128, 64, 32, 16, 8])

    def body(b_ref, o_ref):
        acc = b_ref[0]
        for t in range(1, s):
            acc = acc + b_ref[t]
        o_ref[...] = acc

    return pl.pallas_call(
        body, name=name, grid=(rows // tr,),
        in_specs=[pl.BlockSpec((s, tr, LANES), lambda i: (0, i, 0))],
        out_specs=pl.BlockSpec((tr, LANES), lambda i: (i, 0)),
        out_shape=jax.ShapeDtypeStruct((rows, LANES), F32), compiler_params=_params())(buf)


def _adamw(w, gsrcs, m, v, name, scale=None):
    nl, rows, cols = w.shape
    assert len(gsrcs) == nl
    s = gsrcs[0].shape[0]
    tr = _pick(rows, [304, 352, 256, 128, 64, 32, 16, 8])
    blk = pl.BlockSpec((None, tr, cols), lambda l, i: (l, i, 0))
    c1 = 1.0 / (1.0 - ADAM_B1 ** ADAM_STEP)
    c2 = 1.0 / (1.0 - ADAM_B2 ** ADAM_STEP)

    def gspec(t):
        return pl.BlockSpec((s, tr, cols), lambda l, i: (0, jnp.where(l == t, i, 0), 0))

    def body(*refs):
        w_ref, g_refs, (m_ref, v_ref) = refs[0], refs[1:1 + nl], refs[1 + nl:3 + nl]
        rest = refs[3 + nl:]
        if scale is not None:
            sc_ref, rest = rest[0], rest[1:]
        go_ref, d_ref, mo_ref, vo_ref = rest
        layer = pl.program_id(0)
        g = None
        for t in range(nl):
            gt = g_refs[t][0].astype(F32)
            for q in range(1, s):
                gt = gt + g_refs[t][q].astype(F32)
            g = gt if g is None else jnp.where(layer == t, gt, g)
        if scale is not None:
            g = g * sc_ref[...]
        mn = ADAM_B1 * m_ref[...] + (1.0 - ADAM_B1) * g
        vn = ADAM_B2 * v_ref[...] + (1.0 - ADAM_B2) * (g * g)
        go_ref[...] = g
        mo_ref[...] = mn
        vo_ref[...] = vn
        d_ref[...] = -ADAM_LR * ((mn * c1) / (jnp.sqrt(vn * c2) + ADAM_EPS) + ADAM_WD * w_ref[...])

    shp = jax.ShapeDtypeStruct((nl, rows, cols), F32)
    ins = [w] + list(gsrcs) + [m, v] + ([] if scale is None else [scale])
    return pl.pallas_call(
        body, name=name, grid=(nl, rows // tr),
        in_specs=[blk] + [gspec(t) for t in range(nl)] + [blk, blk] + ([] if scale is None else [blk]),
        out_specs=[blk] * 4, out_shape=[shp] * 4, compiler_params=_params())(*ins)


def _pack(arrs):
    flat = []
    for a in arrs:
        a = a.reshape(-1).astype(F32)
        pad = (-a.shape[0]) % (8 * LANES)
        flat.append(jnp.pad(a, (0, pad)) if pad else a)
    return jnp.concatenate(flat).reshape(-1, LANES)


def _unpack(buf, shapes, lead=()):
    out, row = [], 0
    for shp in shapes:
        n = 1
        for d in shp:
            n *= d
        nrows = -(-n // (8 * LANES)) * 8
        piece = buf[..., row:row + nrows, :].reshape(lead + (nrows * LANES,))[..., :n]
        out.append(piece.reshape(lead + tuple(shp)))
        row += nrows
    return out


def _silu(x):
    return x * jax.nn.sigmoid(x)


def _rope_tables(n_lat, m_ctx):
    pos = jnp.arange(n_lat)
    row = (pos // GRID_W).astype(F32)
    col = (pos % GRID_W).astype(F32)
    half = HEAD_DIM // 2
    inv = ROPE_THETA ** (-jnp.arange(0, half, 2, dtype=F32) / half)
    ang = jnp.concatenate([row[:, None] * inv, col[:, None] * inv], axis=-1)
    cos, sin = jnp.cos(ang), jnp.sin(ang)
    cos2 = jnp.tile(jnp.concatenate([cos, cos], axis=-1), (1, LANES // HEAD_DIM))
    sin2 = jnp.tile(jnp.concatenate([-sin, sin], axis=-1), (1, LANES // HEAD_DIM))
    return (jnp.concatenate([cos2, jnp.ones((m_ctx, LANES), F32)], axis=0),
            jnp.concatenate([sin2, jnp.zeros((m_ctx, LANES), F32)], axis=0))


def _to_slots(full, axis):
    shp = full.shape
    new = shp[:axis] + (N_DEV, shp[axis] // N_DEV) + shp[axis + 1:]
    return jnp.moveaxis(full.reshape(new), axis, 0)


def _from_slots(slots, axis):
    moved = jnp.moveaxis(slots, 0, axis)
    shp = moved.shape
    return moved.reshape(shp[:axis] + (shp[axis] * shp[axis + 1],) + shp[axis + 2:])


def _permute_in(wt, name):
    blk = 512
    segs = [(OFF_G, IN_W), (0, OFF_K), (OFF_A, OFF_B), (OFF_B, OFF_G), (OFF_K, OFF_A)]

    def source(j):
        src, at = 0, 0
        for lo, hi in segs:
            n = (hi - lo) // blk
            src = src + jnp.where((j >= at) & (j < at + n), j - at + lo // blk, 0)
            at += n
        return (src, 0)

    def body(i_ref, o_ref):
        o_ref[...] = i_ref[...]

    return pl.pallas_call(
        body, name=name, grid=(IN_W // blk,), in_specs=[pl.BlockSpec((blk, D_MODEL), source)],
        out_specs=pl.BlockSpec((blk, D_MODEL), lambda j: (j, 0)), out_shape=jax.ShapeDtypeStruct(wt.shape, wt.dtype),
        compiler_params=_params())(wt)


def kernel(x, c, ctx, c_ctx, w_mod, b_mod, g_mix, w_in, b_gate, sink, w_spatial, b_spatial, g_v, w_sconv, w_branch, w_out, g_ffn, w_up, w_fconv, w_down, g_final, loss_target, m_c_ctx, m_w_mod, m_b_mod, m_g_mix, m_w_in, m_b_gate, m_sink, m_w_spatial, m_b_spatial, m_g_v, m_w_sconv, m_w_branch, m_w_out, m_g_ffn, m_w_up, m_w_fconv, m_w_down, m_g_final, v_c_ctx, v_w_mod, v_b_mod, v_g_mix, v_w_in, v_b_gate, v_sink, v_w_spatial, v_b_spatial, v_g_v, v_w_sconv, v_w_branch, v_w_out, v_g_ffn, v_w_up, v_w_fconv, v_w_down, v_g_final):
    n_lat, m_ctx = x.shape[1], ctx.shape[1]
    r = n_lat + m_ctx
    me = 4 * lax.axis_index("x") + 2 * lax.axis_index("y") + lax.axis_index("c")
    mod_w = w_mod.shape[2]
    bm = _pick(r, [1408, 768, 256])
    bm_fused = _pick(r, [768, 256])

    tr = lambda a: jnp.swapaxes(a, -1, -2)
    shard_axis = {"in": 0, "br": 1, "out": 0, "up": 0, "dn": 0}
    shards = {}
    for kind, wt in (("in", tr(w_in)), ("br", w_branch), ("out", w_out), ("up", tr(w_up)), ("dn", w_down)):
        wb = wt.astype(BF16)
        for l in range(DEPTH):
            shards[kind, l] = wb[l]
    full = {}

    def arrive(items, got):
        for key, slots in zip(items, got):
            wfull = _from_slots(slots, shard_axis[key[0]])
            full[key] = _permute_in(wfull, f"permute_in{key[1]}") if key[0] == "in" else wfull

    def gather_of(items):
        return [shards[key] for key in items], ["gather"] * len(items)

    small_shapes = [c.shape, w_sconv.shape, w_fconv.shape]
    (g_small,) = _exchange([_pack([c, w_sconv, w_fconv])], ["gather"], "gather_first")
    c_all, sconv_all, fconv_all = _unpack(g_small, small_shapes, lead=(N_DEV,))
    c_all = c_all.reshape(N_DEV, D_MODEL)
    w_sconv_full = _from_slots(sconv_all, 2)
    w_fconv_full = _from_slots(fconv_all, 2)

    act = jnp.concatenate([_silu(c_all), _silu(c_ctx)[None], jnp.zeros((7, D_MODEL), F32)], axis=0)
    mod_part = jnp.stack([_mm(act, w_mod[l], name=f"mod_fwd{l}", bm=16, bn=mod_w, bk=D_MODEL, out_dtype=F32)
                          for l in range(DEPTH)])
    (mod_all,) = _exchange([mod_part], ["gather"], "gather_mod")
    mod_full = _from_slots(mod_all, 2) + b_mod[:, None, :]
    mods = []
    for l in range(DEPTH):
        mine = lax.dynamic_index_in_dim(mod_full[l], me, axis=0, keepdims=False).reshape(6, D_MODEL)
        theirs = mod_full[l, N_DEV].reshape(6, D_MODEL)
        mods.append(jnp.pad(jnp.stack([mine, theirs]), ((0, 0), (0, 2), (0, 0))))

    cos, sin = _rope_tables(n_lat, m_ctx)
    xs = jnp.concatenate([x[0], ctx[0]], axis=0)
    ws_b = w_spatial.astype(BF16)
    ws_t = jnp.swapaxes(w_spatial, 2, 3).astype(BF16)
    vec = lambda a: a.reshape(1, -1)

    saved = []
    res = None
    for l in range(DEPTH):
        s = {}
        if res is None:
            s["x0"] = xs
            s["h"], got = _norm_fwd(xs, vec(g_mix[l]), mods[l], n_lat, 0, 1, f"norm_mix{l}",
                                    carry=gather_of([("in", l)]))
            arrive([("in", l)], got)
        else:
            s["x0"], s["h"] = _norm_fwd(xs, vec(g_mix[l]), mods[l], n_lat, 0, 1, f"norm_mix{l}", res=res)
        items = [("br", l), ("out", l), ("up", l)]
        s["z"], got = _mm(s["h"], full["in", l], tb=True, name=f"proj_in{l}", bm=bm, bn=_pick(IN_W, [2432, 512]),
                          bk=D_MODEL, out_dtype=BF16, cols_outer=True, carry=gather_of(items))
        arrive(items, got)
        s["qkv"] = _rope_fwd(s["z"], cos, sin, f"rope{l}")
        items = [("dn", l)] + ([("in", l + 1)] if l + 1 < DEPTH else [])
        (s["y0"], s["lse"]), got = _attn_fwd(s["qkv"], sink[l], n_lat, m_ctx, f"attn{l}", carry=gather_of(items))
        arrive(items, got)
        s["bias"] = jnp.repeat(b_spatial[l].T, LANES, axis=1)
        s["y1"] = _gating_fwd(s["z"], ws_b[l], s["bias"], vec(g_v[l]), f"gating{l}")
        s["y2"] = _sconv_fwd(s["z"], w_sconv_full[l], n_lat, f"sconv{l}")
        s["p"] = [_mm(s[f"y{t}"], full["br", l], b_lead=t, name=f"branch{l}_{t}", bm=bm, bn=D_MODEL, bk=D_MODEL,
                      out_dtype=BF16) for t in range(3)]
        s["merged"], s["o"] = _merge_proj_out(s["z"], vec(b_gate[l]), s["p"], full["out", l], bm_fused,
                                              f"proj_out{l}")
        s["x1"], s["h2"] = _norm_fwd(s["x0"], vec(g_ffn[l]), mods[l], n_lat, 3, 4, f"norm_ffn{l}",
                                     res=(s["o"], mods[l], 2))
        s["up"] = _mm(s["h2"], full["up", l], tb=True, name=f"ffn_up{l}", bm=bm, bn=_pick(2 * D_FF, [1408]),
                      bk=D_MODEL, out_dtype=BF16, cols_outer=True)
        s["f"] = _ffn_conv_fwd(s["up"], w_fconv_full[l], n_lat, f"ffn_conv{l}")
        s["dd"] = _mm(s["f"], full["dn", l], name=f"ffn_down{l}", bm=bm, bn=D_MODEL, bk=D_FF, out_dtype=F32)
        saved.append(s)
        xs, res = s["x1"], (s["dd"], mods[l], 5)

    top = saved[DEPTH - 1]
    dxs, d_dd, acc_final, accs_top = _loss_bwd(top["x1"], top["dd"], mods[DEPTH - 1], vec(g_final), loss_target[0],
                                               n_lat, "loss")
    loss_part = acc_final[1, 0]
    dg_final = acc_final[0]
    dmods = [None] * DEPTH
    gate2 = accs_top[:, 2]
    grads = {k: [None] * DEPTH for k in ("g_mix", "g_ffn", "g_v", "b_gate", "sink", "w_spatial", "b_spatial",
                                         "w_sconv", "w_fconv", "w_in", "w_branch", "w_out", "w_up", "w_down")}
    bk_r = _pick(r, [2816, 768, 256])
    small_names = ["g_ffn", "g_v", "b_gate", "sink", "w_spatial", "b_spatial", "w_sconv", "w_fconv"]
    recv = {}
    small_recv, small_shapes_of = {}, {}

    def small_pack(l):
        arrs = [grads[k][l] for k in small_names]
        if l > 0:
            arrs.append(grads["g_mix"][l])
        if l == DEPTH - 1:
            arrs += [loss_part.reshape(1), dg_final]
        small_shapes_of[l] = [a.shape for a in arrs]
        return _pack(arrs)

    for l in reversed(range(DEPTH)):
        s = saved[l]
        df = _mm(d_dd, full["dn", l], tb=True, name=f"d_ffn_down{l}", bm=bm, bn=_pick(D_FF, [1408]), bk=D_MODEL,
                 out_dtype=BF16)
        grads["w_down"][l] = _mm(s["f"], d_dd, ta=True, name=f"g_ffn_down{l}", bm=_pick(D_FF, [1408]), bn=D_MODEL,
                                 bk=bk_r, out_dtype=BF16)
        arrs, modes = [_to_slots(grads["w_down"][l], 0)], ["a2a"]
        if l + 1 < DEPTH:
            arrs.append(small_pack(l + 1))
            modes.append("gather")
        (dup, dwf), got = _ffn_conv_bwd(s["up"], w_fconv_full[l], df, n_lat, f"d_ffn_conv{l}", carry=(arrs, modes))
        recv["dn", l] = got[0]
        if l + 1 < DEPTH:
            small_recv[l + 1] = got[1]
        grads["w_fconv"][l] = dwf[:3]
        dh2 = _mm_pieces([(dup, "step", 0, 2, 0, D_FF)], full["up", l], w_t=True, name=f"d_ffn_up{l}", bm=bm,
                         out_dtype=BF16)
        kb = 1408
        nbh = D_FF // kb
        grads["w_up"][l] = _mm(
            dup, s["h2"], ta=True, name=f"g_ffn_up{l}", bm=kb, bn=D_MODEL, bk=bk_r, out_dtype=BF16,
            a_spec=((None, bk_r, kb), lambda i, j, k: (i // nbh, k, i % nbh), 2 * D_FF))
        dx1, d_o, acc, accs = _norm_bwd(s["x1"], vec(g_ffn[l]), mods[l], dh2, dxs, n_lat, 3, 4, f"d_norm_ffn{l}",
                                        res=(s["o"], mods[l], 2))
        grads["g_ffn"][l] = acc[0]
        shift2, scale2, gate1 = accs[:, 0], accs[:, 1], accs[:, 2]
        dmerged = _mm(d_o, full["out", l], tb=True, name=f"d_proj_out{l}", bm=bm, bn=D_MODEL, bk=D_MODEL,
                      out_dtype=BF16)
        grads["w_out"][l] = _mm(s["merged"], d_o, ta=True, name=f"g_proj_out{l}", bm=D_MODEL, bn=D_MODEL, bk=bk_r,
                                out_dtype=BF16)
        dp0, dp1, dp2, dz_g, dbg = _merge_bwd(s["z"], vec(b_gate[l]), s["p"], dmerged, f"d_merge{l}")
        grads["b_gate"][l] = dbg[0]
        dps = (dp0, dp1, dp2)
        dys = [_mm(dps[t], full["br", l], tb=True, b_lead=t, name=f"d_branch{l}_{t}", bm=bm, bn=D_MODEL, bk=D_MODEL,
                   out_dtype=BF16) for t in range(3)]
        grads["w_branch"][l] = jnp.stack(
            [_mm(s[f"y{t}"], dps[t], ta=True, name=f"g_branch{l}_{t}", bm=D_MODEL, bn=D_MODEL, bk=bk_r,
                 out_dtype=BF16) for t in range(3)])
        arrs = [_to_slots(grads["w_up"][l], 0)] + ([_to_slots(grads["w_in"][l + 1], 0)] if l + 1 < DEPTH else [])
        (dq, dk, dv, dsk), got = _attn_bwd(s["qkv"], sink[l], s["y0"], dys[0], s["lse"], n_lat, m_ctx, f"d_attn{l}",
                                           carry=(arrs, ["a2a"] * len(arrs)))
        recv["up", l] = got[0]
        if l + 1 < DEPTH:
            recv["in", l + 1] = got[1]
        grads["sink"][l] = dsk[:, 0, :8].reshape(N_HEADS)
        dz_qkv = _rope_bwd(dq, dk, dv, cos, sin, f"d_rope{l}")
        (dz_a, dws, dbs, dgv), (recv["out", l], recv["br", l]) = _gating_bwd(
            s["z"], ws_b[l], ws_t[l], s["bias"], vec(g_v[l]), dys[1], f"d_gating{l}",
            carry=([_to_slots(grads["w_out"][l], 0), _to_slots(grads["w_branch"][l], 1)], ["a2a"] * 2))
        grads["w_spatial"][l], grads["b_spatial"][l], grads["g_v"][l] = dws, dbs.T, dgv[0]
        dz_b, dwsc = _sconv_bwd(s["z"], w_sconv_full[l], dys[2], n_lat, f"d_sconv{l}")
        grads["w_sconv"][l] = dwsc[:3]
        kvw = 2 * N_KV_HEADS * HEAD_DIM
        pieces = [(dz_g, None, 0, 3, 0, D_MODEL), (dz_qkv, None, 0, 1, P_Q // D_MODEL, D_MODEL),
                  (dz_a, None, 0, 2, P_A // D_MODEL, D_MODEL), (dz_b, "step", 0, 3, P_B // D_MODEL, D_MODEL),
                  (dz_qkv, None, D_MODEL // kvw, 1, P_KV // kvw, kvw)]
        gw = lambda a, nm, rows, **kw: _mm(a, s["h"], ta=True, name=f"g_proj_in{l}_{nm}", bm=rows, bn=D_MODEL,
                                           bk=bk_r, out_dtype=BF16, **kw)
        gw_g = gw(dz_g, "gate", 1536)
        gw_qkv = gw(dz_qkv, "qkv", 1536)
        gw_a = gw(dz_a, "gating", 1024)
        gw_b = [gw(dz_b, f"sconv{t}", 1024, a_lead=t) for t in range(3)]
        grads["w_in"][l] = jnp.concatenate([gw_qkv, gw_a] + gw_b + [gw_g], axis=0)
        bm_in = _pick(r, [1408, 768, 256])
        if l > 0:
            dh = _mm_pieces(pieces, full["in", l], w_t=True, name=f"d_proj_in{l}", bm=bm_in, out_dtype=BF16)
        else:
            dh, (recv["in", 0], small_recv[0]) = _mm_pieces(
                pieces, full["in", l], w_t=True, name=f"d_proj_in{l}", bm=bm_in, out_dtype=BF16,
                carry=([_to_slots(grads["w_in"][0], 0), small_pack(0)], ["a2a", "gather"]))
        below = None if l == 0 else (saved[l - 1]["dd"], mods[l - 1], 5)
        outs = _norm_bwd(s["x0"], vec(g_mix[l]), mods[l], dh, dx1, n_lat, 0, 1, f"d_norm_mix{l}", res=below)
        if below is None:
            dxs, acc, accs = outs
        else:
            dxs, d_dd, acc, accs = outs
        grads["g_mix"][l] = acc[0]
        dmods[l] = jnp.stack([accs[:, 0], accs[:, 1], gate1, shift2, scale2, gate2], axis=1)
        gate2 = accs[:, 2]

    dmod_own = jnp.stack([dmods[l][0].reshape(-1) for l in range(DEPTH)])
    dmod_ctx = jnp.stack([dmods[l][1].reshape(-1) for l in range(DEPTH)])
    late = [grads["g_mix"][0], dmod_own + dmod_ctx, dmod_ctx, dmod_own]
    late_shapes = [a.shape for a in late]
    (late_all,) = _exchange([_pack(late)], ["gather"], "gather_late_grads")
    g_mix0_g, b_mod_g, dmodc_tot, _ = _unpack(_sum_slots(late_all, "sum_late_grads"), late_shapes)
    dmod_all = _unpack(late_all, late_shapes, lead=(N_DEV,))[-1]
    layer_sums = [_unpack(_sum_slots(small_recv[l], f"sum_small_grads{l}"), small_shapes_of[l]) for l in range(DEPTH)]
    by_name = {k: jnp.stack([layer_sums[l][t] for l in range(DEPTH)]) for t, k in enumerate(small_names)}
    g_mix_g = jnp.stack([g_mix0_g] + [layer_sums[l][len(small_names)] for l in range(1, DEPTH)])
    loss_sum, g_final_g = layer_sums[DEPTH - 1][-2], layer_sums[DEPTH - 1][-1]
    g_ffn_g, g_v_g, b_gate_g, sink_g = by_name["g_ffn"], by_name["g_v"], by_name["b_gate"], by_name["sink"]
    w_spatial_g, b_spatial_g = by_name["w_spatial"], by_name["b_spatial"]
    w_sconv_g = lax.dynamic_slice_in_dim(by_name["w_sconv"], me * w_sconv.shape[2], w_sconv.shape[2], axis=2)
    w_fconv_g = lax.dynamic_slice_in_dim(by_name["w_fconv"], me * w_fconv.shape[2], w_fconv.shape[2], axis=2)

    dmod_cols = lax.dynamic_slice_in_dim(dmod_all, me * mod_w, mod_w, axis=2)
    dmodc_cols = lax.dynamic_slice_in_dim(dmodc_tot, me * mod_w, mod_w, axis=1)
    g_w_mod, cctx_part = [], None
    for l in range(DEPTH):
        rhs = jnp.concatenate([dmod_cols[:, l], dmodc_cols[l][None], jnp.zeros((7, mod_w), F32)], axis=0)
        g_w_mod.append(_mm(act, rhs, ta=True, name=f"g_mod{l}", bm=D_MODEL, bn=mod_w, bk=16, out_dtype=F32))
        lhs = jnp.pad(dmodc_cols[l][None], ((0, 7), (0, 0)))
        part = _mm(lhs, w_mod[l], tb=True, name=f"d_cctx{l}", bm=8, bn=D_MODEL, bk=mod_w, out_dtype=F32)
        cctx_part = part if cctx_part is None else cctx_part + part
    sg = jax.nn.sigmoid(c_ctx)
    dsilu = (sg * (1.0 + c_ctx * (1.0 - sg))).reshape(8, LANES)

    (r_cctx,) = _exchange([cctx_part[0].reshape(8, LANES)], ["gather"], "gather_c_ctx_grad")

    def per_layer(a):
        return a.reshape(a.shape[0], -1, a.shape[-1])

    upd = {}
    for nm, kind, wv, mv, vv in (("w_in", "in", w_in, m_w_in, v_w_in), ("w_branch", "br", w_branch, m_w_branch, v_w_branch),
                                 ("w_out", "out", w_out, m_w_out, v_w_out), ("w_up", "up", w_up, m_w_up, v_w_up),
                                 ("w_down", "dn", w_down, m_w_down, v_w_down), ("w_mod", None, w_mod, m_w_mod, v_w_mod)):
        if kind is None:
            gsrcs = [g[None] for g in g_w_mod]
        else:
            gsrcs = [per_layer(recv[kind, l]) for l in range(DEPTH)]
        if kind in ("in", "up"):
            outs = _adamw(tr(wv), gsrcs, tr(mv), tr(vv), f"adamw_{nm}")
            upd[nm] = [tr(o) for o in outs]
        else:
            outs = _adamw(per_layer(wv), gsrcs, per_layer(mv), per_layer(vv), f"adamw_{nm}")
            upd[nm] = [o.reshape(wv.shape) for o in outs]
    as_tile = lambda a: a.reshape(1, 8, LANES)
    upd["c_ctx"] = [o.reshape(D_MODEL) for o in _adamw(
        as_tile(c_ctx), [r_cctx], as_tile(m_c_ctx), as_tile(v_c_ctx), "adamw_c_ctx", scale=as_tile(dsilu))]

    names = ["b_mod", "g_mix", "b_gate", "sink", "w_spatial", "b_spatial", "g_v", "w_sconv", "g_ffn", "w_fconv",
             "g_final"]
    w_s = [b_mod, g_mix, b_gate, sink, w_spatial, b_spatial, g_v, w_sconv, g_ffn, w_fconv, g_final]
    g_s = [b_mod_g, g_mix_g, b_gate_g, sink_g, w_spatial_g, b_spatial_g, g_v_g, w_sconv_g, g_ffn_g, w_fconv_g,
           g_final_g]
    m_s = [m_b_mod, m_g_mix, m_b_gate, m_sink, m_w_spatial, m_b_spatial, m_g_v, m_w_sconv, m_g_ffn, m_w_fconv,
           m_g_final]
    v_s = [v_b_mod, v_g_mix, v_b_gate, v_sink, v_w_spatial, v_b_spatial, v_g_v, v_w_sconv, v_g_ffn, v_w_fconv,
           v_g_final]
    shapes = [a.shape for a in w_s]
    packed = _adamw(_pack(w_s)[None], [_pack(g_s)[None]], _pack(m_s)[None], _pack(v_s)[None], "adamw_small")
    unpacked = [_unpack(o[0], shapes) for o in packed]
    for t, nm in enumerate(names):
        upd[nm] = [unpacked[q][t] for q in range(4)]

    order = ["c_ctx", "w_mod", "b_mod", "g_mix", "w_in", "b_gate", "sink", "w_spatial", "b_spatial", "g_v", "w_sconv",
             "w_branch", "w_out", "g_ffn", "w_up", "w_fconv", "w_down", "g_final"]
    result = [loss_sum.reshape(()), dxs[None]]
    for q in range(4):
        result += [upd[nm][q] for nm in order]
    return tuple(result)
```

```python
import jax
import jax.numpy as jnp
from jax import lax
from jax.experimental import pallas as pl
from jax.experimental.pallas import tpu as pltpu

F32, BF16 = jnp.float32, jnp.bfloat16

D_MODEL = 1024
DEPTH = 2
GRID_W = 64
N_HEADS = 16
N_KV_HEADS = 4
HEAD_DIM = 64
WINDOW = 128
BLK = 128
ROPE_THETA = 10000.0
CHUNK = 128
A_GROUPS = 8
D_FF = 2816
EPS = 1e-6
NEG = -1e30
IN_W = 9728
OFF_K, OFF_A, OFF_B, OFF_G = 1024, 1536, 3584, 6656
P_Q, P_A, P_B, P_KV = 3072, 4096, 6144, 9216

N_DEV = 8
LANES = 128
ROW_TILE = 256
CONV_CHUNK = 256
PAD = 8
VMEM_LIMIT = 52 * 1024 * 1024

ADAM_LR, ADAM_B1, ADAM_B2, ADAM_EPS, ADAM_WD, ADAM_STEP = 0.001, 0.9, 0.999, 1e-08, 0.01, 10

HBM_SPEC = pl.BlockSpec(memory_space=pltpu.HBM)
SMEM_SPEC = pl.BlockSpec(memory_space=pltpu.SMEM)


def _params():
    return pltpu.CompilerParams(vmem_limit_bytes=VMEM_LIMIT)


def _pick(n, prefs):
    for p in prefs:
        if n % p == 0:
            return p
    raise ValueError((n, prefs))


def _sigmoid(x):
    return 0.5 * jnp.tanh(0.5 * x) + 0.5


def _mm(a, b, *, name, ta=False, tb=False, bm, bn, bk, out_dtype, a_lead=None, b_lead=None, a_spec=None,
        b_spec=None, cols_outer=False, carry=None):
    ash = a.shape[1:] if a_lead is not None else a.shape
    bsh = b.shape[1:] if b_lead is not None else b.shape
    kc = (bsh[1] if tb else bsh[0]) if b_spec is None else (ash[0] if ta else ash[1])
    mo =(ash[1] if ta else ash[0]) if a_spec is None else a_spec[2]
    no = (bsh[0] if tb else bsh[1]) if b_spec is None else b_spec[2]
    assert mo % bm == 0 and no % bn == 0 and kc % bk == 0, (name, mo, no, kc, bm, bn, bk)
    nk = kc // bk

    def spec(shape, fn, idx=None):
        if idx is not None:
            shape, inner = (None,) + shape, fn
            fn = lambda i, j, k: (idx,) + inner(i, j, k)
        if cols_outer:
            return pl.BlockSpec(shape, lambda j, i, k: fn(i, j, k))
        return pl.BlockSpec(shape, fn)

    if a_spec is not None:
        a_bs = spec(a_spec[0], a_spec[1])
    elif ta:
        a_bs = spec((bk, bm), lambda i, j, k: (k, i), a_lead)
    else:
        a_bs = spec((bm, bk), lambda i, j, k: (i, k), a_lead)
    if b_spec is not None:
        b_bs = spec(b_spec[0], b_spec[1])
    elif tb:
        b_bs = spec((bn, bk), lambda i, j, k: (j, k), b_lead)
    else:
        b_bs = spec((bk, bn), lambda i, j, k: (k, j), b_lead)
    dims = (((0 if ta else 1,), (1 if tb else 0,)), ((), ()))
    grid = (no // bn, mo // bm, nk) if cols_outer else (mo // bm, no // bn, nk)

    def body(a_ref, b_ref, o_ref, *scratch):
        if nk == 1:
            o_ref[...] = lax.dot_general(a_ref[...], b_ref[...], dims, preferred_element_type=F32).astype(o_ref.dtype)
        else:
            acc = scratch[0]
            k = pl.program_id(2)

            @pl.when(k == 0)
            def _():
                acc[...] = jnp.zeros_like(acc)

            acc[...] += lax.dot_general(a_ref[...], b_ref[...], dims, preferred_element_type=F32)

            @pl.when(k == nk - 1)
            def _():
                o_ref[...] = acc[...].astype(o_ref.dtype)

    outs, carried = _pcall(
        body, (a, b), name=name, grid=grid, in_specs=[a_bs, b_bs],
        out_specs=[spec((bm, bn), lambda i, j, k: (i, j))],
        out_shape=[jax.ShapeDtypeStruct((mo, no), out_dtype)],
        scratch_shapes=[pltpu.VMEM((bm, bn), F32)] if nk > 1 else [], carry=carry)
    return outs[0] if carry is None else (outs[0], carried)


def _mm_pieces(pieces, w, *, name, bm, w_t=False, out_dtype=F32, carry=None):
    kout = w.shape[1] if w_t else w.shape[0]
    starts, total = [], 0
    for piece in pieces:
        starts.append(total)
        total += piece[3]
    mo = pieces[0][0].shape[-2]
    assert mo % bm == 0
    widths = sorted({piece[5] for piece in pieces}, reverse=True)

    def inside(p, k):
        return (k >= starts[p]) & (k < starts[p] + pieces[p][3])

    def a_spec(p):
        _, lead, col0, nblk, _, bk = pieces[p]

        def fn(i, k):
            t = jnp.clip(k - starts[p], 0, nblk - 1)
            if lead is None:
                return (i, col0 + t)
            return (t, i, col0) if lead == "step" else (lead, i, col0 + t)
        return pl.BlockSpec((bm, bk) if lead is None else (None, bm, bk), fn)

    def w_spec(bk):
        def fn(i, k):
            col = 0
            for p, piece in enumerate(pieces):
                if piece[5] == bk:
                    col = col + jnp.where(inside(p, k), piece[4] + k - starts[p], 0)
            return (col, 0) if w_t else (0, col)
        return pl.BlockSpec((bk, kout) if w_t else (kout, bk), fn)

    n_p, n_w = len(pieces), len(widths)

    def body(*refs):
        a_refs, w_refs, o_ref, acc = refs[:n_p], refs[n_p:n_p + n_w], refs[n_p + n_w], refs[n_p + n_w + 1]
        k = pl.program_id(1)

        @pl.when(k == 0)
        def _():
            acc[...] = jnp.zeros_like(acc)

        for p in range(n_p):
            w_ref = w_refs[widths.index(pieces[p][5])]

            @pl.when(inside(p, k))
            def _(p=p, w_ref=w_ref):
                acc[...] += lax.dot_general(a_refs[p][...], w_ref[...], NN if w_t else NT,
                                            preferred_element_type=F32)

        @pl.when(k == total - 1)
        def _():
            o_ref[...] = acc[...].astype(o_ref.dtype)

    outs, carried = _pcall(
        body, [piece[0] for piece in pieces] + [w] * n_w, name=name, grid=(mo // bm, total),
        in_specs=[a_spec(p) for p in range(n_p)] + [w_spec(bk) for bk in widths],
        out_specs=[pl.BlockSpec((bm, kout), lambda i, k: (i, 0))],
        out_shape=[jax.ShapeDtypeStruct((mo, kout), out_dtype)],
        scratch_shapes=[pltpu.VMEM((bm, kout), F32)], carry=carry)
    return outs[0] if carry is None else (outs[0], carried)


def _xchg_out_shapes(arrs, modes):
    return [jax.ShapeDtypeStruct((N_DEV,) + a.shape if m == "gather" else a.shape, a.dtype)
            for a, m in zip(arrs, modes)]


def _xchg_sems(n):
    return [pltpu.SemaphoreType.DMA((n, N_DEV - 1)), pltpu.SemaphoreType.DMA((n, N_DEV - 1)),
            pltpu.SemaphoreType.DMA((n,))]


def _xchg_copies(ins, outs, modes, sems):
    send_sems, recv_sems, local_sems = sems
    x, y, c = lax.axis_index("x"), lax.axis_index("y"), lax.axis_index("c")
    me = 4 * x + 2 * y + c

    def place(q):
        px = 1 - x if (q >> 2) & 1 else x
        py = 1 - y if (q >> 1) & 1 else y
        pc = 1 - c if q & 1 else c
        return (px, py, pc), 4 * px + 2 * py + pc

    sibling, _ = place(1)
    out = dict(local=[], direct=[], landed=[], passed=[], others=[])
    for t, mode in enumerate(modes):
        gather = mode == "gather"
        mine = ins[t] if gather else ins[t].at[me]
        out["local"].append(pltpu.make_async_copy(mine, outs[t].at[me], local_sems.at[t]))
        for q in range(1, N_DEV):
            peer, slot = place(q)
            sem = dict(send_sem=send_sems.at[t, q - 1], recv_sem=recv_sems.at[t, q - 1],
                       device_id_type=pl.DeviceIdType.MESH)
            arrival = pltpu.make_async_remote_copy(src_ref=outs[t].at[slot], dst_ref=outs[t].at[slot],
                                                   device_id=peer, **sem)
            if not gather:
                out["direct"].append(pltpu.make_async_remote_copy(src_ref=ins[t].at[slot], dst_ref=outs[t].at[me],
                                                                  device_id=peer, **sem))
                out["others"].append(arrival)
            elif q == 1 or q % 2 == 0:
                out["direct"].append(pltpu.make_async_remote_copy(src_ref=ins[t], dst_ref=outs[t].at[me],
                                                                  device_id=peer, **sem))
                out["others" if q == 1 else "landed"].append(arrival)
            else:
                _, origin = place(q - 1)
                out["passed"].append(pltpu.make_async_remote_copy(src_ref=outs[t].at[origin], dst_ref=outs[t].at[origin],
                                                                  device_id=sibling, **sem))
                out["others"].append(arrival)
    return out


def _xchg_start(copies):
    for cp in copies["local"] + copies["direct"]:
        cp.start()


def _xchg_wait(copies):
    for cp in copies["landed"]:
        cp.wait_recv()
    for cp in copies["passed"]:
        cp.start()
    for cp in copies["others"]:
        cp.wait_recv()
    for cp in copies["direct"] + copies["passed"]:
        cp.wait_send()
    for cp in copies["local"]:
        cp.wait()


def _exchange(arrs, modes, name):
    n = len(arrs)

    def body(*refs):
        copies = _xchg_copies(refs[:n], refs[n:2 * n], modes, refs[2 * n:])
        _xchg_start(copies)
        _xchg_wait(copies)

    outs = pl.pallas_call(
        body, name=name, in_specs=[HBM_SPEC] * n, out_specs=[HBM_SPEC] * n, out_shape=_xchg_out_shapes(arrs, modes),
        scratch_shapes=_xchg_sems(n), compiler_params=pltpu.CompilerParams(has_side_effects=True),
    )(*arrs)
    return list(outs)


def _pcall(body, operands, *, name, grid, in_specs, out_specs, out_shape, scratch_shapes=(), carry=None):
    out_specs, out_shape, scratch_shapes = list(out_specs), list(out_shape), list(scratch_shapes)
    if carry is None:
        outs = pl.pallas_call(body, name=name, grid=grid, in_specs=in_specs, out_specs=out_specs,
                              out_shape=out_shape, scratch_shapes=scratch_shapes, compiler_params=_params())(*operands)
        return list(outs), []
    arrs, modes = carry
    n, n_in, n_out, n_scr = len(arrs), len(in_specs), len(out_specs), len(scratch_shapes)

    def wrapped(*refs):
        ins, c_in = refs[:n_in], refs[n_in:n_in + n]
        outs, c_out = refs[n_in + n:n_in + n + n_out], refs[n_in + n + n_out:n_in + 2 * n + n_out]
        rest = refs[n_in + 2 * n + n_out:]
        scr, sems = rest[:n_scr], rest[n_scr:]
        first, last = None, None
        for d, size in enumerate(grid):
            f, e = pl.program_id(d) == 0, pl.program_id(d) == size - 1
            first = f if first is None else first & f
            last = e if last is None else last & e

        @pl.when(first)
        def _():
            _xchg_start(_xchg_copies(c_in, c_out, modes, sems))

        body(*ins, *outs, *scr)

        @pl.when(last)
        def _():
            _xchg_wait(_xchg_copies(c_in, c_out, modes, sems))

    outs = pl.pallas_call(
        wrapped, name=name, grid=grid, in_specs=list(in_specs) + [HBM_SPEC] * n,
        out_specs=out_specs + [HBM_SPEC] * n, out_shape=out_shape + _xchg_out_shapes(arrs, modes),
        scratch_shapes=scratch_shapes + _xchg_sems(n), compiler_params=_params())(*operands, *arrs)
    return list(outs[:n_out]), list(outs[n_out:])


def _row_specs(r, n_lat, tm):
    nbl = n_lat // tm
    row = pl.BlockSpec((tm, D_MODEL), lambda i: (i, 0))
    mod = pl.BlockSpec((None, 8, D_MODEL), lambda i: (i // nbl, 0, 0))
    vec = pl.BlockSpec((1, D_MODEL), lambda i: (0, 0))
    return nbl, row, mod, vec


def _rows(vals):
    width = [v for v in vals if v is not None][0].shape[1]
    return jnp.concatenate([jnp.zeros((1, width), F32) if v is None else v for v in vals], axis=0)


def _norm_fwd(xs, g, mods, n_lat, sh, sc, name, res=None, carry=None):
    r = xs.shape[0]
    tm = ROW_TILE
    nbl, row, mod, vec = _row_specs(r, n_lat, tm)

    def norm(x, g_ref, m_ref, h_ref):
        rs = lax.rsqrt(jnp.mean(x * x, axis=-1, keepdims=True) + EPS)
        m = m_ref[...]
        h_ref[...] = ((x * rs * g_ref[...]) * (1.0 + m[sc:sc + 1]) + m[sh:sh + 1]).astype(BF16)

    if res is None:
        def body(x_ref, g_ref, m_ref, h_ref):
            norm(x_ref[...], g_ref, m_ref, h_ref)
        outs, carried = _pcall(body, (xs, g, mods), name=name, grid=(r // tm,), in_specs=[row, vec, mod],
                               out_specs=[row], out_shape=[jax.ShapeDtypeStruct((r, D_MODEL), BF16)], carry=carry)
        return outs[0] if carry is None else (outs[0], carried)

    o, mods_res, gt = res

    def body(x_ref, o_ref, mr_ref, g_ref, m_ref, x1_ref, h_ref):
        x = x_ref[...] + mr_ref[...][gt:gt + 1] * o_ref[...]
        x1_ref[...] = x
        norm(x, g_ref, m_ref, h_ref)

    return pl.pallas_call(
        body, name=name, grid=(r // tm,), in_specs=[row, row, mod, vec, mod], out_specs=[row, row],
        out_shape=[jax.ShapeDtypeStruct((r, D_MODEL), F32), jax.ShapeDtypeStruct((r, D_MODEL), BF16)],
        compiler_params=_params())(xs, o, mods_res, g, mods)


def _rms_bwd(x, g, dy):
    rs = lax.rsqrt(jnp.mean(x * x, axis=-1, keepdims=True) + EPS)
    xh = x * rs
    dxh = dy * g
    dx = rs * (dxh - xh * jnp.mean(dxh * xh, axis=-1, keepdims=True))
    return dx, dy * xh, xh


def _acc_specs(nbl):
    acc_all = pl.BlockSpec((8, D_MODEL), lambda i: (0, 0))
    acc_stream = pl.BlockSpec((None, 8, D_MODEL), lambda i: (i // nbl, 0, 0))
    return acc_all, acc_stream


def _loss_bwd(xs, o, mods, g_final, target, n_lat, name):
    r = xs.shape[0]
    tm = ROW_TILE
    nbl, row, mod, vec = _row_specs(r, n_lat, tm)
    acc_all, acc_stream = _acc_specs(nbl)
    tgt = pl.BlockSpec((tm, D_MODEL), lambda i: (jnp.minimum(i, nbl - 1), 0))

    def body(x_ref, o_ref, m_ref, g_ref, t_ref, dx_ref, do_ref, acc_ref, accs_ref):
        i = pl.program_id(0)
        lat = i < nbl
        gate = m_ref[...][5:6]
        o_val = o_ref[...]
        x = x_ref[...] + gate * o_val
        g = g_ref[...]
        rs = lax.rsqrt(jnp.mean(x * x, axis=-1, keepdims=True) + EPS)
        y = x * rs * g
        err = jnp.where(lat, y - t_ref[...], 0.0)
        loss = 0.5 * jnp.sum(jnp.mean(err * err, axis=-1, keepdims=True), axis=0, keepdims=True)
        dy = err * (1.0 / D_MODEL)
        dx, dg_rows, _ = _rms_bwd(x, g, dy)
        dx_ref[...] = dx
        do_ref[...] = (gate * dx).astype(BF16)

        @pl.when(i == 0)
        def _():
            acc_ref[...] = jnp.zeros_like(acc_ref)

        @pl.when((i == 0) | (i == nbl))
        def _():
            accs_ref[...] = jnp.zeros_like(accs_ref)

        acc_ref[...] += _rows([jnp.sum(dg_rows, axis=0, keepdims=True), jnp.broadcast_to(loss, (1, D_MODEL))]
                              + [None] * 6)
        accs_ref[...] += _rows([None, None, jnp.sum(dx * o_val, axis=0, keepdims=True)] + [None] * 5)

    return pl.pallas_call(
        body, name=name, grid=(r // tm,), in_specs=[row, row, mod, vec, tgt],
        out_specs=[row, row, acc_all, acc_stream],
        out_shape=[jax.ShapeDtypeStruct((r, D_MODEL), F32), jax.ShapeDtypeStruct((r, D_MODEL), BF16),
                   jax.ShapeDtypeStruct((8, D_MODEL), F32), jax.ShapeDtypeStruct((2, 8, D_MODEL), F32)],
        compiler_params=_params())(xs, o, mods, g_final, target)


def _norm_bwd(xs, g, mods, dh, dx_in, n_lat, sh, sc, name, res=None):
    r = xs.shape[0]
    tm = ROW_TILE
    nbl, row, mod, vec = _row_specs(r, n_lat, tm)
    acc_all, acc_stream = _acc_specs(nbl)
    has_res = res is not None

    def body(*refs):
        if has_res:
            x_ref, g_ref, m_ref, dh_ref, dxi_ref, o_ref, mr_ref, dx_ref, do_ref, acc_ref, accs_ref = refs
        else:
            x_ref, g_ref, m_ref, dh_ref, dxi_ref, dx_ref, acc_ref, accs_ref = refs
        i = pl.program_id(0)
        x, g, m, dhv = x_ref[...], g_ref[...], m_ref[...], dh_ref[...].astype(F32)
        dy = dhv * (1.0 + m[sc:sc + 1])
        dxn, dg_rows, xh = _rms_bwd(x, g, dy)
        dx = dxi_ref[...] + dxn
        if has_res:
            dx_ref[...] = dx
        else:
            @pl.when(i < nbl)
            def _():
                dx_ref[...] = dx
        d_gate = None
        if has_res:
            o_val = o_ref[...]
            do_ref[...] = (mr_ref[...][res[2]:res[2] + 1] * dx).astype(BF16)
            d_gate = jnp.sum(dx * o_val, axis=0, keepdims=True)

        @pl.when(i == 0)
        def _():
            acc_ref[...] = jnp.zeros_like(acc_ref)

        @pl.when((i == 0) | (i == nbl))
        def _():
            accs_ref[...] = jnp.zeros_like(accs_ref)

        acc_ref[...] += _rows([jnp.sum(dg_rows, axis=0, keepdims=True)] + [None] * 7)
        accs_ref[...] += _rows([jnp.sum(dhv, axis=0, keepdims=True),
                                jnp.sum(dhv * (xh * g), axis=0, keepdims=True), d_gate] + [None] * 5)

    ins = [xs, g, mods, dh, dx_in]
    in_specs = [row, vec, mod, row, row]
    if has_res:
        out_specs, out_shape = [row], [jax.ShapeDtypeStruct((r, D_MODEL), F32)]
    else:
        out_specs = [pl.BlockSpec((tm, D_MODEL), lambda i: (jnp.minimum(i, nbl - 1), 0))]
        out_shape = [jax.ShapeDtypeStruct((n_lat, D_MODEL), F32)]
    if has_res:
        ins += [res[0], res[1]]
        in_specs += [row, mod]
        out_specs.append(row)
        out_shape.append(jax.ShapeDtypeStruct((r, D_MODEL), BF16))
    out_specs += [acc_all, acc_stream]
    out_shape += [jax.ShapeDtypeStruct((8, D_MODEL), F32), jax.ShapeDtypeStruct((2, 8, D_MODEL), F32)]
    return pl.pallas_call(body, name=name, grid=(r // tm,), in_specs=in_specs, out_specs=out_specs,
                          out_shape=out_shape, compiler_params=_params())(*ins)


def _rotate(t, cos, sin):
    width = t.shape[1]
    reps = width // LANES
    lane = lax.broadcasted_iota(jnp.int32, (1, width), 1)
    first = (lane % HEAD_DIM) < (HEAD_DIM // 2)
    swapped = jnp.where(first, pltpu.roll(t, width - HEAD_DIM // 2, 1), pltpu.roll(t, HEAD_DIM // 2, 1))
    return t * jnp.tile(cos, (1, reps)) + swapped * jnp.tile(sin, (1, reps))


def _rope_fwd(z, cos, sin, name):
    r = z.shape[0]
    tm = ROW_TILE
    kvw = 2 * N_KV_HEADS * HEAD_DIM
    tab = pl.BlockSpec((tm, LANES), lambda i: (i, 0))

    def body(q_ref, kv_ref, c_ref, s_ref, o_ref):
        c, s = c_ref[...], s_ref[...]
        kv = kv_ref[...]
        o_ref[:, :D_MODEL] = (_rotate(q_ref[...].astype(F32), c, s) * (HEAD_DIM ** -0.5)).astype(BF16)
        o_ref[:, D_MODEL:D_MODEL + kvw // 2] = _rotate(kv[:, :kvw // 2].astype(F32), c, s).astype(BF16)
        o_ref[:, D_MODEL + kvw // 2:] = kv[:, kvw // 2:]

    return pl.pallas_call(
        body, name=name, grid=(r // tm,),
        in_specs=[pl.BlockSpec((tm, D_MODEL), lambda i: (i, P_Q // D_MODEL)),
                  pl.BlockSpec((tm, kvw), lambda i: (i, P_KV // kvw)), tab, tab],
        out_specs=pl.BlockSpec((tm, D_MODEL + kvw), lambda i: (i, 0)),
        out_shape=jax.ShapeDtypeStruct((r, D_MODEL + kvw), BF16), compiler_params=_params())(z, z, cos, sin)


def _rope_bwd(dq, dk, dv, cos, sin, name):
    r = dq.shape[0]
    tm = ROW_TILE
    kw = N_KV_HEADS * HEAD_DIM
    tab = pl.BlockSpec((tm, LANES), lambda i: (i, 0))

    def body(dq_ref, dk_ref, dv_ref, c_ref, s_ref, o_ref):
        c, s = c_ref[...], -s_ref[...]
        o_ref[:, :D_MODEL] = (_rotate(dq_ref[...].astype(F32), c, s) * (HEAD_DIM ** -0.5)).astype(BF16)
        o_ref[:, D_MODEL:D_MODEL + kw] = _rotate(dk_ref[...], c, s).astype(BF16)
        o_ref[:, D_MODEL + kw:] = dv_ref[...].astype(BF16)

    return pl.pallas_call(
        body, name=name, grid=(r // tm,),
        in_specs=[pl.BlockSpec((tm, D_MODEL), lambda i: (i, 0)), pl.BlockSpec((tm, kw), lambda i: (i, 0)),
                  pl.BlockSpec((tm, kw), lambda i: (i, 0)), tab, tab],
        out_specs=pl.BlockSpec((tm, D_MODEL + 2 * kw), lambda i: (i, 0)),
        out_shape=jax.ShapeDtypeStruct((r, D_MODEL + 2 * kw), BF16), compiler_params=_params())(dq, dk, dv, cos, sin)


def _attn_setup(i, n_lat, m_ctx, nbl, k_ref, v_ref):
    start = pl.multiple_of(jnp.clip((i - 1) * BLK, 0, n_lat - 3 * BLK), BLK)
    nkeys = 3 * BLK + m_ctx
    rows = lax.broadcasted_iota(jnp.int32, (4 * BLK, nkeys), 0)
    cols = lax.broadcasted_iota(jnp.int32, (4 * BLK, nkeys), 1)
    qpos = i * BLK + (rows & (BLK - 1))
    seen = (cols >= 3 * BLK) | ((jnp.abs(start + cols - qpos) <= WINDOW) & (i < nbl))
    mask = jnp.where(seen, 0.0, NEG)
    kblk = jnp.concatenate([k_ref[pl.ds(start, 3 * BLK), :], k_ref[pl.ds(n_lat, m_ctx), :]], axis=0)
    vblk = jnp.concatenate([v_ref[pl.ds(start, 3 * BLK), :], v_ref[pl.ds(n_lat, m_ctx), :]], axis=0)
    lo = lax.broadcasted_iota(jnp.int32, (1, LANES), 1) < HEAD_DIM
    return start, mask, kblk, vblk, lo


def _stack_heads(ref, kh, lo):
    a = ref[:, (2 * kh) * LANES:(2 * kh + 1) * LANES]
    b = ref[:, (2 * kh + 1) * LANES:(2 * kh + 2) * LANES]
    z = jnp.zeros_like(a)
    return jnp.concatenate([jnp.where(lo, a, z), jnp.where(lo, z, a), jnp.where(lo, b, z), jnp.where(lo, z, b)],
                           axis=0)


def _kv_variants(blk, rolled, kh, lo):
    z = jnp.zeros_like(blk)
    if kh == 0:
        return jnp.where(lo, blk, rolled), jnp.where(lo, blk, z), jnp.where(lo, z, rolled)
    return jnp.where(lo, rolled, blk), jnp.where(lo, rolled, z), jnp.where(lo, z, blk)


NN = (((1,), (0,)), ((), ()))
NT = (((1,), (1,)), ((), ()))
TN = (((0,), (0,)), ((), ()))


def _scores(qst, k2, mask, sink_ref, p, kh):
    s = lax.dot_general(qst, k2, NT, preferred_element_type=F32) + mask
    snk = jnp.concatenate([jnp.full((BLK, 1), sink_ref[p * 8 + kh * 4 + g], F32) for g in range(4)], axis=0)
    return s, snk


def _attn_fwd(qkv, sink, n_lat, m_ctx, name, carry=None):
    r = qkv.shape[0]
    nb, nbl = r // BLK, n_lat // BLK
    kcol = D_MODEL // LANES

    def body(sink_ref, q_ref, k_ref, v_ref, o_ref, lse_ref):
        p, i = pl.program_id(0), pl.program_id(1)
        _, mask, kblk, vblk, lo = _attn_setup(i, n_lat, m_ctx, nbl, k_ref, v_ref)
        kr, vr = pltpu.roll(kblk, HEAD_DIM, 1), pltpu.roll(vblk, HEAD_DIM, 1)
        for kh in range(2):
            k2, _, _ = _kv_variants(kblk, kr, kh, lo)
            _, vlo, vhi = _kv_variants(vblk, vr, kh, lo)
            qst = _stack_heads(q_ref, kh, lo)
            s, snk = _scores(qst, k2, mask, sink_ref, p, kh)
            mx = jnp.maximum(jnp.max(s, axis=-1, keepdims=True), snk)
            pe = jnp.exp(s - mx)
            den = jnp.sum(pe, axis=-1, keepdims=True) + jnp.exp(snk - mx)
            inv = 1.0 / den
            pb = pe.astype(BF16)
            for jp in range(2):
                r0 = 2 * jp * BLK
                pair = (jnp.dot(pb[r0:r0 + BLK], vlo, preferred_element_type=F32)
                        + jnp.dot(pb[r0 + BLK:r0 + 2 * BLK], vhi, preferred_element_type=F32))
                pair = pair * jnp.where(lo, inv[r0:r0 + BLK], inv[r0 + BLK:r0 + 2 * BLK])
                o_ref[:, (2 * kh + jp) * LANES:(2 * kh + jp + 1) * LANES] = pair.astype(BF16)
            lse = mx + jnp.log(den)
            for g in range(4):
                lse_ref[:, kh * 4 + g:kh * 4 + g + 1] = lse[g * BLK:(g + 1) * BLK]

    outs, carried = _pcall(
        body, (sink, qkv, qkv, qkv), name=name, grid=(2, nb),
        in_specs=[SMEM_SPEC,
                  pl.BlockSpec((BLK, 4 * LANES), lambda p, i: (i, p)),
                  pl.BlockSpec((r, LANES), lambda p, i: (0, kcol + p)),
                  pl.BlockSpec((r, LANES), lambda p, i: (0, kcol + 2 + p))],
        out_specs=[pl.BlockSpec((BLK, 4 * LANES), lambda p, i: (i, p)),
                   pl.BlockSpec((None, BLK, 8), lambda p, i: (p, i, 0))],
        out_shape=[jax.ShapeDtypeStruct((r, D_MODEL), BF16), jax.ShapeDtypeStruct((2, r, 8), F32)], carry=carry)
    return outs if carry is None else (outs, carried)


def _attn_bwd(qkv, sink, o, do, lse, n_lat, m_ctx, name, carry=None):
    r = qkv.shape[0]
    nb, nbl = r // BLK, n_lat // BLK
    kcol = D_MODEL // LANES

    def body(sink_ref, q_ref, k_ref, v_ref, o_ref, do_ref, lse_ref, dq_ref, dk_ref, dv_ref, ds_ref):
        p, i = pl.program_id(0), pl.program_id(1)

        @pl.when(i == 0)
        def _():
            dk_ref[...] = jnp.zeros_like(dk_ref)
            dv_ref[...] = jnp.zeros_like(dv_ref)
            ds_ref[...] = jnp.zeros_like(ds_ref)

        start, mask, kblk, vblk, lo = _attn_setup(i, n_lat, m_ctx, nbl, k_ref, v_ref)
        kr, vr = pltpu.roll(kblk, HEAD_DIM, 1), pltpu.roll(vblk, HEAD_DIM, 1)
        lane = lax.broadcasted_iota(jnp.int32, (1, LANES), 1)
        dks, dvs = [], []
        for kh in range(2):
            k2, klo, khi = _kv_variants(kblk, kr, kh, lo)
            v2, _, _ = _kv_variants(vblk, vr, kh, lo)
            qst = _stack_heads(q_ref, kh, lo)
            dost = _stack_heads(do_ref, kh, lo)
            s, snk = _scores(qst, k2, mask, sink_ref, p, kh)
            lse4 = jnp.concatenate([lse_ref[:, kh * 4 + g:kh * 4 + g + 1] for g in range(4)], axis=0)
            pe = jnp.exp(s - lse4)
            dp = lax.dot_general(dost, v2, NT, preferred_element_type=F32)
            deltas = []
            for jp in range(2):
                cols = slice((2 * kh + jp) * LANES, (2 * kh + jp + 1) * LANES)
                prod = do_ref[:, cols].astype(F32) * o_ref[:, cols].astype(F32)
                deltas.append(jnp.sum(jnp.where(lo, prod, 0.0), axis=-1, keepdims=True))
                deltas.append(jnp.sum(jnp.where(lo, 0.0, prod), axis=-1, keepdims=True))
            delta = jnp.concatenate(deltas, axis=0)
            dsc = pe * (dp - delta)
            dsb, pb = dsc.astype(BF16), pe.astype(BF16)
            for jp in range(2):
                r0 = 2 * jp * BLK
                dq_ref[:, (2 * kh + jp) * LANES:(2 * kh + jp + 1) * LANES] = (
                    jnp.dot(dsb[r0:r0 + BLK], klo, preferred_element_type=F32)
                    + jnp.dot(dsb[r0 + BLK:r0 + 2 * BLK], khi, preferred_element_type=F32)).astype(BF16)
            dkf = lax.dot_general(dsb, qst, TN, preferred_element_type=F32)
            dvf = lax.dot_general(pb, dost, TN, preferred_element_type=F32)
            dks.append(dkf + pltpu.roll(dkf, HEAD_DIM, 1))
            dvs.append(dvf + pltpu.roll(dvf, HEAD_DIM, 1))
            contrib = -jnp.exp(snk - lse4) * delta
            for g in range(4):
                tot = jnp.sum(contrib[g * BLK:(g + 1) * BLK], axis=0, keepdims=True)
                ds_ref[0:1, :] += jnp.where(lane == kh * 4 + g, tot, 0.0)
        dk_blk = jnp.where(lo, dks[0], dks[1])
        dv_blk = jnp.where(lo, dvs[0], dvs[1])
        dk_ref[pl.ds(start, 3 * BLK), :] += dk_blk[:3 * BLK]
        dk_ref[pl.ds(n_lat, m_ctx), :] += dk_blk[3 * BLK:]
        dv_ref[pl.ds(start, 3 * BLK), :] += dv_blk[:3 * BLK]
        dv_ref[pl.ds(n_lat, m_ctx), :] += dv_blk[3 * BLK:]

    qspec = pl.BlockSpec((BLK, 4 * LANES), lambda p, i: (i, p))
    outs, carried = _pcall(
        body, (sink, qkv, qkv, qkv, o, do, lse), name=name, grid=(2, nb),
        in_specs=[SMEM_SPEC, qspec,
                  pl.BlockSpec((r, LANES), lambda p, i: (0, kcol + p)),
                  pl.BlockSpec((r, LANES), lambda p, i: (0, kcol + 2 + p)),
                  qspec, qspec, pl.BlockSpec((None, BLK, 8), lambda p, i: (p, i, 0))],
        out_specs=[qspec, pl.BlockSpec((r, LANES), lambda p, i: (0, p)), pl.BlockSpec((r, LANES), lambda p, i: (0, p)),
                   pl.BlockSpec((None, 8, LANES), lambda p, i: (p, 0, 0))],
        out_shape=[jax.ShapeDtypeStruct((r, D_MODEL), BF16), jax.ShapeDtypeStruct((r, 2 * LANES), F32),
                   jax.ShapeDtypeStruct((r, 2 * LANES), F32), jax.ShapeDtypeStruct((2, 8, LANES), F32)], carry=carry)
    return outs if carry is None else (outs, carried)


def _gating_parts(z_ref, gv_ref):
    za = z_ref[...].astype(F32)
    zg = jax.nn.gelu(za)
    u, v = zg[:, :D_MODEL], zg[:, D_MODEL:]
    rs = lax.rsqrt(jnp.mean(v * v, axis=-1, keepdims=True) + EPS)
    return za, u, v, rs, v * rs * gv_ref[...]


def _mix(w_ref, vals):
    vb = vals.astype(BF16)
    return jnp.concatenate(
        [jnp.dot(w_ref[g], vb[:, g * LANES:(g + 1) * LANES], preferred_element_type=F32) for g in range(A_GROUPS)],
        axis=1)


def _gating_fwd(z, ws, bias, g_v, name):
    r = z.shape[0]

    def body(z_ref, w_ref, b_ref, gv_ref, y_ref):
        _, u, _, _, vn = _gating_parts(z_ref, gv_ref)
        y_ref[...] = (u * (_mix(w_ref, vn) + b_ref[...])).astype(BF16)

    return pl.pallas_call(
        body, name=name, grid=(r // CHUNK,),
        in_specs=[pl.BlockSpec((CHUNK, 2 * D_MODEL), lambda i: (i, P_A // (2 * D_MODEL))),
                  pl.BlockSpec((A_GROUPS, CHUNK, CHUNK), lambda i: (0, 0, 0)),
                  pl.BlockSpec((CHUNK, D_MODEL), lambda i: (0, 0)),
                  pl.BlockSpec((1, D_MODEL), lambda i: (0, 0))],
        out_specs=pl.BlockSpec((CHUNK, D_MODEL), lambda i: (i, 0)),
        out_shape=jax.ShapeDtypeStruct((r, D_MODEL), BF16), compiler_params=_params())(z, ws, bias, g_v)


def _gating_bwd(z, ws, ws_t, bias, g_v, dy, name, carry=None):
    r = z.shape[0]

    def body(z_ref, w_ref, wt_ref, b_ref, gv_ref, dy_ref, dz_ref, dw_ref, db_ref, dg_ref):
        i = pl.program_id(0)

        @pl.when(i == 0)
        def _():
            dw_ref[...] = jnp.zeros_like(dw_ref)
            db_ref[...] = jnp.zeros_like(db_ref)
            dg_ref[...] = jnp.zeros_like(dg_ref)

        za, u, v, rs, vn = _gating_parts(z_ref, gv_ref)
        dyv = dy_ref[...].astype(F32)
        du = dyv * (_mix(w_ref, vn) + b_ref[...])
        dmixed = dyv * u
        dvn = _mix(wt_ref, dmixed)
        dmb, vnb = dmixed.astype(BF16), vn.astype(BF16)
        for g in range(A_GROUPS):
            cols = slice(g * LANES, (g + 1) * LANES)
            dw_ref[g] += lax.dot_general(dmb[:, cols], vnb[:, cols], NT, preferred_element_type=F32)
            db_ref[:, g:g + 1] += jnp.sum(dmixed[:, cols], axis=-1, keepdims=True)
        gv = gv_ref[...]
        vh = v * rs
        dg_ref[0:1, :] += jnp.sum(dvn * vh, axis=0, keepdims=True)
        dvh = dvn * gv
        dv = rs * (dvh - vh * jnp.mean(dvh * vh, axis=-1, keepdims=True))
        _, vjp = jax.vjp(jax.nn.gelu, za)
        dz_ref[...] = vjp(jnp.concatenate([du, dv], axis=1))[0].astype(BF16)

    wspec = pl.BlockSpec((A_GROUPS, CHUNK, CHUNK), lambda i: (0, 0, 0))
    outs, carried = _pcall(
        body, (z, ws, ws_t, bias, g_v, dy), name=name, grid=(r // CHUNK,),
        in_specs=[pl.BlockSpec((CHUNK, 2 * D_MODEL), lambda i: (i, P_A // (2 * D_MODEL))), wspec, wspec,
                  pl.BlockSpec((CHUNK, D_MODEL), lambda i: (0, 0)), pl.BlockSpec((1, D_MODEL), lambda i: (0, 0)),
                  pl.BlockSpec((CHUNK, D_MODEL), lambda i: (i, 0))],
        out_specs=[pl.BlockSpec((CHUNK, 2 * D_MODEL), lambda i: (i, 0)), wspec,
                   pl.BlockSpec((CHUNK, A_GROUPS), lambda i: (0, 0)), pl.BlockSpec((8, D_MODEL), lambda i: (0, 0))],
        out_shape=[jax.ShapeDtypeStruct((r, 2 * D_MODEL), BF16), jax.ShapeDtypeStruct((A_GROUPS, CHUNK, CHUNK), F32),
                   jax.ShapeDtypeStruct((CHUNK, A_GROUPS), F32), jax.ShapeDtypeStruct((8, D_MODEL), F32)], carry=carry)
    return outs if carry is None else (outs, carried)


def _scr_rows(j, n_lat):
    s = pl.multiple_of(j * CONV_CHUNK, CONV_CHUNK)
    shift = jnp.where(j >= n_lat // CONV_CHUNK, 2 * PAD, PAD)
    return s, pl.multiple_of(s + shift, PAD)


def _taps(scr, j, n_lat):
    c = CONV_CHUNK
    s, at = _scr_rows(j, n_lat)
    ext = scr[pl.ds(pl.multiple_of(at - PAD, PAD), c + 2 * PAD), :]
    xm = pltpu.roll(ext, 1, 0)[PAD:PAD + c]
    xp = pltpu.roll(ext, c + 2 * PAD - 1, 0)[PAD:PAD + c]
    return s, xm, ext[PAD:PAD + c], xp


def _zero_pads(scr, r, n_lat):
    for at in (0, PAD + n_lat, 2 * PAD + r):
        scr[pl.ds(at, PAD), :] = jnp.zeros((PAD, LANES), F32)


def _col(arr_cols, c0):
    return pl.BlockSpec((arr_cols, LANES), lambda c: (0, c0 + c))


def _ffn_conv_fwd(up, w, n_lat, name):
    r = up.shape[0]
    nct = D_FF // LANES
    nchunk = r // CONV_CHUNK

    def body(a_ref, g_ref, w_ref, f_ref, scr):
        _zero_pads(scr, r, n_lat)

        def fill(j, _):
            s, at = _scr_rows(j, n_lat)
            scr[pl.ds(at, CONV_CHUNK), :] = a_ref[pl.ds(s, CONV_CHUNK), :].astype(F32)
            return 0
        lax.fori_loop(0, nchunk, fill, 0)
        wv = w_ref[...]

        def step(j, _):
            s, xm, x0, xp = _taps(scr, j, n_lat)
            ca = wv[0:1] * xm + wv[1:2] * x0 + wv[2:3] * xp
            gv = g_ref[pl.ds(s, CONV_CHUNK), :].astype(F32)
            f_ref[pl.ds(s, CONV_CHUNK), :] = (ca * _sigmoid(ca) * gv).astype(BF16)
            return 0
        lax.fori_loop(0, nchunk, step, 0)

    return pl.pallas_call(
        body, name=name, grid=(nct,),
        in_specs=[_col(r, 0), _col(r, nct), _col(3, 0)],
        out_specs=_col(r, 0), out_shape=jax.ShapeDtypeStruct((r, D_FF), BF16),
        scratch_shapes=[pltpu.VMEM((r + 3 * PAD, LANES), F32)], compiler_params=_params())(up, up, w)


def _ffn_conv_bwd(up, w, df, n_lat, name, carry=None):
    r = up.shape[0]
    nct = D_FF // LANES
    nchunk = r // CONV_CHUNK

    def body(a_ref, g_ref, w_ref, df_ref, dup_ref, dw_ref, scr, scr2):
        _zero_pads(scr, r, n_lat)
        _zero_pads(scr2, r, n_lat)

        def fill(j, _):
            s, at = _scr_rows(j, n_lat)
            scr[pl.ds(at, CONV_CHUNK), :] = a_ref[pl.ds(s, CONV_CHUNK), :].astype(F32)
            return 0
        lax.fori_loop(0, nchunk, fill, 0)
        wv = w_ref[...]

        def first(j, carry):
            s, xm, x0, xp = _taps(scr, j, n_lat)
            ca = wv[0:1] * xm + wv[1:2] * x0 + wv[2:3] * xp
            sg = _sigmoid(ca)
            gv = g_ref[pl.ds(s, CONV_CHUNK), :].astype(F32)
            dfv = df_ref[pl.ds(s, CONV_CHUNK), :].astype(F32)
            dup_ref[1, pl.ds(s, CONV_CHUNK), :] = (dfv * ca * sg).astype(BF16)
            dca = dfv * gv * (sg * (1.0 + ca * (1.0 - sg)))
            scr2[pl.ds(_scr_rows(j, n_lat)[1], CONV_CHUNK), :] = dca
            return tuple(cw + jnp.sum(dca * xv, axis=0, keepdims=True) for cw, xv in zip(carry, (xm, x0, xp)))
        zero = jnp.zeros((1, LANES), F32)
        dws = lax.fori_loop(0, nchunk, first, (zero, zero, zero))
        dw_ref[...] = _rows(list(dws) + [None] * 5)

        def second(j, _):
            s, ym, y0, yp = _taps(scr2, j, n_lat)
            dup_ref[0, pl.ds(s, CONV_CHUNK), :] = (wv[0:1] * yp + wv[1:2] * y0 + wv[2:3] * ym).astype(BF16)
            return 0
        lax.fori_loop(0, nchunk, second, 0)

    outs, carried = _pcall(
        body, (up, up, w, df), name=name, grid=(nct,),
        in_specs=[_col(r, 0), _col(r, nct), _col(3, 0), _col(r, 0)],
        out_specs=[pl.BlockSpec((2, r, LANES), lambda c: (0, 0, c)), _col(8, 0)],
        out_shape=[jax.ShapeDtypeStruct((2, r, D_FF), BF16), jax.ShapeDtypeStruct((8, D_FF), F32)],
        scratch_shapes=[pltpu.VMEM((r + 3 * PAD, LANES), F32), pltpu.VMEM((r + 3 * PAD, LANES), F32)], carry=carry)
    return outs if carry is None else (outs, carried)


def _sconv_fwd(z, w, n_lat, name):
    r = z.shape[0]
    nct = D_MODEL // LANES
    nchunk = r // CONV_CHUNK
    c0 = P_B // LANES

    def body(bg_ref, cg_ref, hb_ref, w_ref, y_ref, scr):
        _zero_pads(scr, r, n_lat)

        def fill(j, _):
            s, at = _scr_rows(j, n_lat)
            rows = pl.ds(s, CONV_CHUNK)
            scr[pl.ds(at, CONV_CHUNK), :] = cg_ref[rows, :].astype(F32) * hb_ref[rows, :].astype(F32)
            return 0
        lax.fori_loop(0, nchunk, fill, 0)
        wv = w_ref[...]

        def step(j, _):
            s, xm, x0, xp = _taps(scr, j, n_lat)
            conv = wv[0:1] * xm + wv[1:2] * x0 + wv[2:3] * xp
            y_ref[pl.ds(s, CONV_CHUNK), :] = (bg_ref[pl.ds(s, CONV_CHUNK), :].astype(F32) * conv).astype(BF16)
            return 0
        lax.fori_loop(0, nchunk, step, 0)

    return pl.pallas_call(
        body, name=name, grid=(nct,),
        in_specs=[_col(r, c0), _col(r, c0 + nct), _col(r, c0 + 2 * nct), _col(3, 0)],
        out_specs=_col(r, 0), out_shape=jax.ShapeDtypeStruct((r, D_MODEL), BF16),
        scratch_shapes=[pltpu.VMEM((r + 3 * PAD, LANES), F32)], compiler_params=_params())(z, z, z, w)


def _sconv_bwd(z, w, dy, n_lat, name):
    r = z.shape[0]
    nct = D_MODEL // LANES
    nchunk = r // CONV_CHUNK
    c0 = P_B // LANES

    def body(bg_ref, cg_ref, hb_ref, w_ref, dy_ref, dz_ref, dw_ref, scr, scr2):
        _zero_pads(scr, r, n_lat)
        _zero_pads(scr2, r, n_lat)

        def fill(j, _):
            s, at = _scr_rows(j, n_lat)
            rows = pl.ds(s, CONV_CHUNK)
            scr[pl.ds(at, CONV_CHUNK), :] = cg_ref[rows, :].astype(F32) * hb_ref[rows, :].astype(F32)
            return 0
        lax.fori_loop(0, nchunk, fill, 0)
        wv = w_ref[...]

        def first(j, carry):
            s, xm, x0, xp = _taps(scr, j, n_lat)
            rows = pl.ds(s, CONV_CHUNK)
            conv = wv[0:1] * xm + wv[1:2] * x0 + wv[2:3] * xp
            dyv = dy_ref[rows, :].astype(F32)
            dz_ref[0, rows, :] = (dyv * conv).astype(BF16)
            dconv = dyv * bg_ref[rows, :].astype(F32)
            scr2[pl.ds(_scr_rows(j, n_lat)[1], CONV_CHUNK), :] = dconv
            return tuple(cw + jnp.sum(dconv * xv, axis=0, keepdims=True) for cw, xv in zip(carry, (xm, x0, xp)))
        zero = jnp.zeros((1, LANES), F32)
        dws = lax.fori_loop(0, nchunk, first, (zero, zero, zero))
        dw_ref[...] = _rows(list(dws) + [None] * 5)

        def second(j, _):
            s, ym, y0, yp = _taps(scr2, j, n_lat)
            rows = pl.ds(s, CONV_CHUNK)
            dx = wv[0:1] * yp + wv[1:2] * y0 + wv[2:3] * ym
            dz_ref[1, rows, :] = (dx * hb_ref[rows, :].astype(F32)).astype(BF16)
            dz_ref[2, rows, :] = (dx * cg_ref[rows, :].astype(F32)).astype(BF16)
            return 0
        lax.fori_loop(0, nchunk, second, 0)

    return pl.pallas_call(
        body, name=name, grid=(nct,),
        in_specs=[_col(r, c0), _col(r, c0 + nct), _col(r, c0 + 2 * nct), _col(3, 0), _col(r, 0)],
        out_specs=[pl.BlockSpec((3, r, LANES), lambda c: (0, 0, c)), _col(8, 0)],
        out_shape=[jax.ShapeDtypeStruct((3, r, D_MODEL), BF16), jax.ShapeDtypeStruct((8, D_MODEL), F32)],
        scratch_shapes=[pltpu.VMEM((r + 3 * PAD, LANES), F32), pltpu.VMEM((r + 3 * PAD, LANES), F32)],
        compiler_params=_params())(z, z, z, w, dy)


def _merge_proj_out(z, b_gate, ps, w_out, bm, name):
    r = z.shape[0]
    sub = ROW_TILE
    row = pl.BlockSpec((bm, D_MODEL), lambda i: (i, 0))

    def body(zg_ref, b_ref, p0_ref, p1_ref, p2_ref, w_ref, m_ref, o_ref):
        def chunk(c, _):
            rows = pl.ds(pl.multiple_of(c * sub, sub), sub)
            gates = _sigmoid(zg_ref[rows, :].astype(F32) + b_ref[...])
            acc = None
            for t, p_ref in enumerate((p0_ref, p1_ref, p2_ref)):
                term = gates[:, t * D_MODEL:(t + 1) * D_MODEL] * p_ref[rows, :].astype(F32)
                acc = term if acc is None else acc + term
            m_ref[rows, :] = acc.astype(BF16)
            return 0
        lax.fori_loop(0, bm // sub, chunk, 0)
        o_ref[...] = jnp.dot(m_ref[...], w_ref[...], preferred_element_type=F32).astype(o_ref.dtype)

    return pl.pallas_call(
        body, name=name, grid=(r // bm,),
        in_specs=[pl.BlockSpec((bm, 3 * D_MODEL), lambda i: (i, 0)), pl.BlockSpec((1, 3 * D_MODEL), lambda i: (0, 0)),
                  row, row, row, pl.BlockSpec((D_MODEL, D_MODEL), lambda i: (0, 0))],
        out_specs=[row, row],
        out_shape=[jax.ShapeDtypeStruct((r, D_MODEL), BF16), jax.ShapeDtypeStruct((r, D_MODEL), BF16)],
        compiler_params=_params())(z, b_gate, *ps, w_out)


def _merge_bwd(z, b_gate, ps, dmerged, name):
    r = z.shape[0]
    tm = ROW_TILE
    row = pl.BlockSpec((tm, D_MODEL), lambda i: (i, 0))
    wide = pl.BlockSpec((tm, 3 * D_MODEL), lambda i: (i, 0))

    def body(zg_ref, b_ref, p0_ref, p1_ref, p2_ref, dm_ref, d0_ref, d1_ref, d2_ref, dz_ref, db_ref):
        @pl.when(pl.program_id(0) == 0)
        def _():
            db_ref[...] = jnp.zeros_like(db_ref)

        gates = _sigmoid(zg_ref[...].astype(F32) + b_ref[...])
        dm = dm_ref[...].astype(F32)
        for t, (p_ref, d_ref) in enumerate(((p0_ref, d0_ref), (p1_ref, d1_ref), (p2_ref, d2_ref))):
            cols = slice(t * D_MODEL, (t + 1) * D_MODEL)
            gt = gates[:, cols]
            d_ref[...] = (dm * gt).astype(BF16)
            dlogit = dm * p_ref[...].astype(F32) * gt * (1.0 - gt)
            dz_ref[:, cols] = dlogit.astype(BF16)
            db_ref[0:1, cols] += jnp.sum(dlogit, axis=0, keepdims=True)

    shp = jax.ShapeDtypeStruct((r, D_MODEL), BF16)
    return pl.pallas_call(
        body, name=name, grid=(r // tm,),
        in_specs=[wide, pl.BlockSpec((1, 3 * D_MODEL), lambda i: (0, 0)), row, row, row, row],
        out_specs=[row, row, row, wide, pl.BlockSpec((8, 3 * D_MODEL), lambda i: (0, 0))],
        out_shape=[shp, shp, shp, jax.ShapeDtypeStruct((r, 3 * D_MODEL), BF16),
                   jax.ShapeDtypeStruct((8, 3 * D_MODEL), F32)],
        compiler_params=_params())(z, b_gate, *ps, dmerged)


def _sum_slots(buf, name):
    s, rows, _ = buf.shape
    whole_bytes = s * rows * LANES * 4
    tr = rows if whole_bytes <= VMEM_LIMIT // 8 else _pick(rows, [512, 256, 128, 64, 32, 16, 8])

    def body(b_ref, o_ref):
        acc = b_ref[0]
        for t in range(1, s):
            acc = acc + b_ref[t]
        o_ref[...] = acc

    return pl.pallas_call(
        body, name=name, grid=(rows // tr,),
        in_specs=[pl.BlockSpec((s, tr, LANES), lambda i: (0, i, 0))],
        out_specs=pl.BlockSpec((tr, LANES), lambda i: (i, 0)),
        out_shape=jax.ShapeDtypeStruct((rows, LANES), F32), compiler_params=_params())(buf)


def _adamw(w, gsrcs, m, v, name, scale=None):
    nl, rows, cols = w.shape
    assert len(gsrcs) == nl
    s = gsrcs[0].shape[0]
    tr = _pick(rows, [304, 352, 256, 128, 64, 32, 16, 8])
    blk = pl.BlockSpec((None, tr, cols), lambda l, i: (l, i, 0))
    c1 = 1.0 / (1.0 - ADAM_B1 ** ADAM_STEP)
    c2 = 1.0 / (1.0 - ADAM_B2 ** ADAM_STEP)

    def gspec(t):
        return pl.BlockSpec((s, tr, cols), lambda l, i: (0, jnp.where(l == t, i, 0), 0))

    def body(*refs):
        w_ref, g_refs, (m_ref, v_ref) = refs[0], refs[1:1 + nl], refs[1 + nl:3 + nl]
        rest = refs[3 + nl:]
        if scale is not None:
            sc_ref, rest = rest[0], rest[1:]
        go_ref, d_ref, mo_ref, vo_ref = rest
        layer = pl.program_id(0)
        g = None
        for t in range(nl):
            gt = g_refs[t][0].astype(F32)
            for q in range(1, s):
                gt = gt + g_refs[t][q].astype(F32)
            g = gt if g is None else jnp.where(layer == t, gt, g)
        if scale is not None:
            g = g * sc_ref[...]
        mn = ADAM_B1 * m_ref[...] + (1.0 - ADAM_B1) * g
        vn = ADAM_B2 * v_ref[...] + (1.0 - ADAM_B2) * (g * g)
        go_ref[...] = g
        mo_ref[...] = mn
        vo_ref[...] = vn
        d_ref[...] = -ADAM_LR * ((mn * c1) / (jnp.sqrt(vn * c2) + ADAM_EPS) + ADAM_WD * w_ref[...])

    shp = jax.ShapeDtypeStruct((nl, rows, cols), F32)
    ins = [w] + list(gsrcs) + [m, v] + ([] if scale is None else [scale])
    return pl.pallas_call(
        body, name=name, grid=(nl, rows // tr),
        in_specs=[blk] + [gspec(t) for t in range(nl)] + [blk, blk] + ([] if scale is None else [blk]),
        out_specs=[blk] * 4, out_shape=[shp] * 4, compiler_params=_params())(*ins)


def _pack(arrs):
    flat = []
    for a in arrs:
        a = a.reshape(-1).astype(F32)
        pad = (-a.shape[0]) % (8 * LANES)
        flat.append(jnp.pad(a, (0, pad)) if pad else a)
    return jnp.concatenate(flat).reshape(-1, LANES)


def _unpack(buf, shapes, lead=()):
    out, row = [], 0
    for shp in shapes:
        n = 1
        for d in shp:
            n *= d
        nrows = -(-n // (8 * LANES)) * 8
        piece = buf[..., row:row + nrows, :].reshape(lead + (nrows * LANES,))[..., :n]
        out.append(piece.reshape(lead + tuple(shp)))
        row += nrows
    return out


def _silu(x):
    return x * jax.nn.sigmoid(x)


def _rope_tables(n_lat, m_ctx):
    pos = jnp.arange(n_lat)
    row = (pos // GRID_W).astype(F32)
    col = (pos % GRID_W).astype(F32)
    half = HEAD_DIM // 2
    inv = ROPE_THETA ** (-jnp.arange(0, half, 2, dtype=F32) / half)
    ang = jnp.concatenate([row[:, None] * inv, col[:, None] * inv], axis=-1)
    cos, sin = jnp.cos(ang), jnp.sin(ang)
    cos2 = jnp.tile(jnp.concatenate([cos, cos], axis=-1), (1, LANES // HEAD_DIM))
    sin2 = jnp.tile(jnp.concatenate([-sin, sin], axis=-1), (1, LANES // HEAD_DIM))
    return (jnp.concatenate([cos2, jnp.ones((m_ctx, LANES), F32)], axis=0),
            jnp.concatenate([sin2, jnp.zeros((m_ctx, LANES), F32)], axis=0))


def _to_slots(full, axis):
    shp = full.shape
    new = shp[:axis] + (N_DEV, shp[axis] // N_DEV) + shp[axis + 1:]
    return jnp.moveaxis(full.reshape(new), axis, 0)


def _from_slots(slots, axis):
    moved = jnp.moveaxis(slots, 0, axis)
    shp = moved.shape
    return moved.reshape(shp[:axis] + (shp[axis] * shp[axis + 1],) + shp[axis + 2:])


def _permute_in(wt, name):
    blk = 512
    segs = [(OFF_G, IN_W), (0, OFF_K), (OFF_A, OFF_B), (OFF_B, OFF_G), (OFF_K, OFF_A)]

    def source(j):
        src, at = 0, 0
        for lo, hi in segs:
            n = (hi - lo) // blk
            src = src + jnp.where((j >= at) & (j < at + n), j - at + lo // blk, 0)
            at += n
        return (src, 0)

    def body(i_ref, o_ref):
        o_ref[...] = i_ref[...]

    return pl.pallas_call(
        body, name=name, grid=(IN_W // blk,), in_specs=[pl.BlockSpec((blk, D_MODEL), source)],
        out_specs=pl.BlockSpec((blk, D_MODEL), lambda j: (j, 0)), out_shape=jax.ShapeDtypeStruct(wt.shape, wt.dtype),
        compiler_params=_params())(wt)


def kernel(x, c, ctx, c_ctx, w_mod, b_mod, g_mix, w_in, b_gate, sink, w_spatial, b_spatial, g_v, w_sconv, w_branch, w_out, g_ffn, w_up, w_fconv, w_down, g_final, loss_target, m_c_ctx, m_w_mod, m_b_mod, m_g_mix, m_w_in, m_b_gate, m_sink, m_w_spatial, m_b_spatial, m_g_v, m_w_sconv, m_w_branch, m_w_out, m_g_ffn, m_w_up, m_w_fconv, m_w_down, m_g_final, v_c_ctx, v_w_mod, v_b_mod, v_g_mix, v_w_in, v_b_gate, v_sink, v_w_spatial, v_b_spatial, v_g_v, v_w_sconv, v_w_branch, v_w_out, v_g_ffn, v_w_up, v_w_fconv, v_w_down, v_g_final):
    n_lat, m_ctx = x.shape[1], ctx.shape[1]
    r = n_lat + m_ctx
    me = 4 * lax.axis_index("x") + 2 * lax.axis_index("y") + lax.axis_index("c")
    mod_w = w_mod.shape[2]
    bm = _pick(r, [1408, 768, 256])
    bm_fused = _pick(r, [768, 256])

    tr = lambda a: jnp.swapaxes(a, -1, -2)
    shard_axis = {"in": 0, "br": 1, "out": 0, "up": 0, "dn": 0}
    shards = {}
    for kind, wt in (("in", tr(w_in)), ("br", w_branch), ("out", w_out), ("up", tr(w_up)), ("dn", w_down)):
        wb = wt.astype(BF16)
        for l in range(DEPTH):
            shards[kind, l] = wb[l]
    full = {}

    def arrive(items, got):
        for key, slots in zip(items, got):
            wfull = _from_slots(slots, shard_axis[key[0]])
            full[key] = _permute_in(wfull, f"permute_in{key[1]}") if key[0] == "in" else wfull

    def gather_of(items):
        return [shards[key] for key in items], ["gather"] * len(items)

    small_shapes = [c.shape, w_sconv.shape, w_fconv.shape]
    (g_small,) = _exchange([_pack([c, w_sconv, w_fconv])], ["gather"], "gather_first")
    c_all, sconv_all, fconv_all = _unpack(g_small, small_shapes, lead=(N_DEV,))
    c_all = c_all.reshape(N_DEV, D_MODEL)
    w_sconv_full = _from_slots(sconv_all, 2)
    w_fconv_full = _from_slots(fconv_all, 2)

    act = jnp.concatenate([_silu(c_all), _silu(c_ctx)[None], jnp.zeros((7, D_MODEL), F32)], axis=0)
    mod_part = jnp.stack([_mm(act, w_mod[l], name=f"mod_fwd{l}", bm=16, bn=mod_w, bk=D_MODEL, out_dtype=F32)
                          for l in range(DEPTH)])
    (mod_all,) = _exchange([mod_part], ["gather"], "gather_mod")
    mod_full = _from_slots(mod_all, 2) + b_mod[:, None, :]
    mods = []
    for l in range(DEPTH):
        mine = lax.dynamic_index_in_dim(mod_full[l], me, axis=0, keepdims=False).reshape(6, D_MODEL)
        theirs = mod_full[l, N_DEV].reshape(6, D_MODEL)
        mods.append(jnp.pad(jnp.stack([mine, theirs]), ((0, 0), (0, 2), (0, 0))))

    cos, sin = _rope_tables(n_lat, m_ctx)
    xs = jnp.concatenate([x[0], ctx[0]], axis=0)
    ws_b = w_spatial.astype(BF16)
    ws_t = jnp.swapaxes(w_spatial, 2, 3).astype(BF16)
    vec = lambda a: a.reshape(1, -1)

    saved = []
    res = None
    for l in range(DEPTH):
        s = {}
        if res is None:
            s["x0"] = xs
            s["h"], got = _norm_fwd(xs, vec(g_mix[l]), mods[l], n_lat, 0, 1, f"norm_mix{l}",
                                    carry=gather_of([("in", l)]))
            arrive([("in", l)], got)
        else:
            s["x0"], s["h"] = _norm_fwd(xs, vec(g_mix[l]), mods[l], n_lat, 0, 1, f"norm_mix{l}", res=res)
        items = [("br", l), ("out", l), ("up", l)]
        s["z"], got = _mm(s["h"], full["in", l], tb=True, name=f"proj_in{l}", bm=bm, bn=_pick(IN_W, [2432, 512]),
                          bk=D_MODEL, out_dtype=BF16, cols_outer=True, carry=gather_of(items))
        arrive(items, got)
        s["qkv"] = _rope_fwd(s["z"], cos, sin, f"rope{l}")
        items = [("dn", l)] + ([("in", l + 1)] if l + 1 < DEPTH else [])
        (s["y0"], s["lse"]), got = _attn_fwd(s["qkv"], sink[l], n_lat, m_ctx, f"attn{l}", carry=gather_of(items))
        arrive(items, got)
        s["bias"] = jnp.repeat(b_spatial[l].T, LANES, axis=1)
        s["y1"] = _gating_fwd(s["z"], ws_b[l], s["bias"], vec(g_v[l]), f"gating{l}")
        s["y2"] = _sconv_fwd(s["z"], w_sconv_full[l], n_lat, f"sconv{l}")
        s["p"] = [_mm(s[f"y{t}"], full["br", l], b_lead=t, name=f"branch{l}_{t}", bm=bm, bn=D_MODEL, bk=D_MODEL,
                      out_dtype=BF16) for t in range(3)]
        s["merged"], s["o"] = _merge_proj_out(s["z"], vec(b_gate[l]), s["p"], full["out", l], bm_fused,
                                              f"proj_out{l}")
        s["x1"], s["h2"] = _norm_fwd(s["x0"], vec(g_ffn[l]), mods[l], n_lat, 3, 4, f"norm_ffn{l}",
                                     res=(s["o"], mods[l], 2))
        s["up"] = _mm(s["h2"], full["up", l], tb=True, name=f"ffn_up{l}", bm=bm, bn=_pick(2 * D_FF, [1408]),
                      bk=D_MODEL, out_dtype=BF16, cols_outer=True)
        s["f"] = _ffn_conv_fwd(s["up"], w_fconv_full[l], n_lat, f"ffn_conv{l}")
        s["dd"] = _mm(s["f"], full["dn", l], name=f"ffn_down{l}", bm=bm, bn=D_MODEL, bk=D_FF, out_dtype=BF16)
        saved.append(s)
        xs, res = s["x1"], (s["dd"], mods[l], 5)

    top = saved[DEPTH - 1]
    dxs, d_dd, acc_final, accs_top = _loss_bwd(top["x1"], top["dd"], mods[DEPTH - 1], vec(g_final), loss_target[0],
                                               n_lat, "loss")
    loss_part = acc_final[1, 0]
    dg_final = acc_final[0]
    dmods = [None] * DEPTH
    gate2 = accs_top[:, 2]
    grads = {k: [None] * DEPTH for k in ("g_mix", "g_ffn", "g_v", "b_gate", "sink", "w_spatial", "b_spatial",
                                         "w_sconv", "w_fconv", "w_in", "w_branch", "w_out", "w_up", "w_down")}
    bk_r = _pick(r, [2816, 768, 256])
    small_names = ["g_ffn", "g_v", "b_gate", "sink", "w_spatial", "b_spatial", "w_sconv", "w_fconv"]
    recv = {}
    small_recv, small_shapes_of = {}, {}

    def small_pack(l):
        arrs = [grads[k][l] for k in small_names]
        if l > 0:
            arrs.append(grads["g_mix"][l])
        if l == DEPTH - 1:
            arrs += [loss_part.reshape(1), dg_final]
        small_shapes_of[l] = [a.shape for a in arrs]
        return _pack(arrs)

    for l in reversed(range(DEPTH)):
        s = saved[l]
        df = _mm(d_dd, full["dn", l], tb=True, name=f"d_ffn_down{l}", bm=bm, bn=_pick(D_FF, [1408]), bk=D_MODEL,
                 out_dtype=BF16)
        grads["w_down"][l] = _mm(s["f"], d_dd, ta=True, name=f"g_ffn_down{l}", bm=_pick(D_FF, [1408]), bn=D_MODEL,
                                 bk=bk_r, out_dtype=BF16)
        arrs, modes = [_to_slots(grads["w_down"][l], 0)], ["a2a"]
        if l + 1 < DEPTH:
            arrs.append(small_pack(l + 1))
            modes.append("gather")
        (dup, dwf), got = _ffn_conv_bwd(s["up"], w_fconv_full[l], df, n_lat, f"d_ffn_conv{l}", carry=(arrs, modes))
        recv["dn", l] = got[0]
        if l + 1 < DEPTH:
            small_recv[l + 1] = got[1]
        grads["w_fconv"][l] = dwf[:3]
        dh2 = _mm_pieces([(dup, "step", 0, 2, 0, D_FF)], full["up", l], w_t=True, name=f"d_ffn_up{l}", bm=bm,
                         out_dtype=BF16)
        kb = 1408
        nbh = D_FF // kb
        grads["w_up"][l] = _mm(
            dup, s["h2"], ta=True, name=f"g_ffn_up{l}", bm=kb, bn=D_MODEL, bk=bk_r, out_dtype=BF16,
            a_spec=((None, bk_r, kb), lambda i, j, k: (i // nbh, k, i % nbh), 2 * D_FF))
        dx1, d_o, acc, accs = _norm_bwd(s["x1"], vec(g_ffn[l]), mods[l], dh2, dxs, n_lat, 3, 4, f"d_norm_ffn{l}",
                                        res=(s["o"], mods[l], 2))
        grads["g_ffn"][l] = acc[0]
        shift2, scale2, gate1 = accs[:, 0], accs[:, 1], accs[:, 2]
        dmerged = _mm(d_o, full["out", l], tb=True, name=f"d_proj_out{l}", bm=bm, bn=D_MODEL, bk=D_MODEL,
                      out_dtype=BF16)
        grads["w_out"][l] = _mm(s["merged"], d_o, ta=True, name=f"g_proj_out{l}", bm=D_MODEL, bn=D_MODEL, bk=bk_r,
                                out_dtype=BF16)
        dp0, dp1, dp2, dz_g, dbg = _merge_bwd(s["z"], vec(b_gate[l]), s["p"], dmerged, f"d_merge{l}")
        grads["b_gate"][l] = dbg[0]
        dps = (dp0, dp1, dp2)
        dys = [_mm(dps[t], full["br", l], tb=True, b_lead=t, name=f"d_branch{l}_{t}", bm=bm, bn=D_MODEL, bk=D_MODEL,
                   out_dtype=BF16) for t in range(3)]
        grads["w_branch"][l] = jnp.stack(
            [_mm(s[f"y{t}"], dps[t], ta=True, name=f"g_branch{l}_{t}", bm=D_MODEL, bn=D_MODEL, bk=bk_r,
                 out_dtype=BF16) for t in range(3)])
        arrs = [_to_slots(grads["w_up"][l], 0)] + ([_to_slots(grads["w_in"][l + 1], 0)] if l + 1 < DEPTH else [])
        (dq, dk, dv, dsk), got = _attn_bwd(s["qkv"], sink[l], s["y0"], dys[0], s["lse"], n_lat, m_ctx, f"d_attn{l}",
                                           carry=(arrs, ["a2a"] * len(arrs)))
        recv["up", l] = got[0]
        if l + 1 < DEPTH:
            recv["in", l + 1] = got[1]
        grads["sink"][l] = dsk[:, 0, :8].reshape(N_HEADS)
        dz_qkv = _rope_bwd(dq, dk, dv, cos, sin, f"d_rope{l}")
        (dz_a, dws, dbs, dgv), (recv["out", l], recv["br", l]) = _gating_bwd(
            s["z"], ws_b[l], ws_t[l], s["bias"], vec(g_v[l]), dys[1], f"d_gating{l}",
            carry=([_to_slots(grads["w_out"][l], 0), _to_slots(grads["w_branch"][l], 1)], ["a2a"] * 2))
        grads["w_spatial"][l], grads["b_spatial"][l], grads["g_v"][l] = dws, dbs.T, dgv[0]
        dz_b, dwsc = _sconv_bwd(s["z"], w_sconv_full[l], dys[2], n_lat, f"d_sconv{l}")
        grads["w_sconv"][l] = dwsc[:3]
        kvw = 2 * N_KV_HEADS * HEAD_DIM
        pieces = [(dz_g, None, 0, 3, 0, D_MODEL), (dz_qkv, None, 0, 1, P_Q // D_MODEL, D_MODEL),
                  (dz_a, None, 0, 2, P_A // D_MODEL, D_MODEL), (dz_b, "step", 0, 3, P_B // D_MODEL, D_MODEL),
                  (dz_qkv, None, D_MODEL // kvw, 1, P_KV // kvw, kvw)]
        gw = lambda a, nm, rows, **kw: _mm(a, s["h"], ta=True, name=f"g_proj_in{l}_{nm}", bm=rows, bn=D_MODEL,
                                           bk=bk_r, out_dtype=BF16, **kw)
        gw_g = gw(dz_g, "gate", 1536)
        gw_qkv = gw(dz_qkv, "qkv", 1536)
        gw_a = gw(dz_a, "gating", 1024)
        gw_b = [gw(dz_b, f"sconv{t}", 1024, a_lead=t) for t in range(3)]
        grads["w_in"][l] = jnp.concatenate([gw_qkv, gw_a] + gw_b + [gw_g], axis=0)
        bm_in = _pick(r, [1408, 768, 256])
        if l > 0:
            dh = _mm_pieces(pieces, full["in", l], w_t=True, name=f"d_proj_in{l}", bm=bm_in, out_dtype=BF16)
        else:
            dh, (recv["in", 0], small_recv[0]) = _mm_pieces(
                pieces, full["in", l], w_t=True, name=f"d_proj_in{l}", bm=bm_in, out_dtype=BF16,
                carry=([_to_slots(grads["w_in"][0], 0), small_pack(0)], ["a2a", "gather"]))
        below = None if l == 0 else (saved[l - 1]["dd"], mods[l - 1], 5)
        outs = _norm_bwd(s["x0"], vec(g_mix[l]), mods[l], dh, dx1, n_lat, 0, 1, f"d_norm_mix{l}", res=below)
        if below is None:
            dxs, acc, accs = outs
        else:
            dxs, d_dd, acc, accs = outs
        grads["g_mix"][l] = acc[0]
        dmods[l] = jnp.stack([accs[:, 0], accs[:, 1], gate1, shift2, scale2, gate2], axis=1)
        gate2 = accs[:, 2]

    dmod_own = jnp.stack([dmods[l][0].reshape(-1) for l in range(DEPTH)])
    dmod_ctx = jnp.stack([dmods[l][1].reshape(-1) for l in range(DEPTH)])
    late = [grads["g_mix"][0], dmod_own + dmod_ctx, dmod_ctx, dmod_own]
    late_shapes = [a.shape for a in late]
    (late_all,) = _exchange([_pack(late)], ["gather"], "gather_late_grads")
    g_mix0_g, b_mod_g, dmodc_tot, _ = _unpack(_sum_slots(late_all, "sum_late_grads"), late_shapes)
    dmod_all = _unpack(late_all, late_shapes, lead=(N_DEV,))[-1]
    layer_sums = [_unpack(_sum_slots(small_recv[l], f"sum_small_grads{l}"), small_shapes_of[l]) for l in range(DEPTH)]
    by_name = {k: jnp.stack([layer_sums[l][t] for l in range(DEPTH)]) for t, k in enumerate(small_names)}
    g_mix_g = jnp.stack([g_mix0_g] + [layer_sums[l][len(small_names)] for l in range(1, DEPTH)])
    loss_sum, g_final_g = layer_sums[DEPTH - 1][-2], layer_sums[DEPTH - 1][-1]
    g_ffn_g, g_v_g, b_gate_g, sink_g = by_name["g_ffn"], by_name["g_v"], by_name["b_gate"], by_name["sink"]
    w_spatial_g, b_spatial_g = by_name["w_spatial"], by_name["b_spatial"]
    w_sconv_g = lax.dynamic_slice_in_dim(by_name["w_sconv"], me * w_sconv.shape[2], w_sconv.shape[2], axis=2)
    w_fconv_g = lax.dynamic_slice_in_dim(by_name["w_fconv"], me * w_fconv.shape[2], w_fconv.shape[2], axis=2)

    dmod_cols = lax.dynamic_slice_in_dim(dmod_all, me * mod_w, mod_w, axis=2)
    dmodc_cols = lax.dynamic_slice_in_dim(dmodc_tot, me * mod_w, mod_w, axis=1)
    g_w_mod, cctx_part = [], None
    for l in range(DEPTH):
        rhs = jnp.concatenate([dmod_cols[:, l], dmodc_cols[l][None], jnp.zeros((7, mod_w), F32)], axis=0)
        g_w_mod.append(_mm(act, rhs, ta=True, name=f"g_mod{l}", bm=D_MODEL, bn=mod_w, bk=16, out_dtype=F32))
        lhs = jnp.pad(dmodc_cols[l][None], ((0, 7), (0, 0)))
        part = _mm(lhs, w_mod[l], tb=True, name=f"d_cctx{l}", bm=8, bn=D_MODEL, bk=mod_w, out_dtype=F32)
        cctx_part = part if cctx_part is None else cctx_part + part
    sg = jax.nn.sigmoid(c_ctx)
    dsilu = (sg * (1.0 + c_ctx * (1.0 - sg))).reshape(8, LANES)

    (r_cctx,) = _exchange([cctx_part[0].reshape(8, LANES)], ["gather"], "gather_c_ctx_grad")

    def per_layer(a):
        return a.reshape(a.shape[0], -1, a.shape[-1])

    upd = {}
    for nm, kind, wv, mv, vv in (("w_in", "in", w_in, m_w_in, v_w_in), ("w_branch", "br", w_branch, m_w_branch, v_w_branch),
                                 ("w_out", "out", w_out, m_w_out, v_w_out), ("w_up", "up", w_up, m_w_up, v_w_up),
                                 ("w_down", "dn", w_down, m_w_down, v_w_down), ("w_mod", None, w_mod, m_w_mod, v_w_mod)):
        if kind is None:
            gsrcs = [g[None] for g in g_w_mod]
        else:
            gsrcs = [per_layer(recv[kind, l]) for l in range(DEPTH)]
        if kind in ("in", "up"):
            outs = _adamw(tr(wv), gsrcs, tr(mv), tr(vv), f"adamw_{nm}")
            upd[nm] = [tr(o) for o in outs]
        else:
            outs = _adamw(per_layer(wv), gsrcs, per_layer(mv), per_layer(vv), f"adamw_{nm}")
            upd[nm] = [o.reshape(wv.shape) for o in outs]
    as_tile = lambda a: a.reshape(1, 8, LANES)
    upd["c_ctx"] = [o.reshape(D_MODEL) for o in _adamw(
        as_tile(c_ctx), [r_cctx], as_tile(m_c_ctx), as_tile(v_c_ctx), "adamw_c_ctx", scale=as_tile(dsilu))]

    names = ["b_mod", "g_mix", "b_gate", "sink", "w_spatial", "b_spatial", "g_v", "w_sconv", "g_ffn", "w_fconv",
             "g_final"]
    w_s = [b_mod, g_mix, b_gate, sink, w_spatial, b_spatial, g_v, w_sconv, g_ffn, w_fconv, g_final]
    g_s = [b_mod_g, g_mix_g, b_gate_g, sink_g, w_spatial_g, b_spatial_g, g_v_g, w_sconv_g, g_ffn_g, w_fconv_g,
           g_final_g]
    m_s = [m_b_mod, m_g_mix, m_b_gate, m_sink, m_w_spatial, m_b_spatial, m_g_v, m_w_sconv, m_g_ffn, m_w_fconv,
           m_g_final]
    v_s = [v_b_mod, v_g_mix, v_b_gate, v_sink, v_w_spatial, v_b_spatial, v_g_v, v_w_sconv, v_g_ffn, v_w_fconv,
           v_g_final]
    shapes = [a.shape for a in w_s]
    packed = _adamw(_pack(w_s)[None], [_pack(g_s)[None]], _pack(m_s)[None], _pack(v_s)[None], "adamw_small")
    unpacked = [_unpack(o[0], shapes) for o in packed]
    for t, nm in enumerate(names):
        upd[nm] = [unpacked[q][t] for q in range(4)]

    order = ["c_ctx", "w_mod", "b_mod", "g_mix", "w_in", "b_gate", "sink", "w_spatial", "b_spatial", "g_v", "w_sconv",
             "w_branch", "w_out", "g_ffn", "w_up", "w_fconv", "w_down", "g_final"]
    result = [loss_sum.reshape(()), dxs[None]]
    for q in range(4):
        result += [upd[nm][q] for nm in order]
    return tuple(result)
```

```python
import jax
import jax.numpy as jnp
from jax import lax
from jax.experimental import pallas as pl
from jax.experimental.pallas import tpu as pltpu

F32, BF16 = jnp.float32, jnp.bfloat16

D_MODEL = 1024
DEPTH = 2
GRID_W = 64
N_HEADS = 16
N_KV_HEADS = 4
HEAD_DIM = 64
WINDOW = 128
BLK = 128
ROPE_THETA = 10000.0
CHUNK = 128
A_GROUPS = 8
D_FF = 2816
EPS = 1e-6
NEG = -1e30
IN_W = 9728
OFF_K, OFF_A, OFF_B, OFF_G = 1024, 1536, 3584, 6656
P_Q, P_A, P_B, P_KV = 3072, 4096, 6144, 9216

N_DEV = 8
LANES = 128
ROW_TILE = 256
CONV_CHUNK = 256
PAD = 8
VMEM_LIMIT = 52 * 1024 * 1024

ADAM_LR, ADAM_B1, ADAM_B2, ADAM_EPS, ADAM_WD, ADAM_STEP = 0.001, 0.9, 0.999, 1e-08, 0.01, 10

HBM_SPEC = pl.BlockSpec(memory_space=pltpu.HBM)
SMEM_SPEC = pl.BlockSpec(memory_space=pltpu.SMEM)


def _params():
    return pltpu.CompilerParams(vmem_limit_bytes=VMEM_LIMIT)


def _pick(n, prefs):
    for p in prefs:
        if n % p == 0:
            return p
    raise ValueError((n, prefs))


def _sigmoid(x):
    return 0.5 * jnp.tanh(0.5 * x) + 0.5


def _mm(a, b, *, name, ta=False, tb=False, bm, bn, bk, out_dtype, a_lead=None, b_lead=None, a_spec=None,
        b_spec=None, cols_outer=False, carry=None):
    ash = a.shape[1:] if a_lead is not None else a.shape
    bsh = b.shape[1:] if b_lead is not None else b.shape
    kc = (bsh[1] if tb else bsh[0]) if b_spec is None else (ash[0] if ta else ash[1])
    mo =(ash[1] if ta else ash[0]) if a_spec is None else a_spec[2]
    no = (bsh[0] if tb else bsh[1]) if b_spec is None else b_spec[2]
    assert mo % bm == 0 and no % bn == 0 and kc % bk == 0, (name, mo, no, kc, bm, bn, bk)
    nk = kc // bk

    def spec(shape, fn, idx=None):
        if idx is not None:
            shape, inner = (None,) + shape, fn
            fn = lambda i, j, k: (idx,) + inner(i, j, k)
        if cols_outer:
            return pl.BlockSpec(shape, lambda j, i, k: fn(i, j, k))
        return pl.BlockSpec(shape, fn)

    if a_spec is not None:
        a_bs = spec(a_spec[0], a_spec[1])
    elif ta:
        a_bs = spec((bk, bm), lambda i, j, k: (k, i), a_lead)
    else:
        a_bs = spec((bm, bk), lambda i, j, k: (i, k), a_lead)
    if b_spec is not None:
        b_bs = spec(b_spec[0], b_spec[1])
    elif tb:
        b_bs = spec((bn, bk), lambda i, j, k: (j, k), b_lead)
    else:
        b_bs = spec((bk, bn), lambda i, j, k: (k, j), b_lead)
    dims = (((0 if ta else 1,), (1 if tb else 0,)), ((), ()))
    grid = (no // bn, mo // bm, nk) if cols_outer else (mo // bm, no // bn, nk)

    def body(a_ref, b_ref, o_ref, *scratch):
        if nk == 1:
            o_ref[...] = lax.dot_general(a_ref[...], b_ref[...], dims, preferred_element_type=F32).astype(o_ref.dtype)
        else:
            acc = scratch[0]
            k = pl.program_id(2)

            @pl.when(k == 0)
            def _():
                acc[...] = jnp.zeros_like(acc)

            acc[...] += lax.dot_general(a_ref[...], b_ref[...], dims, preferred_element_type=F32)

            @pl.when(k == nk - 1)
            def _():
                o_ref[...] = acc[...].astype(o_ref.dtype)

    outs, carried = _pcall(
        body, (a, b), name=name, grid=grid, in_specs=[a_bs, b_bs],
        out_specs=[spec((bm, bn), lambda i, j, k: (i, j))],
        out_shape=[jax.ShapeDtypeStruct((mo, no), out_dtype)],
        scratch_shapes=[pltpu.VMEM((bm, bn), F32)] if nk > 1 else [], carry=carry)
    return outs[0] if carry is None else (outs[0], carried)


def _mm_pieces(pieces, w, *, name, bm, w_t=False, out_dtype=F32, carry=None):
    kout = w.shape[1] if w_t else w.shape[0]
    starts, total = [], 0
    for piece in pieces:
        starts.append(total)
        total += piece[3]
    mo = pieces[0][0].shape[-2]
    assert mo % bm == 0
    widths = sorted({piece[5] for piece in pieces}, reverse=True)

    def inside(p, k):
        return (k >= starts[p]) & (k < starts[p] + pieces[p][3])

    def a_spec(p):
        _, lead, col0, nblk, _, bk = pieces[p]

        def fn(i, k):
            t = jnp.clip(k - starts[p], 0, nblk - 1)
            if lead is None:
                return (i, col0 + t)
            return (t, i, col0) if lead == "step" else (lead, i, col0 + t)
        return pl.BlockSpec((bm, bk) if lead is None else (None, bm, bk), fn)

    def w_spec(bk):
        def fn(i, k):
            col = 0
            for p, piece in enumerate(pieces):
                if piece[5] == bk:
                    col = col + jnp.where(inside(p, k), piece[4] + k - starts[p], 0)
            return (col, 0) if w_t else (0, col)
        return pl.BlockSpec((bk, kout) if w_t else (kout, bk), fn)

    n_p, n_w = len(pieces), len(widths)

    def body(*refs):
        a_refs, w_refs, o_ref, acc = refs[:n_p], refs[n_p:n_p + n_w], refs[n_p + n_w], refs[n_p + n_w + 1]
        k = pl.program_id(1)

        @pl.when(k == 0)
        def _():
            acc[...] = jnp.zeros_like(acc)

        for p in range(n_p):
            w_ref = w_refs[widths.index(pieces[p][5])]

            @pl.when(inside(p, k))
            def _(p=p, w_ref=w_ref):
                acc[...] += lax.dot_general(a_refs[p][...], w_ref[...], NN if w_t else NT,
                                            preferred_element_type=F32)

        @pl.when(k == total - 1)
        def _():
            o_ref[...] = acc[...].astype(o_ref.dtype)

    outs, carried = _pcall(
        body, [piece[0] for piece in pieces] + [w] * n_w, name=name, grid=(mo // bm, total),
        in_specs=[a_spec(p) for p in range(n_p)] + [w_spec(bk) for bk in widths],
        out_specs=[pl.BlockSpec((bm, kout), lambda i, k: (i, 0))],
        out_shape=[jax.ShapeDtypeStruct((mo, kout), out_dtype)],
        scratch_shapes=[pltpu.VMEM((bm, kout), F32)], carry=carry)
    return outs[0] if carry is None else (outs[0], carried)


def _xchg_out_shapes(arrs, modes):
    return [jax.ShapeDtypeStruct((N_DEV,) + a.shape if m == "gather" else a.shape, a.dtype)
            for a, m in zip(arrs, modes)]


def _xchg_sems(n):
    return [pltpu.SemaphoreType.DMA((n, N_DEV - 1)), pltpu.SemaphoreType.DMA((n, N_DEV - 1)),
            pltpu.SemaphoreType.DMA((n,))]


def _xchg_copies(ins, outs, modes, sems):
    send_sems, recv_sems, local_sems = sems
    x, y, c = lax.axis_index("x"), lax.axis_index("y"), lax.axis_index("c")
    me = 4 * x + 2 * y + c

    def place(q):
        px = 1 - x if (q >> 2) & 1 else x
        py = 1 - y if (q >> 1) & 1 else y
        pc = 1 - c if q & 1 else c
        return (px, py, pc), 4 * px + 2 * py + pc

    sibling, _ = place(1)
    out = dict(local=[], direct=[], landed=[], passed=[], others=[])
    for t, mode in enumerate(modes):
        gather = mode == "gather"
        mine = ins[t] if gather else ins[t].at[me]
        out["local"].append(pltpu.make_async_copy(mine, outs[t].at[me], local_sems.at[t]))
        for q in range(1, N_DEV):
            peer, slot = place(q)
            sem = dict(send_sem=send_sems.at[t, q - 1], recv_sem=recv_sems.at[t, q - 1],
                       device_id_type=pl.DeviceIdType.MESH)
            arrival = pltpu.make_async_remote_copy(src_ref=outs[t].at[slot], dst_ref=outs[t].at[slot],
                                                   device_id=peer, **sem)
            if not gather:
                out["direct"].append(pltpu.make_async_remote_copy(src_ref=ins[t].at[slot], dst_ref=outs[t].at[me],
                                                                  device_id=peer, **sem))
                out["others"].append(arrival)
            elif q == 1 or q % 2 == 0:
                out["direct"].append(pltpu.make_async_remote_copy(src_ref=ins[t], dst_ref=outs[t].at[me],
                                                                  device_id=peer, **sem))
                out["others" if q == 1 else "landed"].append(arrival)
            else:
                _, origin = place(q - 1)
                out["passed"].append(pltpu.make_async_remote_copy(src_ref=outs[t].at[origin], dst_ref=outs[t].at[origin],
                                                                  device_id=sibling, **sem))
                out["others"].append(arrival)
    return out


def _xchg_start(copies):
    for cp in copies["local"] + copies["direct"]:
        cp.start()


def _xchg_wait(copies):
    for cp in copies["landed"]:
        cp.wait_recv()
    for cp in copies["passed"]:
        cp.start()
    for cp in copies["others"]:
        cp.wait_recv()
    for cp in copies["direct"] + copies["passed"]:
        cp.wait_send()
    for cp in copies["local"]:
        cp.wait()


def _exchange(arrs, modes, name):
    n = len(arrs)

    def body(*refs):
        copies = _xchg_copies(refs[:n], refs[n:2 * n], modes, refs[2 * n:])
        _xchg_start(copies)
        _xchg_wait(copies)

    outs = pl.pallas_call(
        body, name=name, in_specs=[HBM_SPEC] * n, out_specs=[HBM_SPEC] * n, out_shape=_xchg_out_shapes(arrs, modes),
        scratch_shapes=_xchg_sems(n), compiler_params=pltpu.CompilerParams(has_side_effects=True),
    )(*arrs)
    return list(outs)


def _pcall(body, operands, *, name, grid, in_specs, out_specs, out_shape, scratch_shapes=(), carry=None):
    out_specs, out_shape, scratch_shapes = list(out_specs), list(out_shape), list(scratch_shapes)
    if carry is None:
        outs = pl.pallas_call(body, name=name, grid=grid, in_specs=in_specs, out_specs=out_specs,
                              out_shape=out_shape, scratch_shapes=scratch_shapes, compiler_params=_params())(*operands)
        return list(outs), []
    arrs, modes = carry
    n, n_in, n_out, n_scr = len(arrs), len(in_specs), len(out_specs), len(scratch_shapes)

    def wrapped(*refs):
        ins, c_in = refs[:n_in], refs[n_in:n_in + n]
        outs, c_out = refs[n_in + n:n_in + n + n_out], refs[n_in + n + n_out:n_in + 2 * n + n_out]
        rest = refs[n_in + 2 * n + n_out:]
        scr, sems = rest[:n_scr], rest[n_scr:]
        first, last = None, None
        for d, size in enumerate(grid):
            f, e = pl.program_id(d) == 0, pl.program_id(d) == size - 1
            first = f if first is None else first & f
            last = e if last is None else last & e

        @pl.when(first)
        def _():
            _xchg_start(_xchg_copies(c_in, c_out, modes, sems))

        body(*ins, *outs, *scr)

        @pl.when(last)
        def _():
            _xchg_wait(_xchg_copies(c_in, c_out, modes, sems))

    outs = pl.pallas_call(
        wrapped, name=name, grid=grid, in_specs=list(in_specs) + [HBM_SPEC] * n,
        out_specs=out_specs + [HBM_SPEC] * n, out_shape=out_shape + _xchg_out_shapes(arrs, modes),
        scratch_shapes=scratch_shapes + _xchg_sems(n), compiler_params=_params())(*operands, *arrs)
    return list(outs[:n_out]), list(outs[n_out:])


def _row_specs(r, n_lat, tm):
    nbl = n_lat // tm
    row = pl.BlockSpec((tm, D_MODEL), lambda i: (i, 0))
    mod = pl.BlockSpec((None, 8, D_MODEL), lambda i: (i // nbl, 0, 0))
    vec = pl.BlockSpec((1, D_MODEL), lambda i: (0, 0))
    return nbl, row, mod, vec


def _rows(vals):
    width = [v for v in vals if v is not None][0].shape[1]
    return jnp.concatenate([jnp.zeros((1, width), F32) if v is None else v for v in vals], axis=0)


def _norm_fwd(xs, g, mods, n_lat, sh, sc, name, res=None, carry=None):
    r = xs.shape[0]
    tm = ROW_TILE
    nbl, row, mod, vec = _row_specs(r, n_lat, tm)

    def norm(x, g_ref, m_ref, h_ref):
        rs = lax.rsqrt(jnp.mean(x * x, axis=-1, keepdims=True) + EPS)
        m = m_ref[...]
        h_ref[...] = ((x * rs * g_ref[...]) * (1.0 + m[sc:sc + 1]) + m[sh:sh + 1]).astype(BF16)

    if res is None:
        def body(x_ref, g_ref, m_ref, h_ref):
            norm(x_ref[...], g_ref, m_ref, h_ref)
        outs, carried = _pcall(body, (xs, g, mods), name=name, grid=(r // tm,), in_specs=[row, vec, mod],
                               out_specs=[row], out_shape=[jax.ShapeDtypeStruct((r, D_MODEL), BF16)], carry=carry)
        return outs[0] if carry is None else (outs[0], carried)

    o, mods_res, gt = res

    def body(x_ref, o_ref, mr_ref, g_ref, m_ref, x1_ref, h_ref):
        x = x_ref[...] + mr_ref[...][gt:gt + 1] * o_ref[...]
        x1_ref[...] = x
        norm(x, g_ref, m_ref, h_ref)

    return pl.pallas_call(
        body, name=name, grid=(r // tm,), in_specs=[row, row, mod, vec, mod], out_specs=[row, row],
        out_shape=[jax.ShapeDtypeStruct((r, D_MODEL), F32), jax.ShapeDtypeStruct((r, D_MODEL), BF16)],
        compiler_params=_params())(xs, o, mods_res, g, mods)


def _rms_bwd(x, g, dy):
    rs = lax.rsqrt(jnp.mean(x * x, axis=-1, keepdims=True) + EPS)
    xh = x * rs
    dxh = dy * g
    dx = rs * (dxh - xh * jnp.mean(dxh * xh, axis=-1, keepdims=True))
    return dx, dy * xh, xh


def _acc_specs(nbl):
    acc_all = pl.BlockSpec((8, D_MODEL), lambda i: (0, 0))
    acc_stream = pl.BlockSpec((None, 8, D_MODEL), lambda i: (i // nbl, 0, 0))
    return acc_all, acc_stream


def _loss_bwd(xs, o, mods, g_final, target, n_lat, name):
    r = xs.shape[0]
    tm = ROW_TILE
    nbl, row, mod, vec = _row_specs(r, n_lat, tm)
    acc_all, acc_stream = _acc_specs(nbl)
    tgt = pl.BlockSpec((tm, D_MODEL), lambda i: (jnp.minimum(i, nbl - 1), 0))

    def body(x_ref, o_ref, m_ref, g_ref, t_ref, dx_ref, do_ref, acc_ref, accs_ref):
        i = pl.program_id(0)
        lat = i < nbl
        gate = m_ref[...][5:6]
        o_val = o_ref[...]
        x = x_ref[...] + gate * o_val
        g = g_ref[...]
        rs = lax.rsqrt(jnp.mean(x * x, axis=-1, keepdims=True) + EPS)
        y = x * rs * g
        err = jnp.where(lat, y - t_ref[...], 0.0)
        loss = 0.5 * jnp.sum(jnp.mean(err * err, axis=-1, keepdims=True), axis=0, keepdims=True)
        dy = err * (1.0 / D_MODEL)
        dx, dg_rows, _ = _rms_bwd(x, g, dy)
        dx_ref[...] = dx
        do_ref[...] = (gate * dx).astype(BF16)

        @pl.when(i == 0)
        def _():
            acc_ref[...] = jnp.zeros_like(acc_ref)

        @pl.when((i == 0) | (i == nbl))
        def _():
            accs_ref[...] = jnp.zeros_like(accs_ref)

        acc_ref[...] += _rows([jnp.sum(dg_rows, axis=0, keepdims=True), jnp.broadcast_to(loss, (1, D_MODEL))]
                              + [None] * 6)
        accs_ref[...] += _rows([None, None, jnp.sum(dx * o_val, axis=0, keepdims=True)] + [None] * 5)

    return pl.pallas_call(
        body, name=name, grid=(r // tm,), in_specs=[row, row, mod, vec, tgt],
        out_specs=[row, row, acc_all, acc_stream],
        out_shape=[jax.ShapeDtypeStruct((r, D_MODEL), F32), jax.ShapeDtypeStruct((r, D_MODEL), BF16),
                   jax.ShapeDtypeStruct((8, D_MODEL), F32), jax.ShapeDtypeStruct((2, 8, D_MODEL), F32)],
        compiler_params=_params())(xs, o, mods, g_final, target)


def _norm_bwd(xs, g, mods, dh, dx_in, n_lat, sh, sc, name, res=None):
    r = xs.shape[0]
    tm = ROW_TILE
    nbl, row, mod, vec = _row_specs(r, n_lat, tm)
    acc_all, acc_stream = _acc_specs(nbl)
    has_res = res is not None

    def body(*refs):
        if has_res:
            x_ref, g_ref, m_ref, dh_ref, dxi_ref, o_ref, mr_ref, dx_ref, do_ref, acc_ref, accs_ref = refs
        else:
            x_ref, g_ref, m_ref, dh_ref, dxi_ref, dx_ref, acc_ref, accs_ref = refs
        i = pl.program_id(0)
        x, g, m, dhv = x_ref[...], g_ref[...], m_ref[...], dh_ref[...].astype(F32)
        dy = dhv * (1.0 + m[sc:sc + 1])
        dxn, dg_rows, xh = _rms_bwd(x, g, dy)
        dx = dxi_ref[...] + dxn
        if has_res:
            dx_ref[...] = dx
        else:
            @pl.when(i < nbl)
            def _():
                dx_ref[...] = dx
        d_gate = None
        if has_res:
            o_val = o_ref[...]
            do_ref[...] = (mr_ref[...][res[2]:res[2] + 1] * dx).astype(BF16)
            d_gate = jnp.sum(dx * o_val, axis=0, keepdims=True)

        @pl.when(i == 0)
        def _():
            acc_ref[...] = jnp.zeros_like(acc_ref)

        @pl.when((i == 0) | (i == nbl))
        def _():
            accs_ref[...] = jnp.zeros_like(accs_ref)

        acc_ref[...] += _rows([jnp.sum(dg_rows, axis=0, keepdims=True)] + [None] * 7)
        accs_ref[...] += _rows([jnp.sum(dhv, axis=0, keepdims=True),
                                jnp.sum(dhv * (xh * g), axis=0, keepdims=True), d_gate] + [None] * 5)

    ins = [xs, g, mods, dh, dx_in]
    in_specs = [row, vec, mod, row, row]
    if has_res:
        out_specs, out_shape = [row], [jax.ShapeDtypeStruct((r, D_MODEL), F32)]
    else:
        out_specs = [pl.BlockSpec((tm, D_MODEL), lambda i: (jnp.minimum(i, nbl - 1), 0))]
        out_shape = [jax.ShapeDtypeStruct((n_lat, D_MODEL), F32)]
    if has_res:
        ins += [res[0], res[1]]
        in_specs += [row, mod]
        out_specs.append(row)
        out_shape.append(jax.ShapeDtypeStruct((r, D_MODEL), BF16))
    out_specs += [acc_all, acc_stream]
    out_shape += [jax.ShapeDtypeStruct((8, D_MODEL), F32), jax.ShapeDtypeStruct((2, 8, D_MODEL), F32)]
    return pl.pallas_call(body, name=name, grid=(r // tm,), in_specs=in_specs, out_specs=out_specs,
                          out_shape=out_shape, compiler_params=_params())(*ins)


def _rotate(t, cos, sin):
    width = t.shape[1]
    reps = width // LANES
    lane = lax.broadcasted_iota(jnp.int32, (1, width), 1)
    first = (lane % HEAD_DIM) < (HEAD_DIM // 2)
    swapped = jnp.where(first, pltpu.roll(t, width - HEAD_DIM // 2, 1), pltpu.roll(t, HEAD_DIM // 2, 1))
    return t * jnp.tile(cos, (1, reps)) + swapped * jnp.tile(sin, (1, reps))


def _rope_fwd(z, cos, sin, name):
    r = z.shape[0]
    tm = ROW_TILE
    kvw = 2 * N_KV_HEADS * HEAD_DIM
    tab = pl.BlockSpec((tm, LANES), lambda i: (i, 0))

    def body(q_ref, kv_ref, c_ref, s_ref, o_ref):
        c, s = c_ref[...], s_ref[...]
        kv = kv_ref[...]
        o_ref[:, :D_MODEL] = (_rotate(q_ref[...].astype(F32), c, s) * (HEAD_DIM ** -0.5)).astype(BF16)
        o_ref[:, D_MODEL:D_MODEL + kvw // 2] = _rotate(kv[:, :kvw // 2].astype(F32), c, s).astype(BF16)
        o_ref[:, D_MODEL + kvw // 2:] = kv[:, kvw // 2:]

    return pl.pallas_call(
        body, name=name, grid=(r // tm,),
        in_specs=[pl.BlockSpec((tm, D_MODEL), lambda i: (i, P_Q // D_MODEL)),
                  pl.BlockSpec((tm, kvw), lambda i: (i, P_KV // kvw)), tab, tab],
        out_specs=pl.BlockSpec((tm, D_MODEL + kvw), lambda i: (i, 0)),
        out_shape=jax.ShapeDtypeStruct((r, D_MODEL + kvw), BF16), compiler_params=_params())(z, z, cos, sin)


def _rope_bwd(dq, dk, dv, cos, sin, name):
    r = dq.shape[0]
    tm = ROW_TILE
    kw = N_KV_HEADS * HEAD_DIM
    tab = pl.BlockSpec((tm, LANES), lambda i: (i, 0))

    def body(dq_ref, dk_ref, dv_ref, c_ref, s_ref, o_ref):
        c, s = c_ref[...], -s_ref[...]
        o_ref[:, :D_MODEL] = (_rotate(dq_ref[...].astype(F32), c, s) * (HEAD_DIM ** -0.5)).astype(BF16)
        o_ref[:, D_MODEL:D_MODEL + kw] = _rotate(dk_ref[...], c, s).astype(BF16)
        o_ref[:, D_MODEL + kw:] = dv_ref[...].astype(BF16)

    return pl.pallas_call(
        body, name=name, grid=(r // tm,),
        in_specs=[pl.BlockSpec((tm, D_MODEL), lambda i: (i, 0)), pl.BlockSpec((tm, kw), lambda i: (i, 0)),
                  pl.BlockSpec((tm, kw), lambda i: (i, 0)), tab, tab],
        out_specs=pl.BlockSpec((tm, D_MODEL + 2 * kw), lambda i: (i, 0)),
        out_shape=jax.ShapeDtypeStruct((r, D_MODEL + 2 * kw), BF16), compiler_params=_params())(dq, dk, dv, cos, sin)


def _attn_setup(i, n_lat, m_ctx, nbl, k_ref, v_ref):
    start = pl.multiple_of(jnp.clip((i - 1) * BLK, 0, n_lat - 3 * BLK), BLK)
    nkeys = 3 * BLK + m_ctx
    rows = lax.broadcasted_iota(jnp.int32, (4 * BLK, nkeys), 0)
    cols = lax.broadcasted_iota(jnp.int32, (4 * BLK, nkeys), 1)
    qpos = i * BLK + (rows & (BLK - 1))
    seen = (cols >= 3 * BLK) | ((jnp.abs(start + cols - qpos) <= WINDOW) & (i < nbl))
    mask = jnp.where(seen, 0.0, NEG)
    kblk = jnp.concatenate([k_ref[pl.ds(start, 3 * BLK), :], k_ref[pl.ds(n_lat, m_ctx), :]], axis=0)
    vblk = jnp.concatenate([v_ref[pl.ds(start, 3 * BLK), :], v_ref[pl.ds(n_lat, m_ctx), :]], axis=0)
    lo = lax.broadcasted_iota(jnp.int32, (1, LANES), 1) < HEAD_DIM
    return start, mask, kblk, vblk, lo


def _stack_heads(ref, kh, lo):
    a = ref[:, (2 * kh) * LANES:(2 * kh + 1) * LANES]
    b = ref[:, (2 * kh + 1) * LANES:(2 * kh + 2) * LANES]
    z = jnp.zeros_like(a)
    return jnp.concatenate([jnp.where(lo, a, z), jnp.where(lo, z, a), jnp.where(lo, b, z), jnp.where(lo, z, b)],
                           axis=0)


def _kv_variants(blk, rolled, kh, lo):
    z = jnp.zeros_like(blk)
    if kh == 0:
        return jnp.where(lo, blk, rolled), jnp.where(lo, blk, z), jnp.where(lo, z, rolled)
    return jnp.where(lo, rolled, blk), jnp.where(lo, rolled, z), jnp.where(lo, z, blk)


NN = (((1,), (0,)), ((), ()))
NT = (((1,), (1,)), ((), ()))
TN = (((0,), (0,)), ((), ()))


def _scores(qst, k2, mask, sink_ref, p, kh):
    s = lax.dot_general(qst, k2, NT, preferred_element_type=F32) + mask
    snk = jnp.concatenate([jnp.full((BLK, 1), sink_ref[p * 8 + kh * 4 + g], F32) for g in range(4)], axis=0)
    return s, snk


def _attn_fwd(qkv, sink, n_lat, m_ctx, name, carry=None):
    r = qkv.shape[0]
    nb, nbl = r // BLK, n_lat // BLK
    kcol = D_MODEL // LANES

    def body(sink_ref, q_ref, k_ref, v_ref, o_ref, lse_ref):
        p, i = pl.program_id(0), pl.program_id(1)
        _, mask, kblk, vblk, lo = _attn_setup(i, n_lat, m_ctx, nbl, k_ref, v_ref)
        kr, vr = pltpu.roll(kblk, HEAD_DIM, 1), pltpu.roll(vblk, HEAD_DIM, 1)
        for kh in range(2):
            k2, _, _ = _kv_variants(kblk, kr, kh, lo)
            _, vlo, vhi = _kv_variants(vblk, vr, kh, lo)
            qst = _stack_heads(q_ref, kh, lo)
            s, snk = _scores(qst, k2, mask, sink_ref, p, kh)
            mx = jnp.maximum(jnp.max(s, axis=-1, keepdims=True), snk)
            pe = jnp.exp(s - mx)
            den = jnp.sum(pe, axis=-1, keepdims=True) + jnp.exp(snk - mx)
            inv = 1.0 / den
            pb = pe.astype(BF16)
            for jp in range(2):
                r0 = 2 * jp * BLK
                pair = (jnp.dot(pb[r0:r0 + BLK], vlo, preferred_element_type=F32)
                        + jnp.dot(pb[r0 + BLK:r0 + 2 * BLK], vhi, preferred_element_type=F32))
                pair = pair * jnp.where(lo, inv[r0:r0 + BLK], inv[r0 + BLK:r0 + 2 * BLK])
                o_ref[:, (2 * kh + jp) * LANES:(2 * kh + jp + 1) * LANES] = pair.astype(BF16)
            lse = mx + jnp.log(den)
            for g in range(4):
                lse_ref[:, kh * 4 + g:kh * 4 + g + 1] = lse[g * BLK:(g + 1) * BLK]

    outs, carried = _pcall(
        body, (sink, qkv, qkv, qkv), name=name, grid=(2, nb),
        in_specs=[SMEM_SPEC,
                  pl.BlockSpec((BLK, 4 * LANES), lambda p, i: (i, p)),
                  pl.BlockSpec((r, LANES), lambda p, i: (0, kcol + p)),
                  pl.BlockSpec((r, LANES), lambda p, i: (0, kcol + 2 + p))],
        out_specs=[pl.BlockSpec((BLK, 4 * LANES), lambda p, i: (i, p)),
                   pl.BlockSpec((None, BLK, 8), lambda p, i: (p, i, 0))],
        out_shape=[jax.ShapeDtypeStruct((r, D_MODEL), BF16), jax.ShapeDtypeStruct((2, r, 8), F32)], carry=carry)
    return outs if carry is None else (outs, carried)


def _attn_bwd(qkv, sink, o, do, lse, n_lat, m_ctx, name, carry=None):
    r = qkv.shape[0]
    nb, nbl = r // BLK, n_lat // BLK
    kcol = D_MODEL // LANES

    def body(sink_ref, q_ref, k_ref, v_ref, o_ref, do_ref, lse_ref, dq_ref, dk_ref, dv_ref, ds_ref):
        p, i = pl.program_id(0), pl.program_id(1)

        @pl.when(i == 0)
        def _():
            dk_ref[...] = jnp.zeros_like(dk_ref)
            dv_ref[...] = jnp.zeros_like(dv_ref)
            ds_ref[...] = jnp.zeros_like(ds_ref)

        start, mask, kblk, vblk, lo = _attn_setup(i, n_lat, m_ctx, nbl, k_ref, v_ref)
        kr, vr = pltpu.roll(kblk, HEAD_DIM, 1), pltpu.roll(vblk, HEAD_DIM, 1)
        lane = lax.broadcasted_iota(jnp.int32, (1, LANES), 1)
        dks, dvs = [], []
        for kh in range(2):
            k2, klo, khi = _kv_variants(kblk, kr, kh, lo)
            v2, _, _ = _kv_variants(vblk, vr, kh, lo)
            qst = _stack_heads(q_ref, kh, lo)
            dost = _stack_heads(do_ref, kh, lo)
            s, snk = _scores(qst, k2, mask, sink_ref, p, kh)
            lse4 = jnp.concatenate([lse_ref[:, kh * 4 + g:kh * 4 + g + 1] for g in range(4)], axis=0)
            pe = jnp.exp(s - lse4)
            dp = lax.dot_general(dost, v2, NT, preferred_element_type=F32)
            deltas = []
            for jp in range(2):
                cols = slice((2 * kh + jp) * LANES, (2 * kh + jp + 1) * LANES)
                prod = do_ref[:, cols].astype(F32) * o_ref[:, cols].astype(F32)
                deltas.append(jnp.sum(jnp.where(lo, prod, 0.0), axis=-1, keepdims=True))
                deltas.append(jnp.sum(jnp.where(lo, 0.0, prod), axis=-1, keepdims=True))
            delta = jnp.concatenate(deltas, axis=0)
            dsc = pe * (dp - delta)
            dsb, pb = dsc.astype(BF16), pe.astype(BF16)
            for jp in range(2):
                r0 = 2 * jp * BLK
                dq_ref[:, (2 * kh + jp) * LANES:(2 * kh + jp + 1) * LANES] = (
                    jnp.dot(dsb[r0:r0 + BLK], klo, preferred_element_type=F32)
                    + jnp.dot(dsb[r0 + BLK:r0 + 2 * BLK], khi, preferred_element_type=F32)).astype(BF16)
            dkf = lax.dot_general(dsb, qst, TN, preferred_element_type=F32)
            dvf = lax.dot_general(pb, dost, TN, preferred_element_type=F32)
            dks.append(dkf + pltpu.roll(dkf, HEAD_DIM, 1))
            dvs.append(dvf + pltpu.roll(dvf, HEAD_DIM, 1))
            contrib = -jnp.exp(snk - lse4) * delta
            for g in range(4):
                tot = jnp.sum(contrib[g * BLK:(g + 1) * BLK], axis=0, keepdims=True)
                ds_ref[0:1, :] += jnp.where(lane == kh * 4 + g, tot, 0.0)
        dk_blk = jnp.where(lo, dks[0], dks[1])
        dv_blk = jnp.where(lo, dvs[0], dvs[1])
        dk_ref[pl.ds(start, 3 * BLK), :] += dk_blk[:3 * BLK]
        dk_ref[pl.ds(n_lat, m_ctx), :] += dk_blk[3 * BLK:]
        dv_ref[pl.ds(start, 3 * BLK), :] += dv_blk[:3 * BLK]
        dv_ref[pl.ds(n_lat, m_ctx), :] += dv_blk[3 * BLK:]

    qspec = pl.BlockSpec((BLK, 4 * LANES), lambda p, i: (i, p))
    outs, carried = _pcall(
        body, (sink, qkv, qkv, qkv, o, do, lse), name=name, grid=(2, nb),
        in_specs=[SMEM_SPEC, qspec,
                  pl.BlockSpec((r, LANES), lambda p, i: (0, kcol + p)),
                  pl.BlockSpec((r, LANES), lambda p, i: (0, kcol + 2 + p)),
                  qspec, qspec, pl.BlockSpec((None, BLK, 8), lambda p, i: (p, i, 0))],
        out_specs=[qspec, pl.BlockSpec((r, LANES), lambda p, i: (0, p)), pl.BlockSpec((r, LANES), lambda p, i: (0, p)),
                   pl.BlockSpec((None, 8, LANES), lambda p, i: (p, 0, 0))],
        out_shape=[jax.ShapeDtypeStruct((r, D_MODEL), BF16), jax.ShapeDtypeStruct((r, 2 * LANES), F32),
                   jax.ShapeDtypeStruct((r, 2 * LANES), F32), jax.ShapeDtypeStruct((2, 8, LANES), F32)], carry=carry)
    return outs if carry is None else (outs, carried)


def _gating_parts(z_ref, gv_ref):
    za = z_ref[...].astype(F32)
    zg = jax.nn.gelu(za)
    u, v = zg[:, :D_MODEL], zg[:, D_MODEL:]
    rs = lax.rsqrt(jnp.mean(v * v, axis=-1, keepdims=True) + EPS)
    return za, u, v, rs, v * rs * gv_ref[...]


def _mix(w_ref, vals):
    vb = vals.astype(BF16)
    return jnp.concatenate(
        [jnp.dot(w_ref[g], vb[:, g * LANES:(g + 1) * LANES], preferred_element_type=F32) for g in range(A_GROUPS)],
        axis=1)


def _gating_fwd(z, ws, bias, g_v, name):
    r = z.shape[0]

    def body(z_ref, w_ref, b_ref, gv_ref, y_ref):
        _, u, _, _, vn = _gating_parts(z_ref, gv_ref)
        y_ref[...] = (u * (_mix(w_ref, vn) + b_ref[...])).astype(BF16)

    return pl.pallas_call(
        body, name=name, grid=(r // CHUNK,),
        in_specs=[pl.BlockSpec((CHUNK, 2 * D_MODEL), lambda i: (i, P_A // (2 * D_MODEL))),
                  pl.BlockSpec((A_GROUPS, CHUNK, CHUNK), lambda i: (0, 0, 0)),
                  pl.BlockSpec((CHUNK, D_MODEL), lambda i: (0, 0)),
                  pl.BlockSpec((1, D_MODEL), lambda i: (0, 0))],
        out_specs=pl.BlockSpec((CHUNK, D_MODEL), lambda i: (i, 0)),
        out_shape=jax.ShapeDtypeStruct((r, D_MODEL), BF16), compiler_params=_params())(z, ws, bias, g_v)


def _gating_bwd(z, ws, ws_t, bias, g_v, dy, name, carry=None):
    r = z.shape[0]

    def body(z_ref, w_ref, wt_ref, b_ref, gv_ref, dy_ref, dz_ref, dw_ref, db_ref, dg_ref):
        i = pl.program_id(0)

        @pl.when(i == 0)
        def _():
            dw_ref[...] = jnp.zeros_like(dw_ref)
            db_ref[...] = jnp.zeros_like(db_ref)
            dg_ref[...] = jnp.zeros_like(dg_ref)

        za, u, v, rs, vn = _gating_parts(z_ref, gv_ref)
        dyv = dy_ref[...].astype(F32)
        du = dyv * (_mix(w_ref, vn) + b_ref[...])
        dmixed = dyv * u
        dvn = _mix(wt_ref, dmixed)
        dmb, vnb = dmixed.astype(BF16), vn.astype(BF16)
        for g in range(A_GROUPS):
            cols = slice(g * LANES, (g + 1) * LANES)
            dw_ref[g] += lax.dot_general(dmb[:, cols], vnb[:, cols], NT, preferred_element_type=F32)
            db_ref[:, g:g + 1] += jnp.sum(dmixed[:, cols], axis=-1, keepdims=True)
        gv = gv_ref[...]
        vh = v * rs
        dg_ref[0:1, :] += jnp.sum(dvn * vh, axis=0, keepdims=True)
        dvh = dvn * gv
        dv = rs * (dvh - vh * jnp.mean(dvh * vh, axis=-1, keepdims=True))
        _, vjp = jax.vjp(jax.nn.gelu, za)
        dz_ref[...] = vjp(jnp.concatenate([du, dv], axis=1))[0].astype(BF16)

    wspec = pl.BlockSpec((A_GROUPS, CHUNK, CHUNK), lambda i: (0, 0, 0))
    outs, carried = _pcall(
        body, (z, ws, ws_t, bias, g_v, dy), name=name, grid=(r // CHUNK,),
        in_specs=[pl.BlockSpec((CHUNK, 2 * D_MODEL), lambda i: (i, P_A // (2 * D_MODEL))), wspec, wspec,
                  pl.BlockSpec((CHUNK, D_MODEL), lambda i: (0, 0)), pl.BlockSpec((1, D_MODEL), lambda i: (0, 0)),
                  pl.BlockSpec((CHUNK, D_MODEL), lambda i: (i, 0))],
        out_specs=[pl.BlockSpec((CHUNK, 2 * D_MODEL), lambda i: (i, 0)), wspec,
                   pl.BlockSpec((CHUNK, A_GROUPS), lambda i: (0, 0)), pl.BlockSpec((8, D_MODEL), lambda i: (0, 0))],
        out_shape=[jax.ShapeDtypeStruct((r, 2 * D_MODEL), BF16), jax.ShapeDtypeStruct((A_GROUPS, CHUNK, CHUNK), F32),
                   jax.ShapeDtypeStruct((CHUNK, A_GROUPS), F32), jax.ShapeDtypeStruct((8, D_MODEL), F32)], carry=carry)
    return outs if carry is None else (outs, carried)


def _scr_rows(j, n_lat):
    s = pl.multiple_of(j * CONV_CHUNK, CONV_CHUNK)
    shift = jnp.where(j >= n_lat // CONV_CHUNK, 2 * PAD, PAD)
    return s, pl.multiple_of(s + shift, PAD)


def _taps(scr, j, n_lat):
    c = CONV_CHUNK
    s, at = _scr_rows(j, n_lat)
    ext = scr[pl.ds(pl.multiple_of(at - PAD, PAD), c + 2 * PAD), :]
    xm = pltpu.roll(ext, 1, 0)[PAD:PAD + c]
    xp = pltpu.roll(ext, c + 2 * PAD - 1, 0)[PAD:PAD + c]
    return s, xm, ext[PAD:PAD + c], xp


def _zero_pads(scr, r, n_lat):
    for at in (0, PAD + n_lat, 2 * PAD + r):
        scr[pl.ds(at, PAD), :] = jnp.zeros((PAD, LANES), F32)


def _col(arr_cols, c0):
    return pl.BlockSpec((arr_cols, LANES), lambda c: (0, c0 + c))


def _ffn_conv_fwd(up, w, n_lat, name):
    r = up.shape[0]
    nct = D_FF // LANES
    nchunk = r // CONV_CHUNK

    def body(a_ref, g_ref, w_ref, f_ref, scr):
        _zero_pads(scr, r, n_lat)

        def fill(j, _):
            s, at = _scr_rows(j, n_lat)
            scr[pl.ds(at, CONV_CHUNK), :] = a_ref[pl.ds(s, CONV_CHUNK), :].astype(F32)
            return 0
        lax.fori_loop(0, nchunk, fill, 0)
        wv = w_ref[...]

        def step(j, _):
            s, xm, x0, xp = _taps(scr, j, n_lat)
            ca = wv[0:1] * xm + wv[1:2] * x0 + wv[2:3] * xp
            gv = g_ref[pl.ds(s, CONV_CHUNK), :].astype(F32)
            f_ref[pl.ds(s, CONV_CHUNK), :] = (ca * _sigmoid(ca) * gv).astype(BF16)
            return 0
        lax.fori_loop(0, nchunk, step, 0)

    return pl.pallas_call(
        body, name=name, grid=(nct,),
        in_specs=[_col(r, 0), _col(r, nct), _col(3, 0)],
        out_specs=_col(r, 0), out_shape=jax.ShapeDtypeStruct((r, D_FF), BF16),
        scratch_shapes=[pltpu.VMEM((r + 3 * PAD, LANES), F32)], compiler_params=_params())(up, up, w)


def _ffn_conv_bwd(up, w, df, n_lat, name, carry=None):
    r = up.shape[0]
    nct = D_FF // LANES
    nchunk = r // CONV_CHUNK

    def body(a_ref, g_ref, w_ref, df_ref, dup_ref, dw_ref, scr, scr2):
        _zero_pads(scr, r, n_lat)
        _zero_pads(scr2, r, n_lat)

        def fill(j, _):
            s, at = _scr_rows(j, n_lat)
            scr[pl.ds(at, CONV_CHUNK), :] = a_ref[pl.ds(s, CONV_CHUNK), :].astype(F32)
            return 0
        lax.fori_loop(0, nchunk, fill, 0)
        wv = w_ref[...]

        def first(j, carry):
            s, xm, x0, xp = _taps(scr, j, n_lat)
            ca = wv[0:1] * xm + wv[1:2] * x0 + wv[2:3] * xp
            sg = _sigmoid(ca)
            gv = g_ref[pl.ds(s, CONV_CHUNK), :].astype(F32)
            dfv = df_ref[pl.ds(s, CONV_CHUNK), :].astype(F32)
            dup_ref[1, pl.ds(s, CONV_CHUNK), :] = (dfv * ca * sg).astype(BF16)
            dca = dfv * gv * (sg * (1.0 + ca * (1.0 - sg)))
            scr2[pl.ds(_scr_rows(j, n_lat)[1], CONV_CHUNK), :] = dca
            return tuple(cw + jnp.sum(dca * xv, axis=0, keepdims=True) for cw, xv in zip(carry, (xm, x0, xp)))
        zero = jnp.zeros((1, LANES), F32)
        dws = lax.fori_loop(0, nchunk, first, (zero, zero, zero))
        dw_ref[...] = _rows(list(dws) + [None] * 5)

        def second(j, _):
            s, ym, y0, yp = _taps(scr2, j, n_lat)
            dup_ref[0, pl.ds(s, CONV_CHUNK), :] = (wv[0:1] * yp + wv[1:2] * y0 + wv[2:3] * ym).astype(BF16)
            return 0
        lax.fori_loop(0, nchunk, second, 0)

    outs, carried = _pcall(
        body, (up, up, w, df), name=name, grid=(nct,),
        in_specs=[_col(r, 0), _col(r, nct), _col(3, 0), _col(r, 0)],
        out_specs=[pl.BlockSpec((2, r, LANES), lambda c: (0, 0, c)), _col(8, 0)],
        out_shape=[jax.ShapeDtypeStruct((2, r, D_FF), BF16), jax.ShapeDtypeStruct((8, D_FF), F32)],
        scratch_shapes=[pltpu.VMEM((r + 3 * PAD, LANES), F32), pltpu.VMEM((r + 3 * PAD, LANES), F32)], carry=carry)
    return outs if carry is None else (outs, carried)


def _sconv_fwd(z, w, n_lat, name):
    r = z.shape[0]
    nct = D_MODEL // LANES
    nchunk = r // CONV_CHUNK
    c0 = P_B // LANES

    def body(bg_ref, cg_ref, hb_ref, w_ref, y_ref, scr):
        _zero_pads(scr, r, n_lat)

        def fill(j, _):
            s, at = _scr_rows(j, n_lat)
            rows = pl.ds(s, CONV_CHUNK)
            scr[pl.ds(at, CONV_CHUNK), :] = cg_ref[rows, :].astype(F32) * hb_ref[rows, :].astype(F32)
            return 0
        lax.fori_loop(0, nchunk, fill, 0)
        wv = w_ref[...]

        def step(j, _):
            s, xm, x0, xp = _taps(scr, j, n_lat)
            conv = wv[0:1] * xm + wv[1:2] * x0 + wv[2:3] * xp
            y_ref[pl.ds(s, CONV_CHUNK), :] = (bg_ref[pl.ds(s, CONV_CHUNK), :].astype(F32) * conv).astype(BF16)
            return 0
        lax.fori_loop(0, nchunk, step, 0)

    return pl.pallas_call(
        body, name=name, grid=(nct,),
        in_specs=[_col(r, c0), _col(r, c0 + nct), _col(r, c0 + 2 * nct), _col(3, 0)],
        out_specs=_col(r, 0), out_shape=jax.ShapeDtypeStruct((r, D_MODEL), BF16),
        scratch_shapes=[pltpu.VMEM((r + 3 * PAD, LANES), F32)], compiler_params=_params())(z, z, z, w)


def _sconv_bwd(z, w, dy, n_lat, name):
    r = z.shape[0]
    nct = D_MODEL // LANES
    nchunk = r // CONV_CHUNK
    c0 = P_B // LANES

    def body(bg_ref, cg_ref, hb_ref, w_ref, dy_ref, dz_ref, dw_ref, scr, scr2):
        _zero_pads(scr, r, n_lat)
        _zero_pads(scr2, r, n_lat)

        def fill(j, _):
            s, at = _scr_rows(j, n_lat)
            rows = pl.ds(s, CONV_CHUNK)
            scr[pl.ds(at, CONV_CHUNK), :] = cg_ref[rows, :].astype(F32) * hb_ref[rows, :].astype(F32)
            return 0
        lax.fori_loop(0, nchunk, fill, 0)
        wv = w_ref[...]

        def first(j, carry):
            s, xm, x0, xp = _taps(scr, j, n_lat)
            rows = pl.ds(s, CONV_CHUNK)
            conv = wv[0:1] * xm + wv[1:2] * x0 + wv[2:3] * xp
            dyv = dy_ref[rows, :].astype(F32)
            dz_ref[0, rows, :] = (dyv * conv).astype(BF16)
            dconv = dyv * bg_ref[rows, :].astype(F32)
            scr2[pl.ds(_scr_rows(j, n_lat)[1], CONV_CHUNK), :] = dconv
            return tuple(cw + jnp.sum(dconv * xv, axis=0, keepdims=True) for cw, xv in zip(carry, (xm, x0, xp)))
        zero = jnp.zeros((1, LANES), F32)
        dws = lax.fori_loop(0, nchunk, first, (zero, zero, zero))
        dw_ref[...] = _rows(list(dws) + [None] * 5)

        def second(j, _):
            s, ym, y0, yp = _taps(scr2, j, n_lat)
            rows = pl.ds(s, CONV_CHUNK)
            dx = wv[0:1] * yp + wv[1:2] * y0 + wv[2:3] * ym
            dz_ref[1, rows, :] = (dx * hb_ref[rows, :].astype(F32)).astype(BF16)
            dz_ref[2, rows, :] = (dx * cg_ref[rows, :].astype(F32)).astype(BF16)
            return 0
        lax.fori_loop(0, nchunk, second, 0)

    return pl.pallas_call(
        body, name=name, grid=(nct,),
        in_specs=[_col(r, c0), _col(r, c0 + nct), _col(r, c0 + 2 * nct), _col(3, 0), _col(r, 0)],
        out_specs=[pl.BlockSpec((3, r, LANES), lambda c: (0, 0, c)), _col(8, 0)],
        out_shape=[jax.ShapeDtypeStruct((3, r, D_MODEL), BF16), jax.ShapeDtypeStruct((8, D_MODEL), F32)],
        scratch_shapes=[pltpu.VMEM((r + 3 * PAD, LANES), F32), pltpu.VMEM((r + 3 * PAD, LANES), F32)],
        compiler_params=_params())(z, z, z, w, dy)


def _merge_proj_out(z, b_gate, ps, w_out, bm, name):
    r = z.shape[0]
    sub = ROW_TILE
    row = pl.BlockSpec((bm, D_MODEL), lambda i: (i, 0))

    def body(zg_ref, b_ref, p0_ref, p1_ref, p2_ref, w_ref, m_ref, o_ref):
        def chunk(c, _):
            rows = pl.ds(pl.multiple_of(c * sub, sub), sub)
            gates = _sigmoid(zg_ref[rows, :].astype(F32) + b_ref[...])
            acc = None
            for t, p_ref in enumerate((p0_ref, p1_ref, p2_ref)):
                term = gates[:, t * D_MODEL:(t + 1) * D_MODEL] * p_ref[rows, :].astype(F32)
                acc = term if acc is None else acc + term
            m_ref[rows, :] = acc.astype(BF16)
            return 0
        lax.fori_loop(0, bm // sub, chunk, 0)
        o_ref[...] = jnp.dot(m_ref[...], w_ref[...], preferred_element_type=F32).astype(o_ref.dtype)

    return pl.pallas_call(
        body, name=name, grid=(r // bm,),
        in_specs=[pl.BlockSpec((bm, 3 * D_MODEL), lambda i: (i, 0)), pl.BlockSpec((1, 3 * D_MODEL), lambda i: (0, 0)),
                  row, row, row, pl.BlockSpec((D_MODEL, D_MODEL), lambda i: (0, 0))],
        out_specs=[row, row],
        out_shape=[jax.ShapeDtypeStruct((r, D_MODEL), BF16), jax.ShapeDtypeStruct((r, D_MODEL), BF16)],
        compiler_params=_params())(z, b_gate, *ps, w_out)


def _merge_bwd(z, b_gate, ps, dmerged, name):
    r = z.shape[0]
    tm = ROW_TILE
    row = pl.BlockSpec((tm, D_MODEL), lambda i: (i, 0))
    wide = pl.BlockSpec((tm, 3 * D_MODEL), lambda i: (i, 0))

    def body(zg_ref, b_ref, p0_ref, p1_ref, p2_ref, dm_ref, d0_ref, d1_ref, d2_ref, dz_ref, db_ref):
        @pl.when(pl.program_id(0) == 0)
        def _():
            db_ref[...] = jnp.zeros_like(db_ref)

        gates = _sigmoid(zg_ref[...].astype(F32) + b_ref[...])
        dm = dm_ref[...].astype(F32)
        for t, (p_ref, d_ref) in enumerate(((p0_ref, d0_ref), (p1_ref, d1_ref), (p2_ref, d2_ref))):
            cols = slice(t * D_MODEL, (t + 1) * D_MODEL)
            gt = gates[:, cols]
            d_ref[...] = (dm * gt).astype(BF16)
            dlogit = dm * p_ref[...].astype(F32) * gt * (1.0 - gt)
            dz_ref[:, cols] = dlogit.astype(BF16)
            db_ref[0:1, cols] += jnp.sum(dlogit, axis=0, keepdims=True)

    shp = jax.ShapeDtypeStruct((r, D_MODEL), BF16)
    return pl.pallas_call(
        body, name=name, grid=(r // tm,),
        in_specs=[wide, pl.BlockSpec((1, 3 * D_MODEL), lambda i: (0, 0)), row, row, row, row],
        out_specs=[row, row, row, wide, pl.BlockSpec((8, 3 * D_MODEL), lambda i: (0, 0))],
        out_shape=[shp, shp, shp, jax.ShapeDtypeStruct((r, 3 * D_MODEL), BF16),
                   jax.ShapeDtypeStruct((8, 3 * D_MODEL), F32)],
        compiler_params=_params())(z, b_gate, *ps, dmerged)


def _sum_slots(buf, name):
    s, rows, _ = buf.shape
    whole_bytes = s * rows * LANES * 4
    tr = rows if whole_bytes <= VMEM_LIMIT // 8 else _pick(rows, [512, 256, 128, 64, 32, 16, 8])

    def body(b_ref, o_ref):
        acc = b_ref[0]
        for t in range(1, s):
            acc = acc + b_ref[t]
        o_ref[...] = acc

    return pl.pallas_call(
        body, name=name, grid=(rows // tr,),
        in_specs=[pl.BlockSpec((s, tr, LANES), lambda i: (0, i, 0))],
        out_specs=pl.BlockSpec((tr, LANES), lambda i: (i, 0)),
        out_shape=jax.ShapeDtypeStruct((rows, LANES), F32), compiler_params=_params())(buf)


def _adamw(w, gsrcs, m, v, name, scale=None):
    nl, rows, cols = w.shape
    assert len(gsrcs) == nl
    s = gsrcs[0].shape[0]
    tr = _pick(rows, [304, 352, 256, 128, 64, 32, 16, 8])
    blk = pl.BlockSpec((None, tr, cols), lambda l, i: (l, i, 0))
    c1 = 1.0 / (1.0 - ADAM_B1 ** ADAM_STEP)
    c2 = 1.0 / (1.0 - ADAM_B2 ** ADAM_STEP)

    def gspec(t):
        return pl.BlockSpec((s, tr, cols), lambda l, i: (0, jnp.where(l == t, i, 0), 0))

    def body(*refs):
        w_ref, g_refs, (m_ref, v_ref) = refs[0], refs[1:1 + nl], refs[1 + nl:3 + nl]
        rest = refs[3 + nl:]
        if scale is not None:
            sc_ref, rest = rest[0], rest[1:]
        go_ref, d_ref, mo_ref, vo_ref = rest
        layer = pl.program_id(0)
        g = None
        for t in range(nl):
            gt = g_refs[t][0].astype(F32)
            for q in range(1, s):
                gt = gt + g_refs[t][q].astype(F32)
            g = gt if g is None else jnp.where(layer == t, gt, g)
        if scale is not None:
            g = g * sc_ref[...]
        mn = ADAM_B1 * m_ref[...] + (1.0 - ADAM_B1) * g
        vn = ADAM_B2 * v_ref[...] + (1.0 - ADAM_B2) * (g * g)
        go_ref[...] = g
        mo_ref[...] = mn
        vo_ref[...] = vn
        d_ref[...] = -ADAM_LR * ((mn * c1) / (jnp.sqrt(vn * c2) + ADAM_EPS) + ADAM_WD * w_ref[...])

    shp = jax.ShapeDtypeStruct((nl, rows, cols), F32)
    ins = [w] + list(gsrcs) + [m, v] + ([] if scale is None else [scale])
    return pl.pallas_call(
        body, name=name, grid=(nl, rows // tr),
        in_specs=[blk] + [gspec(t) for t in range(nl)] + [blk, blk] + ([] if scale is None else [blk]),
        out_specs=[blk] * 4, out_shape=[shp] * 4, compiler_params=_params())(*ins)


def _pack(arrs):
    flat = []
    for a in arrs:
        a = a.reshape(-1).astype(F32)
        pad = (-a.shape[0]) % (8 * LANES)
        flat.append(jnp.pad(a, (0, pad)) if pad else a)
    return jnp.concatenate(flat).reshape(-1, LANES)


def _unpack(buf, shapes, lead=()):
    out, row = [], 0
    for shp in shapes:
        n = 1
        for d in shp:
            n *= d
        nrows = -(-n // (8 * LANES)) * 8
        piece = buf[..., row:row + nrows, :].reshape(lead + (nrows * LANES,))[..., :n]
        out.append(piece.reshape(lead + tuple(shp)))
        row += nrows
    return out


def _silu(x):
    return x * jax.nn.sigmoid(x)


def _rope_tables(n_lat, m_ctx):
    pos = jnp.arange(n_lat)
    row = (pos // GRID_W).astype(F32)
    col = (pos % GRID_W).astype(F32)
    half = HEAD_DIM // 2
    inv = ROPE_THETA ** (-jnp.arange(0, half, 2, dtype=F32) / half)
    ang = jnp.concatenate([row[:, None] * inv, col[:, None] * inv], axis=-1)
    cos, sin = jnp.cos(ang), jnp.sin(ang)
    cos2 = jnp.tile(jnp.concatenate([cos, cos], axis=-1), (1, LANES // HEAD_DIM))
    sin2 = jnp.tile(jnp.concatenate([-sin, sin], axis=-1), (1, LANES // HEAD_DIM))
    return (jnp.concatenate([cos2, jnp.ones((m_ctx, LANES), F32)], axis=0),
            jnp.concatenate([sin2, jnp.zeros((m_ctx, LANES), F32)], axis=0))


def _to_slots(full, axis):
    shp = full.shape
    new = shp[:axis] + (N_DEV, shp[axis] // N_DEV) + shp[axis + 1:]
    return jnp.moveaxis(full.reshape(new), axis, 0)


def _from_slots(slots, axis):
    moved = jnp.moveaxis(slots, 0, axis)
    shp = moved.shape
    return moved.reshape(shp[:axis] + (shp[axis] * shp[axis + 1],) + shp[axis + 2:])


def _permute_in(wt, name):
    blk = 512
    segs = [(OFF_G, IN_W), (0, OFF_K), (OFF_A, OFF_B), (OFF_B, OFF_G), (OFF_K, OFF_A)]

    def source(j):
        src, at = 0, 0
        for lo, hi in segs:
            n = (hi - lo) // blk
            src = src + jnp.where((j >= at) & (j < at + n), j - at + lo // blk, 0)
            at += n
        return (src, 0)

    def body(i_ref, o_ref):
        o_ref[...] = i_ref[...]

    return pl.pallas_call(
        body, name=name, grid=(IN_W // blk,), in_specs=[pl.BlockSpec((blk, D_MODEL), source)],
        out_specs=pl.BlockSpec((blk, D_MODEL), lambda j: (j, 0)), out_shape=jax.ShapeDtypeStruct(wt.shape, wt.dtype),
        compiler_params=_params())(wt)


def kernel(x, c, ctx, c_ctx, w_mod, b_mod, g_mix, w_in, b_gate, sink, w_spatial, b_spatial, g_v, w_sconv, w_branch, w_out, g_ffn, w_up, w_fconv, w_down, g_final, loss_target, m_c_ctx, m_w_mod, m_b_mod, m_g_mix, m_w_in, m_b_gate, m_sink, m_w_spatial, m_b_spatial, m_g_v, m_w_sconv, m_w_branch, m_w_out, m_g_ffn, m_w_up, m_w_fconv, m_w_down, m_g_final, v_c_ctx, v_w_mod, v_b_mod, v_g_mix, v_w_in, v_b_gate, v_sink, v_w_spatial, v_b_spatial, v_g_v, v_w_sconv, v_w_branch, v_w_out, v_g_ffn, v_w_up, v_w_fconv, v_w_down, v_g_final):
    n_lat, m_ctx = x.shape[1], ctx.shape[1]
    r = n_lat + m_ctx
    me = 4 * lax.axis_index("x") + 2 * lax.axis_index("y") + lax.axis_index("c")
    mod_w = w_mod.shape[2]
    bm = _pick(r, [1408, 768, 256])
    bm_fused = _pick(r, [768, 256])

    tr = lambda a: jnp.swapaxes(a, -1, -2)
    branches = [(f"br{t}", w_branch[:, t]) for t in range(3)]
    shards = {}
    for kind, wt in [("in", tr(w_in)), ("out", w_out), ("up", tr(w_up)), ("dn", w_down)] + branches:
        wb = wt.astype(BF16)
        for l in range(DEPTH):
            shards[kind, l] = wb[l]
    full = {}

    def arrive(items, got):
        for key, slots in zip(items, got):
            wfull = _from_slots(slots, 0)
            full[key] = _permute_in(wfull, f"permute_in{key[1]}") if key[0] == "in" else wfull

    def gather_of(items):
        return [shards[key] for key in items], ["gather"] * len(items)

    small_shapes = [c.shape, w_sconv.shape, w_fconv.shape]
    (g_small,) = _exchange([_pack([c, w_sconv, w_fconv])], ["gather"], "gather_first")
    c_all, sconv_all, fconv_all = _unpack(g_small, small_shapes, lead=(N_DEV,))
    c_all = c_all.reshape(N_DEV, D_MODEL)
    w_sconv_full = _from_slots(sconv_all, 2)
    w_fconv_full = _from_slots(fconv_all, 2)

    act = jnp.concatenate([_silu(c_all), _silu(c_ctx)[None], jnp.zeros((7, D_MODEL), F32)], axis=0)
    mod_part = jnp.stack([_mm(act, w_mod[l], name=f"mod_fwd{l}", bm=16, bn=mod_w, bk=D_MODEL, out_dtype=F32)
                          for l in range(DEPTH)])
    (mod_all,) = _exchange([mod_part], ["gather"], "gather_mod")
    mod_full = _from_slots(mod_all, 2) + b_mod[:, None, :]
    mods = []
    for l in range(DEPTH):
        mine = lax.dynamic_index_in_dim(mod_full[l], me, axis=0, keepdims=False).reshape(6, D_MODEL)
        theirs = mod_full[l, N_DEV].reshape(6, D_MODEL)
        mods.append(jnp.pad(jnp.stack([mine, theirs]), ((0, 0), (0, 2), (0, 0))))

    cos, sin = _rope_tables(n_lat, m_ctx)
    xs = jnp.concatenate([x[0], ctx[0]], axis=0)
    ws_b = w_spatial.astype(BF16)
    ws_t = jnp.swapaxes(w_spatial, 2, 3).astype(BF16)
    vec = lambda a: a.reshape(1, -1)

    saved = []
    res = None
    for l in range(DEPTH):
        s = {}
        if res is None:
            s["x0"] = xs
            s["h"], got = _norm_fwd(xs, vec(g_mix[l]), mods[l], n_lat, 0, 1, f"norm_mix{l}",
                                    carry=gather_of([("in", l)]))
            arrive([("in", l)], got)
        else:
            s["x0"], s["h"] = _norm_fwd(xs, vec(g_mix[l]), mods[l], n_lat, 0, 1, f"norm_mix{l}", res=res)
        items = [("br0", l), ("br1", l), ("br2", l), ("out", l), ("up", l)]
        s["z"], got = _mm(s["h"], full["in", l], tb=True, name=f"proj_in{l}", bm=bm, bn=_pick(IN_W, [2432, 512]),
                          bk=D_MODEL, out_dtype=BF16, cols_outer=True, carry=gather_of(items))
        arrive(items, got)
        s["qkv"] = _rope_fwd(s["z"], cos, sin, f"rope{l}")
        items = [("dn", l)] + ([("in", l + 1)] if l + 1 < DEPTH else [])
        (s["y0"], s["lse"]), got = _attn_fwd(s["qkv"], sink[l], n_lat, m_ctx, f"attn{l}", carry=gather_of(items))
        arrive(items, got)
        s["bias"] = jnp.repeat(b_spatial[l].T, LANES, axis=1)
        s["y1"] = _gating_fwd(s["z"], ws_b[l], s["bias"], vec(g_v[l]), f"gating{l}")
        s["y2"] = _sconv_fwd(s["z"], w_sconv_full[l], n_lat, f"sconv{l}")
        s["p"] = [_mm(s[f"y{t}"], full[f"br{t}", l], name=f"branch{l}_{t}", bm=bm, bn=D_MODEL, bk=D_MODEL,
                      out_dtype=BF16) for t in range(3)]
        s["merged"], s["o"] = _merge_proj_out(s["z"], vec(b_gate[l]), s["p"], full["out", l], bm_fused,
                                              f"proj_out{l}")
        s["x1"], s["h2"] = _norm_fwd(s["x0"], vec(g_ffn[l]), mods[l], n_lat, 3, 4, f"norm_ffn{l}",
                                     res=(s["o"], mods[l], 2))
        s["up"] = _mm(s["h2"], full["up", l], tb=True, name=f"ffn_up{l}", bm=bm, bn=_pick(2 * D_FF, [1408]),
                      bk=D_MODEL, out_dtype=BF16, cols_outer=True)
        s["f"] = _ffn_conv_fwd(s["up"], w_fconv_full[l], n_lat, f"ffn_conv{l}")
        s["dd"] = _mm(s["f"], full["dn", l], name=f"ffn_down{l}", bm=bm, bn=D_MODEL, bk=D_FF, out_dtype=BF16)
        saved.append(s)
        xs, res = s["x1"], (s["dd"], mods[l], 5)

    top = saved[DEPTH - 1]
    dxs, d_dd, acc_final, accs_top = _loss_bwd(top["x1"], top["dd"], mods[DEPTH - 1], vec(g_final), loss_target[0],
                                               n_lat, "loss")
    loss_part = acc_final[1, 0]
    dg_final = acc_final[0]
    dmods = [None] * DEPTH
    gate2 = accs_top[:, 2]
    grads = {k: [None] * DEPTH for k in ("g_mix", "g_ffn", "g_v", "b_gate", "sink", "w_spatial", "b_spatial",
                                         "w_sconv", "w_fconv", "w_in", "w_branch", "w_out", "w_up", "w_down")}
    bk_r = _pick(r, [2816, 768, 256])
    small_names = ["g_ffn", "g_v", "b_gate", "sink", "w_spatial", "b_spatial", "w_sconv", "w_fconv"]
    recv = {}
    small_recv, small_shapes_of = {}, {}

    def small_pack(l):
        arrs = [grads[k][l] for k in small_names]
        if l > 0:
            arrs.append(grads["g_mix"][l])
        if l == DEPTH - 1:
            arrs += [loss_part.reshape(1), dg_final]
        small_shapes_of[l] = [a.shape for a in arrs]
        return _pack(arrs)

    for l in reversed(range(DEPTH)):
        s = saved[l]
        df = _mm(d_dd, full["dn", l], tb=True, name=f"d_ffn_down{l}", bm=bm, bn=_pick(D_FF, [1408]), bk=D_MODEL,
                 out_dtype=BF16)
        grads["w_down"][l] = _mm(s["f"], d_dd, ta=True, name=f"g_ffn_down{l}", bm=_pick(D_FF, [1408]), bn=D_MODEL,
                                 bk=bk_r, out_dtype=BF16)
        arrs, modes = [_to_slots(grads["w_down"][l], 0)], ["a2a"]
        if l + 1 < DEPTH:
            arrs.append(small_pack(l + 1))
            modes.append("gather")
        (dup, dwf), got = _ffn_conv_bwd(s["up"], w_fconv_full[l], df, n_lat, f"d_ffn_conv{l}", carry=(arrs, modes))
        recv["dn", l] = got[0]
        if l + 1 < DEPTH:
            small_recv[l + 1] = got[1]
        grads["w_fconv"][l] = dwf[:3]
        dh2 = _mm_pieces([(dup, "step", 0, 2, 0, D_FF)], full["up", l], w_t=True, name=f"d_ffn_up{l}", bm=bm,
                         out_dtype=BF16)
        kb = 1408
        nbh = D_FF // kb
        grads["w_up"][l] = _mm(
            dup, s["h2"], ta=True, name=f"g_ffn_up{l}", bm=kb, bn=D_MODEL, bk=bk_r, out_dtype=BF16,
            a_spec=((None, bk_r, kb), lambda i, j, k: (i // nbh, k, i % nbh), 2 * D_FF))
        dx1, d_o, acc, accs = _norm_bwd(s["x1"], vec(g_ffn[l]), mods[l], dh2, dxs, n_lat, 3, 4, f"d_norm_ffn{l}",
                                        res=(s["o"], mods[l], 2))
        grads["g_ffn"][l] = acc[0]
        shift2, scale2, gate1 = accs[:, 0], accs[:, 1], accs[:, 2]
        dmerged = _mm(d_o, full["out", l], tb=True, name=f"d_proj_out{l}", bm=bm, bn=D_MODEL, bk=D_MODEL,
                      out_dtype=BF16)
        grads["w_out"][l] = _mm(s["merged"], d_o, ta=True, name=f"g_proj_out{l}", bm=D_MODEL, bn=D_MODEL, bk=bk_r,
                                out_dtype=BF16)
        dp0, dp1, dp2, dz_g, dbg = _merge_bwd(s["z"], vec(b_gate[l]), s["p"], dmerged, f"d_merge{l}")
        grads["b_gate"][l] = dbg[0]
        dps = (dp0, dp1, dp2)
        dys = [_mm(dps[t], full[f"br{t}", l], tb=True, name=f"d_branch{l}_{t}", bm=bm, bn=D_MODEL, bk=D_MODEL,
                   out_dtype=BF16) for t in range(3)]
        grads["w_branch"][l] = jnp.stack(
            [_mm(s[f"y{t}"], dps[t], ta=True, name=f"g_branch{l}_{t}", bm=D_MODEL, bn=D_MODEL, bk=bk_r,
                 out_dtype=BF16) for t in range(3)])
        arrs = [_to_slots(grads["w_up"][l], 0)] + ([_to_slots(grads["w_in"][l + 1], 0)] if l + 1 < DEPTH else [])
        (dq, dk, dv, dsk), got = _attn_bwd(s["qkv"], sink[l], s["y0"], dys[0], s["lse"], n_lat, m_ctx, f"d_attn{l}",
                                           carry=(arrs, ["a2a"] * len(arrs)))
        recv["up", l] = got[0]
        if l + 1 < DEPTH:
            recv["in", l + 1] = got[1]
        grads["sink"][l] = dsk[:, 0, :8].reshape(N_HEADS)
        dz_qkv = _rope_bwd(dq, dk, dv, cos, sin, f"d_rope{l}")
        (dz_a, dws, dbs, dgv), (recv["out", l], recv["br", l]) = _gating_bwd(
            s["z"], ws_b[l], ws_t[l], s["bias"], vec(g_v[l]), dys[1], f"d_gating{l}",
            carry=([_to_slots(grads["w_out"][l], 0), _to_slots(grads["w_branch"][l], 1)], ["a2a"] * 2))
        grads["w_spatial"][l], grads["b_spatial"][l], grads["g_v"][l] = dws, dbs.T, dgv[0]
        dz_b, dwsc = _sconv_bwd(s["z"], w_sconv_full[l], dys[2], n_lat, f"d_sconv{l}")
        grads["w_sconv"][l] = dwsc[:3]
        kvw = 2 * N_KV_HEADS * HEAD_DIM
        pieces = [(dz_g, None, 0, 3, 0, D_MODEL), (dz_qkv, None, 0, 1, P_Q // D_MODEL, D_MODEL),
                  (dz_a, None, 0, 2, P_A // D_MODEL, D_MODEL), (dz_b, "step", 0, 3, P_B // D_MODEL, D_MODEL),
                  (dz_qkv, None, D_MODEL // kvw, 1, P_KV // kvw, kvw)]
        gw = lambda a, nm, rows, **kw: _mm(a, s["h"], ta=True, name=f"g_proj_in{l}_{nm}", bm=rows, bn=D_MODEL,
                                           bk=bk_r, out_dtype=BF16, **kw)
        gw_g = gw(dz_g, "gate", 1536)
        gw_qkv = gw(dz_qkv, "qkv", 1536)
        gw_a = gw(dz_a, "gating", 1024)
        gw_b = [gw(dz_b, f"sconv{t}", 1024, a_lead=t) for t in range(3)]
        grads["w_in"][l] = jnp.concatenate([gw_qkv, gw_a] + gw_b + [gw_g], axis=0)
        bm_in = _pick(r, [1408, 768, 256])
        if l > 0:
            dh = _mm_pieces(pieces, full["in", l], w_t=True, name=f"d_proj_in{l}", bm=bm_in, out_dtype=BF16)
        else:
            dh, (recv["in", 0], small_recv[0]) = _mm_pieces(
                pieces, full["in", l], w_t=True, name=f"d_proj_in{l}", bm=bm_in, out_dtype=BF16,
                carry=([_to_slots(grads["w_in"][0], 0), small_pack(0)], ["a2a", "gather"]))
        below = None if l == 0 else (saved[l - 1]["dd"], mods[l - 1], 5)
        outs = _norm_bwd(s["x0"], vec(g_mix[l]), mods[l], dh, dx1, n_lat, 0, 1, f"d_norm_mix{l}", res=below)
        if below is None:
            dxs, acc, accs = outs
        else:
            dxs, d_dd, acc, accs = outs
        grads["g_mix"][l] = acc[0]
        dmods[l] = jnp.stack([accs[:, 0], accs[:, 1], gate1, shift2, scale2, gate2], axis=1)
        gate2 = accs[:, 2]

    dmod_own = jnp.stack([dmods[l][0].reshape(-1) for l in range(DEPTH)])
    dmod_ctx = jnp.stack([dmods[l][1].reshape(-1) for l in range(DEPTH)])
    late = [grads["g_mix"][0], dmod_own + dmod_ctx, dmod_ctx, dmod_own]
    late_shapes = [a.shape for a in late]
    (late_all,) = _exchange([_pack(late)], ["gather"], "gather_late_grads")
    g_mix0_g, b_mod_g, dmodc_tot, _ = _unpack(_sum_slots(late_all, "sum_late_grads"), late_shapes)
    dmod_all = _unpack(late_all, late_shapes, lead=(N_DEV,))[-1]
    layer_sums = [_unpack(_sum_slots(small_recv[l], f"sum_small_grads{l}"), small_shapes_of[l]) for l in range(DEPTH)]
    by_name = {k: jnp.stack([layer_sums[l][t] for l in range(DEPTH)]) for t, k in enumerate(small_names)}
    g_mix_g = jnp.stack([g_mix0_g] + [layer_sums[l][len(small_names)] for l in range(1, DEPTH)])
    loss_sum, g_final_g = layer_sums[DEPTH - 1][-2], layer_sums[DEPTH - 1][-1]
    g_ffn_g, g_v_g, b_gate_g, sink_g = by_name["g_ffn"], by_name["g_v"], by_name["b_gate"], by_name["sink"]
    w_spatial_g, b_spatial_g = by_name["w_spatial"], by_name["b_spatial"]
    w_sconv_g = lax.dynamic_slice_in_dim(by_name["w_sconv"], me * w_sconv.shape[2], w_sconv.shape[2], axis=2)
    w_fconv_g = lax.dynamic_slice_in_dim(by_name["w_fconv"], me * w_fconv.shape[2], w_fconv.shape[2], axis=2)

    dmod_cols = lax.dynamic_slice_in_dim(dmod_all, me * mod_w, mod_w, axis=2)
    dmodc_cols = lax.dynamic_slice_in_dim(dmodc_tot, me * mod_w, mod_w, axis=1)
    g_w_mod, cctx_part = [], None
    for l in range(DEPTH):
        rhs = jnp.concatenate([dmod_cols[:, l], dmodc_cols[l][None], jnp.zeros((7, mod_w), F32)], axis=0)
        g_w_mod.append(_mm(act, rhs, ta=True, name=f"g_mod{l}", bm=D_MODEL, bn=mod_w, bk=16, out_dtype=F32))
        lhs = jnp.pad(dmodc_cols[l][None], ((0, 7), (0, 0)))
        part = _mm(lhs, w_mod[l], tb=True, name=f"d_cctx{l}", bm=8, bn=D_MODEL, bk=mod_w, out_dtype=F32)
        cctx_part = part if cctx_part is None else cctx_part + part
    sg = jax.nn.sigmoid(c_ctx)
    dsilu = (sg * (1.0 + c_ctx * (1.0 - sg))).reshape(8, LANES)

    (r_cctx,) = _exchange([cctx_part[0].reshape(8, LANES)], ["gather"], "gather_c_ctx_grad")

    def per_layer(a):
        return a.reshape(a.shape[0], -1, a.shape[-1])

    upd = {}
    for nm, kind, wv, mv, vv in (("w_in", "in", w_in, m_w_in, v_w_in), ("w_branch", "br", w_branch, m_w_branch, v_w_branch),
                                 ("w_out", "out", w_out, m_w_out, v_w_out), ("w_up", "up", w_up, m_w_up, v_w_up),
                                 ("w_down", "dn", w_down, m_w_down, v_w_down), ("w_mod", None, w_mod, m_w_mod, v_w_mod)):
        if kind is None:
            gsrcs = [g[None] for g in g_w_mod]
        else:
            gsrcs = [per_layer(recv[kind, l]) for l in range(DEPTH)]
        if kind in ("in", "up"):
            outs = _adamw(tr(wv), gsrcs, tr(mv), tr(vv), f"adamw_{nm}")
            upd[nm] = [tr(o) for o in outs]
        else:
            outs = _adamw(per_layer(wv), gsrcs, per_layer(mv), per_layer(vv), f"adamw_{nm}")
            upd[nm] = [o.reshape(wv.shape) for o in outs]
    as_tile = lambda a: a.reshape(1, 8, LANES)
    upd["c_ctx"] = [o.reshape(D_MODEL) for o in _adamw(
        as_tile(c_ctx), [r_cctx], as_tile(m_c_ctx), as_tile(v_c_ctx), "adamw_c_ctx", scale=as_tile(dsilu))]

    names = ["b_mod", "g_mix", "b_gate", "sink", "w_spatial", "b_spatial", "g_v", "w_sconv", "g_ffn", "w_fconv",
             "g_final"]
    w_s = [b_mod, g_mix, b_gate, sink, w_spatial, b_spatial, g_v, w_sconv, g_ffn, w_fconv, g_final]
    g_s = [b_mod_g, g_mix_g, b_gate_g, sink_g, w_spatial_g, b_spatial_g, g_v_g, w_sconv_g, g_ffn_g, w_fconv_g,
           g_final_g]
    m_s = [m_b_mod, m_g_mix, m_b_gate, m_sink, m_w_spatial, m_b_spatial, m_g_v, m_w_sconv, m_g_ffn, m_w_fconv,
           m_g_final]
    v_s = [v_b_mod, v_g_mix, v_b_gate, v_sink, v_w_spatial, v_b_spatial, v_g_v, v_w_sconv, v_g_ffn, v_w_fconv,
           v_g_final]
    shapes = [a.shape for a in w_s]
    packed = _adamw(_pack(w_s)[None], [_pack(g_s)[None]], _pack(m_s)[None], _pack(v_s)[None], "adamw_small")
    unpacked = [_unpack(o[0], shapes) for o in packed]
    for t, nm in enumerate(names):
        upd[nm] = [unpacked[q][t] for q in range(4)]

    order = ["c_ctx", "w_mod", "b_mod", "g_mix", "w_in", "b_gate", "sink", "w_spatial", "b_spatial", "g_v", "w_sconv",
             "w_branch", "w_out", "g_ffn", "w_up", "w_fconv", "w_down", "g_final"]
    result = [loss_sum.reshape(()), dxs[None]]
    for q in range(4):
        result += [upd[nm][q] for nm in order]
    return tuple(result)
```

```python
import jax
import jax.numpy as jnp
from jax import lax
from jax.experimental import pallas as pl
from jax.experimental.pallas import tpu as pltpu

F32, BF16 = jnp.float32, jnp.bfloat16

D_MODEL = 1024
DEPTH = 2
GRID_W = 64
N_HEADS = 16
N_KV_HEADS = 4
HEAD_DIM = 64
WINDOW = 128
BLK = 128
ROPE_THETA = 10000.0
CHUNK = 128
A_GROUPS = 8
D_FF = 2816
EPS = 1e-6
NEG = -1e30
IN_W = 9728
OFF_K, OFF_A, OFF_B, OFF_G = 1024, 1536, 3584, 6656
P_Q, P_A, P_B, P_KV = 3072, 4096, 6144, 9216

N_DEV = 8
LANES = 128
ROW_TILE = 256
CONV_CHUNK = 256
PAD = 8
VMEM_LIMIT = 52 * 1024 * 1024

ADAM_LR, ADAM_B1, ADAM_B2, ADAM_EPS, ADAM_WD, ADAM_STEP = 0.001, 0.9, 0.999, 1e-08, 0.01, 10

HBM_SPEC = pl.BlockSpec(memory_space=pltpu.HBM)
SMEM_SPEC = pl.BlockSpec(memory_space=pltpu.SMEM)


def _params():
    return pltpu.CompilerParams(vmem_limit_bytes=VMEM_LIMIT)


def _pick(n, prefs):
    for p in prefs:
        if n % p == 0:
            return p
    raise ValueError((n, prefs))


def _sigmoid(x):
    return 0.5 * jnp.tanh(0.5 * x) + 0.5


def _mm(a, b, *, name, ta=False, tb=False, bm, bn, bk, out_dtype, a_lead=None, b_lead=None, a_spec=None,
        b_spec=None, cols_outer=False, carry=None):
    ash = a.shape[1:] if a_lead is not None else a.shape
    bsh = b.shape[1:] if b_lead is not None else b.shape
    kc = (bsh[1] if tb else bsh[0]) if b_spec is None else (ash[0] if ta else ash[1])
    mo =(ash[1] if ta else ash[0]) if a_spec is None else a_spec[2]
    no = (bsh[0] if tb else bsh[1]) if b_spec is None else b_spec[2]
    assert mo % bm == 0 and no % bn == 0 and kc % bk == 0, (name, mo, no, kc, bm, bn, bk)
    nk = kc // bk

    def spec(shape, fn, idx=None):
        if idx is not None:
            shape, inner = (None,) + shape, fn
            fn = lambda i, j, k: (idx,) + inner(i, j, k)
        if cols_outer:
            return pl.BlockSpec(shape, lambda j, i, k: fn(i, j, k))
        return pl.BlockSpec(shape, fn)

    if a_spec is not None:
        a_bs = spec(a_spec[0], a_spec[1])
    elif ta:
        a_bs = spec((bk, bm), lambda i, j, k: (k, i), a_lead)
    else:
        a_bs = spec((bm, bk), lambda i, j, k: (i, k), a_lead)
    if b_spec is not None:
        b_bs = spec(b_spec[0], b_spec[1])
    elif tb:
        b_bs = spec((bn, bk), lambda i, j, k: (j, k), b_lead)
    else:
        b_bs = spec((bk, bn), lambda i, j, k: (k, j), b_lead)
    dims = (((0 if ta else 1,), (1 if tb else 0,)), ((), ()))
    grid = (no // bn, mo // bm, nk) if cols_outer else (mo // bm, no // bn, nk)

    def body(a_ref, b_ref, o_ref, *scratch):
        if nk == 1:
            o_ref[...] = lax.dot_general(a_ref[...], b_ref[...], dims, preferred_element_type=F32).astype(o_ref.dtype)
        else:
            acc = scratch[0]
            k = pl.program_id(2)

            @pl.when(k == 0)
            def _():
                acc[...] = jnp.zeros_like(acc)

            acc[...] += lax.dot_general(a_ref[...], b_ref[...], dims, preferred_element_type=F32)

            @pl.when(k == nk - 1)
            def _():
                o_ref[...] = acc[...].astype(o_ref.dtype)

    outs, carried = _pcall(
        body, (a, b), name=name, grid=grid, in_specs=[a_bs, b_bs],
        out_specs=[spec((bm, bn), lambda i, j, k: (i, j))],
        out_shape=[jax.ShapeDtypeStruct((mo, no), out_dtype)],
        scratch_shapes=[pltpu.VMEM((bm, bn), F32)] if nk > 1 else [], carry=carry)
    return outs[0] if carry is None else (outs[0], carried)


def _mm_pieces(pieces, w, *, name, bm, w_t=False, out_dtype=F32, carry=None):
    kout = w.shape[1] if w_t else w.shape[0]
    starts, total = [], 0
    for piece in pieces:
        starts.append(total)
        total += piece[3]
    mo = pieces[0][0].shape[-2]
    assert mo % bm == 0
    widths = sorted({piece[5] for piece in pieces}, reverse=True)

    def inside(p, k):
        return (k >= starts[p]) & (k < starts[p] + pieces[p][3])

    def a_spec(p):
        _, lead, col0, nblk, _, bk = pieces[p]

        def fn(i, k):
            t = jnp.clip(k - starts[p], 0, nblk - 1)
            if lead is None:
                return (i, col0 + t)
            return (t, i, col0) if lead == "step" else (lead, i, col0 + t)
        return pl.BlockSpec((bm, bk) if lead is None else (None, bm, bk), fn)

    def w_spec(bk):
        def fn(i, k):
            col = 0
            for p, piece in enumerate(pieces):
                if piece[5] == bk:
                    col = col + jnp.where(inside(p, k), piece[4] + k - starts[p], 0)
            return (col, 0) if w_t else (0, col)
        return pl.BlockSpec((bk, kout) if w_t else (kout, bk), fn)

    n_p, n_w = len(pieces), len(widths)

    def body(*refs):
        a_refs, w_refs, o_ref, acc = refs[:n_p], refs[n_p:n_p + n_w], refs[n_p + n_w], refs[n_p + n_w + 1]
        k = pl.program_id(1)

        @pl.when(k == 0)
        def _():
            acc[...] = jnp.zeros_like(acc)

        for p in range(n_p):
            w_ref = w_refs[widths.index(pieces[p][5])]

            @pl.when(inside(p, k))
            def _(p=p, w_ref=w_ref):
                acc[...] += lax.dot_general(a_refs[p][...], w_ref[...], NN if w_t else NT,
                                            preferred_element_type=F32)

        @pl.when(k == total - 1)
        def _():
            o_ref[...] = acc[...].astype(o_ref.dtype)

    outs, carried = _pcall(
        body, [piece[0] for piece in pieces] + [w] * n_w, name=name, grid=(mo // bm, total),
        in_specs=[a_spec(p) for p in range(n_p)] + [w_spec(bk) for bk in widths],
        out_specs=[pl.BlockSpec((bm, kout), lambda i, k: (i, 0))],
        out_shape=[jax.ShapeDtypeStruct((mo, kout), out_dtype)],
        scratch_shapes=[pltpu.VMEM((bm, kout), F32)], carry=carry)
    return outs[0] if carry is None else (outs[0], carried)


def _xchg_out_shapes(arrs, modes):
    return [jax.ShapeDtypeStruct((N_DEV,) + a.shape if m == "gather" else a.shape, a.dtype)
            for a, m in zip(arrs, modes)]


def _xchg_sems(n):
    return [pltpu.SemaphoreType.DMA((n, N_DEV - 1)), pltpu.SemaphoreType.DMA((n, N_DEV - 1)),
            pltpu.SemaphoreType.DMA((n,))]


def _xchg_copies(ins, outs, modes, sems):
    send_sems, recv_sems, local_sems = sems
    x, y, c = lax.axis_index("x"), lax.axis_index("y"), lax.axis_index("c")
    me = 4 * x + 2 * y + c

    def place(q):
        px = 1 - x if (q >> 2) & 1 else x
        py = 1 - y if (q >> 1) & 1 else y
        pc = 1 - c if q & 1 else c
        return (px, py, pc), 4 * px + 2 * py + pc

    sibling, _ = place(1)
    out = dict(local=[], direct=[], landed=[], passed=[], others=[])
    for t, mode in enumerate(modes):
        gather = mode == "gather"
        mine = ins[t] if gather else ins[t].at[me]
        out["local"].append(pltpu.make_async_copy(mine, outs[t].at[me], local_sems.at[t]))
        for q in range(1, N_DEV):
            peer, slot = place(q)
            sem = dict(send_sem=send_sems.at[t, q - 1], recv_sem=recv_sems.at[t, q - 1],
                       device_id_type=pl.DeviceIdType.MESH)
            arrival = pltpu.make_async_remote_copy(src_ref=outs[t].at[slot], dst_ref=outs[t].at[slot],
                                                   device_id=peer, **sem)
            if not gather:
                out["direct"].append(pltpu.make_async_remote_copy(src_ref=ins[t].at[slot], dst_ref=outs[t].at[me],
                                                                  device_id=peer, **sem))
                out["others"].append(arrival)
            elif q == 1 or q % 2 == 0:
                out["direct"].append(pltpu.make_async_remote_copy(src_ref=ins[t], dst_ref=outs[t].at[me],
                                                                  device_id=peer, **sem))
                out["others" if q == 1 else "landed"].append(arrival)
            else:
                _, origin = place(q - 1)
                out["passed"].append(pltpu.make_async_remote_copy(src_ref=outs[t].at[origin], dst_ref=outs[t].at[origin],
                                                                  device_id=sibling, **sem))
                out["others"].append(arrival)
    return out


def _xchg_start(copies):
    for cp in copies["local"] + copies["direct"]:
        cp.start()


def _xchg_wait(copies):
    for cp in copies["landed"]:
        cp.wait_recv()
    for cp in copies["passed"]:
        cp.start()
    for cp in copies["others"]:
        cp.wait_recv()
    for cp in copies["direct"] + copies["passed"]:
        cp.wait_send()
    for cp in copies["local"]:
        cp.wait()


def _exchange(arrs, modes, name):
    n = len(arrs)

    def body(*refs):
        copies = _xchg_copies(refs[:n], refs[n:2 * n], modes, refs[2 * n:])
        _xchg_start(copies)
        _xchg_wait(copies)

    outs = pl.pallas_call(
        body, name=name, in_specs=[HBM_SPEC] * n, out_specs=[HBM_SPEC] * n, out_shape=_xchg_out_shapes(arrs, modes),
        scratch_shapes=_xchg_sems(n), compiler_params=pltpu.CompilerParams(has_side_effects=True),
    )(*arrs)
    return list(outs)


def _pcall(body, operands, *, name, grid, in_specs, out_specs, out_shape, scratch_shapes=(), carry=None):
    out_specs, out_shape, scratch_shapes = list(out_specs), list(out_shape), list(scratch_shapes)
    if carry is None:
        outs = pl.pallas_call(body, name=name, grid=grid, in_specs=in_specs, out_specs=out_specs,
                              out_shape=out_shape, scratch_shapes=scratch_shapes, compiler_params=_params())(*operands)
        return list(outs), []
    arrs, modes = carry
    n, n_in, n_out, n_scr = len(arrs), len(in_specs), len(out_specs), len(scratch_shapes)

    def wrapped(*refs):
        ins, c_in = refs[:n_in], refs[n_in:n_in + n]
        outs, c_out = refs[n_in + n:n_in + n + n_out], refs[n_in + n + n_out:n_in + 2 * n + n_out]
        rest = refs[n_in + 2 * n + n_out:]
        scr, sems = rest[:n_scr], rest[n_scr:]
        first, last = None, None
        for d, size in enumerate(grid):
            f, e = pl.program_id(d) == 0, pl.program_id(d) == size - 1
            first = f if first is None else first & f
            last = e if last is None else last & e

        @pl.when(first)
        def _():
            _xchg_start(_xchg_copies(c_in, c_out, modes, sems))

        body(*ins, *outs, *scr)

        @pl.when(last)
        def _():
            _xchg_wait(_xchg_copies(c_in, c_out, modes, sems))

    outs = pl.pallas_call(
        wrapped, name=name, grid=grid, in_specs=list(in_specs) + [HBM_SPEC] * n,
        out_specs=out_specs + [HBM_SPEC] * n, out_shape=out_shape + _xchg_out_shapes(arrs, modes),
        scratch_shapes=scratch_shapes + _xchg_sems(n), compiler_params=_params())(*operands, *arrs)
    return list(outs[:n_out]), list(outs[n_out:])


def _row_specs(r, n_lat, tm):
    nbl = n_lat // tm
    row = pl.BlockSpec((tm, D_MODEL), lambda i: (i, 0))
    mod = pl.BlockSpec((None, 8, D_MODEL), lambda i: (i // nbl, 0, 0))
    vec = pl.BlockSpec((1, D_MODEL), lambda i: (0, 0))
    return nbl, row, mod, vec


def _rows(vals):
    width = [v for v in vals if v is not None][0].shape[1]
    return jnp.concatenate([jnp.zeros((1, width), F32) if v is None else v for v in vals], axis=0)


def _norm_fwd(xs, g, mods, n_lat, sh, sc, name, res=None, carry=None):
    r = xs.shape[0]
    tm = ROW_TILE
    nbl, row, mod, vec = _row_specs(r, n_lat, tm)

    def norm(x, g_ref, m_ref, h_ref):
        rs = lax.rsqrt(jnp.mean(x * x, axis=-1, keepdims=True) + EPS)
        m = m_ref[...]
        h_ref[...] = ((x * rs * g_ref[...]) * (1.0 + m[sc:sc + 1]) + m[sh:sh + 1]).astype(BF16)

    if res is None:
        def body(x_ref, g_ref, m_ref, h_ref):
            norm(x_ref[...], g_ref, m_ref, h_ref)
        outs, carried = _pcall(body, (xs, g, mods), name=name, grid=(r // tm,), in_specs=[row, vec, mod],
                               out_specs=[row], out_shape=[jax.ShapeDtypeStruct((r, D_MODEL), BF16)], carry=carry)
        return outs[0] if carry is None else (outs[0], carried)

    o, mods_res, gt = res

    def body(x_ref, o_ref, mr_ref, g_ref, m_ref, x1_ref, h_ref):
        x = x_ref[...] + mr_ref[...][gt:gt + 1] * o_ref[...]
        x1_ref[...] = x
        norm(x, g_ref, m_ref, h_ref)

    return pl.pallas_call(
        body, name=name, grid=(r // tm,), in_specs=[row, row, mod, vec, mod], out_specs=[row, row],
        out_shape=[jax.ShapeDtypeStruct((r, D_MODEL), F32), jax.ShapeDtypeStruct((r, D_MODEL), BF16)],
        compiler_params=_params())(xs, o, mods_res, g, mods)


def _rms_bwd(x, g, dy):
    rs = lax.rsqrt(jnp.mean(x * x, axis=-1, keepdims=True) + EPS)
    xh = x * rs
    dxh = dy * g
    dx = rs * (dxh - xh * jnp.mean(dxh * xh, axis=-1, keepdims=True))
    return dx, dy * xh, xh


def _acc_specs(nbl):
    acc_all = pl.BlockSpec((8, D_MODEL), lambda i: (0, 0))
    acc_stream = pl.BlockSpec((None, 8, D_MODEL), lambda i: (i // nbl, 0, 0))
    return acc_all, acc_stream


def _loss_bwd(xs, o, mods, g_final, target, n_lat, name):
    r = xs.shape[0]
    tm = ROW_TILE
    nbl, row, mod, vec = _row_specs(r, n_lat, tm)
    acc_all, acc_stream = _acc_specs(nbl)
    tgt = pl.BlockSpec((tm, D_MODEL), lambda i: (jnp.minimum(i, nbl - 1), 0))

    def body(x_ref, o_ref, m_ref, g_ref, t_ref, dx_ref, do_ref, acc_ref, accs_ref):
        i = pl.program_id(0)
        lat = i < nbl
        gate = m_ref[...][5:6]
        o_val = o_ref[...]
        x = x_ref[...] + gate * o_val
        g = g_ref[...]
        rs = lax.rsqrt(jnp.mean(x * x, axis=-1, keepdims=True) + EPS)
        y = x * rs * g
        err = jnp.where(lat, y - t_ref[...], 0.0)
        loss = 0.5 * jnp.sum(jnp.mean(err * err, axis=-1, keepdims=True), axis=0, keepdims=True)
        dy = err * (1.0 / D_MODEL)
        dx, dg_rows, _ = _rms_bwd(x, g, dy)
        dx_ref[...] = dx
        do_ref[...] = (gate * dx).astype(BF16)

        @pl.when(i == 0)
        def _():
            acc_ref[...] = jnp.zeros_like(acc_ref)

        @pl.when((i == 0) | (i == nbl))
        def _():
            accs_ref[...] = jnp.zeros_like(accs_ref)

        acc_ref[...] += _rows([jnp.sum(dg_rows, axis=0, keepdims=True), jnp.broadcast_to(loss, (1, D_MODEL))]
                              + [None] * 6)
        accs_ref[...] += _rows([None, None, jnp.sum(dx * o_val, axis=0, keepdims=True)] + [None] * 5)

    return pl.pallas_call(
        body, name=name, grid=(r // tm,), in_specs=[row, row, mod, vec, tgt],
        out_specs=[row, row, acc_all, acc_stream],
        out_shape=[jax.ShapeDtypeStruct((r, D_MODEL), F32), jax.ShapeDtypeStruct((r, D_MODEL), BF16),
                   jax.ShapeDtypeStruct((8, D_MODEL), F32), jax.ShapeDtypeStruct((2, 8, D_MODEL), F32)],
        compiler_params=_params())(xs, o, mods, g_final, target)


def _norm_bwd(xs, g, mods, dh, dx_in, n_lat, sh, sc, name, res=None):
    r = xs.shape[0]
    tm = ROW_TILE
    nbl, row, mod, vec = _row_specs(r, n_lat, tm)
    acc_all, acc_stream = _acc_specs(nbl)
    has_res = res is not None

    def body(*refs):
        if has_res:
            x_ref, g_ref, m_ref, dh_ref, dxi_ref, o_ref, mr_ref, dx_ref, do_ref, acc_ref, accs_ref = refs
        else:
            x_ref, g_ref, m_ref, dh_ref, dxi_ref, dx_ref, acc_ref, accs_ref = refs
        i = pl.program_id(0)
        x, g, m, dhv = x_ref[...], g_ref[...], m_ref[...], dh_ref[...].astype(F32)
        dy = dhv * (1.0 + m[sc:sc + 1])
        dxn, dg_rows, xh = _rms_bwd(x, g, dy)
        dx = dxi_ref[...] + dxn
        if has_res:
            dx_ref[...] = dx
        else:
            @pl.when(i < nbl)
            def _():
                dx_ref[...] = dx
        d_gate = None
        if has_res:
            o_val = o_ref[...]
            do_ref[...] = (mr_ref[...][res[2]:res[2] + 1] * dx).astype(BF16)
            d_gate = jnp.sum(dx * o_val, axis=0, keepdims=True)

        @pl.when(i == 0)
        def _():
            acc_ref[...] = jnp.zeros_like(acc_ref)

        @pl.when((i == 0) | (i == nbl))
        def _():
            accs_ref[...] = jnp.zeros_like(accs_ref)

        acc_ref[...] += _rows([jnp.sum(dg_rows, axis=0, keepdims=True)] + [None] * 7)
        accs_ref[...] += _rows([jnp.sum(dhv, axis=0, keepdims=True),
                                jnp.sum(dhv * (xh * g), axis=0, keepdims=True), d_gate] + [None] * 5)

    ins = [xs, g, mods, dh, dx_in]
    in_specs = [row, vec, mod, row, row]
    if has_res:
        out_specs, out_shape = [row], [jax.ShapeDtypeStruct((r, D_MODEL), F32)]
    else:
        out_specs = [pl.BlockSpec((tm, D_MODEL), lambda i: (jnp.minimum(i, nbl - 1), 0))]
        out_shape = [jax.ShapeDtypeStruct((n_lat, D_MODEL), F32)]
    if has_res:
        ins += [res[0], res[1]]
        in_specs += [row, mod]
        out_specs.append(row)
        out_shape.append(jax.ShapeDtypeStruct((r, D_MODEL), BF16))
    out_specs += [acc_all, acc_stream]
    out_shape += [jax.ShapeDtypeStruct((8, D_MODEL), F32), jax.ShapeDtypeStruct((2, 8, D_MODEL), F32)]
    return pl.pallas_call(body, name=name, grid=(r // tm,), in_specs=in_specs, out_specs=out_specs,
                          out_shape=out_shape, compiler_params=_params())(*ins)


def _rotate(t, cos, sin):
    width = t.shape[1]
    reps = width // LANES
    lane = lax.broadcasted_iota(jnp.int32, (1, width), 1)
    first = (lane % HEAD_DIM) < (HEAD_DIM // 2)
    swapped = jnp.where(first, pltpu.roll(t, width - HEAD_DIM // 2, 1), pltpu.roll(t, HEAD_DIM // 2, 1))
    return t * jnp.tile(cos, (1, reps)) + swapped * jnp.tile(sin, (1, reps))


def _rope_fwd(z, cos, sin, name):
    r = z.shape[0]
    tm = ROW_TILE
    kvw = 2 * N_KV_HEADS * HEAD_DIM
    tab = pl.BlockSpec((tm, LANES), lambda i: (i, 0))

    def body(q_ref, kv_ref, c_ref, s_ref, o_ref):
        c, s = c_ref[...], s_ref[...]
        kv = kv_ref[...]
        o_ref[:, :D_MODEL] = (_rotate(q_ref[...].astype(F32), c, s) * (HEAD_DIM ** -0.5)).astype(BF16)
        o_ref[:, D_MODEL:D_MODEL + kvw // 2] = _rotate(kv[:, :kvw // 2].astype(F32), c, s).astype(BF16)
        o_ref[:, D_MODEL + kvw // 2:] = kv[:, kvw // 2:]

    return pl.pallas_call(
        body, name=name, grid=(r // tm,),
        in_specs=[pl.BlockSpec((tm, D_MODEL), lambda i: (i, P_Q // D_MODEL)),
                  pl.BlockSpec((tm, kvw), lambda i: (i, P_KV // kvw)), tab, tab],
        out_specs=pl.BlockSpec((tm, D_MODEL + kvw), lambda i: (i, 0)),
        out_shape=jax.ShapeDtypeStruct((r, D_MODEL + kvw), BF16), compiler_params=_params())(z, z, cos, sin)


def _rope_bwd(dq, dk, dv, cos, sin, name):
    r = dq.shape[0]
    tm = ROW_TILE
    kw = N_KV_HEADS * HEAD_DIM
    tab = pl.BlockSpec((tm, LANES), lambda i: (i, 0))

    def body(dq_ref, dk_ref, dv_ref, c_ref, s_ref, o_ref):
        c, s = c_ref[...], -s_ref[...]
        o_ref[:, :D_MODEL] = (_rotate(dq_ref[...].astype(F32), c, s) * (HEAD_DIM ** -0.5)).astype(BF16)
        o_ref[:, D_MODEL:D_MODEL + kw] = _rotate(dk_ref[...], c, s).astype(BF16)
        o_ref[:, D_MODEL + kw:] = dv_ref[...].astype(BF16)

    return pl.pallas_call(
        body, name=name, grid=(r // tm,),
        in_specs=[pl.BlockSpec((tm, D_MODEL), lambda i: (i, 0)), pl.BlockSpec((tm, kw), lambda i: (i, 0)),
                  pl.BlockSpec((tm, kw), lambda i: (i, 0)), tab, tab],
        out_specs=pl.BlockSpec((tm, D_MODEL + 2 * kw), lambda i: (i, 0)),
        out_shape=jax.ShapeDtypeStruct((r, D_MODEL + 2 * kw), BF16), compiler_params=_params())(dq, dk, dv, cos, sin)


def _attn_setup(i, n_lat, m_ctx, nbl, k_ref, v_ref):
    start = pl.multiple_of(jnp.clip((i - 1) * BLK, 0, n_lat - 3 * BLK), BLK)
    nkeys = 3 * BLK + m_ctx
    rows = lax.broadcasted_iota(jnp.int32, (4 * BLK, nkeys), 0)
    cols = lax.broadcasted_iota(jnp.int32, (4 * BLK, nkeys), 1)
    qpos = i * BLK + (rows & (BLK - 1))
    seen = (cols >= 3 * BLK) | ((jnp.abs(start + cols - qpos) <= WINDOW) & (i < nbl))
    mask = jnp.where(seen, 0.0, NEG)
    kblk = jnp.concatenate([k_ref[pl.ds(start, 3 * BLK), :], k_ref[pl.ds(n_lat, m_ctx), :]], axis=0)
    vblk = jnp.concatenate([v_ref[pl.ds(start, 3 * BLK), :], v_ref[pl.ds(n_lat, m_ctx), :]], axis=0)
    lo = lax.broadcasted_iota(jnp.int32, (1, LANES), 1) < HEAD_DIM
    return start, mask, kblk, vblk, lo


def _stack_heads(ref, kh, lo):
    a = ref[:, (2 * kh) * LANES:(2 * kh + 1) * LANES]
    b = ref[:, (2 * kh + 1) * LANES:(2 * kh + 2) * LANES]
    z = jnp.zeros_like(a)
    return jnp.concatenate([jnp.where(lo, a, z), jnp.where(lo, z, a), jnp.where(lo, b, z), jnp.where(lo, z, b)],
                           axis=0)


def _kv_variants(blk, rolled, kh, lo):
    z = jnp.zeros_like(blk)
    if kh == 0:
        return jnp.where(lo, blk, rolled), jnp.where(lo, blk, z), jnp.where(lo, z, rolled)
    return jnp.where(lo, rolled, blk), jnp.where(lo, rolled, z), jnp.where(lo, z, blk)


NN = (((1,), (0,)), ((), ()))
NT = (((1,), (1,)), ((), ()))
TN = (((0,), (0,)), ((), ()))


def _scores(qst, k2, mask, sink_ref, p, kh):
    s = lax.dot_general(qst, k2, NT, preferred_element_type=F32) + mask
    snk = jnp.concatenate([jnp.full((BLK, 1), sink_ref[p * 8 + kh * 4 + g], F32) for g in range(4)], axis=0)
    return s, snk


def _attn_fwd(qkv, sink, n_lat, m_ctx, name, carry=None):
    r = qkv.shape[0]
    nb, nbl = r // BLK, n_lat // BLK
    kcol = D_MODEL // LANES

    def body(sink_ref, q_ref, k_ref, v_ref, o_ref, lse_ref):
        p, i = pl.program_id(0), pl.program_id(1)
        _, mask, kblk, vblk, lo = _attn_setup(i, n_lat, m_ctx, nbl, k_ref, v_ref)
        kr, vr = pltpu.roll(kblk, HEAD_DIM, 1), pltpu.roll(vblk, HEAD_DIM, 1)
        for kh in range(2):
            k2, _, _ = _kv_variants(kblk, kr, kh, lo)
            _, vlo, vhi = _kv_variants(vblk, vr, kh, lo)
            qst = _stack_heads(q_ref, kh, lo)
            s, snk = _scores(qst, k2, mask, sink_ref, p, kh)
            mx = jnp.maximum(jnp.max(s, axis=-1, keepdims=True), snk)
            pe = jnp.exp(s - mx)
            den = jnp.sum(pe, axis=-1, keepdims=True) + jnp.exp(snk - mx)
            inv = 1.0 / den
            pb = pe.astype(BF16)
            for jp in range(2):
                r0 = 2 * jp * BLK
                pair = (jnp.dot(pb[r0:r0 + BLK], vlo, preferred_element_type=F32)
                        + jnp.dot(pb[r0 + BLK:r0 + 2 * BLK], vhi, preferred_element_type=F32))
                pair = pair * jnp.where(lo, inv[r0:r0 + BLK], inv[r0 + BLK:r0 + 2 * BLK])
                o_ref[:, (2 * kh + jp) * LANES:(2 * kh + jp + 1) * LANES] = pair.astype(BF16)
            lse = mx + jnp.log(den)
            for g in range(4):
                lse_ref[:, kh * 4 + g:kh * 4 + g + 1] = lse[g * BLK:(g + 1) * BLK]

    outs, carried = _pcall(
        body, (sink, qkv, qkv, qkv), name=name, grid=(2, nb),
        in_specs=[SMEM_SPEC,
                  pl.BlockSpec((BLK, 4 * LANES), lambda p, i: (i, p)),
                  pl.BlockSpec((r, LANES), lambda p, i: (0, kcol + p)),
                  pl.BlockSpec((r, LANES), lambda p, i: (0, kcol + 2 + p))],
        out_specs=[pl.BlockSpec((BLK, 4 * LANES), lambda p, i: (i, p)),
                   pl.BlockSpec((None, BLK, 8), lambda p, i: (p, i, 0))],
        out_shape=[jax.ShapeDtypeStruct((r, D_MODEL), BF16), jax.ShapeDtypeStruct((2, r, 8), F32)], carry=carry)
    return outs if carry is None else (outs, carried)


def _attn_bwd(qkv, sink, o, do, lse, n_lat, m_ctx, name, carry=None):
    r = qkv.shape[0]
    nb, nbl = r // BLK, n_lat // BLK
    kcol = D_MODEL // LANES

    def body(sink_ref, q_ref, k_ref, v_ref, o_ref, do_ref, lse_ref, dq_ref, dk_ref, dv_ref, ds_ref):
        p, i = pl.program_id(0), pl.program_id(1)

        @pl.when(i == 0)
        def _():
            dk_ref[...] = jnp.zeros_like(dk_ref)
            dv_ref[...] = jnp.zeros_like(dv_ref)
            ds_ref[...] = jnp.zeros_like(ds_ref)

        start, mask, kblk, vblk, lo = _attn_setup(i, n_lat, m_ctx, nbl, k_ref, v_ref)
        kr, vr = pltpu.roll(kblk, HEAD_DIM, 1), pltpu.roll(vblk, HEAD_DIM, 1)
        lane = lax.broadcasted_iota(jnp.int32, (1, LANES), 1)
        dks, dvs = [], []
        for kh in range(2):
            k2, klo, khi = _kv_variants(kblk, kr, kh, lo)
            v2, _, _ = _kv_variants(vblk, vr, kh, lo)
            qst = _stack_heads(q_ref, kh, lo)
            dost = _stack_heads(do_ref, kh, lo)
            s, snk = _scores(qst, k2, mask, sink_ref, p, kh)
            lse4 = jnp.concatenate([lse_ref[:, kh * 4 + g:kh * 4 + g + 1] for g in range(4)], axis=0)
            pe = jnp.exp(s - lse4)
            dp = lax.dot_general(dost, v2, NT, preferred_element_type=F32)
            deltas = []
            for jp in range(2):
                cols = slice((2 * kh + jp) * LANES, (2 * kh + jp + 1) * LANES)
                prod = do_ref[:, cols].astype(F32) * o_ref[:, cols].astype(F32)
                deltas.append(jnp.sum(jnp.where(lo, prod, 0.0), axis=-1, keepdims=True))
                deltas.append(jnp.sum(jnp.where(lo, 0.0, prod), axis=-1, keepdims=True))
            delta = jnp.concatenate(deltas, axis=0)
            dsc = pe * (dp - delta)
            dsb, pb = dsc.astype(BF16), pe.astype(BF16)
            for jp in range(2):
                r0 = 2 * jp * BLK
                dq_ref[:, (2 * kh + jp) * LANES:(2 * kh + jp + 1) * LANES] = (
                    jnp.dot(dsb[r0:r0 + BLK], klo, preferred_element_type=F32)
                    + jnp.dot(dsb[r0 + BLK:r0 + 2 * BLK], khi, preferred_element_type=F32)).astype(BF16)
            dkf = lax.dot_general(dsb, qst, TN, preferred_element_type=F32)
            dvf = lax.dot_general(pb, dost, TN, preferred_element_type=F32)
            dks.append(dkf + pltpu.roll(dkf, HEAD_DIM, 1))
            dvs.append(dvf + pltpu.roll(dvf, HEAD_DIM, 1))
            contrib = -jnp.exp(snk - lse4) * delta
            for g in range(4):
                tot = jnp.sum(contrib[g * BLK:(g + 1) * BLK], axis=0, keepdims=True)
                ds_ref[0:1, :] += jnp.where(lane == kh * 4 + g, tot, 0.0)
        dk_blk = jnp.where(lo, dks[0], dks[1])
        dv_blk = jnp.where(lo, dvs[0], dvs[1])
        dk_ref[pl.ds(start, 3 * BLK), :] += dk_blk[:3 * BLK]
        dk_ref[pl.ds(n_lat, m_ctx), :] += dk_blk[3 * BLK:]
        dv_ref[pl.ds(start, 3 * BLK), :] += dv_blk[:3 * BLK]
        dv_ref[pl.ds(n_lat, m_ctx), :] += dv_blk[3 * BLK:]

    qspec = pl.BlockSpec((BLK, 4 * LANES), lambda p, i: (i, p))
    outs, carried = _pcall(
        body, (sink, qkv, qkv, qkv, o, do, lse), name=name, grid=(2, nb),
        in_specs=[SMEM_SPEC, qspec,
                  pl.BlockSpec((r, LANES), lambda p, i: (0, kcol + p)),
                  pl.BlockSpec((r, LANES), lambda p, i: (0, kcol + 2 + p)),
                  qspec, qspec, pl.BlockSpec((None, BLK, 8), lambda p, i: (p, i, 0))],
        out_specs=[qspec, pl.BlockSpec((r, LANES), lambda p, i: (0, p)), pl.BlockSpec((r, LANES), lambda p, i: (0, p)),
                   pl.BlockSpec((None, 8, LANES), lambda p, i: (p, 0, 0))],
        out_shape=[jax.ShapeDtypeStruct((r, D_MODEL), BF16), jax.ShapeDtypeStruct((r, 2 * LANES), F32),
                   jax.ShapeDtypeStruct((r, 2 * LANES), F32), jax.ShapeDtypeStruct((2, 8, LANES), F32)], carry=carry)
    return outs if carry is None else (outs, carried)


def _gating_parts(z_ref, gv_ref):
    za = z_ref[...].astype(F32)
    zg = jax.nn.gelu(za)
    u, v = zg[:, :D_MODEL], zg[:, D_MODEL:]
    rs = lax.rsqrt(jnp.mean(v * v, axis=-1, keepdims=True) + EPS)
    return za, u, v, rs, v * rs * gv_ref[...]


def _mix(w_ref, vals):
    vb = vals.astype(BF16)
    return jnp.concatenate(
        [jnp.dot(w_ref[g], vb[:, g * LANES:(g + 1) * LANES], preferred_element_type=F32) for g in range(A_GROUPS)],
        axis=1)


def _gating_fwd(z, ws, bias, g_v, name):
    r = z.shape[0]

    def body(z_ref, w_ref, b_ref, gv_ref, y_ref):
        _, u, _, _, vn = _gating_parts(z_ref, gv_ref)
        y_ref[...] = (u * (_mix(w_ref, vn) + b_ref[...])).astype(BF16)

    return pl.pallas_call(
        body, name=name, grid=(r // CHUNK,),
        in_specs=[pl.BlockSpec((CHUNK, 2 * D_MODEL), lambda i: (i, P_A // (2 * D_MODEL))),
                  pl.BlockSpec((A_GROUPS, CHUNK, CHUNK), lambda i: (0, 0, 0)),
                  pl.BlockSpec((CHUNK, D_MODEL), lambda i: (0, 0)),
                  pl.BlockSpec((1, D_MODEL), lambda i: (0, 0))],
        out_specs=pl.BlockSpec((CHUNK, D_MODEL), lambda i: (i, 0)),
        out_shape=jax.ShapeDtypeStruct((r, D_MODEL), BF16), compiler_params=_params())(z, ws, bias, g_v)


def _gating_bwd(z, ws, ws_t, bias, g_v, dy, name, carry=None):
    r = z.shape[0]

    def body(z_ref, w_ref, wt_ref, b_ref, gv_ref, dy_ref, dz_ref, dw_ref, db_ref, dg_ref):
        i = pl.program_id(0)

        @pl.when(i == 0)
        def _():
            dw_ref[...] = jnp.zeros_like(dw_ref)
            db_ref[...] = jnp.zeros_like(db_ref)
            dg_ref[...] = jnp.zeros_like(dg_ref)

        za, u, v, rs, vn = _gating_parts(z_ref, gv_ref)
        dyv = dy_ref[...].astype(F32)
        du = dyv * (_mix(w_ref, vn) + b_ref[...])
        dmixed = dyv * u
        dvn = _mix(wt_ref, dmixed)
        dmb, vnb = dmixed.astype(BF16), vn.astype(BF16)
        for g in range(A_GROUPS):
            cols = slice(g * LANES, (g + 1) * LANES)
            dw_ref[g] += lax.dot_general(dmb[:, cols], vnb[:, cols], NT, preferred_element_type=F32)
            db_ref[:, g:g + 1] += jnp.sum(dmixed[:, cols], axis=-1, keepdims=True)
        gv = gv_ref[...]
        vh = v * rs
        dg_ref[0:1, :] += jnp.sum(dvn * vh, axis=0, keepdims=True)
        dvh = dvn * gv
        dv = rs * (dvh - vh * jnp.mean(dvh * vh, axis=-1, keepdims=True))
        _, vjp = jax.vjp(jax.nn.gelu, za)
        dz_ref[...] = vjp(jnp.concatenate([du, dv], axis=1))[0].astype(BF16)

    wspec = pl.BlockSpec((A_GROUPS, CHUNK, CHUNK), lambda i: (0, 0, 0))
    outs, carried = _pcall(
        body, (z, ws, ws_t, bias, g_v, dy), name=name, grid=(r // CHUNK,),
        in_specs=[pl.BlockSpec((CHUNK, 2 * D_MODEL), lambda i: (i, P_A // (2 * D_MODEL))), wspec, wspec,
                  pl.BlockSpec((CHUNK, D_MODEL), lambda i: (0, 0)), pl.BlockSpec((1, D_MODEL), lambda i: (0, 0)),
                  pl.BlockSpec((CHUNK, D_MODEL), lambda i: (i, 0))],
        out_specs=[pl.BlockSpec((CHUNK, 2 * D_MODEL), lambda i: (i, 0)), wspec,
                   pl.BlockSpec((CHUNK, A_GROUPS), lambda i: (0, 0)), pl.BlockSpec((8, D_MODEL), lambda i: (0, 0))],
        out_shape=[jax.ShapeDtypeStruct((r, 2 * D_MODEL), BF16), jax.ShapeDtypeStruct((A_GROUPS, CHUNK, CHUNK), F32),
                   jax.ShapeDtypeStruct((CHUNK, A_GROUPS), F32), jax.ShapeDtypeStruct((8, D_MODEL), F32)], carry=carry)
    return outs if carry is None else (outs, carried)


def _scr_rows(j, n_lat):
    s = pl.multiple_of(j * CONV_CHUNK, CONV_CHUNK)
    shift = jnp.where(j >= n_lat // CONV_CHUNK, 2 * PAD, PAD)
    return s, pl.multiple_of(s + shift, PAD)


def _taps(scr, j, n_lat):
    c = CONV_CHUNK
    s, at = _scr_rows(j, n_lat)
    ext = scr[pl.ds(pl.multiple_of(at - PAD, PAD), c + 2 * PAD), :]
    xm = pltpu.roll(ext, 1, 0)[PAD:PAD + c]
    xp = pltpu.roll(ext, c + 2 * PAD - 1, 0)[PAD:PAD + c]
    return s, xm, ext[PAD:PAD + c], xp


def _zero_pads(scr, r, n_lat):
    for at in (0, PAD + n_lat, 2 * PAD + r):
        scr[pl.ds(at, PAD), :] = jnp.zeros((PAD, LANES), F32)


def _col(arr_cols, c0):
    return pl.BlockSpec((arr_cols, LANES), lambda c: (0, c0 + c))


def _ffn_conv_fwd(up, w, n_lat, name):
    r = up.shape[0]
    nct = D_FF // LANES
    nchunk = r // CONV_CHUNK

    def body(a_ref, g_ref, w_ref, f_ref, scr):
        _zero_pads(scr, r, n_lat)

        def fill(j, _):
            s, at = _scr_rows(j, n_lat)
            scr[pl.ds(at, CONV_CHUNK), :] = a_ref[pl.ds(s, CONV_CHUNK), :].astype(F32)
            return 0
        lax.fori_loop(0, nchunk, fill, 0)
        wv = w_ref[...]

        def step(j, _):
            s, xm, x0, xp = _taps(scr, j, n_lat)
            ca = wv[0:1] * xm + wv[1:2] * x0 + wv[2:3] * xp
            gv = g_ref[pl.ds(s, CONV_CHUNK), :].astype(F32)
            f_ref[pl.ds(s, CONV_CHUNK), :] = (ca * _sigmoid(ca) * gv).astype(BF16)
            return 0
        lax.fori_loop(0, nchunk, step, 0)

    return pl.pallas_call(
        body, name=name, grid=(nct,),
        in_specs=[_col(r, 0), _col(r, nct), _col(3, 0)],
        out_specs=_col(r, 0), out_shape=jax.ShapeDtypeStruct((r, D_FF), BF16),
        scratch_shapes=[pltpu.VMEM((r + 3 * PAD, LANES), F32)], compiler_params=_params())(up, up, w)


def _ffn_conv_bwd(up, w, df, n_lat, name, carry=None):
    r = up.shape[0]
    nct = D_FF // LANES
    nchunk = r // CONV_CHUNK

    def body(a_ref, g_ref, w_ref, df_ref, dup_ref, dw_ref, scr, scr2):
        _zero_pads(scr, r, n_lat)
        _zero_pads(scr2, r, n_lat)

        def fill(j, _):
            s, at = _scr_rows(j, n_lat)
            scr[pl.ds(at, CONV_CHUNK), :] = a_ref[pl.ds(s, CONV_CHUNK), :].astype(F32)
            return 0
        lax.fori_loop(0, nchunk, fill, 0)
        wv = w_ref[...]

        def first(j, carry):
            s, xm, x0, xp = _taps(scr, j, n_lat)
            ca = wv[0:1] * xm + wv[1:2] * x0 + wv[2:3] * xp
            sg = _sigmoid(ca)
            gv = g_ref[pl.ds(s, CONV_CHUNK), :].astype(F32)
            dfv = df_ref[pl.ds(s, CONV_CHUNK), :].astype(F32)
            dup_ref[1, pl.ds(s, CONV_CHUNK), :] = (dfv * ca * sg).astype(BF16)
            dca = dfv * gv * (sg * (1.0 + ca * (1.0 - sg)))
            scr2[pl.ds(_scr_rows(j, n_lat)[1], CONV_CHUNK), :] = dca
            return tuple(cw + jnp.sum(dca * xv, axis=0, keepdims=True) for cw, xv in zip(carry, (xm, x0, xp)))
        zero = jnp.zeros((1, LANES), F32)
        dws = lax.fori_loop(0, nchunk, first, (zero, zero, zero))
        dw_ref[...] = _rows(list(dws) + [None] * 5)

        def second(j, _):
            s, ym, y0, yp = _taps(scr2, j, n_lat)
            dup_ref[0, pl.ds(s, CONV_CHUNK), :] = (wv[0:1] * yp + wv[1:2] * y0 + wv[2:3] * ym).astype(BF16)
            return 0
        lax.fori_loop(0, nchunk, second, 0)

    outs, carried = _pcall(
        body, (up, up, w, df), name=name, grid=(nct,),
        in_specs=[_col(r, 0), _col(r, nct), _col(3, 0), _col(r, 0)],
        out_specs=[pl.BlockSpec((2, r, LANES), lambda c: (0, 0, c)), _col(8, 0)],
        out_shape=[jax.ShapeDtypeStruct((2, r, D_FF), BF16), jax.ShapeDtypeStruct((8, D_FF), F32)],
        scratch_shapes=[pltpu.VMEM((r + 3 * PAD, LANES), F32), pltpu.VMEM((r + 3 * PAD, LANES), F32)], carry=carry)
    return outs if carry is None else (outs, carried)


def _sconv_fwd(z, w, n_lat, name):
    r = z.shape[0]
    nct = D_MODEL // LANES
    nchunk = r // CONV_CHUNK
    c0 = P_B // LANES

    def body(bg_ref, cg_ref, hb_ref, w_ref, y_ref, scr):
        _zero_pads(scr, r, n_lat)

        def fill(j, _):
            s, at = _scr_rows(j, n_lat)
            rows = pl.ds(s, CONV_CHUNK)
            scr[pl.ds(at, CONV_CHUNK), :] = cg_ref[rows, :].astype(F32) * hb_ref[rows, :].astype(F32)
            return 0
        lax.fori_loop(0, nchunk, fill, 0)
        wv = w_ref[...]

        def step(j, _):
            s, xm, x0, xp = _taps(scr, j, n_lat)
            conv = wv[0:1] * xm + wv[1:2] * x0 + wv[2:3] * xp
            y_ref[pl.ds(s, CONV_CHUNK), :] = (bg_ref[pl.ds(s, CONV_CHUNK), :].astype(F32) * conv).astype(BF16)
            return 0
        lax.fori_loop(0, nchunk, step, 0)

    return pl.pallas_call(
        body, name=name, grid=(nct,),
        in_specs=[_col(r, c0), _col(r, c0 + nct), _col(r, c0 + 2 * nct), _col(3, 0)],
        out_specs=_col(r, 0), out_shape=jax.ShapeDtypeStruct((r, D_MODEL), BF16),
        scratch_shapes=[pltpu.VMEM((r + 3 * PAD, LANES), F32)], compiler_params=_params())(z, z, z, w)


def _sconv_bwd(z, w, dy, n_lat, name):
    r = z.shape[0]
    nct = D_MODEL // LANES
    nchunk = r // CONV_CHUNK
    c0 = P_B // LANES

    def body(bg_ref, cg_ref, hb_ref, w_ref, dy_ref, dz_ref, dw_ref, scr, scr2):
        _zero_pads(scr, r, n_lat)
        _zero_pads(scr2, r, n_lat)

        def fill(j, _):
            s, at = _scr_rows(j, n_lat)
            rows = pl.ds(s, CONV_CHUNK)
            scr[pl.ds(at, CONV_CHUNK), :] = cg_ref[rows, :].astype(F32) * hb_ref[rows, :].astype(F32)
            return 0
        lax.fori_loop(0, nchunk, fill, 0)
        wv = w_ref[...]

        def first(j, carry):
            s, xm, x0, xp = _taps(scr, j, n_lat)
            rows = pl.ds(s, CONV_CHUNK)
            conv = wv[0:1] * xm + wv[1:2] * x0 + wv[2:3] * xp
            dyv = dy_ref[rows, :].astype(F32)
            dz_ref[0, rows, :] = (dyv * conv).astype(BF16)
            dconv = dyv * bg_ref[rows, :].astype(F32)
            scr2[pl.ds(_scr_rows(j, n_lat)[1], CONV_CHUNK), :] = dconv
            return tuple(cw + jnp.sum(dconv * xv, axis=0, keepdims=True) for cw, xv in zip(carry, (xm, x0, xp)))
        zero = jnp.zeros((1, LANES), F32)
        dws = lax.fori_loop(0, nchunk, first, (zero, zero, zero))
        dw_ref[...] = _rows(list(dws) + [None] * 5)

        def second(j, _):
            s, ym, y0, yp = _taps(scr2, j, n_lat)
            rows = pl.ds(s, CONV_CHUNK)
            dx = wv[0:1] * yp + wv[1:2] * y0 + wv[2:3] * ym
            dz_ref[1, rows, :] = (dx * hb_ref[rows, :].astype(F32)).astype(BF16)
            dz_ref[2, rows, :] = (dx * cg_ref[rows, :].astype(F32)).astype(BF16)
            return 0
        lax.fori_loop(0, nchunk, second, 0)

    return pl.pallas_call(
        body, name=name, grid=(nct,),
        in_specs=[_col(r, c0), _col(r, c0 + nct), _col(r, c0 + 2 * nct), _col(3, 0), _col(r, 0)],
        out_specs=[pl.BlockSpec((3, r, LANES), lambda c: (0, 0, c)), _col(8, 0)],
        out_shape=[jax.ShapeDtypeStruct((3, r, D_MODEL), BF16), jax.ShapeDtypeStruct((8, D_MODEL), F32)],
        scratch_shapes=[pltpu.VMEM((r + 3 * PAD, LANES), F32), pltpu.VMEM((r + 3 * PAD, LANES), F32)],
        compiler_params=_params())(z, z, z, w, dy)


def _merge_proj_out(z, b_gate, ps, w_out, bm, name):
    r = z.shape[0]
    sub = ROW_TILE
    row = pl.BlockSpec((bm, D_MODEL), lambda i: (i, 0))

    def body(zg_ref, b_ref, p0_ref, p1_ref, p2_ref, w_ref, m_ref, o_ref):
        def chunk(c, _):
            rows = pl.ds(pl.multiple_of(c * sub, sub), sub)
            gates = _sigmoid(zg_ref[rows, :].astype(F32) + b_ref[...])
            acc = None
            for t, p_ref in enumerate((p0_ref, p1_ref, p2_ref)):
                term = gates[:, t * D_MODEL:(t + 1) * D_MODEL] * p_ref[rows, :].astype(F32)
                acc = term if acc is None else acc + term
            m_ref[rows, :] = acc.astype(BF16)
            return 0
        lax.fori_loop(0, bm // sub, chunk, 0)
        o_ref[...] = jnp.dot(m_ref[...], w_ref[...], preferred_element_type=F32).astype(o_ref.dtype)

    return pl.pallas_call(
        body, name=name, grid=(r // bm,),
        in_specs=[pl.BlockSpec((bm, 3 * D_MODEL), lambda i: (i, 0)), pl.BlockSpec((1, 3 * D_MODEL), lambda i: (0, 0)),
                  row, row, row, pl.BlockSpec((D_MODEL, D_MODEL), lambda i: (0, 0))],
        out_specs=[row, row],
        out_shape=[jax.ShapeDtypeStruct((r, D_MODEL), BF16), jax.ShapeDtypeStruct((r, D_MODEL), BF16)],
        compiler_params=_params())(z, b_gate, *ps, w_out)


def _merge_bwd(z, b_gate, ps, dmerged, name):
    r = z.shape[0]
    tm = ROW_TILE
    row = pl.BlockSpec((tm, D_MODEL), lambda i: (i, 0))
    wide = pl.BlockSpec((tm, 3 * D_MODEL), lambda i: (i, 0))

    def body(zg_ref, b_ref, p0_ref, p1_ref, p2_ref, dm_ref, d0_ref, d1_ref, d2_ref, dz_ref, db_ref):
        @pl.when(pl.program_id(0) == 0)
        def _():
            db_ref[...] = jnp.zeros_like(db_ref)

        gates = _sigmoid(zg_ref[...].astype(F32) + b_ref[...])
        dm = dm_ref[...].astype(F32)
        for t, (p_ref, d_ref) in enumerate(((p0_ref, d0_ref), (p1_ref, d1_ref), (p2_ref, d2_ref))):
            cols = slice(t * D_MODEL, (t + 1) * D_MODEL)
            gt = gates[:, cols]
            d_ref[...] = (dm * gt).astype(BF16)
            dlogit = dm * p_ref[...].astype(F32) * gt * (1.0 - gt)
            dz_ref[:, cols] = dlogit.astype(BF16)
            db_ref[0:1, cols] += jnp.sum(dlogit, axis=0, keepdims=True)

    shp = jax.ShapeDtypeStruct((r, D_MODEL), BF16)
    return pl.pallas_call(
        body, name=name, grid=(r // tm,),
        in_specs=[wide, pl.BlockSpec((1, 3 * D_MODEL), lambda i: (0, 0)), row, row, row, row],
        out_specs=[row, row, row, wide, pl.BlockSpec((8, 3 * D_MODEL), lambda i: (0, 0))],
        out_shape=[shp, shp, shp, jax.ShapeDtypeStruct((r, 3 * D_MODEL), BF16),
                   jax.ShapeDtypeStruct((8, 3 * D_MODEL), F32)],
        compiler_params=_params())(z, b_gate, *ps, dmerged)


def _sum_slots(buf, name):
    s, rows, _ = buf.shape
    whole_bytes = s * rows * LANES * 4
    tr = rows if whole_bytes <= VMEM_LIMIT // 8 else _pick(rows, [512, 256, 128, 64, 32, 16, 8])

    def body(b_ref, o_ref):
        acc = b_ref[0]
        for t in range(1, s):
            acc = acc + b_ref[t]
        o_ref[...] = acc

    return pl.pallas_call(
        body, name=name, grid=(rows // tr,),
        in_specs=[pl.BlockSpec((s, tr, LANES), lambda i: (0, i, 0))],
        out_specs=pl.BlockSpec((tr, LANES), lambda i: (i, 0)),
        out_shape=jax.ShapeDtypeStruct((rows, LANES), F32), compiler_params=_params())(buf)


def _adamw(w, gsrcs, m, v, name, scale=None):
    nl, rows, cols = w.shape
    assert len(gsrcs) == nl
    s = gsrcs[0].shape[0]
    tr = _pick(rows, [304, 352, 256, 128, 64, 32, 16, 8])
    blk = pl.BlockSpec((None, tr, cols), lambda l, i: (l, i, 0))
    c1 = 1.0 / (1.0 - ADAM_B1 ** ADAM_STEP)
    c2 = 1.0 / (1.0 - ADAM_B2 ** ADAM_STEP)

    def gspec(t):
        return pl.BlockSpec((s, tr, cols), lambda l, i: (0, jnp.where(l == t, i, 0), 0))

    def body(*refs):
        w_ref, g_refs, (m_ref, v_ref) = refs[0], refs[1:1 + nl], refs[1 + nl:3 + nl]
        rest = refs[3 + nl:]
        if scale is not None:
            sc_ref, rest = rest[0], rest[1:]
        go_ref, d_ref, mo_ref, vo_ref = rest
        layer = pl.program_id(0)
        g = None
        for t in range(nl):
            gt = g_refs[t][0].astype(F32)
            for q in range(1, s):
                gt = gt + g_refs[t][q].astype(F32)
            g = gt if g is None else jnp.where(layer == t, gt, g)
        if scale is not None:
            g = g * sc_ref[...]
        mn = ADAM_B1 * m_ref[...] + (1.0 - ADAM_B1) * g
        vn = ADAM_B2 * v_ref[...] + (1.0 - ADAM_B2) * (g * g)
        go_ref[...] = g
        mo_ref[...] = mn
        vo_ref[...] = vn
        d_ref[...] = -ADAM_LR * ((mn * c1) / (jnp.sqrt(vn * c2) + ADAM_EPS) + ADAM_WD * w_ref[...])

    shp = jax.ShapeDtypeStruct((nl, rows, cols), F32)
    ins = [w] + list(gsrcs) + [m, v] + ([] if scale is None else [scale])
    return pl.pallas_call(
        body, name=name, grid=(nl, rows // tr),
        in_specs=[blk] + [gspec(t) for t in range(nl)] + [blk, blk] + ([] if scale is None else [blk]),
        out_specs=[blk] * 4, out_shape=[shp] * 4, compiler_params=_params())(*ins)


def _pack(arrs):
    flat = []
    for a in arrs:
        a = a.reshape(-1).astype(F32)
        pad = (-a.shape[0]) % (8 * LANES)
        flat.append(jnp.pad(a, (0, pad)) if pad else a)
    return jnp.concatenate(flat).reshape(-1, LANES)


def _unpack(buf, shapes, lead=()):
    out, row = [], 0
    for shp in shapes:
        n = 1
        for d in shp:
            n *= d
        nrows = -(-n // (8 * LANES)) * 8
        piece = buf[..., row:row + nrows, :].reshape(lead + (nrows * LANES,))[..., :n]
        out.append(piece.reshape(lead + tuple(shp)))
        row += nrows
    return out


def _silu(x):
    return x * jax.nn.sigmoid(x)


def _rope_tables(n_lat, m_ctx):
    pos = jnp.arange(n_lat)
    row = (pos // GRID_W).astype(F32)
    col = (pos % GRID_W).astype(F32)
    half = HEAD_DIM // 2
    inv = ROPE_THETA ** (-jnp.arange(0, half, 2, dtype=F32) / half)
    ang = jnp.concatenate([row[:, None] * inv, col[:, None] * inv], axis=-1)
    cos, sin = jnp.cos(ang), jnp.sin(ang)
    cos2 = jnp.tile(jnp.concatenate([cos, cos], axis=-1), (1, LANES // HEAD_DIM))
    sin2 = jnp.tile(jnp.concatenate([-sin, sin], axis=-1), (1, LANES // HEAD_DIM))
    return (jnp.concatenate([cos2, jnp.ones((m_ctx, LANES), F32)], axis=0),
            jnp.concatenate([sin2, jnp.zeros((m_ctx, LANES), F32)], axis=0))


def _to_slots(full, axis):
    shp = full.shape
    new = shp[:axis] + (N_DEV, shp[axis] // N_DEV) + shp[axis + 1:]
    return jnp.moveaxis(full.reshape(new), axis, 0)


def _from_slots(slots, axis):
    moved = jnp.moveaxis(slots, 0, axis)
    shp = moved.shape
    return moved.reshape(shp[:axis] + (shp[axis] * shp[axis + 1],) + shp[axis + 2:])


def _permute_in(wt, name):
    blk = 512
    segs = [(OFF_G, IN_W), (0, OFF_K), (OFF_A, OFF_B), (OFF_B, OFF_G), (OFF_K, OFF_A)]

    def source(j):
        src, at = 0, 0
        for lo, hi in segs:
            n = (hi - lo) // blk
            src = src + jnp.where((j >= at) & (j < at + n), j - at + lo // blk, 0)
            at += n
        return (src, 0)

    def body(i_ref, o_ref):
        o_ref[...] = i_ref[...]

    return pl.pallas_call(
        body, name=name, grid=(IN_W // blk,), in_specs=[pl.BlockSpec((blk, D_MODEL), source)],
        out_specs=pl.BlockSpec((blk, D_MODEL), lambda j: (j, 0)), out_shape=jax.ShapeDtypeStruct(wt.shape, wt.dtype),
        compiler_params=_params())(wt)


def kernel(x, c, ctx, c_ctx, w_mod, b_mod, g_mix, w_in, b_gate, sink, w_spatial, b_spatial, g_v, w_sconv, w_branch, w_out, g_ffn, w_up, w_fconv, w_down, g_final, loss_target, m_c_ctx, m_w_mod, m_b_mod, m_g_mix, m_w_in, m_b_gate, m_sink, m_w_spatial, m_b_spatial, m_g_v, m_w_sconv, m_w_branch, m_w_out, m_g_ffn, m_w_up, m_w_fconv, m_w_down, m_g_final, v_c_ctx, v_w_mod, v_b_mod, v_g_mix, v_w_in, v_b_gate, v_sink, v_w_spatial, v_b_spatial, v_g_v, v_w_sconv, v_w_branch, v_w_out, v_g_ffn, v_w_up, v_w_fconv, v_w_down, v_g_final):
    n_lat, m_ctx = x.shape[1], ctx.shape[1]
    r = n_lat + m_ctx
    me = 4 * lax.axis_index("x") + 2 * lax.axis_index("y") + lax.axis_index("c")
    mod_w = w_mod.shape[2]
    bm = _pick(r, [1408, 768, 256])
    bm_fused = _pick(r, [768, 256])

    tr = lambda a: jnp.swapaxes(a, -1, -2)
    branches = [(f"br{t}", w_branch[:, t]) for t in range(3)]
    shards = {}
    for kind, wt in [("in", tr(w_in)), ("out", w_out), ("up", tr(w_up)), ("dn", w_down)] + branches:
        wb = wt.astype(BF16)
        for l in range(DEPTH):
            shards[kind, l] = wb[l]
    full = {}

    def arrive(items, got):
        for key, slots in zip(items, got):
            wfull = _from_slots(slots, 0)
            full[key] = _permute_in(wfull, f"permute_in{key[1]}") if key[0] == "in" else wfull

    def gather_of(items):
        return [shards[key] for key in items], ["gather"] * len(items)

    small_shapes = [c.shape, w_sconv.shape, w_fconv.shape]
    (g_small,) = _exchange([_pack([c, w_sconv, w_fconv])], ["gather"], "gather_first")
    c_all, sconv_all, fconv_all = _unpack(g_small, small_shapes, lead=(N_DEV,))
    c_all = c_all.reshape(N_DEV, D_MODEL)
    w_sconv_full = _from_slots(sconv_all, 2)
    w_fconv_full = _from_slots(fconv_all, 2)

    act = jnp.concatenate([_silu(c_all), _silu(c_ctx)[None], jnp.zeros((7, D_MODEL), F32)], axis=0)
    mod_part = jnp.stack([_mm(act, w_mod[l], name=f"mod_fwd{l}", bm=16, bn=mod_w, bk=D_MODEL, out_dtype=F32)
                          for l in range(DEPTH)])
    (mod_all,) = _exchange([mod_part], ["gather"], "gather_mod")
    mod_full = _from_slots(mod_all, 2) + b_mod[:, None, :]
    mods = []
    for l in range(DEPTH):
        mine = lax.dynamic_index_in_dim(mod_full[l], me, axis=0, keepdims=False).reshape(6, D_MODEL)
        theirs = mod_full[l, N_DEV].reshape(6, D_MODEL)
        mods.append(jnp.pad(jnp.stack([mine, theirs]), ((0, 0), (0, 2), (0, 0))))

    cos, sin = _rope_tables(n_lat, m_ctx)
    xs = jnp.concatenate([x[0], ctx[0]], axis=0)
    ws_b = w_spatial.astype(BF16)
    ws_t = jnp.swapaxes(w_spatial, 2, 3).astype(BF16)
    vec = lambda a: a.reshape(1, -1)

    saved = []
    res = None
    for l in range(DEPTH):
        s = {}
        if res is None:
            s["x0"] = xs
            s["h"], got = _norm_fwd(xs, vec(g_mix[l]), mods[l], n_lat, 0, 1, f"norm_mix{l}",
                                    carry=gather_of([("in", l)]))
            arrive([("in", l)], got)
        else:
            s["x0"], s["h"] = _norm_fwd(xs, vec(g_mix[l]), mods[l], n_lat, 0, 1, f"norm_mix{l}", res=res)
        items = [("br0", l), ("br1", l), ("br2", l), ("out", l), ("up", l)]
        s["z"], got = _mm(s["h"], full["in", l], tb=True, name=f"proj_in{l}", bm=bm, bn=_pick(IN_W, [2432, 512]),
                          bk=D_MODEL, out_dtype=BF16, cols_outer=True, carry=gather_of(items))
        arrive(items, got)
        s["qkv"] = _rope_fwd(s["z"], cos, sin, f"rope{l}")
        items = [("dn", l)] + ([("in", l + 1)] if l + 1 < DEPTH else [])
        (s["y0"], s["lse"]), got = _attn_fwd(s["qkv"], sink[l], n_lat, m_ctx, f"attn{l}", carry=gather_of(items))
        arrive(items, got)
        s["bias"] = jnp.repeat(b_spatial[l].T, LANES, axis=1)
        s["y1"] = _gating_fwd(s["z"], ws_b[l], s["bias"], vec(g_v[l]), f"gating{l}")
        s["y2"] = _sconv_fwd(s["z"], w_sconv_full[l], n_lat, f"sconv{l}")
        s["p"] = [_mm(s[f"y{t}"], full[f"br{t}", l], name=f"branch{l}_{t}", bm=bm, bn=D_MODEL, bk=D_MODEL,
                      out_dtype=BF16) for t in range(3)]
        s["merged"], s["o"] = _merge_proj_out(s["z"], vec(b_gate[l]), s["p"], full["out", l], bm_fused,
                                              f"proj_out{l}")
        s["x1"], s["h2"] = _norm_fwd(s["x0"], vec(g_ffn[l]), mods[l], n_lat, 3, 4, f"norm_ffn{l}",
                                     res=(s["o"], mods[l], 2))
        s["up"] = _mm(s["h2"], full["up", l], tb=True, name=f"ffn_up{l}", bm=bm, bn=_pick(2 * D_FF, [1408]),
                      bk=D_MODEL, out_dtype=BF16, cols_outer=True)
        s["f"] = _ffn_conv_fwd(s["up"], w_fconv_full[l], n_lat, f"ffn_conv{l}")
        s["dd"] = _mm(s["f"], full["dn", l], name=f"ffn_down{l}", bm=bm, bn=D_MODEL, bk=D_FF, out_dtype=BF16)
        saved.append(s)
        xs, res = s["x1"], (s["dd"], mods[l], 5)

    top = saved[DEPTH - 1]
    dxs, d_dd, acc_final, accs_top = _loss_bwd(top["x1"], top["dd"], mods[DEPTH - 1], vec(g_final), loss_target[0],
                                               n_lat, "loss")
    loss_part = acc_final[1, 0]
    dg_final = acc_final[0]
    dmods = [None] * DEPTH
    gate2 = accs_top[:, 2]
    grads = {k: [None] * DEPTH for k in ("g_mix", "g_ffn", "g_v", "b_gate", "sink", "w_spatial", "b_spatial",
                                         "w_sconv", "w_fconv", "w_in", "w_branch", "w_out", "w_up", "w_down")}
    bk_r = _pick(r, [2816, 768, 256])
    small_names = ["g_ffn", "g_v", "b_gate", "sink", "w_spatial", "b_spatial", "w_sconv", "w_fconv"]
    recv = {}
    small_recv, small_shapes_of = {}, {}

    def small_pack(l):
        arrs = [grads[k][l] for k in small_names]
        if l > 0:
            arrs.append(grads["g_mix"][l])
        if l == DEPTH - 1:
            arrs += [loss_part.reshape(1), dg_final]
        small_shapes_of[l] = [a.shape for a in arrs]
        return _pack(arrs)

    for l in reversed(range(DEPTH)):
        s = saved[l]
        df = _mm(d_dd, full["dn", l], tb=True, name=f"d_ffn_down{l}", bm=bm, bn=_pick(D_FF, [1408]), bk=D_MODEL,
                 out_dtype=BF16)
        grads["w_down"][l] = _mm(s["f"], d_dd, ta=True, name=f"g_ffn_down{l}", bm=_pick(D_FF, [1408]), bn=D_MODEL,
                                 bk=bk_r, out_dtype=BF16)
        arrs, modes = [_to_slots(grads["w_down"][l], 0)], ["a2a"]
        if l + 1 < DEPTH:
            arrs.append(small_pack(l + 1))
            modes.append("gather")
        (dup, dwf), got = _ffn_conv_bwd(s["up"], w_fconv_full[l], df, n_lat, f"d_ffn_conv{l}", carry=(arrs, modes))
        recv["dn", l] = got[0]
        if l + 1 < DEPTH:
            small_recv[l + 1] = got[1]
        grads["w_fconv"][l] = dwf[:3]
        dh2 = _mm_pieces([(dup, "step", 0, 2, 0, D_FF)], full["up", l], w_t=True, name=f"d_ffn_up{l}", bm=bm,
                         out_dtype=BF16)
        kb = 1408
        nbh = D_FF // kb
        grads["w_up"][l] = _mm(
            dup, s["h2"], ta=True, name=f"g_ffn_up{l}", bm=kb, bn=D_MODEL, bk=bk_r, out_dtype=BF16,
            a_spec=((None, bk_r, kb), lambda i, j, k: (i // nbh, k, i % nbh), 2 * D_FF))
        dx1, d_o, acc, accs = _norm_bwd(s["x1"], vec(g_ffn[l]), mods[l], dh2, dxs, n_lat, 3, 4, f"d_norm_ffn{l}",
                                        res=(s["o"], mods[l], 2))
        grads["g_ffn"][l] = acc[0]
        shift2, scale2, gate1 = accs[:, 0], accs[:, 1], accs[:, 2]
        dmerged = _mm(d_o, full["out", l], tb=True, name=f"d_proj_out{l}", bm=bm, bn=D_MODEL, bk=D_MODEL,
                      out_dtype=BF16)
        grads["w_out"][l] = _mm(s["merged"], d_o, ta=True, name=f"g_proj_out{l}", bm=D_MODEL, bn=D_MODEL, bk=bk_r,
                                out_dtype=BF16)
        dp0, dp1, dp2, dz_g, dbg = _merge_bwd(s["z"], vec(b_gate[l]), s["p"], dmerged, f"d_merge{l}")
        grads["b_gate"][l] = dbg[0]
        dps = (dp0, dp1, dp2)
        dys = [_mm(dps[t], full[f"br{t}", l], tb=True, name=f"d_branch{l}_{t}", bm=bm, bn=D_MODEL, bk=D_MODEL,
                   out_dtype=BF16) for t in range(3)]
        grads["w_branch"][l] = jnp.stack(
            [_mm(s[f"y{t}"], dps[t], ta=True, name=f"g_branch{l}_{t}", bm=D_MODEL, bn=D_MODEL, bk=bk_r,
                 out_dtype=BF16) for t in range(3)])
        arrs = [_to_slots(grads["w_up"][l], 0)] + ([_to_slots(grads["w_in"][l + 1], 0)] if l + 1 < DEPTH else [])
        (dq, dk, dv, dsk), got = _attn_bwd(s["qkv"], sink[l], s["y0"], dys[0], s["lse"], n_lat, m_ctx, f"d_attn{l}",
                                           carry=(arrs, ["a2a"] * len(arrs)))
        recv["up", l] = got[0]
        if l + 1 < DEPTH:
            recv["in", l + 1] = got[1]
        grads["sink"][l] = dsk[:, 0, :8].reshape(N_HEADS)
        dz_qkv = _rope_bwd(dq, dk, dv, cos, sin, f"d_rope{l}")
        (dz_a, dws, dbs, dgv), (recv["out", l], recv["br", l]) = _gating_bwd(
            s["z"], ws_b[l], ws_t[l], s["bias"], vec(g_v[l]), dys[1], f"d_gating{l}",
            carry=([_to_slots(grads["w_out"][l], 0), _to_slots(grads["w_branch"][l], 1)], ["a2a"] * 2))
        grads["w_spatial"][l], grads["b_spatial"][l], grads["g_v"][l] = dws, dbs.T, dgv[0]
        dz_b, dwsc = _sconv_bwd(s["z"], w_sconv_full[l], dys[2], n_lat, f"d_sconv{l}")
        grads["w_sconv"][l] = dwsc[:3]
        kvw = 2 * N_KV_HEADS * HEAD_DIM
        pieces = [(dz_g, None, 0, 3, 0, D_MODEL), (dz_qkv, None, 0, 1, P_Q // D_MODEL, D_MODEL),
                  (dz_a, None, 0, 2, P_A // D_MODEL, D_MODEL), (dz_b, "step", 0, 3, P_B // D_MODEL, D_MODEL),
                  (dz_qkv, None, D_MODEL // kvw, 1, P_KV // kvw, kvw)]
        gw = lambda a, nm, rows, **kw: _mm(a, s["h"], ta=True, name=f"g_proj_in{l}_{nm}", bm=rows, bn=D_MODEL,
                                           bk=bk_r, out_dtype=BF16, **kw)
        if l > 0:
            gw_g = gw(dz_g, "gate", 1536)
        else:
            gw_g, (small_recv[0],) = gw(dz_g, "gate", 1536, carry=([small_pack(0)], ["gather"]))
        gw_qkv = gw(dz_qkv, "qkv", 1536)
        gw_a = gw(dz_a, "gating", 1024)
        gw_b = [gw(dz_b, f"sconv{t}", 1024, a_lead=t) for t in range(3)]
        grads["w_in"][l] = jnp.concatenate([gw_qkv, gw_a] + gw_b + [gw_g], axis=0)
        bm_in = _pick(r, [1408, 768, 256])
        if l > 0:
            dh = _mm_pieces(pieces, full["in", l], w_t=True, name=f"d_proj_in{l}", bm=bm_in, out_dtype=BF16)
        else:
            dh, (recv["in", 0],) = _mm_pieces(
                pieces, full["in", l], w_t=True, name=f"d_proj_in{l}", bm=bm_in, out_dtype=BF16,
                carry=([_to_slots(grads["w_in"][0], 0)], ["a2a"]))
        below = None if l == 0 else (saved[l - 1]["dd"], mods[l - 1], 5)
        outs = _norm_bwd(s["x0"], vec(g_mix[l]), mods[l], dh, dx1, n_lat, 0, 1, f"d_norm_mix{l}", res=below)
        if below is None:
            dxs, acc, accs = outs
        else:
            dxs, d_dd, acc, accs = outs
        grads["g_mix"][l] = acc[0]
        dmods[l] = jnp.stack([accs[:, 0], accs[:, 1], gate1, shift2, scale2, gate2], axis=1)
        gate2 = accs[:, 2]

    dmod_own = jnp.stack([dmods[l][0].reshape(-1) for l in range(DEPTH)])
    dmod_ctx = jnp.stack([dmods[l][1].reshape(-1) for l in range(DEPTH)])
    late = [grads["g_mix"][0], dmod_own + dmod_ctx, dmod_ctx, dmod_own]
    late_shapes = [a.shape for a in late]
    (late_all,) = _exchange([_pack(late)], ["gather"], "gather_late_grads")
    g_mix0_g, b_mod_g, dmodc_tot, _ = _unpack(_sum_slots(late_all, "sum_late_grads"), late_shapes)
    dmod_all = _unpack(late_all, late_shapes, lead=(N_DEV,))[-1]
    layer_sums = [_unpack(_sum_slots(small_recv[l], f"sum_small_grads{l}"), small_shapes_of[l]) for l in range(DEPTH)]
    by_name = {k: jnp.stack([layer_sums[l][t] for l in range(DEPTH)]) for t, k in enumerate(small_names)}
    g_mix_g = jnp.stack([g_mix0_g] + [layer_sums[l][len(small_names)] for l in range(1, DEPTH)])
    loss_sum, g_final_g = layer_sums[DEPTH - 1][-2], layer_sums[DEPTH - 1][-1]
    g_ffn_g, g_v_g, b_gate_g, sink_g = by_name["g_ffn"], by_name["g_v"], by_name["b_gate"], by_name["sink"]
    w_spatial_g, b_spatial_g = by_name["w_spatial"], by_name["b_spatial"]
    w_sconv_g = lax.dynamic_slice_in_dim(by_name["w_sconv"], me * w_sconv.shape[2], w_sconv.shape[2], axis=2)
    w_fconv_g = lax.dynamic_slice_in_dim(by_name["w_fconv"], me * w_fconv.shape[2], w_fconv.shape[2], axis=2)

    dmod_cols = lax.dynamic_slice_in_dim(dmod_all, me * mod_w, mod_w, axis=2)
    dmodc_cols = lax.dynamic_slice_in_dim(dmodc_tot, me * mod_w, mod_w, axis=1)
    g_w_mod, cctx_part = [], None
    for l in range(DEPTH):
        rhs = jnp.concatenate([dmod_cols[:, l], dmodc_cols[l][None], jnp.zeros((7, mod_w), F32)], axis=0)
        g_w_mod.append(_mm(act, rhs, ta=True, name=f"g_mod{l}", bm=D_MODEL, bn=mod_w, bk=16, out_dtype=F32))
        lhs = jnp.pad(dmodc_cols[l][None], ((0, 7), (0, 0)))
        part = _mm(lhs, w_mod[l], tb=True, name=f"d_cctx{l}", bm=8, bn=D_MODEL, bk=mod_w, out_dtype=F32)
        cctx_part = part if cctx_part is None else cctx_part + part
    sg = jax.nn.sigmoid(c_ctx)
    dsilu = (sg * (1.0 + c_ctx * (1.0 - sg))).reshape(8, LANES)

    (r_cctx,) = _exchange([cctx_part[0].reshape(8, LANES)], ["gather"], "gather_c_ctx_grad")

    def per_layer(a):
        return a.reshape(a.shape[0], -1, a.shape[-1])

    upd = {}
    for nm, kind, wv, mv, vv in (("w_in", "in", w_in, m_w_in, v_w_in), ("w_branch", "br", w_branch, m_w_branch, v_w_branch),
                                 ("w_out", "out", w_out, m_w_out, v_w_out), ("w_up", "up", w_up, m_w_up, v_w_up),
                                 ("w_down", "dn", w_down, m_w_down, v_w_down), ("w_mod", None, w_mod, m_w_mod, v_w_mod)):
        if kind is None:
            gsrcs = [g[None] for g in g_w_mod]
        else:
            gsrcs = [per_layer(recv[kind, l]) for l in range(DEPTH)]
        if kind in ("in", "up"):
            outs = _adamw(tr(wv), gsrcs, tr(mv), tr(vv), f"adamw_{nm}")
            upd[nm] = [tr(o) for o in outs]
        else:
            outs = _adamw(per_layer(wv), gsrcs, per_layer(mv), per_layer(vv), f"adamw_{nm}")
            upd[nm] = [o.reshape(wv.shape) for o in outs]
    as_tile = lambda a: a.reshape(1, 8, LANES)
    upd["c_ctx"] = [o.reshape(D_MODEL) for o in _adamw(
        as_tile(c_ctx), [r_cctx], as_tile(m_c_ctx), as_tile(v_c_ctx), "adamw_c_ctx", scale=as_tile(dsilu))]

    names = ["b_mod", "g_mix", "b_gate", "sink", "w_spatial", "b_spatial", "g_v", "w_sconv", "g_ffn", "w_fconv",
             "g_final"]
    w_s = [b_mod, g_mix, b_gate, sink, w_spatial, b_spatial, g_v, w_sconv, g_ffn, w_fconv, g_final]
    g_s = [b_mod_g, g_mix_g, b_gate_g, sink_g, w_spatial_g, b_spatial_g, g_v_g, w_sconv_g, g_ffn_g, w_fconv_g,
           g_final_g]
    m_s = [m_b_mod, m_g_mix, m_b_gate, m_sink, m_w_spatial, m_b_spatial, m_g_v, m_w_sconv, m_g_ffn, m_w_fconv,
           m_g_final]
    v_s = [v_b_mod, v_g_mix, v_b_gate, v_sink, v_w_spatial, v_b_spatial, v_g_v, v_w_sconv, v_g_ffn, v_w_fconv,
           v_g_final]
    shapes = [a.shape for a in w_s]
    packed = _adamw(_pack(w_s)[None], [_pack(g_s)[None]], _pack(m_s)[None], _pack(v_s)[None], "adamw_small")
    unpacked = [_unpack(o[0], shapes) for o in packed]
    for t, nm in enumerate(names):
        upd[nm] = [unpacked[q][t] for q in range(4)]

    order = ["c_ctx", "w_mod", "b_mod", "g_mix", "w_in", "b_gate", "sink", "w_spatial", "b_spatial", "g_v", "w_sconv",
             "w_branch", "w_out", "g_ffn", "w_up", "w_fconv", "w_down", "g_final"]
    result = [loss_sum.reshape(()), dxs[None]]
    for q in range(4):
        result += [upd[nm][q] for nm in order]
    return tuple(result)
```

```python
import jax
import jax.numpy as jnp
from jax import lax
from jax.experimental import pallas as pl
from jax.experimental.pallas import tpu as pltpu

F32, BF16 = jnp.float32, jnp.bfloat16

D_MODEL = 1024
DEPTH = 2
GRID_W = 64
N_HEADS = 16
N_KV_HEADS = 4
HEAD_DIM = 64
WINDOW = 128
BLK = 128
ROPE_THETA = 10000.0
CHUNK = 128
A_GROUPS = 8
D_FF = 2816
EPS = 1e-6
NEG = -1e30
IN_W = 9728
OFF_K, OFF_A, OFF_B, OFF_G = 1024, 1536, 3584, 6656
P_Q, P_A, P_B, P_KV = 3072, 4096, 6144, 9216

N_DEV = 8
LANES = 128
ROW_TILE = 256
CONV_CHUNK = 256
PAD = 8
VMEM_LIMIT = 52 * 1024 * 1024

ADAM_LR, ADAM_B1, ADAM_B2, ADAM_EPS, ADAM_WD, ADAM_STEP = 0.001, 0.9, 0.999, 1e-08, 0.01, 10

HBM_SPEC = pl.BlockSpec(memory_space=pltpu.HBM)
SMEM_SPEC = pl.BlockSpec(memory_space=pltpu.SMEM)


def _params():
    return pltpu.CompilerParams(vmem_limit_bytes=VMEM_LIMIT)


def _pick(n, prefs):
    for p in prefs:
        if n % p == 0:
            return p
    raise ValueError((n, prefs))


def _sigmoid(x):
    return 0.5 * jnp.tanh(0.5 * x) + 0.5


def _mm(a, b, *, name, ta=False, tb=False, bm, bn, bk, out_dtype, a_lead=None, b_lead=None, a_spec=None,
        b_spec=None, cols_outer=False, carry=None):
    ash = a.shape[1:] if a_lead is not None else a.shape
    bsh = b.shape[1:] if b_lead is not None else b.shape
    kc = (bsh[1] if tb else bsh[0]) if b_spec is None else (ash[0] if ta else ash[1])
    mo =(ash[1] if ta else ash[0]) if a_spec is None else a_spec[2]
    no = (bsh[0] if tb else bsh[1]) if b_spec is None else b_spec[2]
    assert mo % bm == 0 and no % bn == 0 and kc % bk == 0, (name, mo, no, kc, bm, bn, bk)
    nk = kc // bk

    def spec(shape, fn, idx=None):
        if idx is not None:
            shape, inner = (None,) + shape, fn
            fn = lambda i, j, k: (idx,) + inner(i, j, k)
        if cols_outer:
            return pl.BlockSpec(shape, lambda j, i, k: fn(i, j, k))
        return pl.BlockSpec(shape, fn)

    if a_spec is not None:
        a_bs = spec(a_spec[0], a_spec[1])
    elif ta:
        a_bs = spec((bk, bm), lambda i, j, k: (k, i), a_lead)
    else:
        a_bs = spec((bm, bk), lambda i, j, k: (i, k), a_lead)
    if b_spec is not None:
        b_bs = spec(b_spec[0], b_spec[1])
    elif tb:
        b_bs = spec((bn, bk), lambda i, j, k: (j, k), b_lead)
    else:
        b_bs = spec((bk, bn), lambda i, j, k: (k, j), b_lead)
    dims = (((0 if ta else 1,), (1 if tb else 0,)), ((), ()))
    grid = (no // bn, mo // bm, nk) if cols_outer else (mo // bm, no // bn, nk)

    def body(a_ref, b_ref, o_ref, *scratch):
        if nk == 1:
            o_ref[...] = lax.dot_general(a_ref[...], b_ref[...], dims, preferred_element_type=F32).astype(o_ref.dtype)
        else:
            acc = scratch[0]
            k = pl.program_id(2)

            @pl.when(k == 0)
            def _():
                acc[...] = jnp.zeros_like(acc)

            acc[...] += lax.dot_general(a_ref[...], b_ref[...], dims, preferred_element_type=F32)

            @pl.when(k == nk - 1)
            def _():
                o_ref[...] = acc[...].astype(o_ref.dtype)

    outs, carried = _pcall(
        body, (a, b), name=name, grid=grid, in_specs=[a_bs, b_bs],
        out_specs=[spec((bm, bn), lambda i, j, k: (i, j))],
        out_shape=[jax.ShapeDtypeStruct((mo, no), out_dtype)],
        scratch_shapes=[pltpu.VMEM((bm, bn), F32)] if nk > 1 else [], carry=carry)
    return outs[0] if carry is None else (outs[0], carried)


def _mm_pieces(pieces, w, *, name, bm, w_t=False, out_dtype=F32, carry=None):
    kout = w.shape[1] if w_t else w.shape[0]
    starts, total = [], 0
    for piece in pieces:
        starts.append(total)
        total += piece[3]
    mo = pieces[0][0].shape[-2]
    assert mo % bm == 0
    widths = sorted({piece[5] for piece in pieces}, reverse=True)

    def inside(p, k):
        return (k >= starts[p]) & (k < starts[p] + pieces[p][3])

    def a_spec(p):
        _, lead, col0, nblk, _, bk = pieces[p]

        def fn(i, k):
            t = jnp.clip(k - starts[p], 0, nblk - 1)
            if lead is None:
                return (i, col0 + t)
            return (t, i, col0) if lead == "step" else (lead, i, col0 + t)
        return pl.BlockSpec((bm, bk) if lead is None else (None, bm, bk), fn)

    def w_spec(bk):
        def fn(i, k):
            col = 0
            for p, piece in enumerate(pieces):
                if piece[5] == bk:
                    col = col + jnp.where(inside(p, k), piece[4] + k - starts[p], 0)
            return (col, 0) if w_t else (0, col)
        return pl.BlockSpec((bk, kout) if w_t else (kout, bk), fn)

    n_p, n_w = len(pieces), len(widths)

    def body(*refs):
        a_refs, w_refs, o_ref, acc = refs[:n_p], refs[n_p:n_p + n_w], refs[n_p + n_w], refs[n_p + n_w + 1]
        k = pl.program_id(1)

        @pl.when(k == 0)
        def _():
            acc[...] = jnp.zeros_like(acc)

        for p in range(n_p):
            w_ref = w_refs[widths.index(pieces[p][5])]

            @pl.when(inside(p, k))
            def _(p=p, w_ref=w_ref):
                acc[...] += lax.dot_general(a_refs[p][...], w_ref[...], NN if w_t else NT,
                                            preferred_element_type=F32)

        @pl.when(k == total - 1)
        def _():
            o_ref[...] = acc[...].astype(o_ref.dtype)

    outs, carried = _pcall(
        body, [piece[0] for piece in pieces] + [w] * n_w, name=name, grid=(mo // bm, total),
        in_specs=[a_spec(p) for p in range(n_p)] + [w_spec(bk) for bk in widths],
        out_specs=[pl.BlockSpec((bm, kout), lambda i, k: (i, 0))],
        out_shape=[jax.ShapeDtypeStruct((mo, kout), out_dtype)],
        scratch_shapes=[pltpu.VMEM((bm, kout), F32)], carry=carry)
    return outs[0] if carry is None else (outs[0], carried)


def _xchg_out_shapes(arrs, modes):
    return [jax.ShapeDtypeStruct((N_DEV,) + a.shape if m == "gather" else a.shape, a.dtype)
            for a, m in zip(arrs, modes)]


def _xchg_sems(n):
    return [pltpu.SemaphoreType.DMA((n, N_DEV - 1)), pltpu.SemaphoreType.DMA((n, N_DEV - 1)),
            pltpu.SemaphoreType.DMA((n,))]


def _xchg_copies(ins, outs, modes, sems):
    send_sems, recv_sems, local_sems = sems
    x, y, c = lax.axis_index("x"), lax.axis_index("y"), lax.axis_index("c")
    me = 4 * x + 2 * y + c

    def place(q):
        px = 1 - x if (q >> 2) & 1 else x
        py = 1 - y if (q >> 1) & 1 else y
        pc = 1 - c if q & 1 else c
        return (px, py, pc), 4 * px + 2 * py + pc

    sibling, _ = place(1)
    out = dict(local=[], direct=[], landed=[], passed=[], others=[])
    for t, mode in enumerate(modes):
        gather = mode == "gather"
        mine = ins[t] if gather else ins[t].at[me]
        out["local"].append(pltpu.make_async_copy(mine, outs[t].at[me], local_sems.at[t]))
        for q in range(1, N_DEV):
            peer, slot = place(q)
            sem = dict(send_sem=send_sems.at[t, q - 1], recv_sem=recv_sems.at[t, q - 1],
                       device_id_type=pl.DeviceIdType.MESH)
            arrival = pltpu.make_async_remote_copy(src_ref=outs[t].at[slot], dst_ref=outs[t].at[slot],
                                                   device_id=peer, **sem)
            if not gather:
                out["direct"].append(pltpu.make_async_remote_copy(src_ref=ins[t].at[slot], dst_ref=outs[t].at[me],
                                                                  device_id=peer, **sem))
                out["others"].append(arrival)
            elif q == 1 or q % 2 == 0:
                out["direct"].append(pltpu.make_async_remote_copy(src_ref=ins[t], dst_ref=outs[t].at[me],
                                                                  device_id=peer, **sem))
                out["others" if q == 1 else "landed"].append(arrival)
            else:
                _, origin = place(q - 1)
                out["passed"].append(pltpu.make_async_remote_copy(src_ref=outs[t].at[origin], dst_ref=outs[t].at[origin],
                                                                  device_id=sibling, **sem))
                out["others"].append(arrival)
    return out


def _xchg_start(copies):
    for cp in copies["local"] + copies["direct"]:
        cp.start()


def _xchg_wait(copies):
    for cp in copies["landed"]:
        cp.wait_recv()
    for cp in copies["passed"]:
        cp.start()
    for cp in copies["others"]:
        cp.wait_recv()
    for cp in copies["direct"] + copies["passed"]:
        cp.wait_send()
    for cp in copies["local"]:
        cp.wait()


def _exchange(arrs, modes, name):
    n = len(arrs)

    def body(*refs):
        copies = _xchg_copies(refs[:n], refs[n:2 * n], modes, refs[2 * n:])
        _xchg_start(copies)
        _xchg_wait(copies)

    outs = pl.pallas_call(
        body, name=name, in_specs=[HBM_SPEC] * n, out_specs=[HBM_SPEC] * n, out_shape=_xchg_out_shapes(arrs, modes),
        scratch_shapes=_xchg_sems(n), compiler_params=pltpu.CompilerParams(has_side_effects=True),
    )(*arrs)
    return list(outs)


def _pcall(body, operands, *, name, grid, in_specs, out_specs, out_shape, scratch_shapes=(), carry=None):
    out_specs, out_shape, scratch_shapes = list(out_specs), list(out_shape), list(scratch_shapes)
    if carry is None:
        outs = pl.pallas_call(body, name=name, grid=grid, in_specs=in_specs, out_specs=out_specs,
                              out_shape=out_shape, scratch_shapes=scratch_shapes, compiler_params=_params())(*operands)
        return list(outs), []
    arrs, modes = carry
    n, n_in, n_out, n_scr = len(arrs), len(in_specs), len(out_specs), len(scratch_shapes)

    def wrapped(*refs):
        ins, c_in = refs[:n_in], refs[n_in:n_in + n]
        outs, c_out = refs[n_in + n:n_in + n + n_out], refs[n_in + n + n_out:n_in + 2 * n + n_out]
        rest = refs[n_in + 2 * n + n_out:]
        scr, sems = rest[:n_scr], rest[n_scr:]
        first, last = None, None
        for d, size in enumerate(grid):
            f, e = pl.program_id(d) == 0, pl.program_id(d) == size - 1
            first = f if first is None else first & f
            last = e if last is None else last & e

        @pl.when(first)
        def _():
            _xchg_start(_xchg_copies(c_in, c_out, modes, sems))

        body(*ins, *outs, *scr)

        @pl.when(last)
        def _():
            _xchg_wait(_xchg_copies(c_in, c_out, modes, sems))

    outs = pl.pallas_call(
        wrapped, name=name, grid=grid, in_specs=list(in_specs) + [HBM_SPEC] * n,
        out_specs=out_specs + [HBM_SPEC] * n, out_shape=out_shape + _xchg_out_shapes(arrs, modes),
        scratch_shapes=scratch_shapes + _xchg_sems(n), compiler_params=_params())(*operands, *arrs)
    return list(outs[:n_out]), list(outs[n_out:])


def _row_specs(r, n_lat, tm):
    nbl = n_lat // tm
    row = pl.BlockSpec((tm, D_MODEL), lambda i: (i, 0))
    mod = pl.BlockSpec((None, 8, D_MODEL), lambda i: (i // nbl, 0, 0))
    vec = pl.BlockSpec((1, D_MODEL), lambda i: (0, 0))
    return nbl, row, mod, vec


def _rows(vals):
    width = [v for v in vals if v is not None][0].shape[1]
    return jnp.concatenate([jnp.zeros((1, width), F32) if v is None else v for v in vals], axis=0)


def _norm_fwd(xs, g, mods, n_lat, sh, sc, name, res=None, carry=None):
    r = xs.shape[0] if res is not None else xs[0].shape[0] + xs[1].shape[0]
    tm = ROW_TILE
    nbl, row, mod, vec = _row_specs(r, n_lat, tm)

    def norm(x, g_ref, m_ref, h_ref):
        rs = lax.rsqrt(jnp.mean(x * x, axis=-1, keepdims=True) + EPS)
        m = m_ref[...]
        h_ref[...] = ((x * rs * g_ref[...]) * (1.0 + m[sc:sc + 1]) + m[sh:sh + 1]).astype(BF16)

    if res is None:
        lat, ctx_rows = xs
        lat_spec = pl.BlockSpec((tm, D_MODEL), lambda i: (jnp.minimum(i, nbl - 1), 0))
        ctx_spec = pl.BlockSpec((tm, D_MODEL), lambda i: (jnp.maximum(i - nbl, 0), 0))

        def body(l_ref, c_ref, g_ref, m_ref, x_ref, h_ref):
            x = jnp.where(pl.program_id(0) < nbl, l_ref[...], c_ref[...])
            x_ref[...] = x
            norm(x, g_ref, m_ref, h_ref)
        outs, carried = _pcall(
            body, (lat, ctx_rows, g, mods), name=name, grid=(r // tm,), in_specs=[lat_spec, ctx_spec, vec, mod],
            out_specs=[row, row],
            out_shape=[jax.ShapeDtypeStruct((r, D_MODEL), F32), jax.ShapeDtypeStruct((r, D_MODEL), BF16)], carry=carry)
        return outs if carry is None else (outs, carried)

    o, mods_res, gt = res

    def body(x_ref, o_ref, mr_ref, g_ref, m_ref, x1_ref, h_ref):
        x = x_ref[...] + mr_ref[...][gt:gt + 1] * o_ref[...]
        x1_ref[...] = x
        norm(x, g_ref, m_ref, h_ref)

    return pl.pallas_call(
        body, name=name, grid=(r // tm,), in_specs=[row, row, mod, vec, mod], out_specs=[row, row],
        out_shape=[jax.ShapeDtypeStruct((r, D_MODEL), F32), jax.ShapeDtypeStruct((r, D_MODEL), BF16)],
        compiler_params=_params())(xs, o, mods_res, g, mods)


def _rms_bwd(x, g, dy):
    rs = lax.rsqrt(jnp.mean(x * x, axis=-1, keepdims=True) + EPS)
    xh = x * rs
    dxh = dy * g
    dx = rs * (dxh - xh * jnp.mean(dxh * xh, axis=-1, keepdims=True))
    return dx, dy * xh, xh


def _acc_specs(nbl):
    acc_all = pl.BlockSpec((8, D_MODEL), lambda i: (0, 0))
    acc_stream = pl.BlockSpec((None, 8, D_MODEL), lambda i: (i // nbl, 0, 0))
    return acc_all, acc_stream


def _loss_bwd(xs, o, mods, g_final, target, n_lat, name):
    r = xs.shape[0]
    tm = ROW_TILE
    nbl, row, mod, vec = _row_specs(r, n_lat, tm)
    acc_all, acc_stream = _acc_specs(nbl)
    tgt = pl.BlockSpec((tm, D_MODEL), lambda i: (jnp.minimum(i, nbl - 1), 0))

    def body(x_ref, o_ref, m_ref, g_ref, t_ref, dx_ref, do_ref, acc_ref, accs_ref):
        i = pl.program_id(0)
        lat = i < nbl
        gate = m_ref[...][5:6]
        o_val = o_ref[...]
        x = x_ref[...] + gate * o_val
        g = g_ref[...]
        rs = lax.rsqrt(jnp.mean(x * x, axis=-1, keepdims=True) + EPS)
        y = x * rs * g
        err = jnp.where(lat, y - t_ref[...], 0.0)
        loss = 0.5 * jnp.sum(jnp.mean(err * err, axis=-1, keepdims=True), axis=0, keepdims=True)
        dy = err * (1.0 / D_MODEL)
        dx, dg_rows, _ = _rms_bwd(x, g, dy)
        dx_ref[...] = dx
        do_ref[...] = (gate * dx).astype(BF16)

        @pl.when(i == 0)
        def _():
            acc_ref[...] = jnp.zeros_like(acc_ref)

        @pl.when((i == 0) | (i == nbl))
        def _():
            accs_ref[...] = jnp.zeros_like(accs_ref)

        acc_ref[...] += _rows([jnp.sum(dg_rows, axis=0, keepdims=True), jnp.broadcast_to(loss, (1, D_MODEL))]
                              + [None] * 6)
        accs_ref[...] += _rows([None, None, jnp.sum(dx * o_val, axis=0, keepdims=True)] + [None] * 5)

    return pl.pallas_call(
        body, name=name, grid=(r // tm,), in_specs=[row, row, mod, vec, tgt],
        out_specs=[row, row, acc_all, acc_stream],
        out_shape=[jax.ShapeDtypeStruct((r, D_MODEL), F32), jax.ShapeDtypeStruct((r, D_MODEL), BF16),
                   jax.ShapeDtypeStruct((8, D_MODEL), F32), jax.ShapeDtypeStruct((2, 8, D_MODEL), F32)],
        compiler_params=_params())(xs, o, mods, g_final, target)


def _norm_bwd(xs, g, mods, dh, dx_in, n_lat, sh, sc, name, res=None):
    r = xs.shape[0]
    tm = ROW_TILE
    nbl, row, mod, vec = _row_specs(r, n_lat, tm)
    acc_all, acc_stream = _acc_specs(nbl)
    has_res = res is not None

    def body(*refs):
        if has_res:
            x_ref, g_ref, m_ref, dh_ref, dxi_ref, o_ref, mr_ref, dx_ref, do_ref, acc_ref, accs_ref = refs
        else:
            x_ref, g_ref, m_ref, dh_ref, dxi_ref, dx_ref, acc_ref, accs_ref = refs
        i = pl.program_id(0)
        x, g, m, dhv = x_ref[...], g_ref[...], m_ref[...], dh_ref[...].astype(F32)
        dy = dhv * (1.0 + m[sc:sc + 1])
        dxn, dg_rows, xh = _rms_bwd(x, g, dy)
        dx = dxi_ref[...] + dxn
        if has_res:
            dx_ref[...] = dx
        else:
            @pl.when(i < nbl)
            def _():
                dx_ref[...] = dx
        d_gate = None
        if has_res:
            o_val = o_ref[...]
            do_ref[...] = (mr_ref[...][res[2]:res[2] + 1] * dx).astype(BF16)
            d_gate = jnp.sum(dx * o_val, axis=0, keepdims=True)

        @pl.when(i == 0)
        def _():
            acc_ref[...] = jnp.zeros_like(acc_ref)

        @pl.when((i == 0) | (i == nbl))
        def _():
            accs_ref[...] = jnp.zeros_like(accs_ref)

        acc_ref[...] += _rows([jnp.sum(dg_rows, axis=0, keepdims=True)] + [None] * 7)
        accs_ref[...] += _rows([jnp.sum(dhv, axis=0, keepdims=True),
                                jnp.sum(dhv * (xh * g), axis=0, keepdims=True), d_gate] + [None] * 5)

    ins = [xs, g, mods, dh, dx_in]
    in_specs = [row, vec, mod, row, row]
    if has_res:
        out_specs, out_shape = [row], [jax.ShapeDtypeStruct((r, D_MODEL), F32)]
    else:
        out_specs = [pl.BlockSpec((tm, D_MODEL), lambda i: (jnp.minimum(i, nbl - 1), 0))]
        out_shape = [jax.ShapeDtypeStruct((n_lat, D_MODEL), F32)]
    if has_res:
        ins += [res[0], res[1]]
        in_specs += [row, mod]
        out_specs.append(row)
        out_shape.append(jax.ShapeDtypeStruct((r, D_MODEL), BF16))
    out_specs += [acc_all, acc_stream]
    out_shape += [jax.ShapeDtypeStruct((8, D_MODEL), F32), jax.ShapeDtypeStruct((2, 8, D_MODEL), F32)]
    return pl.pallas_call(body, name=name, grid=(r // tm,), in_specs=in_specs, out_specs=out_specs,
                          out_shape=out_shape, compiler_params=_params())(*ins)


def _rotate(t, cos, sin):
    width = t.shape[1]
    reps = width // LANES
    lane = lax.broadcasted_iota(jnp.int32, (1, width), 1)
    first = (lane % HEAD_DIM) < (HEAD_DIM // 2)
    swapped = jnp.where(first, pltpu.roll(t, width - HEAD_DIM // 2, 1), pltpu.roll(t, HEAD_DIM // 2, 1))
    return t * jnp.tile(cos, (1, reps)) + swapped * jnp.tile(sin, (1, reps))


def _rope_fwd(z, cos, sin, name):
    r = z.shape[0]
    tm = ROW_TILE
    kvw = 2 * N_KV_HEADS * HEAD_DIM
    tab = pl.BlockSpec((tm, LANES), lambda i: (i, 0))

    def body(q_ref, kv_ref, c_ref, s_ref, o_ref):
        c, s = c_ref[...], s_ref[...]
        kv = kv_ref[...]
        o_ref[:, :D_MODEL] = (_rotate(q_ref[...].astype(F32), c, s) * (HEAD_DIM ** -0.5)).astype(BF16)
        o_ref[:, D_MODEL:D_MODEL + kvw // 2] = _rotate(kv[:, :kvw // 2].astype(F32), c, s).astype(BF16)
        o_ref[:, D_MODEL + kvw // 2:] = kv[:, kvw // 2:]

    return pl.pallas_call(
        body, name=name, grid=(r // tm,),
        in_specs=[pl.BlockSpec((tm, D_MODEL), lambda i: (i, P_Q // D_MODEL)),
                  pl.BlockSpec((tm, kvw), lambda i: (i, P_KV // kvw)), tab, tab],
        out_specs=pl.BlockSpec((tm, D_MODEL + kvw), lambda i: (i, 0)),
        out_shape=jax.ShapeDtypeStruct((r, D_MODEL + kvw), BF16), compiler_params=_params())(z, z, cos, sin)


def _rope_bwd(dq, dk, dv, cos, sin, name):
    r = dq.shape[0]
    tm = ROW_TILE
    kw = N_KV_HEADS * HEAD_DIM
    tab = pl.BlockSpec((tm, LANES), lambda i: (i, 0))

    def body(dq_ref, dk_ref, dv_ref, c_ref, s_ref, o_ref):
        c, s = c_ref[...], -s_ref[...]
        o_ref[:, :D_MODEL] = (_rotate(dq_ref[...].astype(F32), c, s) * (HEAD_DIM ** -0.5)).astype(BF16)
        o_ref[:, D_MODEL:D_MODEL + kw] = _rotate(dk_ref[...], c, s).astype(BF16)
        o_ref[:, D_MODEL + kw:] = dv_ref[...].astype(BF16)

    return pl.pallas_call(
        body, name=name, grid=(r // tm,),
        in_specs=[pl.BlockSpec((tm, D_MODEL), lambda i: (i, 0)), pl.BlockSpec((tm, kw), lambda i: (i, 0)),
                  pl.BlockSpec((tm, kw), lambda i: (i, 0)), tab, tab],
        out_specs=pl.BlockSpec((tm, D_MODEL + 2 * kw), lambda i: (i, 0)),
        out_shape=jax.ShapeDtypeStruct((r, D_MODEL + 2 * kw), BF16), compiler_params=_params())(dq, dk, dv, cos, sin)


def _attn_setup(i, n_lat, m_ctx, nbl, k_ref, v_ref):
    start = pl.multiple_of(jnp.clip((i - 1) * BLK, 0, n_lat - 3 * BLK), BLK)
    nkeys = 3 * BLK + m_ctx
    rows = lax.broadcasted_iota(jnp.int32, (4 * BLK, nkeys), 0)
    cols = lax.broadcasted_iota(jnp.int32, (4 * BLK, nkeys), 1)
    qpos = i * BLK + (rows & (BLK - 1))
    seen = (cols >= 3 * BLK) | ((jnp.abs(start + cols - qpos) <= WINDOW) & (i < nbl))
    mask = jnp.where(seen, 0.0, NEG)
    kblk = jnp.concatenate([k_ref[pl.ds(start, 3 * BLK), :], k_ref[pl.ds(n_lat, m_ctx), :]], axis=0)
    vblk = jnp.concatenate([v_ref[pl.ds(start, 3 * BLK), :], v_ref[pl.ds(n_lat, m_ctx), :]], axis=0)
    lo = lax.broadcasted_iota(jnp.int32, (1, LANES), 1) < HEAD_DIM
    return start, mask, kblk, vblk, lo


def _stack_heads(ref, kh, lo):
    a = ref[:, (2 * kh) * LANES:(2 * kh + 1) * LANES]
    b = ref[:, (2 * kh + 1) * LANES:(2 * kh + 2) * LANES]
    z = jnp.zeros_like(a)
    return jnp.concatenate([jnp.where(lo, a, z), jnp.where(lo, z, a), jnp.where(lo, b, z), jnp.where(lo, z, b)],
                           axis=0)


def _kv_variants(blk, rolled, kh, lo):
    z = jnp.zeros_like(blk)
    if kh == 0:
        return jnp.where(lo, blk, rolled), jnp.where(lo, blk, z), jnp.where(lo, z, rolled)
    return jnp.where(lo, rolled, blk), jnp.where(lo, rolled, z), jnp.where(lo, z, blk)


NN = (((1,), (0,)), ((), ()))
NT = (((1,), (1,)), ((), ()))
TN = (((0,), (0,)), ((), ()))


def _scores(qst, k2, mask, sink_ref, p, kh):
    s = lax.dot_general(qst, k2, NT, preferred_element_type=F32) + mask
    snk = jnp.concatenate([jnp.full((BLK, 1), sink_ref[p * 8 + kh * 4 + g], F32) for g in range(4)], axis=0)
    return s, snk


def _attn_fwd(qkv, sink, n_lat, m_ctx, name, carry=None):
    r = qkv.shape[0]
    nb, nbl = r // BLK, n_lat // BLK
    kcol = D_MODEL // LANES

    def body(sink_ref, q_ref, k_ref, v_ref, o_ref, lse_ref):
        p, i = pl.program_id(0), pl.program_id(1)
        _, mask, kblk, vblk, lo = _attn_setup(i, n_lat, m_ctx, nbl, k_ref, v_ref)
        kr, vr = pltpu.roll(kblk, HEAD_DIM, 1), pltpu.roll(vblk, HEAD_DIM, 1)
        for kh in range(2):
            k2, _, _ = _kv_variants(kblk, kr, kh, lo)
            _, vlo, vhi = _kv_variants(vblk, vr, kh, lo)
            qst = _stack_heads(q_ref, kh, lo)
            s, snk = _scores(qst, k2, mask, sink_ref, p, kh)
            mx = jnp.maximum(jnp.max(s, axis=-1, keepdims=True), snk)
            pe = jnp.exp(s - mx)
            den = jnp.sum(pe, axis=-1, keepdims=True) + jnp.exp(snk - mx)
            inv = 1.0 / den
            pb = pe.astype(BF16)
            for jp in range(2):
                r0 = 2 * jp * BLK
                pair = (jnp.dot(pb[r0:r0 + BLK], vlo, preferred_element_type=F32)
                        + jnp.dot(pb[r0 + BLK:r0 + 2 * BLK], vhi, preferred_element_type=F32))
                pair = pair * jnp.where(lo, inv[r0:r0 + BLK], inv[r0 + BLK:r0 + 2 * BLK])
                o_ref[:, (2 * kh + jp) * LANES:(2 * kh + jp + 1) * LANES] = pair.astype(BF16)
            lse = mx + jnp.log(den)
            for g in range(4):
                lse_ref[:, kh * 4 + g:kh * 4 + g + 1] = lse[g * BLK:(g + 1) * BLK]

    outs, carried = _pcall(
        body, (sink, qkv, qkv, qkv), name=name, grid=(2, nb),
        in_specs=[SMEM_SPEC,
                  pl.BlockSpec((BLK, 4 * LANES), lambda p, i: (i, p)),
                  pl.BlockSpec((r, LANES), lambda p, i: (0, kcol + p)),
                  pl.BlockSpec((r, LANES), lambda p, i: (0, kcol + 2 + p))],
        out_specs=[pl.BlockSpec((BLK, 4 * LANES), lambda p, i: (i, p)),
                   pl.BlockSpec((None, BLK, 8), lambda p, i: (p, i, 0))],
        out_shape=[jax.ShapeDtypeStruct((r, D_MODEL), BF16), jax.ShapeDtypeStruct((2, r, 8), F32)], carry=carry)
    return outs if carry is None else (outs, carried)


def _attn_bwd(qkv, sink, o, do, lse, n_lat, m_ctx, name, carry=None):
    r = qkv.shape[0]
    nb, nbl = r // BLK, n_lat // BLK
    kcol = D_MODEL // LANES

    def body(sink_ref, q_ref, k_ref, v_ref, o_ref, do_ref, lse_ref, dq_ref, dk_ref, dv_ref, ds_ref):
        p, i = pl.program_id(0), pl.program_id(1)

        @pl.when(i == 0)
        def _():
            dk_ref[...] = jnp.zeros_like(dk_ref)
            dv_ref[...] = jnp.zeros_like(dv_ref)
            ds_ref[...] = jnp.zeros_like(ds_ref)

        start, mask, kblk, vblk, lo = _attn_setup(i, n_lat, m_ctx, nbl, k_ref, v_ref)
        kr, vr = pltpu.roll(kblk, HEAD_DIM, 1), pltpu.roll(vblk, HEAD_DIM, 1)
        lane = lax.broadcasted_iota(jnp.int32, (1, LANES), 1)
        dks, dvs = [], []
        for kh in range(2):
            k2, klo, khi = _kv_variants(kblk, kr, kh, lo)
            v2, _, _ = _kv_variants(vblk, vr, kh, lo)
            qst = _stack_heads(q_ref, kh, lo)
            dost = _stack_heads(do_ref, kh, lo)
            s, snk = _scores(qst, k2, mask, sink_ref, p, kh)
            lse4 = jnp.concatenate([lse_ref[:, kh * 4 + g:kh * 4 + g + 1] for g in range(4)], axis=0)
            pe = jnp.exp(s - lse4)
            dp = lax.dot_general(dost, v2, NT, preferred_element_type=F32)
            deltas = []
            for jp in range(2):
                cols = slice((2 * kh + jp) * LANES, (2 * kh + jp + 1) * LANES)
                prod = do_ref[:, cols].astype(F32) * o_ref[:, cols].astype(F32)
                deltas.append(jnp.sum(jnp.where(lo, prod, 0.0), axis=-1, keepdims=True))
                deltas.append(jnp.sum(jnp.where(lo, 0.0, prod), axis=-1, keepdims=True))
            delta = jnp.concatenate(deltas, axis=0)
            dsc = pe * (dp - delta)
            dsb, pb = dsc.astype(BF16), pe.astype(BF16)
            for jp in range(2):
                r0 = 2 * jp * BLK
                dq_ref[:, (2 * kh + jp) * LANES:(2 * kh + jp + 1) * LANES] = (
                    jnp.dot(dsb[r0:r0 + BLK], klo, preferred_element_type=F32)
                    + jnp.dot(dsb[r0 + BLK:r0 + 2 * BLK], khi, preferred_element_type=F32)).astype(BF16)
            dkf = lax.dot_general(dsb, qst, TN, preferred_element_type=F32)
            dvf = lax.dot_general(pb, dost, TN, preferred_element_type=F32)
            dks.append(dkf + pltpu.roll(dkf, HEAD_DIM, 1))
            dvs.append(dvf + pltpu.roll(dvf, HEAD_DIM, 1))
            contrib = -jnp.exp(snk - lse4) * delta
            for g in range(4):
                tot = jnp.sum(contrib[g * BLK:(g + 1) * BLK], axis=0, keepdims=True)
                ds_ref[0:1, :] += jnp.where(lane == kh * 4 + g, tot, 0.0)
        dk_blk = jnp.where(lo, dks[0], dks[1])
        dv_blk = jnp.where(lo, dvs[0], dvs[1])
        dk_ref[pl.ds(start, 3 * BLK), :] += dk_blk[:3 * BLK]
        dk_ref[pl.ds(n_lat, m_ctx), :] += dk_blk[3 * BLK:]
        dv_ref[pl.ds(start, 3 * BLK), :] += dv_blk[:3 * BLK]
        dv_ref[pl.ds(n_lat, m_ctx), :] += dv_blk[3 * BLK:]

    qspec = pl.BlockSpec((BLK, 4 * LANES), lambda p, i: (i, p))
    outs, carried = _pcall(
        body, (sink, qkv, qkv, qkv, o, do, lse), name=name, grid=(2, nb),
        in_specs=[SMEM_SPEC, qspec,
                  pl.BlockSpec((r, LANES), lambda p, i: (0, kcol + p)),
                  pl.BlockSpec((r, LANES), lambda p, i: (0, kcol + 2 + p)),
                  qspec, qspec, pl.BlockSpec((None, BLK, 8), lambda p, i: (p, i, 0))],
        out_specs=[qspec, pl.BlockSpec((r, LANES), lambda p, i: (0, p)), pl.BlockSpec((r, LANES), lambda p, i: (0, p)),
                   pl.BlockSpec((None, 8, LANES), lambda p, i: (p, 0, 0))],
        out_shape=[jax.ShapeDtypeStruct((r, D_MODEL), BF16), jax.ShapeDtypeStruct((r, 2 * LANES), F32),
                   jax.ShapeDtypeStruct((r, 2 * LANES), F32), jax.ShapeDtypeStruct((2, 8, LANES), F32)], carry=carry)
    return outs if carry is None else (outs, carried)


def _gating_parts(z_ref, gv_ref):
    za = z_ref[...].astype(F32)
    zg = jax.nn.gelu(za)
    u, v = zg[:, :D_MODEL], zg[:, D_MODEL:]
    rs = lax.rsqrt(jnp.mean(v * v, axis=-1, keepdims=True) + EPS)
    return za, u, v, rs, v * rs * gv_ref[...]


def _mix(w_ref, vals):
    vb = vals.astype(BF16)
    return jnp.concatenate(
        [jnp.dot(w_ref[g], vb[:, g * LANES:(g + 1) * LANES], preferred_element_type=F32) for g in range(A_GROUPS)],
        axis=1)


def _gating_fwd(z, ws, bias, g_v, name):
    r = z.shape[0]

    def body(z_ref, w_ref, b_ref, gv_ref, y_ref):
        _, u, _, _, vn = _gating_parts(z_ref, gv_ref)
        y_ref[...] = (u * (_mix(w_ref, vn) + b_ref[...])).astype(BF16)

    return pl.pallas_call(
        body, name=name, grid=(r // CHUNK,),
        in_specs=[pl.BlockSpec((CHUNK, 2 * D_MODEL), lambda i: (i, P_A // (2 * D_MODEL))),
                  pl.BlockSpec((A_GROUPS, CHUNK, CHUNK), lambda i: (0, 0, 0)),
                  pl.BlockSpec((CHUNK, D_MODEL), lambda i: (0, 0)),
                  pl.BlockSpec((1, D_MODEL), lambda i: (0, 0))],
        out_specs=pl.BlockSpec((CHUNK, D_MODEL), lambda i: (i, 0)),
        out_shape=jax.ShapeDtypeStruct((r, D_MODEL), BF16), compiler_params=_params())(z, ws, bias, g_v)


def _gating_bwd(z, ws, ws_t, bias, g_v, dy, name, carry=None):
    r = z.shape[0]

    def body(z_ref, w_ref, wt_ref, b_ref, gv_ref, dy_ref, dz_ref, dw_ref, db_ref, dg_ref):
        i = pl.program_id(0)

        @pl.when(i == 0)
        def _():
            dw_ref[...] = jnp.zeros_like(dw_ref)
            db_ref[...] = jnp.zeros_like(db_ref)
            dg_ref[...] = jnp.zeros_like(dg_ref)

        za, u, v, rs, vn = _gating_parts(z_ref, gv_ref)
        dyv = dy_ref[...].astype(F32)
        du = dyv * (_mix(w_ref, vn) + b_ref[...])
        dmixed = dyv * u
        dvn = _mix(wt_ref, dmixed)
        dmb, vnb = dmixed.astype(BF16), vn.astype(BF16)
        for g in range(A_GROUPS):
            cols = slice(g * LANES, (g + 1) * LANES)
            dw_ref[g] += lax.dot_general(dmb[:, cols], vnb[:, cols], NT, preferred_element_type=F32)
            db_ref[:, g:g + 1] += jnp.sum(dmixed[:, cols], axis=-1, keepdims=True)
        gv = gv_ref[...]
        vh = v * rs
        dg_ref[0:1, :] += jnp.sum(dvn * vh, axis=0, keepdims=True)
        dvh = dvn * gv
        dv = rs * (dvh - vh * jnp.mean(dvh * vh, axis=-1, keepdims=True))
        _, vjp = jax.vjp(jax.nn.gelu, za)
        dz_ref[...] = vjp(jnp.concatenate([du, dv], axis=1))[0].astype(BF16)

    wspec = pl.BlockSpec((A_GROUPS, CHUNK, CHUNK), lambda i: (0, 0, 0))
    outs, carried = _pcall(
        body, (z, ws, ws_t, bias, g_v, dy), name=name, grid=(r // CHUNK,),
        in_specs=[pl.BlockSpec((CHUNK, 2 * D_MODEL), lambda i: (i, P_A // (2 * D_MODEL))), wspec, wspec,
                  pl.BlockSpec((CHUNK, D_MODEL), lambda i: (0, 0)), pl.BlockSpec((1, D_MODEL), lambda i: (0, 0)),
                  pl.BlockSpec((CHUNK, D_MODEL), lambda i: (i, 0))],
        out_specs=[pl.BlockSpec((CHUNK, 2 * D_MODEL), lambda i: (i, 0)), wspec,
                   pl.BlockSpec((CHUNK, A_GROUPS), lambda i: (0, 0)), pl.BlockSpec((8, D_MODEL), lambda i: (0, 0))],
        out_shape=[jax.ShapeDtypeStruct((r, 2 * D_MODEL), BF16), jax.ShapeDtypeStruct((A_GROUPS, CHUNK, CHUNK), F32),
                   jax.ShapeDtypeStruct((CHUNK, A_GROUPS), F32), jax.ShapeDtypeStruct((8, D_MODEL), F32)], carry=carry)
    return outs if carry is None else (outs, carried)


def _scr_rows(j, n_lat):
    s = pl.multiple_of(j * CONV_CHUNK, CONV_CHUNK)
    shift = jnp.where(j >= n_lat // CONV_CHUNK, 2 * PAD, PAD)
    return s, pl.multiple_of(s + shift, PAD)


def _taps(scr, j, n_lat):
    c = CONV_CHUNK
    s, at = _scr_rows(j, n_lat)
    ext = scr[pl.ds(pl.multiple_of(at - PAD, PAD), c + 2 * PAD), :]
    xm = pltpu.roll(ext, 1, 0)[PAD:PAD + c]
    xp = pltpu.roll(ext, c + 2 * PAD - 1, 0)[PAD:PAD + c]
    return s, xm, ext[PAD:PAD + c], xp


def _zero_pads(scr, r, n_lat):
    for at in (0, PAD + n_lat, 2 * PAD + r):
        scr[pl.ds(at, PAD), :] = jnp.zeros((PAD, LANES), F32)


def _col(arr_cols, c0):
    return pl.BlockSpec((arr_cols, LANES), lambda c: (0, c0 + c))


def _ffn_conv_fwd(up, w, n_lat, name):
    r = up.shape[0]
    nct = D_FF // LANES
    nchunk = r // CONV_CHUNK

    def body(a_ref, g_ref, w_ref, f_ref, scr):
        _zero_pads(scr, r, n_lat)

        def fill(j, _):
            s, at = _scr_rows(j, n_lat)
            scr[pl.ds(at, CONV_CHUNK), :] = a_ref[pl.ds(s, CONV_CHUNK), :].astype(F32)
            return 0
        lax.fori_loop(0, nchunk, fill, 0)
        wv = w_ref[...]

        def step(j, _):
            s, xm, x0, xp = _taps(scr, j, n_lat)
            ca = wv[0:1] * xm + wv[1:2] * x0 + wv[2:3] * xp
            gv = g_ref[pl.ds(s, CONV_CHUNK), :].astype(F32)
            f_ref[pl.ds(s, CONV_CHUNK), :] = (ca * _sigmoid(ca) * gv).astype(BF16)
            return 0
        lax.fori_loop(0, nchunk, step, 0)

    return pl.pallas_call(
        body, name=name, grid=(nct,),
        in_specs=[_col(r, 0), _col(r, nct), _col(3, 0)],
        out_specs=_col(r, 0), out_shape=jax.ShapeDtypeStruct((r, D_FF), BF16),
        scratch_shapes=[pltpu.VMEM((r + 3 * PAD, LANES), F32)], compiler_params=_params())(up, up, w)


def _ffn_conv_bwd(up, w, df, n_lat, name, carry=None):
    r = up.shape[0]
    nct = D_FF // LANES
    nchunk = r // CONV_CHUNK

    def body(a_ref, g_ref, w_ref, df_ref, dup_ref, dw_ref, scr, scr2):
        _zero_pads(scr, r, n_lat)
        _zero_pads(scr2, r, n_lat)

        def fill(j, _):
            s, at = _scr_rows(j, n_lat)
            scr[pl.ds(at, CONV_CHUNK), :] = a_ref[pl.ds(s, CONV_CHUNK), :].astype(F32)
            return 0
        lax.fori_loop(0, nchunk, fill, 0)
        wv = w_ref[...]

        def first(j, carry):
            s, xm, x0, xp = _taps(scr, j, n_lat)
            ca = wv[0:1] * xm + wv[1:2] * x0 + wv[2:3] * xp
            sg = _sigmoid(ca)
            gv = g_ref[pl.ds(s, CONV_CHUNK), :].astype(F32)
            dfv = df_ref[pl.ds(s, CONV_CHUNK), :].astype(F32)
            dup_ref[1, pl.ds(s, CONV_CHUNK), :] = (dfv * ca * sg).astype(BF16)
            dca = dfv * gv * (sg * (1.0 + ca * (1.0 - sg)))
            scr2[pl.ds(_scr_rows(j, n_lat)[1], CONV_CHUNK), :] = dca
            return tuple(cw + jnp.sum(dca * xv, axis=0, keepdims=True) for cw, xv in zip(carry, (xm, x0, xp)))
        zero = jnp.zeros((1, LANES), F32)
        dws = lax.fori_loop(0, nchunk, first, (zero, zero, zero))
        dw_ref[...] = _rows(list(dws) + [None] * 5)

        def second(j, _):
            s, ym, y0, yp = _taps(scr2, j, n_lat)
            dup_ref[0, pl.ds(s, CONV_CHUNK), :] = (wv[0:1] * yp + wv[1:2] * y0 + wv[2:3] * ym).astype(BF16)
            return 0
        lax.fori_loop(0, nchunk, second, 0)

    outs, carried = _pcall(
        body, (up, up, w, df), name=name, grid=(nct,),
        in_specs=[_col(r, 0), _col(r, nct), _col(3, 0), _col(r, 0)],
        out_specs=[pl.BlockSpec((2, r, LANES), lambda c: (0, 0, c)), _col(8, 0)],
        out_shape=[jax.ShapeDtypeStruct((2, r, D_FF), BF16), jax.ShapeDtypeStruct((8, D_FF), F32)],
        scratch_shapes=[pltpu.VMEM((r + 3 * PAD, LANES), F32), pltpu.VMEM((r + 3 * PAD, LANES), F32)], carry=carry)
    return outs if carry is None else (outs, carried)


def _sconv_fwd(z, w, n_lat, name):
    r = z.shape[0]
    nct = D_MODEL // LANES
    nchunk = r // CONV_CHUNK
    c0 = P_B // LANES

    def body(bg_ref, cg_ref, hb_ref, w_ref, y_ref, scr):
        _zero_pads(scr, r, n_lat)

        def fill(j, _):
            s, at = _scr_rows(j, n_lat)
            rows = pl.ds(s, CONV_CHUNK)
            scr[pl.ds(at, CONV_CHUNK), :] = cg_ref[rows, :].astype(F32) * hb_ref[rows, :].astype(F32)
            return 0
        lax.fori_loop(0, nchunk, fill, 0)
        wv = w_ref[...]

        def step(j, _):
            s, xm, x0, xp = _taps(scr, j, n_lat)
            conv = wv[0:1] * xm + wv[1:2] * x0 + wv[2:3] * xp
            y_ref[pl.ds(s, CONV_CHUNK), :] = (bg_ref[pl.ds(s, CONV_CHUNK), :].astype(F32) * conv).astype(BF16)
            return 0
        lax.fori_loop(0, nchunk, step, 0)

    return pl.pallas_call(
        body, name=name, grid=(nct,),
        in_specs=[_col(r, c0), _col(r, c0 + nct), _col(r, c0 + 2 * nct), _col(3, 0)],
        out_specs=_col(r, 0), out_shape=jax.ShapeDtypeStruct((r, D_MODEL), BF16),
        scratch_shapes=[pltpu.VMEM((r + 3 * PAD, LANES), F32)], compiler_params=_params())(z, z, z, w)


def _sconv_bwd(z, w, dy, n_lat, name):
    r = z.shape[0]
    nct = D_MODEL // LANES
    nchunk = r // CONV_CHUNK
    c0 = P_B // LANES

    def body(bg_ref, cg_ref, hb_ref, w_ref, dy_ref, dz_ref, dw_ref, scr, scr2):
        _zero_pads(scr, r, n_lat)
        _zero_pads(scr2, r, n_lat)

        def fill(j, _):
            s, at = _scr_rows(j, n_lat)
            rows = pl.ds(s, CONV_CHUNK)
            scr[pl.ds(at, CONV_CHUNK), :] = cg_ref[rows, :].astype(F32) * hb_ref[rows, :].astype(F32)
            return 0
        lax.fori_loop(0, nchunk, fill, 0)
        wv = w_ref[...]

        def first(j, carry):
            s, xm, x0, xp = _taps(scr, j, n_lat)
            rows = pl.ds(s, CONV_CHUNK)
            conv = wv[0:1] * xm + wv[1:2] * x0 + wv[2:3] * xp
            dyv = dy_ref[rows, :].astype(F32)
            dz_ref[0, rows, :] = (dyv * conv).astype(BF16)
            dconv = dyv * bg_ref[rows, :].astype(F32)
            scr2[pl.ds(_scr_rows(j, n_lat)[1], CONV_CHUNK), :] = dconv
            return tuple(cw + jnp.sum(dconv * xv, axis=0, keepdims=True) for cw, xv in zip(carry, (xm, x0, xp)))
        zero = jnp.zeros((1, LANES), F32)
        dws = lax.fori_loop(0, nchunk, first, (zero, zero, zero))
        dw_ref[...] = _rows(list(dws) + [None] * 5)

        def second(j, _):
            s, ym, y0, yp = _taps(scr2, j, n_lat)
            rows = pl.ds(s, CONV_CHUNK)
            dx = wv[0:1] * yp + wv[1:2] * y0 + wv[2:3] * ym
            dz_ref[1, rows, :] = (dx * hb_ref[rows, :].astype(F32)).astype(BF16)
            dz_ref[2, rows, :] = (dx * cg_ref[rows, :].astype(F32)).astype(BF16)
            return 0
        lax.fori_loop(0, nchunk, second, 0)

    return pl.pallas_call(
        body, name=name, grid=(nct,),
        in_specs=[_col(r, c0), _col(r, c0 + nct), _col(r, c0 + 2 * nct), _col(3, 0), _col(r, 0)],
        out_specs=[pl.BlockSpec((3, r, LANES), lambda c: (0, 0, c)), _col(8, 0)],
        out_shape=[jax.ShapeDtypeStruct((3, r, D_MODEL), BF16), jax.ShapeDtypeStruct((8, D_MODEL), F32)],
        scratch_shapes=[pltpu.VMEM((r + 3 * PAD, LANES), F32), pltpu.VMEM((r + 3 * PAD, LANES), F32)],
        compiler_params=_params())(z, z, z, w, dy)


def _merge_proj_out(z, b_gate, ps, w_out, bm, name):
    r = z.shape[0]
    sub = ROW_TILE
    row = pl.BlockSpec((bm, D_MODEL), lambda i: (i, 0))

    def body(zg_ref, b_ref, p0_ref, p1_ref, p2_ref, w_ref, m_ref, o_ref):
        def chunk(c, _):
            rows = pl.ds(pl.multiple_of(c * sub, sub), sub)
            gates = _sigmoid(zg_ref[rows, :].astype(F32) + b_ref[...])
            acc = None
            for t, p_ref in enumerate((p0_ref, p1_ref, p2_ref)):
                term = gates[:, t * D_MODEL:(t + 1) * D_MODEL] * p_ref[rows, :].astype(F32)
                acc = term if acc is None else acc + term
            m_ref[rows, :] = acc.astype(BF16)
            return 0
        lax.fori_loop(0, bm // sub, chunk, 0)
        o_ref[...] = jnp.dot(m_ref[...], w_ref[...], preferred_element_type=F32).astype(o_ref.dtype)

    return pl.pallas_call(
        body, name=name, grid=(r // bm,),
        in_specs=[pl.BlockSpec((bm, 3 * D_MODEL), lambda i: (i, 0)), pl.BlockSpec((1, 3 * D_MODEL), lambda i: (0, 0)),
                  row, row, row, pl.BlockSpec((D_MODEL, D_MODEL), lambda i: (0, 0))],
        out_specs=[row, row],
        out_shape=[jax.ShapeDtypeStruct((r, D_MODEL), BF16), jax.ShapeDtypeStruct((r, D_MODEL), BF16)],
        compiler_params=_params())(z, b_gate, *ps, w_out)


def _merge_bwd(z, b_gate, ps, dmerged, name):
    r = z.shape[0]
    tm = ROW_TILE
    row = pl.BlockSpec((tm, D_MODEL), lambda i: (i, 0))
    wide = pl.BlockSpec((tm, 3 * D_MODEL), lambda i: (i, 0))

    def body(zg_ref, b_ref, p0_ref, p1_ref, p2_ref, dm_ref, d0_ref, d1_ref, d2_ref, dz_ref, db_ref):
        @pl.when(pl.program_id(0) == 0)
        def _():
            db_ref[...] = jnp.zeros_like(db_ref)

        gates = _sigmoid(zg_ref[...].astype(F32) + b_ref[...])
        dm = dm_ref[...].astype(F32)
        for t, (p_ref, d_ref) in enumerate(((p0_ref, d0_ref), (p1_ref, d1_ref), (p2_ref, d2_ref))):
            cols = slice(t * D_MODEL, (t + 1) * D_MODEL)
            gt = gates[:, cols]
            d_ref[...] = (dm * gt).astype(BF16)
            dlogit = dm * p_ref[...].astype(F32) * gt * (1.0 - gt)
            dz_ref[:, cols] = dlogit.astype(BF16)
            db_ref[0:1, cols] += jnp.sum(dlogit, axis=0, keepdims=True)

    shp = jax.ShapeDtypeStruct((r, D_MODEL), BF16)
    return pl.pallas_call(
        body, name=name, grid=(r // tm,),
        in_specs=[wide, pl.BlockSpec((1, 3 * D_MODEL), lambda i: (0, 0)), row, row, row, row],
        out_specs=[row, row, row, wide, pl.BlockSpec((8, 3 * D_MODEL), lambda i: (0, 0))],
        out_shape=[shp, shp, shp, jax.ShapeDtypeStruct((r, 3 * D_MODEL), BF16),
                   jax.ShapeDtypeStruct((8, 3 * D_MODEL), F32)],
        compiler_params=_params())(z, b_gate, *ps, dmerged)


def _sum_slots(buf, name):
    s, rows, _ = buf.shape
    whole_bytes = s * rows * LANES * 4
    tr = rows if whole_bytes <= VMEM_LIMIT // 8 else _pick(rows, [512, 256, 128, 64, 32, 16, 8])

    def body(b_ref, o_ref):
        acc = b_ref[0]
        for t in range(1, s):
            acc = acc + b_ref[t]
        o_ref[...] = acc

    return pl.pallas_call(
        body, name=name, grid=(rows // tr,),
        in_specs=[pl.BlockSpec((s, tr, LANES), lambda i: (0, i, 0))],
        out_specs=pl.BlockSpec((tr, LANES), lambda i: (i, 0)),
        out_shape=jax.ShapeDtypeStruct((rows, LANES), F32), compiler_params=_params())(buf)


def _adamw(w, gsrcs, m, v, name, scale=None):
    nl, rows, cols = w.shape
    assert len(gsrcs) == nl
    s = gsrcs[0].shape[0]
    tr = _pick(rows, [304, 352, 256, 128, 64, 32, 16, 8])
    blk = pl.BlockSpec((None, tr, cols), lambda l, i: (l, i, 0))
    c1 = 1.0 / (1.0 - ADAM_B1 ** ADAM_STEP)
    c2 = 1.0 / (1.0 - ADAM_B2 ** ADAM_STEP)

    def gspec(t):
        return pl.BlockSpec((s, tr, cols), lambda l, i: (0, jnp.where(l == t, i, 0), 0))

    def body(*refs):
        w_ref, g_refs, (m_ref, v_ref) = refs[0], refs[1:1 + nl], refs[1 + nl:3 + nl]
        rest = refs[3 + nl:]
        if scale is not None:
            sc_ref, rest = rest[0], rest[1:]
        go_ref, d_ref, mo_ref, vo_ref = rest
        layer = pl.program_id(0)
        g = None
        for t in range(nl):
            gt = g_refs[t][0].astype(F32)
            for q in range(1, s):
                gt = gt + g_refs[t][q].astype(F32)
            g = gt if g is None else jnp.where(layer == t, gt, g)
        if scale is not None:
            g = g * sc_ref[...]
        mn = ADAM_B1 * m_ref[...] + (1.0 - ADAM_B1) * g
        vn = ADAM_B2 * v_ref[...] + (1.0 - ADAM_B2) * (g * g)
        go_ref[...] = g
        mo_ref[...] = mn
        vo_ref[...] = vn
        d_ref[...] = -ADAM_LR * ((mn * c1) / (jnp.sqrt(vn * c2) + ADAM_EPS) + ADAM_WD * w_ref[...])

    shp = jax.ShapeDtypeStruct((nl, rows, cols), F32)
    ins = [w] + list(gsrcs) + [m, v] + ([] if scale is None else [scale])
    return pl.pallas_call(
        body, name=name, grid=(nl, rows // tr),
        in_specs=[blk] + [gspec(t) for t in range(nl)] + [blk, blk] + ([] if scale is None else [blk]),
        out_specs=[blk] * 4, out_shape=[shp] * 4, compiler_params=_params())(*ins)


def _pack(arrs):
    flat = []
    for a in arrs:
        a = a.reshape(-1).astype(F32)
        pad = (-a.shape[0]) % (8 * LANES)
        flat.append(jnp.pad(a, (0, pad)) if pad else a)
    return jnp.concatenate(flat).reshape(-1, LANES)


def _unpack(buf, shapes, lead=()):
    out, row = [], 0
    for shp in shapes:
        n = 1
        for d in shp:
            n *= d
        nrows = -(-n // (8 * LANES)) * 8
        piece = buf[..., row:row + nrows, :].reshape(lead + (nrows * LANES,))[..., :n]
        out.append(piece.reshape(lead + tuple(shp)))
        row += nrows
    return out


def _silu(x):
    return x * jax.nn.sigmoid(x)


def _rope_tables(n_lat, m_ctx):
    pos = jnp.arange(n_lat)
    row = (pos // GRID_W).astype(F32)
    col = (pos % GRID_W).astype(F32)
    half = HEAD_DIM // 2
    inv = ROPE_THETA ** (-jnp.arange(0, half, 2, dtype=F32) / half)
    ang = jnp.concatenate([row[:, None] * inv, col[:, None] * inv], axis=-1)
    cos, sin = jnp.cos(ang), jnp.sin(ang)
    cos2 = jnp.tile(jnp.concatenate([cos, cos], axis=-1), (1, LANES // HEAD_DIM))
    sin2 = jnp.tile(jnp.concatenate([-sin, sin], axis=-1), (1, LANES // HEAD_DIM))
    return (jnp.concatenate([cos2, jnp.ones((m_ctx, LANES), F32)], axis=0),
            jnp.concatenate([sin2, jnp.zeros((m_ctx, LANES), F32)], axis=0))


def _to_slots(full, axis):
    shp = full.shape
    new = shp[:axis] + (N_DEV, shp[axis] // N_DEV) + shp[axis + 1:]
    return jnp.moveaxis(full.reshape(new), axis, 0)


def _from_slots(slots, axis):
    moved = jnp.moveaxis(slots, 0, axis)
    shp = moved.shape
    return moved.reshape(shp[:axis] + (shp[axis] * shp[axis + 1],) + shp[axis + 2:])


def _permute_in(wt, name):
    blk = 512
    segs = [(OFF_G, IN_W), (0, OFF_K), (OFF_A, OFF_B), (OFF_B, OFF_G), (OFF_K, OFF_A)]

    def source(j):
        src, at = 0, 0
        for lo, hi in segs:
            n = (hi - lo) // blk
            src = src + jnp.where((j >= at) & (j < at + n), j - at + lo // blk, 0)
            at += n
        return (src, 0)

    def body(i_ref, o_ref):
        o_ref[...] = i_ref[...]

    return pl.pallas_call(
        body, name=name, grid=(IN_W // blk,), in_specs=[pl.BlockSpec((blk, D_MODEL), source)],
        out_specs=pl.BlockSpec((blk, D_MODEL), lambda j: (j, 0)), out_shape=jax.ShapeDtypeStruct(wt.shape, wt.dtype),
        compiler_params=_params())(wt)


def kernel(x, c, ctx, c_ctx, w_mod, b_mod, g_mix, w_in, b_gate, sink, w_spatial, b_spatial, g_v, w_sconv, w_branch, w_out, g_ffn, w_up, w_fconv, w_down, g_final, loss_target, m_c_ctx, m_w_mod, m_b_mod, m_g_mix, m_w_in, m_b_gate, m_sink, m_w_spatial, m_b_spatial, m_g_v, m_w_sconv, m_w_branch, m_w_out, m_g_ffn, m_w_up, m_w_fconv, m_w_down, m_g_final, v_c_ctx, v_w_mod, v_b_mod, v_g_mix, v_w_in, v_b_gate, v_sink, v_w_spatial, v_b_spatial, v_g_v, v_w_sconv, v_w_branch, v_w_out, v_g_ffn, v_w_up, v_w_fconv, v_w_down, v_g_final):
    n_lat, m_ctx = x.shape[1], ctx.shape[1]
    r = n_lat + m_ctx
    me = 4 * lax.axis_index("x") + 2 * lax.axis_index("y") + lax.axis_index("c")
    mod_w = w_mod.shape[2]
    bm = _pick(r, [1408, 768, 256])
    bm_fused = _pick(r, [768, 256])

    tr = lambda a: jnp.swapaxes(a, -1, -2)
    branches = [(f"br{t}", w_branch[:, t]) for t in range(3)]
    shards = {}
    for kind, wt in [("in", tr(w_in)), ("out", w_out), ("up", tr(w_up)), ("dn", w_down)] + branches:
        wb = wt.astype(BF16)
        for l in range(DEPTH):
            shards[kind, l] = wb[l]
    full = {}

    def arrive(items, got):
        for key, slots in zip(items, got):
            wfull = _from_slots(slots, 0)
            full[key] = _permute_in(wfull, f"permute_in{key[1]}") if key[0] == "in" else wfull

    def gather_of(items):
        return [shards[key] for key in items], ["gather"] * len(items)

    small_shapes = [c.shape, w_sconv.shape, w_fconv.shape]
    (g_small,) = _exchange([_pack([c, w_sconv, w_fconv])], ["gather"], "gather_first")
    c_all, sconv_all, fconv_all = _unpack(g_small, small_shapes, lead=(N_DEV,))
    c_all = c_all.reshape(N_DEV, D_MODEL)
    w_sconv_full = _from_slots(sconv_all, 2)
    w_fconv_full = _from_slots(fconv_all, 2)

    act = jnp.concatenate([_silu(c_all), _silu(c_ctx)[None], jnp.zeros((7, D_MODEL), F32)], axis=0)
    mod_part = jnp.stack([_mm(act, w_mod[l], name=f"mod_fwd{l}", bm=16, bn=mod_w, bk=D_MODEL, out_dtype=F32)
                          for l in range(DEPTH)])
    (mod_all,) = _exchange([mod_part], ["gather"], "gather_mod")
    mod_full = _from_slots(mod_all, 2) + b_mod[:, None, :]
    mods = []
    for l in range(DEPTH):
        mine = lax.dynamic_index_in_dim(mod_full[l], me, axis=0, keepdims=False).reshape(6, D_MODEL)
        theirs = mod_full[l, N_DEV].reshape(6, D_MODEL)
        mods.append(jnp.pad(jnp.stack([mine, theirs]), ((0, 0), (0, 2), (0, 0))))

    cos, sin = _rope_tables(n_lat, m_ctx)
    xs = None
    ws_b = w_spatial.astype(BF16)
    ws_t = jnp.swapaxes(w_spatial, 2, 3).astype(BF16)
    vec = lambda a: a.reshape(1, -1)

    saved = []
    res = None
    for l in range(DEPTH):
        s = {}
        if res is None:
            (s["x0"], s["h"]), got = _norm_fwd((x[0], ctx[0]), vec(g_mix[l]), mods[l], n_lat, 0, 1, f"norm_mix{l}",
                                               carry=gather_of([("in", l)]))
            arrive([("in", l)], got)
        else:
            s["x0"], s["h"] = _norm_fwd(xs, vec(g_mix[l]), mods[l], n_lat, 0, 1, f"norm_mix{l}", res=res)
        items = [("br0", l), ("br1", l), ("br2", l), ("out", l), ("up", l)]
        s["z"], got = _mm(s["h"], full["in", l], tb=True, name=f"proj_in{l}", bm=bm, bn=_pick(IN_W, [2432, 512]),
                          bk=D_MODEL, out_dtype=BF16, cols_outer=True, carry=gather_of(items))
        arrive(items, got)
        s["qkv"] = _rope_fwd(s["z"], cos, sin, f"rope{l}")
        items = [("dn", l)] + ([("in", l + 1)] if l + 1 < DEPTH else [])
        (s["y0"], s["lse"]), got = _attn_fwd(s["qkv"], sink[l], n_lat, m_ctx, f"attn{l}", carry=gather_of(items))
        arrive(items, got)
        s["bias"] = jnp.repeat(b_spatial[l].T, LANES, axis=1)
        s["y1"] = _gating_fwd(s["z"], ws_b[l], s["bias"], vec(g_v[l]), f"gating{l}")
        s["y2"] = _sconv_fwd(s["z"], w_sconv_full[l], n_lat, f"sconv{l}")
        s["p"] = [_mm(s[f"y{t}"], full[f"br{t}", l], name=f"branch{l}_{t}", bm=bm, bn=D_MODEL, bk=D_MODEL,
                      out_dtype=BF16) for t in range(3)]
        s["merged"], s["o"] = _merge_proj_out(s["z"], vec(b_gate[l]), s["p"], full["out", l], bm_fused,
                                              f"proj_out{l}")
        s["x1"], s["h2"] = _norm_fwd(s["x0"], vec(g_ffn[l]), mods[l], n_lat, 3, 4, f"norm_ffn{l}",
                                     res=(s["o"], mods[l], 2))
        s["up"] = _mm(s["h2"], full["up", l], tb=True, name=f"ffn_up{l}", bm=bm, bn=_pick(2 * D_FF, [1408]),
                      bk=D_MODEL, out_dtype=BF16, cols_outer=True)
        s["f"] = _ffn_conv_fwd(s["up"], w_fconv_full[l], n_lat, f"ffn_conv{l}")
        s["dd"] = _mm(s["f"], full["dn", l], name=f"ffn_down{l}", bm=bm, bn=D_MODEL, bk=D_FF, out_dtype=BF16)
        saved.append(s)
        xs, res = s["x1"], (s["dd"], mods[l], 5)

    top = saved[DEPTH - 1]
    dxs, d_dd, acc_final, accs_top = _loss_bwd(top["x1"], top["dd"], mods[DEPTH - 1], vec(g_final), loss_target[0],
                                               n_lat, "loss")
    loss_part = acc_final[1, 0]
    dg_final = acc_final[0]
    dmods = [None] * DEPTH
    gate2 = accs_top[:, 2]
    grads = {k: [None] * DEPTH for k in ("g_mix", "g_ffn", "g_v", "b_gate", "sink", "w_spatial", "b_spatial",
                                         "w_sconv", "w_fconv", "w_in", "w_branch", "w_out", "w_up", "w_down")}
    bk_r = _pick(r, [2816, 768, 256])
    small_names = ["g_ffn", "g_v", "b_gate", "sink", "w_spatial", "b_spatial", "w_sconv", "w_fconv"]
    recv = {}
    small_recv, small_shapes_of = {}, {}

    def small_pack(l):
        arrs = [grads[k][l] for k in small_names]
        if l > 0:
            arrs.append(grads["g_mix"][l])
        if l == DEPTH - 1:
            arrs += [loss_part.reshape(1), dg_final]
        small_shapes_of[l] = [a.shape for a in arrs]
        return _pack(arrs)

    for l in reversed(range(DEPTH)):
        s = saved[l]
        df = _mm(d_dd, full["dn", l], tb=True, name=f"d_ffn_down{l}", bm=bm, bn=_pick(D_FF, [1408]), bk=D_MODEL,
                 out_dtype=BF16)
        grads["w_down"][l] = _mm(s["f"], d_dd, ta=True, name=f"g_ffn_down{l}", bm=_pick(D_FF, [1408]), bn=D_MODEL,
                                 bk=bk_r, out_dtype=BF16)
        arrs, modes = [_to_slots(grads["w_down"][l], 0)], ["a2a"]
        if l + 1 < DEPTH:
            arrs.append(small_pack(l + 1))
            modes.append("gather")
        (dup, dwf), got = _ffn_conv_bwd(s["up"], w_fconv_full[l], df, n_lat, f"d_ffn_conv{l}", carry=(arrs, modes))
        recv["dn", l] = got[0]
        if l + 1 < DEPTH:
            small_recv[l + 1] = got[1]
        grads["w_fconv"][l] = dwf[:3]
        dh2 = _mm_pieces([(dup, "step", 0, 2, 0, D_FF)], full["up", l], w_t=True, name=f"d_ffn_up{l}", bm=bm,
                         out_dtype=BF16)
        kb = 1408
        nbh = D_FF // kb
        grads["w_up"][l] = _mm(
            dup, s["h2"], ta=True, name=f"g_ffn_up{l}", bm=kb, bn=D_MODEL, bk=bk_r, out_dtype=BF16,
            a_spec=((None, bk_r, kb), lambda i, j, k: (i // nbh, k, i % nbh), 2 * D_FF))
        dx1, d_o, acc, accs = _norm_bwd(s["x1"], vec(g_ffn[l]), mods[l], dh2, dxs, n_lat, 3, 4, f"d_norm_ffn{l}",
                                        res=(s["o"], mods[l], 2))
        grads["g_ffn"][l] = acc[0]
        shift2, scale2, gate1 = accs[:, 0], accs[:, 1], accs[:, 2]
        dmerged = _mm(d_o, full["out", l], tb=True, name=f"d_proj_out{l}", bm=bm, bn=D_MODEL, bk=D_MODEL,
                      out_dtype=BF16)
        grads["w_out"][l] = _mm(s["merged"], d_o, ta=True, name=f"g_proj_out{l}", bm=D_MODEL, bn=D_MODEL, bk=bk_r,
                                out_dtype=BF16)
        dp0, dp1, dp2, dz_g, dbg = _merge_bwd(s["z"], vec(b_gate[l]), s["p"], dmerged, f"d_merge{l}")
        grads["b_gate"][l] = dbg[0]
        dps = (dp0, dp1, dp2)
        dys = [_mm(dps[t], full[f"br{t}", l], tb=True, name=f"d_branch{l}_{t}", bm=bm, bn=D_MODEL, bk=D_MODEL,
                   out_dtype=BF16) for t in range(3)]
        grads["w_branch"][l] = jnp.stack(
            [_mm(s[f"y{t}"], dps[t], ta=True, name=f"g_branch{l}_{t}", bm=D_MODEL, bn=D_MODEL, bk=bk_r,
                 out_dtype=BF16) for t in range(3)])
        arrs = [_to_slots(grads["w_up"][l], 0)] + ([_to_slots(grads["w_in"][l + 1], 0)] if l + 1 < DEPTH else [])
        (dq, dk, dv, dsk), got = _attn_bwd(s["qkv"], sink[l], s["y0"], dys[0], s["lse"], n_lat, m_ctx, f"d_attn{l}",
                                           carry=(arrs, ["a2a"] * len(arrs)))
        recv["up", l] = got[0]
        if l + 1 < DEPTH:
            recv["in", l + 1] = got[1]
        grads["sink"][l] = dsk[:, 0, :8].reshape(N_HEADS)
        dz_qkv = _rope_bwd(dq, dk, dv, cos, sin, f"d_rope{l}")
        (dz_a, dws, dbs, dgv), (recv["out", l], recv["br", l]) = _gating_bwd(
            s["z"], ws_b[l], ws_t[l], s["bias"], vec(g_v[l]), dys[1], f"d_gating{l}",
            carry=([_to_slots(grads["w_out"][l], 0), _to_slots(grads["w_branch"][l], 1)], ["a2a"] * 2))
        grads["w_spatial"][l], grads["b_spatial"][l], grads["g_v"][l] = dws, dbs.T, dgv[0]
        dz_b, dwsc = _sconv_bwd(s["z"], w_sconv_full[l], dys[2], n_lat, f"d_sconv{l}")
        grads["w_sconv"][l] = dwsc[:3]
        kvw = 2 * N_KV_HEADS * HEAD_DIM
        pieces = [(dz_g, None, 0, 3, 0, D_MODEL), (dz_qkv, None, 0, 1, P_Q // D_MODEL, D_MODEL),
                  (dz_a, None, 0, 2, P_A // D_MODEL, D_MODEL), (dz_b, "step", 0, 3, P_B // D_MODEL, D_MODEL),
                  (dz_qkv, None, D_MODEL // kvw, 1, P_KV // kvw, kvw)]
        gw = lambda a, nm, rows, **kw: _mm(a, s["h"], ta=True, name=f"g_proj_in{l}_{nm}", bm=rows, bn=D_MODEL,
                                           bk=bk_r, out_dtype=BF16, **kw)
        if l > 0:
            gw_g = gw(dz_g, "gate", 1536)
        else:
            gw_g, (small_recv[0],) = gw(dz_g, "gate", 1536, carry=([small_pack(0)], ["gather"]))
        gw_qkv = gw(dz_qkv, "qkv", 1536)
        gw_a = gw(dz_a, "gating", 1024)
        gw_b = [gw(dz_b, f"sconv{t}", 1024, a_lead=t) for t in range(3)]
        grads["w_in"][l] = jnp.concatenate([gw_qkv, gw_a] + gw_b + [gw_g], axis=0)
        bm_in = _pick(r, [1408, 768, 256])
        if l > 0:
            dh = _mm_pieces(pieces, full["in", l], w_t=True, name=f"d_proj_in{l}", bm=bm_in, out_dtype=BF16)
        else:
            dh, (recv["in", 0],) = _mm_pieces(
                pieces, full["in", l], w_t=True, name=f"d_proj_in{l}", bm=bm_in, out_dtype=BF16,
                carry=([_to_slots(grads["w_in"][0], 0)], ["a2a"]))
        below = None if l == 0 else (saved[l - 1]["dd"], mods[l - 1], 5)
        outs = _norm_bwd(s["x0"], vec(g_mix[l]), mods[l], dh, dx1, n_lat, 0, 1, f"d_norm_mix{l}", res=below)
        if below is None:
            dxs, acc, accs = outs
        else:
            dxs, d_dd, acc, accs = outs
        grads["g_mix"][l] = acc[0]
        dmods[l] = jnp.stack([accs[:, 0], accs[:, 1], gate1, shift2, scale2, gate2], axis=1)
        gate2 = accs[:, 2]

    dmod_own = jnp.stack([dmods[l][0].reshape(-1) for l in range(DEPTH)])
    dmod_ctx = jnp.stack([dmods[l][1].reshape(-1) for l in range(DEPTH)])
    late = [grads["g_mix"][0], dmod_own + dmod_ctx, dmod_ctx, dmod_own]
    late_shapes = [a.shape for a in late]
    (late_all,) = _exchange([_pack(late)], ["gather"], "gather_late_grads")
    g_mix0_g, b_mod_g, dmodc_tot, _ = _unpack(_sum_slots(late_all, "sum_late_grads"), late_shapes)
    dmod_all = _unpack(late_all, late_shapes, lead=(N_DEV,))[-1]
    layer_sums = [_unpack(_sum_slots(small_recv[l], f"sum_small_grads{l}"), small_shapes_of[l]) for l in range(DEPTH)]
    by_name = {k: jnp.stack([layer_sums[l][t] for l in range(DEPTH)]) for t, k in enumerate(small_names)}
    g_mix_g = jnp.stack([g_mix0_g] + [layer_sums[l][len(small_names)] for l in range(1, DEPTH)])
    loss_sum, g_final_g = layer_sums[DEPTH - 1][-2], layer_sums[DEPTH - 1][-1]
    g_ffn_g, g_v_g, b_gate_g, sink_g = by_name["g_ffn"], by_name["g_v"], by_name["b_gate"], by_name["sink"]
    w_spatial_g, b_spatial_g = by_name["w_spatial"], by_name["b_spatial"]
    w_sconv_g = lax.dynamic_slice_in_dim(by_name["w_sconv"], me * w_sconv.shape[2], w_sconv.shape[2], axis=2)
    w_fconv_g = lax.dynamic_slice_in_dim(by_name["w_fconv"], me * w_fconv.shape[2], w_fconv.shape[2], axis=2)

    dmod_cols = lax.dynamic_slice_in_dim(dmod_all, me * mod_w, mod_w, axis=2)
    dmodc_cols = lax.dynamic_slice_in_dim(dmodc_tot, me * mod_w, mod_w, axis=1)
    g_w_mod, cctx_part = [], None
    for l in range(DEPTH):
        rhs = jnp.concatenate([dmod_cols[:, l], dmodc_cols[l][None], jnp.zeros((7, mod_w), F32)], axis=0)
        g_w_mod.append(_mm(act, rhs, ta=True, name=f"g_mod{l}", bm=D_MODEL, bn=mod_w, bk=16, out_dtype=F32))
        lhs = jnp.pad(dmodc_cols[l][None], ((0, 7), (0, 0)))
        part = _mm(lhs, w_mod[l], tb=True, name=f"d_cctx{l}", bm=8, bn=D_MODEL, bk=mod_w, out_dtype=F32)
        cctx_part = part if cctx_part is None else cctx_part + part
    sg = jax.nn.sigmoid(c_ctx)
    dsilu = (sg * (1.0 + c_ctx * (1.0 - sg))).reshape(8, LANES)

    (r_cctx,) = _exchange([cctx_part[0].reshape(8, LANES)], ["gather"], "gather_c_ctx_grad")

    def per_layer(a):
        return a.reshape(a.shape[0], -1, a.shape[-1])

    upd = {}
    for nm, kind, wv, mv, vv in (("w_in", "in", w_in, m_w_in, v_w_in), ("w_branch", "br", w_branch, m_w_branch, v_w_branch),
                                 ("w_out", "out", w_out, m_w_out, v_w_out), ("w_up", "up", w_up, m_w_up, v_w_up),
                                 ("w_down", "dn", w_down, m_w_down, v_w_down), ("w_mod", None, w_mod, m_w_mod, v_w_mod)):
        if kind is None:
            gsrcs = [g[None] for g in g_w_mod]
        else:
            gsrcs = [per_layer(recv[kind, l]) for l in range(DEPTH)]
        if kind in ("in", "up"):
            outs = _adamw(tr(wv), gsrcs, tr(mv), tr(vv), f"adamw_{nm}")
            upd[nm] = [tr(o) for o in outs]
        else:
            outs = _adamw(per_layer(wv), gsrcs, per_layer(mv), per_layer(vv), f"adamw_{nm}")
            upd[nm] = [o.reshape(wv.shape) for o in outs]
    as_tile = lambda a: a.reshape(1, 8, LANES)
    upd["c_ctx"] = [o.reshape(D_MODEL) for o in _adamw(
        as_tile(c_ctx), [r_cctx], as_tile(m_c_ctx), as_tile(v_c_ctx), "adamw_c_ctx", scale=as_tile(dsilu))]

    names = ["b_mod", "g_mix", "b_gate", "sink", "w_spatial", "b_spatial", "g_v", "w_sconv", "g_ffn", "w_fconv",
             "g_final"]
    w_s = [b_mod, g_mix, b_gate, sink, w_spatial, b_spatial, g_v, w_sconv, g_ffn, w_fconv, g_final]
    g_s = [b_mod_g, g_mix_g, b_gate_g, sink_g, w_spatial_g, b_spatial_g, g_v_g, w_sconv_g, g_ffn_g, w_fconv_g,
           g_final_g]
    m_s = [m_b_mod, m_g_mix, m_b_gate, m_sink, m_w_spatial, m_b_spatial, m_g_v, m_w_sconv, m_g_ffn, m_w_fconv,
           m_g_final]
    v_s = [v_b_mod, v_g_mix, v_b_gate, v_sink, v_w_spatial, v_b_spatial, v_g_v, v_w_sconv, v_g_ffn, v_w_fconv,
           v_g_final]
    shapes = [a.shape for a in w_s]
    packed = _adamw(_pack(w_s)[None], [_pack(g_s)[None]], _pack(m_s)[None], _pack(v_s)[None], "adamw_small")
    unpacked = [_unpack(o[0], shapes) for o in packed]
    for t, nm in enumerate(names):
        upd[nm] = [unpacked[q][t] for q in range(4)]

    order = ["c_ctx", "w_mod", "b_mod", "g_mix", "w_in", "b_gate", "sink", "w_spatial", "b_spatial", "g_v", "w_sconv",
             "w_branch", "w_out", "g_ffn", "w_up", "w_fconv", "w_down", "g_final"]
    result = [loss_sum.reshape(()), dxs[None]]
    for q in range(4):
        result += [upd[nm][q] for nm in order]
    return tuple(result)
```
